```python
import math
import jax, jax.numpy as jnp
from jax import lax
import numpy as np

D_MODEL = 1024
BATCH = 4
SEQ = 4096
DEPTH = 1

MIX_WIDTH = D_MODEL
GLA_WIDTH = MIX_WIDTH // 2
GLA_HEADS = 4
GLA_DV = GLA_WIDTH // GLA_HEADS
GLA_DK = GLA_DV // 2
GLA_KEY_WIDTH = GLA_HEADS * GLA_DK
GLA_GATE_RANK = 16
GLA_GATE_TEMP = 16.0
GLA_CHUNK = 16
DIL_WIDTH = MIX_WIDTH - GLA_WIDTH
DIL_HEADS = 8
DIL_DH = DIL_WIDTH // DIL_HEADS
DIL_CONFIGS = ((128, 1), (512, 4), (2048, 16))
DIL_BLOCK = 128
N_GROUPS = 4
EXPERTS_PER_GROUP = 8
N_EXPERTS = N_GROUPS * EXPERTS_PER_GROUP
TOP_K_INNER = 2
D_FF_EXPERT = 512
MOE_BLOCK = 128
DEEPNORM_ALPHA = (2.0 * DEPTH) ** 0.25
DEEPNORM_BETA = (8.0 * DEPTH) ** -0.25
EPS = 1e-5
SPLIT_SIZES = (GLA_KEY_WIDTH, GLA_KEY_WIDTH, GLA_WIDTH, GLA_WIDTH, GLA_GATE_RANK, DIL_WIDTH, DIL_WIDTH, DIL_WIDTH)
SPLIT_IS_VALUE = (False, False, True, False, False, False, False, True)
SPLIT_POINTS = tuple(int(v) for v in np.cumsum(SPLIT_SIZES)[:-1])
PROJ_WIDTH = int(sum(SPLIT_SIZES))

kernel_name = "hymba_gla_dilated_alibi_hmoe_deepnorm"


def layer_norm(x, g, b):
    xf = x.astype(jnp.float32)
    mu = jnp.mean(xf, axis=-1, keepdims=True)
    var = jnp.mean(jnp.square(xf - mu), axis=-1, keepdims=True)
    return ((xf - mu) * lax.rsqrt(var + EPS)).astype(x.dtype) * g + b


def rms_norm(x, g):
    xf = x.astype(jnp.float32)
    return (xf * lax.rsqrt(jnp.mean(xf * xf, axis=-1, keepdims=True) + EPS)).astype(x.dtype) * g


def to_heads(t, n_heads):
    b, s, w = t.shape
    return t.reshape(b, s, n_heads, w // n_heads).transpose(0, 2, 1, 3)


def from_heads(t):
    b, h, s, d = t.shape
    return t.transpose(0, 2, 1, 3).reshape(b, s, h * d)


def gla_chunked(q, k, v, log_a):
    B, H, S, dk = q.shape
    dv = v.shape[-1]
    C = GLA_CHUNK
    n = S // C
    q = q.reshape(B, H, n, C, dk)
    k = k.reshape(B, H, n, C, dk)
    v = v.reshape(B, H, n, C, dv)
    b = jnp.cumsum(log_a.astype(jnp.float32).reshape(B, H, n, C, dk), axis=3)
    b_last = b[:, :, :, -1:, :]
    causal = jnp.tril(jnp.ones((C, C), dtype=bool))
    diff = b[:, :, :, :, None, :] - b[:, :, :, None, :, :]
    decay_ij = jnp.exp(jnp.where(causal[:, :, None], diff, -jnp.inf))
    A = jnp.sum(q[:, :, :, :, None, :] * k[:, :, :, None, :, :] * decay_ij, axis=-1)
    o_intra = jnp.einsum('bhnij,bhnje->bhnie', A, v)
    dS = jnp.einsum('bhncd,bhnce->bhnde', k * jnp.exp(b_last - b), v)
    chunk_decay = jnp.exp(b_last[:, :, :, 0, :])

    def step(state, inp):
        ds_n, dec_n = inp
        return state * dec_n[..., None] + ds_n, state

    s0 = jnp.zeros((B, H, dk, dv), dS.dtype)
    _, s_before = lax.scan(step, s0, (jnp.moveaxis(dS, 2, 0), jnp.moveaxis(chunk_decay, 2, 0)))
    s_before = jnp.moveaxis(s_before, 0, 2)
    o_inter = jnp.einsum('bhncd,bhnde->bhnce', q * jnp.exp(b), s_before)
    return (o_intra + o_inter).reshape(B, H, S, dv).astype(v.dtype)


def dilated_branch(q, k, v, slopes, window, dilation):
    B, H, S, dh = q.shape
    r = dilation
    L = S // r
    w = window // r
    nb = -(-L // DIL_BLOCK)
    Lp = nb * DIL_BLOCK

    def to_sub(t):
        t = t.reshape(B, H, L, r, dh).transpose(0, 1, 3, 2, 4)
        t = jnp.pad(t, ((0, 0), (0, 0), (0, 0), (0, Lp - L), (0, 0)))
        return t.reshape(B, H, r, nb, DIL_BLOCK, dh)

    qs, ks, vs = to_sub(q), to_sub(k), to_sub(v)

    def band(t):
        tp = jnp.concatenate([jnp.zeros_like(t[:, :, :, :1]), t], axis=3)
        return jnp.concatenate([tp[:, :, :, :-1], tp[:, :, :, 1:]], axis=4)

    kb, vb = band(ks), band(vs)
    scores = jnp.einsum('bhrnid,bhrnjd->bhrnij', qs, kb).astype(jnp.float32) * (DIL_DH ** -0.5)
    i_loc = jnp.arange(DIL_BLOCK)[:, None]
    j_loc = jnp.arange(2 * DIL_BLOCK)[None, :]
    rel = i_loc + DIL_BLOCK - j_loc
    blk = jnp.arange(nb)[:, None, None]
    valid = (rel >= 0) & (rel <= w) & (blk * DIL_BLOCK + j_loc - DIL_BLOCK >= 0)
    alibi = -slopes[:, None, None] * (rel * r).astype(jnp.float32)[None]
    scores = jnp.where(valid[None, None, None], scores + alibi[None, :, None, None], -jnp.inf)
    m = jnp.max(scores, axis=-1, keepdims=True)
    p = jnp.exp(scores - m)
    z = jnp.sum(p, axis=-1, keepdims=True)
    out = jnp.einsum('bhrnij,bhrnjd->bhrnid', (p / z).astype(v.dtype), vb)
    lse = (m + jnp.log(z))[..., 0]
    out = out.reshape(B, H, r, Lp, dh)[:, :, :, :L].transpose(0, 1, 3, 2, 4).reshape(B, H, S, dh)
    lse = lse.reshape(B, H, r, Lp)[:, :, :, :L].transpose(0, 1, 3, 2).reshape(B, H, S)
    return out, lse


def dilated_mixture(q, k, v, slopes):
    outs, lses = [], []
    for window, dilation in DIL_CONFIGS:
        o, lse = dilated_branch(q, k, v, slopes, window, dilation)
        outs.append(o)
        lses.append(lse)
    wts = jax.nn.softmax(jnp.stack(lses, axis=0), axis=0)
    return jnp.einsum('cbhs,cbhsd->bhsd', wts.astype(q.dtype), jnp.stack(outs, axis=0))


def grouped_expert_ffn(ht, expert_id, gate, w_gate, w_up, w_down):
    T, D = ht.shape
    K = expert_id.shape[1]
    P = T * K
    flat_e = expert_id.reshape(P)
    flat_tok = jnp.arange(P, dtype=jnp.int32) // K
    order = jnp.argsort(flat_e)
    e_sorted = flat_e[order]
    counts = jnp.bincount(flat_e, length=N_EXPERTS)
    padded = ((counts + MOE_BLOCK - 1) // MOE_BLOCK) * MOE_BLOCK
    start_sorted = jnp.cumsum(counts) - counts
    end_padded = jnp.cumsum(padded)
    start_padded = end_padded - padded
    dest = start_padded[e_sorted] + (jnp.arange(P) - start_sorted[e_sorted])
    n_rows = P + N_EXPERTS * MOE_BLOCK
    n_blocks = n_rows // MOE_BLOCK
    buf = jnp.zeros((n_rows, D), ht.dtype).at[dest].set(ht[flat_tok[order]])
    block_expert = jnp.minimum(
        jnp.searchsorted(end_padded, jnp.arange(n_blocks) * MOE_BLOCK, side='right'), N_EXPERTS - 1)

    def block_ffn(args):
        xblk, e = args
        hid = jax.nn.silu(xblk @ w_gate[e]) * (xblk @ w_up[e])
        return hid @ w_down[e]

    yb = lax.map(block_ffn, (buf.reshape(n_blocks, MOE_BLOCK, D), block_expert))
    y_sorted = yb.reshape(n_rows, D)[dest]
    y_pairs = jnp.zeros((P, D), yb.dtype).at[order].set(y_sorted).reshape(T, K, D)
    return jnp.einsum('tkd,tk->td', y_pairs, gate.astype(y_pairs.dtype))


def hier_moe(ht, rc_w, rc_b, rf_w, rf_b, w_gate, w_up, w_down):
    coarse = (ht @ rc_w + rc_b).astype(jnp.float32)
    p_coarse = jax.nn.softmax(coarse, axis=-1)
    g_idx = jnp.argmax(coarse, axis=-1).astype(jnp.int32)
    p_group = jnp.take_along_axis(p_coarse, g_idx[:, None], axis=1)
    fine = (jnp.einsum('td,dge->tge', ht, rf_w) + rf_b).astype(jnp.float32)
    fine_sel = jnp.take_along_axis(fine, g_idx[:, None, None], axis=1)[:, 0]
    top_val, top_idx = lax.top_k(fine_sel, TOP_K_INNER)
    gate = p_group * jax.nn.softmax(top_val, axis=-1)
    expert_id = g_idx[:, None] * EXPERTS_PER_GROUP + top_idx.astype(jnp.int32)
    return grouped_expert_ffn(ht, expert_id, gate, w_gate, w_up, w_down)


def setup_inputs(seed: int = 0) -> dict:
    key = jax.random.key(seed)
    ks = jax.random.split(key, 20)
    f32 = jnp.float32

    def nrm(k, shape, scale):
        return jax.random.normal(k, shape, f32) * scale

    col_scale = jnp.asarray(np.concatenate(
        [np.full((sz,), DEEPNORM_BETA if is_v else 1.0, np.float32) for sz, is_v in zip(SPLIT_SIZES, SPLIT_IS_VALUE)]))
    L = DEPTH
    return {
        "x": nrm(ks[0], (BATCH, SEQ, D_MODEL), 1.0),
        "w_in": nrm(ks[1], (L, D_MODEL, PROJ_WIDTH), D_MODEL ** -0.5) * col_scale,
        "gla_gate_w2": nrm(ks[2], (L, GLA_GATE_RANK, GLA_KEY_WIDTH), GLA_GATE_RANK ** -0.5),
        "gla_gate_b": nrm(ks[3], (L, GLA_KEY_WIDTH), 0.1),
        "gla_norm_g": 1.0 + nrm(ks[4], (L, GLA_DV), 0.02),
        "dil_norm_g": 1.0 + nrm(ks[5], (L, DIL_DH), 0.02),
        "w_out": nrm(ks[6], (L, MIX_WIDTH, D_MODEL), MIX_WIDTH ** -0.5 * DEEPNORM_BETA),
        "ln1_g": 1.0 + nrm(ks[7], (L, D_MODEL), 0.02),
        "ln1_b": nrm(ks[8], (L, D_MODEL), 0.02),
        "router_coarse_w": nrm(ks[9], (L, D_MODEL, N_GROUPS), D_MODEL ** -0.5),
        "router_coarse_b": nrm(ks[10], (L, N_GROUPS), 0.01),
        "router_fine_w": nrm(ks[11], (L, D_MODEL, N_GROUPS, EXPERTS_PER_GROUP), D_MODEL ** -0.5),
        "router_fine_b": nrm(ks[12], (L, N_GROUPS, EXPERTS_PER_GROUP), 0.01),
        "expert_w_gate": nrm(ks[13], (L, N_EXPERTS, D_MODEL, D_FF_EXPERT), D_MODEL ** -0.5),
        "expert_w_up": nrm(ks[14], (L, N_EXPERTS, D_MODEL, D_FF_EXPERT), D_MODEL ** -0.5 * DEEPNORM_BETA),
        "expert_w_down": nrm(ks[15], (L, N_EXPERTS, D_FF_EXPERT, D_MODEL), D_FF_EXPERT ** -0.5 * DEEPNORM_BETA),
        "ln2_g": 1.0 + nrm(ks[16], (L, D_MODEL), 0.02),
        "ln2_b": nrm(ks[17], (L, D_MODEL), 0.02),
    }


def reference(x, w_in, gla_gate_w2, gla_gate_b, gla_norm_g, dil_norm_g, w_out, ln1_g, ln1_b,
              router_coarse_w, router_coarse_b, router_fine_w, router_fine_b,
              expert_w_gate, expert_w_up, expert_w_down, ln2_g, ln2_b):
    B, S, D = x.shape
    slopes = jnp.exp2(-8.0 * jnp.arange(1, DIL_HEADS + 1, dtype=jnp.float32) / DIL_HEADS)
    h = x
    for l in range(DEPTH):
        proj = jnp.einsum('bsd,dp->bsp', h, w_in[l])
        g_q, g_k, g_v, g_r, g_a, d_q, d_k, d_v = jnp.split(proj, SPLIT_POINTS, axis=-1)
        log_a = jax.nn.log_sigmoid((g_a @ gla_gate_w2[l] + gla_gate_b[l]).astype(jnp.float32)) / GLA_GATE_TEMP
        o_gla = gla_chunked(to_heads(g_q, GLA_HEADS) * (GLA_DK ** -0.5), to_heads(g_k, GLA_HEADS),
                            to_heads(g_v, GLA_HEADS), to_heads(log_a, GLA_HEADS))
        o_gla = from_heads(rms_norm(o_gla, gla_norm_g[l])) * jax.nn.silu(g_r)
        o_dil = dilated_mixture(to_heads(d_q, DIL_HEADS), to_heads(d_k, DIL_HEADS),
                                to_heads(d_v, DIL_HEADS), slopes)
        o_dil = from_heads(rms_norm(o_dil, dil_norm_g[l]))
        mix = jnp.einsum('bsm,md->bsd', jnp.concatenate([o_gla, o_dil], axis=-1), w_out[l])
        h = layer_norm(DEEPNORM_ALPHA * h + mix, ln1_g[l], ln1_b[l])
        ffn = hier_moe(h.reshape(B * S, D), router_coarse_w[l], router_coarse_b[l], router_fine_w[l],
                       router_fine_b[l], expert_w_gate[l], expert_w_up[l], expert_w_down[l])
        h = layer_norm(DEEPNORM_ALPHA * h + ffn.reshape(B, S, D), ln2_g[l], ln2_b[l])
    return h
```

```python
import functools
import math

import jax
import jax.numpy as jnp
import numpy as np
from jax import lax
from jax.experimental import pallas as pl
from jax.experimental.pallas import tpu as pltpu

D_MODEL = 1024
GLA_HEADS = 4
GLA_DK = 64
GLA_DV = 128
GLA_KEY_WIDTH = GLA_HEADS * GLA_DK
GLA_WIDTH = GLA_HEADS * GLA_DV
GLA_GATE_RANK = 16
GLA_GATE_TEMP = 16.0
DIL_HEADS = 8
DIL_DH = 64
DIL_WIDTH = DIL_HEADS * DIL_DH
DIL_CONFIGS = ((128, 1), (512, 4), (2048, 16))
DIL_BLOCK = 128
N_GROUPS = 4
EXPERTS_PER_GROUP = 8
N_EXPERTS = N_GROUPS * EXPERTS_PER_GROUP
D_FF = 512
DEEPNORM_ALPHA = 2.0 ** 0.25
EPS = 1e-5

LANES = 128
GLA_CHUNK = 64
ROW_BLOCK = 256
VMEM_LIMIT = 56 * 1024 * 1024

F32 = jnp.float32
BF16 = jnp.bfloat16


def _dot(a, b):
    return jnp.dot(a, b, preferred_element_type=F32)


def _dot_nt(a, b):
    return lax.dot_general(a, b, (((1,), (1,)), ((), ())), preferred_element_type=F32)


def _dot_tn(a, b):
    return lax.dot_general(a, b, (((0,), (0,)), ((), ())), preferred_element_type=F32)


def _split_bf16(v):
    hi = v.astype(BF16)
    lo = (v - hi.astype(F32)).astype(BF16)
    return hi, lo


def _layer_norm(v, g, b):
    mu = jnp.mean(v, axis=-1, keepdims=True)
    c = v - mu
    var = jnp.mean(c * c, axis=-1, keepdims=True)
    return c * lax.rsqrt(var + EPS) * g + b


def _in_proj_kernel(x_ref, wm_ref, wa_ref, w2h_ref, w2l_ref, gb_ref,
                    q_ref, k_ref, v_ref, r_ref, la_ref, dq_ref, dk_ref, dv_ref):
    xb = x_ref[...].astype(BF16)

    def piece(c0, c1):
        return _dot(xb, wm_ref[:, c0:c1])

    q_ref[...] = (piece(0, 256) * (GLA_DK ** -0.5)).astype(BF16)
    k_ref[...] = piece(256, 512).astype(BF16)
    v_ref[...] = piece(512, 1024).astype(BF16)
    r_ref[...] = piece(1024, 1536).astype(BF16)
    dq_ref[...] = (piece(1536, 2048) * (DIL_DH ** -0.5)).astype(BF16)
    dk_ref[...] = piece(2048, 2560).astype(BF16)
    dv_ref[...] = piece(2560, 3072).astype(BF16)
    ga = _dot(xb, wa_ref[...])
    ga_hi, ga_lo = _split_bf16(ga)
    z = _dot(ga_hi, w2h_ref[...]) + _dot(ga_lo, w2h_ref[...]) + _dot(ga_hi, w2l_ref[...]) + gb_ref[...]
    log_sig = jnp.minimum(z, 0.0) - jnp.log1p(jnp.exp(-jnp.abs(z)))
    la_ref[...] = log_sig * (1.0 / GLA_GATE_TEMP)


def _in_proj(x2, wm, wa, w2h, w2l, gb, tm=512):
    T = x2.shape[0]
    row = lambda w: pl.BlockSpec((tm, w), lambda i: (i, 0))
    full = lambda a: pl.BlockSpec(a.shape, lambda i: (0,) * a.ndim)
    outs = [(256, BF16), (256, BF16), (512, BF16), (512, BF16), (256, F32),
            (512, BF16), (512, BF16), (512, BF16)]
    return pl.pallas_call(
        _in_proj_kernel,
        grid=(T // tm,),
        in_specs=[row(D_MODEL), full(wm), full(wa), full(w2h), full(w2l), full(gb)],
        out_specs=[row(w) for w, _ in outs],
        out_shape=[jax.ShapeDtypeStruct((T, w), dt) for w, dt in outs],
        compiler_params=pltpu.CompilerParams(
            dimension_semantics=("parallel",), vmem_limit_bytes=VMEM_LIMIT),
        name="in_proj",
    )(x2, wm, wa, w2h, w2l, gb)


def _gla_kernel(q_ref, k_ref, v_ref, r_ref, la_ref, g_ref, o_ref, s_ref, *, seq_block):
    C = GLA_CHUNK

    @pl.when(pl.program_id(1) == 0)
    def _():
        s_ref[...] = jnp.zeros_like(s_ref)

    ri = lax.broadcasted_iota(jnp.int32, (C, C), 0)
    ci = lax.broadcasted_iota(jnp.int32, (C, C), 1)
    causal = ci <= ri
    tri = causal.astype(BF16)
    ones_cl = jnp.ones((C, LANES), BF16)
    lane_k = lax.broadcasted_iota(jnp.int32, (1, GLA_KEY_WIDTH), 1) // GLA_DK
    head_masks = [(lane_k == h).astype(F32) for h in range(GLA_HEADS)]
    srow = lax.broadcasted_iota(jnp.int32, (GLA_KEY_WIDTH, GLA_WIDTH), 0) // GLA_DK
    scol = lax.broadcasted_iota(jnp.int32, (GLA_KEY_WIDTH, GLA_WIDTH), 1) // GLA_DV
    state_mask = (srow == scol).astype(F32)
    g = g_ref[...]

    def chunk(c, carry):
        rows = pl.ds(pl.multiple_of(c * C, C), C)
        la = la_ref[rows, :]
        la_hi, la_lo = _split_bf16(la)
        b = _dot(tri, la_hi) + _dot(tri, la_lo)
        b_last = b[C - 1:C, :]
        q = q_ref[rows, :].astype(F32)
        k = k_ref[rows, :].astype(F32)
        v = v_ref[rows, :]
        qd = q * jnp.exp(b)
        kd = (k * jnp.exp(-b)).astype(BF16)
        ke = (k * jnp.exp(b_last - b)).astype(BF16)
        state = s_ref[...]
        o_inter = _dot(qd.astype(BF16), state.astype(BF16))
        outs = []
        for h in range(GLA_HEADS):
            a = _dot_nt((qd * head_masks[h]).astype(BF16), kd)
            a = jnp.where(causal, a, 0.0).astype(BF16)
            cols = slice(h * GLA_DV, (h + 1) * GLA_DV)
            o = _dot(a, v[:, cols]) + o_inter[:, cols]
            o = o * lax.rsqrt(jnp.mean(o * o, axis=-1, keepdims=True) + EPS) * g
            outs.append(o)
        o_all = jnp.concatenate(outs, axis=-1)
        rr = r_ref[rows, :].astype(F32)
        o_ref[rows, :] = (o_all * (rr * jax.nn.sigmoid(rr))).astype(BF16)
        tot = _dot_tn(la_hi, ones_cl) + _dot_tn(la_lo, ones_cl)
        dec = jnp.exp(tot)
        upd = _dot_tn(ke, v) * state_mask
        for h in range(GLA_HEADS):
            cols = slice(h * GLA_DV, (h + 1) * GLA_DV)
            s_ref[:, cols] = state[:, cols] * dec + upd[:, cols]
        return carry

    lax.fori_loop(0, seq_block // C, chunk, 0)


def _gla(q, k, v, r, la, g, batch, seq, seq_block=1024):
    nsb = seq // seq_block
    row = lambda w: pl.BlockSpec((seq_block, w), lambda b, s: (b * nsb + s, 0))
    return pl.pallas_call(
        functools.partial(_gla_kernel, seq_block=seq_block),
        grid=(batch, nsb),
        in_specs=[row(256), row(256), row(512), row(512), row(256),
                  pl.BlockSpec((1, GLA_DV), lambda b, s: (0, 0))],
        out_specs=row(512),
        out_shape=jax.ShapeDtypeStruct((batch * seq, GLA_WIDTH), BF16),
        scratch_shapes=[pltpu.VMEM((GLA_KEY_WIDTH, GLA_WIDTH), F32)],
        compiler_params=pltpu.CompilerParams(
            dimension_semantics=("parallel", "arbitrary"), vmem_limit_bytes=VMEM_LIMIT),
        name="gla",
    )(q, k, v, r, la, g)


def _dil_kernel(slope_ref, q_ref, k_ref, v_ref, g_ref, o_ref,
                qf, kf, vf, oc, lc, *, seq):
    B = DIL_BLOCK
    pair = pl.program_id(1)
    qf[...] = q_ref[...].astype(F32)
    kf[...] = k_ref[...].astype(F32)
    vf[...] = v_ref[...].astype(F32)

    lane = lax.broadcasted_iota(jnp.int32, (1, LANES), 1)
    first = lane < DIL_DH
    ii = lax.broadcasted_iota(jnp.int32, (B, B), 0)
    jj = lax.broadcasted_iota(jnp.int32, (B, B), 1)
    rel_prev = (ii + B - jj).astype(F32)
    rel_cur = (ii - jj).astype(F32)
    ok_prev = jj >= ii
    ok_cur = jj <= ii
    neg = jnp.float32(-jnp.inf)

    for cfg, (window, r) in enumerate(DIL_CONFIGS):
        nb = seq // r // B

        def block(it, carry, cfg=cfg, r=r, nb=nb):
            c = it // nb
            n = it % nb
            start = c + n * (B * r)
            start_prev = jnp.maximum(start - B * r, c)
            has_prev = n > 0
            rows = pl.ds(start, B, stride=r) if r > 1 else pl.ds(start, B)
            rows_prev = pl.ds(start_prev, B, stride=r) if r > 1 else pl.ds(start_prev, B)
            q = qf[rows, :]
            kc = kf[rows, :].astype(BF16)
            vc = vf[rows, :].astype(BF16)
            kp = kf[rows_prev, :].astype(BF16)
            vp = vf[rows_prev, :].astype(BF16)
            o_pair = None
            l_pair = None
            for hh in range(2):
                slope = slope_ref[2 * pair + hh] * float(r)
                hmask = first if hh == 0 else jnp.logical_not(first)
                qh = jnp.where(hmask, q, 0.0).astype(BF16)
                s_prev = _dot_nt(qh, kp) - slope * rel_prev
                s_cur = _dot_nt(qh, kc) - slope * rel_cur
                s_prev = jnp.where(jnp.logical_and(ok_prev, has_prev), s_prev, neg)
                s_cur = jnp.where(ok_cur, s_cur, neg)
                m = jnp.maximum(jnp.max(s_prev, axis=-1, keepdims=True),
                                jnp.max(s_cur, axis=-1, keepdims=True))
                p_prev = jnp.exp(s_prev - m)
                p_cur = jnp.exp(s_cur - m)
                z = jnp.sum(p_prev, axis=-1, keepdims=True) + jnp.sum(p_cur, axis=-1, keepdims=True)
                inv = 1.0 / z
                o = _dot((p_prev * inv).astype(BF16), vp) + _dot((p_cur * inv).astype(BF16), vc)
                lse = jnp.broadcast_to(m + jnp.log(z), (B, LANES))
                if hh == 0:
                    o_pair, l_pair = o, lse
                else:
                    o_pair = jnp.where(first, o_pair, o)
                    l_pair = jnp.where(first, l_pair, lse)
            oc[cfg, rows, :] = o_pair
            lc[cfg, rows, :] = l_pair
            return carry

        lax.fori_loop(0, seq // B, block, 0)

    g = g_ref[...]
    CH = 512

    def mix(i, carry):
        rows = pl.ds(pl.multiple_of(i * CH, CH), CH)
        l0, l1, l2 = lc[0, rows, :], lc[1, rows, :], lc[2, rows, :]
        m = jnp.maximum(jnp.maximum(l0, l1), l2)
        e0, e1, e2 = jnp.exp(l0 - m), jnp.exp(l1 - m), jnp.exp(l2 - m)
        den = e0 + e1 + e2
        o = (e0 / den) * oc[0, rows, :] + (e1 / den) * oc[1, rows, :] + (e2 / den) * oc[2, rows, :]
        sq = o * o
        ms_a = jnp.sum(jnp.where(first, sq, 0.0), axis=-1, keepdims=True) * (1.0 / DIL_DH)
        ms_b = jnp.sum(jnp.where(first, 0.0, sq), axis=-1, keepdims=True) * (1.0 / DIL_DH)
        ms = jnp.where(first, ms_a, ms_b)
        o_ref[rows, :] = (o * lax.rsqrt(ms + EPS) * g).astype(BF16)
        return carry

    lax.fori_loop(0, seq // CH, mix, 0)


def _dilated(slopes, dq, dk, dv, g2, batch, seq):
    blk = pl.BlockSpec((seq, LANES), lambda b, p, s: (b, p))
    return pl.pallas_call(
        functools.partial(_dil_kernel, seq=seq),
        grid_spec=pltpu.PrefetchScalarGridSpec(
            num_scalar_prefetch=1,
            grid=(batch, DIL_WIDTH // LANES),
            in_specs=[blk, blk, blk, pl.BlockSpec((1, LANES), lambda b, p, s: (0, 0))],
            out_specs=blk,
            scratch_shapes=[pltpu.VMEM((seq, LANES), F32)] * 3
                           + [pltpu.VMEM((3, seq, LANES), F32)] * 2,
        ),
        out_shape=jax.ShapeDtypeStruct((batch * seq, DIL_WIDTH), BF16),
        compiler_params=pltpu.CompilerParams(
            dimension_semantics=("parallel", "parallel"), vmem_limit_bytes=VMEM_LIMIT),
        name="dilated",
    )(slopes, dq, dk, dv, g2)


def _out_proj_kernel(og_ref, od_ref, x_ref, wg_ref, wd_ref, g_ref, b_ref,
                     rwh_ref, rwl_ref, rb_ref, h_ref, eid_ref, gate_ref):
    mix = _dot(og_ref[...], wg_ref[...]) + _dot(od_ref[...], wd_ref[...])
    h = _layer_norm(DEEPNORM_ALPHA * x_ref[...] + mix, g_ref[...], b_ref[...])
    h_ref[...] = h
    h_hi, h_lo = _split_bf16(h)
    logits = (_dot(h_hi, rwh_ref[...]) + _dot(h_lo, rwh_ref[...]) + _dot(h_hi, rwl_ref[...])
              + rb_ref[...])
    lane = lax.broadcasted_iota(jnp.int32, logits.shape, 1)
    lane_f = lane.astype(F32)
    neg = jnp.float32(-jnp.inf)
    big = jnp.float32(1e9)
    is_coarse = jnp.logical_and(lane >= N_EXPERTS, lane < N_EXPERTS + N_GROUPS)
    coarse = jnp.where(is_coarse, logits, neg)
    cmax = jnp.max(coarse, axis=-1, keepdims=True)
    g_idx = jnp.min(jnp.where(coarse == cmax, lane_f, big), axis=-1, keepdims=True) - N_EXPERTS
    p_group = 1.0 / jnp.sum(jnp.exp(coarse - cmax), axis=-1, keepdims=True)
    lo = g_idx * EXPERTS_PER_GROUP
    in_group = jnp.logical_and(lane_f >= lo, lane_f < lo + EXPERTS_PER_GROUP)
    fine = jnp.where(in_group, logits, neg)
    v1 = jnp.max(fine, axis=-1, keepdims=True)
    i1 = jnp.min(jnp.where(fine == v1, lane_f, big), axis=-1, keepdims=True)
    fine2 = jnp.where(lane_f == i1, neg, fine)
    v2 = jnp.max(fine2, axis=-1, keepdims=True)
    i2 = jnp.min(jnp.where(fine2 == v2, lane_f, big), axis=-1, keepdims=True)
    e2 = jnp.exp(v2 - v1)
    den = 1.0 + e2
    gate1 = p_group * (1.0 / den)
    gate2 = p_group * (e2 / den)
    lane2 = lax.broadcasted_iota(jnp.int32, (logits.shape[0], 2), 1)
    eid_ref[...] = jnp.where(lane2 == 0, i1, i2).astype(jnp.int32)
    gate_ref[...] = jnp.where(lane2 == 0, gate1, gate2)


def _out_proj(og, od, x2, wg, wd, g, b, rwh, rwl, rb, tm=512):
    T = x2.shape[0]
    row = lambda w: pl.BlockSpec((tm, w), lambda i: (i, 0))
    full = lambda a: pl.BlockSpec(a.shape, lambda i: (0,) * a.ndim)
    return pl.pallas_call(
        _out_proj_kernel,
        grid=(T // tm,),
        in_specs=[row(512), row(512), row(D_MODEL), full(wg), full(wd), full(g), full(b),
                  full(rwh), full(rwl), full(rb)],
        out_specs=[row(D_MODEL), row(2), row(2)],
        out_shape=[jax.ShapeDtypeStruct((T, D_MODEL), F32),
                   jax.ShapeDtypeStruct((T, 2), jnp.int32),
                   jax.ShapeDtypeStruct((T, 2), F32)],
        compiler_params=pltpu.CompilerParams(
            dimension_semantics=("parallel",), vmem_limit_bytes=VMEM_LIMIT),
        name="out_proj_router",
    )(og, od, x2, wg, wd, g, b, rwh, rwl, rb)


def _positions_kernel(eid_ref, dest_ref, be_ref, nv_ref, carry_ref, cnt_col_ref, sp_ref, *, tb):
    phase = pl.program_id(0)
    i = pl.program_id(1)
    lane = lax.broadcasted_iota(jnp.int32, (tb, LANES), 1)
    eid = eid_ref[...]
    oh1 = lane == eid[:, 0:1]
    oh2 = lane == eid[:, 1:2]
    oh = jnp.logical_or(oh1, oh2).astype(BF16)

    @pl.when(jnp.logical_and(phase == 0, i == 0))
    def _():
        carry_ref[...] = jnp.zeros_like(carry_ref)
        cnt_col_ref[...] = jnp.zeros_like(cnt_col_ref)

    @pl.when(phase == 0)
    def _():
        ones_r = jnp.ones((8, tb), BF16)
        ones_c = jnp.ones((tb, LANES), BF16)
        carry_ref[...] += _dot(ones_r, oh)
        cnt_col_ref[...] += _dot_tn(oh, ones_c)

    @pl.when(jnp.logical_and(phase == 1, i == 0))
    def _():
        shift = int(math.log2(ROW_BLOCK))
        nb_row = ((carry_ref[...].astype(jnp.int32) + (ROW_BLOCK - 1)) >> shift)
        nb_col = ((cnt_col_ref[...].astype(jnp.int32) + (ROW_BLOCK - 1)) >> shift)
        r = lax.broadcasted_iota(jnp.int32, (LANES, LANES), 0)
        c = lax.broadcasted_iota(jnp.int32, (LANES, LANES), 1)
        excl = (r < c).astype(BF16)
        incl = (c <= r).astype(BF16)
        start_blk = _dot(nb_row.astype(F32).astype(BF16), excl)
        sp_ref[...] = start_blk * float(ROW_BLOCK)
        end_col = _dot(incl, nb_col.astype(F32).astype(BF16))
        end2 = jnp.concatenate([end_col, end_col], axis=1)
        nidx = lax.broadcasted_iota(jnp.int32, (LANES, 2 * LANES), 1).astype(F32)
        eidx = lax.broadcasted_iota(jnp.int32, (LANES, 2 * LANES), 0)
        hit = jnp.logical_and(end2 <= nidx, eidx < N_EXPERTS).astype(F32)
        be = jnp.minimum(jnp.sum(hit, axis=0, keepdims=True), float(N_EXPERTS - 1))
        be_ref[...] = be.astype(jnp.int32)
        lane1 = lax.broadcasted_iota(jnp.int32, (1, LANES), 1)
        total = jnp.sum(jnp.where(lane1 < N_EXPERTS, nb_row[0:1, :].astype(F32), 0.0), axis=-1, keepdims=True)
        nv_ref[...] = jnp.broadcast_to(total, (1, LANES)).astype(jnp.int32)
        carry_ref[...] = jnp.zeros_like(carry_ref)

    @pl.when(phase == 1)
    def _():
        r = lax.broadcasted_iota(jnp.int32, (tb, tb), 0)
        c = lax.broadcasted_iota(jnp.int32, (tb, tb), 1)
        strict = (c < r).astype(BF16)
        before = _dot(strict, oh) + carry_ref[0:1, :] + sp_ref[0:1, :]
        d1 = jnp.sum(jnp.where(oh1, before, 0.0), axis=-1, keepdims=True)
        d2 = jnp.sum(jnp.where(oh2, before, 0.0), axis=-1, keepdims=True)
        lane2 = lax.broadcasted_iota(jnp.int32, (tb, 2), 1)
        dest_ref[...] = jnp.where(lane2 == 0, d1, d2).astype(jnp.int32)
        carry_ref[...] += _dot(jnp.ones((8, tb), BF16), oh)


def _positions(eid, n_blocks_pad, tb=512):
    T = eid.shape[0]
    nb = T // tb
    return pl.pallas_call(
        functools.partial(_positions_kernel, tb=tb),
        grid=(2, nb),
        in_specs=[pl.BlockSpec((tb, 2), lambda p, i: (i, 0))],
        out_specs=[pl.BlockSpec((tb, 2), lambda p, i: (i * p, 0)),
                   pl.BlockSpec((1, n_blocks_pad), lambda p, i: (0, 0)),
                   pl.BlockSpec((1, LANES), lambda p, i: (0, 0))],
        out_shape=[jax.ShapeDtypeStruct((T, 2), jnp.int32),
                   jax.ShapeDtypeStruct((1, n_blocks_pad), jnp.int32),
                   jax.ShapeDtypeStruct((1, LANES), jnp.int32)],
        scratch_shapes=[pltpu.VMEM((8, LANES), F32), pltpu.VMEM((LANES, LANES), F32),
                        pltpu.VMEM((8, LANES), F32)],
        compiler_params=pltpu.CompilerParams(dimension_semantics=("arbitrary", "arbitrary")),
        name="positions",
    )(eid)


def _dispatch_kernel(dest_ref, h_ref, xs_in_ref, xs_ref, sem, *, tm):
    del xs_in_ref
    base = pl.program_id(0) * tm

    def issue(i, carry):
        for kk in range(2):
            d = dest_ref[2 * (base + i) + kk]
            pltpu.make_async_copy(h_ref.at[pl.ds(i, 1)], xs_ref.at[pl.ds(d, 1)], sem).start()
        return carry

    lax.fori_loop(0, tm, issue, 0, unroll=8)

    def drain(i, carry):
        for kk in range(2):
            pltpu.make_async_copy(h_ref.at[pl.ds(0, 1)], xs_ref.at[pl.ds(0, 1)], sem).wait()
        return carry

    lax.fori_loop(0, tm, drain, 0, unroll=8)


def _dispatch(dest_flat, h, xs_init, tm=256):
    T = h.shape[0]
    return pl.pallas_call(
        functools.partial(_dispatch_kernel, tm=tm),
        grid_spec=pltpu.PrefetchScalarGridSpec(
            num_scalar_prefetch=1,
            grid=(T // tm,),
            in_specs=[pl.BlockSpec((tm, D_MODEL), lambda i, d: (i, 0)),
                      pl.BlockSpec(memory_space=pl.ANY)],
            out_specs=pl.BlockSpec(memory_space=pl.ANY),
            scratch_shapes=[pltpu.SemaphoreType.DMA],
        ),
        out_shape=jax.ShapeDtypeStruct(xs_init.shape, xs_init.dtype),
        input_output_aliases={2: 0},
        compiler_params=pltpu.CompilerParams(dimension_semantics=("arbitrary",)),
        name="dispatch",
    )(dest_flat, h, xs_init)


def _ffn_kernel(be_ref, nv_ref, xs_ref, wg_ref, wu_ref, wd_ref, y_ref, wgb, wub, wdb):
    n = pl.program_id(0)
    prev = be_ref[jnp.maximum(n - 1, 0)]
    fresh = jnp.logical_or(n == 0, be_ref[n] != prev)

    @pl.when(jnp.logical_and(fresh, n < nv_ref[0]))
    def _():
        wgb[...] = wg_ref[0].astype(BF16)
        wub[...] = wu_ref[0].astype(BF16)
        wdb[...] = wd_ref[0].astype(BF16)

    @pl.when(n < nv_ref[0])
    def _():
        xb = xs_ref[...].astype(BF16)
        a = _dot(xb, wgb[...])
        u = _dot(xb, wub[...])
        hid = (a * jax.nn.sigmoid(a) * u).astype(BF16)
        y_ref[...] = _dot(hid, wdb[...])

    @pl.when(n >= nv_ref[0])
    def _():
        y_ref[...] = jnp.zeros_like(y_ref)


def _ffn(be, nv, xs, w_gate, w_up, w_down):
    n_rows = xs.shape[0]
    n_blocks = n_rows // ROW_BLOCK

    def blk(n, be, nv):
        return (jnp.minimum(n, nv[0] - 1), 0)

    def wsel(n, be, nv):
        return (be[jnp.minimum(n, nv[0] - 1)], 0, 0)

    return pl.pallas_call(
        _ffn_kernel,
        grid_spec=pltpu.PrefetchScalarGridSpec(
            num_scalar_prefetch=2,
            grid=(n_blocks,),
            in_specs=[pl.BlockSpec((ROW_BLOCK, D_MODEL), blk),
                      pl.BlockSpec((1, D_MODEL, D_FF), wsel),
                      pl.BlockSpec((1, D_MODEL, D_FF), wsel),
                      pl.BlockSpec((1, D_FF, D_MODEL), wsel)],
            out_specs=pl.BlockSpec((ROW_BLOCK, D_MODEL), lambda n, be, nv: (n, 0)),
            scratch_shapes=[pltpu.VMEM((D_MODEL, D_FF), BF16), pltpu.VMEM((D_MODEL, D_FF), BF16),
                            pltpu.VMEM((D_FF, D_MODEL), BF16)],
        ),
        out_shape=jax.ShapeDtypeStruct((n_rows, D_MODEL), F32),
        compiler_params=pltpu.CompilerParams(
            dimension_semantics=("arbitrary",), vmem_limit_bytes=VMEM_LIMIT),
        name="expert_ffn",
    )(be, nv, xs, w_gate, w_up, w_down)


def _combine_kernel(dest_ref, h_ref, gate_ref, g_ref, b_ref, y_ref, o_ref, ybuf, sem, *, tm):
    base = pl.program_id(0) * tm

    def issue(i, carry):
        for kk in range(2):
            d = dest_ref[2 * (base + i) + kk]
            pltpu.make_async_copy(y_ref.at[pl.ds(d, 1)], ybuf.at[kk, pl.ds(i, 1)], sem).start()
        return carry

    lax.fori_loop(0, tm, issue, 0, unroll=8)

    def drain(i, carry):
        for kk in range(2):
            pltpu.make_async_copy(y_ref.at[pl.ds(0, 1)], ybuf.at[kk, pl.ds(0, 1)], sem).wait()
        return carry

    lax.fori_loop(0, tm, drain, 0, unroll=8)

    gate = gate_ref[...]
    ffn = ybuf[0] * gate[:, 0:1] + ybuf[1] * gate[:, 1:2]
    o_ref[...] = _layer_norm(DEEPNORM_ALPHA * h_ref[...] + ffn, g_ref[...], b_ref[...])


def _combine(dest_flat, h, gate, g, b, y, tm=256):
    T = h.shape[0]
    return pl.pallas_call(
        functools.partial(_combine_kernel, tm=tm),
        grid_spec=pltpu.PrefetchScalarGridSpec(
            num_scalar_prefetch=1,
            grid=(T // tm,),
            in_specs=[pl.BlockSpec((tm, D_MODEL), lambda i, d: (i, 0)),
                      pl.BlockSpec((tm, 2), lambda i, d: (i, 0)),
                      pl.BlockSpec((1, D_MODEL), lambda i, d: (0, 0)),
                      pl.BlockSpec((1, D_MODEL), lambda i, d: (0, 0)),
                      pl.BlockSpec(memory_space=pl.ANY)],
            out_specs=pl.BlockSpec((tm, D_MODEL), lambda i, d: (i, 0)),
            scratch_shapes=[pltpu.VMEM((2, tm, D_MODEL), F32), pltpu.SemaphoreType.DMA],
        ),
        out_shape=jax.ShapeDtypeStruct((T, D_MODEL), F32),
        compiler_params=pltpu.CompilerParams(dimension_semantics=("arbitrary",)),
        name="combine",
    )(dest_flat, h, gate, g, b, y)


def kernel(x, w_in, gla_gate_w2, gla_gate_b, gla_norm_g, dil_norm_g, w_out, ln1_g, ln1_b,
           router_coarse_w, router_coarse_b, router_fine_w, router_fine_b,
           expert_w_gate, expert_w_up, expert_w_down, ln2_g, ln2_b):
    B, S, D = x.shape
    T = B * S
    depth = w_in.shape[0]
    slopes = jnp.exp2(-8.0 * jnp.arange(1, DIL_HEADS + 1, dtype=F32) / DIL_HEADS)
    n_rows = 2 * T + N_EXPERTS * ROW_BLOCK
    n_blocks = n_rows // ROW_BLOCK
    n_blocks_pad = -(-n_blocks // (2 * LANES)) * (2 * LANES)
    assert n_blocks_pad == 2 * LANES
    a0 = 1536
    h = x.reshape(T, D)
    for l in range(depth):
        w = w_in[l]
        wm = jnp.concatenate([w[:, :a0], w[:, a0 + GLA_GATE_RANK:]], axis=1).astype(BF16)
        wa = jnp.pad(w[:, a0:a0 + GLA_GATE_RANK], ((0, 0), (0, LANES - GLA_GATE_RANK))).astype(BF16)
        w2 = jnp.pad(gla_gate_w2[l], ((0, LANES - GLA_GATE_RANK), (0, 0)))
        w2h, w2l = _split_bf16(w2)
        q, k, v, r, la, dq, dk, dv = _in_proj(h, wm, wa, w2h, w2l, gla_gate_b[l][None, :])
        o_gla = _gla(q, k, v, r, la, gla_norm_g[l][None, :], B, S)
        g2 = jnp.tile(dil_norm_g[l], 2)[None, :]
        o_dil = _dilated(slopes, dq, dk, dv, g2, B, S)
        wo = w_out[l].astype(BF16)
        rw = jnp.concatenate([router_fine_w[l].reshape(D, N_EXPERTS), router_coarse_w[l]], axis=1)
        rw = jnp.pad(rw, ((0, 0), (0, LANES - N_EXPERTS - N_GROUPS)))
        rwh, rwl = _split_bf16(rw)
        rb = jnp.concatenate([router_fine_b[l].reshape(N_EXPERTS), router_coarse_b[l]])
        rb = jnp.pad(rb, (0, LANES - N_EXPERTS - N_GROUPS))[None, :]
        h1, eid, gate = _out_proj(o_gla, o_dil, h, wo[:GLA_WIDTH], wo[GLA_WIDTH:],
                                  ln1_g[l][None, :], ln1_b[l][None, :], rwh, rwl, rb)
        dest, be, nv = _positions(eid, n_blocks_pad)
        dest_flat = dest.reshape(2 * T)
        xs = _dispatch(dest_flat, h1, jnp.zeros((n_rows, D), F32))
        y = _ffn(be.reshape(n_blocks_pad), nv.reshape(LANES)[:1], xs,
                 expert_w_gate[l], expert_w_up[l], expert_w_down[l])
        h = _combine(dest_flat, h1, gate, ln2_g[l][None, :], ln2_b[l][None, :], y)
    return h.reshape(B, S, D)
```

```python
import functools
import math

import jax
import jax.numpy as jnp
import numpy as np
from jax import lax
from jax.experimental import pallas as pl
from jax.experimental.pallas import tpu as pltpu

D_MODEL = 1024
GLA_HEADS = 4
GLA_DK = 64
GLA_DV = 128
GLA_KEY_WIDTH = GLA_HEADS * GLA_DK
GLA_WIDTH = GLA_HEADS * GLA_DV
GLA_GATE_RANK = 16
GLA_GATE_TEMP = 16.0
DIL_HEADS = 8
DIL_DH = 64
DIL_WIDTH = DIL_HEADS * DIL_DH
DIL_CONFIGS = ((128, 1), (512, 4), (2048, 16))
DIL_BLOCK = 128
DIL_PAD = DIL_BLOCK * max(r for _, r in DIL_CONFIGS)
DIL_UNROLL = 4
N_GROUPS = 4
EXPERTS_PER_GROUP = 8
N_EXPERTS = N_GROUPS * EXPERTS_PER_GROUP
D_FF = 512
DEEPNORM_ALPHA = 2.0 ** 0.25
EPS = 1e-5

LANES = 128
GLA_CHUNK = 64
GLA_UNROLL = 4
ROW_BLOCK = 256
VMEM_LIMIT = 56 * 1024 * 1024

F32 = jnp.float32
BF16 = jnp.bfloat16


def _dot(a, b):
    return jnp.dot(a, b, preferred_element_type=F32)


def _dot_nt(a, b):
    return lax.dot_general(a, b, (((1,), (1,)), ((), ())), preferred_element_type=F32)


def _dot_tn(a, b):
    return lax.dot_general(a, b, (((0,), (0,)), ((), ())), preferred_element_type=F32)


def _split_bf16(v):
    hi = v.astype(BF16)
    lo = (v - hi.astype(F32)).astype(BF16)
    return hi, lo


def _layer_norm(v, g, b):
    mu = jnp.mean(v, axis=-1, keepdims=True)
    c = v - mu
    var = jnp.mean(c * c, axis=-1, keepdims=True)
    return c * lax.rsqrt(var + EPS) * g + b


def _in_proj_kernel(x_ref, wm_ref, wa_ref, w2h_ref, w2l_ref, gb_ref,
                    q_ref, k_ref, v_ref, r_ref, la_ref, dq_ref, dk_ref, dv_ref):
    xb = x_ref[...].astype(BF16)

    def piece(c0, c1):
        return _dot(xb, wm_ref[:, c0:c1])

    q_ref[...] = (piece(0, 256) * (GLA_DK ** -0.5)).astype(BF16)
    k_ref[...] = piece(256, 512).astype(BF16)
    v_ref[...] = piece(512, 1024).astype(BF16)
    r_ref[...] = piece(1024, 1536).astype(BF16)
    dq_ref[...] = (piece(1536, 2048) * (DIL_DH ** -0.5)).astype(BF16)
    dk_ref[...] = piece(2048, 2560).astype(BF16)
    dv_ref[...] = piece(2560, 3072).astype(BF16)
    ga = _dot(xb, wa_ref[...])
    ga_hi, ga_lo = _split_bf16(ga)
    z = _dot(ga_hi, w2h_ref[...]) + _dot(ga_lo, w2h_ref[...]) + _dot(ga_hi, w2l_ref[...]) + gb_ref[...]
    log_sig = jnp.minimum(z, 0.0) - jnp.log1p(jnp.exp(-jnp.abs(z)))
    la_ref[...] = log_sig * (1.0 / GLA_GATE_TEMP)


def _in_proj(x2, wm, wa, w2h, w2l, gb, tm=512):
    T = x2.shape[0]
    row = lambda w: pl.BlockSpec((tm, w), lambda i: (i, 0))
    full = lambda a: pl.BlockSpec(a.shape, lambda i: (0,) * a.ndim)
    outs = [(256, BF16), (256, BF16), (512, BF16), (512, BF16), (256, F32),
            (512, BF16), (512, BF16), (512, BF16)]
    return pl.pallas_call(
        _in_proj_kernel,
        grid=(T // tm,),
        in_specs=[row(D_MODEL), full(wm), full(wa), full(w2h), full(w2l), full(gb)],
        out_specs=[row(w) for w, _ in outs],
        out_shape=[jax.ShapeDtypeStruct((T, w), dt) for w, dt in outs],
        compiler_params=pltpu.CompilerParams(
            dimension_semantics=("parallel",), vmem_limit_bytes=VMEM_LIMIT),
        name="in_proj",
    )(x2, wm, wa, w2h, w2l, gb)


def _gla_kernel(q_ref, k_ref, v_ref, r_ref, la_ref, g_ref, o_ref, s_ref, *, seq_block):
    C = GLA_CHUNK

    @pl.when(pl.program_id(1) == 0)
    def _():
        s_ref[...] = jnp.zeros_like(s_ref)

    ri = lax.broadcasted_iota(jnp.int32, (C, C), 0)
    ci = lax.broadcasted_iota(jnp.int32, (C, C), 1)
    causal = ci <= ri
    tri = causal.astype(BF16)
    ones_cl = jnp.ones((C, LANES), BF16)
    lane_k = lax.broadcasted_iota(jnp.int32, (1, GLA_KEY_WIDTH), 1) // GLA_DK
    head_masks = [(lane_k == h).astype(F32) for h in range(GLA_HEADS)]
    srow = lax.broadcasted_iota(jnp.int32, (GLA_KEY_WIDTH, GLA_WIDTH), 0) // GLA_DK
    scol = lax.broadcasted_iota(jnp.int32, (GLA_KEY_WIDTH, GLA_WIDTH), 1) // GLA_DV
    state_mask = (srow == scol).astype(F32)
    g = g_ref[...]

    def chunk(c, carry):
        rows = pl.ds(pl.multiple_of(c * C, C), C)
        la = la_ref[rows, :]
        la_hi, la_lo = _split_bf16(la)
        b = _dot(tri, la_hi) + _dot(tri, la_lo)
        b_last = b[C - 1:C, :]
        q = q_ref[rows, :].astype(F32)
        k = k_ref[rows, :].astype(F32)
        v = v_ref[rows, :]
        qd = q * jnp.exp(b)
        kd = (k * jnp.exp(-b)).astype(BF16)
        ke = (k * jnp.exp(b_last - b)).astype(BF16)
        state = s_ref[...]
        o_inter = _dot(qd.astype(BF16), state.astype(BF16))
        outs = []
        for h in range(GLA_HEADS):
            a = _dot_nt((qd * head_masks[h]).astype(BF16), kd)
            a = jnp.where(causal, a, 0.0).astype(BF16)
            cols = slice(h * GLA_DV, (h + 1) * GLA_DV)
            o = _dot(a, v[:, cols]) + o_inter[:, cols]
            o = o * lax.rsqrt(jnp.mean(o * o, axis=-1, keepdims=True) + EPS) * g
            outs.append(o)
        o_all = jnp.concatenate(outs, axis=-1)
        rr = r_ref[rows, :].astype(F32)
        o_ref[rows, :] = (o_all * (rr * jax.nn.sigmoid(rr))).astype(BF16)
        tot = _dot_tn(la_hi, ones_cl) + _dot_tn(la_lo, ones_cl)
        dec = jnp.exp(tot)
        upd = _dot_tn(ke, v) * state_mask
        for h in range(GLA_HEADS):
            cols = slice(h * GLA_DV, (h + 1) * GLA_DV)
            s_ref[:, cols] = state[:, cols] * dec + upd[:, cols]
        return carry

    lax.fori_loop(0, seq_block // C, chunk, 0, unroll=GLA_UNROLL)


def _gla(q, k, v, r, la, g, batch, seq, seq_block=1024):
    nsb = seq // seq_block
    row = lambda w: pl.BlockSpec((seq_block, w), lambda b, s: (b * nsb + s, 0))
    return pl.pallas_call(
        functools.partial(_gla_kernel, seq_block=seq_block),
        grid=(batch, nsb),
        in_specs=[row(256), row(256), row(512), row(512), row(256),
                  pl.BlockSpec((1, GLA_DV), lambda b, s: (0, 0))],
        out_specs=row(512),
        out_shape=jax.ShapeDtypeStruct((batch * seq, GLA_WIDTH), BF16),
        scratch_shapes=[pltpu.VMEM((GLA_KEY_WIDTH, GLA_WIDTH), F32)],
        compiler_params=pltpu.CompilerParams(
            dimension_semantics=("parallel", "arbitrary"), vmem_limit_bytes=VMEM_LIMIT),
        name="gla",
    )(q, k, v, r, la, g)


def _dil_kernel(slope_ref, q_ref, k_ref, v_ref, g_ref, o_ref,
                qf, kf, vf, oc, lc, *, seq):
    B = DIL_BLOCK
    pair = pl.program_id(1)
    qf[...] = q_ref[...].astype(F32)
    kf[0:DIL_PAD, :] = jnp.zeros((DIL_PAD, LANES), F32)
    vf[0:DIL_PAD, :] = jnp.zeros((DIL_PAD, LANES), F32)
    kf[DIL_PAD:DIL_PAD + seq, :] = k_ref[...].astype(F32)
    vf[DIL_PAD:DIL_PAD + seq, :] = v_ref[...].astype(F32)

    lane = lax.broadcasted_iota(jnp.int32, (1, LANES), 1)
    first = lane < DIL_DH
    ii = lax.broadcasted_iota(jnp.int32, (B, B), 0)
    jj = lax.broadcasted_iota(jnp.int32, (B, B), 1)
    upper = jj > ii
    eye = jj == ii
    dist = jnp.bitwise_and(ii - jj, B - 1).astype(F32)
    neg = jnp.float32(-jnp.inf)

    for cfg, (window, r) in enumerate(DIL_CONFIGS):
        nb = seq // r // B
        slopes = [slope_ref[2 * pair + hh] * float(r) for hh in range(2)]
        biases = [dist * (-slopes[hh]) for hh in range(2)]

        def block(it, cfg=cfg, r=r, nb=nb, slopes=slopes, biases=biases):
            c = it // nb
            n = it % nb
            start = c + n * (B * r)
            has_prev = n > 0
            rows = pl.ds(start, B, stride=r) if r > 1 else pl.ds(start, B)
            kv0 = DIL_PAD + start - B * r
            rows_kv = pl.ds(kv0, 2 * B, stride=r) if r > 1 else pl.ds(kv0, 2 * B)
            q = qf[rows, :]
            kcat = kf[rows_kv, :].astype(BF16)
            vcat_f = vf[rows_kv, :]
            vcat = vcat_f.astype(BF16)
            v_far = vcat_f[0:B, :]
            o_pair = None
            l_pair = None
            for hh in range(2):
                hmask = first if hh == 0 else jnp.logical_not(first)
                qh = jnp.where(hmask, q, 0.0).astype(BF16)
                s2 = _dot_nt(qh, kcat)
                s_prev = s2[:, 0:B]
                s_cur = s2[:, B:2 * B]
                s = jnp.where(upper, jnp.where(has_prev, s_prev, neg), s_cur) + biases[hh]
                far = jnp.sum(jnp.where(eye, s_prev, 0.0), axis=-1, keepdims=True) - slopes[hh] * float(B)
                far = jnp.where(has_prev, far, neg)
                m = jnp.maximum(jnp.max(s, axis=-1, keepdims=True), far)
                p = jnp.exp(s - m)
                p_far = jnp.exp(far - m)
                z = jnp.sum(p, axis=-1, keepdims=True) + p_far
                pcat = jnp.concatenate([jnp.where(upper, p, 0.0), jnp.where(upper, 0.0, p)], axis=1)
                o = (_dot(pcat.astype(BF16), vcat) + p_far * v_far) * (1.0 / z)
                lse = jnp.broadcast_to(m + jnp.log(z), (B, LANES))
                if hh == 0:
                    o_pair, l_pair = o, lse
                else:
                    o_pair = jnp.where(first, o_pair, o)
                    l_pair = jnp.where(first, l_pair, lse)
            oc[cfg, rows, :] = o_pair
            lc[cfg, rows, :] = l_pair

        def body(t, carry, block=block):
            for u in range(DIL_UNROLL):
                block(t * DIL_UNROLL + u)
            return carry

        lax.fori_loop(0, seq // B // DIL_UNROLL, body, 0)

    g = g_ref[...]
    CH = 512

    def mix(i, carry):
        rows = pl.ds(pl.multiple_of(i * CH, CH), CH)
        l0, l1, l2 = lc[0, rows, :], lc[1, rows, :], lc[2, rows, :]
        m = jnp.maximum(jnp.maximum(l0, l1), l2)
        e0, e1, e2 = jnp.exp(l0 - m), jnp.exp(l1 - m), jnp.exp(l2 - m)
        den = e0 + e1 + e2
        o = (e0 / den) * oc[0, rows, :] + (e1 / den) * oc[1, rows, :] + (e2 / den) * oc[2, rows, :]
        sq = o * o
        ms_a = jnp.sum(jnp.where(first, sq, 0.0), axis=-1, keepdims=True) * (1.0 / DIL_DH)
        ms_b = jnp.sum(jnp.where(first, 0.0, sq), axis=-1, keepdims=True) * (1.0 / DIL_DH)
        ms = jnp.where(first, ms_a, ms_b)
        o_ref[rows, :] = (o * lax.rsqrt(ms + EPS) * g).astype(BF16)
        return carry

    lax.fori_loop(0, seq // CH, mix, 0)


def _dilated(slopes, dq, dk, dv, g2, batch, seq):
    blk = pl.BlockSpec((seq, LANES), lambda b, p, s: (b, p))
    return pl.pallas_call(
        functools.partial(_dil_kernel, seq=seq),
        grid_spec=pltpu.PrefetchScalarGridSpec(
            num_scalar_prefetch=1,
            grid=(batch, DIL_WIDTH // LANES),
            in_specs=[blk, blk, blk, pl.BlockSpec((1, LANES), lambda b, p, s: (0, 0))],
            out_specs=blk,
            scratch_shapes=[pltpu.VMEM((seq, LANES), F32)] + [pltpu.VMEM((seq + DIL_PAD, LANES), F32)] * 2
                           + [pltpu.VMEM((3, seq, LANES), F32)] * 2,
        ),
        out_shape=jax.ShapeDtypeStruct((batch * seq, DIL_WIDTH), BF16),
        compiler_params=pltpu.CompilerParams(
            dimension_semantics=("parallel", "parallel"), vmem_limit_bytes=VMEM_LIMIT),
        name="dilated",
    )(slopes, dq, dk, dv, g2)


def _out_proj_kernel(og_ref, od_ref, x_ref, wg_ref, wd_ref, g_ref, b_ref,
                     rwh_ref, rwl_ref, rb_ref, h_ref, eid_ref, gate_ref):
    mix = _dot(og_ref[...], wg_ref[...]) + _dot(od_ref[...], wd_ref[...])
    h = _layer_norm(DEEPNORM_ALPHA * x_ref[...] + mix, g_ref[...], b_ref[...])
    h_ref[...] = h
    h_hi, h_lo = _split_bf16(h)
    logits = (_dot(h_hi, rwh_ref[...]) + _dot(h_lo, rwh_ref[...]) + _dot(h_hi, rwl_ref[...])
              + rb_ref[...])
    lane = lax.broadcasted_iota(jnp.int32, logits.shape, 1)
    lane_f = lane.astype(F32)
    neg = jnp.float32(-jnp.inf)
    big = jnp.float32(1e9)
    is_coarse = jnp.logical_and(lane >= N_EXPERTS, lane < N_EXPERTS + N_GROUPS)
    coarse = jnp.where(is_coarse, logits, neg)
    cmax = jnp.max(coarse, axis=-1, keepdims=True)
    g_idx = jnp.min(jnp.where(coarse == cmax, lane_f, big), axis=-1, keepdims=True) - N_EXPERTS
    p_group = 1.0 / jnp.sum(jnp.exp(coarse - cmax), axis=-1, keepdims=True)
    lo = g_idx * EXPERTS_PER_GROUP
    in_group = jnp.logical_and(lane_f >= lo, lane_f < lo + EXPERTS_PER_GROUP)
    fine = jnp.where(in_group, logits, neg)
    v1 = jnp.max(fine, axis=-1, keepdims=True)
    i1 = jnp.min(jnp.where(fine == v1, lane_f, big), axis=-1, keepdims=True)
    fine2 = jnp.where(lane_f == i1, neg, fine)
    v2 = jnp.max(fine2, axis=-1, keepdims=True)
    i2 = jnp.min(jnp.where(fine2 == v2, lane_f, big), axis=-1, keepdims=True)
    e2 = jnp.exp(v2 - v1)
    den = 1.0 + e2
    gate1 = p_group * (1.0 / den)
    gate2 = p_group * (e2 / den)
    lane2 = lax.broadcasted_iota(jnp.int32, (logits.shape[0], 2), 1)
    eid_ref[...] = jnp.where(lane2 == 0, i1, i2).astype(jnp.int32)
    gate_ref[...] = jnp.where(lane2 == 0, gate1, gate2)


def _out_proj(og, od, x2, wg, wd, g, b, rwh, rwl, rb, tm=512):
    T = x2.shape[0]
    row = lambda w: pl.BlockSpec((tm, w), lambda i: (i, 0))
    full = lambda a: pl.BlockSpec(a.shape, lambda i: (0,) * a.ndim)
    return pl.pallas_call(
        _out_proj_kernel,
        grid=(T // tm,),
        in_specs=[row(512), row(512), row(D_MODEL), full(wg), full(wd), full(g), full(b),
                  full(rwh), full(rwl), full(rb)],
        out_specs=[row(D_MODEL), row(2), row(2)],
        out_shape=[jax.ShapeDtypeStruct((T, D_MODEL), F32),
                   jax.ShapeDtypeStruct((T, 2), jnp.int32),
                   jax.ShapeDtypeStruct((T, 2), F32)],
        compiler_params=pltpu.CompilerParams(
            dimension_semantics=("parallel",), vmem_limit_bytes=VMEM_LIMIT),
        name="out_proj_router",
    )(og, od, x2, wg, wd, g, b, rwh, rwl, rb)


def _positions_kernel(eid_ref, dest_ref, be_ref, nv_ref, carry_ref, cnt_col_ref, sp_ref, *, tb):
    phase = pl.program_id(0)
    i = pl.program_id(1)
    lane = lax.broadcasted_iota(jnp.int32, (tb, LANES), 1)
    eid = eid_ref[...]
    oh1 = lane == eid[:, 0:1]
    oh2 = lane == eid[:, 1:2]
    oh = jnp.logical_or(oh1, oh2).astype(BF16)

    @pl.when(jnp.logical_and(phase == 0, i == 0))
    def _():
        carry_ref[...] = jnp.zeros_like(carry_ref)
        cnt_col_ref[...] = jnp.zeros_like(cnt_col_ref)

    @pl.when(phase == 0)
    def _():
        ones_r = jnp.ones((8, tb), BF16)
        ones_c = jnp.ones((tb, LANES), BF16)
        carry_ref[...] += _dot(ones_r, oh)
        cnt_col_ref[...] += _dot_tn(oh, ones_c)

    @pl.when(jnp.logical_and(phase == 1, i == 0))
    def _():
        shift = int(math.log2(ROW_BLOCK))
        nb_row = ((carry_ref[...].astype(jnp.int32) + (ROW_BLOCK - 1)) >> shift)
        nb_col = ((cnt_col_ref[...].astype(jnp.int32) + (ROW_BLOCK - 1)) >> shift)
        r = lax.broadcasted_iota(jnp.int32, (LANES, LANES), 0)
        c = lax.broadcasted_iota(jnp.int32, (LANES, LANES), 1)
        excl = (r < c).astype(BF16)
        incl = (c <= r).astype(BF16)
        start_blk = _dot(nb_row.astype(F32).astype(BF16), excl)
        sp_ref[...] = start_blk * float(ROW_BLOCK)
        end_col = _dot(incl, nb_col.astype(F32).astype(BF16))
        end2 = jnp.concatenate([end_col, end_col], axis=1)
        nidx = lax.broadcasted_iota(jnp.int32, (LANES, 2 * LANES), 1).astype(F32)
        eidx = lax.broadcasted_iota(jnp.int32, (LANES, 2 * LANES), 0)
        hit = jnp.logical_and(end2 <= nidx, eidx < N_EXPERTS).astype(F32)
        be = jnp.minimum(jnp.sum(hit, axis=0, keepdims=True), float(N_EXPERTS - 1))
        be_ref[...] = be.astype(jnp.int32)
        lane1 = lax.broadcasted_iota(jnp.int32, (1, LANES), 1)
        total = jnp.sum(jnp.where(lane1 < N_EXPERTS, nb_row[0:1, :].astype(F32), 0.0), axis=-1, keepdims=True)
        nv_ref[...] = jnp.broadcast_to(total, (1, LANES)).astype(jnp.int32)
        carry_ref[...] = jnp.zeros_like(carry_ref)

    @pl.when(phase == 1)
    def _():
        r = lax.broadcasted_iota(jnp.int32, (tb, tb), 0)
        c = lax.broadcasted_iota(jnp.int32, (tb, tb), 1)
        strict = (c < r).astype(BF16)
        before = _dot(strict, oh) + carry_ref[0:1, :] + sp_ref[0:1, :]
        d1 = jnp.sum(jnp.where(oh1, before, 0.0), axis=-1, keepdims=True)
        d2 = jnp.sum(jnp.where(oh2, before, 0.0), axis=-1, keepdims=True)
        lane2 = lax.broadcasted_iota(jnp.int32, (tb, 2), 1)
        dest_ref[...] = jnp.where(lane2 == 0, d1, d2).astype(jnp.int32)
        carry_ref[...] += _dot(jnp.ones((8, tb), BF16), oh)


def _positions(eid, n_blocks_pad, tb=512):
    T = eid.shape[0]
    nb = T // tb
    return pl.pallas_call(
        functools.partial(_positions_kernel, tb=tb),
        grid=(2, nb),
        in_specs=[pl.BlockSpec((tb, 2), lambda p, i: (i, 0))],
        out_specs=[pl.BlockSpec((tb, 2), lambda p, i: (i * p, 0)),
                   pl.BlockSpec((1, n_blocks_pad), lambda p, i: (0, 0)),
                   pl.BlockSpec((1, LANES), lambda p, i: (0, 0))],
        out_shape=[jax.ShapeDtypeStruct((T, 2), jnp.int32),
                   jax.ShapeDtypeStruct((1, n_blocks_pad), jnp.int32),
                   jax.ShapeDtypeStruct((1, LANES), jnp.int32)],
        scratch_shapes=[pltpu.VMEM((8, LANES), F32), pltpu.VMEM((LANES, LANES), F32),
                        pltpu.VMEM((8, LANES), F32)],
        compiler_params=pltpu.CompilerParams(dimension_semantics=("arbitrary", "arbitrary")),
        name="positions",
    )(eid)


def _dispatch_kernel(dest_ref, h_ref, xs_in_ref, xs_ref, sem, *, tm):
    del xs_in_ref
    base = pl.program_id(0) * tm

    def issue(i, carry):
        for kk in range(2):
            d = dest_ref[2 * (base + i) + kk]
            pltpu.make_async_copy(h_ref.at[pl.ds(i, 1)], xs_ref.at[pl.ds(d, 1)], sem).start()
        return carry

    lax.fori_loop(0, tm, issue, 0, unroll=8)

    for kk in range(2):
        pltpu.make_async_copy(h_ref, xs_ref.at[pl.ds(0, tm)], sem).wait()


def _dispatch(dest_flat, h, xs_init, tm=256):
    T = h.shape[0]
    return pl.pallas_call(
        functools.partial(_dispatch_kernel, tm=tm),
        grid_spec=pltpu.PrefetchScalarGridSpec(
            num_scalar_prefetch=1,
            grid=(T // tm,),
            in_specs=[pl.BlockSpec((tm, D_MODEL), lambda i, d: (i, 0)),
                      pl.BlockSpec(memory_space=pl.ANY)],
            out_specs=pl.BlockSpec(memory_space=pl.ANY),
            scratch_shapes=[pltpu.SemaphoreType.DMA],
        ),
        out_shape=jax.ShapeDtypeStruct(xs_init.shape, xs_init.dtype),
        input_output_aliases={2: 0},
        compiler_params=pltpu.CompilerParams(dimension_semantics=("arbitrary",)),
        name="dispatch",
    )(dest_flat, h, xs_init)


def _ffn_kernel(be_ref, nv_ref, xs_ref, wg_ref, wu_ref, wd_ref, y_ref, wgb, wub, wdb):
    n = pl.program_id(0)
    prev = be_ref[jnp.maximum(n - 1, 0)]
    fresh = jnp.logical_or(n == 0, be_ref[n] != prev)

    @pl.when(jnp.logical_and(fresh, n < nv_ref[0]))
    def _():
        wgb[...] = wg_ref[0].astype(BF16)
        wub[...] = wu_ref[0].astype(BF16)
        wdb[...] = wd_ref[0].astype(BF16)

    @pl.when(n < nv_ref[0])
    def _():
        xb = xs_ref[...].astype(BF16)
        a = _dot(xb, wgb[...])
        u = _dot(xb, wub[...])
        hid = (a * jax.nn.sigmoid(a) * u).astype(BF16)
        y_ref[...] = _dot(hid, wdb[...])

    @pl.when(n >= nv_ref[0])
    def _():
        y_ref[...] = jnp.zeros_like(y_ref)


def _ffn(be, nv, xs, w_gate, w_up, w_down):
    n_rows = xs.shape[0]
    n_blocks = n_rows // ROW_BLOCK

    def blk(n, be, nv):
        return (jnp.minimum(n, nv[0] - 1), 0)

    def wsel(n, be, nv):
        return (be[jnp.minimum(n, nv[0] - 1)], 0, 0)

    return pl.pallas_call(
        _ffn_kernel,
        grid_spec=pltpu.PrefetchScalarGridSpec(
            num_scalar_prefetch=2,
            grid=(n_blocks,),
            in_specs=[pl.BlockSpec((ROW_BLOCK, D_MODEL), blk),
                      pl.BlockSpec((1, D_MODEL, D_FF), wsel),
                      pl.BlockSpec((1, D_MODEL, D_FF), wsel),
                      pl.BlockSpec((1, D_FF, D_MODEL), wsel)],
            out_specs=pl.BlockSpec((ROW_BLOCK, D_MODEL), lambda n, be, nv: (n, 0)),
            scratch_shapes=[pltpu.VMEM((D_MODEL, D_FF), BF16), pltpu.VMEM((D_MODEL, D_FF), BF16),
                            pltpu.VMEM((D_FF, D_MODEL), BF16)],
        ),
        out_shape=jax.ShapeDtypeStruct((n_rows, D_MODEL), F32),
        compiler_params=pltpu.CompilerParams(
            dimension_semantics=("arbitrary",), vmem_limit_bytes=VMEM_LIMIT),
        name="expert_ffn",
    )(be, nv, xs, w_gate, w_up, w_down)


def _combine_kernel(dest_ref, h_ref, gate_ref, g_ref, b_ref, y_ref, o_ref, ybuf, sem, *, tm):
    base = pl.program_id(0) * tm

    def issue(i, carry):
        for kk in range(2):
            d = dest_ref[2 * (base + i) + kk]
            pltpu.make_async_copy(y_ref.at[pl.ds(d, 1)], ybuf.at[kk, pl.ds(i, 1)], sem).start()
        return carry

    lax.fori_loop(0, tm, issue, 0, unroll=8)

    for kk in range(2):
        pltpu.make_async_copy(y_ref.at[pl.ds(0, tm)], ybuf.at[kk], sem).wait()

    gate = gate_ref[...]
    ffn = ybuf[0] * gate[:, 0:1] + ybuf[1] * gate[:, 1:2]
    o_ref[...] = _layer_norm(DEEPNORM_ALPHA * h_ref[...] + ffn, g_ref[...], b_ref[...])


def _combine(dest_flat, h, gate, g, b, y, tm=256):
    T = h.shape[0]
    return pl.pallas_call(
        functools.partial(_combine_kernel, tm=tm),
        grid_spec=pltpu.PrefetchScalarGridSpec(
            num_scalar_prefetch=1,
            grid=(T // tm,),
            in_specs=[pl.BlockSpec((tm, D_MODEL), lambda i, d: (i, 0)),
                      pl.BlockSpec((tm, 2), lambda i, d: (i, 0)),
                      pl.BlockSpec((1, D_MODEL), lambda i, d: (0, 0)),
                      pl.BlockSpec((1, D_MODEL), lambda i, d: (0, 0)),
                      pl.BlockSpec(memory_space=pl.ANY)],
            out_specs=pl.BlockSpec((tm, D_MODEL), lambda i, d: (i, 0)),
            scratch_shapes=[pltpu.VMEM((2, tm, D_MODEL), F32), pltpu.SemaphoreType.DMA],
        ),
        out_shape=jax.ShapeDtypeStruct((T, D_MODEL), F32),
        compiler_params=pltpu.CompilerParams(dimension_semantics=("arbitrary",)),
        name="combine",
    )(dest_flat, h, gate, g, b, y)


def kernel(x, w_in, gla_gate_w2, gla_gate_b, gla_norm_g, dil_norm_g, w_out, ln1_g, ln1_b,
           router_coarse_w, router_coarse_b, router_fine_w, router_fine_b,
           expert_w_gate, expert_w_up, expert_w_down, ln2_g, ln2_b):
    B, S, D = x.shape
    T = B * S
    depth = w_in.shape[0]
    slopes = jnp.exp2(-8.0 * jnp.arange(1, DIL_HEADS + 1, dtype=F32) / DIL_HEADS)
    n_rows = 2 * T + N_EXPERTS * ROW_BLOCK
    n_blocks = n_rows // ROW_BLOCK
    n_blocks_pad = -(-n_blocks // (2 * LANES)) * (2 * LANES)
    assert n_blocks_pad == 2 * LANES
    a0 = 1536
    h = x.reshape(T, D)
    for l in range(depth):
        w = w_in[l]
        wm = jnp.concatenate([w[:, :a0], w[:, a0 + GLA_GATE_RANK:]], axis=1).astype(BF16)
        wa = jnp.pad(w[:, a0:a0 + GLA_GATE_RANK], ((0, 0), (0, LANES - GLA_GATE_RANK))).astype(BF16)
        w2 = jnp.pad(gla_gate_w2[l], ((0, LANES - GLA_GATE_RANK), (0, 0)))
        w2h, w2l = _split_bf16(w2)
        q, k, v, r, la, dq, dk, dv = _in_proj(h, wm, wa, w2h, w2l, gla_gate_b[l][None, :])
        o_gla = _gla(q, k, v, r, la, gla_norm_g[l][None, :], B, S)
        g2 = jnp.tile(dil_norm_g[l], 2)[None, :]
        o_dil = _dilated(slopes, dq, dk, dv, g2, B, S)
        wo = w_out[l].astype(BF16)
        rw = jnp.concatenate([router_fine_w[l].reshape(D, N_EXPERTS), router_coarse_w[l]], axis=1)
        rw = jnp.pad(rw, ((0, 0), (0, LANES - N_EXPERTS - N_GROUPS)))
        rwh, rwl = _split_bf16(rw)
        rb = jnp.concatenate([router_fine_b[l].reshape(N_EXPERTS), router_coarse_b[l]])
        rb = jnp.pad(rb, (0, LANES - N_EXPERTS - N_GROUPS))[None, :]
        h1, eid, gate = _out_proj(o_gla, o_dil, h, wo[:GLA_WIDTH], wo[GLA_WIDTH:],
                                  ln1_g[l][None, :], ln1_b[l][None, :], rwh, rwl, rb)
        dest, be, nv = _positions(eid, n_blocks_pad)
        dest_flat = dest.reshape(2 * T)
        xs = _dispatch(dest_flat, h1, jnp.zeros((n_rows, D), F32))
        y = _ffn(be.reshape(n_blocks_pad), nv.reshape(LANES)[:1], xs,
                 expert_w_gate[l], expert_w_up[l], expert_w_down[l])
        h = _combine(dest_flat, h1, gate, ln2_g[l][None, :], ln2_b[l][None, :], y)
    return h.reshape(B, S, D)
```

```python
import functools
import math

import jax
import jax.numpy as jnp
import numpy as np
from jax import lax
from jax.experimental import pallas as pl
from jax.experimental.pallas import tpu as pltpu
from jax.experimental.pallas import tpu_sc as plsc

D_MODEL = 1024
GLA_HEADS = 4
GLA_DK = 64
GLA_DV = 128
GLA_KEY_WIDTH = GLA_HEADS * GLA_DK
GLA_WIDTH = GLA_HEADS * GLA_DV
GLA_GATE_RANK = 16
GLA_GATE_TEMP = 16.0
DIL_HEADS = 8
DIL_DH = 64
DIL_WIDTH = DIL_HEADS * DIL_DH
DIL_CONFIGS = ((128, 1), (512, 4), (2048, 16))
DIL_BLOCK = 128
DIL_PAD = DIL_BLOCK * max(r for _, r in DIL_CONFIGS)
DIL_UNROLL = 4
N_GROUPS = 4
EXPERTS_PER_GROUP = 8
N_EXPERTS = N_GROUPS * EXPERTS_PER_GROUP
D_FF = 512
DEEPNORM_ALPHA = 2.0 ** 0.25
EPS = 1e-5

LANES = 128
GLA_CHUNK = 64
GLA_UNROLL = 4
SC_INDEX_WINDOW = 128
SC_GATHER_ROWS = 32
ROW_BLOCK = 256
VMEM_LIMIT = 56 * 1024 * 1024

F32 = jnp.float32
BF16 = jnp.bfloat16


def _dot(a, b):
    return jnp.dot(a, b, preferred_element_type=F32)


def _dot_nt(a, b):
    return lax.dot_general(a, b, (((1,), (1,)), ((), ())), preferred_element_type=F32)


def _dot_tn(a, b):
    return lax.dot_general(a, b, (((0,), (0,)), ((), ())), preferred_element_type=F32)


def _split_bf16(v):
    hi = v.astype(BF16)
    lo = (v - hi.astype(F32)).astype(BF16)
    return hi, lo


def _layer_norm(v, g, b):
    mu = jnp.mean(v, axis=-1, keepdims=True)
    c = v - mu
    var = jnp.mean(c * c, axis=-1, keepdims=True)
    return c * lax.rsqrt(var + EPS) * g + b


def _in_proj_kernel(x_ref, wm_ref, wa_ref, w2h_ref, w2l_ref, gb_ref,
                    q_ref, k_ref, v_ref, r_ref, la_ref, dq_ref, dk_ref, dv_ref):
    xb = x_ref[...].astype(BF16)

    def piece(c0, c1):
        return _dot(xb, wm_ref[:, c0:c1])

    q_ref[...] = (piece(0, 256) * (GLA_DK ** -0.5)).astype(BF16)
    k_ref[...] = piece(256, 512).astype(BF16)
    v_ref[...] = piece(512, 1024).astype(BF16)
    r_ref[...] = piece(1024, 1536).astype(BF16)
    dq_ref[...] = (piece(1536, 2048) * (DIL_DH ** -0.5)).astype(BF16)
    dk_ref[...] = piece(2048, 2560).astype(BF16)
    dv_ref[...] = piece(2560, 3072).astype(BF16)
    ga = _dot(xb, wa_ref[...])
    ga_hi, ga_lo = _split_bf16(ga)
    z = _dot(ga_hi, w2h_ref[...]) + _dot(ga_lo, w2h_ref[...]) + _dot(ga_hi, w2l_ref[...]) + gb_ref[...]
    log_sig = jnp.minimum(z, 0.0) - jnp.log1p(jnp.exp(-jnp.abs(z)))
    la_ref[...] = log_sig * (1.0 / GLA_GATE_TEMP)


def _in_proj(x2, wm, wa, w2h, w2l, gb, tm=512):
    T = x2.shape[0]
    row = lambda w: pl.BlockSpec((tm, w), lambda i: (i, 0))
    full = lambda a: pl.BlockSpec(a.shape, lambda i: (0,) * a.ndim)
    outs = [(256, BF16), (256, BF16), (512, BF16), (512, BF16), (256, F32),
            (512, BF16), (512, BF16), (512, BF16)]
    return pl.pallas_call(
        _in_proj_kernel,
        grid=(T // tm,),
        in_specs=[row(D_MODEL), full(wm), full(wa), full(w2h), full(w2l), full(gb)],
        out_specs=[row(w) for w, _ in outs],
        out_shape=[jax.ShapeDtypeStruct((T, w), dt) for w, dt in outs],
        compiler_params=pltpu.CompilerParams(
            dimension_semantics=("parallel",), vmem_limit_bytes=VMEM_LIMIT),
        name="in_proj",
    )(x2, wm, wa, w2h, w2l, gb)


def _gla_kernel(q_ref, k_ref, v_ref, r_ref, la_ref, g_ref, o_ref, s_ref, *, seq_block):
    C = GLA_CHUNK

    @pl.when(pl.program_id(1) == 0)
    def _():
        s_ref[...] = jnp.zeros_like(s_ref)

    ri = lax.broadcasted_iota(jnp.int32, (C, C), 0)
    ci = lax.broadcasted_iota(jnp.int32, (C, C), 1)
    causal = ci <= ri
    tri = causal.astype(BF16)
    ones_cl = jnp.ones((C, LANES), BF16)
    lane_k = lax.broadcasted_iota(jnp.int32, (1, GLA_KEY_WIDTH), 1) // GLA_DK
    head_masks = [(lane_k == h).astype(F32) for h in range(GLA_HEADS)]
    srow = lax.broadcasted_iota(jnp.int32, (GLA_KEY_WIDTH, GLA_WIDTH), 0) // GLA_DK
    scol = lax.broadcasted_iota(jnp.int32, (GLA_KEY_WIDTH, GLA_WIDTH), 1) // GLA_DV
    state_mask = (srow == scol).astype(F32)
    g = g_ref[...]

    def chunk(c, carry):
        rows = pl.ds(pl.multiple_of(c * C, C), C)
        la = la_ref[rows, :]
        la_hi, la_lo = _split_bf16(la)
        b = _dot(tri, la_hi) + _dot(tri, la_lo)
        b_last = b[C - 1:C, :]
        q = q_ref[rows, :].astype(F32)
        k = k_ref[rows, :].astype(F32)
        v = v_ref[rows, :]
        qd = q * jnp.exp(b)
        kd = (k * jnp.exp(-b)).astype(BF16)
        ke = (k * jnp.exp(b_last - b)).astype(BF16)
        state = s_ref[...]
        o_inter = _dot(qd.astype(BF16), state.astype(BF16))
        outs = []
        for h in range(GLA_HEADS):
            a = _dot_nt((qd * head_masks[h]).astype(BF16), kd)
            a = jnp.where(causal, a, 0.0).astype(BF16)
            cols = slice(h * GLA_DV, (h + 1) * GLA_DV)
            o = _dot(a, v[:, cols]) + o_inter[:, cols]
            o = o * lax.rsqrt(jnp.mean(o * o, axis=-1, keepdims=True) + EPS) * g
            outs.append(o)
        o_all = jnp.concatenate(outs, axis=-1)
        rr = r_ref[rows, :].astype(F32)
        o_ref[rows, :] = (o_all * (rr * jax.nn.sigmoid(rr))).astype(BF16)
        tot = _dot_tn(la_hi, ones_cl) + _dot_tn(la_lo, ones_cl)
        dec = jnp.exp(tot)
        upd = _dot_tn(ke, v) * state_mask
        for h in range(GLA_HEADS):
            cols = slice(h * GLA_DV, (h + 1) * GLA_DV)
            s_ref[:, cols] = state[:, cols] * dec + upd[:, cols]
        return carry

    lax.fori_loop(0, seq_block // C, chunk, 0, unroll=GLA_UNROLL)


def _gla(q, k, v, r, la, g, batch, seq, seq_block=1024):
    nsb = seq // seq_block
    row = lambda w: pl.BlockSpec((seq_block, w), lambda b, s: (b * nsb + s, 0))
    return pl.pallas_call(
        functools.partial(_gla_kernel, seq_block=seq_block),
        grid=(batch, nsb),
        in_specs=[row(256), row(256), row(512), row(512), row(256),
                  pl.BlockSpec((1, GLA_DV), lambda b, s: (0, 0))],
        out_specs=row(512),
        out_shape=jax.ShapeDtypeStruct((batch * seq, GLA_WIDTH), BF16),
        scratch_shapes=[pltpu.VMEM((GLA_KEY_WIDTH, GLA_WIDTH), F32)],
        compiler_params=pltpu.CompilerParams(
            dimension_semantics=("parallel", "arbitrary"), vmem_limit_bytes=VMEM_LIMIT),
        name="gla",
    )(q, k, v, r, la, g)


def _dil_kernel(slope_ref, q_ref, k_ref, v_ref, g_ref, o_ref,
                qf, kf, vf, oc, lc, *, seq):
    B = DIL_BLOCK
    pair = pl.program_id(1)
    qf[...] = q_ref[...].astype(F32)
    kf[0:DIL_PAD, :] = jnp.zeros((DIL_PAD, LANES), F32)
    vf[0:DIL_PAD, :] = jnp.zeros((DIL_PAD, LANES), F32)
    kf[DIL_PAD:DIL_PAD + seq, :] = k_ref[...].astype(F32)
    vf[DIL_PAD:DIL_PAD + seq, :] = v_ref[...].astype(F32)

    lane = lax.broadcasted_iota(jnp.int32, (1, LANES), 1)
    first = lane < DIL_DH
    ii = lax.broadcasted_iota(jnp.int32, (B, B), 0)
    jj = lax.broadcasted_iota(jnp.int32, (B, B), 1)
    upper = jj > ii
    eye = jj == ii
    dist = jnp.bitwise_and(ii - jj, B - 1).astype(F32)
    neg = jnp.float32(-jnp.inf)

    for cfg, (window, r) in enumerate(DIL_CONFIGS):
        nb = seq // r // B
        slopes = [slope_ref[2 * pair + hh] * float(r) for hh in range(2)]
        biases = [dist * (-slopes[hh]) for hh in range(2)]

        def block(it, cfg=cfg, r=r, nb=nb, slopes=slopes, biases=biases):
            c = it // nb
            n = it % nb
            start = c + n * (B * r)
            has_prev = n > 0
            rows = pl.ds(start, B, stride=r) if r > 1 else pl.ds(start, B)
            kv0 = DIL_PAD + start - B * r
            rows_kv = pl.ds(kv0, 2 * B, stride=r) if r > 1 else pl.ds(kv0, 2 * B)
            q = qf[rows, :]
            kcat = kf[rows_kv, :].astype(BF16)
            vcat_f = vf[rows_kv, :]
            vcat = vcat_f.astype(BF16)
            v_far = vcat_f[0:B, :]
            o_pair = None
            l_pair = None
            for hh in range(2):
                hmask = first if hh == 0 else jnp.logical_not(first)
                qh = jnp.where(hmask, q, 0.0).astype(BF16)
                s2 = _dot_nt(qh, kcat)
                s_prev = s2[:, 0:B]
                s_cur = s2[:, B:2 * B]
                s = jnp.where(upper, jnp.where(has_prev, s_prev, neg), s_cur) + biases[hh]
                far = jnp.sum(jnp.where(eye, s_prev, 0.0), axis=-1, keepdims=True) - slopes[hh] * float(B)
                far = jnp.where(has_prev, far, neg)
                m = jnp.maximum(jnp.max(s, axis=-1, keepdims=True), far)
                p = jnp.exp(s - m)
                p_far = jnp.exp(far - m)
                z = jnp.sum(p, axis=-1, keepdims=True) + p_far
                pcat = jnp.concatenate([jnp.where(upper, p, 0.0), jnp.where(upper, 0.0, p)], axis=1)
                o = (_dot(pcat.astype(BF16), vcat) + p_far * v_far) * (1.0 / z)
                lse = jnp.broadcast_to(m + jnp.log(z), (B, LANES))
                if hh == 0:
                    o_pair, l_pair = o, lse
                else:
                    o_pair = jnp.where(first, o_pair, o)
                    l_pair = jnp.where(first, l_pair, lse)
            oc[cfg, rows, :] = o_pair
            lc[cfg, rows, :] = l_pair

        def body(t, carry, block=block):
            for u in range(DIL_UNROLL):
                block(t * DIL_UNROLL + u)
            return carry

        lax.fori_loop(0, seq // B // DIL_UNROLL, body, 0)

    g = g_ref[...]
    CH = 512

    def mix(i, carry):
        rows = pl.ds(pl.multiple_of(i * CH, CH), CH)
        l0, l1, l2 = lc[0, rows, :], lc[1, rows, :], lc[2, rows, :]
        m = jnp.maximum(jnp.maximum(l0, l1), l2)
        e0, e1, e2 = jnp.exp(l0 - m), jnp.exp(l1 - m), jnp.exp(l2 - m)
        den = e0 + e1 + e2
        o = (e0 / den) * oc[0, rows, :] + (e1 / den) * oc[1, rows, :] + (e2 / den) * oc[2, rows, :]
        sq = o * o
        ms_a = jnp.sum(jnp.where(first, sq, 0.0), axis=-1, keepdims=True) * (1.0 / DIL_DH)
        ms_b = jnp.sum(jnp.where(first, 0.0, sq), axis=-1, keepdims=True) * (1.0 / DIL_DH)
        ms = jnp.where(first, ms_a, ms_b)
        o_ref[rows, :] = (o * lax.rsqrt(ms + EPS) * g).astype(BF16)
        return carry

    lax.fori_loop(0, seq // CH, mix, 0)


def _dilated(slopes, dq, dk, dv, g2, batch, seq):
    blk = pl.BlockSpec((seq, LANES), lambda b, p, s: (b, p))
    return pl.pallas_call(
        functools.partial(_dil_kernel, seq=seq),
        grid_spec=pltpu.PrefetchScalarGridSpec(
            num_scalar_prefetch=1,
            grid=(batch, DIL_WIDTH // LANES),
            in_specs=[blk, blk, blk, pl.BlockSpec((1, LANES), lambda b, p, s: (0, 0))],
            out_specs=blk,
            scratch_shapes=[pltpu.VMEM((seq, LANES), F32)] + [pltpu.VMEM((seq + DIL_PAD, LANES), F32)] * 2
                           + [pltpu.VMEM((3, seq, LANES), F32)] * 2,
        ),
        out_shape=jax.ShapeDtypeStruct((batch * seq, DIL_WIDTH), BF16),
        compiler_params=pltpu.CompilerParams(
            dimension_semantics=("parallel", "parallel"), vmem_limit_bytes=VMEM_LIMIT),
        name="dilated",
    )(slopes, dq, dk, dv, g2)


def _out_proj_kernel(og_ref, od_ref, x_ref, wg_ref, wd_ref, g_ref, b_ref,
                     rwh_ref, rwl_ref, rb_ref, h_ref, eid_ref, gate_ref):
    mix = _dot(og_ref[...], wg_ref[...]) + _dot(od_ref[...], wd_ref[...])
    h = _layer_norm(DEEPNORM_ALPHA * x_ref[...] + mix, g_ref[...], b_ref[...])
    h_ref[...] = h
    h_hi, h_lo = _split_bf16(h)
    logits = (_dot(h_hi, rwh_ref[...]) + _dot(h_lo, rwh_ref[...]) + _dot(h_hi, rwl_ref[...])
              + rb_ref[...])
    lane = lax.broadcasted_iota(jnp.int32, logits.shape, 1)
    lane_f = lane.astype(F32)
    neg = jnp.float32(-jnp.inf)
    big = jnp.float32(1e9)
    is_coarse = jnp.logical_and(lane >= N_EXPERTS, lane < N_EXPERTS + N_GROUPS)
    coarse = jnp.where(is_coarse, logits, neg)
    cmax = jnp.max(coarse, axis=-1, keepdims=True)
    g_idx = jnp.min(jnp.where(coarse == cmax, lane_f, big), axis=-1, keepdims=True) - N_EXPERTS
    p_group = 1.0 / jnp.sum(jnp.exp(coarse - cmax), axis=-1, keepdims=True)
    lo = g_idx * EXPERTS_PER_GROUP
    in_group = jnp.logical_and(lane_f >= lo, lane_f < lo + EXPERTS_PER_GROUP)
    fine = jnp.where(in_group, logits, neg)
    v1 = jnp.max(fine, axis=-1, keepdims=True)
    i1 = jnp.min(jnp.where(fine == v1, lane_f, big), axis=-1, keepdims=True)
    fine2 = jnp.where(lane_f == i1, neg, fine)
    v2 = jnp.max(fine2, axis=-1, keepdims=True)
    i2 = jnp.min(jnp.where(fine2 == v2, lane_f, big), axis=-1, keepdims=True)
    e2 = jnp.exp(v2 - v1)
    den = 1.0 + e2
    gate1 = p_group * (1.0 / den)
    gate2 = p_group * (e2 / den)
    lane2 = lax.broadcasted_iota(jnp.int32, (logits.shape[0], 2), 1)
    eid_ref[...] = jnp.where(lane2 == 0, i1, i2).astype(jnp.int32)
    gate_ref[...] = jnp.where(lane2 == 0, gate1, gate2)


def _out_proj(og, od, x2, wg, wd, g, b, rwh, rwl, rb, tm=512):
    T = x2.shape[0]
    row = lambda w: pl.BlockSpec((tm, w), lambda i: (i, 0))
    full = lambda a: pl.BlockSpec(a.shape, lambda i: (0,) * a.ndim)
    return pl.pallas_call(
        _out_proj_kernel,
        grid=(T // tm,),
        in_specs=[row(512), row(512), row(D_MODEL), full(wg), full(wd), full(g), full(b),
                  full(rwh), full(rwl), full(rb)],
        out_specs=[row(D_MODEL), row(2), row(2)],
        out_shape=[jax.ShapeDtypeStruct((T, D_MODEL), F32),
                   jax.ShapeDtypeStruct((T, 2), jnp.int32),
                   jax.ShapeDtypeStruct((T, 2), F32)],
        compiler_params=pltpu.CompilerParams(
            dimension_semantics=("parallel",), vmem_limit_bytes=VMEM_LIMIT),
        name="out_proj_router",
    )(og, od, x2, wg, wd, g, b, rwh, rwl, rb)


def _positions_kernel(eid_ref, dest_ref, be_ref, nv_ref, carry_ref, cnt_col_ref, sp_ref, *, tb):
    phase = pl.program_id(0)
    i = pl.program_id(1)
    lane = lax.broadcasted_iota(jnp.int32, (tb, LANES), 1)
    eid = eid_ref[...]
    oh1 = lane == eid[:, 0:1]
    oh2 = lane == eid[:, 1:2]
    oh = jnp.logical_or(oh1, oh2).astype(BF16)

    @pl.when(jnp.logical_and(phase == 0, i == 0))
    def _():
        carry_ref[...] = jnp.zeros_like(carry_ref)
        cnt_col_ref[...] = jnp.zeros_like(cnt_col_ref)

    @pl.when(phase == 0)
    def _():
        ones_r = jnp.ones((8, tb), BF16)
        ones_c = jnp.ones((tb, LANES), BF16)
        carry_ref[...] += _dot(ones_r, oh)
        cnt_col_ref[...] += _dot_tn(oh, ones_c)

    @pl.when(jnp.logical_and(phase == 1, i == 0))
    def _():
        shift = int(math.log2(ROW_BLOCK))
        nb_row = ((carry_ref[...].astype(jnp.int32) + (ROW_BLOCK - 1)) >> shift)
        nb_col = ((cnt_col_ref[...].astype(jnp.int32) + (ROW_BLOCK - 1)) >> shift)
        r = lax.broadcasted_iota(jnp.int32, (LANES, LANES), 0)
        c = lax.broadcasted_iota(jnp.int32, (LANES, LANES), 1)
        excl = (r < c).astype(BF16)
        incl = (c <= r).astype(BF16)
        start_blk = _dot(nb_row.astype(F32).astype(BF16), excl)
        sp_ref[...] = start_blk * float(ROW_BLOCK)
        end_col = _dot(incl, nb_col.astype(F32).astype(BF16))
        end2 = jnp.concatenate([end_col, end_col], axis=1)
        nidx = lax.broadcasted_iota(jnp.int32, (LANES, 2 * LANES), 1).astype(F32)
        eidx = lax.broadcasted_iota(jnp.int32, (LANES, 2 * LANES), 0)
        hit = jnp.logical_and(end2 <= nidx, eidx < N_EXPERTS).astype(F32)
        be = jnp.minimum(jnp.sum(hit, axis=0, keepdims=True), float(N_EXPERTS - 1))
        be_ref[...] = be.astype(jnp.int32)
        lane1 = lax.broadcasted_iota(jnp.int32, (1, LANES), 1)
        total = jnp.sum(jnp.where(lane1 < N_EXPERTS, nb_row[0:1, :].astype(F32), 0.0), axis=-1, keepdims=True)
        nv_ref[...] = jnp.broadcast_to(total, (1, LANES)).astype(jnp.int32)
        carry_ref[...] = jnp.zeros_like(carry_ref)

    @pl.when(phase == 1)
    def _():
        r = lax.broadcasted_iota(jnp.int32, (tb, tb), 0)
        c = lax.broadcasted_iota(jnp.int32, (tb, tb), 1)
        strict = (c < r).astype(BF16)
        before = _dot(strict, oh) + carry_ref[0:1, :] + sp_ref[0:1, :]
        d1 = jnp.sum(jnp.where(oh1, before, 0.0), axis=-1, keepdims=True)
        d2 = jnp.sum(jnp.where(oh2, before, 0.0), axis=-1, keepdims=True)
        lane2 = lax.broadcasted_iota(jnp.int32, (tb, 2), 1)
        dest_ref[...] = jnp.where(lane2 == 0, d1, d2).astype(jnp.int32)
        carry_ref[...] += _dot(jnp.ones((8, tb), BF16), oh)


def _positions(eid, n_blocks_pad, tb=512):
    T = eid.shape[0]
    nb = T // tb
    return pl.pallas_call(
        functools.partial(_positions_kernel, tb=tb),
        grid=(2, nb),
        in_specs=[pl.BlockSpec((tb, 2), lambda p, i: (i, 0))],
        out_specs=[pl.BlockSpec((tb, 2), lambda p, i: (i * p, 0)),
                   pl.BlockSpec((1, n_blocks_pad), lambda p, i: (0, 0)),
                   pl.BlockSpec((1, LANES), lambda p, i: (0, 0))],
        out_shape=[jax.ShapeDtypeStruct((T, 2), jnp.int32),
                   jax.ShapeDtypeStruct((1, n_blocks_pad), jnp.int32),
                   jax.ShapeDtypeStruct((1, LANES), jnp.int32)],
        scratch_shapes=[pltpu.VMEM((8, LANES), F32), pltpu.VMEM((LANES, LANES), F32),
                        pltpu.VMEM((8, LANES), F32)],
        compiler_params=pltpu.CompilerParams(dimension_semantics=("arbitrary", "arbitrary")),
        name="positions",
    )(eid)


def _dispatch_kernel(dest_ref, h_ref, xs_in_ref, xs_ref, sem, *, tm):
    del xs_in_ref
    base = pl.program_id(0) * tm

    def issue(i, carry):
        for kk in range(2):
            d = dest_ref[2 * (base + i) + kk]
            pltpu.make_async_copy(h_ref.at[pl.ds(i, 1)], xs_ref.at[pl.ds(d, 1)], sem).start()
        return carry

    lax.fori_loop(0, tm, issue, 0, unroll=8)

    for kk in range(2):
        pltpu.make_async_copy(h_ref, xs_ref.at[pl.ds(0, tm)], sem).wait()


def _dispatch(dest_flat, h, xs_init, tm=256):
    T = h.shape[0]
    return pl.pallas_call(
        functools.partial(_dispatch_kernel, tm=tm),
        grid_spec=pltpu.PrefetchScalarGridSpec(
            num_scalar_prefetch=1,
            grid=(T // tm,),
            in_specs=[pl.BlockSpec((tm, D_MODEL), lambda i, d: (i, 0)),
                      pl.BlockSpec(memory_space=pl.ANY)],
            out_specs=pl.BlockSpec(memory_space=pl.ANY),
            scratch_shapes=[pltpu.SemaphoreType.DMA],
        ),
        out_shape=jax.ShapeDtypeStruct(xs_init.shape, xs_init.dtype),
        input_output_aliases={2: 0},
        compiler_params=pltpu.CompilerParams(dimension_semantics=("arbitrary",)),
        name="dispatch",
    )(dest_flat, h, xs_init)


def _ffn_kernel(be_ref, nv_ref, xs_ref, wg_ref, wu_ref, wd_ref, y_ref, wgb, wub, wdb):
    n = pl.program_id(0)
    prev = be_ref[jnp.maximum(n - 1, 0)]
    fresh = jnp.logical_or(n == 0, be_ref[n] != prev)

    @pl.when(jnp.logical_and(fresh, n < nv_ref[0]))
    def _():
        wgb[...] = wg_ref[0].astype(BF16)
        wub[...] = wu_ref[0].astype(BF16)
        wdb[...] = wd_ref[0].astype(BF16)

    @pl.when(n < nv_ref[0])
    def _():
        xb = xs_ref[...].astype(BF16)
        a = _dot(xb, wgb[...])
        u = _dot(xb, wub[...])
        hid = (a * jax.nn.sigmoid(a) * u).astype(BF16)
        y_ref[...] = _dot(hid, wdb[...])

    @pl.when(n >= nv_ref[0])
    def _():
        y_ref[...] = jnp.zeros_like(y_ref)


def _ffn(be, nv, xs, w_gate, w_up, w_down):
    n_rows = xs.shape[0]
    n_blocks = n_rows // ROW_BLOCK

    def blk(n, be, nv):
        return (jnp.minimum(n, nv[0] - 1), 0)

    def wsel(n, be, nv):
        return (be[jnp.minimum(n, nv[0] - 1)], 0, 0)

    return pl.pallas_call(
        _ffn_kernel,
        grid_spec=pltpu.PrefetchScalarGridSpec(
            num_scalar_prefetch=2,
            grid=(n_blocks,),
            in_specs=[pl.BlockSpec((ROW_BLOCK, D_MODEL), blk),
                      pl.BlockSpec((1, D_MODEL, D_FF), wsel),
                      pl.BlockSpec((1, D_MODEL, D_FF), wsel),
                      pl.BlockSpec((1, D_FF, D_MODEL), wsel)],
            out_specs=pl.BlockSpec((ROW_BLOCK, D_MODEL), lambda n, be, nv: (n, 0)),
            scratch_shapes=[pltpu.VMEM((D_MODEL, D_FF), BF16), pltpu.VMEM((D_MODEL, D_FF), BF16),
                            pltpu.VMEM((D_FF, D_MODEL), BF16)],
        ),
        out_shape=jax.ShapeDtypeStruct((n_rows, D_MODEL), F32),
        compiler_params=pltpu.CompilerParams(
            dimension_semantics=("arbitrary",), vmem_limit_bytes=VMEM_LIMIT),
        name="expert_ffn",
    )(be, nv, xs, w_gate, w_up, w_down)


def _sc_gather_rows(table, idx):
    n = idx.shape[0]
    d = table.shape[1]
    info = plsc.get_sparse_core_info()
    nc, ns = info.num_cores, info.num_subcores
    per_w = n // (nc * ns)
    assert per_w * nc * ns == n and per_w % SC_INDEX_WINDOW == 0
    mesh = plsc.VectorSubcoreMesh(core_axis_name="core", subcore_axis_name="subcore")

    @functools.partial(
        pl.kernel, out_type=jax.ShapeDtypeStruct((n, d), table.dtype), mesh=mesh,
        scratch_types=[pltpu.VMEM((SC_INDEX_WINDOW,), jnp.int32),
                       pltpu.VMEM((SC_GATHER_ROWS, d), table.dtype)],
        name="sc_gather_rows")
    def gather(x_hbm, i_hbm, o_hbm, idx_v, buf):
        wid = lax.axis_index("subcore") * nc + lax.axis_index("core")
        base = wid * per_w

        @pl.loop(0, per_w // SC_INDEX_WINDOW)
        def _(j):
            off = base + j * SC_INDEX_WINDOW
            pltpu.sync_copy(i_hbm.at[pl.ds(off, SC_INDEX_WINDOW)], idx_v)
            for s in range(SC_INDEX_WINDOW // SC_GATHER_ROWS):
                pltpu.sync_copy(x_hbm.at[idx_v.at[pl.ds(s * SC_GATHER_ROWS, SC_GATHER_ROWS)]], buf)
                pltpu.sync_copy(buf, o_hbm.at[pl.ds(off + s * SC_GATHER_ROWS, SC_GATHER_ROWS)])

    return gather(table, idx)


def _combine_kernel(h_ref, y2_ref, gate_ref, g_ref, b_ref, o_ref):
    gate = gate_ref[...]
    ffn = y2_ref[:, 0:D_MODEL] * gate[:, 0:1] + y2_ref[:, D_MODEL:2 * D_MODEL] * gate[:, 1:2]
    o_ref[...] = _layer_norm(DEEPNORM_ALPHA * h_ref[...] + ffn, g_ref[...], b_ref[...])


def _combine(h, y2, gate, g, b, tm=512):
    T = h.shape[0]
    return pl.pallas_call(
        _combine_kernel,
        grid=(T // tm,),
        in_specs=[pl.BlockSpec((tm, D_MODEL), lambda i: (i, 0)),
                  pl.BlockSpec((tm, 2 * D_MODEL), lambda i: (i, 0)),
                  pl.BlockSpec((tm, 2), lambda i: (i, 0)),
                  pl.BlockSpec((1, D_MODEL), lambda i: (0, 0)),
                  pl.BlockSpec((1, D_MODEL), lambda i: (0, 0))],
        out_specs=pl.BlockSpec((tm, D_MODEL), lambda i: (i, 0)),
        out_shape=jax.ShapeDtypeStruct((T, D_MODEL), F32),
        compiler_params=pltpu.CompilerParams(
            dimension_semantics=("parallel",), vmem_limit_bytes=VMEM_LIMIT),
        name="combine",
    )(h, y2, gate, g, b)


def kernel(x, w_in, gla_gate_w2, gla_gate_b, gla_norm_g, dil_norm_g, w_out, ln1_g, ln1_b,
           router_coarse_w, router_coarse_b, router_fine_w, router_fine_b,
           expert_w_gate, expert_w_up, expert_w_down, ln2_g, ln2_b):
    B, S, D = x.shape
    T = B * S
    depth = w_in.shape[0]
    slopes = jnp.exp2(-8.0 * jnp.arange(1, DIL_HEADS + 1, dtype=F32) / DIL_HEADS)
    n_rows = 2 * T + N_EXPERTS * ROW_BLOCK
    n_blocks = n_rows // ROW_BLOCK
    n_blocks_pad = -(-n_blocks // (2 * LANES)) * (2 * LANES)
    assert n_blocks_pad == 2 * LANES
    a0 = 1536
    h = x.reshape(T, D)
    for l in range(depth):
        w = w_in[l]
        wm = jnp.concatenate([w[:, :a0], w[:, a0 + GLA_GATE_RANK:]], axis=1).astype(BF16)
        wa = jnp.pad(w[:, a0:a0 + GLA_GATE_RANK], ((0, 0), (0, LANES - GLA_GATE_RANK))).astype(BF16)
        w2 = jnp.pad(gla_gate_w2[l], ((0, LANES - GLA_GATE_RANK), (0, 0)))
        w2h, w2l = _split_bf16(w2)
        q, k, v, r, la, dq, dk, dv = _in_proj(h, wm, wa, w2h, w2l, gla_gate_b[l][None, :])
        o_gla = _gla(q, k, v, r, la, gla_norm_g[l][None, :], B, S)
        g2 = jnp.tile(dil_norm_g[l], 2)[None, :]
        o_dil = _dilated(slopes, dq, dk, dv, g2, B, S)
        wo = w_out[l].astype(BF16)
        rw = jnp.concatenate([router_fine_w[l].reshape(D, N_EXPERTS), router_coarse_w[l]], axis=1)
        rw = jnp.pad(rw, ((0, 0), (0, LANES - N_EXPERTS - N_GROUPS)))
        rwh, rwl = _split_bf16(rw)
        rb = jnp.concatenate([router_fine_b[l].reshape(N_EXPERTS), router_coarse_b[l]])
        rb = jnp.pad(rb, (0, LANES - N_EXPERTS - N_GROUPS))[None, :]
        h1, eid, gate = _out_proj(o_gla, o_dil, h, wo[:GLA_WIDTH], wo[GLA_WIDTH:],
                                  ln1_g[l][None, :], ln1_b[l][None, :], rwh, rwl, rb)
        dest, be, nv = _positions(eid, n_blocks_pad)
        dest_flat = dest.reshape(2 * T)
        xs = _dispatch(dest_flat, h1, jnp.zeros((n_rows, D), F32))
        y = _ffn(be.reshape(n_blocks_pad), nv.reshape(LANES)[:1], xs,
                 expert_w_gate[l], expert_w_up[l], expert_w_down[l])
        y2 = _sc_gather_rows(y, dest_flat).reshape(T, 2 * D)
        h = _combine(h1, y2, gate, ln2_g[l][None, :], ln2_b[l][None, :])
    return h.reshape(B, S, D)
```

```python
import functools
import math

import jax
import jax.numpy as jnp
import numpy as np
from jax import lax
from jax.experimental import pallas as pl
from jax.experimental.pallas import tpu as pltpu
from jax.experimental.pallas import tpu_sc as plsc

D_MODEL = 1024
GLA_HEADS = 4
GLA_DK = 64
GLA_DV = 128
GLA_KEY_WIDTH = GLA_HEADS * GLA_DK
GLA_WIDTH = GLA_HEADS * GLA_DV
GLA_GATE_RANK = 16
GLA_GATE_TEMP = 16.0
DIL_HEADS = 8
DIL_DH = 64
DIL_WIDTH = DIL_HEADS * DIL_DH
DIL_CONFIGS = ((128, 1), (512, 4), (2048, 16))
DIL_BLOCK = 128
DIL_PAD = DIL_BLOCK * max(r for _, r in DIL_CONFIGS)
DIL_UNROLL = 4
N_GROUPS = 4
EXPERTS_PER_GROUP = 8
N_EXPERTS = N_GROUPS * EXPERTS_PER_GROUP
D_FF = 512
DEEPNORM_ALPHA = 2.0 ** 0.25
EPS = 1e-5

LANES = 128
GLA_CHUNK = 64
GLA_UNROLL = 4
SC_LANES = 16
SC_INDEX_WINDOW = 128
SC_GATHER_ROWS = 32
ROW_BLOCK = 256
VMEM_LIMIT = 56 * 1024 * 1024

F32 = jnp.float32
BF16 = jnp.bfloat16


def _dot(a, b):
    return jnp.dot(a, b, preferred_element_type=F32)


def _dot_nt(a, b):
    return lax.dot_general(a, b, (((1,), (1,)), ((), ())), preferred_element_type=F32)


def _dot_tn(a, b):
    return lax.dot_general(a, b, (((0,), (0,)), ((), ())), preferred_element_type=F32)


def _split_bf16(v):
    hi = v.astype(BF16)
    lo = (v - hi.astype(F32)).astype(BF16)
    return hi, lo


def _layer_norm(v, g, b):
    mu = jnp.mean(v, axis=-1, keepdims=True)
    c = v - mu
    var = jnp.mean(c * c, axis=-1, keepdims=True)
    return c * lax.rsqrt(var + EPS) * g + b


def _in_proj_kernel(x_ref, wm_ref, wa_ref, w2h_ref, w2l_ref, gb_ref,
                    q_ref, k_ref, v_ref, r_ref, la_ref, dq_ref, dk_ref, dv_ref):
    xb = x_ref[...].astype(BF16)

    def piece(c0, c1):
        return _dot(xb, wm_ref[:, c0:c1])

    q_ref[...] = (piece(0, 256) * (GLA_DK ** -0.5)).astype(BF16)
    k_ref[...] = piece(256, 512).astype(BF16)
    v_ref[...] = piece(512, 1024).astype(BF16)
    r_ref[...] = piece(1024, 1536).astype(BF16)
    dq_ref[...] = (piece(1536, 2048) * (DIL_DH ** -0.5)).astype(BF16)
    dk_ref[...] = piece(2048, 2560).astype(BF16)
    dv_ref[...] = piece(2560, 3072).astype(BF16)
    ga = _dot(xb, wa_ref[...])
    ga_hi, ga_lo = _split_bf16(ga)
    z = _dot(ga_hi, w2h_ref[...]) + _dot(ga_lo, w2h_ref[...]) + _dot(ga_hi, w2l_ref[...]) + gb_ref[...]
    log_sig = jnp.minimum(z, 0.0) - jnp.log1p(jnp.exp(-jnp.abs(z)))
    la_ref[...] = log_sig * (1.0 / GLA_GATE_TEMP)


def _in_proj(x2, wm, wa, w2h, w2l, gb, tm=512):
    T = x2.shape[0]
    row = lambda w: pl.BlockSpec((tm, w), lambda i: (i, 0))
    full = lambda a: pl.BlockSpec(a.shape, lambda i: (0,) * a.ndim)
    outs = [(256, BF16), (256, BF16), (512, BF16), (512, BF16), (256, F32),
            (512, BF16), (512, BF16), (512, BF16)]
    return pl.pallas_call(
        _in_proj_kernel,
        grid=(T // tm,),
        in_specs=[row(D_MODEL), full(wm), full(wa), full(w2h), full(w2l), full(gb)],
        out_specs=[row(w) for w, _ in outs],
        out_shape=[jax.ShapeDtypeStruct((T, w), dt) for w, dt in outs],
        compiler_params=pltpu.CompilerParams(
            dimension_semantics=("parallel",), vmem_limit_bytes=VMEM_LIMIT),
        name="in_proj",
    )(x2, wm, wa, w2h, w2l, gb)


def _gla_kernel(q_ref, k_ref, v_ref, r_ref, la_ref, g_ref, o_ref, s_ref, *, seq_block):
    C = GLA_CHUNK

    @pl.when(pl.program_id(1) == 0)
    def _():
        s_ref[...] = jnp.zeros_like(s_ref)

    ri = lax.broadcasted_iota(jnp.int32, (C, C), 0)
    ci = lax.broadcasted_iota(jnp.int32, (C, C), 1)
    causal = ci <= ri
    tri = causal.astype(BF16)
    ones_cl = jnp.ones((C, LANES), BF16)
    lane_k = lax.broadcasted_iota(jnp.int32, (1, GLA_KEY_WIDTH), 1) // GLA_DK
    head_masks = [(lane_k == h).astype(F32) for h in range(GLA_HEADS)]
    srow = lax.broadcasted_iota(jnp.int32, (GLA_KEY_WIDTH, GLA_WIDTH), 0) // GLA_DK
    scol = lax.broadcasted_iota(jnp.int32, (GLA_KEY_WIDTH, GLA_WIDTH), 1) // GLA_DV
    state_mask = (srow == scol).astype(F32)
    g = g_ref[...]

    def chunk(c, carry):
        rows = pl.ds(pl.multiple_of(c * C, C), C)
        la = la_ref[rows, :]
        la_hi, la_lo = _split_bf16(la)
        b = _dot(tri, la_hi) + _dot(tri, la_lo)
        b_last = b[C - 1:C, :]
        q = q_ref[rows, :].astype(F32)
        k = k_ref[rows, :].astype(F32)
        v = v_ref[rows, :]
        qd = q * jnp.exp(b)
        kd = (k * jnp.exp(-b)).astype(BF16)
        ke = (k * jnp.exp(b_last - b)).astype(BF16)
        state = s_ref[...]
        o_inter = _dot(qd.astype(BF16), state.astype(BF16))
        outs = []
        for h in range(GLA_HEADS):
            a = _dot_nt((qd * head_masks[h]).astype(BF16), kd)
            a = jnp.where(causal, a, 0.0).astype(BF16)
            cols = slice(h * GLA_DV, (h + 1) * GLA_DV)
            o = _dot(a, v[:, cols]) + o_inter[:, cols]
            o = o * lax.rsqrt(jnp.mean(o * o, axis=-1, keepdims=True) + EPS) * g
            outs.append(o)
        o_all = jnp.concatenate(outs, axis=-1)
        rr = r_ref[rows, :].astype(F32)
        o_ref[rows, :] = (o_all * (rr * jax.nn.sigmoid(rr))).astype(BF16)
        tot = _dot_tn(la_hi, ones_cl) + _dot_tn(la_lo, ones_cl)
        dec = jnp.exp(tot)
        upd = _dot_tn(ke, v) * state_mask
        for h in range(GLA_HEADS):
            cols = slice(h * GLA_DV, (h + 1) * GLA_DV)
            s_ref[:, cols] = state[:, cols] * dec + upd[:, cols]
        return carry

    lax.fori_loop(0, seq_block // C, chunk, 0, unroll=GLA_UNROLL)


def _gla(q, k, v, r, la, g, batch, seq, seq_block=1024):
    nsb = seq // seq_block
    row = lambda w: pl.BlockSpec((seq_block, w), lambda b, s: (b * nsb + s, 0))
    return pl.pallas_call(
        functools.partial(_gla_kernel, seq_block=seq_block),
        grid=(batch, nsb),
        in_specs=[row(256), row(256), row(512), row(512), row(256),
                  pl.BlockSpec((1, GLA_DV), lambda b, s: (0, 0))],
        out_specs=row(512),
        out_shape=jax.ShapeDtypeStruct((batch * seq, GLA_WIDTH), BF16),
        scratch_shapes=[pltpu.VMEM((GLA_KEY_WIDTH, GLA_WIDTH), F32)],
        compiler_params=pltpu.CompilerParams(
            dimension_semantics=("parallel", "arbitrary"), vmem_limit_bytes=VMEM_LIMIT),
        name="gla",
    )(q, k, v, r, la, g)


def _dil_kernel(slope_ref, q_ref, k_ref, v_ref, g_ref, o_ref,
                qf, kf, vf, oc, lc, *, seq):
    B = DIL_BLOCK
    pair = pl.program_id(1)
    qf[...] = q_ref[...].astype(F32)
    kf[0:DIL_PAD, :] = jnp.zeros((DIL_PAD, LANES), F32)
    vf[0:DIL_PAD, :] = jnp.zeros((DIL_PAD, LANES), F32)
    kf[DIL_PAD:DIL_PAD + seq, :] = k_ref[...].astype(F32)
    vf[DIL_PAD:DIL_PAD + seq, :] = v_ref[...].astype(F32)

    lane = lax.broadcasted_iota(jnp.int32, (1, LANES), 1)
    first = lane < DIL_DH
    ii = lax.broadcasted_iota(jnp.int32, (B, B), 0)
    jj = lax.broadcasted_iota(jnp.int32, (B, B), 1)
    upper = jj > ii
    eye = jj == ii
    dist = jnp.bitwise_and(ii - jj, B - 1).astype(F32)
    neg = jnp.float32(-jnp.inf)

    for cfg, (window, r) in enumerate(DIL_CONFIGS):
        nb = seq // r // B
        slopes = [slope_ref[2 * pair + hh] * float(r) for hh in range(2)]
        biases = [dist * (-slopes[hh]) for hh in range(2)]

        def block(it, cfg=cfg, r=r, nb=nb, slopes=slopes, biases=biases):
            c = it // nb
            n = it % nb
            start = c + n * (B * r)
            has_prev = n > 0
            rows = pl.ds(start, B, stride=r) if r > 1 else pl.ds(start, B)
            kv0 = DIL_PAD + start - B * r
            rows_kv = pl.ds(kv0, 2 * B, stride=r) if r > 1 else pl.ds(kv0, 2 * B)
            q = qf[rows, :]
            kcat = kf[rows_kv, :].astype(BF16)
            vcat_f = vf[rows_kv, :]
            vcat = vcat_f.astype(BF16)
            v_far = vcat_f[0:B, :]
            o_pair = None
            l_pair = None
            for hh in range(2):
                hmask = first if hh == 0 else jnp.logical_not(first)
                qh = jnp.where(hmask, q, 0.0).astype(BF16)
                s2 = _dot_nt(qh, kcat)
                s_prev = s2[:, 0:B]
                s_cur = s2[:, B:2 * B]
                s = jnp.where(upper, jnp.where(has_prev, s_prev, neg), s_cur) + biases[hh]
                far = jnp.sum(jnp.where(eye, s_prev, 0.0), axis=-1, keepdims=True) - slopes[hh] * float(B)
                far = jnp.where(has_prev, far, neg)
                m = jnp.maximum(jnp.max(s, axis=-1, keepdims=True), far)
                p = jnp.exp(s - m)
                p_far = jnp.exp(far - m)
                z = jnp.sum(p, axis=-1, keepdims=True) + p_far
                pcat = jnp.concatenate([jnp.where(upper, p, 0.0), jnp.where(upper, 0.0, p)], axis=1)
                o = (_dot(pcat.astype(BF16), vcat) + p_far * v_far) * (1.0 / z)
                lse = jnp.broadcast_to(m + jnp.log(z), (B, LANES))
                if hh == 0:
                    o_pair, l_pair = o, lse
                else:
                    o_pair = jnp.where(first, o_pair, o)
                    l_pair = jnp.where(first, l_pair, lse)
            oc[cfg, rows, :] = o_pair
            lc[cfg, rows, :] = l_pair

        def body(t, carry, block=block):
            for u in range(DIL_UNROLL):
                block(t * DIL_UNROLL + u)
            return carry

        lax.fori_loop(0, seq // B // DIL_UNROLL, body, 0)

    g = g_ref[...]
    CH = 512

    def mix(i, carry):
        rows = pl.ds(pl.multiple_of(i * CH, CH), CH)
        l0, l1, l2 = lc[0, rows, :], lc[1, rows, :], lc[2, rows, :]
        m = jnp.maximum(jnp.maximum(l0, l1), l2)
        e0, e1, e2 = jnp.exp(l0 - m), jnp.exp(l1 - m), jnp.exp(l2 - m)
        den = e0 + e1 + e2
        o = (e0 / den) * oc[0, rows, :] + (e1 / den) * oc[1, rows, :] + (e2 / den) * oc[2, rows, :]
        sq = o * o
        ms_a = jnp.sum(jnp.where(first, sq, 0.0), axis=-1, keepdims=True) * (1.0 / DIL_DH)
        ms_b = jnp.sum(jnp.where(first, 0.0, sq), axis=-1, keepdims=True) * (1.0 / DIL_DH)
        ms = jnp.where(first, ms_a, ms_b)
        o_ref[rows, :] = (o * lax.rsqrt(ms + EPS) * g).astype(BF16)
        return carry

    lax.fori_loop(0, seq // CH, mix, 0)


def _dilated(slopes, dq, dk, dv, g2, batch, seq):
    blk = pl.BlockSpec((seq, LANES), lambda b, p, s: (b, p))
    return pl.pallas_call(
        functools.partial(_dil_kernel, seq=seq),
        grid_spec=pltpu.PrefetchScalarGridSpec(
            num_scalar_prefetch=1,
            grid=(batch, DIL_WIDTH // LANES),
            in_specs=[blk, blk, blk, pl.BlockSpec((1, LANES), lambda b, p, s: (0, 0))],
            out_specs=blk,
            scratch_shapes=[pltpu.VMEM((seq, LANES), F32)] + [pltpu.VMEM((seq + DIL_PAD, LANES), F32)] * 2
                           + [pltpu.VMEM((3, seq, LANES), F32)] * 2,
        ),
        out_shape=jax.ShapeDtypeStruct((batch * seq, DIL_WIDTH), BF16),
        compiler_params=pltpu.CompilerParams(
            dimension_semantics=("parallel", "parallel"), vmem_limit_bytes=VMEM_LIMIT),
        name="dilated",
    )(slopes, dq, dk, dv, g2)


def _out_proj_kernel(og_ref, od_ref, x_ref, wg_ref, wd_ref, g_ref, b_ref,
                     rwh_ref, rwl_ref, rb_ref, h_ref, eid_ref, gate_ref):
    mix = _dot(og_ref[...], wg_ref[...]) + _dot(od_ref[...], wd_ref[...])
    h = _layer_norm(DEEPNORM_ALPHA * x_ref[...] + mix, g_ref[...], b_ref[...])
    h_ref[...] = h
    h_hi, h_lo = _split_bf16(h)
    logits = (_dot(h_hi, rwh_ref[...]) + _dot(h_lo, rwh_ref[...]) + _dot(h_hi, rwl_ref[...])
              + rb_ref[...])
    lane = lax.broadcasted_iota(jnp.int32, logits.shape, 1)
    lane_f = lane.astype(F32)
    neg = jnp.float32(-jnp.inf)
    big = jnp.float32(1e9)
    is_coarse = jnp.logical_and(lane >= N_EXPERTS, lane < N_EXPERTS + N_GROUPS)
    coarse = jnp.where(is_coarse, logits, neg)
    cmax = jnp.max(coarse, axis=-1, keepdims=True)
    g_idx = jnp.min(jnp.where(coarse == cmax, lane_f, big), axis=-1, keepdims=True) - N_EXPERTS
    p_group = 1.0 / jnp.sum(jnp.exp(coarse - cmax), axis=-1, keepdims=True)
    lo = g_idx * EXPERTS_PER_GROUP
    in_group = jnp.logical_and(lane_f >= lo, lane_f < lo + EXPERTS_PER_GROUP)
    fine = jnp.where(in_group, logits, neg)
    v1 = jnp.max(fine, axis=-1, keepdims=True)
    i1 = jnp.min(jnp.where(fine == v1, lane_f, big), axis=-1, keepdims=True)
    fine2 = jnp.where(lane_f == i1, neg, fine)
    v2 = jnp.max(fine2, axis=-1, keepdims=True)
    i2 = jnp.min(jnp.where(fine2 == v2, lane_f, big), axis=-1, keepdims=True)
    e2 = jnp.exp(v2 - v1)
    den = 1.0 + e2
    gate1 = p_group * (1.0 / den)
    gate2 = p_group * (e2 / den)
    lane2 = lax.broadcasted_iota(jnp.int32, (logits.shape[0], 2), 1)
    eid_ref[...] = jnp.where(lane2 == 0, i1, i2).astype(jnp.int32)
    gate_ref[...] = jnp.where(lane2 == 0, gate1, gate2)


def _out_proj(og, od, x2, wg, wd, g, b, rwh, rwl, rb, tm=512):
    T = x2.shape[0]
    row = lambda w: pl.BlockSpec((tm, w), lambda i: (i, 0))
    full = lambda a: pl.BlockSpec(a.shape, lambda i: (0,) * a.ndim)
    return pl.pallas_call(
        _out_proj_kernel,
        grid=(T // tm,),
        in_specs=[row(512), row(512), row(D_MODEL), full(wg), full(wd), full(g), full(b),
                  full(rwh), full(rwl), full(rb)],
        out_specs=[row(D_MODEL), row(2), row(2)],
        out_shape=[jax.ShapeDtypeStruct((T, D_MODEL), F32),
                   jax.ShapeDtypeStruct((T, 2), jnp.int32),
                   jax.ShapeDtypeStruct((T, 2), F32)],
        compiler_params=pltpu.CompilerParams(
            dimension_semantics=("parallel",), vmem_limit_bytes=VMEM_LIMIT),
        name="out_proj_router",
    )(og, od, x2, wg, wd, g, b, rwh, rwl, rb)


def _positions_kernel(eid_ref, dest_ref, be_ref, nv_ref, carry_ref, cnt_col_ref, sp_ref, *, tb):
    phase = pl.program_id(0)
    i = pl.program_id(1)
    lane = lax.broadcasted_iota(jnp.int32, (tb, LANES), 1)
    eid = eid_ref[...]
    oh1 = lane == eid[:, 0:1]
    oh2 = lane == eid[:, 1:2]
    oh = jnp.logical_or(oh1, oh2).astype(BF16)

    @pl.when(jnp.logical_and(phase == 0, i == 0))
    def _():
        carry_ref[...] = jnp.zeros_like(carry_ref)
        cnt_col_ref[...] = jnp.zeros_like(cnt_col_ref)

    @pl.when(phase == 0)
    def _():
        ones_r = jnp.ones((8, tb), BF16)
        ones_c = jnp.ones((tb, LANES), BF16)
        carry_ref[...] += _dot(ones_r, oh)
        cnt_col_ref[...] += _dot_tn(oh, ones_c)

    @pl.when(jnp.logical_and(phase == 1, i == 0))
    def _():
        shift = int(math.log2(ROW_BLOCK))
        nb_row = ((carry_ref[...].astype(jnp.int32) + (ROW_BLOCK - 1)) >> shift)
        nb_col = ((cnt_col_ref[...].astype(jnp.int32) + (ROW_BLOCK - 1)) >> shift)
        r = lax.broadcasted_iota(jnp.int32, (LANES, LANES), 0)
        c = lax.broadcasted_iota(jnp.int32, (LANES, LANES), 1)
        excl = (r < c).astype(BF16)
        incl = (c <= r).astype(BF16)
        start_blk = _dot(nb_row.astype(F32).astype(BF16), excl)
        sp_ref[...] = start_blk * float(ROW_BLOCK)
        end_col = _dot(incl, nb_col.astype(F32).astype(BF16))
        end2 = jnp.concatenate([end_col, end_col], axis=1)
        nidx = lax.broadcasted_iota(jnp.int32, (LANES, 2 * LANES), 1).astype(F32)
        eidx = lax.broadcasted_iota(jnp.int32, (LANES, 2 * LANES), 0)
        hit = jnp.logical_and(end2 <= nidx, eidx < N_EXPERTS).astype(F32)
        be = jnp.minimum(jnp.sum(hit, axis=0, keepdims=True), float(N_EXPERTS - 1))
        be_ref[...] = be.astype(jnp.int32)
        lane1 = lax.broadcasted_iota(jnp.int32, (1, LANES), 1)
        total = jnp.sum(jnp.where(lane1 < N_EXPERTS, nb_row[0:1, :].astype(F32), 0.0), axis=-1, keepdims=True)
        nv_ref[...] = jnp.broadcast_to(total, (1, LANES)).astype(jnp.int32)
        carry_ref[...] = jnp.zeros_like(carry_ref)

    @pl.when(phase == 1)
    def _():
        r = lax.broadcasted_iota(jnp.int32, (tb, tb), 0)
        c = lax.broadcasted_iota(jnp.int32, (tb, tb), 1)
        strict = (c < r).astype(BF16)
        before = _dot(strict, oh) + carry_ref[0:1, :] + sp_ref[0:1, :]
        d1 = jnp.sum(jnp.where(oh1, before, 0.0), axis=-1, keepdims=True)
        d2 = jnp.sum(jnp.where(oh2, before, 0.0), axis=-1, keepdims=True)
        lane2 = lax.broadcasted_iota(jnp.int32, (tb, 2), 1)
        dest_ref[...] = jnp.where(lane2 == 0, d1, d2).astype(jnp.int32)
        carry_ref[...] += _dot(jnp.ones((8, tb), BF16), oh)


def _positions(eid, n_blocks_pad, tb=512):
    T = eid.shape[0]
    nb = T // tb
    return pl.pallas_call(
        functools.partial(_positions_kernel, tb=tb),
        grid=(2, nb),
        in_specs=[pl.BlockSpec((tb, 2), lambda p, i: (i, 0))],
        out_specs=[pl.BlockSpec((tb, 2), lambda p, i: (i * p, 0)),
                   pl.BlockSpec((1, n_blocks_pad), lambda p, i: (0, 0)),
                   pl.BlockSpec((1, LANES), lambda p, i: (0, 0))],
        out_shape=[jax.ShapeDtypeStruct((T, 2), jnp.int32),
                   jax.ShapeDtypeStruct((1, n_blocks_pad), jnp.int32),
                   jax.ShapeDtypeStruct((1, LANES), jnp.int32)],
        scratch_shapes=[pltpu.VMEM((8, LANES), F32), pltpu.VMEM((LANES, LANES), F32),
                        pltpu.VMEM((8, LANES), F32)],
        compiler_params=pltpu.CompilerParams(dimension_semantics=("arbitrary", "arbitrary")),
        name="positions",
    )(eid)


def _sc_gather_rows(table, idx):
    n = idx.shape[0]
    d = table.shape[1]
    info = plsc.get_sparse_core_info()
    nc, ns = info.num_cores, info.num_subcores
    per_w = n // (nc * ns)
    assert per_w * nc * ns == n and per_w % SC_INDEX_WINDOW == 0
    mesh = plsc.VectorSubcoreMesh(core_axis_name="core", subcore_axis_name="subcore")
    nsub = SC_INDEX_WINDOW // SC_GATHER_ROWS

    @functools.partial(
        pl.kernel, out_type=jax.ShapeDtypeStruct((n, d), table.dtype), mesh=mesh,
        scratch_types=[pltpu.VMEM((SC_INDEX_WINDOW,), jnp.int32),
                       pltpu.VMEM((2, SC_GATHER_ROWS, d), table.dtype),
                       pltpu.SemaphoreType.DMA((2,)), pltpu.SemaphoreType.DMA((2,))],
        name="sc_gather_rows")
    def gather(x_hbm, i_hbm, o_hbm, idx_v, buf, gsem, wsem):
        wid = lax.axis_index("subcore") * nc + lax.axis_index("core")
        base = wid * per_w

        def gather_copy(s):
            rows = idx_v.at[pl.ds(s * SC_GATHER_ROWS, SC_GATHER_ROWS)]
            return pltpu.make_async_copy(x_hbm.at[rows], buf.at[s % 2], gsem.at[s % 2])

        def write_copy(off, s):
            dst = o_hbm.at[pl.ds(off + s * SC_GATHER_ROWS, SC_GATHER_ROWS)]
            return pltpu.make_async_copy(buf.at[s % 2], dst, wsem.at[s % 2])

        @pl.loop(0, per_w // SC_INDEX_WINDOW)
        def _(j):
            off = base + j * SC_INDEX_WINDOW
            pltpu.sync_copy(i_hbm.at[pl.ds(off, SC_INDEX_WINDOW)], idx_v)
            gather_copy(0).start()
            for s in range(nsub):
                gather_copy(s).wait()
                write_copy(off, s).start()
                if s >= 1:
                    write_copy(off, s - 1).wait()
                if s + 1 < nsub:
                    gather_copy(s + 1).start()
            write_copy(off, nsub - 1).wait()

    return gather(table, idx)


def _sc_inverse_rows(dest_flat, n_rows, chunk=2048):
    n = dest_flat.shape[0]
    nc = plsc.get_sparse_core_info().num_cores
    mesh = plsc.VectorSubcoreMesh(core_axis_name="core", subcore_axis_name="subcore")

    @functools.partial(
        pl.kernel, out_type=jax.ShapeDtypeStruct((n_rows,), jnp.int32), mesh=mesh,
        scratch_types=[pltpu.VMEM((n_rows,), jnp.int32), pltpu.VMEM((chunk,), jnp.int32)],
        compiler_params=pltpu.CompilerParams(needs_layout_passes=False),
        name="sc_inverse_rows")
    def inverse(d_hbm, o_hbm, inv_v, d_v):
        wid = lax.axis_index("subcore") * nc + lax.axis_index("core")

        @pl.when(wid == 0)
        def _():
            zero = jnp.zeros((SC_LANES,), jnp.int32)

            @pl.loop(0, n_rows // SC_LANES)
            def _(i):
                inv_v[pl.ds(i * SC_LANES, SC_LANES)] = zero

            @pl.loop(0, n // chunk)
            def _(c):
                pltpu.sync_copy(d_hbm.at[pl.ds(c * chunk, chunk)], d_v)

                @pl.loop(0, chunk // SC_LANES)
                def _(j):
                    rows = d_v[pl.ds(j * SC_LANES, SC_LANES)]
                    pair = lax.iota(jnp.int32, SC_LANES) + (c * chunk + j * SC_LANES)
                    plsc.store_scatter(inv_v, [rows], lax.shift_right_logical(pair, 1))

            pltpu.sync_copy(inv_v, o_hbm)

    return inverse(dest_flat)


def _ffn_kernel(be_ref, nv_ref, xs_ref, wg_ref, wu_ref, wd_ref, y_ref, wgb, wub, wdb):
    n = pl.program_id(0)
    prev = be_ref[jnp.maximum(n - 1, 0)]
    fresh = jnp.logical_or(n == 0, be_ref[n] != prev)

    @pl.when(jnp.logical_and(fresh, n < nv_ref[0]))
    def _():
        wgb[...] = wg_ref[0].astype(BF16)
        wub[...] = wu_ref[0].astype(BF16)
        wdb[...] = wd_ref[0].astype(BF16)

    @pl.when(n < nv_ref[0])
    def _():
        xb = xs_ref[...].astype(BF16)
        a = _dot(xb, wgb[...])
        u = _dot(xb, wub[...])
        hid = (a * jax.nn.sigmoid(a) * u).astype(BF16)
        y_ref[...] = _dot(hid, wdb[...])

    @pl.when(n >= nv_ref[0])
    def _():
        y_ref[...] = jnp.zeros_like(y_ref)


def _ffn(be, nv, xs, w_gate, w_up, w_down):
    n_rows = xs.shape[0]
    n_blocks = n_rows // ROW_BLOCK

    def blk(n, be, nv):
        return (jnp.minimum(n, nv[0] - 1), 0)

    def wsel(n, be, nv):
        return (be[jnp.minimum(n, nv[0] - 1)], 0, 0)

    return pl.pallas_call(
        _ffn_kernel,
        grid_spec=pltpu.PrefetchScalarGridSpec(
            num_scalar_prefetch=2,
            grid=(n_blocks,),
            in_specs=[pl.BlockSpec((ROW_BLOCK, D_MODEL), blk),
                      pl.BlockSpec((1, D_MODEL, D_FF), wsel),
                      pl.BlockSpec((1, D_MODEL, D_FF), wsel),
                      pl.BlockSpec((1, D_FF, D_MODEL), wsel)],
            out_specs=pl.BlockSpec((ROW_BLOCK, D_MODEL), lambda n, be, nv: (n, 0)),
            scratch_shapes=[pltpu.VMEM((D_MODEL, D_FF), BF16), pltpu.VMEM((D_MODEL, D_FF), BF16),
                            pltpu.VMEM((D_FF, D_MODEL), BF16)],
        ),
        out_shape=jax.ShapeDtypeStruct((n_rows, D_MODEL), F32),
        compiler_params=pltpu.CompilerParams(
            dimension_semantics=("arbitrary",), vmem_limit_bytes=VMEM_LIMIT),
        name="expert_ffn",
    )(be, nv, xs, w_gate, w_up, w_down)


def _combine_kernel(h_ref, ya_ref, yb_ref, gate_ref, g_ref, b_ref, o_ref):
    gate = gate_ref[...]
    ffn = ya_ref[...] * gate[:, 0:1] + yb_ref[...] * gate[:, 1:2]
    o_ref[...] = _layer_norm(DEEPNORM_ALPHA * h_ref[...] + ffn, g_ref[...], b_ref[...])


def _combine(h, y2, gate, g, b, tm=512):
    T = h.shape[0]
    nt = T // tm
    return pl.pallas_call(
        _combine_kernel,
        grid=(nt,),
        in_specs=[pl.BlockSpec((tm, D_MODEL), lambda i: (i, 0)),
                  pl.BlockSpec((tm, D_MODEL), lambda i: (i, 0)),
                  pl.BlockSpec((tm, D_MODEL), lambda i: (i + nt, 0)),
                  pl.BlockSpec((tm, 2), lambda i: (i, 0)),
                  pl.BlockSpec((1, D_MODEL), lambda i: (0, 0)),
                  pl.BlockSpec((1, D_MODEL), lambda i: (0, 0))],
        out_specs=pl.BlockSpec((tm, D_MODEL), lambda i: (i, 0)),
        out_shape=jax.ShapeDtypeStruct((T, D_MODEL), F32),
        compiler_params=pltpu.CompilerParams(
            dimension_semantics=("parallel",), vmem_limit_bytes=VMEM_LIMIT),
        name="combine",
    )(h, y2, y2, gate, g, b)


def kernel(x, w_in, gla_gate_w2, gla_gate_b, gla_norm_g, dil_norm_g, w_out, ln1_g, ln1_b,
           router_coarse_w, router_coarse_b, router_fine_w, router_fine_b,
           expert_w_gate, expert_w_up, expert_w_down, ln2_g, ln2_b):
    B, S, D = x.shape
    T = B * S
    depth = w_in.shape[0]
    slopes = jnp.exp2(-8.0 * jnp.arange(1, DIL_HEADS + 1, dtype=F32) / DIL_HEADS)
    n_rows = 2 * T + N_EXPERTS * ROW_BLOCK
    n_blocks = n_rows // ROW_BLOCK
    n_blocks_pad = -(-n_blocks // (2 * LANES)) * (2 * LANES)
    assert n_blocks_pad == 2 * LANES
    a0 = 1536
    h = x.reshape(T, D)
    for l in range(depth):
        w = w_in[l]
        wm = jnp.concatenate([w[:, :a0], w[:, a0 + GLA_GATE_RANK:]], axis=1).astype(BF16)
        wa = jnp.pad(w[:, a0:a0 + GLA_GATE_RANK], ((0, 0), (0, LANES - GLA_GATE_RANK))).astype(BF16)
        w2 = jnp.pad(gla_gate_w2[l], ((0, LANES - GLA_GATE_RANK), (0, 0)))
        w2h, w2l = _split_bf16(w2)
        q, k, v, r, la, dq, dk, dv = _in_proj(h, wm, wa, w2h, w2l, gla_gate_b[l][None, :])
        o_gla = _gla(q, k, v, r, la, gla_norm_g[l][None, :], B, S)
        g2 = jnp.tile(dil_norm_g[l], 2)[None, :]
        o_dil = _dilated(slopes, dq, dk, dv, g2, B, S)
        wo = w_out[l].astype(BF16)
        rw = jnp.concatenate([router_fine_w[l].reshape(D, N_EXPERTS), router_coarse_w[l]], axis=1)
        rw = jnp.pad(rw, ((0, 0), (0, LANES - N_EXPERTS - N_GROUPS)))
        rwh, rwl = _split_bf16(rw)
        rb = jnp.concatenate([router_fine_b[l].reshape(N_EXPERTS), router_coarse_b[l]])
        rb = jnp.pad(rb, (0, LANES - N_EXPERTS - N_GROUPS))[None, :]
        h1, eid, gate = _out_proj(o_gla, o_dil, h, wo[:GLA_WIDTH], wo[GLA_WIDTH:],
                                  ln1_g[l][None, :], ln1_b[l][None, :], rwh, rwl, rb)
        dest, be, nv = _positions(eid, n_blocks_pad)
        src_tok = _sc_inverse_rows(dest.reshape(2 * T), n_rows)
        xs = _sc_gather_rows(h1, src_tok)
        y = _ffn(be.reshape(n_blocks_pad), nv.reshape(LANES)[:1], xs,
                 expert_w_gate[l], expert_w_up[l], expert_w_down[l])
        y2 = _sc_gather_rows(y, dest.T.reshape(2 * T))
        h = _combine(h1, y2, gate, ln2_g[l][None, :], ln2_b[l][None, :])
    return h.reshape(B, S, D)
```

```python
import functools
import math

import jax
import jax.numpy as jnp
import numpy as np
from jax import lax
from jax.experimental import pallas as pl
from jax.experimental.pallas import tpu as pltpu
from jax.experimental.pallas import tpu_sc as plsc

D_MODEL = 1024
GLA_HEADS = 4
GLA_DK = 64
GLA_DV = 128
GLA_KEY_WIDTH = GLA_HEADS * GLA_DK
GLA_WIDTH = GLA_HEADS * GLA_DV
GLA_GATE_RANK = 16
GLA_GATE_TEMP = 16.0
DIL_HEADS = 8
DIL_DH = 64
DIL_WIDTH = DIL_HEADS * DIL_DH
DIL_CONFIGS = ((128, 1), (512, 4), (2048, 16))
DIL_BLOCK = 128
DIL_PAD = DIL_BLOCK * max(r for _, r in DIL_CONFIGS)
DIL_UNROLL = 4
N_GROUPS = 4
EXPERTS_PER_GROUP = 8
N_EXPERTS = N_GROUPS * EXPERTS_PER_GROUP
D_FF = 512
DEEPNORM_ALPHA = 2.0 ** 0.25
EPS = 1e-5

LANES = 128
GLA_CHUNK = 64
GLA_UNROLL = 4
SC_LANES = 16
SC_INDEX_WINDOW = 128
SC_GATHER_ROWS = 32
ROW_BLOCK = 256
VMEM_LIMIT = 56 * 1024 * 1024

F32 = jnp.float32
BF16 = jnp.bfloat16


def _dot(a, b):
    return jnp.dot(a, b, preferred_element_type=F32)


def _dot_nt(a, b):
    return lax.dot_general(a, b, (((1,), (1,)), ((), ())), preferred_element_type=F32)


def _dot_tn(a, b):
    return lax.dot_general(a, b, (((0,), (0,)), ((), ())), preferred_element_type=F32)


def _split_bf16(v):
    hi = v.astype(BF16)
    lo = (v - hi.astype(F32)).astype(BF16)
    return hi, lo


def _layer_norm(v, g, b):
    mu = jnp.mean(v, axis=-1, keepdims=True)
    c = v - mu
    var = jnp.mean(c * c, axis=-1, keepdims=True)
    return c * lax.rsqrt(var + EPS) * g + b


def _in_proj_kernel(x_ref, wm_ref, wa_ref, w2h_ref, w2l_ref, gb_ref,
                    q_ref, k_ref, v_ref, r_ref, la_ref, dq_ref, dk_ref, dv_ref):
    xb = x_ref[...].astype(BF16)

    def piece(c0, c1):
        return _dot(xb, wm_ref[:, c0:c1])

    q_ref[...] = (piece(0, 256) * (GLA_DK ** -0.5)).astype(BF16)
    k_ref[...] = piece(256, 512).astype(BF16)
    v_ref[...] = piece(512, 1024).astype(BF16)
    r_ref[...] = piece(1024, 1536).astype(BF16)
    dq_ref[...] = (piece(1536, 2048) * (DIL_DH ** -0.5)).astype(BF16)
    dk_ref[...] = piece(2048, 2560).astype(BF16)
    dv_ref[...] = piece(2560, 3072).astype(BF16)
    ga = _dot(xb, wa_ref[...])
    ga_hi, ga_lo = _split_bf16(ga)
    z = _dot(ga_hi, w2h_ref[...]) + _dot(ga_lo, w2h_ref[...]) + _dot(ga_hi, w2l_ref[...]) + gb_ref[...]
    log_sig = jnp.minimum(z, 0.0) - jnp.log1p(jnp.exp(-jnp.abs(z)))
    la_ref[...] = log_sig * (1.0 / GLA_GATE_TEMP)


def _in_proj(x2, wm, wa, w2h, w2l, gb, tm=512):
    T = x2.shape[0]
    row = lambda w: pl.BlockSpec((tm, w), lambda i: (i, 0))
    full = lambda a: pl.BlockSpec(a.shape, lambda i: (0,) * a.ndim)
    outs = [(256, BF16), (256, BF16), (512, BF16), (512, BF16), (256, F32),
            (512, BF16), (512, BF16), (512, BF16)]
    return pl.pallas_call(
        _in_proj_kernel,
        grid=(T // tm,),
        in_specs=[row(D_MODEL), full(wm), full(wa), full(w2h), full(w2l), full(gb)],
        out_specs=[row(w) for w, _ in outs],
        out_shape=[jax.ShapeDtypeStruct((T, w), dt) for w, dt in outs],
        compiler_params=pltpu.CompilerParams(
            dimension_semantics=("parallel",), vmem_limit_bytes=VMEM_LIMIT),
        name="in_proj",
    )(x2, wm, wa, w2h, w2l, gb)


def _gla_kernel(q_ref, k_ref, v_ref, r_ref, la_ref, g_ref, o_ref, s_ref, *, seq_block):
    C = GLA_CHUNK

    @pl.when(pl.program_id(1) == 0)
    def _():
        s_ref[...] = jnp.zeros_like(s_ref)

    ri = lax.broadcasted_iota(jnp.int32, (C, C), 0)
    ci = lax.broadcasted_iota(jnp.int32, (C, C), 1)
    causal = ci <= ri
    tri = causal.astype(BF16)
    ones_cl = jnp.ones((C, LANES), BF16)
    lane_k = lax.broadcasted_iota(jnp.int32, (1, GLA_KEY_WIDTH), 1) // GLA_DK
    head_masks = [(lane_k == h).astype(F32) for h in range(GLA_HEADS)]
    srow = lax.broadcasted_iota(jnp.int32, (GLA_KEY_WIDTH, GLA_WIDTH), 0) // GLA_DK
    scol = lax.broadcasted_iota(jnp.int32, (GLA_KEY_WIDTH, GLA_WIDTH), 1) // GLA_DV
    state_mask = (srow == scol).astype(F32)
    g = g_ref[...]

    def chunk(c, carry):
        rows = pl.ds(pl.multiple_of(c * C, C), C)
        la = la_ref[rows, :]
        la_hi, la_lo = _split_bf16(la)
        b = _dot(tri, la_hi) + _dot(tri, la_lo)
        b_last = b[C - 1:C, :]
        q = q_ref[rows, :].astype(F32)
        k = k_ref[rows, :].astype(F32)
        v = v_ref[rows, :]
        qd = q * jnp.exp(b)
        kd = (k * jnp.exp(-b)).astype(BF16)
        ke = (k * jnp.exp(b_last - b)).astype(BF16)
        state = s_ref[...]
        o_inter = _dot(qd.astype(BF16), state.astype(BF16))
        outs = []
        for h in range(GLA_HEADS):
            a = _dot_nt((qd * head_masks[h]).astype(BF16), kd)
            a = jnp.where(causal, a, 0.0).astype(BF16)
            cols = slice(h * GLA_DV, (h + 1) * GLA_DV)
            o = _dot(a, v[:, cols]) + o_inter[:, cols]
            o = o * lax.rsqrt(jnp.mean(o * o, axis=-1, keepdims=True) + EPS) * g
            outs.append(o)
        o_all = jnp.concatenate(outs, axis=-1)
        rr = r_ref[rows, :].astype(F32)
        o_ref[rows, :] = (o_all * (rr * jax.nn.sigmoid(rr))).astype(BF16)
        tot = _dot_tn(la_hi, ones_cl) + _dot_tn(la_lo, ones_cl)
        dec = jnp.exp(tot)
        upd = _dot_tn(ke, v) * state_mask
        for h in range(GLA_HEADS):
            cols = slice(h * GLA_DV, (h + 1) * GLA_DV)
            s_ref[:, cols] = state[:, cols] * dec + upd[:, cols]
        return carry

    lax.fori_loop(0, seq_block // C, chunk, 0, unroll=GLA_UNROLL)


def _gla(q, k, v, r, la, g, batch, seq, seq_block=1024):
    nsb = seq // seq_block
    row = lambda w: pl.BlockSpec((seq_block, w), lambda b, s: (b * nsb + s, 0))
    return pl.pallas_call(
        functools.partial(_gla_kernel, seq_block=seq_block),
        grid=(batch, nsb),
        in_specs=[row(256), row(256), row(512), row(512), row(256),
                  pl.BlockSpec((1, GLA_DV), lambda b, s: (0, 0))],
        out_specs=row(512),
        out_shape=jax.ShapeDtypeStruct((batch * seq, GLA_WIDTH), BF16),
        scratch_shapes=[pltpu.VMEM((GLA_KEY_WIDTH, GLA_WIDTH), F32)],
        compiler_params=pltpu.CompilerParams(
            dimension_semantics=("parallel", "arbitrary"), vmem_limit_bytes=VMEM_LIMIT),
        name="gla",
    )(q, k, v, r, la, g)


def _dil_kernel(slope_ref, q_ref, k_ref, v_ref, g_ref, o_ref,
                qf, kf, vf, oc, lc, *, seq):
    B = DIL_BLOCK
    pair = pl.program_id(1)
    qf[...] = q_ref[...].astype(F32)
    kf[0:DIL_PAD, :] = jnp.zeros((DIL_PAD, LANES), F32)
    vf[0:DIL_PAD, :] = jnp.zeros((DIL_PAD, LANES), F32)
    kf[DIL_PAD:DIL_PAD + seq, :] = k_ref[...].astype(F32)
    vf[DIL_PAD:DIL_PAD + seq, :] = v_ref[...].astype(F32)

    lane = lax.broadcasted_iota(jnp.int32, (1, LANES), 1)
    first = lane < DIL_DH
    ii = lax.broadcasted_iota(jnp.int32, (B, B), 0)
    jj = lax.broadcasted_iota(jnp.int32, (B, B), 1)
    upper = jj > ii
    eye = jj == ii
    dist = jnp.bitwise_and(ii - jj, B - 1).astype(F32)
    neg = jnp.float32(-jnp.inf)

    for cfg, (window, r) in enumerate(DIL_CONFIGS):
        nb = seq // r // B
        slopes = [slope_ref[2 * pair + hh] * float(r) for hh in range(2)]
        biases = [dist * (-slopes[hh]) for hh in range(2)]

        def block(it, cfg=cfg, r=r, nb=nb, slopes=slopes, biases=biases):
            c = it // nb
            n = it % nb
            start = c + n * (B * r)
            has_prev = n > 0
            rows = pl.ds(start, B, stride=r) if r > 1 else pl.ds(start, B)
            kv0 = DIL_PAD + start - B * r
            rows_kv = pl.ds(kv0, 2 * B, stride=r) if r > 1 else pl.ds(kv0, 2 * B)
            q = qf[rows, :]
            kcat = kf[rows_kv, :].astype(BF16)
            vcat_f = vf[rows_kv, :]
            vcat = vcat_f.astype(BF16)
            v_far = vcat_f[0:B, :]
            o_pair = None
            l_pair = None
            for hh in range(2):
                hmask = first if hh == 0 else jnp.logical_not(first)
                qh = jnp.where(hmask, q, 0.0).astype(BF16)
                s2 = _dot_nt(qh, kcat)
                s_prev = s2[:, 0:B]
                s_cur = s2[:, B:2 * B]
                s = jnp.where(upper, jnp.where(has_prev, s_prev, neg), s_cur) + biases[hh]
                far = jnp.sum(jnp.where(eye, s_prev, 0.0), axis=-1, keepdims=True) - slopes[hh] * float(B)
                far = jnp.where(has_prev, far, neg)
                m = jnp.maximum(jnp.max(s, axis=-1, keepdims=True), far)
                p = jnp.exp(s - m)
                p_far = jnp.exp(far - m)
                z = jnp.sum(p, axis=-1, keepdims=True) + p_far
                pcat = jnp.concatenate([jnp.where(upper, p, 0.0), jnp.where(upper, 0.0, p)], axis=1)
                o = (_dot(pcat.astype(BF16), vcat) + p_far * v_far) * (1.0 / z)
                lse = jnp.broadcast_to(m + jnp.log(z), (B, LANES))
                if hh == 0:
                    o_pair, l_pair = o, lse
                else:
                    o_pair = jnp.where(first, o_pair, o)
                    l_pair = jnp.where(first, l_pair, lse)
            oc[cfg, rows, :] = o_pair
            lc[cfg, rows, :] = l_pair

        def body(t, carry, block=block):
            for u in range(DIL_UNROLL):
                block(t * DIL_UNROLL + u)
            return carry

        lax.fori_loop(0, seq // B // DIL_UNROLL, body, 0)

    g = g_ref[...]
    CH = 512

    def mix(i, carry):
        rows = pl.ds(pl.multiple_of(i * CH, CH), CH)
        l0, l1, l2 = lc[0, rows, :], lc[1, rows, :], lc[2, rows, :]
        m = jnp.maximum(jnp.maximum(l0, l1), l2)
        e0, e1, e2 = jnp.exp(l0 - m), jnp.exp(l1 - m), jnp.exp(l2 - m)
        den = e0 + e1 + e2
        o = (e0 / den) * oc[0, rows, :] + (e1 / den) * oc[1, rows, :] + (e2 / den) * oc[2, rows, :]
        sq = o * o
        ms_a = jnp.sum(jnp.where(first, sq, 0.0), axis=-1, keepdims=True) * (1.0 / DIL_DH)
        ms_b = jnp.sum(jnp.where(first, 0.0, sq), axis=-1, keepdims=True) * (1.0 / DIL_DH)
        ms = jnp.where(first, ms_a, ms_b)
        o_ref[rows, :] = (o * lax.rsqrt(ms + EPS) * g).astype(BF16)
        return carry

    lax.fori_loop(0, seq // CH, mix, 0)


def _dilated(slopes, dq, dk, dv, g2, batch, seq):
    blk = pl.BlockSpec((seq, LANES), lambda b, p, s: (b, p))
    return pl.pallas_call(
        functools.partial(_dil_kernel, seq=seq),
        grid_spec=pltpu.PrefetchScalarGridSpec(
            num_scalar_prefetch=1,
            grid=(batch, DIL_WIDTH // LANES),
            in_specs=[blk, blk, blk, pl.BlockSpec((1, LANES), lambda b, p, s: (0, 0))],
            out_specs=blk,
            scratch_shapes=[pltpu.VMEM((seq, LANES), F32)] + [pltpu.VMEM((seq + DIL_PAD, LANES), F32)] * 2
                           + [pltpu.VMEM((3, seq, LANES), F32)] * 2,
        ),
        out_shape=jax.ShapeDtypeStruct((batch * seq, DIL_WIDTH), BF16),
        compiler_params=pltpu.CompilerParams(
            dimension_semantics=("parallel", "parallel"), vmem_limit_bytes=VMEM_LIMIT),
        name="dilated",
    )(slopes, dq, dk, dv, g2)


def _out_proj_kernel(og_ref, od_ref, x_ref, wg_ref, wd_ref, g_ref, b_ref,
                     rwh_ref, rwl_ref, rb_ref, h_ref, eid_ref, gate_ref):
    mix = _dot(og_ref[...], wg_ref[...]) + _dot(od_ref[...], wd_ref[...])
    h = _layer_norm(DEEPNORM_ALPHA * x_ref[...] + mix, g_ref[...], b_ref[...])
    h_ref[...] = h
    h_hi, h_lo = _split_bf16(h)
    logits = (_dot(h_hi, rwh_ref[...]) + _dot(h_lo, rwh_ref[...]) + _dot(h_hi, rwl_ref[...])
              + rb_ref[...])
    lane = lax.broadcasted_iota(jnp.int32, logits.shape, 1)
    lane_f = lane.astype(F32)
    neg = jnp.float32(-jnp.inf)
    big = jnp.float32(1e9)
    is_coarse = jnp.logical_and(lane >= N_EXPERTS, lane < N_EXPERTS + N_GROUPS)
    coarse = jnp.where(is_coarse, logits, neg)
    cmax = jnp.max(coarse, axis=-1, keepdims=True)
    g_idx = jnp.min(jnp.where(coarse == cmax, lane_f, big), axis=-1, keepdims=True) - N_EXPERTS
    p_group = 1.0 / jnp.sum(jnp.exp(coarse - cmax), axis=-1, keepdims=True)
    lo = g_idx * EXPERTS_PER_GROUP
    in_group = jnp.logical_and(lane_f >= lo, lane_f < lo + EXPERTS_PER_GROUP)
    fine = jnp.where(in_group, logits, neg)
    v1 = jnp.max(fine, axis=-1, keepdims=True)
    i1 = jnp.min(jnp.where(fine == v1, lane_f, big), axis=-1, keepdims=True)
    fine2 = jnp.where(lane_f == i1, neg, fine)
    v2 = jnp.max(fine2, axis=-1, keepdims=True)
    i2 = jnp.min(jnp.where(fine2 == v2, lane_f, big), axis=-1, keepdims=True)
    e2 = jnp.exp(v2 - v1)
    den = 1.0 + e2
    gate1 = p_group * (1.0 / den)
    gate2 = p_group * (e2 / den)
    lane2 = lax.broadcasted_iota(jnp.int32, (logits.shape[0], 2), 1)
    eid_ref[...] = jnp.where(lane2 == 0, i1, i2).astype(jnp.int32)
    gate_ref[...] = jnp.where(lane2 == 0, gate1, gate2)


def _out_proj(og, od, x2, wg, wd, g, b, rwh, rwl, rb, tm=512):
    T = x2.shape[0]
    row = lambda w: pl.BlockSpec((tm, w), lambda i: (i, 0))
    full = lambda a: pl.BlockSpec(a.shape, lambda i: (0,) * a.ndim)
    return pl.pallas_call(
        _out_proj_kernel,
        grid=(T // tm,),
        in_specs=[row(512), row(512), row(D_MODEL), full(wg), full(wd), full(g), full(b),
                  full(rwh), full(rwl), full(rb)],
        out_specs=[row(D_MODEL), row(2), row(2)],
        out_shape=[jax.ShapeDtypeStruct((T, D_MODEL), F32),
                   jax.ShapeDtypeStruct((T, 2), jnp.int32),
                   jax.ShapeDtypeStruct((T, 2), F32)],
        compiler_params=pltpu.CompilerParams(
            dimension_semantics=("parallel",), vmem_limit_bytes=VMEM_LIMIT),
        name="out_proj_router",
    )(og, od, x2, wg, wd, g, b, rwh, rwl, rb)


def _positions_kernel(eid_ref, dest_ref, be_ref, nv_ref, carry_ref, cnt_col_ref, sp_ref, *, tb):
    phase = pl.program_id(0)
    i = pl.program_id(1)
    lane = lax.broadcasted_iota(jnp.int32, (tb, LANES), 1)
    eid = eid_ref[...]
    oh1 = lane == eid[:, 0:1]
    oh2 = lane == eid[:, 1:2]
    oh = jnp.logical_or(oh1, oh2).astype(BF16)

    @pl.when(jnp.logical_and(phase == 0, i == 0))
    def _():
        carry_ref[...] = jnp.zeros_like(carry_ref)
        cnt_col_ref[...] = jnp.zeros_like(cnt_col_ref)

    @pl.when(phase == 0)
    def _():
        ones_r = jnp.ones((8, tb), BF16)
        ones_c = jnp.ones((tb, LANES), BF16)
        carry_ref[...] += _dot(ones_r, oh)
        cnt_col_ref[...] += _dot_tn(oh, ones_c)

    @pl.when(jnp.logical_and(phase == 1, i == 0))
    def _():
        shift = int(math.log2(ROW_BLOCK))
        nb_row = ((carry_ref[...].astype(jnp.int32) + (ROW_BLOCK - 1)) >> shift)
        nb_col = ((cnt_col_ref[...].astype(jnp.int32) + (ROW_BLOCK - 1)) >> shift)
        r = lax.broadcasted_iota(jnp.int32, (LANES, LANES), 0)
        c = lax.broadcasted_iota(jnp.int32, (LANES, LANES), 1)
        excl = (r < c).astype(BF16)
        incl = (c <= r).astype(BF16)
        start_blk = _dot(nb_row.astype(F32).astype(BF16), excl)
        sp_ref[...] = start_blk * float(ROW_BLOCK)
        end_col = _dot(incl, nb_col.astype(F32).astype(BF16))
        end2 = jnp.concatenate([end_col, end_col], axis=1)
        nidx = lax.broadcasted_iota(jnp.int32, (LANES, 2 * LANES), 1).astype(F32)
        eidx = lax.broadcasted_iota(jnp.int32, (LANES, 2 * LANES), 0)
        hit = jnp.logical_and(end2 <= nidx, eidx < N_EXPERTS).astype(F32)
        be = jnp.minimum(jnp.sum(hit, axis=0, keepdims=True), float(N_EXPERTS - 1))
        be_ref[...] = be.astype(jnp.int32)
        lane1 = lax.broadcasted_iota(jnp.int32, (1, LANES), 1)
        total = jnp.sum(jnp.where(lane1 < N_EXPERTS, nb_row[0:1, :].astype(F32), 0.0), axis=-1, keepdims=True)
        nv_ref[...] = jnp.broadcast_to(total, (1, LANES)).astype(jnp.int32)
        carry_ref[...] = jnp.zeros_like(carry_ref)

    @pl.when(phase == 1)
    def _():
        r = lax.broadcasted_iota(jnp.int32, (tb, tb), 0)
        c = lax.broadcasted_iota(jnp.int32, (tb, tb), 1)
        strict = (c < r).astype(BF16)
        before = _dot(strict, oh) + carry_ref[0:1, :] + sp_ref[0:1, :]
        d1 = jnp.sum(jnp.where(oh1, before, 0.0), axis=-1, keepdims=True)
        d2 = jnp.sum(jnp.where(oh2, before, 0.0), axis=-1, keepdims=True)
        lane2 = lax.broadcasted_iota(jnp.int32, (tb, 2), 1)
        dest_ref[...] = jnp.where(lane2 == 0, d1, d2).astype(jnp.int32)
        carry_ref[...] += _dot(jnp.ones((8, tb), BF16), oh)


def _positions(eid, n_blocks_pad, tb=512):
    T = eid.shape[0]
    nb = T // tb
    return pl.pallas_call(
        functools.partial(_positions_kernel, tb=tb),
        grid=(2, nb),
        in_specs=[pl.BlockSpec((tb, 2), lambda p, i: (i, 0))],
        out_specs=[pl.BlockSpec((tb, 2), lambda p, i: (i * p, 0)),
                   pl.BlockSpec((1, n_blocks_pad), lambda p, i: (0, 0)),
                   pl.BlockSpec((1, LANES), lambda p, i: (0, 0))],
        out_shape=[jax.ShapeDtypeStruct((T, 2), jnp.int32),
                   jax.ShapeDtypeStruct((1, n_blocks_pad), jnp.int32),
                   jax.ShapeDtypeStruct((1, LANES), jnp.int32)],
        scratch_shapes=[pltpu.VMEM((8, LANES), F32), pltpu.VMEM((LANES, LANES), F32),
                        pltpu.VMEM((8, LANES), F32)],
        compiler_params=pltpu.CompilerParams(dimension_semantics=("arbitrary", "arbitrary")),
        name="positions",
    )(eid)


def _sc_gather_rows(table, idx):
    n = idx.shape[0]
    d = table.shape[1]
    info = plsc.get_sparse_core_info()
    nc, ns = info.num_cores, info.num_subcores
    per_w = n // (nc * ns)
    assert per_w * nc * ns == n and per_w % SC_INDEX_WINDOW == 0
    mesh = plsc.VectorSubcoreMesh(core_axis_name="core", subcore_axis_name="subcore")
    nsub = SC_INDEX_WINDOW // SC_GATHER_ROWS

    @functools.partial(
        pl.kernel, out_type=jax.ShapeDtypeStruct((n, d), table.dtype), mesh=mesh,
        scratch_types=[pltpu.VMEM((SC_INDEX_WINDOW,), jnp.int32),
                       pltpu.VMEM((2, SC_GATHER_ROWS, d), table.dtype),
                       pltpu.SemaphoreType.DMA((2,)), pltpu.SemaphoreType.DMA((2,))],
        name="sc_gather_rows")
    def gather(x_hbm, i_hbm, o_hbm, idx_v, buf, gsem, wsem):
        wid = lax.axis_index("subcore") * nc + lax.axis_index("core")
        base = wid * per_w

        def gather_copy(s):
            rows = idx_v.at[pl.ds(s * SC_GATHER_ROWS, SC_GATHER_ROWS)]
            return pltpu.make_async_copy(x_hbm.at[rows], buf.at[s % 2], gsem.at[s % 2])

        def write_copy(off, s):
            dst = o_hbm.at[pl.ds(off + s * SC_GATHER_ROWS, SC_GATHER_ROWS)]
            return pltpu.make_async_copy(buf.at[s % 2], dst, wsem.at[s % 2])

        @pl.loop(0, per_w // SC_INDEX_WINDOW)
        def _(j):
            off = base + j * SC_INDEX_WINDOW
            pltpu.sync_copy(i_hbm.at[pl.ds(off, SC_INDEX_WINDOW)], idx_v)
            gather_copy(0).start()
            for s in range(nsub):
                gather_copy(s).wait()
                write_copy(off, s).start()
                if s >= 1:
                    write_copy(off, s - 1).wait()
                if s + 1 < nsub:
                    gather_copy(s + 1).start()
            write_copy(off, nsub - 1).wait()

    return gather(table, idx)


def _sc_inverse_rows(dest_flat, n_rows, chunk=2048):
    n = dest_flat.shape[0]
    n_tokens = n // 2
    assert n_rows <= 3 * n_tokens
    nc = plsc.get_sparse_core_info().num_cores
    mesh = plsc.VectorSubcoreMesh(core_axis_name="core", subcore_axis_name="subcore")

    @functools.partial(
        pl.kernel, out_type=jax.ShapeDtypeStruct((n_rows,), jnp.int32), mesh=mesh,
        scratch_types=[pltpu.VMEM((n_rows,), jnp.int32), pltpu.VMEM((chunk,), jnp.int32)],
        compiler_params=pltpu.CompilerParams(needs_layout_passes=False),
        name="sc_inverse_rows")
    def inverse(d_hbm, o_hbm, inv_v, d_v):
        wid = lax.axis_index("subcore") * nc + lax.axis_index("core")

        @pl.when(wid == 0)
        def _():
            lanes = lax.iota(jnp.int32, SC_LANES)

            @pl.loop(0, n_rows // SC_LANES)
            def _(i):
                r = lanes + i * SC_LANES
                r = jnp.where(r >= n_tokens, r - n_tokens, r)
                inv_v[pl.ds(i * SC_LANES, SC_LANES)] = jnp.where(r >= n_tokens, r - n_tokens, r)

            @pl.loop(0, n // chunk)
            def _(c):
                pltpu.sync_copy(d_hbm.at[pl.ds(c * chunk, chunk)], d_v)

                @pl.loop(0, chunk // SC_LANES)
                def _(j):
                    rows = d_v[pl.ds(j * SC_LANES, SC_LANES)]
                    pair = lax.iota(jnp.int32, SC_LANES) + (c * chunk + j * SC_LANES)
                    plsc.store_scatter(inv_v, [rows], lax.shift_right_logical(pair, 1))

            pltpu.sync_copy(inv_v, o_hbm)

    return inverse(dest_flat)


def _ffn_kernel(be_ref, nv_ref, xs_ref, wg_ref, wu_ref, wd_ref, y_ref, wgb, wub, wdb):
    n = pl.program_id(0)
    prev = be_ref[jnp.maximum(n - 1, 0)]
    fresh = jnp.logical_or(n == 0, be_ref[n] != prev)

    @pl.when(jnp.logical_and(fresh, n < nv_ref[0]))
    def _():
        wgb[...] = wg_ref[0].astype(BF16)
        wub[...] = wu_ref[0].astype(BF16)
        wdb[...] = wd_ref[0].astype(BF16)

    @pl.when(n < nv_ref[0])
    def _():
        xb = xs_ref[...].astype(BF16)
        a = _dot(xb, wgb[...])
        u = _dot(xb, wub[...])
        hid = (a * jax.nn.sigmoid(a) * u).astype(BF16)
        y_ref[...] = _dot(hid, wdb[...])

    @pl.when(n >= nv_ref[0])
    def _():
        y_ref[...] = jnp.zeros_like(y_ref)


def _ffn(be, nv, xs, w_gate, w_up, w_down):
    n_rows = xs.shape[0]
    n_blocks = n_rows // ROW_BLOCK

    def blk(n, be, nv):
        return (jnp.minimum(n, nv[0] - 1), 0)

    def wsel(n, be, nv):
        return (be[jnp.minimum(n, nv[0] - 1)], 0, 0)

    return pl.pallas_call(
        _ffn_kernel,
        grid_spec=pltpu.PrefetchScalarGridSpec(
            num_scalar_prefetch=2,
            grid=(n_blocks,),
            in_specs=[pl.BlockSpec((ROW_BLOCK, D_MODEL), blk),
                      pl.BlockSpec((1, D_MODEL, D_FF), wsel),
                      pl.BlockSpec((1, D_MODEL, D_FF), wsel),
                      pl.BlockSpec((1, D_FF, D_MODEL), wsel)],
            out_specs=pl.BlockSpec((ROW_BLOCK, D_MODEL), lambda n, be, nv: (n, 0)),
            scratch_shapes=[pltpu.VMEM((D_MODEL, D_FF), BF16), pltpu.VMEM((D_MODEL, D_FF), BF16),
                            pltpu.VMEM((D_FF, D_MODEL), BF16)],
        ),
        out_shape=jax.ShapeDtypeStruct((n_rows, D_MODEL), F32),
        compiler_params=pltpu.CompilerParams(
            dimension_semantics=("arbitrary",), vmem_limit_bytes=VMEM_LIMIT),
        name="expert_ffn",
    )(be, nv, xs, w_gate, w_up, w_down)


def _combine_kernel(h_ref, ya_ref, yb_ref, gate_ref, g_ref, b_ref, o_ref):
    gate = gate_ref[...]
    ffn = ya_ref[...] * gate[:, 0:1] + yb_ref[...] * gate[:, 1:2]
    o_ref[...] = _layer_norm(DEEPNORM_ALPHA * h_ref[...] + ffn, g_ref[...], b_ref[...])


def _combine(h, y2, gate, g, b, tm=512):
    T = h.shape[0]
    nt = T // tm
    return pl.pallas_call(
        _combine_kernel,
        grid=(nt,),
        in_specs=[pl.BlockSpec((tm, D_MODEL), lambda i: (i, 0)),
                  pl.BlockSpec((tm, D_MODEL), lambda i: (i, 0)),
                  pl.BlockSpec((tm, D_MODEL), lambda i: (i + nt, 0)),
                  pl.BlockSpec((tm, 2), lambda i: (i, 0)),
                  pl.BlockSpec((1, D_MODEL), lambda i: (0, 0)),
                  pl.BlockSpec((1, D_MODEL), lambda i: (0, 0))],
        out_specs=pl.BlockSpec((tm, D_MODEL), lambda i: (i, 0)),
        out_shape=jax.ShapeDtypeStruct((T, D_MODEL), F32),
        compiler_params=pltpu.CompilerParams(
            dimension_semantics=("parallel",), vmem_limit_bytes=VMEM_LIMIT),
        name="combine",
    )(h, y2, y2, gate, g, b)


def kernel(x, w_in, gla_gate_w2, gla_gate_b, gla_norm_g, dil_norm_g, w_out, ln1_g, ln1_b,
           router_coarse_w, router_coarse_b, router_fine_w, router_fine_b,
           expert_w_gate, expert_w_up, expert_w_down, ln2_g, ln2_b):
    B, S, D = x.shape
    T = B * S
    depth = w_in.shape[0]
    slopes = jnp.exp2(-8.0 * jnp.arange(1, DIL_HEADS + 1, dtype=F32) / DIL_HEADS)
    n_rows = 2 * T + N_EXPERTS * ROW_BLOCK
    n_blocks = n_rows // ROW_BLOCK
    n_blocks_pad = -(-n_blocks // (2 * LANES)) * (2 * LANES)
    assert n_blocks_pad == 2 * LANES
    a0 = 1536
    h = x.reshape(T, D)
    for l in range(depth):
        w = w_in[l]
        wm = jnp.concatenate([w[:, :a0], w[:, a0 + GLA_GATE_RANK:]], axis=1).astype(BF16)
        wa = jnp.pad(w[:, a0:a0 + GLA_GATE_RANK], ((0, 0), (0, LANES - GLA_GATE_RANK))).astype(BF16)
        w2 = jnp.pad(gla_gate_w2[l], ((0, LANES - GLA_GATE_RANK), (0, 0)))
        w2h, w2l = _split_bf16(w2)
        q, k, v, r, la, dq, dk, dv = _in_proj(h, wm, wa, w2h, w2l, gla_gate_b[l][None, :])
        o_gla = _gla(q, k, v, r, la, gla_norm_g[l][None, :], B, S)
        g2 = jnp.tile(dil_norm_g[l], 2)[None, :]
        o_dil = _dilated(slopes, dq, dk, dv, g2, B, S)
        wo = w_out[l].astype(BF16)
        rw = jnp.concatenate([router_fine_w[l].reshape(D, N_EXPERTS), router_coarse_w[l]], axis=1)
        rw = jnp.pad(rw, ((0, 0), (0, LANES - N_EXPERTS - N_GROUPS)))
        rwh, rwl = _split_bf16(rw)
        rb = jnp.concatenate([router_fine_b[l].reshape(N_EXPERTS), router_coarse_b[l]])
        rb = jnp.pad(rb, (0, LANES - N_EXPERTS - N_GROUPS))[None, :]
        h1, eid, gate = _out_proj(o_gla, o_dil, h, wo[:GLA_WIDTH], wo[GLA_WIDTH:],
                                  ln1_g[l][None, :], ln1_b[l][None, :], rwh, rwl, rb)
        dest, be, nv = _positions(eid, n_blocks_pad)
        src_tok = _sc_inverse_rows(dest.reshape(2 * T), n_rows)
        xs = _sc_gather_rows(h1, src_tok)
        y = _ffn(be.reshape(n_blocks_pad), nv.reshape(LANES)[:1], xs,
                 expert_w_gate[l], expert_w_up[l], expert_w_down[l])
        y2 = _sc_gather_rows(y, dest.T.reshape(2 * T))
        h = _combine(h1, y2, gate, ln2_g[l][None, :], ln2_b[l][None, :])
    return h.reshape(B, S, D)
```

```python
import functools
import math

import jax
import jax.numpy as jnp
import numpy as np
from jax import lax
from jax.experimental import pallas as pl
from jax.experimental.pallas import tpu as pltpu
from jax.experimental.pallas import tpu_sc as plsc

D_MODEL = 1024
GLA_HEADS = 4
GLA_DK = 64
GLA_DV = 128
GLA_KEY_WIDTH = GLA_HEADS * GLA_DK
GLA_WIDTH = GLA_HEADS * GLA_DV
GLA_GATE_RANK = 16
GLA_GATE_TEMP = 16.0
DIL_HEADS = 8
DIL_DH = 64
DIL_WIDTH = DIL_HEADS * DIL_DH
DIL_CONFIGS = ((128, 1), (512, 4), (2048, 16))
DIL_BLOCK = 128
DIL_MAX_R = max(r for _, r in DIL_CONFIGS)
DIL_PAD = DIL_BLOCK * DIL_MAX_R
DIL_UNROLL = 2
N_GROUPS = 4
EXPERTS_PER_GROUP = 8
N_EXPERTS = N_GROUPS * EXPERTS_PER_GROUP
D_FF = 512
DEEPNORM_ALPHA = 2.0 ** 0.25
EPS = 1e-5

LANES = 128
GLA_CHUNK = 64
GLA_UNROLL = 4
SC_LANES = 16
SC_INDEX_WINDOW = 128
SC_GATHER_ROWS = 64
ROW_BLOCK = 256
VMEM_LIMIT = 56 * 1024 * 1024

F32 = jnp.float32
BF16 = jnp.bfloat16


def _dot(a, b):
    return jnp.dot(a, b, preferred_element_type=F32)


def _dot_nt(a, b):
    return lax.dot_general(a, b, (((1,), (1,)), ((), ())), preferred_element_type=F32)


def _dot_tn(a, b):
    return lax.dot_general(a, b, (((0,), (0,)), ((), ())), preferred_element_type=F32)


def _split_bf16(v):
    hi = v.astype(BF16)
    lo = (v - hi.astype(F32)).astype(BF16)
    return hi, lo


def _pack_bf16_halves(v):
    w = v.shape[1] // 2
    hi = lax.bitcast_convert_type(v[:, :w].astype(BF16).astype(F32), jnp.int32)
    lo = lax.bitcast_convert_type(v[:, w:].astype(BF16).astype(F32), jnp.int32)
    return hi | lax.shift_right_logical(lo, 16)


def _unpack_bf16_halves(words):
    hi = lax.bitcast_convert_type(words & jnp.int32(-65536), F32)
    lo = lax.bitcast_convert_type(lax.shift_left(words, 16), F32)
    return hi, lo


def _layer_norm(v, g, b):
    mu = jnp.mean(v, axis=-1, keepdims=True)
    c = v - mu
    var = jnp.mean(c * c, axis=-1, keepdims=True)
    return c * lax.rsqrt(var + EPS) * g + b


def _in_proj_kernel(x_ref, wm_ref, wa_ref, w2h_ref, w2l_ref, gb_ref,
                    q_ref, k_ref, v_ref, r_ref, la_ref, dq_ref, dk_ref, dv_ref):
    xb = x_ref[...].astype(BF16)

    def piece(c0, c1):
        return _dot(xb, wm_ref[:, c0:c1])

    q_ref[...] = (piece(0, 256) * (GLA_DK ** -0.5)).astype(BF16)
    k_ref[...] = piece(256, 512).astype(BF16)
    v_ref[...] = piece(512, 1024).astype(BF16)
    r_ref[...] = piece(1024, 1536).astype(BF16)
    dq_ref[...] = (piece(1536, 2048) * (DIL_DH ** -0.5)).astype(BF16)
    dk_ref[...] = piece(2048, 2560).astype(BF16)
    dv_ref[...] = piece(2560, 3072).astype(BF16)
    ga = _dot(xb, wa_ref[...])
    ga_hi, ga_lo = _split_bf16(ga)
    z = _dot(ga_hi, w2h_ref[...]) + _dot(ga_lo, w2h_ref[...]) + _dot(ga_hi, w2l_ref[...]) + gb_ref[...]
    log_sig = jnp.minimum(z, 0.0) - jnp.log1p(jnp.exp(-jnp.abs(z)))
    la_ref[...] = log_sig * (1.0 / GLA_GATE_TEMP)


def _in_proj(x2, wm, wa, w2h, w2l, gb, tm=512):
    T = x2.shape[0]
    row = lambda w: pl.BlockSpec((tm, w), lambda i: (i, 0))
    full = lambda a: pl.BlockSpec(a.shape, lambda i: (0,) * a.ndim)
    outs = [(256, BF16), (256, BF16), (512, BF16), (512, BF16), (256, F32),
            (512, BF16), (512, BF16), (512, BF16)]
    return pl.pallas_call(
        _in_proj_kernel,
        grid=(T // tm,),
        in_specs=[row(D_MODEL), full(wm), full(wa), full(w2h), full(w2l), full(gb)],
        out_specs=[row(w) for w, _ in outs],
        out_shape=[jax.ShapeDtypeStruct((T, w), dt) for w, dt in outs],
        compiler_params=pltpu.CompilerParams(
            dimension_semantics=("parallel",), vmem_limit_bytes=VMEM_LIMIT),
        name="in_proj",
    )(x2, wm, wa, w2h, w2l, gb)


def _gla_kernel(q_ref, k_ref, v_ref, r_ref, la_ref, g_ref, o_ref, s_ref, *, seq_block):
    C = GLA_CHUNK

    @pl.when(pl.program_id(1) == 0)
    def _():
        s_ref[...] = jnp.zeros_like(s_ref)

    ri = lax.broadcasted_iota(jnp.int32, (C, C), 0)
    ci = lax.broadcasted_iota(jnp.int32, (C, C), 1)
    causal = ci <= ri
    tri = causal.astype(BF16)
    ones_cl = jnp.ones((C, LANES), BF16)
    lane_k = lax.broadcasted_iota(jnp.int32, (1, GLA_KEY_WIDTH), 1) // GLA_DK
    head_masks = [(lane_k == h).astype(F32) for h in range(GLA_HEADS)]
    srow = lax.broadcasted_iota(jnp.int32, (GLA_KEY_WIDTH, GLA_WIDTH), 0) // GLA_DK
    scol = lax.broadcasted_iota(jnp.int32, (GLA_KEY_WIDTH, GLA_WIDTH), 1) // GLA_DV
    state_mask = (srow == scol).astype(F32)
    g = g_ref[...]

    def chunk(c, carry):
        rows = pl.ds(pl.multiple_of(c * C, C), C)
        la = la_ref[rows, :]
        la_hi, la_lo = _split_bf16(la)
        b = _dot(tri, la_hi) + _dot(tri, la_lo)
        b_last = b[C - 1:C, :]
        q = q_ref[rows, :].astype(F32)
        k = k_ref[rows, :].astype(F32)
        v = v_ref[rows, :]
        qd = q * jnp.exp(b)
        kd = (k * jnp.exp(-b)).astype(BF16)
        ke = (k * jnp.exp(b_last - b)).astype(BF16)
        state = s_ref[...]
        o_inter = _dot(qd.astype(BF16), state.astype(BF16))
        outs = []
        for h in range(GLA_HEADS):
            a = _dot_nt((qd * head_masks[h]).astype(BF16), kd)
            a = jnp.where(causal, a, 0.0).astype(BF16)
            cols = slice(h * GLA_DV, (h + 1) * GLA_DV)
            o = _dot(a, v[:, cols]) + o_inter[:, cols]
            o = o * lax.rsqrt(jnp.mean(o * o, axis=-1, keepdims=True) + EPS) * g
            outs.append(o)
        o_all = jnp.concatenate(outs, axis=-1)
        rr = r_ref[rows, :].astype(F32)
        o_ref[rows, :] = (o_all * (rr * jax.nn.sigmoid(rr))).astype(BF16)
        tot = _dot_tn(la_hi, ones_cl) + _dot_tn(la_lo, ones_cl)
        dec = jnp.exp(tot)
        upd = _dot_tn(ke, v) * state_mask
        for h in range(GLA_HEADS):
            cols = slice(h * GLA_DV, (h + 1) * GLA_DV)
            s_ref[:, cols] = state[:, cols] * dec + upd[:, cols]
        return carry

    lax.fori_loop(0, seq_block // C, chunk, 0, unroll=GLA_UNROLL)


def _gla(q, k, v, r, la, g, batch, seq, seq_block=1024):
    nsb = seq // seq_block
    row = lambda w: pl.BlockSpec((seq_block, w), lambda b, s: (b * nsb + s, 0))
    return pl.pallas_call(
        functools.partial(_gla_kernel, seq_block=seq_block),
        grid=(batch, nsb),
        in_specs=[row(256), row(256), row(512), row(512), row(256),
                  pl.BlockSpec((1, GLA_DV), lambda b, s: (0, 0))],
        out_specs=row(512),
        out_shape=jax.ShapeDtypeStruct((batch * seq, GLA_WIDTH), BF16),
        scratch_shapes=[pltpu.VMEM((GLA_KEY_WIDTH, GLA_WIDTH), F32)],
        compiler_params=pltpu.CompilerParams(
            dimension_semantics=("parallel", "arbitrary"), vmem_limit_bytes=VMEM_LIMIT),
        name="gla",
    )(q, k, v, r, la, g)


def _dil_kernel(slope_ref, q_ref, k_ref, v_ref, g_ref, o_ref,
                qf, kf, vf, kd, vt, sbuf, pbuf, stat, oc, lc, *, seq):
    B = DIL_BLOCK
    U = DIL_UNROLL
    pair = pl.program_id(1)
    qf[...] = q_ref[...].astype(F32)
    kf[...] = k_ref[...].astype(F32)
    vf[...] = v_ref[...].astype(F32)

    lane = lax.broadcasted_iota(jnp.int32, (1, LANES), 1)
    first = lane < DIL_DH
    kk = lax.broadcasted_iota(jnp.int32, (B, B), 0)
    qq = lax.broadcasted_iota(jnp.int32, (B, B), 1)
    upper = kk > qq
    eye = kk == qq
    dist = jnp.bitwise_and(qq - kk, B - 1).astype(F32)
    neg = jnp.float32(-jnp.inf)
    zero_tile = jnp.zeros((B, LANES), BF16)

    for cfg, (window, r) in enumerate(DIL_CONFIGS):
        nb = seq // r // B
        cs = nb + 1
        slopes = [slope_ref[2 * pair + hh] * float(r) for hh in range(2)]
        biases = [dist * (-slopes[hh]) for hh in range(2)]

        for c in range(r):
            kd[c * cs * B:(c * cs + 1) * B, :] = zero_tile
            vt[c * cs] = zero_tile

        def prep(t4, carry, r=r, nb=nb, cs=cs):
            for j in range(4):
                t = t4 * 4 + j
                c = t // nb
                n = t % nb
                start = c + n * (B * r)
                rows = pl.ds(start, B, stride=r) if r > 1 else pl.ds(pl.multiple_of(start, B), B)
                tile = c * cs + 1 + n
                kd[pl.ds(pl.multiple_of(tile * B, B), B), :] = kf[rows, :].astype(BF16)
                vt[tile] = vf[rows, :].T.astype(BF16)
            return carry

        lax.fori_loop(0, seq // B // 4, prep, 0)

        def geom(it, r=r, nb=nb, cs=cs):
            c = it // nb
            n = it % nb
            start = c + n * (B * r)
            rows = pl.ds(start, B, stride=r) if r > 1 else pl.ds(pl.multiple_of(start, B), B)
            return rows, c * cs + 1 + n

        def has_prev_of(g, u, nb=nb):
            if nb % U == 0:
                return True if u > 0 else (g * U) % nb > 0
            assert U % nb == 0
            return (u % nb) > 0

        def stage_scores(g, slot):
            for u in range(U):
                rows, tile = geom(g * U + u)
                q = qf[rows, :]
                kcat = kd[pl.ds(pl.multiple_of((tile - 1) * B, B), 2 * B), :]
                for hh in range(2):
                    hmask = first if hh == 0 else jnp.logical_not(first)
                    qh = jnp.where(hmask, q, 0.0).astype(BF16)
                    sbuf[slot, 2 * u + hh] = _dot_nt(kcat, qh)

        def stage_softmax(g, slot, slopes=slopes, biases=biases):
            for u in range(U):
                has_prev = has_prev_of(g, u)
                for hh in range(2):
                    s_prev = sbuf[slot, 2 * u + hh, 0:B, :]
                    s_cur = sbuf[slot, 2 * u + hh, B:2 * B, :]
                    if has_prev is False:
                        s = jnp.where(upper, neg, s_cur + biases[hh])
                        far = jnp.full((1, LANES), neg, F32)
                    else:
                        if has_prev is not True:
                            s_prev = jnp.where(has_prev, s_prev, neg)
                        s = jnp.where(upper, s_prev, s_cur) + biases[hh]
                        far = jnp.sum(jnp.where(eye, s_prev, 0.0), axis=0, keepdims=True) - slopes[hh] * float(B)
                    m = jnp.maximum(jnp.max(s, axis=0, keepdims=True), far)
                    p = jnp.exp(s - m)
                    p_far = jnp.exp(far - m)
                    z = jnp.sum(p, axis=0, keepdims=True) + p_far
                    top = jnp.where(upper, p, jnp.where(eye, p_far, 0.0))
                    bot = jnp.where(upper, 0.0, p)
                    pbuf[slot, 2 * u + hh] = jnp.concatenate([top, bot], axis=0).astype(BF16)
                    stat[slot, 2 * u + hh, 0:1, :] = 1.0 / z
                    stat[slot, 2 * u + hh, 1:2, :] = m + jnp.log(z)

        def stage_values(g, slot, cfg=cfg):
            for u in range(U):
                rows, tile = geom(g * U + u)
                vcat = jnp.concatenate([vt[tile - 1], vt[tile]], axis=1)
                o_heads = []
                l_heads = []
                for hh in range(2):
                    o_t = _dot(vcat, pbuf[slot, 2 * u + hh])
                    half = slice(hh * DIL_DH, (hh + 1) * DIL_DH)
                    o_heads.append(o_t[half, :] * stat[slot, 2 * u + hh, 0:1, :])
                    l_heads.append(jnp.broadcast_to(stat[slot, 2 * u + hh, 1:2, :], (DIL_DH, LANES)))
                oc[cfg, rows, :] = jnp.concatenate(o_heads, axis=0).T
                lc[cfg, rows, :] = jnp.concatenate(l_heads, axis=0).T

        ng = seq // B // U
        sbuf[...] = jnp.zeros_like(sbuf)
        pbuf[...] = jnp.zeros_like(pbuf)
        stat[...] = jnp.zeros_like(stat)

        def trip(t, carry, ng=ng, stage_scores=stage_scores, stage_softmax=stage_softmax, stage_values=stage_values):
            g3 = jnp.clip(t - 2, 0, ng - 1)
            g2 = jnp.clip(t - 1, 0, ng - 1)
            g1 = jnp.minimum(t, ng - 1)
            stage_values(g3, g3 % 2)
            stage_softmax(g2, g2 % 2)
            stage_scores(g1, g1 % 2)
            return carry

        lax.fori_loop(0, ng + 2, trip, 0)

    g = g_ref[...]
    CH = 512

    def mix(i, carry):
        rows = pl.ds(pl.multiple_of(i * CH, CH), CH)
        l0, l1, l2 = lc[0, rows, :], lc[1, rows, :], lc[2, rows, :]
        m = jnp.maximum(jnp.maximum(l0, l1), l2)
        e0, e1, e2 = jnp.exp(l0 - m), jnp.exp(l1 - m), jnp.exp(l2 - m)
        den = e0 + e1 + e2
        o = (e0 / den) * oc[0, rows, :] + (e1 / den) * oc[1, rows, :] + (e2 / den) * oc[2, rows, :]
        sq = o * o
        ms_a = jnp.sum(jnp.where(first, sq, 0.0), axis=-1, keepdims=True) * (1.0 / DIL_DH)
        ms_b = jnp.sum(jnp.where(first, 0.0, sq), axis=-1, keepdims=True) * (1.0 / DIL_DH)
        ms = jnp.where(first, ms_a, ms_b)
        o_ref[rows, :] = (o * lax.rsqrt(ms + EPS) * g).astype(BF16)
        return carry

    lax.fori_loop(0, seq // CH, mix, 0)


def _dilated(slopes, dq, dk, dv, g2, batch, seq):
    blk = pl.BlockSpec((seq, LANES), lambda b, p, s: (b, p))
    return pl.pallas_call(
        functools.partial(_dil_kernel, seq=seq),
        grid_spec=pltpu.PrefetchScalarGridSpec(
            num_scalar_prefetch=1,
            grid=(batch, DIL_WIDTH // LANES),
            in_specs=[blk, blk, blk, pl.BlockSpec((1, LANES), lambda b, p, s: (0, 0))],
            out_specs=blk,
            scratch_shapes=[pltpu.VMEM((seq, LANES), F32)] * 3
                           + [pltpu.VMEM((seq + DIL_PAD, LANES), BF16),
                              pltpu.VMEM((seq // DIL_BLOCK + DIL_MAX_R, LANES, DIL_BLOCK), BF16),
                              pltpu.VMEM((2, 2 * DIL_UNROLL, 2 * DIL_BLOCK, LANES), F32),
                              pltpu.VMEM((2, 2 * DIL_UNROLL, 2 * DIL_BLOCK, LANES), BF16),
                              pltpu.VMEM((2, 2 * DIL_UNROLL, 8, LANES), F32)]
                           + [pltpu.VMEM((3, seq, LANES), F32)] * 2,
        ),
        out_shape=jax.ShapeDtypeStruct((batch * seq, DIL_WIDTH), BF16),
        compiler_params=pltpu.CompilerParams(
            dimension_semantics=("parallel", "parallel"), vmem_limit_bytes=VMEM_LIMIT),
        name="dilated",
    )(slopes, dq, dk, dv, g2)


def _out_proj_kernel(og_ref, od_ref, x_ref, wg_ref, wd_ref, g_ref, b_ref,
                     rwh_ref, rwl_ref, rb_ref, h_ref, hp_ref, eid_ref, gate_ref):
    mix = _dot(og_ref[...], wg_ref[...]) + _dot(od_ref[...], wd_ref[...])
    h = _layer_norm(DEEPNORM_ALPHA * x_ref[...] + mix, g_ref[...], b_ref[...])
    h_ref[...] = h
    hp_ref[...] = _pack_bf16_halves(h)
    h_hi, h_lo = _split_bf16(h)
    logits = (_dot(h_hi, rwh_ref[...]) + _dot(h_lo, rwh_ref[...]) + _dot(h_hi, rwl_ref[...])
              + rb_ref[...])
    lane = lax.broadcasted_iota(jnp.int32, logits.shape, 1)
    lane_f = lane.astype(F32)
    neg = jnp.float32(-jnp.inf)
    big = jnp.float32(1e9)
    is_coarse = jnp.logical_and(lane >= N_EXPERTS, lane < N_EXPERTS + N_GROUPS)
    coarse = jnp.where(is_coarse, logits, neg)
    cmax = jnp.max(coarse, axis=-1, keepdims=True)
    g_idx = jnp.min(jnp.where(coarse == cmax, lane_f, big), axis=-1, keepdims=True) - N_EXPERTS
    p_group = 1.0 / jnp.sum(jnp.exp(coarse - cmax), axis=-1, keepdims=True)
    lo = g_idx * EXPERTS_PER_GROUP
    in_group = jnp.logical_and(lane_f >= lo, lane_f < lo + EXPERTS_PER_GROUP)
    fine = jnp.where(in_group, logits, neg)
    v1 = jnp.max(fine, axis=-1, keepdims=True)
    i1 = jnp.min(jnp.where(fine == v1, lane_f, big), axis=-1, keepdims=True)
    fine2 = jnp.where(lane_f == i1, neg, fine)
    v2 = jnp.max(fine2, axis=-1, keepdims=True)
    i2 = jnp.min(jnp.where(fine2 == v2, lane_f, big), axis=-1, keepdims=True)
    e2 = jnp.exp(v2 - v1)
    den = 1.0 + e2
    gate1 = p_group * (1.0 / den)
    gate2 = p_group * (e2 / den)
    lane2 = lax.broadcasted_iota(jnp.int32, (logits.shape[0], 2), 1)
    eid_ref[...] = jnp.where(lane2 == 0, i1, i2).astype(jnp.int32)
    gate_ref[...] = jnp.where(lane2 == 0, gate1, gate2)


def _out_proj(og, od, x2, wg, wd, g, b, rwh, rwl, rb, tm=512):
    T = x2.shape[0]
    row = lambda w: pl.BlockSpec((tm, w), lambda i: (i, 0))
    full = lambda a: pl.BlockSpec(a.shape, lambda i: (0,) * a.ndim)
    return pl.pallas_call(
        _out_proj_kernel,
        grid=(T // tm,),
        in_specs=[row(512), row(512), row(D_MODEL), full(wg), full(wd), full(g), full(b),
                  full(rwh), full(rwl), full(rb)],
        out_specs=[row(D_MODEL), row(D_MODEL // 2), row(2), row(2)],
        out_shape=[jax.ShapeDtypeStruct((T, D_MODEL), F32),
                   jax.ShapeDtypeStruct((T, D_MODEL // 2), jnp.int32),
                   jax.ShapeDtypeStruct((T, 2), jnp.int32),
                   jax.ShapeDtypeStruct((T, 2), F32)],
        compiler_params=pltpu.CompilerParams(
            dimension_semantics=("parallel",), vmem_limit_bytes=VMEM_LIMIT),
        name="out_proj_router",
    )(og, od, x2, wg, wd, g, b, rwh, rwl, rb)


def _positions_kernel(eid_ref, dest_ref, be_ref, nv_ref, carry_ref, cnt_col_ref, sp_ref, *, tb):
    phase = pl.program_id(0)
    i = pl.program_id(1)
    lane = lax.broadcasted_iota(jnp.int32, (tb, LANES), 1)
    eid = eid_ref[...]
    oh1 = lane == eid[:, 0:1]
    oh2 = lane == eid[:, 1:2]
    oh = jnp.logical_or(oh1, oh2).astype(BF16)

    @pl.when(jnp.logical_and(phase == 0, i == 0))
    def _():
        carry_ref[...] = jnp.zeros_like(carry_ref)
        cnt_col_ref[...] = jnp.zeros_like(cnt_col_ref)

    @pl.when(phase == 0)
    def _():
        ones_r = jnp.ones((8, tb), BF16)
        ones_c = jnp.ones((tb, LANES), BF16)
        carry_ref[...] += _dot(ones_r, oh)
        cnt_col_ref[...] += _dot_tn(oh, ones_c)

    @pl.when(jnp.logical_and(phase == 1, i == 0))
    def _():
        shift = int(math.log2(ROW_BLOCK))
        nb_row = ((carry_ref[...].astype(jnp.int32) + (ROW_BLOCK - 1)) >> shift)
        nb_col = ((cnt_col_ref[...].astype(jnp.int32) + (ROW_BLOCK - 1)) >> shift)
        r = lax.broadcasted_iota(jnp.int32, (LANES, LANES), 0)
        c = lax.broadcasted_iota(jnp.int32, (LANES, LANES), 1)
        excl = (r < c).astype(BF16)
        incl = (c <= r).astype(BF16)
        start_blk = _dot(nb_row.astype(F32).astype(BF16), excl)
        sp_ref[...] = start_blk * float(ROW_BLOCK)
        end_col = _dot(incl, nb_col.astype(F32).astype(BF16))
        end2 = jnp.concatenate([end_col, end_col], axis=1)
        nidx = lax.broadcasted_iota(jnp.int32, (LANES, 2 * LANES), 1).astype(F32)
        eidx = lax.broadcasted_iota(jnp.int32, (LANES, 2 * LANES), 0)
        hit = jnp.logical_and(end2 <= nidx, eidx < N_EXPERTS).astype(F32)
        be = jnp.minimum(jnp.sum(hit, axis=0, keepdims=True), float(N_EXPERTS - 1))
        be_ref[...] = be.astype(jnp.int32)
        lane1 = lax.broadcasted_iota(jnp.int32, (1, LANES), 1)
        total = jnp.sum(jnp.where(lane1 < N_EXPERTS, nb_row[0:1, :].astype(F32), 0.0), axis=-1, keepdims=True)
        nv_ref[...] = jnp.broadcast_to(total, (1, LANES)).astype(jnp.int32)
        carry_ref[...] = jnp.zeros_like(carry_ref)

    @pl.when(phase == 1)
    def _():
        r = lax.broadcasted_iota(jnp.int32, (tb, tb), 0)
        c = lax.broadcasted_iota(jnp.int32, (tb, tb), 1)
        strict = (c < r).astype(BF16)
        before = _dot(strict, oh) + carry_ref[0:1, :] + sp_ref[0:1, :]
        d1 = jnp.sum(jnp.where(oh1, before, 0.0), axis=-1, keepdims=True)
        d2 = jnp.sum(jnp.where(oh2, before, 0.0), axis=-1, keepdims=True)
        lane2 = lax.broadcasted_iota(jnp.int32, (tb, 2), 1)
        dest_ref[...] = jnp.where(lane2 == 0, d1, d2).astype(jnp.int32)
        carry_ref[...] += _dot(jnp.ones((8, tb), BF16), oh)


def _positions(eid, n_blocks_pad, tb=512):
    T = eid.shape[0]
    nb = T // tb
    return pl.pallas_call(
        functools.partial(_positions_kernel, tb=tb),
        grid=(2, nb),
        in_specs=[pl.BlockSpec((tb, 2), lambda p, i: (i, 0))],
        out_specs=[pl.BlockSpec((tb, 2), lambda p, i: (i * p, 0)),
                   pl.BlockSpec((1, n_blocks_pad), lambda p, i: (0, 0)),
                   pl.BlockSpec((1, LANES), lambda p, i: (0, 0))],
        out_shape=[jax.ShapeDtypeStruct((T, 2), jnp.int32),
                   jax.ShapeDtypeStruct((1, n_blocks_pad), jnp.int32),
                   jax.ShapeDtypeStruct((1, LANES), jnp.int32)],
        scratch_shapes=[pltpu.VMEM((8, LANES), F32), pltpu.VMEM((LANES, LANES), F32),
                        pltpu.VMEM((8, LANES), F32)],
        compiler_params=pltpu.CompilerParams(dimension_semantics=("arbitrary", "arbitrary")),
        name="positions",
    )(eid)


def _sc_gather_rows(table, idx):
    n = idx.shape[0]
    d = table.shape[1]
    info = plsc.get_sparse_core_info()
    nc, ns = info.num_cores, info.num_subcores
    per_w = n // (nc * ns)
    assert per_w * nc * ns == n and per_w % SC_INDEX_WINDOW == 0
    mesh = plsc.VectorSubcoreMesh(core_axis_name="core", subcore_axis_name="subcore")
    nsub = SC_INDEX_WINDOW // SC_GATHER_ROWS

    @functools.partial(
        pl.kernel, out_type=jax.ShapeDtypeStruct((n, d), table.dtype), mesh=mesh,
        scratch_types=[pltpu.VMEM((SC_INDEX_WINDOW,), jnp.int32),
                       pltpu.VMEM((2, SC_GATHER_ROWS, d), table.dtype),
                       pltpu.SemaphoreType.DMA((2,)), pltpu.SemaphoreType.DMA((2,))],
        name="sc_gather_rows")
    def gather(x_hbm, i_hbm, o_hbm, idx_v, buf, gsem, wsem):
        wid = lax.axis_index("subcore") * nc + lax.axis_index("core")
        base = wid * per_w

        def gather_copy(s):
            rows = idx_v.at[pl.ds(s * SC_GATHER_ROWS, SC_GATHER_ROWS)]
            return pltpu.make_async_copy(x_hbm.at[rows], buf.at[s % 2], gsem.at[s % 2])

        def write_copy(off, s):
            dst = o_hbm.at[pl.ds(off + s * SC_GATHER_ROWS, SC_GATHER_ROWS)]
            return pltpu.make_async_copy(buf.at[s % 2], dst, wsem.at[s % 2])

        @pl.loop(0, per_w // SC_INDEX_WINDOW)
        def _(j):
            off = base + j * SC_INDEX_WINDOW
            pltpu.sync_copy(i_hbm.at[pl.ds(off, SC_INDEX_WINDOW)], idx_v)
            gather_copy(0).start()
            for s in range(nsub):
                gather_copy(s).wait()
                write_copy(off, s).start()
                if s >= 1:
                    write_copy(off, s - 1).wait()
                if s + 1 < nsub:
                    gather_copy(s + 1).start()
            write_copy(off, nsub - 1).wait()

    return gather(table, idx)


def _sc_inverse_rows(dest_flat, n_rows, chunk=2048):
    n = dest_flat.shape[0]
    n_tokens = n // 2
    assert n_rows <= 3 * n_tokens
    nc = plsc.get_sparse_core_info().num_cores
    mesh = plsc.VectorSubcoreMesh(core_axis_name="core", subcore_axis_name="subcore")

    @functools.partial(
        pl.kernel, out_type=jax.ShapeDtypeStruct((n_rows,), jnp.int32), mesh=mesh,
        scratch_types=[pltpu.VMEM((n_rows,), jnp.int32), pltpu.VMEM((chunk,), jnp.int32)],
        compiler_params=pltpu.CompilerParams(needs_layout_passes=False),
        name="sc_inverse_rows")
    def inverse(d_hbm, o_hbm, inv_v, d_v):
        wid = lax.axis_index("subcore") * nc + lax.axis_index("core")

        @pl.when(wid == 0)
        def _():
            lanes = lax.iota(jnp.int32, SC_LANES)

            @pl.loop(0, n_rows // SC_LANES)
            def _(i):
                r = lanes + i * SC_LANES
                r = jnp.where(r >= n_tokens, r - n_tokens, r)
                inv_v[pl.ds(i * SC_LANES, SC_LANES)] = jnp.where(r >= n_tokens, r - n_tokens, r)

            @pl.loop(0, n // chunk)
            def _(c):
                pltpu.sync_copy(d_hbm.at[pl.ds(c * chunk, chunk)], d_v)

                @pl.loop(0, chunk // SC_LANES)
                def _(j):
                    rows = d_v[pl.ds(j * SC_LANES, SC_LANES)]
                    pair = lax.iota(jnp.int32, SC_LANES) + (c * chunk + j * SC_LANES)
                    plsc.store_scatter(inv_v, [rows], lax.shift_right_logical(pair, 1))

            pltpu.sync_copy(inv_v, o_hbm)

    return inverse(dest_flat)


def _ffn_kernel(be_ref, nv_ref, xs_ref, wg_ref, wu_ref, wd_ref, y_ref, wgb, wub, wdb):
    n = pl.program_id(0)
    prev = be_ref[jnp.maximum(n - 1, 0)]
    fresh = jnp.logical_or(n == 0, be_ref[n] != prev)

    @pl.when(jnp.logical_and(fresh, n < nv_ref[0]))
    def _():
        wgb[...] = wg_ref[0].astype(BF16)
        wub[...] = wu_ref[0].astype(BF16)
        wdb[...] = wd_ref[0].astype(BF16)

    @pl.when(n < nv_ref[0])
    def _():
        x_hi, x_lo = _unpack_bf16_halves(xs_ref[...])
        xb = jnp.concatenate([x_hi.astype(BF16), x_lo.astype(BF16)], axis=1)
        a = _dot(xb, wgb[...])
        u = _dot(xb, wub[...])
        hid = (a * jax.nn.sigmoid(a) * u).astype(BF16)
        y_ref[...] = _pack_bf16_halves(_dot(hid, wdb[...]))

    @pl.when(n >= nv_ref[0])
    def _():
        y_ref[...] = jnp.zeros_like(y_ref)


def _ffn(be, nv, xs, w_gate, w_up, w_down):
    n_rows = xs.shape[0]
    n_blocks = n_rows // ROW_BLOCK

    def blk(n, be, nv):
        return (jnp.minimum(n, nv[0] - 1), 0)

    def wsel(n, be, nv):
        return (be[jnp.minimum(n, nv[0] - 1)], 0, 0)

    return pl.pallas_call(
        _ffn_kernel,
        grid_spec=pltpu.PrefetchScalarGridSpec(
            num_scalar_prefetch=2,
            grid=(n_blocks,),
            in_specs=[pl.BlockSpec((ROW_BLOCK, D_MODEL // 2), blk),
                      pl.BlockSpec((1, D_MODEL, D_FF), wsel),
                      pl.BlockSpec((1, D_MODEL, D_FF), wsel),
                      pl.BlockSpec((1, D_FF, D_MODEL), wsel)],
            out_specs=pl.BlockSpec((ROW_BLOCK, D_MODEL // 2), lambda n, be, nv: (n, 0)),
            scratch_shapes=[pltpu.VMEM((D_MODEL, D_FF), BF16), pltpu.VMEM((D_MODEL, D_FF), BF16),
                            pltpu.VMEM((D_FF, D_MODEL), BF16)],
        ),
        out_shape=jax.ShapeDtypeStruct((n_rows, D_MODEL // 2), jnp.int32),
        compiler_params=pltpu.CompilerParams(
            dimension_semantics=("arbitrary",), vmem_limit_bytes=VMEM_LIMIT),
        name="expert_ffn",
    )(be, nv, xs, w_gate, w_up, w_down)


def _combine_kernel(h_ref, ya_ref, yb_ref, gate_ref, g_ref, b_ref, o_ref):
    gate = gate_ref[...]
    a_hi, a_lo = _unpack_bf16_halves(ya_ref[...])
    b_hi, b_lo = _unpack_bf16_halves(yb_ref[...])
    g0, g1 = gate[:, 0:1], gate[:, 1:2]
    ffn = jnp.concatenate([a_hi * g0 + b_hi * g1, a_lo * g0 + b_lo * g1], axis=1)
    o_ref[...] = _layer_norm(DEEPNORM_ALPHA * h_ref[...] + ffn, g_ref[...], b_ref[...])


def _combine(h, y2, gate, g, b, tm=512):
    T = h.shape[0]
    nt = T // tm
    return pl.pallas_call(
        _combine_kernel,
        grid=(nt,),
        in_specs=[pl.BlockSpec((tm, D_MODEL), lambda i: (i, 0)),
                  pl.BlockSpec((tm, D_MODEL // 2), lambda i: (i, 0)),
                  pl.BlockSpec((tm, D_MODEL // 2), lambda i: (i + nt, 0)),
                  pl.BlockSpec((tm, 2), lambda i: (i, 0)),
                  pl.BlockSpec((1, D_MODEL), lambda i: (0, 0)),
                  pl.BlockSpec((1, D_MODEL), lambda i: (0, 0))],
        out_specs=pl.BlockSpec((tm, D_MODEL), lambda i: (i, 0)),
        out_shape=jax.ShapeDtypeStruct((T, D_MODEL), F32),
        compiler_params=pltpu.CompilerParams(
            dimension_semantics=("parallel",), vmem_limit_bytes=VMEM_LIMIT),
        name="combine",
    )(h, y2, y2, gate, g, b)


def kernel(x, w_in, gla_gate_w2, gla_gate_b, gla_norm_g, dil_norm_g, w_out, ln1_g, ln1_b,
           router_coarse_w, router_coarse_b, router_fine_w, router_fine_b,
           expert_w_gate, expert_w_up, expert_w_down, ln2_g, ln2_b):
    B, S, D = x.shape
    T = B * S
    depth = w_in.shape[0]
    slopes = jnp.exp2(-8.0 * jnp.arange(1, DIL_HEADS + 1, dtype=F32) / DIL_HEADS)
    n_rows = 2 * T + N_EXPERTS * ROW_BLOCK
    n_blocks = n_rows // ROW_BLOCK
    n_blocks_pad = -(-n_blocks // (2 * LANES)) * (2 * LANES)
    assert n_blocks_pad == 2 * LANES
    a0 = 1536
    h = x.reshape(T, D)
    for l in range(depth):
        w = w_in[l]
        wm = jnp.concatenate([w[:, :a0], w[:, a0 + GLA_GATE_RANK:]], axis=1).astype(BF16)
        wa = jnp.pad(w[:, a0:a0 + GLA_GATE_RANK], ((0, 0), (0, LANES - GLA_GATE_RANK))).astype(BF16)
        w2 = jnp.pad(gla_gate_w2[l], ((0, LANES - GLA_GATE_RANK), (0, 0)))
        w2h, w2l = _split_bf16(w2)
        q, k, v, r, la, dq, dk, dv = _in_proj(h, wm, wa, w2h, w2l, gla_gate_b[l][None, :])
        o_gla = _gla(q, k, v, r, la, gla_norm_g[l][None, :], B, S)
        g2 = jnp.tile(dil_norm_g[l], 2)[None, :]
        o_dil = _dilated(slopes, dq, dk, dv, g2, B, S)
        wo = w_out[l].astype(BF16)
        rw = jnp.concatenate([router_fine_w[l].reshape(D, N_EXPERTS), router_coarse_w[l]], axis=1)
        rw = jnp.pad(rw, ((0, 0), (0, LANES - N_EXPERTS - N_GROUPS)))
        rwh, rwl = _split_bf16(rw)
        rb = jnp.concatenate([router_fine_b[l].reshape(N_EXPERTS), router_coarse_b[l]])
        rb = jnp.pad(rb, (0, LANES - N_EXPERTS - N_GROUPS))[None, :]
        h1, h1p, eid, gate = _out_proj(o_gla, o_dil, h, wo[:GLA_WIDTH], wo[GLA_WIDTH:],
                                  ln1_g[l][None, :], ln1_b[l][None, :], rwh, rwl, rb)
        dest, be, nv = _positions(eid, n_blocks_pad)
        src_tok = _sc_inverse_rows(dest.reshape(2 * T), n_rows)
        xs = _sc_gather_rows(h1p, src_tok)
        y = _ffn(be.reshape(n_blocks_pad), nv.reshape(LANES)[:1], xs,
                 expert_w_gate[l], expert_w_up[l], expert_w_down[l])
        y2 = _sc_gather_rows(y, dest.T.reshape(2 * T))
        h = _combine(h1, y2, gate, ln2_g[l][None, :], ln2_b[l][None, :])
    return h.reshape(B, S, D)
```

```python
import functools
import math

import jax
import jax.numpy as jnp
import numpy as np
from jax import lax
from jax.experimental import pallas as pl
from jax.experimental.pallas import tpu as pltpu
from jax.experimental.pallas import tpu_sc as plsc

D_MODEL = 1024
GLA_HEADS = 4
GLA_DK = 64
GLA_DV = 128
GLA_KEY_WIDTH = GLA_HEADS * GLA_DK
GLA_WIDTH = GLA_HEADS * GLA_DV
GLA_GATE_RANK = 16
GLA_GATE_TEMP = 16.0
DIL_HEADS = 8
DIL_DH = 64
DIL_WIDTH = DIL_HEADS * DIL_DH
DIL_CONFIGS = ((128, 1), (512, 4), (2048, 16))
DIL_BLOCK = 128
DIL_MAX_R = max(r for _, r in DIL_CONFIGS)
DIL_PAD = DIL_BLOCK * DIL_MAX_R
DIL_UNROLL = 8
N_GROUPS = 4
EXPERTS_PER_GROUP = 8
N_EXPERTS = N_GROUPS * EXPERTS_PER_GROUP
D_FF = 512
DEEPNORM_ALPHA = 2.0 ** 0.25
EPS = 1e-5
LOG2E = math.log2(math.e)
LN2 = math.log(2.0)

LANES = 128
GLA_CHUNK = 64
GLA_UNROLL = 4
SC_LANES = 16
SC_INDEX_WINDOW = 128
SC_GATHER_ROWS = 64
ROW_BLOCK = 256
VMEM_LIMIT = 56 * 1024 * 1024

F32 = jnp.float32
BF16 = jnp.bfloat16


def _dot(a, b):
    return jnp.dot(a, b, preferred_element_type=F32)


def _dot_nt(a, b):
    return lax.dot_general(a, b, (((1,), (1,)), ((), ())), preferred_element_type=F32)


def _dot_tn(a, b):
    return lax.dot_general(a, b, (((0,), (0,)), ((), ())), preferred_element_type=F32)


def _split_bf16(v):
    hi = v.astype(BF16)
    lo = (v - hi.astype(F32)).astype(BF16)
    return hi, lo


def _pack_bf16_halves(v):
    w = v.shape[1] // 2
    hi = lax.bitcast_convert_type(v[:, :w].astype(BF16).astype(F32), jnp.int32)
    lo = lax.bitcast_convert_type(v[:, w:].astype(BF16).astype(F32), jnp.int32)
    return hi | lax.shift_right_logical(lo, 16)


def _unpack_bf16_halves(words):
    hi = lax.bitcast_convert_type(words & jnp.int32(-65536), F32)
    lo = lax.bitcast_convert_type(lax.shift_left(words, 16), F32)
    return hi, lo


def _layer_norm(v, g, b):
    mu = jnp.mean(v, axis=-1, keepdims=True)
    c = v - mu
    var = jnp.mean(c * c, axis=-1, keepdims=True)
    return c * lax.rsqrt(var + EPS) * g + b


def _in_proj_kernel(x_ref, wm_ref, wa_ref, w2h_ref, w2l_ref, gb_ref,
                    q_ref, k_ref, v_ref, r_ref, la_ref, dq_ref, dk_ref, dv_ref):
    xb = x_ref[...].astype(BF16)

    def piece(c0, c1):
        return _dot(xb, wm_ref[:, c0:c1])

    q_ref[...] = (piece(0, 256) * (GLA_DK ** -0.5)).astype(BF16)
    k_ref[...] = piece(256, 512).astype(BF16)
    v_ref[...] = piece(512, 1024).astype(BF16)
    r_ref[...] = piece(1024, 1536).astype(BF16)
    dq_ref[...] = (piece(1536, 2048) * (DIL_DH ** -0.5 * LOG2E)).astype(BF16)
    dk_ref[...] = piece(2048, 2560).astype(BF16)
    dv_ref[...] = piece(2560, 3072).astype(BF16)
    ga = _dot(xb, wa_ref[...])
    ga_hi, ga_lo = _split_bf16(ga)
    z = _dot(ga_hi, w2h_ref[...]) + _dot(ga_lo, w2h_ref[...]) + _dot(ga_hi, w2l_ref[...]) + gb_ref[...]
    log_sig = jnp.minimum(z, 0.0) - jnp.log1p(jnp.exp(-jnp.abs(z)))
    la_ref[...] = log_sig * (1.0 / GLA_GATE_TEMP)


def _in_proj(x2, wm, wa, w2h, w2l, gb, tm=512):
    T = x2.shape[0]
    row = lambda w: pl.BlockSpec((tm, w), lambda i: (i, 0))
    full = lambda a: pl.BlockSpec(a.shape, lambda i: (0,) * a.ndim)
    outs = [(256, BF16), (256, BF16), (512, BF16), (512, BF16), (256, F32),
            (512, BF16), (512, BF16), (512, BF16)]
    return pl.pallas_call(
        _in_proj_kernel,
        grid=(T // tm,),
        in_specs=[row(D_MODEL), full(wm), full(wa), full(w2h), full(w2l), full(gb)],
        out_specs=[row(w) for w, _ in outs],
        out_shape=[jax.ShapeDtypeStruct((T, w), dt) for w, dt in outs],
        compiler_params=pltpu.CompilerParams(
            dimension_semantics=("parallel",), vmem_limit_bytes=VMEM_LIMIT),
        name="in_proj",
    )(x2, wm, wa, w2h, w2l, gb)


def _gla_kernel(q_ref, k_ref, v_ref, r_ref, la_ref, g_ref, o_ref, s_ref, *, seq_block):
    C = GLA_CHUNK

    @pl.when(pl.program_id(1) == 0)
    def _():
        s_ref[...] = jnp.zeros_like(s_ref)

    ri = lax.broadcasted_iota(jnp.int32, (C, C), 0)
    ci = lax.broadcasted_iota(jnp.int32, (C, C), 1)
    causal = ci <= ri
    tri = causal.astype(BF16)
    ones_cl = jnp.ones((C, LANES), BF16)
    lane_k = lax.broadcasted_iota(jnp.int32, (1, GLA_KEY_WIDTH), 1) // GLA_DK
    head_masks = [(lane_k == h).astype(F32) for h in range(GLA_HEADS)]
    srow = lax.broadcasted_iota(jnp.int32, (GLA_KEY_WIDTH, GLA_WIDTH), 0) // GLA_DK
    scol = lax.broadcasted_iota(jnp.int32, (GLA_KEY_WIDTH, GLA_WIDTH), 1) // GLA_DV
    state_mask = (srow == scol).astype(F32)
    g = g_ref[...]

    def chunk(c, carry):
        rows = pl.ds(pl.multiple_of(c * C, C), C)
        la = la_ref[rows, :]
        la_hi, la_lo = _split_bf16(la)
        b = _dot(tri, la_hi) + _dot(tri, la_lo)
        b_last = b[C - 1:C, :]
        q = q_ref[rows, :].astype(F32)
        k = k_ref[rows, :].astype(F32)
        v = v_ref[rows, :]
        qd = q * jnp.exp(b)
        kd = (k * jnp.exp(-b)).astype(BF16)
        ke = (k * jnp.exp(b_last - b)).astype(BF16)
        state = s_ref[...]
        o_inter = _dot(qd.astype(BF16), state.astype(BF16))
        outs = []
        for h in range(GLA_HEADS):
            a = _dot_nt((qd * head_masks[h]).astype(BF16), kd)
            a = jnp.where(causal, a, 0.0).astype(BF16)
            cols = slice(h * GLA_DV, (h + 1) * GLA_DV)
            o = _dot(a, v[:, cols]) + o_inter[:, cols]
            o = o * lax.rsqrt(jnp.mean(o * o, axis=-1, keepdims=True) + EPS) * g
            outs.append(o)
        o_all = jnp.concatenate(outs, axis=-1)
        rr = r_ref[rows, :].astype(F32)
        o_ref[rows, :] = (o_all * (rr * jax.nn.sigmoid(rr))).astype(BF16)
        tot = _dot_tn(la_hi, ones_cl) + _dot_tn(la_lo, ones_cl)
        dec = jnp.exp(tot)
        upd = _dot_tn(ke, v) * state_mask
        for h in range(GLA_HEADS):
            cols = slice(h * GLA_DV, (h + 1) * GLA_DV)
            s_ref[:, cols] = state[:, cols] * dec + upd[:, cols]
        return carry

    lax.fori_loop(0, seq_block // C, chunk, 0, unroll=GLA_UNROLL)


def _gla(q, k, v, r, la, g, batch, seq, seq_block=1024):
    nsb = seq // seq_block
    row = lambda w: pl.BlockSpec((seq_block, w), lambda b, s: (b * nsb + s, 0))
    return pl.pallas_call(
        functools.partial(_gla_kernel, seq_block=seq_block),
        grid=(batch, nsb),
        in_specs=[row(256), row(256), row(512), row(512), row(256),
                  pl.BlockSpec((1, GLA_DV), lambda b, s: (0, 0))],
        out_specs=row(512),
        out_shape=jax.ShapeDtypeStruct((batch * seq, GLA_WIDTH), BF16),
        scratch_shapes=[pltpu.VMEM((GLA_KEY_WIDTH, GLA_WIDTH), F32)],
        compiler_params=pltpu.CompilerParams(
            dimension_semantics=("parallel", "arbitrary"), vmem_limit_bytes=VMEM_LIMIT),
        name="gla",
    )(q, k, v, r, la, g)


def _dil_kernel(slope_ref, q_ref, k_ref, v_ref, g_ref, o_ref,
                qf, kf, vf, kd, va, vb, oc, lc, *, seq):
    B = DIL_BLOCK
    U = DIL_UNROLL
    pair = pl.program_id(1)
    qf[...] = q_ref[...].astype(F32)
    kf[...] = k_ref[...].astype(F32)
    vf[...] = v_ref[...].astype(F32)

    lane = lax.broadcasted_iota(jnp.int32, (1, LANES), 1)
    first = lane < DIL_DH
    ii = lax.broadcasted_iota(jnp.int32, (B, B), 0)
    jj = lax.broadcasted_iota(jnp.int32, (B, B), 1)
    upper = jj > ii
    eye = jj == ii
    dist = jnp.bitwise_and(ii - jj, B - 1).astype(F32)
    neg = jnp.float32(-jnp.inf)
    neg_tile = jnp.full((B, B), neg, F32)
    zero_tile = jnp.zeros((B, LANES), BF16)

    for cfg, (window, r) in enumerate(DIL_CONFIGS):
        nb = seq // r // B
        cs = nb + 1
        bias_prev, bias_cur = [], []
        for hh in range(2):
            slope = slope_ref[2 * pair + hh] * (float(r) * LOG2E)
            bias = dist * (-slope)
            bias_prev.append(jnp.where(upper, bias, jnp.where(eye, -slope * float(B), neg)))
            bias_cur.append(jnp.where(upper, neg, bias))

        for c in range(r):
            rows0 = slice(c * cs * B, (c * cs + 1) * B)
            kd[rows0, :] = zero_tile
            va[rows0, :] = zero_tile
            vb[rows0, :] = zero_tile

        def prep(t4, carry, r=r, nb=nb, cs=cs):
            for j in range(4):
                t = t4 * 4 + j
                c = t // nb
                n = t % nb
                start = c + n * (B * r)
                rows = pl.ds(start, B, stride=r) if r > 1 else pl.ds(pl.multiple_of(start, B), B)
                dst = pl.ds(pl.multiple_of((c * cs + 1 + n) * B, B), B)
                kd[dst, :] = kf[rows, :].astype(BF16)
                v = vf[rows, :]
                va[dst, :] = jnp.where(first, v, 1.0).astype(BF16)
                vb[dst, :] = jnp.where(first, 1.0, v).astype(BF16)
            return carry

        lax.fori_loop(0, seq // B // 4, prep, 0)

        def block(it, has_prev, cfg=cfg, r=r, nb=nb, cs=cs, bias_prev=bias_prev, bias_cur=bias_cur):
            c = it // nb
            n = it % nb
            start = c + n * (B * r)
            rows = pl.ds(start, B, stride=r) if r > 1 else pl.ds(pl.multiple_of(start, B), B)
            kv = pl.ds(pl.multiple_of((c * cs + n) * B, B), 2 * B)
            q = qf[rows, :]
            kcat = kd[kv, :]
            acc, mx = [], []
            for hh in range(2):
                hmask = first if hh == 0 else jnp.logical_not(first)
                qh = jnp.where(hmask, q, 0.0).astype(BF16)
                s2 = _dot_nt(qh, kcat)
                if has_prev is True:
                    bp = bias_prev[hh]
                elif has_prev is False:
                    bp = neg_tile
                else:
                    bp = jnp.where(has_prev, bias_prev[hh], neg)
                s_prev = s2[:, 0:B] + bp
                s_cur = s2[:, B:2 * B] + bias_cur[hh]
                m = jnp.max(jnp.maximum(s_prev, s_cur), axis=-1, keepdims=True)
                pcat = jnp.concatenate([jnp.exp2(s_prev - m), jnp.exp2(s_cur - m)], axis=1).astype(BF16)
                vals = va[kv, :] if hh == 0 else vb[kv, :]
                acc.append(_dot(pcat, vals))
                mx.append(m)
            num = jnp.where(first, acc[0], acc[1])
            den = pltpu.roll(jnp.where(first, acc[1], acc[0]), DIL_DH, axis=1)
            oc[cfg, rows, :] = num * (1.0 / den)
            lc[cfg, rows, :] = (jnp.where(first, mx[0], mx[1]) + jnp.log2(den)) * LN2

        def body(t, carry, block=block, nb=nb):
            for u in range(U):
                if nb % U == 0:
                    has_prev = True if u > 0 else (t * U) % nb > 0
                else:
                    assert U % nb == 0
                    has_prev = (u % nb) > 0
                block(t * U + u, has_prev)
            return carry

        lax.fori_loop(0, seq // B // U, body, 0)

    g = g_ref[...]
    CH = 512

    def mix(i, carry):
        rows = pl.ds(pl.multiple_of(i * CH, CH), CH)
        l0, l1, l2 = lc[0, rows, :], lc[1, rows, :], lc[2, rows, :]
        m = jnp.maximum(jnp.maximum(l0, l1), l2)
        e0, e1, e2 = jnp.exp(l0 - m), jnp.exp(l1 - m), jnp.exp(l2 - m)
        den = e0 + e1 + e2
        o = (e0 / den) * oc[0, rows, :] + (e1 / den) * oc[1, rows, :] + (e2 / den) * oc[2, rows, :]
        sq = o * o
        ms_a = jnp.sum(jnp.where(first, sq, 0.0), axis=-1, keepdims=True) * (1.0 / DIL_DH)
        ms_b = jnp.sum(jnp.where(first, 0.0, sq), axis=-1, keepdims=True) * (1.0 / DIL_DH)
        ms = jnp.where(first, ms_a, ms_b)
        o_ref[rows, :] = (o * lax.rsqrt(ms + EPS) * g).astype(BF16)
        return carry

    lax.fori_loop(0, seq // CH, mix, 0)


def _dilated(slopes, dq, dk, dv, g2, batch, seq):
    blk = pl.BlockSpec((seq, LANES), lambda b, p, s: (b, p))
    return pl.pallas_call(
        functools.partial(_dil_kernel, seq=seq),
        grid_spec=pltpu.PrefetchScalarGridSpec(
            num_scalar_prefetch=1,
            grid=(batch, DIL_WIDTH // LANES),
            in_specs=[blk, blk, blk, pl.BlockSpec((1, LANES), lambda b, p, s: (0, 0))],
            out_specs=blk,
            scratch_shapes=[pltpu.VMEM((seq, LANES), F32)] * 3
                           + [pltpu.VMEM((seq + DIL_PAD, LANES), BF16)] * 3
                           + [pltpu.VMEM((3, seq, LANES), F32)] * 2,
        ),
        out_shape=jax.ShapeDtypeStruct((batch * seq, DIL_WIDTH), BF16),
        compiler_params=pltpu.CompilerParams(
            dimension_semantics=("parallel", "parallel"), vmem_limit_bytes=VMEM_LIMIT),
        name="dilated",
    )(slopes, dq, dk, dv, g2)


def _out_proj_kernel(og_ref, od_ref, x_ref, wg_ref, wd_ref, g_ref, b_ref,
                     rwh_ref, rwl_ref, rb_ref, h_ref, hp_ref, eid_ref, gate_ref):
    mix = _dot(og_ref[...], wg_ref[...]) + _dot(od_ref[...], wd_ref[...])
    h = _layer_norm(DEEPNORM_ALPHA * x_ref[...] + mix, g_ref[...], b_ref[...])
    h_ref[...] = h
    hp_ref[...] = _pack_bf16_halves(h)
    h_hi, h_lo = _split_bf16(h)
    logits = (_dot(h_hi, rwh_ref[...]) + _dot(h_lo, rwh_ref[...]) + _dot(h_hi, rwl_ref[...])
              + rb_ref[...])
    lane = lax.broadcasted_iota(jnp.int32, logits.shape, 1)
    lane_f = lane.astype(F32)
    neg = jnp.float32(-jnp.inf)
    big = jnp.float32(1e9)
    is_coarse = jnp.logical_and(lane >= N_EXPERTS, lane < N_EXPERTS + N_GROUPS)
    coarse = jnp.where(is_coarse, logits, neg)
    cmax = jnp.max(coarse, axis=-1, keepdims=True)
    g_idx = jnp.min(jnp.where(coarse == cmax, lane_f, big), axis=-1, keepdims=True) - N_EXPERTS
    p_group = 1.0 / jnp.sum(jnp.exp(coarse - cmax), axis=-1, keepdims=True)
    lo = g_idx * EXPERTS_PER_GROUP
    in_group = jnp.logical_and(lane_f >= lo, lane_f < lo + EXPERTS_PER_GROUP)
    fine = jnp.where(in_group, logits, neg)
    v1 = jnp.max(fine, axis=-1, keepdims=True)
    i1 = jnp.min(jnp.where(fine == v1, lane_f, big), axis=-1, keepdims=True)
    fine2 = jnp.where(lane_f == i1, neg, fine)
    v2 = jnp.max(fine2, axis=-1, keepdims=True)
    i2 = jnp.min(jnp.where(fine2 == v2, lane_f, big), axis=-1, keepdims=True)
    e2 = jnp.exp(v2 - v1)
    den = 1.0 + e2
    gate1 = p_group * (1.0 / den)
    gate2 = p_group * (e2 / den)
    lane2 = lax.broadcasted_iota(jnp.int32, (logits.shape[0], 2), 1)
    eid_ref[...] = jnp.where(lane2 == 0, i1, i2).astype(jnp.int32)
    gate_ref[...] = jnp.where(lane2 == 0, gate1, gate2)


def _out_proj(og, od, x2, wg, wd, g, b, rwh, rwl, rb, tm=512):
    T = x2.shape[0]
    row = lambda w: pl.BlockSpec((tm, w), lambda i: (i, 0))
    full = lambda a: pl.BlockSpec(a.shape, lambda i: (0,) * a.ndim)
    return pl.pallas_call(
        _out_proj_kernel,
        grid=(T // tm,),
        in_specs=[row(512), row(512), row(D_MODEL), full(wg), full(wd), full(g), full(b),
                  full(rwh), full(rwl), full(rb)],
        out_specs=[row(D_MODEL), row(D_MODEL // 2), row(2), row(2)],
        out_shape=[jax.ShapeDtypeStruct((T, D_MODEL), F32),
                   jax.ShapeDtypeStruct((T, D_MODEL // 2), jnp.int32),
                   jax.ShapeDtypeStruct((T, 2), jnp.int32),
                   jax.ShapeDtypeStruct((T, 2), F32)],
        compiler_params=pltpu.CompilerParams(
            dimension_semantics=("parallel",), vmem_limit_bytes=VMEM_LIMIT),
        name="out_proj_router",
    )(og, od, x2, wg, wd, g, b, rwh, rwl, rb)


def _positions_kernel(eid_ref, dest_ref, be_ref, nv_ref, carry_ref, cnt_col_ref, sp_ref, *, tb):
    phase = pl.program_id(0)
    i = pl.program_id(1)
    lane = lax.broadcasted_iota(jnp.int32, (tb, LANES), 1)
    eid = eid_ref[...]
    oh1 = lane == eid[:, 0:1]
    oh2 = lane == eid[:, 1:2]
    oh = jnp.logical_or(oh1, oh2).astype(BF16)

    @pl.when(jnp.logical_and(phase == 0, i == 0))
    def _():
        carry_ref[...] = jnp.zeros_like(carry_ref)
        cnt_col_ref[...] = jnp.zeros_like(cnt_col_ref)

    @pl.when(phase == 0)
    def _():
        ones_r = jnp.ones((8, tb), BF16)
        ones_c = jnp.ones((tb, LANES), BF16)
        carry_ref[...] += _dot(ones_r, oh)
        cnt_col_ref[...] += _dot_tn(oh, ones_c)

    @pl.when(jnp.logical_and(phase == 1, i == 0))
    def _():
        shift = int(math.log2(ROW_BLOCK))
        nb_row = ((carry_ref[...].astype(jnp.int32) + (ROW_BLOCK - 1)) >> shift)
        nb_col = ((cnt_col_ref[...].astype(jnp.int32) + (ROW_BLOCK - 1)) >> shift)
        r = lax.broadcasted_iota(jnp.int32, (LANES, LANES), 0)
        c = lax.broadcasted_iota(jnp.int32, (LANES, LANES), 1)
        excl = (r < c).astype(BF16)
        incl = (c <= r).astype(BF16)
        start_blk = _dot(nb_row.astype(F32).astype(BF16), excl)
        sp_ref[...] = start_blk * float(ROW_BLOCK)
        end_col = _dot(incl, nb_col.astype(F32).astype(BF16))
        end2 = jnp.concatenate([end_col, end_col], axis=1)
        nidx = lax.broadcasted_iota(jnp.int32, (LANES, 2 * LANES), 1).astype(F32)
        eidx = lax.broadcasted_iota(jnp.int32, (LANES, 2 * LANES), 0)
        hit = jnp.logical_and(end2 <= nidx, eidx < N_EXPERTS).astype(F32)
        be = jnp.minimum(jnp.sum(hit, axis=0, keepdims=True), float(N_EXPERTS - 1))
        be_ref[...] = be.astype(jnp.int32)
        lane1 = lax.broadcasted_iota(jnp.int32, (1, LANES), 1)
        total = jnp.sum(jnp.where(lane1 < N_EXPERTS, nb_row[0:1, :].astype(F32), 0.0), axis=-1, keepdims=True)
        nv_ref[...] = jnp.broadcast_to(total, (1, LANES)).astype(jnp.int32)
        carry_ref[...] = jnp.zeros_like(carry_ref)

    @pl.when(phase == 1)
    def _():
        r = lax.broadcasted_iota(jnp.int32, (tb, tb), 0)
        c = lax.broadcasted_iota(jnp.int32, (tb, tb), 1)
        strict = (c < r).astype(BF16)
        before = _dot(strict, oh) + carry_ref[0:1, :] + sp_ref[0:1, :]
        d1 = jnp.sum(jnp.where(oh1, before, 0.0), axis=-1, keepdims=True)
        d2 = jnp.sum(jnp.where(oh2, before, 0.0), axis=-1, keepdims=True)
        lane2 = lax.broadcasted_iota(jnp.int32, (tb, 2), 1)
        dest_ref[...] = jnp.where(lane2 == 0, d1, d2).astype(jnp.int32)
        carry_ref[...] += _dot(jnp.ones((8, tb), BF16), oh)


def _positions(eid, n_blocks_pad, tb=512):
    T = eid.shape[0]
    nb = T // tb
    return pl.pallas_call(
        functools.partial(_positions_kernel, tb=tb),
        grid=(2, nb),
        in_specs=[pl.BlockSpec((tb, 2), lambda p, i: (i, 0))],
        out_specs=[pl.BlockSpec((tb, 2), lambda p, i: (i * p, 0)),
                   pl.BlockSpec((1, n_blocks_pad), lambda p, i: (0, 0)),
                   pl.BlockSpec((1, LANES), lambda p, i: (0, 0))],
        out_shape=[jax.ShapeDtypeStruct((T, 2), jnp.int32),
                   jax.ShapeDtypeStruct((1, n_blocks_pad), jnp.int32),
                   jax.ShapeDtypeStruct((1, LANES), jnp.int32)],
        scratch_shapes=[pltpu.VMEM((8, LANES), F32), pltpu.VMEM((LANES, LANES), F32),
                        pltpu.VMEM((8, LANES), F32)],
        compiler_params=pltpu.CompilerParams(dimension_semantics=("arbitrary", "arbitrary")),
        name="positions",
    )(eid)


def _sc_gather_rows(table, idx):
    n = idx.shape[0]
    d = table.shape[1]
    info = plsc.get_sparse_core_info()
    nc, ns = info.num_cores, info.num_subcores
    per_w = n // (nc * ns)
    assert per_w * nc * ns == n and per_w % SC_INDEX_WINDOW == 0
    mesh = plsc.VectorSubcoreMesh(core_axis_name="core", subcore_axis_name="subcore")
    nsub = SC_INDEX_WINDOW // SC_GATHER_ROWS

    @functools.partial(
        pl.kernel, out_type=jax.ShapeDtypeStruct((n, d), table.dtype), mesh=mesh,
        scratch_types=[pltpu.VMEM((SC_INDEX_WINDOW,), jnp.int32),
                       pltpu.VMEM((2, SC_GATHER_ROWS, d), table.dtype),
                       pltpu.SemaphoreType.DMA((2,)), pltpu.SemaphoreType.DMA((2,))],
        name="sc_gather_rows")
    def gather(x_hbm, i_hbm, o_hbm, idx_v, buf, gsem, wsem):
        wid = lax.axis_index("subcore") * nc + lax.axis_index("core")
        base = wid * per_w

        def gather_copy(s):
            rows = idx_v.at[pl.ds(s * SC_GATHER_ROWS, SC_GATHER_ROWS)]
            return pltpu.make_async_copy(x_hbm.at[rows], buf.at[s % 2], gsem.at[s % 2])

        def write_copy(off, s):
            dst = o_hbm.at[pl.ds(off + s * SC_GATHER_ROWS, SC_GATHER_ROWS)]
            return pltpu.make_async_copy(buf.at[s % 2], dst, wsem.at[s % 2])

        @pl.loop(0, per_w // SC_INDEX_WINDOW)
        def _(j):
            off = base + j * SC_INDEX_WINDOW
            pltpu.sync_copy(i_hbm.at[pl.ds(off, SC_INDEX_WINDOW)], idx_v)
            gather_copy(0).start()
            for s in range(nsub):
                gather_copy(s).wait()
                write_copy(off, s).start()
                if s >= 1:
                    write_copy(off, s - 1).wait()
                if s + 1 < nsub:
                    gather_copy(s + 1).start()
            write_copy(off, nsub - 1).wait()

    return gather(table, idx)


def _sc_inverse_rows(dest_flat, n_rows, chunk=2048):
    n = dest_flat.shape[0]
    n_tokens = n // 2
    assert n_rows <= 3 * n_tokens
    nc = plsc.get_sparse_core_info().num_cores
    mesh = plsc.VectorSubcoreMesh(core_axis_name="core", subcore_axis_name="subcore")

    @functools.partial(
        pl.kernel, out_type=jax.ShapeDtypeStruct((n_rows,), jnp.int32), mesh=mesh,
        scratch_types=[pltpu.VMEM((n_rows,), jnp.int32), pltpu.VMEM((chunk,), jnp.int32)],
        compiler_params=pltpu.CompilerParams(needs_layout_passes=False),
        name="sc_inverse_rows")
    def inverse(d_hbm, o_hbm, inv_v, d_v):
        wid = lax.axis_index("subcore") * nc + lax.axis_index("core")

        @pl.when(wid == 0)
        def _():
            lanes = lax.iota(jnp.int32, SC_LANES)

            @pl.loop(0, n_rows // SC_LANES)
            def _(i):
                r = lanes + i * SC_LANES
                r = jnp.where(r >= n_tokens, r - n_tokens, r)
                inv_v[pl.ds(i * SC_LANES, SC_LANES)] = jnp.where(r >= n_tokens, r - n_tokens, r)

            @pl.loop(0, n // chunk)
            def _(c):
                pltpu.sync_copy(d_hbm.at[pl.ds(c * chunk, chunk)], d_v)

                @pl.loop(0, chunk // SC_LANES)
                def _(j):
                    rows = d_v[pl.ds(j * SC_LANES, SC_LANES)]
                    pair = lax.iota(jnp.int32, SC_LANES) + (c * chunk + j * SC_LANES)
                    plsc.store_scatter(inv_v, [rows], lax.shift_right_logical(pair, 1))

            pltpu.sync_copy(inv_v, o_hbm)

    return inverse(dest_flat)


def _ffn_kernel(be_ref, nv_ref, xs_ref, wg_ref, wu_ref, wd_ref, y_ref, wgb, wub, wdb):
    n = pl.program_id(0)
    prev = be_ref[jnp.maximum(n - 1, 0)]
    fresh = jnp.logical_or(n == 0, be_ref[n] != prev)

    @pl.when(jnp.logical_and(fresh, n < nv_ref[0]))
    def _():
        wgb[...] = wg_ref[0].astype(BF16)
        wub[...] = wu_ref[0].astype(BF16)
        wdb[...] = wd_ref[0].astype(BF16)

    @pl.when(n < nv_ref[0])
    def _():
        x_hi, x_lo = _unpack_bf16_halves(xs_ref[...])
        xb = jnp.concatenate([x_hi.astype(BF16), x_lo.astype(BF16)], axis=1)
        a = _dot(xb, wgb[...])
        u = _dot(xb, wub[...])
        hid = (a * jax.nn.sigmoid(a) * u).astype(BF16)
        y_ref[...] = _pack_bf16_halves(_dot(hid, wdb[...]))

    @pl.when(n >= nv_ref[0])
    def _():
        y_ref[...] = jnp.zeros_like(y_ref)


def _ffn(be, nv, xs, w_gate, w_up, w_down):
    n_rows = xs.shape[0]
    n_blocks = n_rows // ROW_BLOCK

    def blk(n, be, nv):
        return (jnp.minimum(n, nv[0] - 1), 0)

    def wsel(n, be, nv):
        return (be[jnp.minimum(n, nv[0] - 1)], 0, 0)

    return pl.pallas_call(
        _ffn_kernel,
        grid_spec=pltpu.PrefetchScalarGridSpec(
            num_scalar_prefetch=2,
            grid=(n_blocks,),
            in_specs=[pl.BlockSpec((ROW_BLOCK, D_MODEL // 2), blk),
                      pl.BlockSpec((1, D_MODEL, D_FF), wsel),
                      pl.BlockSpec((1, D_MODEL, D_FF), wsel),
                      pl.BlockSpec((1, D_FF, D_MODEL), wsel)],
            out_specs=pl.BlockSpec((ROW_BLOCK, D_MODEL // 2), lambda n, be, nv: (n, 0)),
            scratch_shapes=[pltpu.VMEM((D_MODEL, D_FF), BF16), pltpu.VMEM((D_MODEL, D_FF), BF16),
                            pltpu.VMEM((D_FF, D_MODEL), BF16)],
        ),
        out_shape=jax.ShapeDtypeStruct((n_rows, D_MODEL // 2), jnp.int32),
        compiler_params=pltpu.CompilerParams(
            dimension_semantics=("arbitrary",), vmem_limit_bytes=VMEM_LIMIT),
        name="expert_ffn",
    )(be, nv, xs, w_gate, w_up, w_down)


def _combine_kernel(h_ref, ya_ref, yb_ref, gate_ref, g_ref, b_ref, o_ref):
    gate = gate_ref[...]
    a_hi, a_lo = _unpack_bf16_halves(ya_ref[...])
    b_hi, b_lo = _unpack_bf16_halves(yb_ref[...])
    g0, g1 = gate[:, 0:1], gate[:, 1:2]
    ffn = jnp.concatenate([a_hi * g0 + b_hi * g1, a_lo * g0 + b_lo * g1], axis=1)
    o_ref[...] = _layer_norm(DEEPNORM_ALPHA * h_ref[...] + ffn, g_ref[...], b_ref[...])


def _combine(h, y2, gate, g, b, tm=512):
    T = h.shape[0]
    nt = T // tm
    return pl.pallas_call(
        _combine_kernel,
        grid=(nt,),
        in_specs=[pl.BlockSpec((tm, D_MODEL), lambda i: (i, 0)),
                  pl.BlockSpec((tm, D_MODEL // 2), lambda i: (i, 0)),
                  pl.BlockSpec((tm, D_MODEL // 2), lambda i: (i + nt, 0)),
                  pl.BlockSpec((tm, 2), lambda i: (i, 0)),
                  pl.BlockSpec((1, D_MODEL), lambda i: (0, 0)),
                  pl.BlockSpec((1, D_MODEL), lambda i: (0, 0))],
        out_specs=pl.BlockSpec((tm, D_MODEL), lambda i: (i, 0)),
        out_shape=jax.ShapeDtypeStruct((T, D_MODEL), F32),
        compiler_params=pltpu.CompilerParams(
            dimension_semantics=("parallel",), vmem_limit_bytes=VMEM_LIMIT),
        name="combine",
    )(h, y2, y2, gate, g, b)


def kernel(x, w_in, gla_gate_w2, gla_gate_b, gla_norm_g, dil_norm_g, w_out, ln1_g, ln1_b,
           router_coarse_w, router_coarse_b, router_fine_w, router_fine_b,
           expert_w_gate, expert_w_up, expert_w_down, ln2_g, ln2_b):
    B, S, D = x.shape
    T = B * S
    depth = w_in.shape[0]
    slopes = jnp.exp2(-8.0 * jnp.arange(1, DIL_HEADS + 1, dtype=F32) / DIL_HEADS)
    n_rows = 2 * T + N_EXPERTS * ROW_BLOCK
    n_blocks = n_rows // ROW_BLOCK
    n_blocks_pad = -(-n_blocks // (2 * LANES)) * (2 * LANES)
    assert n_blocks_pad == 2 * LANES
    a0 = 1536
    h = x.reshape(T, D)
    for l in range(depth):
        w = w_in[l]
        wm = jnp.concatenate([w[:, :a0], w[:, a0 + GLA_GATE_RANK:]], axis=1).astype(BF16)
        wa = jnp.pad(w[:, a0:a0 + GLA_GATE_RANK], ((0, 0), (0, LANES - GLA_GATE_RANK))).astype(BF16)
        w2 = jnp.pad(gla_gate_w2[l], ((0, LANES - GLA_GATE_RANK), (0, 0)))
        w2h, w2l = _split_bf16(w2)
        q, k, v, r, la, dq, dk, dv = _in_proj(h, wm, wa, w2h, w2l, gla_gate_b[l][None, :])
        o_gla = _gla(q, k, v, r, la, gla_norm_g[l][None, :], B, S)
        g2 = jnp.tile(dil_norm_g[l], 2)[None, :]
        o_dil = _dilated(slopes, dq, dk, dv, g2, B, S)
        wo = w_out[l].astype(BF16)
        rw = jnp.concatenate([router_fine_w[l].reshape(D, N_EXPERTS), router_coarse_w[l]], axis=1)
        rw = jnp.pad(rw, ((0, 0), (0, LANES - N_EXPERTS - N_GROUPS)))
        rwh, rwl = _split_bf16(rw)
        rb = jnp.concatenate([router_fine_b[l].reshape(N_EXPERTS), router_coarse_b[l]])
        rb = jnp.pad(rb, (0, LANES - N_EXPERTS - N_GROUPS))[None, :]
        h1, h1p, eid, gate = _out_proj(o_gla, o_dil, h, wo[:GLA_WIDTH], wo[GLA_WIDTH:],
                                  ln1_g[l][None, :], ln1_b[l][None, :], rwh, rwl, rb)
        dest, be, nv = _positions(eid, n_blocks_pad)
        src_tok = _sc_inverse_rows(dest.reshape(2 * T), n_rows)
        xs = _sc_gather_rows(h1p, src_tok)
        y = _ffn(be.reshape(n_blocks_pad), nv.reshape(LANES)[:1], xs,
                 expert_w_gate[l], expert_w_up[l], expert_w_down[l])
        y2 = _sc_gather_rows(y, dest.T.reshape(2 * T))
        h = _combine(h1, y2, gate, ln2_g[l][None, :], ln2_b[l][None, :])
    return h.reshape(B, S, D)
```

```python
import functools
import math

import jax
import jax.numpy as jnp
import numpy as np
from jax import lax
from jax.experimental import pallas as pl
from jax.experimental.pallas import tpu as pltpu
from jax.experimental.pallas import tpu_sc as plsc

D_MODEL = 1024
GLA_HEADS = 4
GLA_DK = 64
GLA_DV = 128
GLA_KEY_WIDTH = GLA_HEADS * GLA_DK
GLA_WIDTH = GLA_HEADS * GLA_DV
GLA_GATE_RANK = 16
GLA_GATE_TEMP = 16.0
DIL_HEADS = 8
DIL_DH = 64
DIL_WIDTH = DIL_HEADS * DIL_DH
DIL_CONFIGS = ((128, 1), (512, 4), (2048, 16))
DIL_BLOCK = 128
DIL_MAX_R = max(r for _, r in DIL_CONFIGS)
DIL_PAD = DIL_BLOCK * DIL_MAX_R
DIL_UNROLL = 8
N_GROUPS = 4
EXPERTS_PER_GROUP = 8
N_EXPERTS = N_GROUPS * EXPERTS_PER_GROUP
D_FF = 512
DEEPNORM_ALPHA = 2.0 ** 0.25
EPS = 1e-5
LOG2E = math.log2(math.e)
LN2 = math.log(2.0)

LANES = 128
GLA_CHUNK = 64
GLA_UNROLL = 4
SC_LANES = 16
SC_INDEX_WINDOW = 128
SC_GATHER_ROWS = 64
ROW_BLOCK = 256
VMEM_LIMIT = 56 * 1024 * 1024

F32 = jnp.float32
BF16 = jnp.bfloat16


def _dot(a, b):
    return jnp.dot(a, b, preferred_element_type=F32)


def _dot_nt(a, b):
    return lax.dot_general(a, b, (((1,), (1,)), ((), ())), preferred_element_type=F32)


def _dot_tn(a, b):
    return lax.dot_general(a, b, (((0,), (0,)), ((), ())), preferred_element_type=F32)


def _split_bf16(v):
    hi = v.astype(BF16)
    lo = (v - hi.astype(F32)).astype(BF16)
    return hi, lo


def _pack_bf16_halves(v):
    w = v.shape[1] // 2
    hi = lax.bitcast_convert_type(v[:, :w].astype(BF16).astype(F32), jnp.int32)
    lo = lax.bitcast_convert_type(v[:, w:].astype(BF16).astype(F32), jnp.int32)
    return hi | lax.shift_right_logical(lo, 16)


def _unpack_bf16_halves(words):
    hi = lax.bitcast_convert_type(words & jnp.int32(-65536), F32)
    lo = lax.bitcast_convert_type(lax.shift_left(words, 16), F32)
    return hi, lo


def _layer_norm(v, g, b):
    mu = jnp.mean(v, axis=-1, keepdims=True)
    c = v - mu
    var = jnp.mean(c * c, axis=-1, keepdims=True)
    return c * lax.rsqrt(var + EPS) * g + b


def _in_proj_kernel(x_ref, wm_ref, wa_ref, w2h_ref, w2l_ref, gb_ref,
                    q_ref, k_ref, v_ref, r_ref, la_ref, dq_ref, dk_ref, dv_ref):
    xb = x_ref[...].astype(BF16)

    def piece(c0, c1):
        return _dot(xb, wm_ref[:, c0:c1])

    q_ref[...] = (piece(0, 256) * (GLA_DK ** -0.5)).astype(BF16)
    k_ref[...] = piece(256, 512).astype(BF16)
    v_ref[...] = piece(512, 1024).astype(BF16)
    r_ref[...] = piece(1024, 1536).astype(BF16)
    dq_ref[...] = (piece(1536, 2048) * (DIL_DH ** -0.5 * LOG2E)).astype(BF16)
    dk_ref[...] = piece(2048, 2560).astype(BF16)
    dv_ref[...] = piece(2560, 3072).astype(BF16)
    ga = _dot(xb, wa_ref[...])
    ga_hi, ga_lo = _split_bf16(ga)
    z = _dot(ga_hi, w2h_ref[...]) + _dot(ga_lo, w2h_ref[...]) + _dot(ga_hi, w2l_ref[...]) + gb_ref[...]
    log_sig = jnp.minimum(z, 0.0) - jnp.log1p(jnp.exp(-jnp.abs(z)))
    la_ref[...] = log_sig * (1.0 / GLA_GATE_TEMP)


def _in_proj(x2, wm, wa, w2h, w2l, gb, tm=512):
    T = x2.shape[0]
    row = lambda w: pl.BlockSpec((tm, w), lambda i: (i, 0))
    full = lambda a: pl.BlockSpec(a.shape, lambda i: (0,) * a.ndim)
    outs = [(256, BF16), (256, BF16), (512, BF16), (512, BF16), (256, F32),
            (512, BF16), (512, BF16), (512, BF16)]
    return pl.pallas_call(
        _in_proj_kernel,
        grid=(T // tm,),
        in_specs=[row(D_MODEL), full(wm), full(wa), full(w2h), full(w2l), full(gb)],
        out_specs=[row(w) for w, _ in outs],
        out_shape=[jax.ShapeDtypeStruct((T, w), dt) for w, dt in outs],
        compiler_params=pltpu.CompilerParams(
            dimension_semantics=("parallel",), vmem_limit_bytes=VMEM_LIMIT),
        name="in_proj",
    )(x2, wm, wa, w2h, w2l, gb)


def _gla_kernel(q_ref, k_ref, v_ref, r_ref, la_ref, g_ref, o_ref, s_ref, *, seq_block):
    C = GLA_CHUNK

    @pl.when(pl.program_id(1) == 0)
    def _():
        s_ref[...] = jnp.zeros_like(s_ref)

    ri = lax.broadcasted_iota(jnp.int32, (C, C), 0)
    ci = lax.broadcasted_iota(jnp.int32, (C, C), 1)
    causal = ci <= ri
    tri = causal.astype(BF16)
    ones_cl = jnp.ones((C, LANES), BF16)
    lane_k = lax.broadcasted_iota(jnp.int32, (1, GLA_KEY_WIDTH), 1) // GLA_DK
    head_masks = [(lane_k == h).astype(F32) for h in range(GLA_HEADS)]
    srow = lax.broadcasted_iota(jnp.int32, (GLA_KEY_WIDTH, GLA_WIDTH), 0) // GLA_DK
    scol = lax.broadcasted_iota(jnp.int32, (GLA_KEY_WIDTH, GLA_WIDTH), 1) // GLA_DV
    state_mask = (srow == scol).astype(F32)
    g = g_ref[...]

    def chunk(c, carry):
        rows = pl.ds(pl.multiple_of(c * C, C), C)
        la = la_ref[rows, :]
        la_hi, la_lo = _split_bf16(la)
        b = _dot(tri, la_hi) + _dot(tri, la_lo)
        b_last = b[C - 1:C, :]
        q = q_ref[rows, :].astype(F32)
        k = k_ref[rows, :].astype(F32)
        v = v_ref[rows, :]
        qd = q * jnp.exp(b)
        kd = (k * jnp.exp(-b)).astype(BF16)
        ke = (k * jnp.exp(b_last - b)).astype(BF16)
        state = s_ref[...]
        o_inter = _dot(qd.astype(BF16), state.astype(BF16))
        outs = []
        for h in range(GLA_HEADS):
            a = _dot_nt((qd * head_masks[h]).astype(BF16), kd)
            a = jnp.where(causal, a, 0.0).astype(BF16)
            cols = slice(h * GLA_DV, (h + 1) * GLA_DV)
            o = _dot(a, v[:, cols]) + o_inter[:, cols]
            o = o * lax.rsqrt(jnp.mean(o * o, axis=-1, keepdims=True) + EPS) * g
            outs.append(o)
        o_all = jnp.concatenate(outs, axis=-1)
        rr = r_ref[rows, :].astype(F32)
        o_ref[rows, :] = (o_all * (rr * jax.nn.sigmoid(rr))).astype(BF16)
        tot = _dot_tn(la_hi, ones_cl) + _dot_tn(la_lo, ones_cl)
        dec = jnp.exp(tot)
        upd = _dot_tn(ke, v) * state_mask
        for h in range(GLA_HEADS):
            cols = slice(h * GLA_DV, (h + 1) * GLA_DV)
            s_ref[:, cols] = state[:, cols] * dec + upd[:, cols]
        return carry

    lax.fori_loop(0, seq_block // C, chunk, 0, unroll=GLA_UNROLL)


def _gla(q, k, v, r, la, g, batch, seq, seq_block=1024):
    nsb = seq // seq_block
    row = lambda w: pl.BlockSpec((seq_block, w), lambda b, s: (b * nsb + s, 0))
    return pl.pallas_call(
        functools.partial(_gla_kernel, seq_block=seq_block),
        grid=(batch, nsb),
        in_specs=[row(256), row(256), row(512), row(512), row(256),
                  pl.BlockSpec((1, GLA_DV), lambda b, s: (0, 0))],
        out_specs=row(512),
        out_shape=jax.ShapeDtypeStruct((batch * seq, GLA_WIDTH), BF16),
        scratch_shapes=[pltpu.VMEM((GLA_KEY_WIDTH, GLA_WIDTH), F32)],
        compiler_params=pltpu.CompilerParams(
            dimension_semantics=("parallel", "arbitrary"), vmem_limit_bytes=VMEM_LIMIT),
        name="gla",
    )(q, k, v, r, la, g)


def _dil_kernel(slope_ref, q_ref, k_ref, v_ref, g_ref, o_ref,
                qf, kf, vf, kd, va, vb, oc, lc, *, seq):
    B = DIL_BLOCK
    U = DIL_UNROLL
    pair = pl.program_id(1)
    qf[...] = q_ref[...].astype(F32)
    kf[...] = k_ref[...].astype(F32)
    vf[...] = v_ref[...].astype(F32)

    lane = lax.broadcasted_iota(jnp.int32, (1, LANES), 1)
    first = lane < DIL_DH
    ii = lax.broadcasted_iota(jnp.int32, (B, B), 0)
    jj = lax.broadcasted_iota(jnp.int32, (B, B), 1)
    upper = jj > ii
    eye = jj == ii
    dist = jnp.bitwise_and(ii - jj, B - 1).astype(F32)
    neg = jnp.float32(-jnp.inf)
    neg_tile = jnp.full((B, B), neg, F32)
    zero_tile = jnp.zeros((B, LANES), BF16)

    for cfg, (window, r) in enumerate(DIL_CONFIGS):
        nb = seq // r // B
        cs = nb + 1
        bias_prev, bias_cur = [], []
        for hh in range(2):
            slope = slope_ref[2 * pair + hh] * (float(r) * LOG2E)
            bias = dist * (-slope)
            bias_prev.append(jnp.where(upper, bias, jnp.where(eye, -slope * float(B), neg)))
            bias_cur.append(jnp.where(upper, neg, bias))

        for c in range(r):
            rows0 = slice(c * cs * B, (c * cs + 1) * B)
            kd[rows0, :] = zero_tile
            va[rows0, :] = zero_tile
            vb[rows0, :] = zero_tile

        def prep(t4, carry, r=r, nb=nb, cs=cs):
            for j in range(4):
                t = t4 * 4 + j
                c = t // nb
                n = t % nb
                start = c + n * (B * r)
                rows = pl.ds(start, B, stride=r) if r > 1 else pl.ds(pl.multiple_of(start, B), B)
                dst = pl.ds(pl.multiple_of((c * cs + 1 + n) * B, B), B)
                kd[dst, :] = kf[rows, :].astype(BF16)
                v = vf[rows, :]
                va[dst, :] = jnp.where(first, v, 1.0).astype(BF16)
                vb[dst, :] = jnp.where(first, 1.0, v).astype(BF16)
            return carry

        lax.fori_loop(0, seq // B // 4, prep, 0)

        def block(it, has_prev, cfg=cfg, r=r, nb=nb, cs=cs, bias_prev=bias_prev, bias_cur=bias_cur):
            c = it // nb
            n = it % nb
            start = c + n * (B * r)
            rows = pl.ds(start, B, stride=r) if r > 1 else pl.ds(pl.multiple_of(start, B), B)
            kv = pl.ds(pl.multiple_of((c * cs + n) * B, B), 2 * B)
            q = qf[rows, :]
            kcat = kd[kv, :]
            acc, mx = [], []
            for hh in range(2):
                hmask = first if hh == 0 else jnp.logical_not(first)
                qh = jnp.where(hmask, q, 0.0).astype(BF16)
                s2 = _dot_nt(qh, kcat)
                if has_prev is True:
                    bp = bias_prev[hh]
                elif has_prev is False:
                    bp = neg_tile
                else:
                    bp = jnp.where(has_prev, bias_prev[hh], neg)
                s_prev = s2[:, 0:B] + bp
                s_cur = s2[:, B:2 * B] + bias_cur[hh]
                m = jnp.max(jnp.maximum(s_prev, s_cur), axis=-1, keepdims=True)
                pcat = jnp.concatenate([jnp.exp2(s_prev - m), jnp.exp2(s_cur - m)], axis=1).astype(BF16)
                vals = va[kv, :] if hh == 0 else vb[kv, :]
                acc.append(_dot(pcat, vals))
                mx.append(m)
            num = jnp.where(first, acc[0], acc[1])
            den = pltpu.roll(jnp.where(first, acc[1], acc[0]), DIL_DH, axis=1)
            oc[cfg, rows, :] = num * (1.0 / den)
            lc[cfg, rows, :] = (jnp.where(first, mx[0], mx[1]) + jnp.log2(den)) * LN2

        def body(t, carry, block=block, nb=nb):
            for u in range(U):
                if nb % U == 0:
                    has_prev = True if u > 0 else (t * U) % nb > 0
                else:
                    assert U % nb == 0
                    has_prev = (u % nb) > 0
                block(t * U + u, has_prev)
            return carry

        lax.fori_loop(0, seq // B // U, body, 0)

    g = g_ref[...]
    CH = 512

    def mix(i, carry):
        rows = pl.ds(pl.multiple_of(i * CH, CH), CH)
        l0, l1, l2 = lc[0, rows, :], lc[1, rows, :], lc[2, rows, :]
        m = jnp.maximum(jnp.maximum(l0, l1), l2)
        e0, e1, e2 = jnp.exp(l0 - m), jnp.exp(l1 - m), jnp.exp(l2 - m)
        den = e0 + e1 + e2
        o = (e0 / den) * oc[0, rows, :] + (e1 / den) * oc[1, rows, :] + (e2 / den) * oc[2, rows, :]
        sq = o * o
        ms_a = jnp.sum(jnp.where(first, sq, 0.0), axis=-1, keepdims=True) * (1.0 / DIL_DH)
        ms_b = jnp.sum(jnp.where(first, 0.0, sq), axis=-1, keepdims=True) * (1.0 / DIL_DH)
        ms = jnp.where(first, ms_a, ms_b)
        o_ref[rows, :] = (o * lax.rsqrt(ms + EPS) * g).astype(BF16)
        return carry

    lax.fori_loop(0, seq // CH, mix, 0)


def _dilated(slopes, dq, dk, dv, g2, batch, seq):
    blk = pl.BlockSpec((seq, LANES), lambda b, p, s: (b, p))
    return pl.pallas_call(
        functools.partial(_dil_kernel, seq=seq),
        grid_spec=pltpu.PrefetchScalarGridSpec(
            num_scalar_prefetch=1,
            grid=(batch, DIL_WIDTH // LANES),
            in_specs=[blk, blk, blk, pl.BlockSpec((1, LANES), lambda b, p, s: (0, 0))],
            out_specs=blk,
            scratch_shapes=[pltpu.VMEM((seq, LANES), F32)] * 3
                           + [pltpu.VMEM((seq + DIL_PAD, LANES), BF16)] * 3
                           + [pltpu.VMEM((3, seq, LANES), F32)] * 2,
        ),
        out_shape=jax.ShapeDtypeStruct((batch * seq, DIL_WIDTH), BF16),
        compiler_params=pltpu.CompilerParams(
            dimension_semantics=("parallel", "parallel"), vmem_limit_bytes=VMEM_LIMIT),
        name="dilated",
    )(slopes, dq, dk, dv, g2)


def _out_proj_kernel(og_ref, od_ref, x_ref, wg_ref, wd_ref, g_ref, b_ref,
                     rwh_ref, rwl_ref, rb_ref, h_ref, hp_ref, eid_ref, gate_ref):
    mix = _dot(og_ref[...], wg_ref[...]) + _dot(od_ref[...], wd_ref[...])
    h = _layer_norm(DEEPNORM_ALPHA * x_ref[...] + mix, g_ref[...], b_ref[...])
    h_ref[...] = h
    hp_ref[...] = _pack_bf16_halves(h)
    h_hi, h_lo = _split_bf16(h)
    logits = (_dot(h_hi, rwh_ref[...]) + _dot(h_lo, rwh_ref[...]) + _dot(h_hi, rwl_ref[...])
              + rb_ref[...])
    lane = lax.broadcasted_iota(jnp.int32, logits.shape, 1)
    lane_f = lane.astype(F32)
    neg = jnp.float32(-jnp.inf)
    big = jnp.float32(1e9)
    is_coarse = jnp.logical_and(lane >= N_EXPERTS, lane < N_EXPERTS + N_GROUPS)
    coarse = jnp.where(is_coarse, logits, neg)
    cmax = jnp.max(coarse, axis=-1, keepdims=True)
    g_idx = jnp.min(jnp.where(coarse == cmax, lane_f, big), axis=-1, keepdims=True) - N_EXPERTS
    p_group = 1.0 / jnp.sum(jnp.exp(coarse - cmax), axis=-1, keepdims=True)
    lo = g_idx * EXPERTS_PER_GROUP
    in_group = jnp.logical_and(lane_f >= lo, lane_f < lo + EXPERTS_PER_GROUP)
    fine = jnp.where(in_group, logits, neg)
    v1 = jnp.max(fine, axis=-1, keepdims=True)
    i1 = jnp.min(jnp.where(fine == v1, lane_f, big), axis=-1, keepdims=True)
    fine2 = jnp.where(lane_f == i1, neg, fine)
    v2 = jnp.max(fine2, axis=-1, keepdims=True)
    i2 = jnp.min(jnp.where(fine2 == v2, lane_f, big), axis=-1, keepdims=True)
    e2 = jnp.exp(v2 - v1)
    den = 1.0 + e2
    gate1 = p_group * (1.0 / den)
    gate2 = p_group * (e2 / den)
    lane2 = lax.broadcasted_iota(jnp.int32, (logits.shape[0], 2), 1)
    eid_ref[...] = jnp.where(lane2 == 0, i1, i2).astype(jnp.int32)
    gate_ref[...] = jnp.where(lane2 == 0, gate1, gate2)


def _out_proj(og, od, x2, wg, wd, g, b, rwh, rwl, rb, tm=512):
    T = x2.shape[0]
    row = lambda w: pl.BlockSpec((tm, w), lambda i: (i, 0))
    full = lambda a: pl.BlockSpec(a.shape, lambda i: (0,) * a.ndim)
    return pl.pallas_call(
        _out_proj_kernel,
        grid=(T // tm,),
        in_specs=[row(512), row(512), row(D_MODEL), full(wg), full(wd), full(g), full(b),
                  full(rwh), full(rwl), full(rb)],
        out_specs=[row(D_MODEL), row(D_MODEL // 2), row(2), row(2)],
        out_shape=[jax.ShapeDtypeStruct((T, D_MODEL), F32),
                   jax.ShapeDtypeStruct((T, D_MODEL // 2), jnp.int32),
                   jax.ShapeDtypeStruct((T, 2), jnp.int32),
                   jax.ShapeDtypeStruct((T, 2), F32)],
        compiler_params=pltpu.CompilerParams(
            dimension_semantics=("parallel",), vmem_limit_bytes=VMEM_LIMIT),
        name="out_proj_router",
    )(og, od, x2, wg, wd, g, b, rwh, rwl, rb)


def _positions_kernel(eid_ref, dest_ref, be_ref, nv_ref, carry_ref, sp_ref, *, tb):
    phase = pl.program_id(0)
    i = pl.program_id(1)
    lane = lax.broadcasted_iota(jnp.int32, (tb, LANES), 1)
    eid = eid_ref[...]
    oh1 = lane == eid[:, 0:1]
    oh2 = lane == eid[:, 1:2]
    oh = jnp.logical_or(oh1, oh2).astype(BF16)

    @pl.when(jnp.logical_and(phase == 0, i == 0))
    def _():
        carry_ref[...] = jnp.zeros_like(carry_ref)

    @pl.when(phase == 0)
    def _():
        ones_r = jnp.ones((8, tb), BF16)
        carry_ref[...] += _dot(ones_r, oh)

    @pl.when(jnp.logical_and(phase == 1, i == 0))
    def _():
        shift = int(math.log2(ROW_BLOCK))
        nb_row = ((carry_ref[...].astype(jnp.int32) + (ROW_BLOCK - 1)) >> shift)
        r = lax.broadcasted_iota(jnp.int32, (LANES, LANES), 0)
        c = lax.broadcasted_iota(jnp.int32, (LANES, LANES), 1)
        excl = (r < c).astype(BF16)
        start_blk = _dot(nb_row.astype(F32).astype(BF16), excl)
        sp_ref[...] = start_blk * float(ROW_BLOCK)
        be_ref[...] = jnp.concatenate([start_blk[0:1, :].astype(jnp.int32), nb_row[0:1, :]], axis=1)
        lane1 = lax.broadcasted_iota(jnp.int32, (1, LANES), 1)
        total = jnp.sum(jnp.where(lane1 < N_EXPERTS, nb_row[0:1, :].astype(F32), 0.0), axis=-1, keepdims=True)
        nv_ref[...] = jnp.broadcast_to(total, (1, LANES)).astype(jnp.int32)
        carry_ref[...] = jnp.zeros_like(carry_ref)

    @pl.when(phase == 1)
    def _():
        r = lax.broadcasted_iota(jnp.int32, (tb, tb), 0)
        c = lax.broadcasted_iota(jnp.int32, (tb, tb), 1)
        strict = (c < r).astype(BF16)
        before = _dot(strict, oh) + carry_ref[0:1, :] + sp_ref[0:1, :]
        d1 = jnp.sum(jnp.where(oh1, before, 0.0), axis=-1, keepdims=True)
        d2 = jnp.sum(jnp.where(oh2, before, 0.0), axis=-1, keepdims=True)
        lane2 = lax.broadcasted_iota(jnp.int32, (tb, 2), 1)
        dest_ref[...] = jnp.where(lane2 == 0, d1, d2).astype(jnp.int32)
        carry_ref[...] += _dot(jnp.ones((8, tb), BF16), oh)


def _positions(eid, n_blocks_pad, tb=512):
    T = eid.shape[0]
    nb = T // tb
    return pl.pallas_call(
        functools.partial(_positions_kernel, tb=tb),
        grid=(2, nb),
        in_specs=[pl.BlockSpec((tb, 2), lambda p, i: (i, 0))],
        out_specs=[pl.BlockSpec((tb, 2), lambda p, i: (i * p, 0)),
                   pl.BlockSpec((1, n_blocks_pad), lambda p, i: (0, 0)),
                   pl.BlockSpec((1, LANES), lambda p, i: (0, 0))],
        out_shape=[jax.ShapeDtypeStruct((T, 2), jnp.int32),
                   jax.ShapeDtypeStruct((1, n_blocks_pad), jnp.int32),
                   jax.ShapeDtypeStruct((1, LANES), jnp.int32)],
        scratch_shapes=[pltpu.VMEM((8, LANES), F32), pltpu.VMEM((8, LANES), F32)],
        compiler_params=pltpu.CompilerParams(dimension_semantics=("arbitrary", "arbitrary")),
        name="positions",
    )(eid)


def _sc_gather_rows(table, idx):
    n = idx.shape[0]
    d = table.shape[1]
    info = plsc.get_sparse_core_info()
    nc, ns = info.num_cores, info.num_subcores
    per_w = n // (nc * ns)
    assert per_w * nc * ns == n and per_w % SC_INDEX_WINDOW == 0
    mesh = plsc.VectorSubcoreMesh(core_axis_name="core", subcore_axis_name="subcore")
    nsub = SC_INDEX_WINDOW // SC_GATHER_ROWS

    @functools.partial(
        pl.kernel, out_type=jax.ShapeDtypeStruct((n, d), table.dtype), mesh=mesh,
        scratch_types=[pltpu.VMEM((SC_INDEX_WINDOW,), jnp.int32),
                       pltpu.VMEM((2, SC_GATHER_ROWS, d), table.dtype),
                       pltpu.SemaphoreType.DMA((2,)), pltpu.SemaphoreType.DMA((2,))],
        name="sc_gather_rows")
    def gather(x_hbm, i_hbm, o_hbm, idx_v, buf, gsem, wsem):
        wid = lax.axis_index("subcore") * nc + lax.axis_index("core")
        base = wid * per_w

        def gather_copy(s):
            rows = idx_v.at[pl.ds(s * SC_GATHER_ROWS, SC_GATHER_ROWS)]
            return pltpu.make_async_copy(x_hbm.at[rows], buf.at[s % 2], gsem.at[s % 2])

        def write_copy(off, s):
            dst = o_hbm.at[pl.ds(off + s * SC_GATHER_ROWS, SC_GATHER_ROWS)]
            return pltpu.make_async_copy(buf.at[s % 2], dst, wsem.at[s % 2])

        @pl.loop(0, per_w // SC_INDEX_WINDOW)
        def _(j):
            off = base + j * SC_INDEX_WINDOW
            pltpu.sync_copy(i_hbm.at[pl.ds(off, SC_INDEX_WINDOW)], idx_v)
            gather_copy(0).start()
            for s in range(nsub):
                gather_copy(s).wait()
                write_copy(off, s).start()
                if s >= 1:
                    write_copy(off, s - 1).wait()
                if s + 1 < nsub:
                    gather_copy(s + 1).start()
            write_copy(off, nsub - 1).wait()

    return gather(table, idx)


def _sc_inverse_rows(dest_flat, n_rows, chunk=2048):
    n = dest_flat.shape[0]
    n_tokens = n // 2
    assert n_rows <= 3 * n_tokens
    nc = plsc.get_sparse_core_info().num_cores
    mesh = plsc.VectorSubcoreMesh(core_axis_name="core", subcore_axis_name="subcore")

    @functools.partial(
        pl.kernel, out_type=jax.ShapeDtypeStruct((n_rows,), jnp.int32), mesh=mesh,
        scratch_types=[pltpu.VMEM((n_rows,), jnp.int32), pltpu.VMEM((chunk,), jnp.int32)],
        compiler_params=pltpu.CompilerParams(needs_layout_passes=False),
        name="sc_inverse_rows")
    def inverse(d_hbm, o_hbm, inv_v, d_v):
        wid = lax.axis_index("subcore") * nc + lax.axis_index("core")

        @pl.when(wid == 0)
        def _():
            lanes = lax.iota(jnp.int32, SC_LANES)

            @pl.loop(0, n_rows // SC_LANES)
            def _(i):
                r = lanes + i * SC_LANES
                r = jnp.where(r >= n_tokens, r - n_tokens, r)
                inv_v[pl.ds(i * SC_LANES, SC_LANES)] = jnp.where(r >= n_tokens, r - n_tokens, r)

            @pl.loop(0, n // chunk)
            def _(c):
                pltpu.sync_copy(d_hbm.at[pl.ds(c * chunk, chunk)], d_v)

                @pl.loop(0, chunk // SC_LANES)
                def _(j):
                    rows = d_v[pl.ds(j * SC_LANES, SC_LANES)]
                    pair = lax.iota(jnp.int32, SC_LANES) + (c * chunk + j * SC_LANES)
                    plsc.store_scatter(inv_v, [rows], lax.shift_right_logical(pair, 1))

            pltpu.sync_copy(inv_v, o_hbm)

    return inverse(dest_flat)


def _ffn_kernel(first_ref, count_ref, nv_ref, wg_ref, wu_ref, wd_ref, xs_hbm, y_hbm,
                wgb, wub, wdb, xbuf, ybuf, isem, osem, *, n_blocks):
    e = pl.program_id(0)
    first = first_ref[e]
    count = count_ref[e]

    def rows_of(j):
        return pl.ds(pl.multiple_of((first + j) * ROW_BLOCK, ROW_BLOCK), ROW_BLOCK)

    def in_copy(j, slot):
        return pltpu.make_async_copy(xs_hbm.at[rows_of(j)], xbuf.at[slot], isem.at[slot])

    def out_copy(j, slot):
        return pltpu.make_async_copy(ybuf.at[slot], y_hbm.at[rows_of(j)], osem.at[slot])

    @pl.when(count > 0)
    def _():
        in_copy(0, 0).start()
        wgb[...] = wg_ref[0].astype(BF16)
        wub[...] = wu_ref[0].astype(BF16)
        wdb[...] = wd_ref[0].astype(BF16)

        def body(j, carry):
            slot = j % 2
            in_copy(j, slot).wait()

            @pl.when(j + 1 < count)
            def _():
                in_copy(j + 1, 1 - slot).start()

            @pl.when(j >= 2)
            def _():
                out_copy(j - 2, slot).wait()

            x_hi, x_lo = _unpack_bf16_halves(xbuf[slot])
            xb = jnp.concatenate([x_hi.astype(BF16), x_lo.astype(BF16)], axis=1)
            a = _dot(xb, wgb[...])
            u = _dot(xb, wub[...])
            hid = (a * jax.nn.sigmoid(a) * u).astype(BF16)
            ybuf[slot] = _pack_bf16_halves(_dot(hid, wdb[...]))
            out_copy(j, slot).start()
            return carry

        lax.fori_loop(0, count, body, 0)

        @pl.when(count >= 2)
        def _():
            out_copy(count - 2, count % 2).wait()

        out_copy(count - 1, (count - 1) % 2).wait()

    @pl.when(e == pl.num_programs(0) - 1)
    def _():
        ybuf[0] = jnp.zeros((ROW_BLOCK, D_MODEL // 2), jnp.int32)

        def fill(b, carry):
            dst = y_hbm.at[pl.ds(pl.multiple_of(b * ROW_BLOCK, ROW_BLOCK), ROW_BLOCK)]
            pltpu.sync_copy(ybuf.at[0], dst)
            return carry

        lax.fori_loop(nv_ref[0], n_blocks, fill, 0)


def _ffn(first_blk, n_blk, nv, xs, w_gate, w_up, w_down):
    n_rows = xs.shape[0]
    n_blocks = n_rows // ROW_BLOCK
    wspec = lambda shape: pl.BlockSpec((1,) + shape, lambda e, *_: (e, 0, 0))
    return pl.pallas_call(
        functools.partial(_ffn_kernel, n_blocks=n_blocks),
        grid_spec=pltpu.PrefetchScalarGridSpec(
            num_scalar_prefetch=3,
            grid=(N_EXPERTS,),
            in_specs=[wspec((D_MODEL, D_FF)), wspec((D_MODEL, D_FF)), wspec((D_FF, D_MODEL)),
                      pl.BlockSpec(memory_space=pl.ANY)],
            out_specs=pl.BlockSpec(memory_space=pl.ANY),
            scratch_shapes=[pltpu.VMEM((D_MODEL, D_FF), BF16), pltpu.VMEM((D_MODEL, D_FF), BF16),
                            pltpu.VMEM((D_FF, D_MODEL), BF16),
                            pltpu.VMEM((2, ROW_BLOCK, D_MODEL // 2), jnp.int32),
                            pltpu.VMEM((2, ROW_BLOCK, D_MODEL // 2), jnp.int32),
                            pltpu.SemaphoreType.DMA((2,)), pltpu.SemaphoreType.DMA((2,))],
        ),
        out_shape=jax.ShapeDtypeStruct((n_rows, D_MODEL // 2), jnp.int32),
        compiler_params=pltpu.CompilerParams(
            dimension_semantics=("arbitrary",), vmem_limit_bytes=VMEM_LIMIT),
        name="expert_ffn",
    )(first_blk, n_blk, nv, w_gate, w_up, w_down, xs)


def _combine_kernel(h_ref, ya_ref, yb_ref, gate_ref, g_ref, b_ref, o_ref):
    gate = gate_ref[...]
    a_hi, a_lo = _unpack_bf16_halves(ya_ref[...])
    b_hi, b_lo = _unpack_bf16_halves(yb_ref[...])
    g0, g1 = gate[:, 0:1], gate[:, 1:2]
    ffn = jnp.concatenate([a_hi * g0 + b_hi * g1, a_lo * g0 + b_lo * g1], axis=1)
    o_ref[...] = _layer_norm(DEEPNORM_ALPHA * h_ref[...] + ffn, g_ref[...], b_ref[...])


def _combine(h, y2, gate, g, b, tm=512):
    T = h.shape[0]
    nt = T // tm
    return pl.pallas_call(
        _combine_kernel,
        grid=(nt,),
        in_specs=[pl.BlockSpec((tm, D_MODEL), lambda i: (i, 0)),
                  pl.BlockSpec((tm, D_MODEL // 2), lambda i: (i, 0)),
                  pl.BlockSpec((tm, D_MODEL // 2), lambda i: (i + nt, 0)),
                  pl.BlockSpec((tm, 2), lambda i: (i, 0)),
                  pl.BlockSpec((1, D_MODEL), lambda i: (0, 0)),
                  pl.BlockSpec((1, D_MODEL), lambda i: (0, 0))],
        out_specs=pl.BlockSpec((tm, D_MODEL), lambda i: (i, 0)),
        out_shape=jax.ShapeDtypeStruct((T, D_MODEL), F32),
        compiler_params=pltpu.CompilerParams(
            dimension_semantics=("parallel",), vmem_limit_bytes=VMEM_LIMIT),
        name="combine",
    )(h, y2, y2, gate, g, b)


def kernel(x, w_in, gla_gate_w2, gla_gate_b, gla_norm_g, dil_norm_g, w_out, ln1_g, ln1_b,
           router_coarse_w, router_coarse_b, router_fine_w, router_fine_b,
           expert_w_gate, expert_w_up, expert_w_down, ln2_g, ln2_b):
    B, S, D = x.shape
    T = B * S
    depth = w_in.shape[0]
    slopes = jnp.exp2(-8.0 * jnp.arange(1, DIL_HEADS + 1, dtype=F32) / DIL_HEADS)
    n_rows = 2 * T + N_EXPERTS * ROW_BLOCK
    n_blocks = n_rows // ROW_BLOCK
    n_blocks_pad = -(-n_blocks // (2 * LANES)) * (2 * LANES)
    assert n_blocks_pad == 2 * LANES
    a0 = 1536
    h = x.reshape(T, D)
    for l in range(depth):
        w = w_in[l]
        wm = jnp.concatenate([w[:, :a0], w[:, a0 + GLA_GATE_RANK:]], axis=1).astype(BF16)
        wa = jnp.pad(w[:, a0:a0 + GLA_GATE_RANK], ((0, 0), (0, LANES - GLA_GATE_RANK))).astype(BF16)
        w2 = jnp.pad(gla_gate_w2[l], ((0, LANES - GLA_GATE_RANK), (0, 0)))
        w2h, w2l = _split_bf16(w2)
        q, k, v, r, la, dq, dk, dv = _in_proj(h, wm, wa, w2h, w2l, gla_gate_b[l][None, :])
        o_gla = _gla(q, k, v, r, la, gla_norm_g[l][None, :], B, S)
        g2 = jnp.tile(dil_norm_g[l], 2)[None, :]
        o_dil = _dilated(slopes, dq, dk, dv, g2, B, S)
        wo = w_out[l].astype(BF16)
        rw = jnp.concatenate([router_fine_w[l].reshape(D, N_EXPERTS), router_coarse_w[l]], axis=1)
        rw = jnp.pad(rw, ((0, 0), (0, LANES - N_EXPERTS - N_GROUPS)))
        rwh, rwl = _split_bf16(rw)
        rb = jnp.concatenate([router_fine_b[l].reshape(N_EXPERTS), router_coarse_b[l]])
        rb = jnp.pad(rb, (0, LANES - N_EXPERTS - N_GROUPS))[None, :]
        h1, h1p, eid, gate = _out_proj(o_gla, o_dil, h, wo[:GLA_WIDTH], wo[GLA_WIDTH:],
                                  ln1_g[l][None, :], ln1_b[l][None, :], rwh, rwl, rb)
        dest, be, nv = _positions(eid, n_blocks_pad)
        src_tok = _sc_inverse_rows(dest.reshape(2 * T), n_rows)
        xs = _sc_gather_rows(h1p, src_tok)
        be = be.reshape(n_blocks_pad)
        y = _ffn(be[:N_EXPERTS], be[LANES:LANES + N_EXPERTS], nv.reshape(LANES)[:1], xs,
                 expert_w_gate[l], expert_w_up[l], expert_w_down[l])
        y2 = _sc_gather_rows(y, dest.T.reshape(2 * T))
        h = _combine(h1, y2, gate, ln2_g[l][None, :], ln2_b[l][None, :])
    return h.reshape(B, S, D)
```

```python
import functools
import math

import jax
import jax.numpy as jnp
import numpy as np
from jax import lax
from jax.experimental import pallas as pl
from jax.experimental.pallas import tpu as pltpu
from jax.experimental.pallas import tpu_sc as plsc

D_MODEL = 1024
GLA_HEADS = 4
GLA_DK = 64
GLA_DV = 128
GLA_KEY_WIDTH = GLA_HEADS * GLA_DK
GLA_WIDTH = GLA_HEADS * GLA_DV
GLA_GATE_RANK = 16
GLA_GATE_TEMP = 16.0
DIL_HEADS = 8
DIL_DH = 64
DIL_WIDTH = DIL_HEADS * DIL_DH
DIL_CONFIGS = ((128, 1), (512, 4), (2048, 16))
DIL_BLOCK = 128
DIL_MAX_R = max(r for _, r in DIL_CONFIGS)
DIL_PAD = DIL_BLOCK * DIL_MAX_R
DIL_UNROLL = 8
N_GROUPS = 4
EXPERTS_PER_GROUP = 8
N_EXPERTS = N_GROUPS * EXPERTS_PER_GROUP
D_FF = 512
DEEPNORM_ALPHA = 2.0 ** 0.25
EPS = 1e-5
LOG2E = math.log2(math.e)
LN2 = math.log(2.0)

LANES = 128
GLA_CHUNK = 64
GLA_UNROLL = 4
SC_LANES = 16
SC_INDEX_WINDOW = 128
SC_GATHER_ROWS = 64
FFN_GROUP = 2
ROW_BLOCK = 256
VMEM_LIMIT = 56 * 1024 * 1024

F32 = jnp.float32
BF16 = jnp.bfloat16


def _dot(a, b):
    return jnp.dot(a, b, preferred_element_type=F32)


def _dot_nt(a, b):
    return lax.dot_general(a, b, (((1,), (1,)), ((), ())), preferred_element_type=F32)


def _dot_tn(a, b):
    return lax.dot_general(a, b, (((0,), (0,)), ((), ())), preferred_element_type=F32)


def _split_bf16(v):
    hi = v.astype(BF16)
    lo = (v - hi.astype(F32)).astype(BF16)
    return hi, lo


def _pack_bf16_halves(v):
    w = v.shape[1] // 2
    hi = lax.bitcast_convert_type(v[:, :w].astype(BF16).astype(F32), jnp.int32)
    lo = lax.bitcast_convert_type(v[:, w:].astype(BF16).astype(F32), jnp.int32)
    return hi | lax.shift_right_logical(lo, 16)


def _unpack_bf16_halves(words):
    hi = lax.bitcast_convert_type(words & jnp.int32(-65536), F32)
    lo = lax.bitcast_convert_type(lax.shift_left(words, 16), F32)
    return hi, lo


def _layer_norm(v, g, b):
    mu = jnp.mean(v, axis=-1, keepdims=True)
    c = v - mu
    var = jnp.mean(c * c, axis=-1, keepdims=True)
    return c * lax.rsqrt(var + EPS) * g + b


def _in_proj_kernel(x_ref, wm_ref, wa_ref, w2h_ref, w2l_ref, gb_ref,
                    q_ref, k_ref, v_ref, r_ref, la_ref, dq_ref, dk_ref, dv_ref):
    xb = x_ref[...].astype(BF16)

    def piece(c0, c1):
        return _dot(xb, wm_ref[:, c0:c1])

    q_ref[...] = (piece(0, 256) * (GLA_DK ** -0.5)).astype(BF16)
    k_ref[...] = piece(256, 512).astype(BF16)
    v_ref[...] = piece(512, 1024).astype(BF16)
    r_ref[...] = piece(1024, 1536).astype(BF16)
    dq_ref[...] = (piece(1536, 2048) * (DIL_DH ** -0.5 * LOG2E)).astype(BF16)
    dk_ref[...] = piece(2048, 2560).astype(BF16)
    dv_ref[...] = piece(2560, 3072).astype(BF16)
    ga = _dot(xb, wa_ref[...])
    ga_hi, ga_lo = _split_bf16(ga)
    z = _dot(ga_hi, w2h_ref[...]) + _dot(ga_lo, w2h_ref[...]) + _dot(ga_hi, w2l_ref[...]) + gb_ref[...]
    log_sig = jnp.minimum(z, 0.0) - jnp.log1p(jnp.exp(-jnp.abs(z)))
    la_ref[...] = log_sig * (1.0 / GLA_GATE_TEMP)


def _in_proj(x2, wm, wa, w2h, w2l, gb, tm=1024):
    T = x2.shape[0]
    row = lambda w: pl.BlockSpec((tm, w), lambda i: (i, 0))
    full = lambda a: pl.BlockSpec(a.shape, lambda i: (0,) * a.ndim)
    outs = [(256, BF16), (256, BF16), (512, BF16), (512, BF16), (256, F32),
            (512, BF16), (512, BF16), (512, BF16)]
    return pl.pallas_call(
        _in_proj_kernel,
        grid=(T // tm,),
        in_specs=[row(D_MODEL), full(wm), full(wa), full(w2h), full(w2l), full(gb)],
        out_specs=[row(w) for w, _ in outs],
        out_shape=[jax.ShapeDtypeStruct((T, w), dt) for w, dt in outs],
        compiler_params=pltpu.CompilerParams(
            dimension_semantics=("parallel",), vmem_limit_bytes=VMEM_LIMIT),
        name="in_proj",
    )(x2, wm, wa, w2h, w2l, gb)


def _gla_kernel(q_ref, k_ref, v_ref, r_ref, la_ref, g_ref, o_ref, s_ref, *, seq_block):
    C = GLA_CHUNK

    @pl.when(pl.program_id(1) == 0)
    def _():
        s_ref[...] = jnp.zeros_like(s_ref)

    ri = lax.broadcasted_iota(jnp.int32, (C, C), 0)
    ci = lax.broadcasted_iota(jnp.int32, (C, C), 1)
    causal = ci <= ri
    tri = causal.astype(BF16)
    ones_cl = jnp.ones((C, LANES), BF16)
    lane_k = lax.broadcasted_iota(jnp.int32, (1, GLA_KEY_WIDTH), 1) // GLA_DK
    head_masks = [(lane_k == h).astype(F32) for h in range(GLA_HEADS)]
    srow = lax.broadcasted_iota(jnp.int32, (GLA_KEY_WIDTH, GLA_WIDTH), 0) // GLA_DK
    scol = lax.broadcasted_iota(jnp.int32, (GLA_KEY_WIDTH, GLA_WIDTH), 1) // GLA_DV
    state_mask = (srow == scol).astype(F32)
    g = g_ref[...]

    def chunk(c, carry):
        rows = pl.ds(pl.multiple_of(c * C, C), C)
        la = la_ref[rows, :]
        la_hi, la_lo = _split_bf16(la)
        b = _dot(tri, la_hi) + _dot(tri, la_lo)
        b_last = b[C - 1:C, :]
        q = q_ref[rows, :].astype(F32)
        k = k_ref[rows, :].astype(F32)
        v = v_ref[rows, :]
        qd = q * jnp.exp(b)
        kd = (k * jnp.exp(-b)).astype(BF16)
        ke = (k * jnp.exp(b_last - b)).astype(BF16)
        state = s_ref[...]
        o_inter = _dot(qd.astype(BF16), state.astype(BF16))
        outs = []
        for h in range(GLA_HEADS):
            a = _dot_nt((qd * head_masks[h]).astype(BF16), kd)
            a = jnp.where(causal, a, 0.0).astype(BF16)
            cols = slice(h * GLA_DV, (h + 1) * GLA_DV)
            o = _dot(a, v[:, cols]) + o_inter[:, cols]
            o = o * lax.rsqrt(jnp.mean(o * o, axis=-1, keepdims=True) + EPS) * g
            outs.append(o)
        o_all = jnp.concatenate(outs, axis=-1)
        rr = r_ref[rows, :].astype(F32)
        o_ref[rows, :] = (o_all * (rr * jax.nn.sigmoid(rr))).astype(BF16)
        tot = _dot_tn(la_hi, ones_cl) + _dot_tn(la_lo, ones_cl)
        dec = jnp.exp(tot)
        upd = _dot_tn(ke, v) * state_mask
        for h in range(GLA_HEADS):
            cols = slice(h * GLA_DV, (h + 1) * GLA_DV)
            s_ref[:, cols] = state[:, cols] * dec + upd[:, cols]
        return carry

    lax.fori_loop(0, seq_block // C, chunk, 0, unroll=GLA_UNROLL)


def _gla(q, k, v, r, la, g, batch, seq, seq_block=1024):
    nsb = seq // seq_block
    row = lambda w: pl.BlockSpec((seq_block, w), lambda b, s: (b * nsb + s, 0))
    return pl.pallas_call(
        functools.partial(_gla_kernel, seq_block=seq_block),
        grid=(batch, nsb),
        in_specs=[row(256), row(256), row(512), row(512), row(256),
                  pl.BlockSpec((1, GLA_DV), lambda b, s: (0, 0))],
        out_specs=row(512),
        out_shape=jax.ShapeDtypeStruct((batch * seq, GLA_WIDTH), BF16),
        scratch_shapes=[pltpu.VMEM((GLA_KEY_WIDTH, GLA_WIDTH), F32)],
        compiler_params=pltpu.CompilerParams(
            dimension_semantics=("parallel", "arbitrary"), vmem_limit_bytes=VMEM_LIMIT),
        name="gla",
    )(q, k, v, r, la, g)


def _dil_kernel(slope_ref, q_ref, k_ref, v_ref, g_ref, o_ref,
                qf, kf, vf, kd, va, vb, oc, lc, *, seq):
    B = DIL_BLOCK
    U = DIL_UNROLL
    pair = pl.program_id(1)
    qf[...] = q_ref[...].astype(F32)
    kf[...] = k_ref[...].astype(F32)
    vf[...] = v_ref[...].astype(F32)

    lane = lax.broadcasted_iota(jnp.int32, (1, LANES), 1)
    first = lane < DIL_DH
    ii = lax.broadcasted_iota(jnp.int32, (B, B), 0)
    jj = lax.broadcasted_iota(jnp.int32, (B, B), 1)
    upper = jj > ii
    eye = jj == ii
    dist = jnp.bitwise_and(ii - jj, B - 1).astype(F32)
    neg = jnp.float32(-jnp.inf)
    neg_tile = jnp.full((B, B), neg, F32)
    zero_tile = jnp.zeros((B, LANES), BF16)

    for cfg, (window, r) in enumerate(DIL_CONFIGS):
        nb = seq // r // B
        cs = nb + 1
        bias_prev, bias_cur = [], []
        for hh in range(2):
            slope = slope_ref[2 * pair + hh] * (float(r) * LOG2E)
            bias = dist * (-slope)
            bias_prev.append(jnp.where(upper, bias, jnp.where(eye, -slope * float(B), neg)))
            bias_cur.append(jnp.where(upper, neg, bias))

        for c in range(r):
            rows0 = slice(c * cs * B, (c * cs + 1) * B)
            kd[rows0, :] = zero_tile
            va[rows0, :] = zero_tile
            vb[rows0, :] = zero_tile

        def prep(t4, carry, r=r, nb=nb, cs=cs):
            for j in range(4):
                t = t4 * 4 + j
                c = t // nb
                n = t % nb
                start = c + n * (B * r)
                rows = pl.ds(start, B, stride=r) if r > 1 else pl.ds(pl.multiple_of(start, B), B)
                dst = pl.ds(pl.multiple_of((c * cs + 1 + n) * B, B), B)
                kd[dst, :] = kf[rows, :].astype(BF16)
                v = vf[rows, :]
                va[dst, :] = jnp.where(first, v, 1.0).astype(BF16)
                vb[dst, :] = jnp.where(first, 1.0, v).astype(BF16)
            return carry

        lax.fori_loop(0, seq // B // 4, prep, 0)

        def block(it, has_prev, cfg=cfg, r=r, nb=nb, cs=cs, bias_prev=bias_prev, bias_cur=bias_cur):
            c = it // nb
            n = it % nb
            start = c + n * (B * r)
            rows = pl.ds(start, B, stride=r) if r > 1 else pl.ds(pl.multiple_of(start, B), B)
            kv = pl.ds(pl.multiple_of((c * cs + n) * B, B), 2 * B)
            q = qf[rows, :]
            kcat = kd[kv, :]
            acc, mx = [], []
            for hh in range(2):
                hmask = first if hh == 0 else jnp.logical_not(first)
                qh = jnp.where(hmask, q, 0.0).astype(BF16)
                s2 = _dot_nt(qh, kcat)
                if has_prev is True:
                    bp = bias_prev[hh]
                elif has_prev is False:
                    bp = neg_tile
                else:
                    bp = jnp.where(has_prev, bias_prev[hh], neg)
                s_prev = s2[:, 0:B] + bp
                s_cur = s2[:, B:2 * B] + bias_cur[hh]
                m = jnp.max(jnp.maximum(s_prev, s_cur), axis=-1, keepdims=True)
                pcat = jnp.concatenate([jnp.exp2(s_prev - m), jnp.exp2(s_cur - m)], axis=1).astype(BF16)
                vals = va[kv, :] if hh == 0 else vb[kv, :]
                acc.append(_dot(pcat, vals))
                mx.append(m)
            num = jnp.where(first, acc[0], acc[1])
            den = pltpu.roll(jnp.where(first, acc[1], acc[0]), DIL_DH, axis=1)
            oc[cfg, rows, :] = num * (1.0 / den)
            lc[cfg, rows, :] = (jnp.where(first, mx[0], mx[1]) + jnp.log2(den)) * LN2

        def body(t, carry, block=block, nb=nb):
            for u in range(U):
                if nb % U == 0:
                    has_prev = True if u > 0 else (t * U) % nb > 0
                else:
                    assert U % nb == 0
                    has_prev = (u % nb) > 0
                block(t * U + u, has_prev)
            return carry

        lax.fori_loop(0, seq // B // U, body, 0)

    g = g_ref[...]
    CH = 512

    def mix(i, carry):
        rows = pl.ds(pl.multiple_of(i * CH, CH), CH)
        l0, l1, l2 = lc[0, rows, :], lc[1, rows, :], lc[2, rows, :]
        m = jnp.maximum(jnp.maximum(l0, l1), l2)
        e0, e1, e2 = jnp.exp(l0 - m), jnp.exp(l1 - m), jnp.exp(l2 - m)
        den = e0 + e1 + e2
        o = (e0 / den) * oc[0, rows, :] + (e1 / den) * oc[1, rows, :] + (e2 / den) * oc[2, rows, :]
        sq = o * o
        ms_a = jnp.sum(jnp.where(first, sq, 0.0), axis=-1, keepdims=True) * (1.0 / DIL_DH)
        ms_b = jnp.sum(jnp.where(first, 0.0, sq), axis=-1, keepdims=True) * (1.0 / DIL_DH)
        ms = jnp.where(first, ms_a, ms_b)
        o_ref[rows, :] = (o * lax.rsqrt(ms + EPS) * g).astype(BF16)
        return carry

    lax.fori_loop(0, seq // CH, mix, 0)


def _dilated(slopes, dq, dk, dv, g2, batch, seq):
    blk = pl.BlockSpec((seq, LANES), lambda b, p, s: (b, p))
    return pl.pallas_call(
        functools.partial(_dil_kernel, seq=seq),
        grid_spec=pltpu.PrefetchScalarGridSpec(
            num_scalar_prefetch=1,
            grid=(batch, DIL_WIDTH // LANES),
            in_specs=[blk, blk, blk, pl.BlockSpec((1, LANES), lambda b, p, s: (0, 0))],
            out_specs=blk,
            scratch_shapes=[pltpu.VMEM((seq, LANES), F32)] * 3
                           + [pltpu.VMEM((seq + DIL_PAD, LANES), BF16)] * 3
                           + [pltpu.VMEM((3, seq, LANES), F32)] * 2,
        ),
        out_shape=jax.ShapeDtypeStruct((batch * seq, DIL_WIDTH), BF16),
        compiler_params=pltpu.CompilerParams(
            dimension_semantics=("parallel", "parallel"), vmem_limit_bytes=VMEM_LIMIT),
        name="dilated",
    )(slopes, dq, dk, dv, g2)


def _out_proj_kernel(og_ref, od_ref, x_ref, wg_ref, wd_ref, g_ref, b_ref,
                     rwh_ref, rwl_ref, rb_ref, h_ref, hp_ref, eid_ref, gate_ref):
    mix = _dot(og_ref[...], wg_ref[...]) + _dot(od_ref[...], wd_ref[...])
    h = _layer_norm(DEEPNORM_ALPHA * x_ref[...] + mix, g_ref[...], b_ref[...])
    h_ref[...] = h
    hp_ref[...] = _pack_bf16_halves(h)
    h_hi, h_lo = _split_bf16(h)
    logits = (_dot(h_hi, rwh_ref[...]) + _dot(h_lo, rwh_ref[...]) + _dot(h_hi, rwl_ref[...])
              + rb_ref[...])
    lane = lax.broadcasted_iota(jnp.int32, logits.shape, 1)
    lane_f = lane.astype(F32)
    neg = jnp.float32(-jnp.inf)
    big = jnp.float32(1e9)
    is_coarse = jnp.logical_and(lane >= N_EXPERTS, lane < N_EXPERTS + N_GROUPS)
    coarse = jnp.where(is_coarse, logits, neg)
    cmax = jnp.max(coarse, axis=-1, keepdims=True)
    g_idx = jnp.min(jnp.where(coarse == cmax, lane_f, big), axis=-1, keepdims=True) - N_EXPERTS
    p_group = 1.0 / jnp.sum(jnp.exp(coarse - cmax), axis=-1, keepdims=True)
    lo = g_idx * EXPERTS_PER_GROUP
    in_group = jnp.logical_and(lane_f >= lo, lane_f < lo + EXPERTS_PER_GROUP)
    fine = jnp.where(in_group, logits, neg)
    v1 = jnp.max(fine, axis=-1, keepdims=True)
    i1 = jnp.min(jnp.where(fine == v1, lane_f, big), axis=-1, keepdims=True)
    fine2 = jnp.where(lane_f == i1, neg, fine)
    v2 = jnp.max(fine2, axis=-1, keepdims=True)
    i2 = jnp.min(jnp.where(fine2 == v2, lane_f, big), axis=-1, keepdims=True)
    e2 = jnp.exp(v2 - v1)
    den = 1.0 + e2
    gate1 = p_group * (1.0 / den)
    gate2 = p_group * (e2 / den)
    lane2 = lax.broadcasted_iota(jnp.int32, (logits.shape[0], 2), 1)
    eid_ref[...] = jnp.where(lane2 == 0, i1, i2).astype(jnp.int32)
    gate_ref[...] = jnp.where(lane2 == 0, gate1, gate2)


def _out_proj(og, od, x2, wg, wd, g, b, rwh, rwl, rb, tm=1024):
    T = x2.shape[0]
    row = lambda w: pl.BlockSpec((tm, w), lambda i: (i, 0))
    full = lambda a: pl.BlockSpec(a.shape, lambda i: (0,) * a.ndim)
    return pl.pallas_call(
        _out_proj_kernel,
        grid=(T // tm,),
        in_specs=[row(512), row(512), row(D_MODEL), full(wg), full(wd), full(g), full(b),
                  full(rwh), full(rwl), full(rb)],
        out_specs=[row(D_MODEL), row(D_MODEL // 2), row(2), row(2)],
        out_shape=[jax.ShapeDtypeStruct((T, D_MODEL), F32),
                   jax.ShapeDtypeStruct((T, D_MODEL // 2), jnp.int32),
                   jax.ShapeDtypeStruct((T, 2), jnp.int32),
                   jax.ShapeDtypeStruct((T, 2), F32)],
        compiler_params=pltpu.CompilerParams(
            dimension_semantics=("parallel",), vmem_limit_bytes=VMEM_LIMIT),
        name="out_proj_router",
    )(og, od, x2, wg, wd, g, b, rwh, rwl, rb)


def _positions_kernel(eid_ref, dest_ref, be_ref, nv_ref, carry_ref, sp_ref, *, tb):
    phase = pl.program_id(0)
    i = pl.program_id(1)
    lane = lax.broadcasted_iota(jnp.int32, (tb, LANES), 1)
    eid = eid_ref[...]
    oh1 = lane == eid[:, 0:1]
    oh2 = lane == eid[:, 1:2]
    oh = jnp.logical_or(oh1, oh2).astype(BF16)

    @pl.when(jnp.logical_and(phase == 0, i == 0))
    def _():
        carry_ref[...] = jnp.zeros_like(carry_ref)

    @pl.when(phase == 0)
    def _():
        ones_r = jnp.ones((8, tb), BF16)
        carry_ref[...] += _dot(ones_r, oh)

    @pl.when(jnp.logical_and(phase == 1, i == 0))
    def _():
        shift = int(math.log2(ROW_BLOCK))
        nb_row = ((carry_ref[...].astype(jnp.int32) + (ROW_BLOCK - 1)) >> shift)
        r = lax.broadcasted_iota(jnp.int32, (LANES, LANES), 0)
        c = lax.broadcasted_iota(jnp.int32, (LANES, LANES), 1)
        excl = (r < c).astype(BF16)
        start_blk = _dot(nb_row.astype(F32).astype(BF16), excl)
        sp_ref[...] = start_blk * float(ROW_BLOCK)
        be_ref[...] = jnp.concatenate([start_blk[0:1, :].astype(jnp.int32), nb_row[0:1, :]], axis=1)
        lane1 = lax.broadcasted_iota(jnp.int32, (1, LANES), 1)
        total = jnp.sum(jnp.where(lane1 < N_EXPERTS, nb_row[0:1, :].astype(F32), 0.0), axis=-1, keepdims=True)
        nv_ref[...] = jnp.broadcast_to(total, (1, LANES)).astype(jnp.int32)
        carry_ref[...] = jnp.zeros_like(carry_ref)

    @pl.when(phase == 1)
    def _():
        r = lax.broadcasted_iota(jnp.int32, (tb, tb), 0)
        c = lax.broadcasted_iota(jnp.int32, (tb, tb), 1)
        strict = (c < r).astype(BF16)
        before = _dot(strict, oh) + carry_ref[0:1, :] + sp_ref[0:1, :]
        d1 = jnp.sum(jnp.where(oh1, before, 0.0), axis=-1, keepdims=True)
        d2 = jnp.sum(jnp.where(oh2, before, 0.0), axis=-1, keepdims=True)
        lane2 = lax.broadcasted_iota(jnp.int32, (tb, 2), 1)
        dest_ref[...] = jnp.where(lane2 == 0, d1, d2).astype(jnp.int32)
        carry_ref[...] += _dot(jnp.ones((8, tb), BF16), oh)


def _positions(eid, n_blocks_pad, tb=512):
    T = eid.shape[0]
    nb = T // tb
    return pl.pallas_call(
        functools.partial(_positions_kernel, tb=tb),
        grid=(2, nb),
        in_specs=[pl.BlockSpec((tb, 2), lambda p, i: (i, 0))],
        out_specs=[pl.BlockSpec((tb, 2), lambda p, i: (i * p, 0)),
                   pl.BlockSpec((1, n_blocks_pad), lambda p, i: (0, 0)),
                   pl.BlockSpec((1, LANES), lambda p, i: (0, 0))],
        out_shape=[jax.ShapeDtypeStruct((T, 2), jnp.int32),
                   jax.ShapeDtypeStruct((1, n_blocks_pad), jnp.int32),
                   jax.ShapeDtypeStruct((1, LANES), jnp.int32)],
        scratch_shapes=[pltpu.VMEM((8, LANES), F32), pltpu.VMEM((8, LANES), F32)],
        compiler_params=pltpu.CompilerParams(dimension_semantics=("arbitrary", "arbitrary")),
        name="positions",
    )(eid)


def _sc_gather_rows(table, idx):
    n = idx.shape[0]
    d = table.shape[1]
    info = plsc.get_sparse_core_info()
    nc, ns = info.num_cores, info.num_subcores
    per_w = n // (nc * ns)
    assert per_w * nc * ns == n and per_w % SC_INDEX_WINDOW == 0
    mesh = plsc.VectorSubcoreMesh(core_axis_name="core", subcore_axis_name="subcore")
    nsub = SC_INDEX_WINDOW // SC_GATHER_ROWS

    @functools.partial(
        pl.kernel, out_type=jax.ShapeDtypeStruct((n, d), table.dtype), mesh=mesh,
        scratch_types=[pltpu.VMEM((SC_INDEX_WINDOW,), jnp.int32),
                       pltpu.VMEM((2, SC_GATHER_ROWS, d), table.dtype),
                       pltpu.SemaphoreType.DMA((2,)), pltpu.SemaphoreType.DMA((2,))],
        name="sc_gather_rows")
    def gather(x_hbm, i_hbm, o_hbm, idx_v, buf, gsem, wsem):
        wid = lax.axis_index("subcore") * nc + lax.axis_index("core")
        base = wid * per_w

        def gather_copy(s):
            rows = idx_v.at[pl.ds(s * SC_GATHER_ROWS, SC_GATHER_ROWS)]
            return pltpu.make_async_copy(x_hbm.at[rows], buf.at[s % 2], gsem.at[s % 2])

        def write_copy(off, s):
            dst = o_hbm.at[pl.ds(off + s * SC_GATHER_ROWS, SC_GATHER_ROWS)]
            return pltpu.make_async_copy(buf.at[s % 2], dst, wsem.at[s % 2])

        @pl.loop(0, per_w // SC_INDEX_WINDOW)
        def _(j):
            off = base + j * SC_INDEX_WINDOW
            pltpu.sync_copy(i_hbm.at[pl.ds(off, SC_INDEX_WINDOW)], idx_v)
            gather_copy(0).start()
            for s in range(nsub):
                gather_copy(s).wait()
                write_copy(off, s).start()
                if s >= 1:
                    write_copy(off, s - 1).wait()
                if s + 1 < nsub:
                    gather_copy(s + 1).start()
            write_copy(off, nsub - 1).wait()

    return gather(table, idx)


def _sc_inverse_rows(dest_flat, n_rows, chunk=2048):
    n = dest_flat.shape[0]
    n_tokens = n // 2
    assert n_rows <= 3 * n_tokens
    nc = plsc.get_sparse_core_info().num_cores
    mesh = plsc.VectorSubcoreMesh(core_axis_name="core", subcore_axis_name="subcore")

    @functools.partial(
        pl.kernel, out_type=jax.ShapeDtypeStruct((n_rows,), jnp.int32), mesh=mesh,
        scratch_types=[pltpu.VMEM((n_rows,), jnp.int32), pltpu.VMEM((chunk,), jnp.int32)],
        compiler_params=pltpu.CompilerParams(needs_layout_passes=False),
        name="sc_inverse_rows")
    def inverse(d_hbm, o_hbm, inv_v, d_v):
        wid = lax.axis_index("subcore") * nc + lax.axis_index("core")

        @pl.when(wid == 0)
        def _():
            lanes = lax.iota(jnp.int32, SC_LANES)

            @pl.loop(0, n_rows // SC_LANES)
            def _(i):
                r = lanes + i * SC_LANES
                r = jnp.where(r >= n_tokens, r - n_tokens, r)
                inv_v[pl.ds(i * SC_LANES, SC_LANES)] = jnp.where(r >= n_tokens, r - n_tokens, r)

            @pl.loop(0, n // chunk)
            def _(c):
                pltpu.sync_copy(d_hbm.at[pl.ds(c * chunk, chunk)], d_v)

                @pl.loop(0, chunk // SC_LANES)
                def _(j):
                    rows = d_v[pl.ds(j * SC_LANES, SC_LANES)]
                    pair = lax.iota(jnp.int32, SC_LANES) + (c * chunk + j * SC_LANES)
                    plsc.store_scatter(inv_v, [rows], lax.shift_right_logical(pair, 1))

            pltpu.sync_copy(inv_v, o_hbm)

    return inverse(dest_flat)


def _ffn_kernel(first_ref, count_ref, nv_ref, wg_ref, wu_ref, wd_ref, xs_hbm, y_hbm,
                wgb, wub, wdb, xbuf, ybuf, xtail, ytail, isem, osem, tsem, *, n_blocks):
    e = pl.program_id(0)
    first = first_ref[e]
    count = count_ref[e]
    group_rows = FFN_GROUP * ROW_BLOCK

    def rows_of(j):
        return pl.ds(pl.multiple_of((first + j * FFN_GROUP) * ROW_BLOCK, ROW_BLOCK), group_rows)

    def in_copy(j, slot):
        return pltpu.make_async_copy(xs_hbm.at[rows_of(j)], xbuf.at[slot], isem.at[slot])

    def out_copy(j, slot):
        return pltpu.make_async_copy(ybuf.at[slot], y_hbm.at[rows_of(j)], osem.at[slot])

    def expert_mlp(words):
        x_hi, x_lo = _unpack_bf16_halves(words)
        xb = jnp.concatenate([x_hi.astype(BF16), x_lo.astype(BF16)], axis=1)
        a = _dot(xb, wgb[...])
        u = _dot(xb, wub[...])
        hid = (a * jax.nn.sigmoid(a) * u).astype(BF16)
        return _pack_bf16_halves(_dot(hid, wdb[...]))

    assert FFN_GROUP == 2
    n_main = count // FFN_GROUP
    tail = count - n_main * FFN_GROUP
    tail_rows = pl.ds(pl.multiple_of((first + n_main * FFN_GROUP) * ROW_BLOCK, ROW_BLOCK), ROW_BLOCK)
    tail_in = pltpu.make_async_copy(xs_hbm.at[tail_rows], xtail, tsem.at[0])
    tail_out = pltpu.make_async_copy(ytail, y_hbm.at[tail_rows], tsem.at[1])

    @pl.when(count > 0)
    def _():
        @pl.when(tail > 0)
        def _():
            tail_in.start()

        @pl.when(n_main > 0)
        def _():
            in_copy(0, 0).start()

        wgb[...] = wg_ref[0].astype(BF16)
        wub[...] = wu_ref[0].astype(BF16)
        wdb[...] = wd_ref[0].astype(BF16)

        def body(j, carry):
            slot = j % 2
            in_copy(j, slot).wait()

            @pl.when(j + 1 < n_main)
            def _():
                in_copy(j + 1, 1 - slot).start()

            @pl.when(j >= 2)
            def _():
                out_copy(j - 2, slot).wait()

            ybuf[slot] = expert_mlp(xbuf[slot])
            out_copy(j, slot).start()
            return carry

        lax.fori_loop(0, n_main, body, 0)

        @pl.when(n_main >= 2)
        def _():
            out_copy(n_main - 2, n_main % 2).wait()

        @pl.when(n_main >= 1)
        def _():
            out_copy(n_main - 1, (n_main - 1) % 2).wait()

        @pl.when(tail > 0)
        def _():
            tail_in.wait()
            ytail[...] = expert_mlp(xtail[...])
            tail_out.start()
            tail_out.wait()

    @pl.when(e == pl.num_programs(0) - 1)
    def _():
        ytail[...] = jnp.zeros((ROW_BLOCK, D_MODEL // 2), jnp.int32)

        def fill(b, carry):
            dst = y_hbm.at[pl.ds(pl.multiple_of(b * ROW_BLOCK, ROW_BLOCK), ROW_BLOCK)]
            pltpu.sync_copy(ytail, dst)
            return carry

        lax.fori_loop(nv_ref[0], n_blocks, fill, 0)


def _ffn(first_blk, n_blk, nv, xs, w_gate, w_up, w_down):
    n_rows = xs.shape[0]
    n_blocks = n_rows // ROW_BLOCK
    wspec = lambda shape: pl.BlockSpec((1,) + shape, lambda e, *_: (e, 0, 0))
    return pl.pallas_call(
        functools.partial(_ffn_kernel, n_blocks=n_blocks),
        grid_spec=pltpu.PrefetchScalarGridSpec(
            num_scalar_prefetch=3,
            grid=(N_EXPERTS,),
            in_specs=[wspec((D_MODEL, D_FF)), wspec((D_MODEL, D_FF)), wspec((D_FF, D_MODEL)),
                      pl.BlockSpec(memory_space=pl.ANY)],
            out_specs=pl.BlockSpec(memory_space=pl.ANY),
            scratch_shapes=[pltpu.VMEM((D_MODEL, D_FF), BF16), pltpu.VMEM((D_MODEL, D_FF), BF16),
                            pltpu.VMEM((D_FF, D_MODEL), BF16),
                            pltpu.VMEM((2, FFN_GROUP * ROW_BLOCK, D_MODEL // 2), jnp.int32),
                            pltpu.VMEM((2, FFN_GROUP * ROW_BLOCK, D_MODEL // 2), jnp.int32),
                            pltpu.VMEM((ROW_BLOCK, D_MODEL // 2), jnp.int32),
                            pltpu.VMEM((ROW_BLOCK, D_MODEL // 2), jnp.int32),
                            pltpu.SemaphoreType.DMA((2,)), pltpu.SemaphoreType.DMA((2,)),
                            pltpu.SemaphoreType.DMA((2,))],
        ),
        out_shape=jax.ShapeDtypeStruct((n_rows, D_MODEL // 2), jnp.int32),
        compiler_params=pltpu.CompilerParams(
            dimension_semantics=("arbitrary",), vmem_limit_bytes=VMEM_LIMIT),
        name="expert_ffn",
    )(first_blk, n_blk, nv, w_gate, w_up, w_down, xs)


def _combine_kernel(h_ref, ya_ref, yb_ref, gate_ref, g_ref, b_ref, o_ref):
    gate = gate_ref[...]
    a_hi, a_lo = _unpack_bf16_halves(ya_ref[...])
    b_hi, b_lo = _unpack_bf16_halves(yb_ref[...])
    g0, g1 = gate[:, 0:1], gate[:, 1:2]
    ffn = jnp.concatenate([a_hi * g0 + b_hi * g1, a_lo * g0 + b_lo * g1], axis=1)
    o_ref[...] = _layer_norm(DEEPNORM_ALPHA * h_ref[...] + ffn, g_ref[...], b_ref[...])


def _combine(h, y2, gate, g, b, tm=512):
    T = h.shape[0]
    nt = T // tm
    return pl.pallas_call(
        _combine_kernel,
        grid=(nt,),
        in_specs=[pl.BlockSpec((tm, D_MODEL), lambda i: (i, 0)),
                  pl.BlockSpec((tm, D_MODEL // 2), lambda i: (i, 0)),
                  pl.BlockSpec((tm, D_MODEL // 2), lambda i: (i + nt, 0)),
                  pl.BlockSpec((tm, 2), lambda i: (i, 0)),
                  pl.BlockSpec((1, D_MODEL), lambda i: (0, 0)),
                  pl.BlockSpec((1, D_MODEL), lambda i: (0, 0))],
        out_specs=pl.BlockSpec((tm, D_MODEL), lambda i: (i, 0)),
        out_shape=jax.ShapeDtypeStruct((T, D_MODEL), F32),
        compiler_params=pltpu.CompilerParams(
            dimension_semantics=("parallel",), vmem_limit_bytes=VMEM_LIMIT),
        name="combine",
    )(h, y2, y2, gate, g, b)


def kernel(x, w_in, gla_gate_w2, gla_gate_b, gla_norm_g, dil_norm_g, w_out, ln1_g, ln1_b,
           router_coarse_w, router_coarse_b, router_fine_w, router_fine_b,
           expert_w_gate, expert_w_up, expert_w_down, ln2_g, ln2_b):
    B, S, D = x.shape
    T = B * S
    depth = w_in.shape[0]
    slopes = jnp.exp2(-8.0 * jnp.arange(1, DIL_HEADS + 1, dtype=F32) / DIL_HEADS)
    n_rows = 2 * T + N_EXPERTS * ROW_BLOCK
    n_blocks = n_rows // ROW_BLOCK
    n_blocks_pad = -(-n_blocks // (2 * LANES)) * (2 * LANES)
    assert n_blocks_pad == 2 * LANES
    a0 = 1536
    h = x.reshape(T, D)
    for l in range(depth):
        w = w_in[l]
        wm = jnp.concatenate([w[:, :a0], w[:, a0 + GLA_GATE_RANK:]], axis=1).astype(BF16)
        wa = jnp.pad(w[:, a0:a0 + GLA_GATE_RANK], ((0, 0), (0, LANES - GLA_GATE_RANK))).astype(BF16)
        w2 = jnp.pad(gla_gate_w2[l], ((0, LANES - GLA_GATE_RANK), (0, 0)))
        w2h, w2l = _split_bf16(w2)
        q, k, v, r, la, dq, dk, dv = _in_proj(h, wm, wa, w2h, w2l, gla_gate_b[l][None, :])
        o_gla = _gla(q, k, v, r, la, gla_norm_g[l][None, :], B, S)
        g2 = jnp.tile(dil_norm_g[l], 2)[None, :]
        o_dil = _dilated(slopes, dq, dk, dv, g2, B, S)
        wo = w_out[l].astype(BF16)
        rw = jnp.concatenate([router_fine_w[l].reshape(D, N_EXPERTS), router_coarse_w[l]], axis=1)
        rw = jnp.pad(rw, ((0, 0), (0, LANES - N_EXPERTS - N_GROUPS)))
        rwh, rwl = _split_bf16(rw)
        rb = jnp.concatenate([router_fine_b[l].reshape(N_EXPERTS), router_coarse_b[l]])
        rb = jnp.pad(rb, (0, LANES - N_EXPERTS - N_GROUPS))[None, :]
        h1, h1p, eid, gate = _out_proj(o_gla, o_dil, h, wo[:GLA_WIDTH], wo[GLA_WIDTH:],
                                  ln1_g[l][None, :], ln1_b[l][None, :], rwh, rwl, rb)
        dest, be, nv = _positions(eid, n_blocks_pad)
        src_tok = _sc_inverse_rows(dest.reshape(2 * T), n_rows)
        xs = _sc_gather_rows(h1p, src_tok)
        be = be.reshape(n_blocks_pad)
        y = _ffn(be[:N_EXPERTS], be[LANES:LANES + N_EXPERTS], nv.reshape(LANES)[:1], xs,
                 expert_w_gate[l], expert_w_up[l], expert_w_down[l])
        y2 = _sc_gather_rows(y, dest.T.reshape(2 * T))
        h = _combine(h1, y2, gate, ln2_g[l][None, :], ln2_b[l][None, :])
    return h.reshape(B, S, D)
```

```python
import functools
import math

import jax
import jax.numpy as jnp
import numpy as np
from jax import lax
from jax.experimental import pallas as pl
from jax.experimental.pallas import tpu as pltpu
from jax.experimental.pallas import tpu_sc as plsc

D_MODEL = 1024
GLA_HEADS = 4
GLA_DK = 64
GLA_DV = 128
GLA_KEY_WIDTH = GLA_HEADS * GLA_DK
GLA_WIDTH = GLA_HEADS * GLA_DV
GLA_GATE_RANK = 16
GLA_GATE_TEMP = 16.0
DIL_HEADS = 8
DIL_DH = 64
DIL_WIDTH = DIL_HEADS * DIL_DH
DIL_CONFIGS = ((128, 1), (512, 4), (2048, 16))
DIL_BLOCK = 128
DIL_MAX_R = max(r for _, r in DIL_CONFIGS)
DIL_PAD = DIL_BLOCK * DIL_MAX_R
DIL_UNROLL = 8
N_GROUPS = 4
EXPERTS_PER_GROUP = 8
N_EXPERTS = N_GROUPS * EXPERTS_PER_GROUP
D_FF = 512
DEEPNORM_ALPHA = 2.0 ** 0.25
EPS = 1e-5
LOG2E = math.log2(math.e)
LN2 = math.log(2.0)

LANES = 128
GLA_CHUNK = 64
GLA_UNROLL = 4
SC_LANES = 16
SC_INDEX_WINDOW = 128
SC_GATHER_ROWS = 64
FFN_BUFFERS = 8
FFN_WEIGHT_CHUNKS = 4
ROW_BLOCK = 256
VMEM_LIMIT = 56 * 1024 * 1024

F32 = jnp.float32
BF16 = jnp.bfloat16


def _dot(a, b):
    return jnp.dot(a, b, preferred_element_type=F32)


def _dot_nt(a, b):
    return lax.dot_general(a, b, (((1,), (1,)), ((), ())), preferred_element_type=F32)


def _dot_tn(a, b):
    return lax.dot_general(a, b, (((0,), (0,)), ((), ())), preferred_element_type=F32)


def _split_bf16(v):
    hi = v.astype(BF16)
    lo = (v - hi.astype(F32)).astype(BF16)
    return hi, lo


def _pack_bf16_halves(v):
    w = v.shape[1] // 2
    hi = lax.bitcast_convert_type(v[:, :w].astype(BF16).astype(F32), jnp.int32)
    lo = lax.bitcast_convert_type(v[:, w:].astype(BF16).astype(F32), jnp.int32)
    return hi | lax.shift_right_logical(lo, 16)


def _unpack_bf16_halves(words):
    hi = lax.bitcast_convert_type(words & jnp.int32(-65536), F32)
    lo = lax.bitcast_convert_type(lax.shift_left(words, 16), F32)
    return hi, lo


def _layer_norm(v, g, b):
    mu = jnp.mean(v, axis=-1, keepdims=True)
    c = v - mu
    var = jnp.mean(c * c, axis=-1, keepdims=True)
    return c * lax.rsqrt(var + EPS) * g + b


def _in_proj_kernel(x_ref, wm_ref, wa_ref, w2h_ref, w2l_ref, gb_ref,
                    q_ref, k_ref, v_ref, r_ref, la_ref, dq_ref, dk_ref, dv_ref):
    xb = x_ref[...].astype(BF16)

    def piece(c0, c1):
        return _dot(xb, wm_ref[:, c0:c1])

    q_ref[...] = (piece(0, 256) * (GLA_DK ** -0.5)).astype(BF16)
    k_ref[...] = piece(256, 512).astype(BF16)
    v_ref[...] = piece(512, 1024).astype(BF16)
    r_ref[...] = piece(1024, 1536).astype(BF16)
    dq_ref[...] = (piece(1536, 2048) * (DIL_DH ** -0.5 * LOG2E)).astype(BF16)
    dk_ref[...] = piece(2048, 2560).astype(BF16)
    dv_ref[...] = piece(2560, 3072).astype(BF16)
    ga = _dot(xb, wa_ref[...])
    ga_hi, ga_lo = _split_bf16(ga)
    z = _dot(ga_hi, w2h_ref[...]) + _dot(ga_lo, w2h_ref[...]) + _dot(ga_hi, w2l_ref[...]) + gb_ref[...]
    log_sig = jnp.minimum(z, 0.0) - jnp.log1p(jnp.exp(-jnp.abs(z)))
    la_ref[...] = log_sig * (1.0 / GLA_GATE_TEMP)


def _in_proj(x2, wm, wa, w2h, w2l, gb, tm=1024):
    T = x2.shape[0]
    row = lambda w: pl.BlockSpec((tm, w), lambda i: (i, 0))
    full = lambda a: pl.BlockSpec(a.shape, lambda i: (0,) * a.ndim)
    outs = [(256, BF16), (256, BF16), (512, BF16), (512, BF16), (256, F32),
            (512, BF16), (512, BF16), (512, BF16)]
    return pl.pallas_call(
        _in_proj_kernel,
        grid=(T // tm,),
        in_specs=[row(D_MODEL), full(wm), full(wa), full(w2h), full(w2l), full(gb)],
        out_specs=[row(w) for w, _ in outs],
        out_shape=[jax.ShapeDtypeStruct((T, w), dt) for w, dt in outs],
        compiler_params=pltpu.CompilerParams(
            dimension_semantics=("parallel",), vmem_limit_bytes=VMEM_LIMIT),
        name="in_proj",
    )(x2, wm, wa, w2h, w2l, gb)


def _gla_kernel(q_ref, k_ref, v_ref, r_ref, la_ref, g_ref, o_ref, s_ref, *, seq_block):
    C = GLA_CHUNK

    @pl.when(pl.program_id(1) == 0)
    def _():
        s_ref[...] = jnp.zeros_like(s_ref)

    ri = lax.broadcasted_iota(jnp.int32, (C, C), 0)
    ci = lax.broadcasted_iota(jnp.int32, (C, C), 1)
    causal = ci <= ri
    tri = causal.astype(BF16)
    ones_cl = jnp.ones((C, LANES), BF16)
    lane_k = lax.broadcasted_iota(jnp.int32, (1, GLA_KEY_WIDTH), 1) // GLA_DK
    head_masks = [(lane_k == h).astype(F32) for h in range(GLA_HEADS)]
    srow = lax.broadcasted_iota(jnp.int32, (GLA_KEY_WIDTH, GLA_WIDTH), 0) // GLA_DK
    scol = lax.broadcasted_iota(jnp.int32, (GLA_KEY_WIDTH, GLA_WIDTH), 1) // GLA_DV
    state_mask = (srow == scol).astype(F32)
    g = g_ref[...]

    def chunk(c, carry):
        rows = pl.ds(pl.multiple_of(c * C, C), C)
        la = la_ref[rows, :]
        la_hi, la_lo = _split_bf16(la)
        b = _dot(tri, la_hi) + _dot(tri, la_lo)
        b_last = b[C - 1:C, :]
        q = q_ref[rows, :].astype(F32)
        k = k_ref[rows, :].astype(F32)
        v = v_ref[rows, :]
        qd = q * jnp.exp(b)
        kd = (k * jnp.exp(-b)).astype(BF16)
        ke = (k * jnp.exp(b_last - b)).astype(BF16)
        state = s_ref[...]
        o_inter = _dot(qd.astype(BF16), state.astype(BF16))
        outs = []
        for h in range(GLA_HEADS):
            a = _dot_nt((qd * head_masks[h]).astype(BF16), kd)
            a = jnp.where(causal, a, 0.0).astype(BF16)
            cols = slice(h * GLA_DV, (h + 1) * GLA_DV)
            o = _dot(a, v[:, cols]) + o_inter[:, cols]
            o = o * lax.rsqrt(jnp.mean(o * o, axis=-1, keepdims=True) + EPS) * g
            outs.append(o)
        o_all = jnp.concatenate(outs, axis=-1)
        rr = r_ref[rows, :].astype(F32)
        o_ref[rows, :] = (o_all * (rr * jax.nn.sigmoid(rr))).astype(BF16)
        tot = _dot_tn(la_hi, ones_cl) + _dot_tn(la_lo, ones_cl)
        dec = jnp.exp(tot)
        upd = _dot_tn(ke, v) * state_mask
        for h in range(GLA_HEADS):
            cols = slice(h * GLA_DV, (h + 1) * GLA_DV)
            s_ref[:, cols] = state[:, cols] * dec + upd[:, cols]
        return carry

    lax.fori_loop(0, seq_block // C, chunk, 0, unroll=GLA_UNROLL)


def _gla(q, k, v, r, la, g, batch, seq, seq_block=1024):
    nsb = seq // seq_block
    row = lambda w: pl.BlockSpec((seq_block, w), lambda b, s: (b * nsb + s, 0))
    return pl.pallas_call(
        functools.partial(_gla_kernel, seq_block=seq_block),
        grid=(batch, nsb),
        in_specs=[row(256), row(256), row(512), row(512), row(256),
                  pl.BlockSpec((1, GLA_DV), lambda b, s: (0, 0))],
        out_specs=row(512),
        out_shape=jax.ShapeDtypeStruct((batch * seq, GLA_WIDTH), BF16),
        scratch_shapes=[pltpu.VMEM((GLA_KEY_WIDTH, GLA_WIDTH), F32)],
        compiler_params=pltpu.CompilerParams(
            dimension_semantics=("parallel", "arbitrary"), vmem_limit_bytes=VMEM_LIMIT),
        name="gla",
    )(q, k, v, r, la, g)


def _dil_kernel(slope_ref, q_ref, k_ref, v_ref, g_ref, o_ref,
                qf, kf, vf, kd, va, vb, oc, lc, *, seq):
    B = DIL_BLOCK
    U = DIL_UNROLL
    pair = pl.program_id(1)
    qf[...] = q_ref[...].astype(F32)
    kf[...] = k_ref[...].astype(F32)
    vf[...] = v_ref[...].astype(F32)

    lane = lax.broadcasted_iota(jnp.int32, (1, LANES), 1)
    first = lane < DIL_DH
    ii = lax.broadcasted_iota(jnp.int32, (B, B), 0)
    jj = lax.broadcasted_iota(jnp.int32, (B, B), 1)
    upper = jj > ii
    eye = jj == ii
    dist = jnp.bitwise_and(ii - jj, B - 1).astype(F32)
    neg = jnp.float32(-jnp.inf)
    neg_tile = jnp.full((B, B), neg, F32)
    zero_tile = jnp.zeros((B, LANES), BF16)

    for cfg, (window, r) in enumerate(DIL_CONFIGS):
        nb = seq // r // B
        cs = nb + 1
        bias_prev, bias_cur = [], []
        for hh in range(2):
            slope = slope_ref[2 * pair + hh] * (float(r) * LOG2E)
            bias = dist * (-slope)
            bias_prev.append(jnp.where(upper, bias, jnp.where(eye, -slope * float(B), neg)))
            bias_cur.append(jnp.where(upper, neg, bias))

        for c in range(r):
            rows0 = slice(c * cs * B, (c * cs + 1) * B)
            kd[rows0, :] = zero_tile
            va[rows0, :] = zero_tile
            vb[rows0, :] = zero_tile

        def prep(t4, carry, r=r, nb=nb, cs=cs):
            for j in range(4):
                t = t4 * 4 + j
                c = t // nb
                n = t % nb
                start = c + n * (B * r)
                rows = pl.ds(start, B, stride=r) if r > 1 else pl.ds(pl.multiple_of(start, B), B)
                dst = pl.ds(pl.multiple_of((c * cs + 1 + n) * B, B), B)
                kd[dst, :] = kf[rows, :].astype(BF16)
                v = vf[rows, :]
                va[dst, :] = jnp.where(first, v, 1.0).astype(BF16)
                vb[dst, :] = jnp.where(first, 1.0, v).astype(BF16)
            return carry

        lax.fori_loop(0, seq // B // 4, prep, 0)

        def block(it, has_prev, cfg=cfg, r=r, nb=nb, cs=cs, bias_prev=bias_prev, bias_cur=bias_cur):
            c = it // nb
            n = it % nb
            start = c + n * (B * r)
            rows = pl.ds(start, B, stride=r) if r > 1 else pl.ds(pl.multiple_of(start, B), B)
            kv = pl.ds(pl.multiple_of((c * cs + n) * B, B), 2 * B)
            q = qf[rows, :]
            kcat = kd[kv, :]
            acc, mx = [], []
            for hh in range(2):
                hmask = first if hh == 0 else jnp.logical_not(first)
                qh = jnp.where(hmask, q, 0.0).astype(BF16)
                s2 = _dot_nt(qh, kcat)
                if has_prev is True:
                    bp = bias_prev[hh]
                elif has_prev is False:
                    bp = neg_tile
                else:
                    bp = jnp.where(has_prev, bias_prev[hh], neg)
                s_prev = s2[:, 0:B] + bp
                s_cur = s2[:, B:2 * B] + bias_cur[hh]
                m = jnp.max(jnp.maximum(s_prev, s_cur), axis=-1, keepdims=True)
                pcat = jnp.concatenate([jnp.exp2(s_prev - m), jnp.exp2(s_cur - m)], axis=1).astype(BF16)
                vals = va[kv, :] if hh == 0 else vb[kv, :]
                acc.append(_dot(pcat, vals))
                mx.append(m)
            num = jnp.where(first, acc[0], acc[1])
            den = pltpu.roll(jnp.where(first, acc[1], acc[0]), DIL_DH, axis=1)
            oc[cfg, rows, :] = num * (1.0 / den)
            lc[cfg, rows, :] = (jnp.where(first, mx[0], mx[1]) + jnp.log2(den)) * LN2

        def body(t, carry, block=block, nb=nb):
            for u in range(U):
                if nb % U == 0:
                    has_prev = True if u > 0 else (t * U) % nb > 0
                else:
                    assert U % nb == 0
                    has_prev = (u % nb) > 0
                block(t * U + u, has_prev)
            return carry

        lax.fori_loop(0, seq // B // U, body, 0)

    g = g_ref[...]
    CH = 512

    def mix(i, carry):
        rows = pl.ds(pl.multiple_of(i * CH, CH), CH)
        l0, l1, l2 = lc[0, rows, :], lc[1, rows, :], lc[2, rows, :]
        m = jnp.maximum(jnp.maximum(l0, l1), l2)
        e0, e1, e2 = jnp.exp(l0 - m), jnp.exp(l1 - m), jnp.exp(l2 - m)
        den = e0 + e1 + e2
        o = (e0 / den) * oc[0, rows, :] + (e1 / den) * oc[1, rows, :] + (e2 / den) * oc[2, rows, :]
        sq = o * o
        ms_a = jnp.sum(jnp.where(first, sq, 0.0), axis=-1, keepdims=True) * (1.0 / DIL_DH)
        ms_b = jnp.sum(jnp.where(first, 0.0, sq), axis=-1, keepdims=True) * (1.0 / DIL_DH)
        ms = jnp.where(first, ms_a, ms_b)
        o_ref[rows, :] = (o * lax.rsqrt(ms + EPS) * g).astype(BF16)
        return carry

    lax.fori_loop(0, seq // CH, mix, 0)


def _dilated(slopes, dq, dk, dv, g2, batch, seq):
    blk = pl.BlockSpec((seq, LANES), lambda b, p, s: (b, p))
    return pl.pallas_call(
        functools.partial(_dil_kernel, seq=seq),
        grid_spec=pltpu.PrefetchScalarGridSpec(
            num_scalar_prefetch=1,
            grid=(batch, DIL_WIDTH // LANES),
            in_specs=[blk, blk, blk, pl.BlockSpec((1, LANES), lambda b, p, s: (0, 0))],
            out_specs=blk,
            scratch_shapes=[pltpu.VMEM((seq, LANES), F32)] * 3
                           + [pltpu.VMEM((seq + DIL_PAD, LANES), BF16)] * 3
                           + [pltpu.VMEM((3, seq, LANES), F32)] * 2,
        ),
        out_shape=jax.ShapeDtypeStruct((batch * seq, DIL_WIDTH), BF16),
        compiler_params=pltpu.CompilerParams(
            dimension_semantics=("parallel", "parallel"), vmem_limit_bytes=VMEM_LIMIT),
        name="dilated",
    )(slopes, dq, dk, dv, g2)


def _out_proj_kernel(og_ref, od_ref, x_ref, wg_ref, wd_ref, g_ref, b_ref,
                     rwh_ref, rwl_ref, rb_ref, h_ref, hp_ref, eid_ref, gate_ref):
    mix = _dot(og_ref[...], wg_ref[...]) + _dot(od_ref[...], wd_ref[...])
    h = _layer_norm(DEEPNORM_ALPHA * x_ref[...] + mix, g_ref[...], b_ref[...])
    h_ref[...] = h
    hp_ref[...] = _pack_bf16_halves(h)
    h_hi, h_lo = _split_bf16(h)
    logits = (_dot(h_hi, rwh_ref[...]) + _dot(h_lo, rwh_ref[...]) + _dot(h_hi, rwl_ref[...])
              + rb_ref[...])
    lane = lax.broadcasted_iota(jnp.int32, logits.shape, 1)
    lane_f = lane.astype(F32)
    neg = jnp.float32(-jnp.inf)
    big = jnp.float32(1e9)
    is_coarse = jnp.logical_and(lane >= N_EXPERTS, lane < N_EXPERTS + N_GROUPS)
    coarse = jnp.where(is_coarse, logits, neg)
    cmax = jnp.max(coarse, axis=-1, keepdims=True)
    g_idx = jnp.min(jnp.where(coarse == cmax, lane_f, big), axis=-1, keepdims=True) - N_EXPERTS
    p_group = 1.0 / jnp.sum(jnp.exp(coarse - cmax), axis=-1, keepdims=True)
    lo = g_idx * EXPERTS_PER_GROUP
    in_group = jnp.logical_and(lane_f >= lo, lane_f < lo + EXPERTS_PER_GROUP)
    fine = jnp.where(in_group, logits, neg)
    v1 = jnp.max(fine, axis=-1, keepdims=True)
    i1 = jnp.min(jnp.where(fine == v1, lane_f, big), axis=-1, keepdims=True)
    fine2 = jnp.where(lane_f == i1, neg, fine)
    v2 = jnp.max(fine2, axis=-1, keepdims=True)
    i2 = jnp.min(jnp.where(fine2 == v2, lane_f, big), axis=-1, keepdims=True)
    e2 = jnp.exp(v2 - v1)
    den = 1.0 + e2
    gate1 = p_group * (1.0 / den)
    gate2 = p_group * (e2 / den)
    lane2 = lax.broadcasted_iota(jnp.int32, (logits.shape[0], 2), 1)
    eid_ref[...] = jnp.where(lane2 == 0, i1, i2).astype(jnp.int32)
    gate_ref[...] = jnp.where(lane2 == 0, gate1, gate2)


def _out_proj(og, od, x2, wg, wd, g, b, rwh, rwl, rb, tm=1024):
    T = x2.shape[0]
    row = lambda w: pl.BlockSpec((tm, w), lambda i: (i, 0))
    full = lambda a: pl.BlockSpec(a.shape, lambda i: (0,) * a.ndim)
    return pl.pallas_call(
        _out_proj_kernel,
        grid=(T // tm,),
        in_specs=[row(512), row(512), row(D_MODEL), full(wg), full(wd), full(g), full(b),
                  full(rwh), full(rwl), full(rb)],
        out_specs=[row(D_MODEL), row(D_MODEL // 2), row(2), row(2)],
        out_shape=[jax.ShapeDtypeStruct((T, D_MODEL), F32),
                   jax.ShapeDtypeStruct((T, D_MODEL // 2), jnp.int32),
                   jax.ShapeDtypeStruct((T, 2), jnp.int32),
                   jax.ShapeDtypeStruct((T, 2), F32)],
        compiler_params=pltpu.CompilerParams(
            dimension_semantics=("parallel",), vmem_limit_bytes=VMEM_LIMIT),
        name="out_proj_router",
    )(og, od, x2, wg, wd, g, b, rwh, rwl, rb)


def _positions_kernel(eid_ref, dest_ref, be_ref, nv_ref, carry_ref, sp_ref, *, tb):
    phase = pl.program_id(0)
    i = pl.program_id(1)
    lane = lax.broadcasted_iota(jnp.int32, (tb, LANES), 1)
    eid = eid_ref[...]
    oh1 = lane == eid[:, 0:1]
    oh2 = lane == eid[:, 1:2]
    oh = jnp.logical_or(oh1, oh2).astype(BF16)

    @pl.when(jnp.logical_and(phase == 0, i == 0))
    def _():
        carry_ref[...] = jnp.zeros_like(carry_ref)

    @pl.when(phase == 0)
    def _():
        ones_r = jnp.ones((8, tb), BF16)
        carry_ref[...] += _dot(ones_r, oh)

    @pl.when(jnp.logical_and(phase == 1, i == 0))
    def _():
        shift = int(math.log2(ROW_BLOCK))
        nb_row = ((carry_ref[...].astype(jnp.int32) + (ROW_BLOCK - 1)) >> shift)
        r = lax.broadcasted_iota(jnp.int32, (LANES, LANES), 0)
        c = lax.broadcasted_iota(jnp.int32, (LANES, LANES), 1)
        excl = (r < c).astype(BF16)
        start_blk = _dot(nb_row.astype(F32).astype(BF16), excl)
        sp_ref[...] = start_blk * float(ROW_BLOCK)
        be_ref[...] = jnp.concatenate([start_blk[0:1, :].astype(jnp.int32), nb_row[0:1, :]], axis=1)
        lane1 = lax.broadcasted_iota(jnp.int32, (1, LANES), 1)
        total = jnp.sum(jnp.where(lane1 < N_EXPERTS, nb_row[0:1, :].astype(F32), 0.0), axis=-1, keepdims=True)
        nv_ref[...] = jnp.broadcast_to(total, (1, LANES)).astype(jnp.int32)
        carry_ref[...] = jnp.zeros_like(carry_ref)

    @pl.when(phase == 1)
    def _():
        r = lax.broadcasted_iota(jnp.int32, (tb, tb), 0)
        c = lax.broadcasted_iota(jnp.int32, (tb, tb), 1)
        strict = (c < r).astype(BF16)
        before = _dot(strict, oh) + carry_ref[0:1, :] + sp_ref[0:1, :]
        d1 = jnp.sum(jnp.where(oh1, before, 0.0), axis=-1, keepdims=True)
        d2 = jnp.sum(jnp.where(oh2, before, 0.0), axis=-1, keepdims=True)
        lane2 = lax.broadcasted_iota(jnp.int32, (tb, 2), 1)
        dest_ref[...] = jnp.where(lane2 == 0, d1, d2).astype(jnp.int32)
        carry_ref[...] += _dot(jnp.ones((8, tb), BF16), oh)


def _positions(eid, n_blocks_pad, tb=512):
    T = eid.shape[0]
    nb = T // tb
    return pl.pallas_call(
        functools.partial(_positions_kernel, tb=tb),
        grid=(2, nb),
        in_specs=[pl.BlockSpec((tb, 2), lambda p, i: (i, 0))],
        out_specs=[pl.BlockSpec((tb, 2), lambda p, i: (i * p, 0)),
                   pl.BlockSpec((1, n_blocks_pad), lambda p, i: (0, 0)),
                   pl.BlockSpec((1, LANES), lambda p, i: (0, 0))],
        out_shape=[jax.ShapeDtypeStruct((T, 2), jnp.int32),
                   jax.ShapeDtypeStruct((1, n_blocks_pad), jnp.int32),
                   jax.ShapeDtypeStruct((1, LANES), jnp.int32)],
        scratch_shapes=[pltpu.VMEM((8, LANES), F32), pltpu.VMEM((8, LANES), F32)],
        compiler_params=pltpu.CompilerParams(dimension_semantics=("arbitrary", "arbitrary")),
        name="positions",
    )(eid)


def _sc_gather_rows(table, idx):
    n = idx.shape[0]
    d = table.shape[1]
    info = plsc.get_sparse_core_info()
    nc, ns = info.num_cores, info.num_subcores
    per_w = n // (nc * ns)
    assert per_w * nc * ns == n and per_w % SC_INDEX_WINDOW == 0
    mesh = plsc.VectorSubcoreMesh(core_axis_name="core", subcore_axis_name="subcore")
    nsub = SC_INDEX_WINDOW // SC_GATHER_ROWS

    @functools.partial(
        pl.kernel, out_type=jax.ShapeDtypeStruct((n, d), table.dtype), mesh=mesh,
        scratch_types=[pltpu.VMEM((SC_INDEX_WINDOW,), jnp.int32),
                       pltpu.VMEM((2, SC_GATHER_ROWS, d), table.dtype),
                       pltpu.SemaphoreType.DMA((2,)), pltpu.SemaphoreType.DMA((2,))],
        name="sc_gather_rows")
    def gather(x_hbm, i_hbm, o_hbm, idx_v, buf, gsem, wsem):
        wid = lax.axis_index("subcore") * nc + lax.axis_index("core")
        base = wid * per_w

        def gather_copy(s):
            rows = idx_v.at[pl.ds(s * SC_GATHER_ROWS, SC_GATHER_ROWS)]
            return pltpu.make_async_copy(x_hbm.at[rows], buf.at[s % 2], gsem.at[s % 2])

        def write_copy(off, s):
            dst = o_hbm.at[pl.ds(off + s * SC_GATHER_ROWS, SC_GATHER_ROWS)]
            return pltpu.make_async_copy(buf.at[s % 2], dst, wsem.at[s % 2])

        @pl.loop(0, per_w // SC_INDEX_WINDOW)
        def _(j):
            off = base + j * SC_INDEX_WINDOW
            pltpu.sync_copy(i_hbm.at[pl.ds(off, SC_INDEX_WINDOW)], idx_v)
            gather_copy(0).start()
            for s in range(nsub):
                gather_copy(s).wait()
                write_copy(off, s).start()
                if s >= 1:
                    write_copy(off, s - 1).wait()
                if s + 1 < nsub:
                    gather_copy(s + 1).start()
            write_copy(off, nsub - 1).wait()

    return gather(table, idx)


def _sc_inverse_rows(dest_flat, n_rows, chunk=2048):
    n = dest_flat.shape[0]
    n_tokens = n // 2
    assert n_rows <= 3 * n_tokens
    nc = plsc.get_sparse_core_info().num_cores
    mesh = plsc.VectorSubcoreMesh(core_axis_name="core", subcore_axis_name="subcore")

    @functools.partial(
        pl.kernel, out_type=jax.ShapeDtypeStruct((n_rows,), jnp.int32), mesh=mesh,
        scratch_types=[pltpu.VMEM((n_rows,), jnp.int32), pltpu.VMEM((chunk,), jnp.int32)],
        compiler_params=pltpu.CompilerParams(needs_layout_passes=False),
        name="sc_inverse_rows")
    def inverse(d_hbm, o_hbm, inv_v, d_v):
        wid = lax.axis_index("subcore") * nc + lax.axis_index("core")

        @pl.when(wid == 0)
        def _():
            lanes = lax.iota(jnp.int32, SC_LANES)

            @pl.loop(0, n_rows // SC_LANES)
            def _(i):
                r = lanes + i * SC_LANES
                r = jnp.where(r >= n_tokens, r - n_tokens, r)
                inv_v[pl.ds(i * SC_LANES, SC_LANES)] = jnp.where(r >= n_tokens, r - n_tokens, r)

            @pl.loop(0, n // chunk)
            def _(c):
                pltpu.sync_copy(d_hbm.at[pl.ds(c * chunk, chunk)], d_v)

                @pl.loop(0, chunk // SC_LANES)
                def _(j):
                    rows = d_v[pl.ds(j * SC_LANES, SC_LANES)]
                    pair = lax.iota(jnp.int32, SC_LANES) + (c * chunk + j * SC_LANES)
                    plsc.store_scatter(inv_v, [rows], lax.shift_right_logical(pair, 1))

            pltpu.sync_copy(inv_v, o_hbm)

    return inverse(dest_flat)


def _ffn_kernel(first_ref, count_ref, nv_ref, wg_hbm, wu_hbm, wd_hbm, xs_hbm, y_hbm,
                wg32, wu32, wd32, wgb, wub, wdb, xbuf, ybuf, wsem, isem, osem, *, n_blocks):
    nv = nv_ref[0]
    nbuf = FFN_BUFFERS

    def next_expert(e):
        def more(t):
            return jnp.logical_and(t < N_EXPERTS, count_ref[jnp.minimum(t, N_EXPERTS - 1)] == 0)
        return lax.while_loop(more, lambda t: t + 1, e + 1)

    def weight_copies(e, slot):
        ee = jnp.minimum(e, N_EXPERTS - 1)
        copies = []
        for i, (src, dst) in enumerate(((wg_hbm, wg32), (wu_hbm, wu32), (wd_hbm, wd32))):
            rows = src.shape[1] // FFN_WEIGHT_CHUNKS
            for c in range(FFN_WEIGHT_CHUNKS):
                part = pl.ds(c * rows, rows)
                copies.append(pltpu.make_async_copy(src.at[ee, part], dst.at[slot, part],
                                                    wsem.at[slot, i * FFN_WEIGHT_CHUNKS + c]))
        return copies

    def fetch_weights(e, slot):
        @pl.when(e < N_EXPERTS)
        def _():
            for c in weight_copies(e, slot):
                c.start()

    def take_weights(e, slot):
        for c in weight_copies(e, slot):
            c.wait()
        wgb[...] = wg32[slot].astype(BF16)
        wub[...] = wu32[slot].astype(BF16)
        wdb[...] = wd32[slot].astype(BF16)
        fetch_weights(next_expert(next_expert(e)), slot)

    def rows_of(b):
        return pl.ds(pl.multiple_of(b * ROW_BLOCK, ROW_BLOCK), ROW_BLOCK)

    def in_copy(b):
        return pltpu.make_async_copy(xs_hbm.at[rows_of(b)], xbuf.at[b % nbuf], isem.at[b % nbuf])

    def out_copy(b):
        return pltpu.make_async_copy(ybuf.at[b % nbuf], y_hbm.at[rows_of(b)], osem.at[b % nbuf])

    @pl.when(nv > 0)
    def _():
        e0 = next_expert(jnp.int32(-1))
        for i in range(nbuf - 1):
            @pl.when(i < nv)
            def _():
                in_copy(i).start()
        fetch_weights(e0, 0)
        fetch_weights(next_expert(e0), 1)
        take_weights(e0, 0)

        def body(b, carry):
            e, k = carry
            switch = b >= first_ref[e] + count_ref[e]
            e_new = jnp.where(switch, next_expert(e), e)
            k_new = jnp.where(switch, k + 1, k)

            @pl.when(switch)
            def _():
                take_weights(e_new, k_new % 2)

            in_copy(b).wait()

            @pl.when(b + nbuf - 1 < nv)
            def _():
                in_copy(b + nbuf - 1).start()

            @pl.when(b >= nbuf)
            def _():
                out_copy(b - nbuf).wait()

            x_hi, x_lo = _unpack_bf16_halves(xbuf[b % nbuf])
            xb = jnp.concatenate([x_hi.astype(BF16), x_lo.astype(BF16)], axis=1)
            a = _dot(xb, wgb[...])
            u = _dot(xb, wub[...])
            hid = (a * jax.nn.sigmoid(a) * u).astype(BF16)
            ybuf[b % nbuf] = _pack_bf16_halves(_dot(hid, wdb[...]))
            out_copy(b).start()
            return e_new, k_new

        lax.fori_loop(0, nv, body, (e0, jnp.int32(0)))

        for i in range(nbuf):
            @pl.when(nv > i)
            def _():
                out_copy(nv - 1 - i).wait()

    ybuf[0] = jnp.zeros((ROW_BLOCK, D_MODEL // 2), jnp.int32)

    def fill(b, carry):
        pltpu.sync_copy(ybuf.at[0], y_hbm.at[rows_of(b)])
        return carry

    lax.fori_loop(nv, n_blocks, fill, 0)


def _ffn(first_blk, n_blk, nv, xs, w_gate, w_up, w_down):
    n_rows = xs.shape[0]
    n_blocks = n_rows // ROW_BLOCK
    anyspec = pl.BlockSpec(memory_space=pl.ANY)
    return pl.pallas_call(
        functools.partial(_ffn_kernel, n_blocks=n_blocks),
        grid_spec=pltpu.PrefetchScalarGridSpec(
            num_scalar_prefetch=3,
            grid=(1,),
            in_specs=[anyspec, anyspec, anyspec, anyspec],
            out_specs=anyspec,
            scratch_shapes=[pltpu.VMEM((2, D_MODEL, D_FF), F32), pltpu.VMEM((2, D_MODEL, D_FF), F32),
                            pltpu.VMEM((2, D_FF, D_MODEL), F32),
                            pltpu.VMEM((D_MODEL, D_FF), BF16), pltpu.VMEM((D_MODEL, D_FF), BF16),
                            pltpu.VMEM((D_FF, D_MODEL), BF16),
                            pltpu.VMEM((FFN_BUFFERS, ROW_BLOCK, D_MODEL // 2), jnp.int32),
                            pltpu.VMEM((FFN_BUFFERS, ROW_BLOCK, D_MODEL // 2), jnp.int32),
                            pltpu.SemaphoreType.DMA((2, 3 * FFN_WEIGHT_CHUNKS)),
                            pltpu.SemaphoreType.DMA((FFN_BUFFERS,)),
                            pltpu.SemaphoreType.DMA((FFN_BUFFERS,))],
        ),
        out_shape=jax.ShapeDtypeStruct((n_rows, D_MODEL // 2), jnp.int32),
        compiler_params=pltpu.CompilerParams(
            dimension_semantics=("arbitrary",), vmem_limit_bytes=VMEM_LIMIT),
        name="expert_ffn",
    )(first_blk, n_blk, nv, w_gate, w_up, w_down, xs)


def _combine_kernel(h_ref, ya_ref, yb_ref, gate_ref, g_ref, b_ref, o_ref):
    gate = gate_ref[...]
    a_hi, a_lo = _unpack_bf16_halves(ya_ref[...])
    b_hi, b_lo = _unpack_bf16_halves(yb_ref[...])
    g0, g1 = gate[:, 0:1], gate[:, 1:2]
    ffn = jnp.concatenate([a_hi * g0 + b_hi * g1, a_lo * g0 + b_lo * g1], axis=1)
    o_ref[...] = _layer_norm(DEEPNORM_ALPHA * h_ref[...] + ffn, g_ref[...], b_ref[...])


def _combine(h, y2, gate, g, b, tm=512):
    T = h.shape[0]
    nt = T // tm
    return pl.pallas_call(
        _combine_kernel,
        grid=(nt,),
        in_specs=[pl.BlockSpec((tm, D_MODEL), lambda i: (i, 0)),
                  pl.BlockSpec((tm, D_MODEL // 2), lambda i: (i, 0)),
                  pl.BlockSpec((tm, D_MODEL // 2), lambda i: (i + nt, 0)),
                  pl.BlockSpec((tm, 2), lambda i: (i, 0)),
                  pl.BlockSpec((1, D_MODEL), lambda i: (0, 0)),
                  pl.BlockSpec((1, D_MODEL), lambda i: (0, 0))],
        out_specs=pl.BlockSpec((tm, D_MODEL), lambda i: (i, 0)),
        out_shape=jax.ShapeDtypeStruct((T, D_MODEL), F32),
        compiler_params=pltpu.CompilerParams(
            dimension_semantics=("parallel",), vmem_limit_bytes=VMEM_LIMIT),
        name="combine",
    )(h, y2, y2, gate, g, b)


def kernel(x, w_in, gla_gate_w2, gla_gate_b, gla_norm_g, dil_norm_g, w_out, ln1_g, ln1_b,
           router_coarse_w, router_coarse_b, router_fine_w, router_fine_b,
           expert_w_gate, expert_w_up, expert_w_down, ln2_g, ln2_b):
    B, S, D = x.shape
    T = B * S
    depth = w_in.shape[0]
    slopes = jnp.exp2(-8.0 * jnp.arange(1, DIL_HEADS + 1, dtype=F32) / DIL_HEADS)
    n_rows = 2 * T + N_EXPERTS * ROW_BLOCK
    n_blocks = n_rows // ROW_BLOCK
    n_blocks_pad = -(-n_blocks // (2 * LANES)) * (2 * LANES)
    assert n_blocks_pad == 2 * LANES
    a0 = 1536
    h = x.reshape(T, D)
    for l in range(depth):
        w = w_in[l]
        wm = jnp.concatenate([w[:, :a0], w[:, a0 + GLA_GATE_RANK:]], axis=1).astype(BF16)
        wa = jnp.pad(w[:, a0:a0 + GLA_GATE_RANK], ((0, 0), (0, LANES - GLA_GATE_RANK))).astype(BF16)
        w2 = jnp.pad(gla_gate_w2[l], ((0, LANES - GLA_GATE_RANK), (0, 0)))
        w2h, w2l = _split_bf16(w2)
        q, k, v, r, la, dq, dk, dv = _in_proj(h, wm, wa, w2h, w2l, gla_gate_b[l][None, :])
        o_gla = _gla(q, k, v, r, la, gla_norm_g[l][None, :], B, S)
        g2 = jnp.tile(dil_norm_g[l], 2)[None, :]
        o_dil = _dilated(slopes, dq, dk, dv, g2, B, S)
        wo = w_out[l].astype(BF16)
        rw = jnp.concatenate([router_fine_w[l].reshape(D, N_EXPERTS), router_coarse_w[l]], axis=1)
        rw = jnp.pad(rw, ((0, 0), (0, LANES - N_EXPERTS - N_GROUPS)))
        rwh, rwl = _split_bf16(rw)
        rb = jnp.concatenate([router_fine_b[l].reshape(N_EXPERTS), router_coarse_b[l]])
        rb = jnp.pad(rb, (0, LANES - N_EXPERTS - N_GROUPS))[None, :]
        h1, h1p, eid, gate = _out_proj(o_gla, o_dil, h, wo[:GLA_WIDTH], wo[GLA_WIDTH:],
                                  ln1_g[l][None, :], ln1_b[l][None, :], rwh, rwl, rb)
        dest, be, nv = _positions(eid, n_blocks_pad)
        src_tok = _sc_inverse_rows(dest.reshape(2 * T), n_rows)
        xs = _sc_gather_rows(h1p, src_tok)
        be = be.reshape(n_blocks_pad)
        y = _ffn(be[:N_EXPERTS], be[LANES:LANES + N_EXPERTS], nv.reshape(LANES)[:1], xs,
                 expert_w_gate[l], expert_w_up[l], expert_w_down[l])
        y2 = _sc_gather_rows(y, dest.T.reshape(2 * T))
        h = _combine(h1, y2, gate, ln2_g[l][None, :], ln2_b[l][None, :])
    return h.reshape(B, S, D)
```

```python
import functools
import math

import jax
import jax.numpy as jnp
import numpy as np
from jax import lax
from jax.experimental import pallas as pl
from jax.experimental.pallas import tpu as pltpu
from jax.experimental.pallas import tpu_sc as plsc

D_MODEL = 1024
GLA_HEADS = 4
GLA_DK = 64
GLA_DV = 128
GLA_KEY_WIDTH = GLA_HEADS * GLA_DK
GLA_WIDTH = GLA_HEADS * GLA_DV
GLA_GATE_RANK = 16
GLA_GATE_TEMP = 16.0
DIL_HEADS = 8
DIL_DH = 64
DIL_WIDTH = DIL_HEADS * DIL_DH
DIL_CONFIGS = ((128, 1), (512, 4), (2048, 16))
DIL_BLOCK = 128
DIL_MAX_R = max(r for _, r in DIL_CONFIGS)
DIL_PAD = DIL_BLOCK * DIL_MAX_R
DIL_UNROLL = 8
N_GROUPS = 4
EXPERTS_PER_GROUP = 8
N_EXPERTS = N_GROUPS * EXPERTS_PER_GROUP
D_FF = 512
DEEPNORM_ALPHA = 2.0 ** 0.25
EPS = 1e-5
LOG2E = math.log2(math.e)
LN2 = math.log(2.0)

LANES = 128
GLA_CHUNK = 64
GLA_UNROLL = 4
SC_LANES = 16
SC_INDEX_WINDOW = 128
SC_GATHER_ROWS = 64
FFN_BUFFERS = 8
FFN_WEIGHT_CHUNKS = 4
ROW_BLOCK = 256
VMEM_LIMIT = 56 * 1024 * 1024

F32 = jnp.float32
BF16 = jnp.bfloat16


def _dot(a, b):
    return jnp.dot(a, b, preferred_element_type=F32)


def _dot_nt(a, b):
    return lax.dot_general(a, b, (((1,), (1,)), ((), ())), preferred_element_type=F32)


def _dot_tn(a, b):
    return lax.dot_general(a, b, (((0,), (0,)), ((), ())), preferred_element_type=F32)


def _split_bf16(v):
    hi = v.astype(BF16)
    lo = (v - hi.astype(F32)).astype(BF16)
    return hi, lo


def _pack_bf16_halves(v):
    w = v.shape[1] // 2
    hi = lax.bitcast_convert_type(v[:, :w].astype(BF16).astype(F32), jnp.int32)
    lo = lax.bitcast_convert_type(v[:, w:].astype(BF16).astype(F32), jnp.int32)
    return hi | lax.shift_right_logical(lo, 16)


def _unpack_bf16_halves(words):
    hi = lax.bitcast_convert_type(words & jnp.int32(-65536), F32)
    lo = lax.bitcast_convert_type(lax.shift_left(words, 16), F32)
    return hi, lo


def _layer_norm(v, g, b):
    mu = jnp.mean(v, axis=-1, keepdims=True)
    c = v - mu
    var = jnp.mean(c * c, axis=-1, keepdims=True)
    return c * lax.rsqrt(var + EPS) * g + b


def _in_proj_kernel(x_ref, wm_ref, wa_ref, w2h_ref, w2l_ref, gb_ref,
                    q_ref, k_ref, v_ref, r_ref, la_ref, dq_ref, dk_ref, dv_ref):
    xb = x_ref[...].astype(BF16)

    def piece(c0, c1):
        return _dot(xb, wm_ref[:, c0:c1])

    q_ref[...] = (piece(0, 256) * (GLA_DK ** -0.5)).astype(BF16)
    k_ref[...] = piece(256, 512).astype(BF16)
    v_ref[...] = piece(512, 1024).astype(BF16)
    r_ref[...] = piece(1024, 1536).astype(BF16)
    dq_ref[...] = (piece(1536, 2048) * (DIL_DH ** -0.5 * LOG2E)).astype(BF16)
    dk_ref[...] = piece(2048, 2560).astype(BF16)
    dv_ref[...] = piece(2560, 3072).astype(BF16)
    ga = _dot(xb, wa_ref[...])
    ga_hi, ga_lo = _split_bf16(ga)
    z = _dot(ga_hi, w2h_ref[...]) + _dot(ga_lo, w2h_ref[...]) + _dot(ga_hi, w2l_ref[...]) + gb_ref[...]
    log_sig = jnp.minimum(z, 0.0) - jnp.log1p(jnp.exp(-jnp.abs(z)))
    la_ref[...] = log_sig * (1.0 / GLA_GATE_TEMP)


def _in_proj(x2, wm, wa, w2h, w2l, gb, tm=1024):
    T = x2.shape[0]
    row = lambda w: pl.BlockSpec((tm, w), lambda i: (i, 0))
    full = lambda a: pl.BlockSpec(a.shape, lambda i: (0,) * a.ndim)
    outs = [(256, BF16), (256, BF16), (512, BF16), (512, BF16), (256, F32),
            (512, BF16), (512, BF16), (512, BF16)]
    return pl.pallas_call(
        _in_proj_kernel,
        grid=(T // tm,),
        in_specs=[row(D_MODEL), full(wm), full(wa), full(w2h), full(w2l), full(gb)],
        out_specs=[row(w) for w, _ in outs],
        out_shape=[jax.ShapeDtypeStruct((T, w), dt) for w, dt in outs],
        compiler_params=pltpu.CompilerParams(
            dimension_semantics=("parallel",), vmem_limit_bytes=VMEM_LIMIT),
        name="in_proj",
    )(x2, wm, wa, w2h, w2l, gb)


def _gla_kernel(q_ref, k_ref, v_ref, r_ref, la_ref, g_ref, o_ref, s_ref, *, seq_block):
    C = GLA_CHUNK

    @pl.when(pl.program_id(1) == 0)
    def _():
        s_ref[...] = jnp.zeros_like(s_ref)

    ri = lax.broadcasted_iota(jnp.int32, (C, C), 0)
    ci = lax.broadcasted_iota(jnp.int32, (C, C), 1)
    causal = ci <= ri
    tri = causal.astype(BF16)
    ones_cl = jnp.ones((C, LANES), BF16)
    lane_k = lax.broadcasted_iota(jnp.int32, (1, GLA_KEY_WIDTH), 1) // GLA_DK
    head_masks = [(lane_k == h).astype(F32) for h in range(GLA_HEADS)]
    srow = lax.broadcasted_iota(jnp.int32, (GLA_KEY_WIDTH, GLA_WIDTH), 0) // GLA_DK
    scol = lax.broadcasted_iota(jnp.int32, (GLA_KEY_WIDTH, GLA_WIDTH), 1) // GLA_DV
    state_mask = (srow == scol).astype(F32)
    g = g_ref[...]

    def chunk(c, carry):
        rows = pl.ds(pl.multiple_of(c * C, C), C)
        la = la_ref[rows, :]
        la_hi, la_lo = _split_bf16(la)
        b = _dot(tri, la_hi) + _dot(tri, la_lo)
        b_last = b[C - 1:C, :]
        q = q_ref[rows, :].astype(F32)
        k = k_ref[rows, :].astype(F32)
        v = v_ref[rows, :]
        qd = q * jnp.exp(b)
        kd = (k * jnp.exp(-b)).astype(BF16)
        ke = (k * jnp.exp(b_last - b)).astype(BF16)
        state = s_ref[...]
        o_inter = _dot(qd.astype(BF16), state.astype(BF16))
        outs = []
        for h in range(GLA_HEADS):
            a = _dot_nt((qd * head_masks[h]).astype(BF16), kd)
            a = jnp.where(causal, a, 0.0).astype(BF16)
            cols = slice(h * GLA_DV, (h + 1) * GLA_DV)
            o = _dot(a, v[:, cols]) + o_inter[:, cols]
            o = o * lax.rsqrt(jnp.mean(o * o, axis=-1, keepdims=True) + EPS) * g
            outs.append(o)
        o_all = jnp.concatenate(outs, axis=-1)
        rr = r_ref[rows, :].astype(F32)
        o_ref[rows, :] = (o_all * (rr * jax.nn.sigmoid(rr))).astype(BF16)
        tot = _dot_tn(la_hi, ones_cl) + _dot_tn(la_lo, ones_cl)
        dec = jnp.exp(tot)
        upd = _dot_tn(ke, v) * state_mask
        for h in range(GLA_HEADS):
            cols = slice(h * GLA_DV, (h + 1) * GLA_DV)
            s_ref[:, cols] = state[:, cols] * dec + upd[:, cols]
        return carry

    lax.fori_loop(0, seq_block // C, chunk, 0, unroll=GLA_UNROLL)


def _gla(q, k, v, r, la, g, batch, seq, seq_block=1024):
    nsb = seq // seq_block
    row = lambda w: pl.BlockSpec((seq_block, w), lambda b, s: (b * nsb + s, 0))
    return pl.pallas_call(
        functools.partial(_gla_kernel, seq_block=seq_block),
        grid=(batch, nsb),
        in_specs=[row(256), row(256), row(512), row(512), row(256),
                  pl.BlockSpec((1, GLA_DV), lambda b, s: (0, 0))],
        out_specs=row(512),
        out_shape=jax.ShapeDtypeStruct((batch * seq, GLA_WIDTH), BF16),
        scratch_shapes=[pltpu.VMEM((GLA_KEY_WIDTH, GLA_WIDTH), F32)],
        compiler_params=pltpu.CompilerParams(
            dimension_semantics=("parallel", "arbitrary"), vmem_limit_bytes=VMEM_LIMIT),
        name="gla",
    )(q, k, v, r, la, g)


def _dil_kernel(slope_ref, q_ref, k_ref, v_ref, g_ref, o_ref,
                qf, kf, vf, kd, va, vb, oc, lc, *, seq):
    B = DIL_BLOCK
    U = DIL_UNROLL
    pair = pl.program_id(1)
    qf[...] = q_ref[...].astype(F32)
    kf[...] = k_ref[...].astype(F32)
    vf[...] = v_ref[...].astype(F32)

    lane = lax.broadcasted_iota(jnp.int32, (1, LANES), 1)
    first = lane < DIL_DH
    ii = lax.broadcasted_iota(jnp.int32, (B, B), 0)
    jj = lax.broadcasted_iota(jnp.int32, (B, B), 1)
    upper = jj > ii
    eye = jj == ii
    dist = jnp.bitwise_and(ii - jj, B - 1).astype(F32)
    neg = jnp.float32(-jnp.inf)
    neg_tile = jnp.full((B, B), neg, F32)
    zero_tile = jnp.zeros((B, LANES), BF16)

    for cfg, (window, r) in enumerate(DIL_CONFIGS):
        nb = seq // r // B
        cs = nb + 1
        bias_prev, bias_cur = [], []
        for hh in range(2):
            slope = slope_ref[2 * pair + hh] * (float(r) * LOG2E)
            bias = dist * (-slope)
            bias_prev.append(jnp.where(upper, bias, jnp.where(eye, -slope * float(B), neg)))
            bias_cur.append(jnp.where(upper, neg, bias))

        for c in range(r):
            rows0 = slice(c * cs * B, (c * cs + 1) * B)
            kd[rows0, :] = zero_tile
            va[rows0, :] = zero_tile
            vb[rows0, :] = zero_tile

        def prep(t4, carry, r=r, nb=nb, cs=cs):
            for j in range(4):
                t = t4 * 4 + j
                c = t // nb
                n = t % nb
                start = c + n * (B * r)
                rows = pl.ds(start, B, stride=r) if r > 1 else pl.ds(pl.multiple_of(start, B), B)
                dst = pl.ds(pl.multiple_of((c * cs + 1 + n) * B, B), B)
                kd[dst, :] = kf[rows, :].astype(BF16)
                v = vf[rows, :]
                va[dst, :] = jnp.where(first, v, 1.0).astype(BF16)
                vb[dst, :] = jnp.where(first, 1.0, v).astype(BF16)
            return carry

        lax.fori_loop(0, seq // B // 4, prep, 0)

        def block(it, has_prev, cfg=cfg, r=r, nb=nb, cs=cs, bias_prev=bias_prev, bias_cur=bias_cur):
            c = it // nb
            n = it % nb
            start = c + n * (B * r)
            rows = pl.ds(start, B, stride=r) if r > 1 else pl.ds(pl.multiple_of(start, B), B)
            kv = pl.ds(pl.multiple_of((c * cs + n) * B, B), 2 * B)
            q = qf[rows, :]
            kcat = kd[kv, :]
            acc, mx = [], []
            for hh in range(2):
                hmask = first if hh == 0 else jnp.logical_not(first)
                qh = jnp.where(hmask, q, 0.0).astype(BF16)
                s2 = _dot_nt(qh, kcat)
                if has_prev is True:
                    bp = bias_prev[hh]
                elif has_prev is False:
                    bp = neg_tile
                else:
                    bp = jnp.where(has_prev, bias_prev[hh], neg)
                s_prev = s2[:, 0:B] + bp
                s_cur = s2[:, B:2 * B] + bias_cur[hh]
                m = jnp.max(jnp.maximum(s_prev, s_cur), axis=-1, keepdims=True)
                pcat = jnp.concatenate([jnp.exp2(s_prev - m), jnp.exp2(s_cur - m)], axis=1).astype(BF16)
                vals = va[kv, :] if hh == 0 else vb[kv, :]
                acc.append(_dot(pcat, vals))
                mx.append(m)
            num = jnp.where(first, acc[0], acc[1])
            den = pltpu.roll(jnp.where(first, acc[1], acc[0]), DIL_DH, axis=1)
            oc[cfg, rows, :] = num * (1.0 / den)
            lc[cfg, rows, :] = (jnp.where(first, mx[0], mx[1]) + jnp.log2(den)) * LN2

        def body(t, carry, block=block, nb=nb):
            for u in range(U):
                if nb % U == 0:
                    has_prev = True if u > 0 else (t * U) % nb > 0
                else:
                    assert U % nb == 0
                    has_prev = (u % nb) > 0
                block(t * U + u, has_prev)
            return carry

        lax.fori_loop(0, seq // B // U, body, 0)

    g = g_ref[...]
    CH = 512

    def mix(i, carry):
        rows = pl.ds(pl.multiple_of(i * CH, CH), CH)
        l0, l1, l2 = lc[0, rows, :], lc[1, rows, :], lc[2, rows, :]
        m = jnp.maximum(jnp.maximum(l0, l1), l2)
        e0, e1, e2 = jnp.exp(l0 - m), jnp.exp(l1 - m), jnp.exp(l2 - m)
        den = e0 + e1 + e2
        o = (e0 / den) * oc[0, rows, :] + (e1 / den) * oc[1, rows, :] + (e2 / den) * oc[2, rows, :]
        sq = o * o
        ms_a = jnp.sum(jnp.where(first, sq, 0.0), axis=-1, keepdims=True) * (1.0 / DIL_DH)
        ms_b = jnp.sum(jnp.where(first, 0.0, sq), axis=-1, keepdims=True) * (1.0 / DIL_DH)
        ms = jnp.where(first, ms_a, ms_b)
        o_ref[rows, :] = (o * lax.rsqrt(ms + EPS) * g).astype(BF16)
        return carry

    lax.fori_loop(0, seq // CH, mix, 0)


def _dilated(slopes, dq, dk, dv, g2, batch, seq):
    blk = pl.BlockSpec((seq, LANES), lambda b, p, s: (b, p))
    return pl.pallas_call(
        functools.partial(_dil_kernel, seq=seq),
        grid_spec=pltpu.PrefetchScalarGridSpec(
            num_scalar_prefetch=1,
            grid=(batch, DIL_WIDTH // LANES),
            in_specs=[blk, blk, blk, pl.BlockSpec((1, LANES), lambda b, p, s: (0, 0))],
            out_specs=blk,
            scratch_shapes=[pltpu.VMEM((seq, LANES), F32)] * 3
                           + [pltpu.VMEM((seq + DIL_PAD, LANES), BF16)] * 3
                           + [pltpu.VMEM((3, seq, LANES), F32)] * 2,
        ),
        out_shape=jax.ShapeDtypeStruct((batch * seq, DIL_WIDTH), BF16),
        compiler_params=pltpu.CompilerParams(
            dimension_semantics=("parallel", "parallel"), vmem_limit_bytes=VMEM_LIMIT),
        name="dilated",
    )(slopes, dq, dk, dv, g2)


def _out_proj_kernel(og_ref, od_ref, x_ref, wg_ref, wd_ref, g_ref, b_ref,
                     rwh_ref, rwl_ref, rb_ref, h_ref, hp_ref, eid_ref, gate_ref, cnt_ref):
    mix = _dot(og_ref[...], wg_ref[...]) + _dot(od_ref[...], wd_ref[...])
    h = _layer_norm(DEEPNORM_ALPHA * x_ref[...] + mix, g_ref[...], b_ref[...])
    h_ref[...] = h
    hp_ref[...] = _pack_bf16_halves(h)
    h_hi, h_lo = _split_bf16(h)
    lt = (_dot_nt(rwh_ref[...], h_hi) + _dot_nt(rwh_ref[...], h_lo) + _dot_nt(rwl_ref[...], h_hi)
          + rb_ref[...])
    tm = lt.shape[1]
    row = lax.broadcasted_iota(jnp.int32, (EXPERTS_PER_GROUP, tm), 0).astype(F32)
    neg = jnp.float32(-jnp.inf)
    big = jnp.float32(1e9)
    coarse = jnp.where(row < N_GROUPS, lt[N_EXPERTS:N_EXPERTS + EXPERTS_PER_GROUP, :], neg)
    cmax = jnp.max(coarse, axis=0, keepdims=True)
    g_idx = jnp.min(jnp.where(coarse == cmax, row, big), axis=0, keepdims=True)
    p_group = 1.0 / jnp.sum(jnp.exp(coarse - cmax), axis=0, keepdims=True)
    fine = lt[(N_GROUPS - 1) * EXPERTS_PER_GROUP:N_EXPERTS, :]
    for g in range(N_GROUPS - 2, -1, -1):
        fine = jnp.where(g_idx == g, lt[g * EXPERTS_PER_GROUP:(g + 1) * EXPERTS_PER_GROUP, :], fine)
    v1 = jnp.max(fine, axis=0, keepdims=True)
    i1 = jnp.min(jnp.where(fine == v1, row, big), axis=0, keepdims=True)
    fine2 = jnp.where(row == i1, neg, fine)
    v2 = jnp.max(fine2, axis=0, keepdims=True)
    i2 = jnp.min(jnp.where(fine2 == v2, row, big), axis=0, keepdims=True)
    e2 = jnp.exp(v2 - v1)
    den = 1.0 + e2
    gate1 = p_group * (1.0 / den)
    gate2 = p_group * (e2 / den)
    id1 = g_idx * EXPERTS_PER_GROUP + i1
    id2 = g_idx * EXPERTS_PER_GROUP + i2
    eid_ref[...] = jnp.concatenate([id1, id2], axis=0).astype(jnp.int32)
    slab = jnp.concatenate([gate1, gate2, jnp.zeros((LANES - 2, tm), F32)], axis=0)
    gate_ref[...] = slab.T[:, 0:2]
    sub = lax.broadcasted_iota(jnp.int32, (LANES, tm), 0).astype(F32)
    onehot = jnp.logical_or(sub == id1, sub == id2).astype(BF16)

    @pl.when(pl.program_id(0) == 0)
    def _():
        cnt_ref[...] = jnp.zeros_like(cnt_ref)

    cnt_ref[...] += _dot(onehot, jnp.ones((tm, LANES), BF16))


def _out_proj(og, od, x2, wg, wd, g, b, rwh, rwl, rb, tm=1024):
    T = x2.shape[0]
    row = lambda w: pl.BlockSpec((tm, w), lambda i: (i, 0))
    full = lambda a: pl.BlockSpec(a.shape, lambda i: (0,) * a.ndim)
    return pl.pallas_call(
        _out_proj_kernel,
        grid=(T // tm,),
        in_specs=[row(512), row(512), row(D_MODEL), full(wg), full(wd), full(g), full(b),
                  full(rwh), full(rwl), full(rb)],
        out_specs=[row(D_MODEL), row(D_MODEL // 2), pl.BlockSpec((2, tm), lambda i: (0, i)), row(2),
                   pl.BlockSpec((LANES, LANES), lambda i: (0, 0))],
        out_shape=[jax.ShapeDtypeStruct((T, D_MODEL), F32),
                   jax.ShapeDtypeStruct((T, D_MODEL // 2), jnp.int32),
                   jax.ShapeDtypeStruct((2, T), jnp.int32),
                   jax.ShapeDtypeStruct((T, 2), F32),
                   jax.ShapeDtypeStruct((LANES, LANES), F32)],
        compiler_params=pltpu.CompilerParams(
            dimension_semantics=("arbitrary",), vmem_limit_bytes=VMEM_LIMIT),
        name="out_proj_router",
    )(og, od, x2, wg, wd, g, b, rwh, rwl, rb)


def _positions_kernel(eid_ref, cnt_ref, dest_ref, be_ref, nv_ref, carry_ref, sp_ref, tri_ref, *, tb):
    i = pl.program_id(0)

    @pl.when(i == 0)
    def _():
        shift = int(math.log2(ROW_BLOCK))
        nb_col = (cnt_ref[...].astype(jnp.int32) + (ROW_BLOCK - 1)) >> shift
        r = lax.broadcasted_iota(jnp.int32, (LANES, LANES), 0)
        c = lax.broadcasted_iota(jnp.int32, (LANES, LANES), 1)
        nb_f = jnp.where(r < N_EXPERTS, nb_col, 0).astype(F32)
        start_col = _dot((c < r).astype(BF16), nb_f.astype(BF16))
        sp_ref[...] = start_col * float(ROW_BLOCK)
        carry_ref[...] = jnp.zeros_like(carry_ref)
        be_ref[...] = jnp.concatenate([start_col.T[0:1, :], nb_f.T[0:1, :]], axis=1).astype(jnp.int32)
        total = jnp.sum(nb_f[:, 0:1], axis=0, keepdims=True)
        nv_ref[...] = jnp.broadcast_to(total, (1, LANES)).astype(jnp.int32)
        tr = lax.broadcasted_iota(jnp.int32, (tb, tb), 0)
        tc = lax.broadcasted_iota(jnp.int32, (tb, tb), 1)
        tri_ref[...] = (tr < tc).astype(BF16)

    sub = lax.broadcasted_iota(jnp.int32, (LANES, tb), 0)
    oh1 = sub == eid_ref[0:1, :]
    oh2 = sub == eid_ref[1:2, :]
    oh = jnp.logical_or(oh1, oh2).astype(BF16)
    offset = jnp.tile(carry_ref[...] + sp_ref[...], (1, tb // LANES))
    before = _dot(oh, tri_ref[...]) + offset
    d1 = jnp.sum(jnp.where(oh1, before, 0.0), axis=0, keepdims=True)
    d2 = jnp.sum(jnp.where(oh2, before, 0.0), axis=0, keepdims=True)
    dest_ref[...] = jnp.concatenate([d1, d2], axis=0).astype(jnp.int32)
    carry_ref[...] += _dot(oh, jnp.ones((tb, LANES), BF16))


def _positions(eid_t, cnt, n_blocks_pad, tb=1024):
    T = eid_t.shape[1]
    return pl.pallas_call(
        functools.partial(_positions_kernel, tb=tb),
        grid=(T // tb,),
        in_specs=[pl.BlockSpec((2, tb), lambda i: (0, i)), pl.BlockSpec((LANES, LANES), lambda i: (0, 0))],
        out_specs=[pl.BlockSpec((2, tb), lambda i: (0, i)),
                   pl.BlockSpec((1, n_blocks_pad), lambda i: (0, 0)),
                   pl.BlockSpec((1, LANES), lambda i: (0, 0))],
        out_shape=[jax.ShapeDtypeStruct((2, T), jnp.int32),
                   jax.ShapeDtypeStruct((1, n_blocks_pad), jnp.int32),
                   jax.ShapeDtypeStruct((1, LANES), jnp.int32)],
        scratch_shapes=[pltpu.VMEM((LANES, LANES), F32), pltpu.VMEM((LANES, LANES), F32),
                        pltpu.VMEM((tb, tb), BF16)],
        compiler_params=pltpu.CompilerParams(dimension_semantics=("arbitrary",)),
        name="positions",
    )(eid_t, cnt)


def _sc_gather_rows(table, idx):
    n = idx.shape[0]
    d = table.shape[1]
    info = plsc.get_sparse_core_info()
    nc, ns = info.num_cores, info.num_subcores
    per_w = n // (nc * ns)
    assert per_w * nc * ns == n and per_w % SC_INDEX_WINDOW == 0
    mesh = plsc.VectorSubcoreMesh(core_axis_name="core", subcore_axis_name="subcore")
    nsub = SC_INDEX_WINDOW // SC_GATHER_ROWS

    @functools.partial(
        pl.kernel, out_type=jax.ShapeDtypeStruct((n, d), table.dtype), mesh=mesh,
        scratch_types=[pltpu.VMEM((SC_INDEX_WINDOW,), jnp.int32),
                       pltpu.VMEM((2, SC_GATHER_ROWS, d), table.dtype),
                       pltpu.SemaphoreType.DMA((2,)), pltpu.SemaphoreType.DMA((2,))],
        name="sc_gather_rows")
    def gather(x_hbm, i_hbm, o_hbm, idx_v, buf, gsem, wsem):
        wid = lax.axis_index("subcore") * nc + lax.axis_index("core")
        base = wid * per_w

        def gather_copy(s):
            rows = idx_v.at[pl.ds(s * SC_GATHER_ROWS, SC_GATHER_ROWS)]
            return pltpu.make_async_copy(x_hbm.at[rows], buf.at[s % 2], gsem.at[s % 2])

        def write_copy(off, s):
            dst = o_hbm.at[pl.ds(off + s * SC_GATHER_ROWS, SC_GATHER_ROWS)]
            return pltpu.make_async_copy(buf.at[s % 2], dst, wsem.at[s % 2])

        @pl.loop(0, per_w // SC_INDEX_WINDOW)
        def _(j):
            off = base + j * SC_INDEX_WINDOW
            pltpu.sync_copy(i_hbm.at[pl.ds(off, SC_INDEX_WINDOW)], idx_v)
            gather_copy(0).start()
            for s in range(nsub):
                gather_copy(s).wait()
                write_copy(off, s).start()
                if s >= 1:
                    write_copy(off, s - 1).wait()
                if s + 1 < nsub:
                    gather_copy(s + 1).start()
            write_copy(off, nsub - 1).wait()

    return gather(table, idx)


def _sc_inverse_rows(dest_flat, n_rows, chunk=2048):
    n = dest_flat.shape[0]
    n_tokens = n // 2
    assert n_rows <= 3 * n_tokens
    nc = plsc.get_sparse_core_info().num_cores
    mesh = plsc.VectorSubcoreMesh(core_axis_name="core", subcore_axis_name="subcore")

    @functools.partial(
        pl.kernel, out_type=jax.ShapeDtypeStruct((n_rows,), jnp.int32), mesh=mesh,
        scratch_types=[pltpu.VMEM((n_rows,), jnp.int32), pltpu.VMEM((chunk,), jnp.int32)],
        compiler_params=pltpu.CompilerParams(needs_layout_passes=False),
        name="sc_inverse_rows")
    def inverse(d_hbm, o_hbm, inv_v, d_v):
        wid = lax.axis_index("subcore") * nc + lax.axis_index("core")

        @pl.when(wid == 0)
        def _():
            lanes = lax.iota(jnp.int32, SC_LANES)

            @pl.loop(0, n_rows // SC_LANES)
            def _(i):
                r = lanes + i * SC_LANES
                r = jnp.where(r >= n_tokens, r - n_tokens, r)
                inv_v[pl.ds(i * SC_LANES, SC_LANES)] = jnp.where(r >= n_tokens, r - n_tokens, r)

            @pl.loop(0, n // chunk)
            def _(c):
                pltpu.sync_copy(d_hbm.at[pl.ds(c * chunk, chunk)], d_v)

                @pl.loop(0, chunk // SC_LANES)
                def _(j):
                    rows = d_v[pl.ds(j * SC_LANES, SC_LANES)]
                    pair = lax.iota(jnp.int32, SC_LANES) + (c * chunk + j * SC_LANES)
                    plsc.store_scatter(inv_v, [rows], jnp.where(pair >= n_tokens, pair - n_tokens, pair))

            pltpu.sync_copy(inv_v, o_hbm)

    return inverse(dest_flat)


def _ffn_kernel(first_ref, count_ref, nv_ref, wg_hbm, wu_hbm, wd_hbm, xs_hbm, y_hbm,
                wg32, wu32, wd32, wgb, wub, wdb, xbuf, ybuf, wsem, isem, osem, *, n_blocks):
    nv = nv_ref[0]
    nbuf = FFN_BUFFERS

    def next_expert(e):
        def more(t):
            return jnp.logical_and(t < N_EXPERTS, count_ref[jnp.minimum(t, N_EXPERTS - 1)] == 0)
        return lax.while_loop(more, lambda t: t + 1, e + 1)

    def weight_copies(e, slot):
        ee = jnp.minimum(e, N_EXPERTS - 1)
        copies = []
        for i, (src, dst) in enumerate(((wg_hbm, wg32), (wu_hbm, wu32), (wd_hbm, wd32))):
            rows = src.shape[1] // FFN_WEIGHT_CHUNKS
            for c in range(FFN_WEIGHT_CHUNKS):
                part = pl.ds(c * rows, rows)
                copies.append(pltpu.make_async_copy(src.at[ee, part], dst.at[slot, part],
                                                    wsem.at[slot, i * FFN_WEIGHT_CHUNKS + c]))
        return copies

    def fetch_weights(e, slot):
        @pl.when(e < N_EXPERTS)
        def _():
            for c in weight_copies(e, slot):
                c.start()

    def take_weights(e, slot):
        for c in weight_copies(e, slot):
            c.wait()
        wgb[...] = wg32[slot].astype(BF16)
        wub[...] = wu32[slot].astype(BF16)
        wdb[...] = wd32[slot].astype(BF16)
        fetch_weights(next_expert(next_expert(e)), slot)

    def rows_of(b):
        return pl.ds(pl.multiple_of(b * ROW_BLOCK, ROW_BLOCK), ROW_BLOCK)

    def in_copy(b):
        return pltpu.make_async_copy(xs_hbm.at[rows_of(b)], xbuf.at[b % nbuf], isem.at[b % nbuf])

    def out_copy(b):
        return pltpu.make_async_copy(ybuf.at[b % nbuf], y_hbm.at[rows_of(b)], osem.at[b % nbuf])

    @pl.when(nv > 0)
    def _():
        e0 = next_expert(jnp.int32(-1))
        for i in range(nbuf - 1):
            @pl.when(i < nv)
            def _():
                in_copy(i).start()
        fetch_weights(e0, 0)
        fetch_weights(next_expert(e0), 1)
        take_weights(e0, 0)

        def body(b, carry):
            e, k = carry
            switch = b >= first_ref[e] + count_ref[e]
            e_new = jnp.where(switch, next_expert(e), e)
            k_new = jnp.where(switch, k + 1, k)

            @pl.when(switch)
            def _():
                take_weights(e_new, k_new % 2)

            in_copy(b).wait()

            @pl.when(b + nbuf - 1 < nv)
            def _():
                in_copy(b + nbuf - 1).start()

            @pl.when(b >= nbuf)
            def _():
                out_copy(b - nbuf).wait()

            x_hi, x_lo = _unpack_bf16_halves(xbuf[b % nbuf])
            xb = jnp.concatenate([x_hi.astype(BF16), x_lo.astype(BF16)], axis=1)
            a = _dot(xb, wgb[...])
            u = _dot(xb, wub[...])
            hid = (a * jax.nn.sigmoid(a) * u).astype(BF16)
            ybuf[b % nbuf] = _pack_bf16_halves(_dot(hid, wdb[...]))
            out_copy(b).start()
            return e_new, k_new

        lax.fori_loop(0, nv, body, (e0, jnp.int32(0)))

        for i in range(nbuf):
            @pl.when(nv > i)
            def _():
                out_copy(nv - 1 - i).wait()

    ybuf[0] = jnp.zeros((ROW_BLOCK, D_MODEL // 2), jnp.int32)

    def fill(b, carry):
        pltpu.sync_copy(ybuf.at[0], y_hbm.at[rows_of(b)])
        return carry

    lax.fori_loop(nv, n_blocks, fill, 0)


def _ffn(first_blk, n_blk, nv, xs, w_gate, w_up, w_down):
    n_rows = xs.shape[0]
    n_blocks = n_rows // ROW_BLOCK
    anyspec = pl.BlockSpec(memory_space=pl.ANY)
    return pl.pallas_call(
        functools.partial(_ffn_kernel, n_blocks=n_blocks),
        grid_spec=pltpu.PrefetchScalarGridSpec(
            num_scalar_prefetch=3,
            grid=(1,),
            in_specs=[anyspec, anyspec, anyspec, anyspec],
            out_specs=anyspec,
            scratch_shapes=[pltpu.VMEM((2, D_MODEL, D_FF), F32), pltpu.VMEM((2, D_MODEL, D_FF), F32),
                            pltpu.VMEM((2, D_FF, D_MODEL), F32),
                            pltpu.VMEM((D_MODEL, D_FF), BF16), pltpu.VMEM((D_MODEL, D_FF), BF16),
                            pltpu.VMEM((D_FF, D_MODEL), BF16),
                            pltpu.VMEM((FFN_BUFFERS, ROW_BLOCK, D_MODEL // 2), jnp.int32),
                            pltpu.VMEM((FFN_BUFFERS, ROW_BLOCK, D_MODEL // 2), jnp.int32),
                            pltpu.SemaphoreType.DMA((2, 3 * FFN_WEIGHT_CHUNKS)),
                            pltpu.SemaphoreType.DMA((FFN_BUFFERS,)),
                            pltpu.SemaphoreType.DMA((FFN_BUFFERS,))],
        ),
        out_shape=jax.ShapeDtypeStruct((n_rows, D_MODEL // 2), jnp.int32),
        compiler_params=pltpu.CompilerParams(
            dimension_semantics=("arbitrary",), vmem_limit_bytes=VMEM_LIMIT),
        name="expert_ffn",
    )(first_blk, n_blk, nv, w_gate, w_up, w_down, xs)


def _combine_kernel(h_ref, ya_ref, yb_ref, gate_ref, g_ref, b_ref, o_ref):
    gate = gate_ref[...]
    a_hi, a_lo = _unpack_bf16_halves(ya_ref[...])
    b_hi, b_lo = _unpack_bf16_halves(yb_ref[...])
    g0, g1 = gate[:, 0:1], gate[:, 1:2]
    ffn = jnp.concatenate([a_hi * g0 + b_hi * g1, a_lo * g0 + b_lo * g1], axis=1)
    o_ref[...] = _layer_norm(DEEPNORM_ALPHA * h_ref[...] + ffn, g_ref[...], b_ref[...])


def _combine(h, y2, gate, g, b, tm=512):
    T = h.shape[0]
    nt = T // tm
    return pl.pallas_call(
        _combine_kernel,
        grid=(nt,),
        in_specs=[pl.BlockSpec((tm, D_MODEL), lambda i: (i, 0)),
                  pl.BlockSpec((tm, D_MODEL // 2), lambda i: (i, 0)),
                  pl.BlockSpec((tm, D_MODEL // 2), lambda i: (i + nt, 0)),
                  pl.BlockSpec((tm, 2), lambda i: (i, 0)),
                  pl.BlockSpec((1, D_MODEL), lambda i: (0, 0)),
                  pl.BlockSpec((1, D_MODEL), lambda i: (0, 0))],
        out_specs=pl.BlockSpec((tm, D_MODEL), lambda i: (i, 0)),
        out_shape=jax.ShapeDtypeStruct((T, D_MODEL), F32),
        compiler_params=pltpu.CompilerParams(
            dimension_semantics=("parallel",), vmem_limit_bytes=VMEM_LIMIT),
        name="combine",
    )(h, y2, y2, gate, g, b)


def kernel(x, w_in, gla_gate_w2, gla_gate_b, gla_norm_g, dil_norm_g, w_out, ln1_g, ln1_b,
           router_coarse_w, router_coarse_b, router_fine_w, router_fine_b,
           expert_w_gate, expert_w_up, expert_w_down, ln2_g, ln2_b):
    B, S, D = x.shape
    T = B * S
    depth = w_in.shape[0]
    slopes = jnp.exp2(-8.0 * jnp.arange(1, DIL_HEADS + 1, dtype=F32) / DIL_HEADS)
    n_rows = 2 * T + N_EXPERTS * ROW_BLOCK
    n_blocks = n_rows // ROW_BLOCK
    n_blocks_pad = -(-n_blocks // (2 * LANES)) * (2 * LANES)
    assert n_blocks_pad == 2 * LANES
    a0 = 1536
    h = x.reshape(T, D)
    for l in range(depth):
        w = w_in[l]
        wm = jnp.concatenate([w[:, :a0], w[:, a0 + GLA_GATE_RANK:]], axis=1).astype(BF16)
        wa = jnp.pad(w[:, a0:a0 + GLA_GATE_RANK], ((0, 0), (0, LANES - GLA_GATE_RANK))).astype(BF16)
        w2 = jnp.pad(gla_gate_w2[l], ((0, LANES - GLA_GATE_RANK), (0, 0)))
        w2h, w2l = _split_bf16(w2)
        q, k, v, r, la, dq, dk, dv = _in_proj(h, wm, wa, w2h, w2l, gla_gate_b[l][None, :])
        o_gla = _gla(q, k, v, r, la, gla_norm_g[l][None, :], B, S)
        g2 = jnp.tile(dil_norm_g[l], 2)[None, :]
        o_dil = _dilated(slopes, dq, dk, dv, g2, B, S)
        wo = w_out[l].astype(BF16)
        rw = jnp.concatenate([router_fine_w[l].reshape(D, N_EXPERTS), router_coarse_w[l]], axis=1)
        rw = jnp.pad(rw, ((0, 0), (0, LANES - N_EXPERTS - N_GROUPS))).T
        rwh, rwl = _split_bf16(rw)
        rb = jnp.concatenate([router_fine_b[l].reshape(N_EXPERTS), router_coarse_b[l]])
        rb = jnp.pad(rb, (0, LANES - N_EXPERTS - N_GROUPS))[:, None]
        h1, h1p, eid_t, gate, cnt = _out_proj(o_gla, o_dil, h, wo[:GLA_WIDTH], wo[GLA_WIDTH:],
                                              ln1_g[l][None, :], ln1_b[l][None, :], rwh, rwl, rb)
        dest_t, be, nv = _positions(eid_t, cnt, n_blocks_pad)
        dest_flat = dest_t.reshape(2 * T)
        src_tok = _sc_inverse_rows(dest_flat, n_rows)
        xs = _sc_gather_rows(h1p, src_tok)
        be = be.reshape(n_blocks_pad)
        y = _ffn(be[:N_EXPERTS], be[LANES:LANES + N_EXPERTS], nv.reshape(LANES)[:1], xs,
                 expert_w_gate[l], expert_w_up[l], expert_w_down[l])
        y2 = _sc_gather_rows(y, dest_flat)
        h = _combine(h1, y2, gate, ln2_g[l][None, :], ln2_b[l][None, :])
    return h.reshape(B, S, D)
```

```python
import functools
import math

import jax
import jax.numpy as jnp
import numpy as np
from jax import lax
from jax.experimental import pallas as pl
from jax.experimental.pallas import tpu as pltpu
from jax.experimental.pallas import tpu_sc as plsc

D_MODEL = 1024
GLA_HEADS = 4
GLA_DK = 64
GLA_DV = 128
GLA_KEY_WIDTH = GLA_HEADS * GLA_DK
GLA_WIDTH = GLA_HEADS * GLA_DV
GLA_GATE_RANK = 16
GLA_GATE_TEMP = 16.0
DIL_HEADS = 8
DIL_DH = 64
DIL_WIDTH = DIL_HEADS * DIL_DH
DIL_CONFIGS = ((128, 1), (512, 4), (2048, 16))
DIL_BLOCK = 128
DIL_MAX_R = max(r for _, r in DIL_CONFIGS)
DIL_PAD = DIL_BLOCK * DIL_MAX_R
DIL_UNROLL = 8
N_GROUPS = 4
EXPERTS_PER_GROUP = 8
N_EXPERTS = N_GROUPS * EXPERTS_PER_GROUP
D_FF = 512
DEEPNORM_ALPHA = 2.0 ** 0.25
EPS = 1e-5
LOG2E = math.log2(math.e)
LN2 = math.log(2.0)

LANES = 128
GLA_CHUNK = 64
GLA_UNROLL = 4
SC_LANES = 16
SC_INDEX_WINDOW = 128
SC_GATHER_BUFFERS = 6
SC_GATHER_ROWS = 32
FFN_BUFFERS = 8
FFN_WEIGHT_CHUNKS = 4
ROW_BLOCK = 256
VMEM_LIMIT = 56 * 1024 * 1024

F32 = jnp.float32
BF16 = jnp.bfloat16


def _dot(a, b):
    return jnp.dot(a, b, preferred_element_type=F32)


def _dot_nt(a, b):
    return lax.dot_general(a, b, (((1,), (1,)), ((), ())), preferred_element_type=F32)


def _dot_tn(a, b):
    return lax.dot_general(a, b, (((0,), (0,)), ((), ())), preferred_element_type=F32)


def _split_bf16(v):
    hi = v.astype(BF16)
    lo = (v - hi.astype(F32)).astype(BF16)
    return hi, lo


def _pack_bf16_halves(v):
    w = v.shape[1] // 2
    hi = lax.bitcast_convert_type(v[:, :w].astype(BF16).astype(F32), jnp.int32)
    lo = lax.bitcast_convert_type(v[:, w:].astype(BF16).astype(F32), jnp.int32)
    return hi | lax.shift_right_logical(lo, 16)


def _unpack_bf16_halves(words):
    hi = lax.bitcast_convert_type(words & jnp.int32(-65536), F32)
    lo = lax.bitcast_convert_type(lax.shift_left(words, 16), F32)
    return hi, lo


def _layer_norm(v, g, b):
    mu = jnp.mean(v, axis=-1, keepdims=True)
    c = v - mu
    var = jnp.mean(c * c, axis=-1, keepdims=True)
    return c * lax.rsqrt(var + EPS) * g + b


def _in_proj_kernel(x_ref, wm_ref, wa_ref, w2h_ref, w2l_ref, gb_ref,
                    q_ref, k_ref, v_ref, r_ref, la_ref, dq_ref, dk_ref, dv_ref):
    xb = x_ref[...].astype(BF16)

    def piece(c0, c1):
        return _dot(xb, wm_ref[:, c0:c1])

    q_ref[...] = (piece(0, 256) * (GLA_DK ** -0.5)).astype(BF16)
    k_ref[...] = piece(256, 512).astype(BF16)
    v_ref[...] = piece(512, 1024).astype(BF16)
    r_ref[...] = piece(1024, 1536).astype(BF16)
    dq_ref[...] = (piece(1536, 2048) * (DIL_DH ** -0.5 * LOG2E)).astype(BF16)
    dk_ref[...] = piece(2048, 2560).astype(BF16)
    dv_ref[...] = piece(2560, 3072).astype(BF16)
    ga = _dot(xb, wa_ref[...])
    ga_hi, ga_lo = _split_bf16(ga)
    z = _dot(ga_hi, w2h_ref[...]) + _dot(ga_lo, w2h_ref[...]) + _dot(ga_hi, w2l_ref[...]) + gb_ref[...]
    log_sig = jnp.minimum(z, 0.0) - jnp.log1p(jnp.exp(-jnp.abs(z)))
    la_ref[...] = log_sig * (1.0 / GLA_GATE_TEMP)


def _in_proj(x2, wm, wa, w2h, w2l, gb, tm=1024):
    T = x2.shape[0]
    row = lambda w: pl.BlockSpec((tm, w), lambda i: (i, 0))
    full = lambda a: pl.BlockSpec(a.shape, lambda i: (0,) * a.ndim)
    outs = [(256, BF16), (256, BF16), (512, BF16), (512, BF16), (256, F32),
            (512, BF16), (512, BF16), (512, BF16)]
    return pl.pallas_call(
        _in_proj_kernel,
        grid=(T // tm,),
        in_specs=[row(D_MODEL), full(wm), full(wa), full(w2h), full(w2l), full(gb)],
        out_specs=[row(w) for w, _ in outs],
        out_shape=[jax.ShapeDtypeStruct((T, w), dt) for w, dt in outs],
        compiler_params=pltpu.CompilerParams(
            dimension_semantics=("parallel",), vmem_limit_bytes=VMEM_LIMIT),
        name="in_proj",
    )(x2, wm, wa, w2h, w2l, gb)


def _gla_kernel(q_ref, k_ref, v_ref, r_ref, la_ref, g_ref, o_ref, s_ref, *, seq_block):
    C = GLA_CHUNK

    @pl.when(pl.program_id(1) == 0)
    def _():
        s_ref[...] = jnp.zeros_like(s_ref)

    ri = lax.broadcasted_iota(jnp.int32, (C, C), 0)
    ci = lax.broadcasted_iota(jnp.int32, (C, C), 1)
    causal = ci <= ri
    tri = causal.astype(BF16)
    ones_cl = jnp.ones((C, LANES), BF16)
    lane_k = lax.broadcasted_iota(jnp.int32, (1, GLA_KEY_WIDTH), 1) // GLA_DK
    head_masks = [(lane_k == h).astype(F32) for h in range(GLA_HEADS)]
    srow = lax.broadcasted_iota(jnp.int32, (GLA_KEY_WIDTH, GLA_WIDTH), 0) // GLA_DK
    scol = lax.broadcasted_iota(jnp.int32, (GLA_KEY_WIDTH, GLA_WIDTH), 1) // GLA_DV
    state_mask = (srow == scol).astype(F32)
    g = g_ref[...]

    def chunk(c, carry):
        rows = pl.ds(pl.multiple_of(c * C, C), C)
        la = la_ref[rows, :]
        la_hi, la_lo = _split_bf16(la)
        b = _dot(tri, la_hi) + _dot(tri, la_lo)
        b_last = b[C - 1:C, :]
        q = q_ref[rows, :].astype(F32)
        k = k_ref[rows, :].astype(F32)
        v = v_ref[rows, :]
        qd = q * jnp.exp(b)
        kd = (k * jnp.exp(-b)).astype(BF16)
        ke = (k * jnp.exp(b_last - b)).astype(BF16)
        state = s_ref[...]
        o_inter = _dot(qd.astype(BF16), state.astype(BF16))
        outs = []
        for h in range(GLA_HEADS):
            a = _dot_nt((qd * head_masks[h]).astype(BF16), kd)
            a = jnp.where(causal, a, 0.0).astype(BF16)
            cols = slice(h * GLA_DV, (h + 1) * GLA_DV)
            o = _dot(a, v[:, cols]) + o_inter[:, cols]
            o = o * lax.rsqrt(jnp.mean(o * o, axis=-1, keepdims=True) + EPS) * g
            outs.append(o)
        o_all = jnp.concatenate(outs, axis=-1)
        rr = r_ref[rows, :].astype(F32)
        o_ref[rows, :] = (o_all * (rr * jax.nn.sigmoid(rr))).astype(BF16)
        tot = _dot_tn(la_hi, ones_cl) + _dot_tn(la_lo, ones_cl)
        dec = jnp.exp(tot)
        upd = _dot_tn(ke, v) * state_mask
        for h in range(GLA_HEADS):
            cols = slice(h * GLA_DV, (h + 1) * GLA_DV)
            s_ref[:, cols] = state[:, cols] * dec + upd[:, cols]
        return carry

    lax.fori_loop(0, seq_block // C, chunk, 0, unroll=GLA_UNROLL)


def _gla(q, k, v, r, la, g, batch, seq, seq_block=1024):
    nsb = seq // seq_block
    row = lambda w: pl.BlockSpec((seq_block, w), lambda b, s: (b * nsb + s, 0))
    return pl.pallas_call(
        functools.partial(_gla_kernel, seq_block=seq_block),
        grid=(batch, nsb),
        in_specs=[row(256), row(256), row(512), row(512), row(256),
                  pl.BlockSpec((1, GLA_DV), lambda b, s: (0, 0))],
        out_specs=row(512),
        out_shape=jax.ShapeDtypeStruct((batch * seq, GLA_WIDTH), BF16),
        scratch_shapes=[pltpu.VMEM((GLA_KEY_WIDTH, GLA_WIDTH), F32)],
        compiler_params=pltpu.CompilerParams(
            dimension_semantics=("parallel", "arbitrary"), vmem_limit_bytes=VMEM_LIMIT),
        name="gla",
    )(q, k, v, r, la, g)


def _dil_kernel(slope_ref, q_ref, k_ref, v_ref, g_ref, o_ref,
                qf, kf, vf, kd, va, vb, oc, lc, *, seq):
    B = DIL_BLOCK
    U = DIL_UNROLL
    pair = pl.program_id(1)
    qf[...] = q_ref[...].astype(F32)
    kf[...] = k_ref[...].astype(F32)
    vf[...] = v_ref[...].astype(F32)

    lane = lax.broadcasted_iota(jnp.int32, (1, LANES), 1)
    first = lane < DIL_DH
    ii = lax.broadcasted_iota(jnp.int32, (B, B), 0)
    jj = lax.broadcasted_iota(jnp.int32, (B, B), 1)
    upper = jj > ii
    eye = jj == ii
    dist = jnp.bitwise_and(ii - jj, B - 1).astype(F32)
    neg = jnp.float32(-jnp.inf)
    neg_tile = jnp.full((B, B), neg, F32)
    zero_tile = jnp.zeros((B, LANES), BF16)

    for cfg, (window, r) in enumerate(DIL_CONFIGS):
        nb = seq // r // B
        cs = nb + 1
        bias_prev, bias_cur = [], []
        for hh in range(2):
            slope = slope_ref[2 * pair + hh] * (float(r) * LOG2E)
            bias = dist * (-slope)
            bias_prev.append(jnp.where(upper, bias, jnp.where(eye, -slope * float(B), neg)))
            bias_cur.append(jnp.where(upper, neg, bias))

        for c in range(r):
            rows0 = slice(c * cs * B, (c * cs + 1) * B)
            kd[rows0, :] = zero_tile
            va[rows0, :] = zero_tile
            vb[rows0, :] = zero_tile

        def prep(t4, carry, r=r, nb=nb, cs=cs):
            for j in range(4):
                t = t4 * 4 + j
                c = t // nb
                n = t % nb
                start = c + n * (B * r)
                rows = pl.ds(start, B, stride=r) if r > 1 else pl.ds(pl.multiple_of(start, B), B)
                dst = pl.ds(pl.multiple_of((c * cs + 1 + n) * B, B), B)
                kd[dst, :] = kf[rows, :].astype(BF16)
                v = vf[rows, :]
                va[dst, :] = jnp.where(first, v, 1.0).astype(BF16)
                vb[dst, :] = jnp.where(first, 1.0, v).astype(BF16)
            return carry

        lax.fori_loop(0, seq // B // 4, prep, 0)

        def block(it, has_prev, cfg=cfg, r=r, nb=nb, cs=cs, bias_prev=bias_prev, bias_cur=bias_cur):
            c = it // nb
            n = it % nb
            start = c + n * (B * r)
            rows = pl.ds(start, B, stride=r) if r > 1 else pl.ds(pl.multiple_of(start, B), B)
            kv = pl.ds(pl.multiple_of((c * cs + n) * B, B), 2 * B)
            q = qf[rows, :]
            kcat = kd[kv, :]
            acc, mx = [], []
            for hh in range(2):
                hmask = first if hh == 0 else jnp.logical_not(first)
                qh = jnp.where(hmask, q, 0.0).astype(BF16)
                s2 = _dot_nt(qh, kcat)
                if has_prev is True:
                    bp = bias_prev[hh]
                elif has_prev is False:
                    bp = neg_tile
                else:
                    bp = jnp.where(has_prev, bias_prev[hh], neg)
                s_prev = s2[:, 0:B] + bp
                s_cur = s2[:, B:2 * B] + bias_cur[hh]
                m = jnp.max(jnp.maximum(s_prev, s_cur), axis=-1, keepdims=True)
                pcat = jnp.concatenate([jnp.exp2(s_prev - m), jnp.exp2(s_cur - m)], axis=1).astype(BF16)
                vals = va[kv, :] if hh == 0 else vb[kv, :]
                acc.append(_dot(pcat, vals))
                mx.append(m)
            num = jnp.where(first, acc[0], acc[1])
            den = pltpu.roll(jnp.where(first, acc[1], acc[0]), DIL_DH, axis=1)
            oc[cfg, rows, :] = num * (1.0 / den)
            lc[cfg, rows, :] = (jnp.where(first, mx[0], mx[1]) + jnp.log2(den)) * LN2

        def body(t, carry, block=block, nb=nb):
            for u in range(U):
                if nb % U == 0:
                    has_prev = True if u > 0 else (t * U) % nb > 0
                else:
                    assert U % nb == 0
                    has_prev = (u % nb) > 0
                block(t * U + u, has_prev)
            return carry

        lax.fori_loop(0, seq // B // U, body, 0)

    g = g_ref[...]
    CH = 512

    def mix(i, carry):
        rows = pl.ds(pl.multiple_of(i * CH, CH), CH)
        l0, l1, l2 = lc[0, rows, :], lc[1, rows, :], lc[2, rows, :]
        m = jnp.maximum(jnp.maximum(l0, l1), l2)
        e0, e1, e2 = jnp.exp(l0 - m), jnp.exp(l1 - m), jnp.exp(l2 - m)
        den = e0 + e1 + e2
        o = (e0 / den) * oc[0, rows, :] + (e1 / den) * oc[1, rows, :] + (e2 / den) * oc[2, rows, :]
        sq = o * o
        ms_a = jnp.sum(jnp.where(first, sq, 0.0), axis=-1, keepdims=True) * (1.0 / DIL_DH)
        ms_b = jnp.sum(jnp.where(first, 0.0, sq), axis=-1, keepdims=True) * (1.0 / DIL_DH)
        ms = jnp.where(first, ms_a, ms_b)
        o_ref[rows, :] = (o * lax.rsqrt(ms + EPS) * g).astype(BF16)
        return carry

    lax.fori_loop(0, seq // CH, mix, 0)


def _dilated(slopes, dq, dk, dv, g2, batch, seq):
    blk = pl.BlockSpec((seq, LANES), lambda b, p, s: (b, p))
    return pl.pallas_call(
        functools.partial(_dil_kernel, seq=seq),
        grid_spec=pltpu.PrefetchScalarGridSpec(
            num_scalar_prefetch=1,
            grid=(batch, DIL_WIDTH // LANES),
            in_specs=[blk, blk, blk, pl.BlockSpec((1, LANES), lambda b, p, s: (0, 0))],
            out_specs=blk,
            scratch_shapes=[pltpu.VMEM((seq, LANES), F32)] * 3
                           + [pltpu.VMEM((seq + DIL_PAD, LANES), BF16)] * 3
                           + [pltpu.VMEM((3, seq, LANES), F32)] * 2,
        ),
        out_shape=jax.ShapeDtypeStruct((batch * seq, DIL_WIDTH), BF16),
        compiler_params=pltpu.CompilerParams(
            dimension_semantics=("parallel", "parallel"), vmem_limit_bytes=VMEM_LIMIT),
        name="dilated",
    )(slopes, dq, dk, dv, g2)


def _out_proj_kernel(og_ref, od_ref, x_ref, wg_ref, wd_ref, g_ref, b_ref,
                     rwh_ref, rwl_ref, rb_ref, h_ref, hp_ref, eid_ref, gate_ref, cnt_ref):
    mix = _dot(og_ref[...], wg_ref[...]) + _dot(od_ref[...], wd_ref[...])
    h = _layer_norm(DEEPNORM_ALPHA * x_ref[...] + mix, g_ref[...], b_ref[...])
    h_ref[...] = h
    hp_ref[...] = _pack_bf16_halves(h)
    h_hi, h_lo = _split_bf16(h)
    lt = (_dot_nt(rwh_ref[...], h_hi) + _dot_nt(rwh_ref[...], h_lo) + _dot_nt(rwl_ref[...], h_hi)
          + rb_ref[...])
    tm = lt.shape[1]
    row = lax.broadcasted_iota(jnp.int32, (EXPERTS_PER_GROUP, tm), 0).astype(F32)
    neg = jnp.float32(-jnp.inf)
    big = jnp.float32(1e9)
    coarse = jnp.where(row < N_GROUPS, lt[N_EXPERTS:N_EXPERTS + EXPERTS_PER_GROUP, :], neg)
    cmax = jnp.max(coarse, axis=0, keepdims=True)
    g_idx = jnp.min(jnp.where(coarse == cmax, row, big), axis=0, keepdims=True)
    p_group = 1.0 / jnp.sum(jnp.exp(coarse - cmax), axis=0, keepdims=True)
    fine = lt[(N_GROUPS - 1) * EXPERTS_PER_GROUP:N_EXPERTS, :]
    for g in range(N_GROUPS - 2, -1, -1):
        fine = jnp.where(g_idx == g, lt[g * EXPERTS_PER_GROUP:(g + 1) * EXPERTS_PER_GROUP, :], fine)
    v1 = jnp.max(fine, axis=0, keepdims=True)
    i1 = jnp.min(jnp.where(fine == v1, row, big), axis=0, keepdims=True)
    fine2 = jnp.where(row == i1, neg, fine)
    v2 = jnp.max(fine2, axis=0, keepdims=True)
    i2 = jnp.min(jnp.where(fine2 == v2, row, big), axis=0, keepdims=True)
    e2 = jnp.exp(v2 - v1)
    den = 1.0 + e2
    gate1 = p_group * (1.0 / den)
    gate2 = p_group * (e2 / den)
    id1 = g_idx * EXPERTS_PER_GROUP + i1
    id2 = g_idx * EXPERTS_PER_GROUP + i2
    eid_ref[...] = jnp.concatenate([id1, id2], axis=0).astype(jnp.int32)
    slab = jnp.concatenate([gate1, gate2, jnp.zeros((LANES - 2, tm), F32)], axis=0)
    gate_ref[...] = slab.T[:, 0:2]
    sub = lax.broadcasted_iota(jnp.int32, (LANES, tm), 0).astype(F32)
    onehot = jnp.logical_or(sub == id1, sub == id2).astype(BF16)

    @pl.when(pl.program_id(0) == 0)
    def _():
        cnt_ref[...] = jnp.zeros_like(cnt_ref)

    cnt_ref[...] += _dot(onehot, jnp.ones((tm, LANES), BF16))


def _out_proj(og, od, x2, wg, wd, g, b, rwh, rwl, rb, tm=1024):
    T = x2.shape[0]
    row = lambda w: pl.BlockSpec((tm, w), lambda i: (i, 0))
    full = lambda a: pl.BlockSpec(a.shape, lambda i: (0,) * a.ndim)
    return pl.pallas_call(
        _out_proj_kernel,
        grid=(T // tm,),
        in_specs=[row(512), row(512), row(D_MODEL), full(wg), full(wd), full(g), full(b),
                  full(rwh), full(rwl), full(rb)],
        out_specs=[row(D_MODEL), row(D_MODEL // 2), pl.BlockSpec((2, tm), lambda i: (0, i)), row(2),
                   pl.BlockSpec((LANES, LANES), lambda i: (0, 0))],
        out_shape=[jax.ShapeDtypeStruct((T, D_MODEL), F32),
                   jax.ShapeDtypeStruct((T, D_MODEL // 2), jnp.int32),
                   jax.ShapeDtypeStruct((2, T), jnp.int32),
                   jax.ShapeDtypeStruct((T, 2), F32),
                   jax.ShapeDtypeStruct((LANES, LANES), F32)],
        compiler_params=pltpu.CompilerParams(
            dimension_semantics=("arbitrary",), vmem_limit_bytes=VMEM_LIMIT),
        name="out_proj_router",
    )(og, od, x2, wg, wd, g, b, rwh, rwl, rb)


def _positions_kernel(eid_ref, cnt_ref, dest_ref, be_ref, nv_ref, carry_ref, sp_ref, tri_ref, *, tb):
    i = pl.program_id(0)

    @pl.when(i == 0)
    def _():
        shift = int(math.log2(ROW_BLOCK))
        nb_col = (cnt_ref[...].astype(jnp.int32) + (ROW_BLOCK - 1)) >> shift
        r = lax.broadcasted_iota(jnp.int32, (LANES, LANES), 0)
        c = lax.broadcasted_iota(jnp.int32, (LANES, LANES), 1)
        nb_f = jnp.where(r < N_EXPERTS, nb_col, 0).astype(F32)
        start_col = _dot((c < r).astype(BF16), nb_f.astype(BF16))
        sp_ref[...] = start_col * float(ROW_BLOCK)
        carry_ref[...] = jnp.zeros_like(carry_ref)
        be_ref[...] = jnp.concatenate([start_col.T[0:1, :], nb_f.T[0:1, :]], axis=1).astype(jnp.int32)
        total = jnp.sum(nb_f[:, 0:1], axis=0, keepdims=True)
        nv_ref[...] = jnp.broadcast_to(total, (1, LANES)).astype(jnp.int32)
        tr = lax.broadcasted_iota(jnp.int32, (tb, tb), 0)
        tc = lax.broadcasted_iota(jnp.int32, (tb, tb), 1)
        tri_ref[...] = (tr < tc).astype(BF16)

    sub = lax.broadcasted_iota(jnp.int32, (LANES, tb), 0)
    oh1 = sub == eid_ref[0:1, :]
    oh2 = sub == eid_ref[1:2, :]
    oh = jnp.logical_or(oh1, oh2).astype(BF16)
    offset = jnp.tile(carry_ref[...] + sp_ref[...], (1, tb // LANES))
    before = _dot(oh, tri_ref[...]) + offset
    d1 = jnp.sum(jnp.where(oh1, before, 0.0), axis=0, keepdims=True)
    d2 = jnp.sum(jnp.where(oh2, before, 0.0), axis=0, keepdims=True)
    dest_ref[...] = jnp.concatenate([d1, d2], axis=0).astype(jnp.int32)
    carry_ref[...] += _dot(oh, jnp.ones((tb, LANES), BF16))


def _positions(eid_t, cnt, n_blocks_pad, tb=1024):
    T = eid_t.shape[1]
    return pl.pallas_call(
        functools.partial(_positions_kernel, tb=tb),
        grid=(T // tb,),
        in_specs=[pl.BlockSpec((2, tb), lambda i: (0, i)), pl.BlockSpec((LANES, LANES), lambda i: (0, 0))],
        out_specs=[pl.BlockSpec((2, tb), lambda i: (0, i)),
                   pl.BlockSpec((1, n_blocks_pad), lambda i: (0, 0)),
                   pl.BlockSpec((1, LANES), lambda i: (0, 0))],
        out_shape=[jax.ShapeDtypeStruct((2, T), jnp.int32),
                   jax.ShapeDtypeStruct((1, n_blocks_pad), jnp.int32),
                   jax.ShapeDtypeStruct((1, LANES), jnp.int32)],
        scratch_shapes=[pltpu.VMEM((LANES, LANES), F32), pltpu.VMEM((LANES, LANES), F32),
                        pltpu.VMEM((tb, tb), BF16)],
        compiler_params=pltpu.CompilerParams(dimension_semantics=("arbitrary",)),
        name="positions",
    )(eid_t, cnt)


def _sc_gather_rows(table, idx):
    n = idx.shape[0]
    d = table.shape[1]
    info = plsc.get_sparse_core_info()
    nc, ns = info.num_cores, info.num_subcores
    per_w = n // (nc * ns)
    assert per_w * nc * ns == n and per_w % SC_INDEX_WINDOW == 0
    mesh = plsc.VectorSubcoreMesh(core_axis_name="core", subcore_axis_name="subcore")
    nchunk = per_w // SC_GATHER_ROWS
    nbuf = SC_GATHER_BUFFERS

    @functools.partial(
        pl.kernel, out_type=jax.ShapeDtypeStruct((n, d), table.dtype), mesh=mesh,
        scratch_types=[pltpu.VMEM((per_w,), jnp.int32),
                       pltpu.VMEM((nbuf, SC_GATHER_ROWS, d), table.dtype),
                       pltpu.SemaphoreType.DMA((nbuf,)), pltpu.SemaphoreType.DMA((nbuf,))],
        name="sc_gather_rows")
    def gather(x_hbm, i_hbm, o_hbm, idx_v, buf, gsem, wsem):
        wid = lax.axis_index("subcore") * nc + lax.axis_index("core")
        base = wid * per_w
        pltpu.sync_copy(i_hbm.at[pl.ds(base, per_w)], idx_v)

        def gather_copy(c):
            rows = idx_v.at[pl.ds(c * SC_GATHER_ROWS, SC_GATHER_ROWS)]
            return pltpu.make_async_copy(x_hbm.at[rows], buf.at[c % nbuf], gsem.at[c % nbuf])

        def write_copy(c):
            dst = o_hbm.at[pl.ds(base + c * SC_GATHER_ROWS, SC_GATHER_ROWS)]
            return pltpu.make_async_copy(buf.at[c % nbuf], dst, wsem.at[c % nbuf])

        for c in range(min(nbuf - 1, nchunk)):
            gather_copy(c).start()
        for c in range(nchunk):
            gather_copy(c).wait()
            write_copy(c).start()
            if c + nbuf - 1 < nchunk:
                if c >= 1:
                    write_copy(c - 1).wait()
                gather_copy(c + nbuf - 1).start()
        for c in range(max(0, nchunk - nbuf), nchunk):
            write_copy(c).wait()

    return gather(table, idx)


def _sc_inverse_rows(dest_flat, n_rows, chunk=2048):
    n = dest_flat.shape[0]
    n_tokens = n // 2
    assert n_rows <= 3 * n_tokens
    nc = plsc.get_sparse_core_info().num_cores
    mesh = plsc.VectorSubcoreMesh(core_axis_name="core", subcore_axis_name="subcore")

    @functools.partial(
        pl.kernel, out_type=jax.ShapeDtypeStruct((n_rows,), jnp.int32), mesh=mesh,
        scratch_types=[pltpu.VMEM((n_rows,), jnp.int32), pltpu.VMEM((chunk,), jnp.int32)],
        compiler_params=pltpu.CompilerParams(needs_layout_passes=False),
        name="sc_inverse_rows")
    def inverse(d_hbm, o_hbm, inv_v, d_v):
        wid = lax.axis_index("subcore") * nc + lax.axis_index("core")

        @pl.when(wid == 0)
        def _():
            lanes = lax.iota(jnp.int32, SC_LANES)

            @pl.loop(0, n_rows // SC_LANES)
            def _(i):
                r = lanes + i * SC_LANES
                r = jnp.where(r >= n_tokens, r - n_tokens, r)
                inv_v[pl.ds(i * SC_LANES, SC_LANES)] = jnp.where(r >= n_tokens, r - n_tokens, r)

            @pl.loop(0, n // chunk)
            def _(c):
                pltpu.sync_copy(d_hbm.at[pl.ds(c * chunk, chunk)], d_v)

                @pl.loop(0, chunk // SC_LANES)
                def _(j):
                    rows = d_v[pl.ds(j * SC_LANES, SC_LANES)]
                    pair = lax.iota(jnp.int32, SC_LANES) + (c * chunk + j * SC_LANES)
                    plsc.store_scatter(inv_v, [rows], jnp.where(pair >= n_tokens, pair - n_tokens, pair))

            pltpu.sync_copy(inv_v, o_hbm)

    return inverse(dest_flat)


def _ffn_kernel(first_ref, count_ref, nv_ref, wg_hbm, wu_hbm, wd_hbm, xs_hbm, y_hbm,
                wg32, wu32, wd32, wgb, wub, wdb, xbuf, ybuf, wsem, isem, osem, *, n_blocks):
    nv = nv_ref[0]
    nbuf = FFN_BUFFERS

    def next_expert(e):
        def more(t):
            return jnp.logical_and(t < N_EXPERTS, count_ref[jnp.minimum(t, N_EXPERTS - 1)] == 0)
        return lax.while_loop(more, lambda t: t + 1, e + 1)

    def weight_copies(e, slot):
        ee = jnp.minimum(e, N_EXPERTS - 1)
        copies = []
        for i, (src, dst) in enumerate(((wg_hbm, wg32), (wu_hbm, wu32), (wd_hbm, wd32))):
            rows = src.shape[1] // FFN_WEIGHT_CHUNKS
            for c in range(FFN_WEIGHT_CHUNKS):
                part = pl.ds(c * rows, rows)
                copies.append(pltpu.make_async_copy(src.at[ee, part], dst.at[slot, part],
                                                    wsem.at[slot, i * FFN_WEIGHT_CHUNKS + c]))
        return copies

    def fetch_weights(e, slot):
        @pl.when(e < N_EXPERTS)
        def _():
            for c in weight_copies(e, slot):
                c.start()

    def take_weights(e, slot):
        for c in weight_copies(e, slot):
            c.wait()
        wgb[...] = wg32[slot].astype(BF16)
        wub[...] = wu32[slot].astype(BF16)
        wdb[...] = wd32[slot].astype(BF16)
        fetch_weights(next_expert(next_expert(e)), slot)

    def rows_of(b):
        return pl.ds(pl.multiple_of(b * ROW_BLOCK, ROW_BLOCK), ROW_BLOCK)

    def in_copy(b):
        return pltpu.make_async_copy(xs_hbm.at[rows_of(b)], xbuf.at[b % nbuf], isem.at[b % nbuf])

    def out_copy(b):
        return pltpu.make_async_copy(ybuf.at[b % nbuf], y_hbm.at[rows_of(b)], osem.at[b % nbuf])

    @pl.when(nv > 0)
    def _():
        e0 = next_expert(jnp.int32(-1))
        for i in range(nbuf - 1):
            @pl.when(i < nv)
            def _():
                in_copy(i).start()
        fetch_weights(e0, 0)
        fetch_weights(next_expert(e0), 1)
        take_weights(e0, 0)

        def body(b, carry):
            e, k = carry
            switch = b >= first_ref[e] + count_ref[e]
            e_new = jnp.where(switch, next_expert(e), e)
            k_new = jnp.where(switch, k + 1, k)

            @pl.when(switch)
            def _():
                take_weights(e_new, k_new % 2)

            in_copy(b).wait()

            @pl.when(b + nbuf - 1 < nv)
            def _():
                in_copy(b + nbuf - 1).start()

            @pl.when(b >= nbuf)
            def _():
                out_copy(b - nbuf).wait()

            x_hi, x_lo = _unpack_bf16_halves(xbuf[b % nbuf])
            xb = jnp.concatenate([x_hi.astype(BF16), x_lo.astype(BF16)], axis=1)
            a = _dot(xb, wgb[...])
            u = _dot(xb, wub[...])
            hid = (a * jax.nn.sigmoid(a) * u).astype(BF16)
            ybuf[b % nbuf] = _pack_bf16_halves(_dot(hid, wdb[...]))
            out_copy(b).start()
            return e_new, k_new

        lax.fori_loop(0, nv, body, (e0, jnp.int32(0)))

        for i in range(nbuf):
            @pl.when(nv > i)
            def _():
                out_copy(nv - 1 - i).wait()

    ybuf[0] = jnp.zeros((ROW_BLOCK, D_MODEL // 2), jnp.int32)

    def fill(b, carry):
        pltpu.sync_copy(ybuf.at[0], y_hbm.at[rows_of(b)])
        return carry

    lax.fori_loop(nv, n_blocks, fill, 0)


def _ffn(first_blk, n_blk, nv, xs, w_gate, w_up, w_down):
    n_rows = xs.shape[0]
    n_blocks = n_rows // ROW_BLOCK
    anyspec = pl.BlockSpec(memory_space=pl.ANY)
    return pl.pallas_call(
        functools.partial(_ffn_kernel, n_blocks=n_blocks),
        grid_spec=pltpu.PrefetchScalarGridSpec(
            num_scalar_prefetch=3,
            grid=(1,),
            in_specs=[anyspec, anyspec, anyspec, anyspec],
            out_specs=anyspec,
            scratch_shapes=[pltpu.VMEM((2, D_MODEL, D_FF), F32), pltpu.VMEM((2, D_MODEL, D_FF), F32),
                            pltpu.VMEM((2, D_FF, D_MODEL), F32),
                            pltpu.VMEM((D_MODEL, D_FF), BF16), pltpu.VMEM((D_MODEL, D_FF), BF16),
                            pltpu.VMEM((D_FF, D_MODEL), BF16),
                            pltpu.VMEM((FFN_BUFFERS, ROW_BLOCK, D_MODEL // 2), jnp.int32),
                            pltpu.VMEM((FFN_BUFFERS, ROW_BLOCK, D_MODEL // 2), jnp.int32),
                            pltpu.SemaphoreType.DMA((2, 3 * FFN_WEIGHT_CHUNKS)),
                            pltpu.SemaphoreType.DMA((FFN_BUFFERS,)),
                            pltpu.SemaphoreType.DMA((FFN_BUFFERS,))],
        ),
        out_shape=jax.ShapeDtypeStruct((n_rows, D_MODEL // 2), jnp.int32),
        compiler_params=pltpu.CompilerParams(
            dimension_semantics=("arbitrary",), vmem_limit_bytes=VMEM_LIMIT),
        name="expert_ffn",
    )(first_blk, n_blk, nv, w_gate, w_up, w_down, xs)


def _combine_kernel(h_ref, ya_ref, yb_ref, gate_ref, g_ref, b_ref, o_ref):
    gate = gate_ref[...]
    a_hi, a_lo = _unpack_bf16_halves(ya_ref[...])
    b_hi, b_lo = _unpack_bf16_halves(yb_ref[...])
    g0, g1 = gate[:, 0:1], gate[:, 1:2]
    ffn = jnp.concatenate([a_hi * g0 + b_hi * g1, a_lo * g0 + b_lo * g1], axis=1)
    o_ref[...] = _layer_norm(DEEPNORM_ALPHA * h_ref[...] + ffn, g_ref[...], b_ref[...])


def _combine(h, y2, gate, g, b, tm=512):
    T = h.shape[0]
    nt = T // tm
    return pl.pallas_call(
        _combine_kernel,
        grid=(nt,),
        in_specs=[pl.BlockSpec((tm, D_MODEL), lambda i: (i, 0)),
                  pl.BlockSpec((tm, D_MODEL // 2), lambda i: (i, 0)),
                  pl.BlockSpec((tm, D_MODEL // 2), lambda i: (i + nt, 0)),
                  pl.BlockSpec((tm, 2), lambda i: (i, 0)),
                  pl.BlockSpec((1, D_MODEL), lambda i: (0, 0)),
                  pl.BlockSpec((1, D_MODEL), lambda i: (0, 0))],
        out_specs=pl.BlockSpec((tm, D_MODEL), lambda i: (i, 0)),
        out_shape=jax.ShapeDtypeStruct((T, D_MODEL), F32),
        compiler_params=pltpu.CompilerParams(
            dimension_semantics=("parallel",), vmem_limit_bytes=VMEM_LIMIT),
        name="combine",
    )(h, y2, y2, gate, g, b)


def kernel(x, w_in, gla_gate_w2, gla_gate_b, gla_norm_g, dil_norm_g, w_out, ln1_g, ln1_b,
           router_coarse_w, router_coarse_b, router_fine_w, router_fine_b,
           expert_w_gate, expert_w_up, expert_w_down, ln2_g, ln2_b):
    B, S, D = x.shape
    T = B * S
    depth = w_in.shape[0]
    slopes = jnp.exp2(-8.0 * jnp.arange(1, DIL_HEADS + 1, dtype=F32) / DIL_HEADS)
    n_rows = 2 * T + N_EXPERTS * ROW_BLOCK
    n_blocks = n_rows // ROW_BLOCK
    n_blocks_pad = -(-n_blocks // (2 * LANES)) * (2 * LANES)
    assert n_blocks_pad == 2 * LANES
    a0 = 1536
    h = x.reshape(T, D)
    for l in range(depth):
        w = w_in[l]
        wm = jnp.concatenate([w[:, :a0], w[:, a0 + GLA_GATE_RANK:]], axis=1).astype(BF16)
        wa = jnp.pad(w[:, a0:a0 + GLA_GATE_RANK], ((0, 0), (0, LANES - GLA_GATE_RANK))).astype(BF16)
        w2 = jnp.pad(gla_gate_w2[l], ((0, LANES - GLA_GATE_RANK), (0, 0)))
        w2h, w2l = _split_bf16(w2)
        q, k, v, r, la, dq, dk, dv = _in_proj(h, wm, wa, w2h, w2l, gla_gate_b[l][None, :])
        o_gla = _gla(q, k, v, r, la, gla_norm_g[l][None, :], B, S)
        g2 = jnp.tile(dil_norm_g[l], 2)[None, :]
        o_dil = _dilated(slopes, dq, dk, dv, g2, B, S)
        wo = w_out[l].astype(BF16)
        rw = jnp.concatenate([router_fine_w[l].reshape(D, N_EXPERTS), router_coarse_w[l]], axis=1)
        rw = jnp.pad(rw, ((0, 0), (0, LANES - N_EXPERTS - N_GROUPS))).T
        rwh, rwl = _split_bf16(rw)
        rb = jnp.concatenate([router_fine_b[l].reshape(N_EXPERTS), router_coarse_b[l]])
        rb = jnp.pad(rb, (0, LANES - N_EXPERTS - N_GROUPS))[:, None]
        h1, h1p, eid_t, gate, cnt = _out_proj(o_gla, o_dil, h, wo[:GLA_WIDTH], wo[GLA_WIDTH:],
                                              ln1_g[l][None, :], ln1_b[l][None, :], rwh, rwl, rb)
        dest_t, be, nv = _positions(eid_t, cnt, n_blocks_pad)
        dest_flat = dest_t.reshape(2 * T)
        src_tok = _sc_inverse_rows(dest_flat, n_rows)
        xs = _sc_gather_rows(h1p, src_tok)
        be = be.reshape(n_blocks_pad)
        y = _ffn(be[:N_EXPERTS], be[LANES:LANES + N_EXPERTS], nv.reshape(LANES)[:1], xs,
                 expert_w_gate[l], expert_w_up[l], expert_w_down[l])
        y2 = _sc_gather_rows(y, dest_flat)
        h = _combine(h1, y2, gate, ln2_g[l][None, :], ln2_b[l][None, :])
    return h.reshape(B, S, D)
```

```python
import functools
import math

import jax
import jax.numpy as jnp
import numpy as np
from jax import lax
from jax.experimental import pallas as pl
from jax.experimental.pallas import tpu as pltpu
from jax.experimental.pallas import tpu_sc as plsc

D_MODEL = 1024
GLA_HEADS = 4
GLA_DK = 64
GLA_DV = 128
GLA_KEY_WIDTH = GLA_HEADS * GLA_DK
GLA_WIDTH = GLA_HEADS * GLA_DV
GLA_GATE_RANK = 16
GLA_GATE_TEMP = 16.0
DIL_HEADS = 8
DIL_DH = 64
DIL_WIDTH = DIL_HEADS * DIL_DH
DIL_CONFIGS = ((128, 1), (512, 4), (2048, 16))
DIL_BLOCK = 128
DIL_MAX_R = max(r for _, r in DIL_CONFIGS)
DIL_PAD = DIL_BLOCK * DIL_MAX_R
DIL_UNROLL = 8
N_GROUPS = 4
EXPERTS_PER_GROUP = 8
N_EXPERTS = N_GROUPS * EXPERTS_PER_GROUP
D_FF = 512
DEEPNORM_ALPHA = 2.0 ** 0.25
EPS = 1e-5
LOG2E = math.log2(math.e)
LN2 = math.log(2.0)

LANES = 128
GLA_CHUNK = 128
GLA_SUB = 64
GLA_UNROLL = 4
SC_LANES = 16
SC_INDEX_WINDOW = 128
SC_GATHER_BUFFERS = 6
SC_GATHER_ROWS = 32
FFN_BUFFERS = 8
FFN_WEIGHT_CHUNKS = 4
ROW_BLOCK = 256
VMEM_LIMIT = 56 * 1024 * 1024

F32 = jnp.float32
BF16 = jnp.bfloat16


def _dot(a, b):
    return jnp.dot(a, b, preferred_element_type=F32)


def _dot_nt(a, b):
    return lax.dot_general(a, b, (((1,), (1,)), ((), ())), preferred_element_type=F32)


def _dot_tn(a, b):
    return lax.dot_general(a, b, (((0,), (0,)), ((), ())), preferred_element_type=F32)


def _split_bf16(v):
    hi = v.astype(BF16)
    lo = (v - hi.astype(F32)).astype(BF16)
    return hi, lo


def _pack_bf16_halves(v):
    w = v.shape[1] // 2
    hi = lax.bitcast_convert_type(v[:, :w].astype(BF16).astype(F32), jnp.int32)
    lo = lax.bitcast_convert_type(v[:, w:].astype(BF16).astype(F32), jnp.int32)
    return hi | lax.shift_right_logical(lo, 16)


def _unpack_bf16_halves(words):
    hi = lax.bitcast_convert_type(words & jnp.int32(-65536), F32)
    lo = lax.bitcast_convert_type(lax.shift_left(words, 16), F32)
    return hi, lo


def _layer_norm(v, g, b):
    mu = jnp.mean(v, axis=-1, keepdims=True)
    c = v - mu
    var = jnp.mean(c * c, axis=-1, keepdims=True)
    return c * lax.rsqrt(var + EPS) * g + b


def _in_proj_kernel(x_ref, wm_ref, wa_ref, w2h_ref, w2l_ref, gb_ref,
                    q_ref, k_ref, v_ref, r_ref, la_ref, dq_ref, dk_ref, dv_ref):
    xb = x_ref[...].astype(BF16)

    def piece(c0, c1):
        return _dot(xb, wm_ref[:, c0:c1])

    q_ref[...] = (piece(0, 256) * (GLA_DK ** -0.5)).astype(BF16)
    k_ref[...] = piece(256, 512).astype(BF16)
    v_ref[...] = piece(512, 1024).astype(BF16)
    r_ref[...] = piece(1024, 1536).astype(BF16)
    dq_ref[...] = (piece(1536, 2048) * (DIL_DH ** -0.5 * LOG2E)).astype(BF16)
    dk_ref[...] = piece(2048, 2560).astype(BF16)
    dv_ref[...] = piece(2560, 3072).astype(BF16)
    ga = _dot(xb, wa_ref[...])
    ga_hi, ga_lo = _split_bf16(ga)
    z = _dot(ga_hi, w2h_ref[...]) + _dot(ga_lo, w2h_ref[...]) + _dot(ga_hi, w2l_ref[...]) + gb_ref[...]
    log_sig = jnp.minimum(z, 0.0) - jnp.log1p(jnp.exp(-jnp.abs(z)))
    la_ref[...] = log_sig * (1.0 / GLA_GATE_TEMP)


def _in_proj(x2, wm, wa, w2h, w2l, gb, tm=1024):
    T = x2.shape[0]
    row = lambda w: pl.BlockSpec((tm, w), lambda i: (i, 0))
    full = lambda a: pl.BlockSpec(a.shape, lambda i: (0,) * a.ndim)
    outs = [(256, BF16), (256, BF16), (512, BF16), (512, BF16), (256, F32),
            (512, BF16), (512, BF16), (512, BF16)]
    return pl.pallas_call(
        _in_proj_kernel,
        grid=(T // tm,),
        in_specs=[row(D_MODEL), full(wm), full(wa), full(w2h), full(w2l), full(gb)],
        out_specs=[row(w) for w, _ in outs],
        out_shape=[jax.ShapeDtypeStruct((T, w), dt) for w, dt in outs],
        compiler_params=pltpu.CompilerParams(
            dimension_semantics=("parallel",), vmem_limit_bytes=VMEM_LIMIT),
        name="in_proj",
    )(x2, wm, wa, w2h, w2l, gb)


def _gla_kernel(q_ref, k_ref, v_ref, r_ref, la_ref, g_ref, o_ref, s_ref, *, seq_block):
    C = GLA_CHUNK
    H = GLA_SUB
    assert C == 2 * H

    @pl.when(pl.program_id(1) == 0)
    def _():
        s_ref[...] = jnp.zeros_like(s_ref)

    ri = lax.broadcasted_iota(jnp.int32, (C, C), 0)
    ci = lax.broadcasted_iota(jnp.int32, (C, C), 1)
    same_sub = (ri // H) == (ci // H)
    sum_ops = jnp.concatenate([jnp.logical_and(same_sub, ci <= ri).astype(BF16), same_sub.astype(BF16),
                               jnp.ones((C, C), BF16)], axis=0)
    diag_mask = jnp.logical_and(same_sub, ci <= ri)
    off_mask = (ri // H) > (ci // H)
    second = lax.broadcasted_iota(jnp.int32, (C, 1), 0) >= H
    ones_cl = jnp.ones((C, LANES), BF16)
    lane_k = lax.broadcasted_iota(jnp.int32, (1, GLA_KEY_WIDTH), 1) // GLA_DK
    head_masks = [(lane_k == h).astype(F32) for h in range(GLA_HEADS)]
    srow = lax.broadcasted_iota(jnp.int32, (GLA_KEY_WIDTH, GLA_WIDTH), 0) // GLA_DK
    scol = lax.broadcasted_iota(jnp.int32, (GLA_KEY_WIDTH, GLA_WIDTH), 1) // GLA_DV
    state_mask = (srow == scol).astype(F32)
    g = g_ref[...]

    def chunk(c, carry):
        rows = pl.ds(pl.multiple_of(c * C, C), C)
        la = la_ref[rows, :]
        la_hi, la_lo = _split_bf16(la)
        la2 = jnp.concatenate([la_hi, la_lo], axis=1)
        sums = _dot(sum_ops, la2)
        sums = sums[:, 0:GLA_KEY_WIDTH] + sums[:, GLA_KEY_WIDTH:]
        b = sums[0:C]
        t = sums[C:2 * C]
        other = sums[2 * C:3 * C] - t
        q = q_ref[rows, :].astype(F32)
        k = k_ref[rows, :].astype(F32)
        v = v_ref[rows, :]
        qd = q * jnp.exp(b)
        kd = (k * jnp.exp(-b)).astype(BF16)
        ke_f = k * jnp.exp(t - b)
        cross = jnp.exp(other)
        q_state = (qd * jnp.where(second, cross, 1.0)).astype(BF16)
        k_state = (ke_f * jnp.where(second, 1.0, cross)).astype(BF16)
        state = s_ref[...]
        o_inter = _dot(q_state, state.astype(BF16))
        q_heads = jnp.concatenate([(qd * head_masks[h]).astype(BF16) for h in range(GLA_HEADS)], axis=0)
        keys2 = jnp.concatenate([kd, ke_f.astype(BF16)], axis=0)
        scores = _dot_nt(q_heads, keys2)
        outs = []
        for h in range(GLA_HEADS):
            sh = scores[h * C:(h + 1) * C]
            a = jnp.where(diag_mask, sh[:, 0:C], 0.0) + jnp.where(off_mask, sh[:, C:2 * C], 0.0)
            cols = slice(h * GLA_DV, (h + 1) * GLA_DV)
            o = _dot(a.astype(BF16), v[:, cols]) + o_inter[:, cols]
            o = o * lax.rsqrt(jnp.mean(o * o, axis=-1, keepdims=True) + EPS) * g
            outs.append(o)
        o_all = jnp.concatenate(outs, axis=-1)
        rr = r_ref[rows, :].astype(F32)
        o_ref[rows, :] = (o_all * (rr * jax.nn.sigmoid(rr))).astype(BF16)
        tot = _dot_tn(la2, ones_cl)
        dec = jnp.exp(tot[0:GLA_KEY_WIDTH] + tot[GLA_KEY_WIDTH:])
        upd = _dot_tn(k_state, v) * state_mask
        for h in range(GLA_HEADS):
            cols = slice(h * GLA_DV, (h + 1) * GLA_DV)
            s_ref[:, cols] = state[:, cols] * dec + upd[:, cols]
        return carry

    lax.fori_loop(0, seq_block // C, chunk, 0, unroll=GLA_UNROLL)


def _gla(q, k, v, r, la, g, batch, seq, seq_block=1024):
    nsb = seq // seq_block
    row = lambda w: pl.BlockSpec((seq_block, w), lambda b, s: (b * nsb + s, 0))
    return pl.pallas_call(
        functools.partial(_gla_kernel, seq_block=seq_block),
        grid=(batch, nsb),
        in_specs=[row(256), row(256), row(512), row(512), row(256),
                  pl.BlockSpec((1, GLA_DV), lambda b, s: (0, 0))],
        out_specs=row(512),
        out_shape=jax.ShapeDtypeStruct((batch * seq, GLA_WIDTH), BF16),
        scratch_shapes=[pltpu.VMEM((GLA_KEY_WIDTH, GLA_WIDTH), F32)],
        compiler_params=pltpu.CompilerParams(
            dimension_semantics=("parallel", "arbitrary"), vmem_limit_bytes=VMEM_LIMIT),
        name="gla",
    )(q, k, v, r, la, g)


def _dil_kernel(slope_ref, q_ref, k_ref, v_ref, g_ref, o_ref,
                qf, kf, vf, kd, va, vb, oc, lc, *, seq):
    B = DIL_BLOCK
    U = DIL_UNROLL
    pair = pl.program_id(1)
    qf[...] = q_ref[...].astype(F32)
    kf[...] = k_ref[...].astype(F32)
    vf[...] = v_ref[...].astype(F32)

    lane = lax.broadcasted_iota(jnp.int32, (1, LANES), 1)
    first = lane < DIL_DH
    ii = lax.broadcasted_iota(jnp.int32, (B, B), 0)
    jj = lax.broadcasted_iota(jnp.int32, (B, B), 1)
    upper = jj > ii
    eye = jj == ii
    dist = jnp.bitwise_and(ii - jj, B - 1).astype(F32)
    neg = jnp.float32(-jnp.inf)
    neg_tile = jnp.full((B, B), neg, F32)
    zero_tile = jnp.zeros((B, LANES), BF16)

    for cfg, (window, r) in enumerate(DIL_CONFIGS):
        nb = seq // r // B
        cs = nb + 1
        bias_prev, bias_cur = [], []
        for hh in range(2):
            slope = slope_ref[2 * pair + hh] * (float(r) * LOG2E)
            bias = dist * (-slope)
            bias_prev.append(jnp.where(upper, bias, jnp.where(eye, -slope * float(B), neg)))
            bias_cur.append(jnp.where(upper, neg, bias))

        for c in range(r):
            rows0 = slice(c * cs * B, (c * cs + 1) * B)
            kd[rows0, :] = zero_tile
            va[rows0, :] = zero_tile
            vb[rows0, :] = zero_tile

        def prep(t4, carry, r=r, nb=nb, cs=cs):
            for j in range(4):
                t = t4 * 4 + j
                c = t // nb
                n = t % nb
                start = c + n * (B * r)
                rows = pl.ds(start, B, stride=r) if r > 1 else pl.ds(pl.multiple_of(start, B), B)
                dst = pl.ds(pl.multiple_of((c * cs + 1 + n) * B, B), B)
                kd[dst, :] = kf[rows, :].astype(BF16)
                v = vf[rows, :]
                va[dst, :] = jnp.where(first, v, 1.0).astype(BF16)
                vb[dst, :] = jnp.where(first, 1.0, v).astype(BF16)
            return carry

        lax.fori_loop(0, seq // B // 4, prep, 0)

        def block(it, has_prev, cfg=cfg, r=r, nb=nb, cs=cs, bias_prev=bias_prev, bias_cur=bias_cur):
            c = it // nb
            n = it % nb
            start = c + n * (B * r)
            rows = pl.ds(start, B, stride=r) if r > 1 else pl.ds(pl.multiple_of(start, B), B)
            kv = pl.ds(pl.multiple_of((c * cs + n) * B, B), 2 * B)
            q = qf[rows, :]
            kcat = kd[kv, :]
            acc, mx = [], []
            for hh in range(2):
                hmask = first if hh == 0 else jnp.logical_not(first)
                qh = jnp.where(hmask, q, 0.0).astype(BF16)
                s2 = _dot_nt(qh, kcat)
                if has_prev is True:
                    bp = bias_prev[hh]
                elif has_prev is False:
                    bp = neg_tile
                else:
                    bp = jnp.where(has_prev, bias_prev[hh], neg)
                s_prev = s2[:, 0:B] + bp
                s_cur = s2[:, B:2 * B] + bias_cur[hh]
                m = jnp.max(jnp.maximum(s_prev, s_cur), axis=-1, keepdims=True)
                pcat = jnp.concatenate([jnp.exp2(s_prev - m), jnp.exp2(s_cur - m)], axis=1).astype(BF16)
                vals = va[kv, :] if hh == 0 else vb[kv, :]
                acc.append(_dot(pcat, vals))
                mx.append(m)
            num = jnp.where(first, acc[0], acc[1])
            den = pltpu.roll(jnp.where(first, acc[1], acc[0]), DIL_DH, axis=1)
            oc[cfg, rows, :] = num * (1.0 / den)
            lc[cfg, rows, :] = (jnp.where(first, mx[0], mx[1]) + jnp.log2(den)) * LN2

        def body(t, carry, block=block, nb=nb):
            for u in range(U):
                if nb % U == 0:
                    has_prev = True if u > 0 else (t * U) % nb > 0
                else:
                    assert U % nb == 0
                    has_prev = (u % nb) > 0
                block(t * U + u, has_prev)
            return carry

        lax.fori_loop(0, seq // B // U, body, 0)

    g = g_ref[...]
    CH = 512

    def mix(i, carry):
        rows = pl.ds(pl.multiple_of(i * CH, CH), CH)
        l0, l1, l2 = lc[0, rows, :], lc[1, rows, :], lc[2, rows, :]
        m = jnp.maximum(jnp.maximum(l0, l1), l2)
        e0, e1, e2 = jnp.exp(l0 - m), jnp.exp(l1 - m), jnp.exp(l2 - m)
        den = e0 + e1 + e2
        o = (e0 / den) * oc[0, rows, :] + (e1 / den) * oc[1, rows, :] + (e2 / den) * oc[2, rows, :]
        sq = o * o
        ms_a = jnp.sum(jnp.where(first, sq, 0.0), axis=-1, keepdims=True) * (1.0 / DIL_DH)
        ms_b = jnp.sum(jnp.where(first, 0.0, sq), axis=-1, keepdims=True) * (1.0 / DIL_DH)
        ms = jnp.where(first, ms_a, ms_b)
        o_ref[rows, :] = (o * lax.rsqrt(ms + EPS) * g).astype(BF16)
        return carry

    lax.fori_loop(0, seq // CH, mix, 0)


def _dilated(slopes, dq, dk, dv, g2, batch, seq):
    blk = pl.BlockSpec((seq, LANES), lambda b, p, s: (b, p))
    return pl.pallas_call(
        functools.partial(_dil_kernel, seq=seq),
        grid_spec=pltpu.PrefetchScalarGridSpec(
            num_scalar_prefetch=1,
            grid=(batch, DIL_WIDTH // LANES),
            in_specs=[blk, blk, blk, pl.BlockSpec((1, LANES), lambda b, p, s: (0, 0))],
            out_specs=blk,
            scratch_shapes=[pltpu.VMEM((seq, LANES), F32)] * 3
                           + [pltpu.VMEM((seq + DIL_PAD, LANES), BF16)] * 3
                           + [pltpu.VMEM((3, seq, LANES), F32)] * 2,
        ),
        out_shape=jax.ShapeDtypeStruct((batch * seq, DIL_WIDTH), BF16),
        compiler_params=pltpu.CompilerParams(
            dimension_semantics=("parallel", "parallel"), vmem_limit_bytes=VMEM_LIMIT),
        name="dilated",
    )(slopes, dq, dk, dv, g2)


def _out_proj_kernel(og_ref, od_ref, x_ref, wg_ref, wd_ref, g_ref, b_ref,
                     rwh_ref, rwl_ref, rb_ref, h_ref, hp_ref, eid_ref, gate_ref, cnt_ref):
    mix = _dot(og_ref[...], wg_ref[...]) + _dot(od_ref[...], wd_ref[...])
    h = _layer_norm(DEEPNORM_ALPHA * x_ref[...] + mix, g_ref[...], b_ref[...])
    h_ref[...] = h
    hp_ref[...] = _pack_bf16_halves(h)
    h_hi, h_lo = _split_bf16(h)
    lt = (_dot_nt(rwh_ref[...], h_hi) + _dot_nt(rwh_ref[...], h_lo) + _dot_nt(rwl_ref[...], h_hi)
          + rb_ref[...])
    tm = lt.shape[1]
    row = lax.broadcasted_iota(jnp.int32, (EXPERTS_PER_GROUP, tm), 0).astype(F32)
    neg = jnp.float32(-jnp.inf)
    big = jnp.float32(1e9)
    coarse = jnp.where(row < N_GROUPS, lt[N_EXPERTS:N_EXPERTS + EXPERTS_PER_GROUP, :], neg)
    cmax = jnp.max(coarse, axis=0, keepdims=True)
    g_idx = jnp.min(jnp.where(coarse == cmax, row, big), axis=0, keepdims=True)
    p_group = 1.0 / jnp.sum(jnp.exp(coarse - cmax), axis=0, keepdims=True)
    fine = lt[(N_GROUPS - 1) * EXPERTS_PER_GROUP:N_EXPERTS, :]
    for g in range(N_GROUPS - 2, -1, -1):
        fine = jnp.where(g_idx == g, lt[g * EXPERTS_PER_GROUP:(g + 1) * EXPERTS_PER_GROUP, :], fine)
    v1 = jnp.max(fine, axis=0, keepdims=True)
    i1 = jnp.min(jnp.where(fine == v1, row, big), axis=0, keepdims=True)
    fine2 = jnp.where(row == i1, neg, fine)
    v2 = jnp.max(fine2, axis=0, keepdims=True)
    i2 = jnp.min(jnp.where(fine2 == v2, row, big), axis=0, keepdims=True)
    e2 = jnp.exp(v2 - v1)
    den = 1.0 + e2
    gate1 = p_group * (1.0 / den)
    gate2 = p_group * (e2 / den)
    id1 = g_idx * EXPERTS_PER_GROUP + i1
    id2 = g_idx * EXPERTS_PER_GROUP + i2
    eid_ref[...] = jnp.concatenate([id1, id2], axis=0).astype(jnp.int32)
    slab = jnp.concatenate([gate1, gate2, jnp.zeros((LANES - 2, tm), F32)], axis=0)
    gate_ref[...] = slab.T[:, 0:2]
    sub = lax.broadcasted_iota(jnp.int32, (LANES, tm), 0).astype(F32)
    onehot = jnp.logical_or(sub == id1, sub == id2).astype(BF16)

    @pl.when(pl.program_id(0) == 0)
    def _():
        cnt_ref[...] = jnp.zeros_like(cnt_ref)

    cnt_ref[...] += _dot(onehot, jnp.ones((tm, LANES), BF16))


def _out_proj(og, od, x2, wg, wd, g, b, rwh, rwl, rb, tm=1024):
    T = x2.shape[0]
    row = lambda w: pl.BlockSpec((tm, w), lambda i: (i, 0))
    full = lambda a: pl.BlockSpec(a.shape, lambda i: (0,) * a.ndim)
    return pl.pallas_call(
        _out_proj_kernel,
        grid=(T // tm,),
        in_specs=[row(512), row(512), row(D_MODEL), full(wg), full(wd), full(g), full(b),
                  full(rwh), full(rwl), full(rb)],
        out_specs=[row(D_MODEL), row(D_MODEL // 2), pl.BlockSpec((2, tm), lambda i: (0, i)), row(2),
                   pl.BlockSpec((LANES, LANES), lambda i: (0, 0))],
        out_shape=[jax.ShapeDtypeStruct((T, D_MODEL), F32),
                   jax.ShapeDtypeStruct((T, D_MODEL // 2), jnp.int32),
                   jax.ShapeDtypeStruct((2, T), jnp.int32),
                   jax.ShapeDtypeStruct((T, 2), F32),
                   jax.ShapeDtypeStruct((LANES, LANES), F32)],
        compiler_params=pltpu.CompilerParams(
            dimension_semantics=("arbitrary",), vmem_limit_bytes=VMEM_LIMIT),
        name="out_proj_router",
    )(og, od, x2, wg, wd, g, b, rwh, rwl, rb)


def _positions_kernel(eid_ref, cnt_ref, dest_ref, be_ref, nv_ref, carry_ref, sp_ref, tri_ref, *, tb):
    i = pl.program_id(0)

    @pl.when(i == 0)
    def _():
        shift = int(math.log2(ROW_BLOCK))
        nb_col = (cnt_ref[...].astype(jnp.int32) + (ROW_BLOCK - 1)) >> shift
        r = lax.broadcasted_iota(jnp.int32, (LANES, LANES), 0)
        c = lax.broadcasted_iota(jnp.int32, (LANES, LANES), 1)
        nb_f = jnp.where(r < N_EXPERTS, nb_col, 0).astype(F32)
        start_col = _dot((c < r).astype(BF16), nb_f.astype(BF16))
        sp_ref[...] = start_col * float(ROW_BLOCK)
        carry_ref[...] = jnp.zeros_like(carry_ref)
        be_ref[...] = jnp.concatenate([start_col.T[0:1, :], nb_f.T[0:1, :]], axis=1).astype(jnp.int32)
        total = jnp.sum(nb_f[:, 0:1], axis=0, keepdims=True)
        nv_ref[...] = jnp.broadcast_to(total, (1, LANES)).astype(jnp.int32)
        tr = lax.broadcasted_iota(jnp.int32, (tb, tb), 0)
        tc = lax.broadcasted_iota(jnp.int32, (tb, tb), 1)
        tri_ref[...] = (tr < tc).astype(BF16)

    sub = lax.broadcasted_iota(jnp.int32, (LANES, tb), 0)
    oh1 = sub == eid_ref[0:1, :]
    oh2 = sub == eid_ref[1:2, :]
    oh = jnp.logical_or(oh1, oh2).astype(BF16)
    offset = jnp.tile(carry_ref[...] + sp_ref[...], (1, tb // LANES))
    before = _dot(oh, tri_ref[...]) + offset
    d1 = jnp.sum(jnp.where(oh1, before, 0.0), axis=0, keepdims=True)
    d2 = jnp.sum(jnp.where(oh2, before, 0.0), axis=0, keepdims=True)
    dest_ref[...] = jnp.concatenate([d1, d2], axis=0).astype(jnp.int32)
    carry_ref[...] += _dot(oh, jnp.ones((tb, LANES), BF16))


def _positions(eid_t, cnt, n_blocks_pad, tb=1024):
    T = eid_t.shape[1]
    return pl.pallas_call(
        functools.partial(_positions_kernel, tb=tb),
        grid=(T // tb,),
        in_specs=[pl.BlockSpec((2, tb), lambda i: (0, i)), pl.BlockSpec((LANES, LANES), lambda i: (0, 0))],
        out_specs=[pl.BlockSpec((2, tb), lambda i: (0, i)),
                   pl.BlockSpec((1, n_blocks_pad), lambda i: (0, 0)),
                   pl.BlockSpec((1, LANES), lambda i: (0, 0))],
        out_shape=[jax.ShapeDtypeStruct((2, T), jnp.int32),
                   jax.ShapeDtypeStruct((1, n_blocks_pad), jnp.int32),
                   jax.ShapeDtypeStruct((1, LANES), jnp.int32)],
        scratch_shapes=[pltpu.VMEM((LANES, LANES), F32), pltpu.VMEM((LANES, LANES), F32),
                        pltpu.VMEM((tb, tb), BF16)],
        compiler_params=pltpu.CompilerParams(dimension_semantics=("arbitrary",)),
        name="positions",
    )(eid_t, cnt)


def _sc_gather_rows(table, idx):
    n = idx.shape[0]
    d = table.shape[1]
    info = plsc.get_sparse_core_info()
    nc, ns = info.num_cores, info.num_subcores
    per_w = n // (nc * ns)
    assert per_w * nc * ns == n and per_w % SC_INDEX_WINDOW == 0
    mesh = plsc.VectorSubcoreMesh(core_axis_name="core", subcore_axis_name="subcore")
    nchunk = per_w // SC_GATHER_ROWS
    nbuf = SC_GATHER_BUFFERS

    @functools.partial(
        pl.kernel, out_type=jax.ShapeDtypeStruct((n, d), table.dtype), mesh=mesh,
        scratch_types=[pltpu.VMEM((per_w,), jnp.int32),
                       pltpu.VMEM((nbuf, SC_GATHER_ROWS, d), table.dtype),
                       pltpu.SemaphoreType.DMA((nbuf,)), pltpu.SemaphoreType.DMA((nbuf,))],
        name="sc_gather_rows")
    def gather(x_hbm, i_hbm, o_hbm, idx_v, buf, gsem, wsem):
        wid = lax.axis_index("subcore") * nc + lax.axis_index("core")
        base = wid * per_w
        pltpu.sync_copy(i_hbm.at[pl.ds(base, per_w)], idx_v)

        def gather_copy(c):
            rows = idx_v.at[pl.ds(c * SC_GATHER_ROWS, SC_GATHER_ROWS)]
            return pltpu.make_async_copy(x_hbm.at[rows], buf.at[c % nbuf], gsem.at[c % nbuf])

        def write_copy(c):
            dst = o_hbm.at[pl.ds(base + c * SC_GATHER_ROWS, SC_GATHER_ROWS)]
            return pltpu.make_async_copy(buf.at[c % nbuf], dst, wsem.at[c % nbuf])

        for c in range(min(nbuf - 1, nchunk)):
            gather_copy(c).start()
        for c in range(nchunk):
            gather_copy(c).wait()
            write_copy(c).start()
            if c + nbuf - 1 < nchunk:
                if c >= 1:
                    write_copy(c - 1).wait()
                gather_copy(c + nbuf - 1).start()
        for c in range(max(0, nchunk - nbuf), nchunk):
            write_copy(c).wait()

    return gather(table, idx)


def _sc_inverse_rows(dest_flat, n_rows, chunk=2048):
    n = dest_flat.shape[0]
    n_tokens = n // 2
    assert n_rows <= 3 * n_tokens
    nc = plsc.get_sparse_core_info().num_cores
    mesh = plsc.VectorSubcoreMesh(core_axis_name="core", subcore_axis_name="subcore")

    @functools.partial(
        pl.kernel, out_type=jax.ShapeDtypeStruct((n_rows,), jnp.int32), mesh=mesh,
        scratch_types=[pltpu.VMEM((n_rows,), jnp.int32), pltpu.VMEM((chunk,), jnp.int32)],
        compiler_params=pltpu.CompilerParams(needs_layout_passes=False),
        name="sc_inverse_rows")
    def inverse(d_hbm, o_hbm, inv_v, d_v):
        wid = lax.axis_index("subcore") * nc + lax.axis_index("core")

        @pl.when(wid == 0)
        def _():
            lanes = lax.iota(jnp.int32, SC_LANES)

            @pl.loop(0, n_rows // SC_LANES)
            def _(i):
                r = lanes + i * SC_LANES
                r = jnp.where(r >= n_tokens, r - n_tokens, r)
                inv_v[pl.ds(i * SC_LANES, SC_LANES)] = jnp.where(r >= n_tokens, r - n_tokens, r)

            @pl.loop(0, n // chunk)
            def _(c):
                pltpu.sync_copy(d_hbm.at[pl.ds(c * chunk, chunk)], d_v)

                @pl.loop(0, chunk // SC_LANES)
                def _(j):
                    rows = d_v[pl.ds(j * SC_LANES, SC_LANES)]
                    pair = lax.iota(jnp.int32, SC_LANES) + (c * chunk + j * SC_LANES)
                    plsc.store_scatter(inv_v, [rows], jnp.where(pair >= n_tokens, pair - n_tokens, pair))

            pltpu.sync_copy(inv_v, o_hbm)

    return inverse(dest_flat)


def _ffn_kernel(first_ref, count_ref, nv_ref, wg_hbm, wu_hbm, wd_hbm, xs_hbm, y_hbm,
                wg32, wu32, wd32, wgb, wub, wdb, xbuf, ybuf, wsem, isem, osem, *, n_blocks):
    nv = nv_ref[0]
    nbuf = FFN_BUFFERS

    def next_expert(e):
        def more(t):
            return jnp.logical_and(t < N_EXPERTS, count_ref[jnp.minimum(t, N_EXPERTS - 1)] == 0)
        return lax.while_loop(more, lambda t: t + 1, e + 1)

    def weight_copies(e, slot):
        ee = jnp.minimum(e, N_EXPERTS - 1)
        copies = []
        for i, (src, dst) in enumerate(((wg_hbm, wg32), (wu_hbm, wu32), (wd_hbm, wd32))):
            rows = src.shape[1] // FFN_WEIGHT_CHUNKS
            for c in range(FFN_WEIGHT_CHUNKS):
                part = pl.ds(c * rows, rows)
                copies.append(pltpu.make_async_copy(src.at[ee, part], dst.at[slot, part],
                                                    wsem.at[slot, i * FFN_WEIGHT_CHUNKS + c]))
        return copies

    def fetch_weights(e, slot):
        @pl.when(e < N_EXPERTS)
        def _():
            for c in weight_copies(e, slot):
                c.start()

    def take_weights(e, slot):
        for c in weight_copies(e, slot):
            c.wait()
        wgb[...] = wg32[slot].astype(BF16)
        wub[...] = wu32[slot].astype(BF16)
        wdb[...] = wd32[slot].astype(BF16)
        fetch_weights(next_expert(next_expert(e)), slot)

    def rows_of(b):
        return pl.ds(pl.multiple_of(b * ROW_BLOCK, ROW_BLOCK), ROW_BLOCK)

    def in_copy(b):
        return pltpu.make_async_copy(xs_hbm.at[rows_of(b)], xbuf.at[b % nbuf], isem.at[b % nbuf])

    def out_copy(b):
        return pltpu.make_async_copy(ybuf.at[b % nbuf], y_hbm.at[rows_of(b)], osem.at[b % nbuf])

    @pl.when(nv > 0)
    def _():
        e0 = next_expert(jnp.int32(-1))
        for i in range(nbuf - 1):
            @pl.when(i < nv)
            def _():
                in_copy(i).start()
        fetch_weights(e0, 0)
        fetch_weights(next_expert(e0), 1)
        take_weights(e0, 0)

        def body(b, carry):
            e, k = carry
            switch = b >= first_ref[e] + count_ref[e]
            e_new = jnp.where(switch, next_expert(e), e)
            k_new = jnp.where(switch, k + 1, k)

            @pl.when(switch)
            def _():
                take_weights(e_new, k_new % 2)

            in_copy(b).wait()

            @pl.when(b + nbuf - 1 < nv)
            def _():
                in_copy(b + nbuf - 1).start()

            @pl.when(b >= nbuf)
            def _():
                out_copy(b - nbuf).wait()

            x_hi, x_lo = _unpack_bf16_halves(xbuf[b % nbuf])
            xb = jnp.concatenate([x_hi.astype(BF16), x_lo.astype(BF16)], axis=1)
            a = _dot(xb, wgb[...])
            u = _dot(xb, wub[...])
            hid = (a * jax.nn.sigmoid(a) * u).astype(BF16)
            ybuf[b % nbuf] = _pack_bf16_halves(_dot(hid, wdb[...]))
            out_copy(b).start()
            return e_new, k_new

        lax.fori_loop(0, nv, body, (e0, jnp.int32(0)))

        for i in range(nbuf):
            @pl.when(nv > i)
            def _():
                out_copy(nv - 1 - i).wait()

    ybuf[0] = jnp.zeros((ROW_BLOCK, D_MODEL // 2), jnp.int32)

    def fill(b, carry):
        pltpu.sync_copy(ybuf.at[0], y_hbm.at[rows_of(b)])
        return carry

    lax.fori_loop(nv, n_blocks, fill, 0)


def _ffn(first_blk, n_blk, nv, xs, w_gate, w_up, w_down):
    n_rows = xs.shape[0]
    n_blocks = n_rows // ROW_BLOCK
    anyspec = pl.BlockSpec(memory_space=pl.ANY)
    return pl.pallas_call(
        functools.partial(_ffn_kernel, n_blocks=n_blocks),
        grid_spec=pltpu.PrefetchScalarGridSpec(
            num_scalar_prefetch=3,
            grid=(1,),
            in_specs=[anyspec, anyspec, anyspec, anyspec],
            out_specs=anyspec,
            scratch_shapes=[pltpu.VMEM((2, D_MODEL, D_FF), F32), pltpu.VMEM((2, D_MODEL, D_FF), F32),
                            pltpu.VMEM((2, D_FF, D_MODEL), F32),
                            pltpu.VMEM((D_MODEL, D_FF), BF16), pltpu.VMEM((D_MODEL, D_FF), BF16),
                            pltpu.VMEM((D_FF, D_MODEL), BF16),
                            pltpu.VMEM((FFN_BUFFERS, ROW_BLOCK, D_MODEL // 2), jnp.int32),
                            pltpu.VMEM((FFN_BUFFERS, ROW_BLOCK, D_MODEL // 2), jnp.int32),
                            pltpu.SemaphoreType.DMA((2, 3 * FFN_WEIGHT_CHUNKS)),
                            pltpu.SemaphoreType.DMA((FFN_BUFFERS,)),
                            pltpu.SemaphoreType.DMA((FFN_BUFFERS,))],
        ),
        out_shape=jax.ShapeDtypeStruct((n_rows, D_MODEL // 2), jnp.int32),
        compiler_params=pltpu.CompilerParams(
            dimension_semantics=("arbitrary",), vmem_limit_bytes=VMEM_LIMIT),
        name="expert_ffn",
    )(first_blk, n_blk, nv, w_gate, w_up, w_down, xs)


def _combine_kernel(h_ref, ya_ref, yb_ref, gate_ref, g_ref, b_ref, o_ref):
    gate = gate_ref[...]
    a_hi, a_lo = _unpack_bf16_halves(ya_ref[...])
    b_hi, b_lo = _unpack_bf16_halves(yb_ref[...])
    g0, g1 = gate[:, 0:1], gate[:, 1:2]
    ffn = jnp.concatenate([a_hi * g0 + b_hi * g1, a_lo * g0 + b_lo * g1], axis=1)
    o_ref[...] = _layer_norm(DEEPNORM_ALPHA * h_ref[...] + ffn, g_ref[...], b_ref[...])


def _combine(h, y2, gate, g, b, tm=512):
    T = h.shape[0]
    nt = T // tm
    return pl.pallas_call(
        _combine_kernel,
        grid=(nt,),
        in_specs=[pl.BlockSpec((tm, D_MODEL), lambda i: (i, 0)),
                  pl.BlockSpec((tm, D_MODEL // 2), lambda i: (i, 0)),
                  pl.BlockSpec((tm, D_MODEL // 2), lambda i: (i + nt, 0)),
                  pl.BlockSpec((tm, 2), lambda i: (i, 0)),
                  pl.BlockSpec((1, D_MODEL), lambda i: (0, 0)),
                  pl.BlockSpec((1, D_MODEL), lambda i: (0, 0))],
        out_specs=pl.BlockSpec((tm, D_MODEL), lambda i: (i, 0)),
        out_shape=jax.ShapeDtypeStruct((T, D_MODEL), F32),
        compiler_params=pltpu.CompilerParams(
            dimension_semantics=("parallel",), vmem_limit_bytes=VMEM_LIMIT),
        name="combine",
    )(h, y2, y2, gate, g, b)


def kernel(x, w_in, gla_gate_w2, gla_gate_b, gla_norm_g, dil_norm_g, w_out, ln1_g, ln1_b,
           router_coarse_w, router_coarse_b, router_fine_w, router_fine_b,
           expert_w_gate, expert_w_up, expert_w_down, ln2_g, ln2_b):
    B, S, D = x.shape
    T = B * S
    depth = w_in.shape[0]
    slopes = jnp.exp2(-8.0 * jnp.arange(1, DIL_HEADS + 1, dtype=F32) / DIL_HEADS)
    n_rows = 2 * T + N_EXPERTS * ROW_BLOCK
    n_blocks = n_rows // ROW_BLOCK
    n_blocks_pad = -(-n_blocks // (2 * LANES)) * (2 * LANES)
    assert n_blocks_pad == 2 * LANES
    a0 = 1536
    h = x.reshape(T, D)
    for l in range(depth):
        w = w_in[l]
        wm = jnp.concatenate([w[:, :a0], w[:, a0 + GLA_GATE_RANK:]], axis=1).astype(BF16)
        wa = jnp.pad(w[:, a0:a0 + GLA_GATE_RANK], ((0, 0), (0, LANES - GLA_GATE_RANK))).astype(BF16)
        w2 = jnp.pad(gla_gate_w2[l], ((0, LANES - GLA_GATE_RANK), (0, 0)))
        w2h, w2l = _split_bf16(w2)
        q, k, v, r, la, dq, dk, dv = _in_proj(h, wm, wa, w2h, w2l, gla_gate_b[l][None, :])
        o_gla = _gla(q, k, v, r, la, gla_norm_g[l][None, :], B, S)
        g2 = jnp.tile(dil_norm_g[l], 2)[None, :]
        o_dil = _dilated(slopes, dq, dk, dv, g2, B, S)
        wo = w_out[l].astype(BF16)
        rw = jnp.concatenate([router_fine_w[l].reshape(D, N_EXPERTS), router_coarse_w[l]], axis=1)
        rw = jnp.pad(rw, ((0, 0), (0, LANES - N_EXPERTS - N_GROUPS))).T
        rwh, rwl = _split_bf16(rw)
        rb = jnp.concatenate([router_fine_b[l].reshape(N_EXPERTS), router_coarse_b[l]])
        rb = jnp.pad(rb, (0, LANES - N_EXPERTS - N_GROUPS))[:, None]
        h1, h1p, eid_t, gate, cnt = _out_proj(o_gla, o_dil, h, wo[:GLA_WIDTH], wo[GLA_WIDTH:],
                                              ln1_g[l][None, :], ln1_b[l][None, :], rwh, rwl, rb)
        dest_t, be, nv = _positions(eid_t, cnt, n_blocks_pad)
        dest_flat = dest_t.reshape(2 * T)
        src_tok = _sc_inverse_rows(dest_flat, n_rows)
        xs = _sc_gather_rows(h1p, src_tok)
        be = be.reshape(n_blocks_pad)
        y = _ffn(be[:N_EXPERTS], be[LANES:LANES + N_EXPERTS], nv.reshape(LANES)[:1], xs,
                 expert_w_gate[l], expert_w_up[l], expert_w_down[l])
        y2 = _sc_gather_rows(y, dest_flat)
        h = _combine(h1, y2, gate, ln2_g[l][None, :], ln2_b[l][None, :])
    return h.reshape(B, S, D)
```

```python
import functools
import math

import jax
import jax.numpy as jnp
import numpy as np
from jax import lax
from jax.experimental import pallas as pl
from jax.experimental.pallas import tpu as pltpu
from jax.experimental.pallas import tpu_sc as plsc

D_MODEL = 1024
GLA_HEADS = 4
GLA_DK = 64
GLA_DV = 128
GLA_KEY_WIDTH = GLA_HEADS * GLA_DK
GLA_WIDTH = GLA_HEADS * GLA_DV
GLA_GATE_RANK = 16
GLA_GATE_TEMP = 16.0
DIL_HEADS = 8
DIL_DH = 64
DIL_WIDTH = DIL_HEADS * DIL_DH
DIL_CONFIGS = ((128, 1), (512, 4), (2048, 16))
DIL_BLOCK = 128
DIL_MAX_R = max(r for _, r in DIL_CONFIGS)
DIL_PAD = DIL_BLOCK * DIL_MAX_R
DIL_UNROLL = 8
N_GROUPS = 4
EXPERTS_PER_GROUP = 8
N_EXPERTS = N_GROUPS * EXPERTS_PER_GROUP
D_FF = 512
DEEPNORM_ALPHA = 2.0 ** 0.25
EPS = 1e-5
LOG2E = math.log2(math.e)
LN2 = math.log(2.0)

LANES = 128
GLA_CHUNK = 128
GLA_SUB = 64
GLA_UNROLL = 4
SC_LANES = 16
SC_INDEX_WINDOW = 128
SC_GATHER_BUFFERS = 6
SC_GATHER_ROWS = 32
FFN_BUFFERS = 8
FFN_WEIGHT_CHUNKS = 4
ROW_BLOCK = 256
VMEM_LIMIT = 56 * 1024 * 1024

F32 = jnp.float32
BF16 = jnp.bfloat16


def _dot(a, b):
    return jnp.dot(a, b, preferred_element_type=F32)


def _dot_nt(a, b):
    return lax.dot_general(a, b, (((1,), (1,)), ((), ())), preferred_element_type=F32)


def _dot_tn(a, b):
    return lax.dot_general(a, b, (((0,), (0,)), ((), ())), preferred_element_type=F32)


def _split_bf16(v):
    hi = v.astype(BF16)
    lo = (v - hi.astype(F32)).astype(BF16)
    return hi, lo


def _pack_bf16_halves(v):
    w = v.shape[1] // 2
    hi = lax.bitcast_convert_type(v[:, :w].astype(BF16).astype(F32), jnp.int32)
    lo = lax.bitcast_convert_type(v[:, w:].astype(BF16).astype(F32), jnp.int32)
    return hi | lax.shift_right_logical(lo, 16)


def _unpack_bf16_halves(words):
    hi = lax.bitcast_convert_type(words & jnp.int32(-65536), F32)
    lo = lax.bitcast_convert_type(lax.shift_left(words, 16), F32)
    return hi, lo


def _layer_norm(v, g, b):
    mu = jnp.mean(v, axis=-1, keepdims=True)
    c = v - mu
    var = jnp.mean(c * c, axis=-1, keepdims=True)
    return c * lax.rsqrt(var + EPS) * g + b


def _in_proj_kernel(x_ref, wm_ref, wa_ref, w2h_ref, w2l_ref, gb_ref,
                    q_ref, k_ref, v_ref, r_ref, la_ref, dq_ref, dk_ref, dv_ref):
    xb = x_ref[...].astype(BF16)

    def piece(c0, c1):
        return _dot(xb, wm_ref[:, c0:c1])

    q_ref[...] = (piece(0, 256) * (GLA_DK ** -0.5)).astype(BF16)
    k_ref[...] = piece(256, 512).astype(BF16)
    v_ref[...] = piece(512, 1024).astype(BF16)
    r_ref[...] = piece(1024, 1536).astype(BF16)
    dq_ref[...] = (piece(1536, 2048) * (DIL_DH ** -0.5 * LOG2E)).astype(BF16)
    dk_ref[...] = piece(2048, 2560).astype(BF16)
    dv_ref[...] = piece(2560, 3072).astype(BF16)
    ga = _dot(xb, wa_ref[...])
    ga_hi, ga_lo = _split_bf16(ga)
    z = _dot(ga_hi, w2h_ref[...]) + _dot(ga_lo, w2h_ref[...]) + _dot(ga_hi, w2l_ref[...]) + gb_ref[...]
    log_sig = jnp.minimum(z, 0.0) - jnp.log1p(jnp.exp(-jnp.abs(z)))
    la_ref[...] = log_sig * (1.0 / GLA_GATE_TEMP)


def _in_proj(x2, wm, wa, w2h, w2l, gb, tm=1024):
    T = x2.shape[0]
    row = lambda w: pl.BlockSpec((tm, w), lambda i: (i, 0))
    full = lambda a: pl.BlockSpec(a.shape, lambda i: (0,) * a.ndim)
    outs = [(256, BF16), (256, BF16), (512, BF16), (512, BF16), (256, F32),
            (512, BF16), (512, BF16), (512, BF16)]
    return pl.pallas_call(
        _in_proj_kernel,
        grid=(T // tm,),
        in_specs=[row(D_MODEL), full(wm), full(wa), full(w2h), full(w2l), full(gb)],
        out_specs=[row(w) for w, _ in outs],
        out_shape=[jax.ShapeDtypeStruct((T, w), dt) for w, dt in outs],
        compiler_params=pltpu.CompilerParams(
            dimension_semantics=("parallel",), vmem_limit_bytes=VMEM_LIMIT),
        name="in_proj",
    )(x2, wm, wa, w2h, w2l, gb)


def _gla_kernel(q_ref, k_ref, v_ref, r_ref, la_ref, g_ref, o_ref, s_ref, *, seq_block):
    C = GLA_CHUNK
    H = GLA_SUB
    assert C == 2 * H

    @pl.when(pl.program_id(1) == 0)
    def _():
        s_ref[...] = jnp.zeros_like(s_ref)

    ri = lax.broadcasted_iota(jnp.int32, (C, C), 0)
    ci = lax.broadcasted_iota(jnp.int32, (C, C), 1)
    same_sub = (ri // H) == (ci // H)
    sum_ops = jnp.concatenate([jnp.logical_and(same_sub, ci <= ri).astype(BF16), same_sub.astype(BF16),
                               jnp.ones((C, C), BF16)], axis=0)
    diag_mask = jnp.logical_and(same_sub, ci <= ri)
    off_mask = (ri // H) > (ci // H)
    second = lax.broadcasted_iota(jnp.int32, (C, 1), 0) >= H
    ones_cl = jnp.ones((C, LANES), BF16)
    lane_k = lax.broadcasted_iota(jnp.int32, (1, GLA_KEY_WIDTH), 1) // GLA_DK
    head_masks = [(lane_k == h).astype(F32) for h in range(GLA_HEADS)]
    srow = lax.broadcasted_iota(jnp.int32, (GLA_KEY_WIDTH, GLA_WIDTH), 0) // GLA_DK
    scol = lax.broadcasted_iota(jnp.int32, (GLA_KEY_WIDTH, GLA_WIDTH), 1) // GLA_DV
    state_mask = (srow == scol).astype(F32)
    g = g_ref[...]

    def chunk(c, carry):
        rows = pl.ds(pl.multiple_of(c * C, C), C)
        la = la_ref[rows, :]
        la_hi, la_lo = _split_bf16(la)
        la2 = jnp.concatenate([la_hi, la_lo], axis=1)
        sums = _dot(sum_ops, la2)
        sums = sums[:, 0:GLA_KEY_WIDTH] + sums[:, GLA_KEY_WIDTH:]
        b = sums[0:C]
        t = sums[C:2 * C]
        other = sums[2 * C:3 * C] - t
        q = q_ref[rows, :].astype(F32)
        k = k_ref[rows, :].astype(F32)
        v = v_ref[rows, :]
        qd = q * jnp.exp(b)
        kd = (k * jnp.exp(-b)).astype(BF16)
        ke_f = k * jnp.exp(t - b)
        cross = jnp.exp(other)
        q_state = (qd * jnp.where(second, cross, 1.0)).astype(BF16)
        k_state = (ke_f * jnp.where(second, 1.0, cross)).astype(BF16)
        state = s_ref[...]
        o_inter = _dot(q_state, state.astype(BF16))
        q_heads = jnp.concatenate([(qd * head_masks[h]).astype(BF16) for h in range(GLA_HEADS)], axis=0)
        keys2 = jnp.concatenate([kd, ke_f.astype(BF16)], axis=0)
        scores = _dot_nt(q_heads, keys2)
        outs = []
        for h in range(GLA_HEADS):
            sh = scores[h * C:(h + 1) * C]
            a = jnp.where(diag_mask, sh[:, 0:C], 0.0) + jnp.where(off_mask, sh[:, C:2 * C], 0.0)
            cols = slice(h * GLA_DV, (h + 1) * GLA_DV)
            o = _dot(a.astype(BF16), v[:, cols]) + o_inter[:, cols]
            o = o * lax.rsqrt(jnp.mean(o * o, axis=-1, keepdims=True) + EPS) * g
            outs.append(o)
        o_all = jnp.concatenate(outs, axis=-1)
        rr = r_ref[rows, :].astype(F32)
        o_ref[rows, :] = (o_all * (rr * jax.nn.sigmoid(rr))).astype(BF16)
        tot = _dot_tn(la2, ones_cl)
        dec = jnp.exp(tot[0:GLA_KEY_WIDTH] + tot[GLA_KEY_WIDTH:])
        upd = _dot_tn(k_state, v) * state_mask
        for h in range(GLA_HEADS):
            cols = slice(h * GLA_DV, (h + 1) * GLA_DV)
            s_ref[:, cols] = state[:, cols] * dec + upd[:, cols]
        return carry

    lax.fori_loop(0, seq_block // C, chunk, 0, unroll=GLA_UNROLL)


def _gla(q, k, v, r, la, g, batch, seq, seq_block=1024):
    nsb = seq // seq_block
    row = lambda w: pl.BlockSpec((seq_block, w), lambda b, s: (b * nsb + s, 0))
    return pl.pallas_call(
        functools.partial(_gla_kernel, seq_block=seq_block),
        grid=(batch, nsb),
        in_specs=[row(256), row(256), row(512), row(512), row(256),
                  pl.BlockSpec((1, GLA_DV), lambda b, s: (0, 0))],
        out_specs=row(512),
        out_shape=jax.ShapeDtypeStruct((batch * seq, GLA_WIDTH), BF16),
        scratch_shapes=[pltpu.VMEM((GLA_KEY_WIDTH, GLA_WIDTH), F32)],
        compiler_params=pltpu.CompilerParams(
            dimension_semantics=("parallel", "arbitrary"), vmem_limit_bytes=VMEM_LIMIT),
        name="gla",
    )(q, k, v, r, la, g)


def _dil_kernel(slope_ref, q_ref, k_ref, v_ref, g_ref, o_ref,
                qf, kf, vf, kd, va, vb, oc, lc, *, seq):
    B = DIL_BLOCK
    U = DIL_UNROLL
    pair = pl.program_id(1)
    qf[...] = q_ref[...].astype(F32)
    kf[...] = k_ref[...].astype(F32)
    vf[...] = v_ref[...].astype(F32)

    lane = lax.broadcasted_iota(jnp.int32, (1, LANES), 1)
    first = lane < DIL_DH
    ii = lax.broadcasted_iota(jnp.int32, (B, B), 0)
    jj = lax.broadcasted_iota(jnp.int32, (B, B), 1)
    upper = jj > ii
    eye = jj == ii
    dist = jnp.bitwise_and(ii - jj, B - 1).astype(F32)
    neg = jnp.float32(-jnp.inf)
    neg_tile = jnp.full((B, B), neg, F32)
    zero_tile = jnp.zeros((B, LANES), BF16)

    for cfg, (window, r) in enumerate(DIL_CONFIGS):
        nb = seq // r // B
        cs = nb + 1
        bias_prev, bias_cur = [], []
        for hh in range(2):
            slope = slope_ref[2 * pair + hh] * (float(r) * LOG2E)
            bias = dist * (-slope)
            bias_prev.append(jnp.where(upper, bias, jnp.where(eye, -slope * float(B), neg)))
            bias_cur.append(jnp.where(upper, neg, bias))

        for c in range(r):
            rows0 = slice(c * cs * B, (c * cs + 1) * B)
            kd[rows0, :] = zero_tile
            va[rows0, :] = zero_tile
            vb[rows0, :] = zero_tile

        def prep(t4, carry, r=r, nb=nb, cs=cs):
            for j in range(4):
                t = t4 * 4 + j
                c = t // nb
                n = t % nb
                start = c + n * (B * r)
                rows = pl.ds(start, B, stride=r) if r > 1 else pl.ds(pl.multiple_of(start, B), B)
                dst = pl.ds(pl.multiple_of((c * cs + 1 + n) * B, B), B)
                kd[dst, :] = kf[rows, :].astype(BF16)
                v = vf[rows, :]
                va[dst, :] = jnp.where(first, v, 1.0).astype(BF16)
                vb[dst, :] = jnp.where(first, 1.0, v).astype(BF16)
            return carry

        lax.fori_loop(0, seq // B // 4, prep, 0)

        def geom(it, r=r, nb=nb, cs=cs):
            c = it // nb
            n = it % nb
            start = c + n * (B * r)
            rows = pl.ds(start, B, stride=r) if r > 1 else pl.ds(pl.multiple_of(start, B), B)
            kv = pl.ds(pl.multiple_of((c * cs + n) * B, B), 2 * B)
            return rows, kv

        def body(t, carry, cfg=cfg, nb=nb, bias_prev=bias_prev, bias_cur=bias_cur, geom=geom):
            geo = [geom(t * U + u) for u in range(U)]
            scores = []
            for u in range(U):
                rows, kv = geo[u]
                q = qf[rows, :]
                q_heads = jnp.concatenate([jnp.where(first, q, 0.0), jnp.where(first, 0.0, q)], axis=0)
                scores.append(_dot_nt(q_heads.astype(BF16), kd[kv, :]))
            probs, maxes = [], []
            for u in range(U):
                if nb % U == 0:
                    has_prev = True if u > 0 else (t * U) % nb > 0
                else:
                    assert U % nb == 0
                    has_prev = (u % nb) > 0
                for hh in range(2):
                    s2 = scores[u][hh * B:(hh + 1) * B]
                    if has_prev is True:
                        bp = bias_prev[hh]
                    elif has_prev is False:
                        bp = neg_tile
                    else:
                        bp = jnp.where(has_prev, bias_prev[hh], neg)
                    s_prev = s2[:, 0:B] + bp
                    s_cur = s2[:, B:2 * B] + bias_cur[hh]
                    m = jnp.max(jnp.maximum(s_prev, s_cur), axis=-1, keepdims=True)
                    probs.append(jnp.concatenate([jnp.exp2(s_prev - m), jnp.exp2(s_cur - m)], axis=1).astype(BF16))
                    maxes.append(m)
            for u in range(U):
                rows, kv = geo[u]
                acc0 = _dot(probs[2 * u], va[kv, :])
                acc1 = _dot(probs[2 * u + 1], vb[kv, :])
                num = jnp.where(first, acc0, acc1)
                den = pltpu.roll(jnp.where(first, acc1, acc0), DIL_DH, axis=1)
                oc[cfg, rows, :] = num * (1.0 / den)
                lc[cfg, rows, :] = (jnp.where(first, maxes[2 * u], maxes[2 * u + 1]) + jnp.log2(den)) * LN2
            return carry

        lax.fori_loop(0, seq // B // U, body, 0)

    g = g_ref[...]
    CH = 512

    def mix(i, carry):
        rows = pl.ds(pl.multiple_of(i * CH, CH), CH)
        l0, l1, l2 = lc[0, rows, :], lc[1, rows, :], lc[2, rows, :]
        m = jnp.maximum(jnp.maximum(l0, l1), l2)
        e0, e1, e2 = jnp.exp(l0 - m), jnp.exp(l1 - m), jnp.exp(l2 - m)
        den = e0 + e1 + e2
        o = (e0 / den) * oc[0, rows, :] + (e1 / den) * oc[1, rows, :] + (e2 / den) * oc[2, rows, :]
        sq = o * o
        ms_a = jnp.sum(jnp.where(first, sq, 0.0), axis=-1, keepdims=True) * (1.0 / DIL_DH)
        ms_b = jnp.sum(jnp.where(first, 0.0, sq), axis=-1, keepdims=True) * (1.0 / DIL_DH)
        ms = jnp.where(first, ms_a, ms_b)
        o_ref[rows, :] = (o * lax.rsqrt(ms + EPS) * g).astype(BF16)
        return carry

    lax.fori_loop(0, seq // CH, mix, 0)


def _dilated(slopes, dq, dk, dv, g2, batch, seq):
    blk = pl.BlockSpec((seq, LANES), lambda b, p, s: (b, p))
    return pl.pallas_call(
        functools.partial(_dil_kernel, seq=seq),
        grid_spec=pltpu.PrefetchScalarGridSpec(
            num_scalar_prefetch=1,
            grid=(batch, DIL_WIDTH // LANES),
            in_specs=[blk, blk, blk, pl.BlockSpec((1, LANES), lambda b, p, s: (0, 0))],
            out_specs=blk,
            scratch_shapes=[pltpu.VMEM((seq, LANES), F32)] * 3
                           + [pltpu.VMEM((seq + DIL_PAD, LANES), BF16)] * 3
                           + [pltpu.VMEM((3, seq, LANES), F32)] * 2,
        ),
        out_shape=jax.ShapeDtypeStruct((batch * seq, DIL_WIDTH), BF16),
        compiler_params=pltpu.CompilerParams(
            dimension_semantics=("parallel", "parallel"), vmem_limit_bytes=VMEM_LIMIT),
        name="dilated",
    )(slopes, dq, dk, dv, g2)


def _out_proj_kernel(og_ref, od_ref, x_ref, wg_ref, wd_ref, g_ref, b_ref,
                     rwh_ref, rwl_ref, rb_ref, h_ref, hp_ref, eid_ref, gate_ref, cnt_ref):
    mix = _dot(og_ref[...], wg_ref[...]) + _dot(od_ref[...], wd_ref[...])
    h = _layer_norm(DEEPNORM_ALPHA * x_ref[...] + mix, g_ref[...], b_ref[...])
    h_ref[...] = h
    hp_ref[...] = _pack_bf16_halves(h)
    h_hi, h_lo = _split_bf16(h)
    lt = (_dot_nt(rwh_ref[...], h_hi) + _dot_nt(rwh_ref[...], h_lo) + _dot_nt(rwl_ref[...], h_hi)
          + rb_ref[...])
    tm = lt.shape[1]
    row = lax.broadcasted_iota(jnp.int32, (EXPERTS_PER_GROUP, tm), 0).astype(F32)
    neg = jnp.float32(-jnp.inf)
    big = jnp.float32(1e9)
    coarse = jnp.where(row < N_GROUPS, lt[N_EXPERTS:N_EXPERTS + EXPERTS_PER_GROUP, :], neg)
    cmax = jnp.max(coarse, axis=0, keepdims=True)
    g_idx = jnp.min(jnp.where(coarse == cmax, row, big), axis=0, keepdims=True)
    p_group = 1.0 / jnp.sum(jnp.exp(coarse - cmax), axis=0, keepdims=True)
    fine = lt[(N_GROUPS - 1) * EXPERTS_PER_GROUP:N_EXPERTS, :]
    for g in range(N_GROUPS - 2, -1, -1):
        fine = jnp.where(g_idx == g, lt[g * EXPERTS_PER_GROUP:(g + 1) * EXPERTS_PER_GROUP, :], fine)
    v1 = jnp.max(fine, axis=0, keepdims=True)
    i1 = jnp.min(jnp.where(fine == v1, row, big), axis=0, keepdims=True)
    fine2 = jnp.where(row == i1, neg, fine)
    v2 = jnp.max(fine2, axis=0, keepdims=True)
    i2 = jnp.min(jnp.where(fine2 == v2, row, big), axis=0, keepdims=True)
    e2 = jnp.exp(v2 - v1)
    den = 1.0 + e2
    gate1 = p_group * (1.0 / den)
    gate2 = p_group * (e2 / den)
    id1 = g_idx * EXPERTS_PER_GROUP + i1
    id2 = g_idx * EXPERTS_PER_GROUP + i2
    eid_ref[...] = jnp.concatenate([id1, id2], axis=0).astype(jnp.int32)
    slab = jnp.concatenate([gate1, gate2, jnp.zeros((LANES - 2, tm), F32)], axis=0)
    gate_ref[...] = slab.T[:, 0:2]
    sub = lax.broadcasted_iota(jnp.int32, (LANES, tm), 0).astype(F32)
    onehot = jnp.logical_or(sub == id1, sub == id2).astype(BF16)

    @pl.when(pl.program_id(0) == 0)
    def _():
        cnt_ref[...] = jnp.zeros_like(cnt_ref)

    cnt_ref[...] += _dot(onehot, jnp.ones((tm, LANES), BF16))


def _out_proj(og, od, x2, wg, wd, g, b, rwh, rwl, rb, tm=1024):
    T = x2.shape[0]
    row = lambda w: pl.BlockSpec((tm, w), lambda i: (i, 0))
    full = lambda a: pl.BlockSpec(a.shape, lambda i: (0,) * a.ndim)
    return pl.pallas_call(
        _out_proj_kernel,
        grid=(T // tm,),
        in_specs=[row(512), row(512), row(D_MODEL), full(wg), full(wd), full(g), full(b),
                  full(rwh), full(rwl), full(rb)],
        out_specs=[row(D_MODEL), row(D_MODEL // 2), pl.BlockSpec((2, tm), lambda i: (0, i)), row(2),
                   pl.BlockSpec((LANES, LANES), lambda i: (0, 0))],
        out_shape=[jax.ShapeDtypeStruct((T, D_MODEL), F32),
                   jax.ShapeDtypeStruct((T, D_MODEL // 2), jnp.int32),
                   jax.ShapeDtypeStruct((2, T), jnp.int32),
                   jax.ShapeDtypeStruct((T, 2), F32),
                   jax.ShapeDtypeStruct((LANES, LANES), F32)],
        compiler_params=pltpu.CompilerParams(
            dimension_semantics=("arbitrary",), vmem_limit_bytes=VMEM_LIMIT),
        name="out_proj_router",
    )(og, od, x2, wg, wd, g, b, rwh, rwl, rb)


def _positions_kernel(eid_ref, cnt_ref, dest_ref, be_ref, nv_ref, carry_ref, sp_ref, tri_ref, *, tb):
    i = pl.program_id(0)

    @pl.when(i == 0)
    def _():
        shift = int(math.log2(ROW_BLOCK))
        nb_col = (cnt_ref[...].astype(jnp.int32) + (ROW_BLOCK - 1)) >> shift
        r = lax.broadcasted_iota(jnp.int32, (LANES, LANES), 0)
        c = lax.broadcasted_iota(jnp.int32, (LANES, LANES), 1)
        nb_f = jnp.where(r < N_EXPERTS, nb_col, 0).astype(F32)
        start_col = _dot((c < r).astype(BF16), nb_f.astype(BF16))
        sp_ref[...] = start_col * float(ROW_BLOCK)
        carry_ref[...] = jnp.zeros_like(carry_ref)
        be_ref[...] = jnp.concatenate([start_col.T[0:1, :], nb_f.T[0:1, :]], axis=1).astype(jnp.int32)
        total = jnp.sum(nb_f[:, 0:1], axis=0, keepdims=True)
        nv_ref[...] = jnp.broadcast_to(total, (1, LANES)).astype(jnp.int32)
        tr = lax.broadcasted_iota(jnp.int32, (tb, tb), 0)
        tc = lax.broadcasted_iota(jnp.int32, (tb, tb), 1)
        tri_ref[...] = (tr < tc).astype(BF16)

    sub = lax.broadcasted_iota(jnp.int32, (LANES, tb), 0)
    oh1 = sub == eid_ref[0:1, :]
    oh2 = sub == eid_ref[1:2, :]
    oh = jnp.logical_or(oh1, oh2).astype(BF16)
    offset = jnp.tile(carry_ref[...] + sp_ref[...], (1, tb // LANES))
    before = _dot(oh, tri_ref[...]) + offset
    d1 = jnp.sum(jnp.where(oh1, before, 0.0), axis=0, keepdims=True)
    d2 = jnp.sum(jnp.where(oh2, before, 0.0), axis=0, keepdims=True)
    dest_ref[...] = jnp.concatenate([d1, d2], axis=0).astype(jnp.int32)
    carry_ref[...] += _dot(oh, jnp.ones((tb, LANES), BF16))


def _positions(eid_t, cnt, n_blocks_pad, tb=1024):
    T = eid_t.shape[1]
    return pl.pallas_call(
        functools.partial(_positions_kernel, tb=tb),
        grid=(T // tb,),
        in_specs=[pl.BlockSpec((2, tb), lambda i: (0, i)), pl.BlockSpec((LANES, LANES), lambda i: (0, 0))],
        out_specs=[pl.BlockSpec((2, tb), lambda i: (0, i)),
                   pl.BlockSpec((1, n_blocks_pad), lambda i: (0, 0)),
                   pl.BlockSpec((1, LANES), lambda i: (0, 0))],
        out_shape=[jax.ShapeDtypeStruct((2, T), jnp.int32),
                   jax.ShapeDtypeStruct((1, n_blocks_pad), jnp.int32),
                   jax.ShapeDtypeStruct((1, LANES), jnp.int32)],
        scratch_shapes=[pltpu.VMEM((LANES, LANES), F32), pltpu.VMEM((LANES, LANES), F32),
                        pltpu.VMEM((tb, tb), BF16)],
        compiler_params=pltpu.CompilerParams(dimension_semantics=("arbitrary",)),
        name="positions",
    )(eid_t, cnt)


def _sc_gather_rows(table, idx):
    n = idx.shape[0]
    d = table.shape[1]
    info = plsc.get_sparse_core_info()
    nc, ns = info.num_cores, info.num_subcores
    per_w = n // (nc * ns)
    assert per_w * nc * ns == n and per_w % SC_INDEX_WINDOW == 0
    mesh = plsc.VectorSubcoreMesh(core_axis_name="core", subcore_axis_name="subcore")
    nchunk = per_w // SC_GATHER_ROWS
    nbuf = SC_GATHER_BUFFERS

    @functools.partial(
        pl.kernel, out_type=jax.ShapeDtypeStruct((n, d), table.dtype), mesh=mesh,
        scratch_types=[pltpu.VMEM((per_w,), jnp.int32),
                       pltpu.VMEM((nbuf, SC_GATHER_ROWS, d), table.dtype),
                       pltpu.SemaphoreType.DMA((nbuf,)), pltpu.SemaphoreType.DMA((nbuf,))],
        name="sc_gather_rows")
    def gather(x_hbm, i_hbm, o_hbm, idx_v, buf, gsem, wsem):
        wid = lax.axis_index("subcore") * nc + lax.axis_index("core")
        base = wid * per_w
        pltpu.sync_copy(i_hbm.at[pl.ds(base, per_w)], idx_v)

        def gather_copy(c):
            rows = idx_v.at[pl.ds(c * SC_GATHER_ROWS, SC_GATHER_ROWS)]
            return pltpu.make_async_copy(x_hbm.at[rows], buf.at[c % nbuf], gsem.at[c % nbuf])

        def write_copy(c):
            dst = o_hbm.at[pl.ds(base + c * SC_GATHER_ROWS, SC_GATHER_ROWS)]
            return pltpu.make_async_copy(buf.at[c % nbuf], dst, wsem.at[c % nbuf])

        for c in range(min(nbuf - 1, nchunk)):
            gather_copy(c).start()
        for c in range(nchunk):
            gather_copy(c).wait()
            write_copy(c).start()
            if c + nbuf - 1 < nchunk:
                if c >= 1:
                    write_copy(c - 1).wait()
                gather_copy(c + nbuf - 1).start()
        for c in range(max(0, nchunk - nbuf), nchunk):
            write_copy(c).wait()

    return gather(table, idx)


def _sc_inverse_rows(dest_flat, n_rows, chunk=2048):
    n = dest_flat.shape[0]
    n_tokens = n // 2
    assert n_rows <= 3 * n_tokens
    nc = plsc.get_sparse_core_info().num_cores
    mesh = plsc.VectorSubcoreMesh(core_axis_name="core", subcore_axis_name="subcore")

    @functools.partial(
        pl.kernel, out_type=jax.ShapeDtypeStruct((n_rows,), jnp.int32), mesh=mesh,
        scratch_types=[pltpu.VMEM((n_rows,), jnp.int32), pltpu.VMEM((chunk,), jnp.int32)],
        compiler_params=pltpu.CompilerParams(needs_layout_passes=False),
        name="sc_inverse_rows")
    def inverse(d_hbm, o_hbm, inv_v, d_v):
        wid = lax.axis_index("subcore") * nc + lax.axis_index("core")

        @pl.when(wid == 0)
        def _():
            lanes = lax.iota(jnp.int32, SC_LANES)

            @pl.loop(0, n_rows // SC_LANES)
            def _(i):
                r = lanes + i * SC_LANES
                r = jnp.where(r >= n_tokens, r - n_tokens, r)
                inv_v[pl.ds(i * SC_LANES, SC_LANES)] = jnp.where(r >= n_tokens, r - n_tokens, r)

            @pl.loop(0, n // chunk)
            def _(c):
                pltpu.sync_copy(d_hbm.at[pl.ds(c * chunk, chunk)], d_v)

                @pl.loop(0, chunk // SC_LANES)
                def _(j):
                    rows = d_v[pl.ds(j * SC_LANES, SC_LANES)]
                    pair = lax.iota(jnp.int32, SC_LANES) + (c * chunk + j * SC_LANES)
                    plsc.store_scatter(inv_v, [rows], jnp.where(pair >= n_tokens, pair - n_tokens, pair))

            pltpu.sync_copy(inv_v, o_hbm)

    return inverse(dest_flat)


def _ffn_kernel(first_ref, count_ref, nv_ref, wg_hbm, wu_hbm, wd_hbm, xs_hbm, y_hbm,
                wg32, wu32, wd32, wgb, wub, wdb, xbuf, ybuf, wsem, isem, osem, *, n_blocks):
    nv = nv_ref[0]
    nbuf = FFN_BUFFERS

    def next_expert(e):
        def more(t):
            return jnp.logical_and(t < N_EXPERTS, count_ref[jnp.minimum(t, N_EXPERTS - 1)] == 0)
        return lax.while_loop(more, lambda t: t + 1, e + 1)

    def weight_copies(e, slot):
        ee = jnp.minimum(e, N_EXPERTS - 1)
        copies = []
        for i, (src, dst) in enumerate(((wg_hbm, wg32), (wu_hbm, wu32), (wd_hbm, wd32))):
            rows = src.shape[1] // FFN_WEIGHT_CHUNKS
            for c in range(FFN_WEIGHT_CHUNKS):
                part = pl.ds(c * rows, rows)
                copies.append(pltpu.make_async_copy(src.at[ee, part], dst.at[slot, part],
                                                    wsem.at[slot, i * FFN_WEIGHT_CHUNKS + c]))
        return copies

    def fetch_weights(e, slot):
        @pl.when(e < N_EXPERTS)
        def _():
            for c in weight_copies(e, slot):
                c.start()

    def take_weights(e, slot):
        for c in weight_copies(e, slot):
            c.wait()
        wgb[...] = wg32[slot].astype(BF16)
        wub[...] = wu32[slot].astype(BF16)
        wdb[...] = wd32[slot].astype(BF16)
        fetch_weights(next_expert(next_expert(e)), slot)

    def rows_of(b):
        return pl.ds(pl.multiple_of(b * ROW_BLOCK, ROW_BLOCK), ROW_BLOCK)

    def in_copy(b):
        return pltpu.make_async_copy(xs_hbm.at[rows_of(b)], xbuf.at[b % nbuf], isem.at[b % nbuf])

    def out_copy(b):
        return pltpu.make_async_copy(ybuf.at[b % nbuf], y_hbm.at[rows_of(b)], osem.at[b % nbuf])

    @pl.when(nv > 0)
    def _():
        e0 = next_expert(jnp.int32(-1))
        for i in range(nbuf - 1):
            @pl.when(i < nv)
            def _():
                in_copy(i).start()
        fetch_weights(e0, 0)
        fetch_weights(next_expert(e0), 1)
        take_weights(e0, 0)

        def body(b, carry):
            e, k = carry
            switch = b >= first_ref[e] + count_ref[e]
            e_new = jnp.where(switch, next_expert(e), e)
            k_new = jnp.where(switch, k + 1, k)

            @pl.when(switch)
            def _():
                take_weights(e_new, k_new % 2)

            in_copy(b).wait()

            @pl.when(b + nbuf - 1 < nv)
            def _():
                in_copy(b + nbuf - 1).start()

            @pl.when(b >= nbuf)
            def _():
                out_copy(b - nbuf).wait()

            x_hi, x_lo = _unpack_bf16_halves(xbuf[b % nbuf])
            xb = jnp.concatenate([x_hi.astype(BF16), x_lo.astype(BF16)], axis=1)
            a = _dot(xb, wgb[...])
            u = _dot(xb, wub[...])
            hid = (a * jax.nn.sigmoid(a) * u).astype(BF16)
            ybuf[b % nbuf] = _pack_bf16_halves(_dot(hid, wdb[...]))
            out_copy(b).start()
            return e_new, k_new

        lax.fori_loop(0, nv, body, (e0, jnp.int32(0)))

        for i in range(nbuf):
            @pl.when(nv > i)
            def _():
                out_copy(nv - 1 - i).wait()

    ybuf[0] = jnp.zeros((ROW_BLOCK, D_MODEL // 2), jnp.int32)

    def fill(b, carry):
        pltpu.sync_copy(ybuf.at[0], y_hbm.at[rows_of(b)])
        return carry

    lax.fori_loop(nv, n_blocks, fill, 0)


def _ffn(first_blk, n_blk, nv, xs, w_gate, w_up, w_down):
    n_rows = xs.shape[0]
    n_blocks = n_rows // ROW_BLOCK
    anyspec = pl.BlockSpec(memory_space=pl.ANY)
    return pl.pallas_call(
        functools.partial(_ffn_kernel, n_blocks=n_blocks),
        grid_spec=pltpu.PrefetchScalarGridSpec(
            num_scalar_prefetch=3,
            grid=(1,),
            in_specs=[anyspec, anyspec, anyspec, anyspec],
            out_specs=anyspec,
            scratch_shapes=[pltpu.VMEM((2, D_MODEL, D_FF), F32), pltpu.VMEM((2, D_MODEL, D_FF), F32),
                            pltpu.VMEM((2, D_FF, D_MODEL), F32),
                            pltpu.VMEM((D_MODEL, D_FF), BF16), pltpu.VMEM((D_MODEL, D_FF), BF16),
                            pltpu.VMEM((D_FF, D_MODEL), BF16),
                            pltpu.VMEM((FFN_BUFFERS, ROW_BLOCK, D_MODEL // 2), jnp.int32),
                            pltpu.VMEM((FFN_BUFFERS, ROW_BLOCK, D_MODEL // 2), jnp.int32),
                            pltpu.SemaphoreType.DMA((2, 3 * FFN_WEIGHT_CHUNKS)),
                            pltpu.SemaphoreType.DMA((FFN_BUFFERS,)),
                            pltpu.SemaphoreType.DMA((FFN_BUFFERS,))],
        ),
        out_shape=jax.ShapeDtypeStruct((n_rows, D_MODEL // 2), jnp.int32),
        compiler_params=pltpu.CompilerParams(
            dimension_semantics=("arbitrary",), vmem_limit_bytes=VMEM_LIMIT),
        name="expert_ffn",
    )(first_blk, n_blk, nv, w_gate, w_up, w_down, xs)


def _combine_kernel(h_ref, ya_ref, yb_ref, gate_ref, g_ref, b_ref, o_ref):
    gate = gate_ref[...]
    a_hi, a_lo = _unpack_bf16_halves(ya_ref[...])
    b_hi, b_lo = _unpack_bf16_halves(yb_ref[...])
    g0, g1 = gate[:, 0:1], gate[:, 1:2]
    ffn = jnp.concatenate([a_hi * g0 + b_hi * g1, a_lo * g0 + b_lo * g1], axis=1)
    o_ref[...] = _layer_norm(DEEPNORM_ALPHA * h_ref[...] + ffn, g_ref[...], b_ref[...])


def _combine(h, y2, gate, g, b, tm=512):
    T = h.shape[0]
    nt = T // tm
    return pl.pallas_call(
        _combine_kernel,
        grid=(nt,),
        in_specs=[pl.BlockSpec((tm, D_MODEL), lambda i: (i, 0)),
                  pl.BlockSpec((tm, D_MODEL // 2), lambda i: (i, 0)),
                  pl.BlockSpec((tm, D_MODEL // 2), lambda i: (i + nt, 0)),
                  pl.BlockSpec((tm, 2), lambda i: (i, 0)),
                  pl.BlockSpec((1, D_MODEL), lambda i: (0, 0)),
                  pl.BlockSpec((1, D_MODEL), lambda i: (0, 0))],
        out_specs=pl.BlockSpec((tm, D_MODEL), lambda i: (i, 0)),
        out_shape=jax.ShapeDtypeStruct((T, D_MODEL), F32),
        compiler_params=pltpu.CompilerParams(
            dimension_semantics=("parallel",), vmem_limit_bytes=VMEM_LIMIT),
        name="combine",
    )(h, y2, y2, gate, g, b)


def kernel(x, w_in, gla_gate_w2, gla_gate_b, gla_norm_g, dil_norm_g, w_out, ln1_g, ln1_b,
           router_coarse_w, router_coarse_b, router_fine_w, router_fine_b,
           expert_w_gate, expert_w_up, expert_w_down, ln2_g, ln2_b):
    B, S, D = x.shape
    T = B * S
    depth = w_in.shape[0]
    slopes = jnp.exp2(-8.0 * jnp.arange(1, DIL_HEADS + 1, dtype=F32) / DIL_HEADS)
    n_rows = 2 * T + N_EXPERTS * ROW_BLOCK
    n_blocks = n_rows // ROW_BLOCK
    n_blocks_pad = -(-n_blocks // (2 * LANES)) * (2 * LANES)
    assert n_blocks_pad == 2 * LANES
    a0 = 1536
    h = x.reshape(T, D)
    for l in range(depth):
        w = w_in[l]
        wm = jnp.concatenate([w[:, :a0], w[:, a0 + GLA_GATE_RANK:]], axis=1).astype(BF16)
        wa = jnp.pad(w[:, a0:a0 + GLA_GATE_RANK], ((0, 0), (0, LANES - GLA_GATE_RANK))).astype(BF16)
        w2 = jnp.pad(gla_gate_w2[l], ((0, LANES - GLA_GATE_RANK), (0, 0)))
        w2h, w2l = _split_bf16(w2)
        q, k, v, r, la, dq, dk, dv = _in_proj(h, wm, wa, w2h, w2l, gla_gate_b[l][None, :])
        o_gla = _gla(q, k, v, r, la, gla_norm_g[l][None, :], B, S)
        g2 = jnp.tile(dil_norm_g[l], 2)[None, :]
        o_dil = _dilated(slopes, dq, dk, dv, g2, B, S)
        wo = w_out[l].astype(BF16)
        rw = jnp.concatenate([router_fine_w[l].reshape(D, N_EXPERTS), router_coarse_w[l]], axis=1)
        rw = jnp.pad(rw, ((0, 0), (0, LANES - N_EXPERTS - N_GROUPS))).T
        rwh, rwl = _split_bf16(rw)
        rb = jnp.concatenate([router_fine_b[l].reshape(N_EXPERTS), router_coarse_b[l]])
        rb = jnp.pad(rb, (0, LANES - N_EXPERTS - N_GROUPS))[:, None]
        h1, h1p, eid_t, gate, cnt = _out_proj(o_gla, o_dil, h, wo[:GLA_WIDTH], wo[GLA_WIDTH:],
                                              ln1_g[l][None, :], ln1_b[l][None, :], rwh, rwl, rb)
        dest_t, be, nv = _positions(eid_t, cnt, n_blocks_pad)
        dest_flat = dest_t.reshape(2 * T)
        src_tok = _sc_inverse_rows(dest_flat, n_rows)
        xs = _sc_gather_rows(h1p, src_tok)
        be = be.reshape(n_blocks_pad)
        y = _ffn(be[:N_EXPERTS], be[LANES:LANES + N_EXPERTS], nv.reshape(LANES)[:1], xs,
                 expert_w_gate[l], expert_w_up[l], expert_w_down[l])
        y2 = _sc_gather_rows(y, dest_flat)
        h = _combine(h1, y2, gate, ln2_g[l][None, :], ln2_b[l][None, :])
    return h.reshape(B, S, D)
```

```python
import functools
import math

import jax
import jax.numpy as jnp
import numpy as np
from jax import lax
from jax.experimental import pallas as pl
from jax.experimental.pallas import tpu as pltpu
from jax.experimental.pallas import tpu_sc as plsc

D_MODEL = 1024
GLA_HEADS = 4
GLA_DK = 64
GLA_DV = 128
GLA_KEY_WIDTH = GLA_HEADS * GLA_DK
GLA_WIDTH = GLA_HEADS * GLA_DV
GLA_GATE_RANK = 16
GLA_GATE_TEMP = 16.0
DIL_HEADS = 8
DIL_DH = 64
DIL_WIDTH = DIL_HEADS * DIL_DH
DIL_CONFIGS = ((128, 1), (512, 4), (2048, 16))
DIL_BLOCK = 128
DIL_MAX_R = max(r for _, r in DIL_CONFIGS)
DIL_PAD = DIL_BLOCK * DIL_MAX_R
DIL_UNROLL = 16
N_GROUPS = 4
EXPERTS_PER_GROUP = 8
N_EXPERTS = N_GROUPS * EXPERTS_PER_GROUP
D_FF = 512
DEEPNORM_ALPHA = 2.0 ** 0.25
EPS = 1e-5
LOG2E = math.log2(math.e)
LN2 = math.log(2.0)

LANES = 128
GLA_CHUNK = 128
GLA_SUB = 64
GLA_UNROLL = 4
SC_LANES = 16
SC_INDEX_WINDOW = 128
SC_GATHER_BUFFERS = 6
SC_GATHER_ROWS = 32
FFN_BUFFERS = 8
FFN_WEIGHT_CHUNKS = 4
ROW_BLOCK = 256
VMEM_LIMIT = 56 * 1024 * 1024

F32 = jnp.float32
BF16 = jnp.bfloat16


def _dot(a, b):
    return jnp.dot(a, b, preferred_element_type=F32)


def _dot_nt(a, b):
    return lax.dot_general(a, b, (((1,), (1,)), ((), ())), preferred_element_type=F32)


def _dot_tn(a, b):
    return lax.dot_general(a, b, (((0,), (0,)), ((), ())), preferred_element_type=F32)


def _split_bf16(v):
    hi = v.astype(BF16)
    lo = (v - hi.astype(F32)).astype(BF16)
    return hi, lo


def _pack_bf16_halves(v):
    w = v.shape[1] // 2
    hi = lax.bitcast_convert_type(v[:, :w].astype(BF16).astype(F32), jnp.int32)
    lo = lax.bitcast_convert_type(v[:, w:].astype(BF16).astype(F32), jnp.int32)
    return hi | lax.shift_right_logical(lo, 16)


def _unpack_bf16_halves(words):
    hi = lax.bitcast_convert_type(words & jnp.int32(-65536), F32)
    lo = lax.bitcast_convert_type(lax.shift_left(words, 16), F32)
    return hi, lo


def _layer_norm(v, g, b):
    mu = jnp.mean(v, axis=-1, keepdims=True)
    c = v - mu
    var = jnp.mean(c * c, axis=-1, keepdims=True)
    return c * lax.rsqrt(var + EPS) * g + b


def _in_proj_kernel(x_ref, wm_ref, wa_ref, w2h_ref, w2l_ref, gb_ref,
                    q_ref, k_ref, v_ref, r_ref, la_ref, dq_ref, dk_ref, dv_ref):
    xb = x_ref[...].astype(BF16)

    def piece(c0, c1):
        return _dot(xb, wm_ref[:, c0:c1])

    q_ref[...] = (piece(0, 256) * (GLA_DK ** -0.5)).astype(BF16)
    k_ref[...] = piece(256, 512).astype(BF16)
    v_ref[...] = piece(512, 1024).astype(BF16)
    r_ref[...] = piece(1024, 1536).astype(BF16)
    dq_ref[...] = (piece(1536, 2048) * (DIL_DH ** -0.5 * LOG2E)).astype(BF16)
    dk_ref[...] = piece(2048, 2560).astype(BF16)
    dv_ref[...] = piece(2560, 3072).astype(BF16)
    ga = _dot(xb, wa_ref[...])
    ga_hi, ga_lo = _split_bf16(ga)
    z = _dot(ga_hi, w2h_ref[...]) + _dot(ga_lo, w2h_ref[...]) + _dot(ga_hi, w2l_ref[...]) + gb_ref[...]
    log_sig = jnp.minimum(z, 0.0) - jnp.log1p(jnp.exp(-jnp.abs(z)))
    la_ref[...] = log_sig * (1.0 / GLA_GATE_TEMP)


def _in_proj(x2, wm, wa, w2h, w2l, gb, tm=1024):
    T = x2.shape[0]
    row = lambda w: pl.BlockSpec((tm, w), lambda i: (i, 0))
    full = lambda a: pl.BlockSpec(a.shape, lambda i: (0,) * a.ndim)
    outs = [(256, BF16), (256, BF16), (512, BF16), (512, BF16), (256, F32),
            (512, BF16), (512, BF16), (512, BF16)]
    return pl.pallas_call(
        _in_proj_kernel,
        grid=(T // tm,),
        in_specs=[row(D_MODEL), full(wm), full(wa), full(w2h), full(w2l), full(gb)],
        out_specs=[row(w) for w, _ in outs],
        out_shape=[jax.ShapeDtypeStruct((T, w), dt) for w, dt in outs],
        compiler_params=pltpu.CompilerParams(
            dimension_semantics=("parallel",), vmem_limit_bytes=VMEM_LIMIT),
        name="in_proj",
    )(x2, wm, wa, w2h, w2l, gb)


def _gla_kernel(q_ref, k_ref, v_ref, r_ref, la_ref, g_ref, o_ref, s_ref, *, seq_block):
    C = GLA_CHUNK
    H = GLA_SUB
    assert C == 2 * H

    @pl.when(pl.program_id(1) == 0)
    def _():
        s_ref[...] = jnp.zeros_like(s_ref)

    ri = lax.broadcasted_iota(jnp.int32, (C, C), 0)
    ci = lax.broadcasted_iota(jnp.int32, (C, C), 1)
    same_sub = (ri // H) == (ci // H)
    sum_ops = jnp.concatenate([jnp.logical_and(same_sub, ci <= ri).astype(BF16), same_sub.astype(BF16),
                               jnp.ones((C, C), BF16)], axis=0)
    diag_mask = jnp.logical_and(same_sub, ci <= ri)
    off_mask = (ri // H) > (ci // H)
    second = lax.broadcasted_iota(jnp.int32, (C, 1), 0) >= H
    ones_cl = jnp.ones((C, LANES), BF16)
    lane_k = lax.broadcasted_iota(jnp.int32, (1, GLA_KEY_WIDTH), 1) // GLA_DK
    head_masks = [(lane_k == h).astype(F32) for h in range(GLA_HEADS)]
    srow = lax.broadcasted_iota(jnp.int32, (GLA_KEY_WIDTH, GLA_WIDTH), 0) // GLA_DK
    scol = lax.broadcasted_iota(jnp.int32, (GLA_KEY_WIDTH, GLA_WIDTH), 1) // GLA_DV
    state_mask = (srow == scol).astype(F32)
    g = g_ref[...]

    def trip(t, carry):
        U = GLA_UNROLL
        rows = [pl.ds(pl.multiple_of((t * U + u) * C, C), C) for u in range(U)]
        la2s, sums = [], []
        for u in range(U):
            la_hi, la_lo = _split_bf16(la_ref[rows[u], :])
            la2 = jnp.concatenate([la_hi, la_lo], axis=1)
            la2s.append(la2)
            sm = _dot(sum_ops, la2)
            sums.append(sm[:, 0:GLA_KEY_WIDTH] + sm[:, GLA_KEY_WIDTH:])
        q_states, k_states, scores = [], [], []
        for u in range(U):
            b = sums[u][0:C]
            t_sub = sums[u][C:2 * C]
            other = sums[u][2 * C:3 * C] - t_sub
            q = q_ref[rows[u], :].astype(F32)
            k = k_ref[rows[u], :].astype(F32)
            qd = q * jnp.exp(b)
            kd = (k * jnp.exp(-b)).astype(BF16)
            ke_f = k * jnp.exp(t_sub - b)
            cross = jnp.exp(other)
            q_states.append((qd * jnp.where(second, cross, 1.0)).astype(BF16))
            k_states.append((ke_f * jnp.where(second, 1.0, cross)).astype(BF16))
            q_heads = jnp.concatenate([(qd * head_masks[h]).astype(BF16) for h in range(GLA_HEADS)], axis=0)
            keys2 = jnp.concatenate([kd, ke_f.astype(BF16)], axis=0)
            scores.append(_dot_nt(q_heads, keys2))
        decs, upds = [], []
        for u in range(U):
            tot = _dot_tn(la2s[u], ones_cl)
            decs.append(jnp.exp(tot[0:GLA_KEY_WIDTH] + tot[GLA_KEY_WIDTH:]))
            upds.append(_dot_tn(k_states[u], v_ref[rows[u], :]) * state_mask)
        o_inters = []
        for u in range(U):
            state = s_ref[...]
            o_inters.append(_dot(q_states[u], state.astype(BF16)))
            for h in range(GLA_HEADS):
                cols = slice(h * GLA_DV, (h + 1) * GLA_DV)
                s_ref[:, cols] = state[:, cols] * decs[u] + upds[u][:, cols]
        for u in range(U):
            v = v_ref[rows[u], :]
            outs = []
            for h in range(GLA_HEADS):
                sh = scores[u][h * C:(h + 1) * C]
                a = jnp.where(diag_mask, sh[:, 0:C], 0.0) + jnp.where(off_mask, sh[:, C:2 * C], 0.0)
                cols = slice(h * GLA_DV, (h + 1) * GLA_DV)
                o = _dot(a.astype(BF16), v[:, cols]) + o_inters[u][:, cols]
                o = o * lax.rsqrt(jnp.mean(o * o, axis=-1, keepdims=True) + EPS) * g
                outs.append(o)
            o_all = jnp.concatenate(outs, axis=-1)
            rr = r_ref[rows[u], :].astype(F32)
            o_ref[rows[u], :] = (o_all * (rr * jax.nn.sigmoid(rr))).astype(BF16)
        return carry

    lax.fori_loop(0, seq_block // C // GLA_UNROLL, trip, 0)


def _gla(q, k, v, r, la, g, batch, seq, seq_block=1024):
    nsb = seq // seq_block
    row = lambda w: pl.BlockSpec((seq_block, w), lambda b, s: (b * nsb + s, 0))
    return pl.pallas_call(
        functools.partial(_gla_kernel, seq_block=seq_block),
        grid=(batch, nsb),
        in_specs=[row(256), row(256), row(512), row(512), row(256),
                  pl.BlockSpec((1, GLA_DV), lambda b, s: (0, 0))],
        out_specs=row(512),
        out_shape=jax.ShapeDtypeStruct((batch * seq, GLA_WIDTH), BF16),
        scratch_shapes=[pltpu.VMEM((GLA_KEY_WIDTH, GLA_WIDTH), F32)],
        compiler_params=pltpu.CompilerParams(
            dimension_semantics=("parallel", "arbitrary"), vmem_limit_bytes=VMEM_LIMIT),
        name="gla",
    )(q, k, v, r, la, g)


def _dil_kernel(slope_ref, q_ref, k_ref, v_ref, g_ref, o_ref,
                qf, kf, vf, kd, va, vb, oc, lc, *, seq):
    B = DIL_BLOCK
    U = DIL_UNROLL
    pair = pl.program_id(1)
    qf[...] = q_ref[...].astype(F32)
    kf[...] = k_ref[...].astype(F32)
    vf[...] = v_ref[...].astype(F32)

    lane = lax.broadcasted_iota(jnp.int32, (1, LANES), 1)
    first = lane < DIL_DH
    ii = lax.broadcasted_iota(jnp.int32, (B, B), 0)
    jj = lax.broadcasted_iota(jnp.int32, (B, B), 1)
    upper = jj > ii
    eye = jj == ii
    dist = jnp.bitwise_and(ii - jj, B - 1).astype(F32)
    neg = jnp.float32(-jnp.inf)
    neg_tile = jnp.full((B, B), neg, F32)
    zero_tile = jnp.zeros((B, LANES), BF16)

    for cfg, (window, r) in enumerate(DIL_CONFIGS):
        nb = seq // r // B
        cs = nb + 1
        bias_prev, bias_cur = [], []
        for hh in range(2):
            slope = slope_ref[2 * pair + hh] * (float(r) * LOG2E)
            bias = dist * (-slope)
            bias_prev.append(jnp.where(upper, bias, jnp.where(eye, -slope * float(B), neg)))
            bias_cur.append(jnp.where(upper, neg, bias))

        for c in range(r):
            rows0 = slice(c * cs * B, (c * cs + 1) * B)
            kd[rows0, :] = zero_tile
            va[rows0, :] = zero_tile
            vb[rows0, :] = zero_tile

        def prep(t4, carry, r=r, nb=nb, cs=cs):
            for j in range(4):
                t = t4 * 4 + j
                c = t // nb
                n = t % nb
                start = c + n * (B * r)
                rows = pl.ds(start, B, stride=r) if r > 1 else pl.ds(pl.multiple_of(start, B), B)
                dst = pl.ds(pl.multiple_of((c * cs + 1 + n) * B, B), B)
                kd[dst, :] = kf[rows, :].astype(BF16)
                v = vf[rows, :]
                va[dst, :] = jnp.where(first, v, 1.0).astype(BF16)
                vb[dst, :] = jnp.where(first, 1.0, v).astype(BF16)
            return carry

        lax.fori_loop(0, seq // B // 4, prep, 0)

        def geom(it, r=r, nb=nb, cs=cs):
            c = it // nb
            n = it % nb
            start = c + n * (B * r)
            rows = pl.ds(start, B, stride=r) if r > 1 else pl.ds(pl.multiple_of(start, B), B)
            kv = pl.ds(pl.multiple_of((c * cs + n) * B, B), 2 * B)
            return rows, kv

        def body(t, carry, cfg=cfg, nb=nb, bias_prev=bias_prev, bias_cur=bias_cur, geom=geom):
            geo = [geom(t * U + u) for u in range(U)]
            scores = []
            for u in range(U):
                rows, kv = geo[u]
                q = qf[rows, :]
                q_heads = jnp.concatenate([jnp.where(first, q, 0.0), jnp.where(first, 0.0, q)], axis=0)
                scores.append(_dot_nt(q_heads.astype(BF16), kd[kv, :]))
            probs, maxes = [], []
            for u in range(U):
                if nb % U == 0:
                    has_prev = True if u > 0 else (t * U) % nb > 0
                else:
                    assert U % nb == 0
                    has_prev = (u % nb) > 0
                for hh in range(2):
                    s2 = scores[u][hh * B:(hh + 1) * B]
                    if has_prev is True:
                        bp = bias_prev[hh]
                    elif has_prev is False:
                        bp = neg_tile
                    else:
                        bp = jnp.where(has_prev, bias_prev[hh], neg)
                    s_prev = s2[:, 0:B] + bp
                    s_cur = s2[:, B:2 * B] + bias_cur[hh]
                    m = jnp.max(jnp.maximum(s_prev, s_cur), axis=-1, keepdims=True)
                    probs.append(jnp.concatenate([jnp.exp2(s_prev - m), jnp.exp2(s_cur - m)], axis=1).astype(BF16))
                    maxes.append(m)
            for u in range(U):
                rows, kv = geo[u]
                acc0 = _dot(probs[2 * u], va[kv, :])
                acc1 = _dot(probs[2 * u + 1], vb[kv, :])
                num = jnp.where(first, acc0, acc1)
                den = pltpu.roll(jnp.where(first, acc1, acc0), DIL_DH, axis=1)
                oc[cfg, rows, :] = num * (1.0 / den)
                lc[cfg, rows, :] = (jnp.where(first, maxes[2 * u], maxes[2 * u + 1]) + jnp.log2(den)) * LN2
            return carry

        lax.fori_loop(0, seq // B // U, body, 0)

    g = g_ref[...]
    CH = 512

    def mix(i, carry):
        rows = pl.ds(pl.multiple_of(i * CH, CH), CH)
        l0, l1, l2 = lc[0, rows, :], lc[1, rows, :], lc[2, rows, :]
        m = jnp.maximum(jnp.maximum(l0, l1), l2)
        e0, e1, e2 = jnp.exp(l0 - m), jnp.exp(l1 - m), jnp.exp(l2 - m)
        den = e0 + e1 + e2
        o = (e0 / den) * oc[0, rows, :] + (e1 / den) * oc[1, rows, :] + (e2 / den) * oc[2, rows, :]
        sq = o * o
        ms_a = jnp.sum(jnp.where(first, sq, 0.0), axis=-1, keepdims=True) * (1.0 / DIL_DH)
        ms_b = jnp.sum(jnp.where(first, 0.0, sq), axis=-1, keepdims=True) * (1.0 / DIL_DH)
        ms = jnp.where(first, ms_a, ms_b)
        o_ref[rows, :] = (o * lax.rsqrt(ms + EPS) * g).astype(BF16)
        return carry

    lax.fori_loop(0, seq // CH, mix, 0)


def _dilated(slopes, dq, dk, dv, g2, batch, seq):
    blk = pl.BlockSpec((seq, LANES), lambda b, p, s: (b, p))
    return pl.pallas_call(
        functools.partial(_dil_kernel, seq=seq),
        grid_spec=pltpu.PrefetchScalarGridSpec(
            num_scalar_prefetch=1,
            grid=(batch, DIL_WIDTH // LANES),
            in_specs=[blk, blk, blk, pl.BlockSpec((1, LANES), lambda b, p, s: (0, 0))],
            out_specs=blk,
            scratch_shapes=[pltpu.VMEM((seq, LANES), F32)] * 3
                           + [pltpu.VMEM((seq + DIL_PAD, LANES), BF16)] * 3
                           + [pltpu.VMEM((3, seq, LANES), F32)] * 2,
        ),
        out_shape=jax.ShapeDtypeStruct((batch * seq, DIL_WIDTH), BF16),
        compiler_params=pltpu.CompilerParams(
            dimension_semantics=("parallel", "parallel"), vmem_limit_bytes=VMEM_LIMIT),
        name="dilated",
    )(slopes, dq, dk, dv, g2)


def _out_proj_kernel(og_ref, od_ref, x_ref, wg_ref, wd_ref, g_ref, b_ref,
                     rwh_ref, rwl_ref, rb_ref, h_ref, hp_ref, eid_ref, gate_ref, cnt_ref):
    mix = _dot(og_ref[...], wg_ref[...]) + _dot(od_ref[...], wd_ref[...])
    h = _layer_norm(DEEPNORM_ALPHA * x_ref[...] + mix, g_ref[...], b_ref[...])
    h_ref[...] = h
    hp_ref[...] = _pack_bf16_halves(h)
    h_hi, h_lo = _split_bf16(h)
    lt = (_dot_nt(rwh_ref[...], h_hi) + _dot_nt(rwh_ref[...], h_lo) + _dot_nt(rwl_ref[...], h_hi)
          + rb_ref[...])
    tm = lt.shape[1]
    row = lax.broadcasted_iota(jnp.int32, (EXPERTS_PER_GROUP, tm), 0).astype(F32)
    neg = jnp.float32(-jnp.inf)
    big = jnp.float32(1e9)
    coarse = jnp.where(row < N_GROUPS, lt[N_EXPERTS:N_EXPERTS + EXPERTS_PER_GROUP, :], neg)
    cmax = jnp.max(coarse, axis=0, keepdims=True)
    g_idx = jnp.min(jnp.where(coarse == cmax, row, big), axis=0, keepdims=True)
    p_group = 1.0 / jnp.sum(jnp.exp(coarse - cmax), axis=0, keepdims=True)
    fine = lt[(N_GROUPS - 1) * EXPERTS_PER_GROUP:N_EXPERTS, :]
    for g in range(N_GROUPS - 2, -1, -1):
        fine = jnp.where(g_idx == g, lt[g * EXPERTS_PER_GROUP:(g + 1) * EXPERTS_PER_GROUP, :], fine)
    v1 = jnp.max(fine, axis=0, keepdims=True)
    i1 = jnp.min(jnp.where(fine == v1, row, big), axis=0, keepdims=True)
    fine2 = jnp.where(row == i1, neg, fine)
    v2 = jnp.max(fine2, axis=0, keepdims=True)
    i2 = jnp.min(jnp.where(fine2 == v2, row, big), axis=0, keepdims=True)
    e2 = jnp.exp(v2 - v1)
    den = 1.0 + e2
    gate1 = p_group * (1.0 / den)
    gate2 = p_group * (e2 / den)
    id1 = g_idx * EXPERTS_PER_GROUP + i1
    id2 = g_idx * EXPERTS_PER_GROUP + i2
    eid_ref[...] = jnp.concatenate([id1, id2], axis=0).astype(jnp.int32)
    slab = jnp.concatenate([gate1, gate2, jnp.zeros((LANES - 2, tm), F32)], axis=0)
    gate_ref[...] = slab.T[:, 0:2]
    sub = lax.broadcasted_iota(jnp.int32, (LANES, tm), 0).astype(F32)
    onehot = jnp.logical_or(sub == id1, sub == id2).astype(BF16)

    @pl.when(pl.program_id(0) == 0)
    def _():
        cnt_ref[...] = jnp.zeros_like(cnt_ref)

    cnt_ref[...] += _dot(onehot, jnp.ones((tm, LANES), BF16))


def _out_proj(og, od, x2, wg, wd, g, b, rwh, rwl, rb, tm=1024):
    T = x2.shape[0]
    row = lambda w: pl.BlockSpec((tm, w), lambda i: (i, 0))
    full = lambda a: pl.BlockSpec(a.shape, lambda i: (0,) * a.ndim)
    return pl.pallas_call(
        _out_proj_kernel,
        grid=(T // tm,),
        in_specs=[row(512), row(512), row(D_MODEL), full(wg), full(wd), full(g), full(b),
                  full(rwh), full(rwl), full(rb)],
        out_specs=[row(D_MODEL), row(D_MODEL // 2), pl.BlockSpec((2, tm), lambda i: (0, i)), row(2),
                   pl.BlockSpec((LANES, LANES), lambda i: (0, 0))],
        out_shape=[jax.ShapeDtypeStruct((T, D_MODEL), F32),
                   jax.ShapeDtypeStruct((T, D_MODEL // 2), jnp.int32),
                   jax.ShapeDtypeStruct((2, T), jnp.int32),
                   jax.ShapeDtypeStruct((T, 2), F32),
                   jax.ShapeDtypeStruct((LANES, LANES), F32)],
        compiler_params=pltpu.CompilerParams(
            dimension_semantics=("arbitrary",), vmem_limit_bytes=VMEM_LIMIT),
        name="out_proj_router",
    )(og, od, x2, wg, wd, g, b, rwh, rwl, rb)


def _positions_kernel(eid_ref, cnt_ref, dest_ref, be_ref, nv_ref, carry_ref, sp_ref, tri_ref, *, tb):
    i = pl.program_id(0)

    @pl.when(i == 0)
    def _():
        shift = int(math.log2(ROW_BLOCK))
        nb_col = (cnt_ref[...].astype(jnp.int32) + (ROW_BLOCK - 1)) >> shift
        r = lax.broadcasted_iota(jnp.int32, (LANES, LANES), 0)
        c = lax.broadcasted_iota(jnp.int32, (LANES, LANES), 1)
        nb_f = jnp.where(r < N_EXPERTS, nb_col, 0).astype(F32)
        start_col = _dot((c < r).astype(BF16), nb_f.astype(BF16))
        sp_ref[...] = start_col * float(ROW_BLOCK)
        carry_ref[...] = jnp.zeros_like(carry_ref)
        be_ref[...] = jnp.concatenate([start_col.T[0:1, :], nb_f.T[0:1, :]], axis=1).astype(jnp.int32)
        total = jnp.sum(nb_f[:, 0:1], axis=0, keepdims=True)
        nv_ref[...] = jnp.broadcast_to(total, (1, LANES)).astype(jnp.int32)
        tr = lax.broadcasted_iota(jnp.int32, (tb, tb), 0)
        tc = lax.broadcasted_iota(jnp.int32, (tb, tb), 1)
        tri_ref[...] = (tr < tc).astype(BF16)

    sub = lax.broadcasted_iota(jnp.int32, (LANES, tb), 0)
    oh1 = sub == eid_ref[0:1, :]
    oh2 = sub == eid_ref[1:2, :]
    oh = jnp.logical_or(oh1, oh2).astype(BF16)
    offset = jnp.tile(carry_ref[...] + sp_ref[...], (1, tb // LANES))
    before = _dot(oh, tri_ref[...]) + offset
    d1 = jnp.sum(jnp.where(oh1, before, 0.0), axis=0, keepdims=True)
    d2 = jnp.sum(jnp.where(oh2, before, 0.0), axis=0, keepdims=True)
    dest_ref[...] = jnp.concatenate([d1, d2], axis=0).astype(jnp.int32)
    carry_ref[...] += _dot(oh, jnp.ones((tb, LANES), BF16))


def _positions(eid_t, cnt, n_blocks_pad, tb=1024):
    T = eid_t.shape[1]
    return pl.pallas_call(
        functools.partial(_positions_kernel, tb=tb),
        grid=(T // tb,),
        in_specs=[pl.BlockSpec((2, tb), lambda i: (0, i)), pl.BlockSpec((LANES, LANES), lambda i: (0, 0))],
        out_specs=[pl.BlockSpec((2, tb), lambda i: (0, i)),
                   pl.BlockSpec((1, n_blocks_pad), lambda i: (0, 0)),
                   pl.BlockSpec((1, LANES), lambda i: (0, 0))],
        out_shape=[jax.ShapeDtypeStruct((2, T), jnp.int32),
                   jax.ShapeDtypeStruct((1, n_blocks_pad), jnp.int32),
                   jax.ShapeDtypeStruct((1, LANES), jnp.int32)],
        scratch_shapes=[pltpu.VMEM((LANES, LANES), F32), pltpu.VMEM((LANES, LANES), F32),
                        pltpu.VMEM((tb, tb), BF16)],
        compiler_params=pltpu.CompilerParams(dimension_semantics=("arbitrary",)),
        name="positions",
    )(eid_t, cnt)


def _sc_gather_rows(table, idx):
    n = idx.shape[0]
    d = table.shape[1]
    info = plsc.get_sparse_core_info()
    nc, ns = info.num_cores, info.num_subcores
    per_w = n // (nc * ns)
    assert per_w * nc * ns == n and per_w % SC_INDEX_WINDOW == 0
    mesh = plsc.VectorSubcoreMesh(core_axis_name="core", subcore_axis_name="subcore")
    nchunk = per_w // SC_GATHER_ROWS
    nbuf = SC_GATHER_BUFFERS

    @functools.partial(
        pl.kernel, out_type=jax.ShapeDtypeStruct((n, d), table.dtype), mesh=mesh,
        scratch_types=[pltpu.VMEM((per_w,), jnp.int32),
                       pltpu.VMEM((nbuf, SC_GATHER_ROWS, d), table.dtype),
                       pltpu.SemaphoreType.DMA((nbuf,)), pltpu.SemaphoreType.DMA((nbuf,))],
        name="sc_gather_rows")
    def gather(x_hbm, i_hbm, o_hbm, idx_v, buf, gsem, wsem):
        wid = lax.axis_index("subcore") * nc + lax.axis_index("core")
        base = wid * per_w
        pltpu.sync_copy(i_hbm.at[pl.ds(base, per_w)], idx_v)

        def gather_copy(c):
            rows = idx_v.at[pl.ds(c * SC_GATHER_ROWS, SC_GATHER_ROWS)]
            return pltpu.make_async_copy(x_hbm.at[rows], buf.at[c % nbuf], gsem.at[c % nbuf])

        def write_copy(c):
            dst = o_hbm.at[pl.ds(base + c * SC_GATHER_ROWS, SC_GATHER_ROWS)]
            return pltpu.make_async_copy(buf.at[c % nbuf], dst, wsem.at[c % nbuf])

        for c in range(min(nbuf - 1, nchunk)):
            gather_copy(c).start()
        for c in range(nchunk):
            gather_copy(c).wait()
            write_copy(c).start()
            if c + nbuf - 1 < nchunk:
                if c >= 1:
                    write_copy(c - 1).wait()
                gather_copy(c + nbuf - 1).start()
        for c in range(max(0, nchunk - nbuf), nchunk):
            write_copy(c).wait()

    return gather(table, idx)


def _sc_inverse_rows(dest_flat, n_rows, chunk=2048):
    n = dest_flat.shape[0]
    n_tokens = n // 2
    assert n_rows <= 3 * n_tokens
    nc = plsc.get_sparse_core_info().num_cores
    mesh = plsc.VectorSubcoreMesh(core_axis_name="core", subcore_axis_name="subcore")

    @functools.partial(
        pl.kernel, out_type=jax.ShapeDtypeStruct((n_rows,), jnp.int32), mesh=mesh,
        scratch_types=[pltpu.VMEM((n_rows,), jnp.int32), pltpu.VMEM((chunk,), jnp.int32)],
        compiler_params=pltpu.CompilerParams(needs_layout_passes=False),
        name="sc_inverse_rows")
    def inverse(d_hbm, o_hbm, inv_v, d_v):
        wid = lax.axis_index("subcore") * nc + lax.axis_index("core")

        @pl.when(wid == 0)
        def _():
            lanes = lax.iota(jnp.int32, SC_LANES)

            @pl.loop(0, n_rows // SC_LANES)
            def _(i):
                r = lanes + i * SC_LANES
                r = jnp.where(r >= n_tokens, r - n_tokens, r)
                inv_v[pl.ds(i * SC_LANES, SC_LANES)] = jnp.where(r >= n_tokens, r - n_tokens, r)

            @pl.loop(0, n // chunk)
            def _(c):
                pltpu.sync_copy(d_hbm.at[pl.ds(c * chunk, chunk)], d_v)

                @pl.loop(0, chunk // SC_LANES)
                def _(j):
                    rows = d_v[pl.ds(j * SC_LANES, SC_LANES)]
                    pair = lax.iota(jnp.int32, SC_LANES) + (c * chunk + j * SC_LANES)
                    plsc.store_scatter(inv_v, [rows], jnp.where(pair >= n_tokens, pair - n_tokens, pair))

            pltpu.sync_copy(inv_v, o_hbm)

    return inverse(dest_flat)


def _ffn_kernel(first_ref, count_ref, nv_ref, wg_hbm, wu_hbm, wd_hbm, xs_hbm, y_hbm,
                wg32, wu32, wd32, wgb, wub, wdb, xbuf, ybuf, wsem, isem, osem, *, n_blocks):
    nv = nv_ref[0]
    nbuf = FFN_BUFFERS

    def next_expert(e):
        def more(t):
            return jnp.logical_and(t < N_EXPERTS, count_ref[jnp.minimum(t, N_EXPERTS - 1)] == 0)
        return lax.while_loop(more, lambda t: t + 1, e + 1)

    def weight_copies(e, slot):
        ee = jnp.minimum(e, N_EXPERTS - 1)
        copies = []
        for i, (src, dst) in enumerate(((wg_hbm, wg32), (wu_hbm, wu32), (wd_hbm, wd32))):
            rows = src.shape[1] // FFN_WEIGHT_CHUNKS
            for c in range(FFN_WEIGHT_CHUNKS):
                part = pl.ds(c * rows, rows)
                copies.append(pltpu.make_async_copy(src.at[ee, part], dst.at[slot, part],
                                                    wsem.at[slot, i * FFN_WEIGHT_CHUNKS + c]))
        return copies

    def fetch_weights(e, slot):
        @pl.when(e < N_EXPERTS)
        def _():
            for c in weight_copies(e, slot):
                c.start()

    def take_weights(e, slot):
        for c in weight_copies(e, slot):
            c.wait()
        wgb[...] = wg32[slot].astype(BF16)
        wub[...] = wu32[slot].astype(BF16)
        wdb[...] = wd32[slot].astype(BF16)
        fetch_weights(next_expert(next_expert(e)), slot)

    def rows_of(b):
        return pl.ds(pl.multiple_of(b * ROW_BLOCK, ROW_BLOCK), ROW_BLOCK)

    def in_copy(b):
        return pltpu.make_async_copy(xs_hbm.at[rows_of(b)], xbuf.at[b % nbuf], isem.at[b % nbuf])

    def out_copy(b):
        return pltpu.make_async_copy(ybuf.at[b % nbuf], y_hbm.at[rows_of(b)], osem.at[b % nbuf])

    @pl.when(nv > 0)
    def _():
        e0 = next_expert(jnp.int32(-1))
        for i in range(nbuf - 1):
            @pl.when(i < nv)
            def _():
                in_copy(i).start()
        fetch_weights(e0, 0)
        fetch_weights(next_expert(e0), 1)
        take_weights(e0, 0)

        def body(b, carry):
            e, k = carry
            switch = b >= first_ref[e] + count_ref[e]
            e_new = jnp.where(switch, next_expert(e), e)
            k_new = jnp.where(switch, k + 1, k)

            @pl.when(switch)
            def _():
                take_weights(e_new, k_new % 2)

            in_copy(b).wait()

            @pl.when(b + nbuf - 1 < nv)
            def _():
                in_copy(b + nbuf - 1).start()

            @pl.when(b >= nbuf)
            def _():
                out_copy(b - nbuf).wait()

            x_hi, x_lo = _unpack_bf16_halves(xbuf[b % nbuf])
            xb = jnp.concatenate([x_hi.astype(BF16), x_lo.astype(BF16)], axis=1)
            a = _dot(xb, wgb[...])
            u = _dot(xb, wub[...])
            hid = (a * jax.nn.sigmoid(a) * u).astype(BF16)
            ybuf[b % nbuf] = _pack_bf16_halves(_dot(hid, wdb[...]))
            out_copy(b).start()
            return e_new, k_new

        lax.fori_loop(0, nv, body, (e0, jnp.int32(0)))

        for i in range(nbuf):
            @pl.when(nv > i)
            def _():
                out_copy(nv - 1 - i).wait()

    ybuf[0] = jnp.zeros((ROW_BLOCK, D_MODEL // 2), jnp.int32)

    def fill(b, carry):
        pltpu.sync_copy(ybuf.at[0], y_hbm.at[rows_of(b)])
        return carry

    lax.fori_loop(nv, n_blocks, fill, 0)


def _ffn(first_blk, n_blk, nv, xs, w_gate, w_up, w_down):
    n_rows = xs.shape[0]
    n_blocks = n_rows // ROW_BLOCK
    anyspec = pl.BlockSpec(memory_space=pl.ANY)
    return pl.pallas_call(
        functools.partial(_ffn_kernel, n_blocks=n_blocks),
        grid_spec=pltpu.PrefetchScalarGridSpec(
            num_scalar_prefetch=3,
            grid=(1,),
            in_specs=[anyspec, anyspec, anyspec, anyspec],
            out_specs=anyspec,
            scratch_shapes=[pltpu.VMEM((2, D_MODEL, D_FF), F32), pltpu.VMEM((2, D_MODEL, D_FF), F32),
                            pltpu.VMEM((2, D_FF, D_MODEL), F32),
                            pltpu.VMEM((D_MODEL, D_FF), BF16), pltpu.VMEM((D_MODEL, D_FF), BF16),
                            pltpu.VMEM((D_FF, D_MODEL), BF16),
                            pltpu.VMEM((FFN_BUFFERS, ROW_BLOCK, D_MODEL // 2), jnp.int32),
                            pltpu.VMEM((FFN_BUFFERS, ROW_BLOCK, D_MODEL // 2), jnp.int32),
                            pltpu.SemaphoreType.DMA((2, 3 * FFN_WEIGHT_CHUNKS)),
                            pltpu.SemaphoreType.DMA((FFN_BUFFERS,)),
                            pltpu.SemaphoreType.DMA((FFN_BUFFERS,))],
        ),
        out_shape=jax.ShapeDtypeStruct((n_rows, D_MODEL // 2), jnp.int32),
        compiler_params=pltpu.CompilerParams(
            dimension_semantics=("arbitrary",), vmem_limit_bytes=VMEM_LIMIT),
        name="expert_ffn",
    )(first_blk, n_blk, nv, w_gate, w_up, w_down, xs)


def _combine_kernel(h_ref, ya_ref, yb_ref, gate_ref, g_ref, b_ref, o_ref):
    gate = gate_ref[...]
    a_hi, a_lo = _unpack_bf16_halves(ya_ref[...])
    b_hi, b_lo = _unpack_bf16_halves(yb_ref[...])
    g0, g1 = gate[:, 0:1], gate[:, 1:2]
    ffn = jnp.concatenate([a_hi * g0 + b_hi * g1, a_lo * g0 + b_lo * g1], axis=1)
    o_ref[...] = _layer_norm(DEEPNORM_ALPHA * h_ref[...] + ffn, g_ref[...], b_ref[...])


def _combine(h, y2, gate, g, b, tm=512):
    T = h.shape[0]
    nt = T // tm
    return pl.pallas_call(
        _combine_kernel,
        grid=(nt,),
        in_specs=[pl.BlockSpec((tm, D_MODEL), lambda i: (i, 0)),
                  pl.BlockSpec((tm, D_MODEL // 2), lambda i: (i, 0)),
                  pl.BlockSpec((tm, D_MODEL // 2), lambda i: (i + nt, 0)),
                  pl.BlockSpec((tm, 2), lambda i: (i, 0)),
                  pl.BlockSpec((1, D_MODEL), lambda i: (0, 0)),
                  pl.BlockSpec((1, D_MODEL), lambda i: (0, 0))],
        out_specs=pl.BlockSpec((tm, D_MODEL), lambda i: (i, 0)),
        out_shape=jax.ShapeDtypeStruct((T, D_MODEL), F32),
        compiler_params=pltpu.CompilerParams(
            dimension_semantics=("parallel",), vmem_limit_bytes=VMEM_LIMIT),
        name="combine",
    )(h, y2, y2, gate, g, b)


def kernel(x, w_in, gla_gate_w2, gla_gate_b, gla_norm_g, dil_norm_g, w_out, ln1_g, ln1_b,
           router_coarse_w, router_coarse_b, router_fine_w, router_fine_b,
           expert_w_gate, expert_w_up, expert_w_down, ln2_g, ln2_b):
    B, S, D = x.shape
    T = B * S
    depth = w_in.shape[0]
    slopes = jnp.exp2(-8.0 * jnp.arange(1, DIL_HEADS + 1, dtype=F32) / DIL_HEADS)
    n_rows = 2 * T + N_EXPERTS * ROW_BLOCK
    n_blocks = n_rows // ROW_BLOCK
    n_blocks_pad = -(-n_blocks // (2 * LANES)) * (2 * LANES)
    assert n_blocks_pad == 2 * LANES
    a0 = 1536
    h = x.reshape(T, D)
    for l in range(depth):
        w = w_in[l]
        wm = jnp.concatenate([w[:, :a0], w[:, a0 + GLA_GATE_RANK:]], axis=1).astype(BF16)
        wa = jnp.pad(w[:, a0:a0 + GLA_GATE_RANK], ((0, 0), (0, LANES - GLA_GATE_RANK))).astype(BF16)
        w2 = jnp.pad(gla_gate_w2[l], ((0, LANES - GLA_GATE_RANK), (0, 0)))
        w2h, w2l = _split_bf16(w2)
        q, k, v, r, la, dq, dk, dv = _in_proj(h, wm, wa, w2h, w2l, gla_gate_b[l][None, :])
        o_gla = _gla(q, k, v, r, la, gla_norm_g[l][None, :], B, S)
        g2 = jnp.tile(dil_norm_g[l], 2)[None, :]
        o_dil = _dilated(slopes, dq, dk, dv, g2, B, S)
        wo = w_out[l].astype(BF16)
        rw = jnp.concatenate([router_fine_w[l].reshape(D, N_EXPERTS), router_coarse_w[l]], axis=1)
        rw = jnp.pad(rw, ((0, 0), (0, LANES - N_EXPERTS - N_GROUPS))).T
        rwh, rwl = _split_bf16(rw)
        rb = jnp.concatenate([router_fine_b[l].reshape(N_EXPERTS), router_coarse_b[l]])
        rb = jnp.pad(rb, (0, LANES - N_EXPERTS - N_GROUPS))[:, None]
        h1, h1p, eid_t, gate, cnt = _out_proj(o_gla, o_dil, h, wo[:GLA_WIDTH], wo[GLA_WIDTH:],
                                              ln1_g[l][None, :], ln1_b[l][None, :], rwh, rwl, rb)
        dest_t, be, nv = _positions(eid_t, cnt, n_blocks_pad)
        dest_flat = dest_t.reshape(2 * T)
        src_tok = _sc_inverse_rows(dest_flat, n_rows)
        xs = _sc_gather_rows(h1p, src_tok)
        be = be.reshape(n_blocks_pad)
        y = _ffn(be[:N_EXPERTS], be[LANES:LANES + N_EXPERTS], nv.reshape(LANES)[:1], xs,
                 expert_w_gate[l], expert_w_up[l], expert_w_down[l])
        y2 = _sc_gather_rows(y, dest_flat)
        h = _combine(h1, y2, gate, ln2_g[l][None, :], ln2_b[l][None, :])
    return h.reshape(B, S, D)
```

```python
import functools
import math

import jax
import jax.numpy as jnp
import numpy as np
from jax import lax
from jax.experimental import pallas as pl
from jax.experimental.pallas import tpu as pltpu
from jax.experimental.pallas import tpu_sc as plsc

D_MODEL = 1024
GLA_HEADS = 4
GLA_DK = 64
GLA_DV = 128
GLA_KEY_WIDTH = GLA_HEADS * GLA_DK
GLA_WIDTH = GLA_HEADS * GLA_DV
GLA_GATE_RANK = 16
GLA_GATE_TEMP = 16.0
DIL_HEADS = 8
DIL_DH = 64
DIL_WIDTH = DIL_HEADS * DIL_DH
DIL_CONFIGS = ((128, 1), (512, 4), (2048, 16))
DIL_BLOCK = 128
DIL_MAX_R = max(r for _, r in DIL_CONFIGS)
DIL_PAD = DIL_BLOCK * DIL_MAX_R
DIL_UNROLL = 16
N_GROUPS = 4
EXPERTS_PER_GROUP = 8
N_EXPERTS = N_GROUPS * EXPERTS_PER_GROUP
D_FF = 512
DEEPNORM_ALPHA = 2.0 ** 0.25
EPS = 1e-5
LOG2E = math.log2(math.e)
LN2 = math.log(2.0)

LANES = 128
GLA_CHUNK = 128
GLA_SUB = 64
GLA_UNROLL = 4
SC_LANES = 16
SC_INDEX_WINDOW = 128
SC_GATHER_BUFFERS = 6
SC_GATHER_ROWS = 32
FFN_BUFFERS = 8
FFN_WEIGHT_CHUNKS = 4
ROW_BLOCK = 256
VMEM_LIMIT = 56 * 1024 * 1024

F32 = jnp.float32
BF16 = jnp.bfloat16


def _dot(a, b):
    return jnp.dot(a, b, preferred_element_type=F32)


def _dot_nt(a, b):
    return lax.dot_general(a, b, (((1,), (1,)), ((), ())), preferred_element_type=F32)


def _dot_tn(a, b):
    return lax.dot_general(a, b, (((0,), (0,)), ((), ())), preferred_element_type=F32)


def _split_bf16(v):
    hi = v.astype(BF16)
    lo = (v - hi.astype(F32)).astype(BF16)
    return hi, lo


def _pack_bf16_halves(v):
    w = v.shape[1] // 2
    hi = lax.bitcast_convert_type(v[:, :w].astype(BF16).astype(F32), jnp.int32)
    lo = lax.bitcast_convert_type(v[:, w:].astype(BF16).astype(F32), jnp.int32)
    return hi | lax.shift_right_logical(lo, 16)


def _unpack_bf16_halves(words):
    hi = lax.bitcast_convert_type(words & jnp.int32(-65536), F32)
    lo = lax.bitcast_convert_type(lax.shift_left(words, 16), F32)
    return hi, lo


def _layer_norm(v, g, b):
    mu = jnp.mean(v, axis=-1, keepdims=True)
    c = v - mu
    var = jnp.mean(c * c, axis=-1, keepdims=True)
    return c * lax.rsqrt(var + EPS) * g + b


def _in_proj_kernel(x_ref, wm_ref, wa_ref, w2h_ref, w2l_ref, gb_ref,
                    q_ref, k_ref, v_ref, r_ref, la_ref, dq_ref, dk_ref, dv_ref):
    xb = x_ref[...].astype(BF16)

    def piece(c0, c1):
        return _dot(xb, wm_ref[:, c0:c1])

    q_ref[...] = (piece(0, 256) * (GLA_DK ** -0.5)).astype(BF16)
    k_ref[...] = piece(256, 512).astype(BF16)
    v_ref[...] = piece(512, 1024).astype(BF16)
    r_ref[...] = piece(1024, 1536).astype(BF16)
    dq_ref[...] = (piece(1536, 2048) * (DIL_DH ** -0.5 * LOG2E)).astype(BF16)
    dk_ref[...] = piece(2048, 2560).astype(BF16)
    dv_ref[...] = piece(2560, 3072).astype(BF16)
    ga = _dot(xb, wa_ref[...])
    ga_hi, ga_lo = _split_bf16(ga)
    z = _dot(ga_hi, w2h_ref[...]) + _dot(ga_lo, w2h_ref[...]) + _dot(ga_hi, w2l_ref[...]) + gb_ref[...]
    log_sig = jnp.minimum(z, 0.0) - jnp.log1p(jnp.exp(-jnp.abs(z)))
    la_ref[...] = log_sig * (1.0 / GLA_GATE_TEMP)


def _in_proj(x2, wm, wa, w2h, w2l, gb, tm=1024):
    T = x2.shape[0]
    row = lambda w: pl.BlockSpec((tm, w), lambda i: (i, 0))
    full = lambda a: pl.BlockSpec(a.shape, lambda i: (0,) * a.ndim)
    outs = [(256, BF16), (256, BF16), (512, BF16), (512, BF16), (256, F32),
            (512, BF16), (512, BF16), (512, BF16)]
    return pl.pallas_call(
        _in_proj_kernel,
        grid=(T // tm,),
        in_specs=[row(D_MODEL), full(wm), full(wa), full(w2h), full(w2l), full(gb)],
        out_specs=[row(w) for w, _ in outs],
        out_shape=[jax.ShapeDtypeStruct((T, w), dt) for w, dt in outs],
        compiler_params=pltpu.CompilerParams(
            dimension_semantics=("parallel",), vmem_limit_bytes=VMEM_LIMIT),
        name="in_proj",
    )(x2, wm, wa, w2h, w2l, gb)


def _gla_kernel(q_ref, k_ref, v_ref, r_ref, la_ref, g_ref, o_ref, s_ref, *, seq_block):
    C = GLA_CHUNK
    H = GLA_SUB
    assert C == 2 * H

    @pl.when(pl.program_id(1) == 0)
    def _():
        s_ref[...] = jnp.zeros_like(s_ref)

    ri = lax.broadcasted_iota(jnp.int32, (C, C), 0)
    ci = lax.broadcasted_iota(jnp.int32, (C, C), 1)
    same_sub = (ri // H) == (ci // H)
    sum_ops = jnp.concatenate([jnp.logical_and(same_sub, ci <= ri).astype(BF16), same_sub.astype(BF16),
                               jnp.ones((C, C), BF16)], axis=0)
    diag_mask = jnp.logical_and(same_sub, ci <= ri)
    off_mask = (ri // H) > (ci // H)
    second = lax.broadcasted_iota(jnp.int32, (C, 1), 0) >= H
    ones_cl = jnp.ones((C, LANES), BF16)
    lane_k = lax.broadcasted_iota(jnp.int32, (1, GLA_KEY_WIDTH), 1) // GLA_DK
    head_masks = [(lane_k == h).astype(F32) for h in range(GLA_HEADS)]
    srow = lax.broadcasted_iota(jnp.int32, (GLA_KEY_WIDTH, GLA_WIDTH), 0) // GLA_DK
    scol = lax.broadcasted_iota(jnp.int32, (GLA_KEY_WIDTH, GLA_WIDTH), 1) // GLA_DV
    state_mask = (srow == scol).astype(F32)
    g = g_ref[...]

    def trip(t, carry):
        U = GLA_UNROLL
        rows = [pl.ds(pl.multiple_of((t * U + u) * C, C), C) for u in range(U)]
        la2s, sums = [], []
        for u in range(U):
            la_hi, la_lo = _split_bf16(la_ref[rows[u], :])
            la2 = jnp.concatenate([la_hi, la_lo], axis=1)
            la2s.append(la2)
            sm = _dot(sum_ops, la2)
            sums.append(sm[:, 0:GLA_KEY_WIDTH] + sm[:, GLA_KEY_WIDTH:])
        q_states, k_states, scores = [], [], []
        for u in range(U):
            b = sums[u][0:C]
            t_sub = sums[u][C:2 * C]
            other = sums[u][2 * C:3 * C] - t_sub
            q = q_ref[rows[u], :].astype(F32)
            k = k_ref[rows[u], :].astype(F32)
            qd = q * jnp.exp(b)
            kd = (k * jnp.exp(-b)).astype(BF16)
            ke_f = k * jnp.exp(t_sub - b)
            cross = jnp.exp(other)
            q_states.append((qd * jnp.where(second, cross, 1.0)).astype(BF16))
            k_states.append((ke_f * jnp.where(second, 1.0, cross)).astype(BF16))
            q_heads = jnp.concatenate([(qd * head_masks[h]).astype(BF16) for h in range(GLA_HEADS)], axis=0)
            keys2 = jnp.concatenate([kd, ke_f.astype(BF16)], axis=0)
            scores.append(_dot_nt(q_heads, keys2))
        decs, upds = [], []
        for u in range(U):
            tot = _dot_tn(la2s[u], ones_cl)
            decs.append(jnp.exp(tot[0:GLA_KEY_WIDTH] + tot[GLA_KEY_WIDTH:]))
            upds.append(_dot_tn(k_states[u], v_ref[rows[u], :]) * state_mask)
        o_inters = []
        for u in range(U):
            state = s_ref[...]
            o_inters.append(_dot(q_states[u], state.astype(BF16)))
            for h in range(GLA_HEADS):
                cols = slice(h * GLA_DV, (h + 1) * GLA_DV)
                s_ref[:, cols] = state[:, cols] * decs[u] + upds[u][:, cols]
        for u in range(U):
            v = v_ref[rows[u], :]
            outs = []
            for h in range(GLA_HEADS):
                sh = scores[u][h * C:(h + 1) * C]
                a = jnp.where(diag_mask, sh[:, 0:C], 0.0) + jnp.where(off_mask, sh[:, C:2 * C], 0.0)
                cols = slice(h * GLA_DV, (h + 1) * GLA_DV)
                o = _dot(a.astype(BF16), v[:, cols]) + o_inters[u][:, cols]
                o = o * lax.rsqrt(jnp.mean(o * o, axis=-1, keepdims=True) + EPS) * g
                outs.append(o)
            o_all = jnp.concatenate(outs, axis=-1)
            rr = r_ref[rows[u], :].astype(F32)
            o_ref[rows[u], :] = (o_all * (rr * jax.nn.sigmoid(rr))).astype(BF16)
        return carry

    lax.fori_loop(0, seq_block // C // GLA_UNROLL, trip, 0)


def _gla(q, k, v, r, la, g, batch, seq, seq_block=1024):
    nsb = seq // seq_block
    row = lambda w: pl.BlockSpec((seq_block, w), lambda b, s: (b * nsb + s, 0))
    return pl.pallas_call(
        functools.partial(_gla_kernel, seq_block=seq_block),
        grid=(batch, nsb),
        in_specs=[row(256), row(256), row(512), row(512), row(256),
                  pl.BlockSpec((1, GLA_DV), lambda b, s: (0, 0))],
        out_specs=row(512),
        out_shape=jax.ShapeDtypeStruct((batch * seq, GLA_WIDTH), BF16),
        scratch_shapes=[pltpu.VMEM((GLA_KEY_WIDTH, GLA_WIDTH), F32)],
        compiler_params=pltpu.CompilerParams(
            dimension_semantics=("parallel", "arbitrary"), vmem_limit_bytes=VMEM_LIMIT),
        name="gla",
    )(q, k, v, r, la, g)


def _dil_kernel(slope_ref, q_ref, k_ref, v_ref, g_ref, o_ref,
                qf, kf, vf, kd, va, vb, oc, lc, *, seq):
    B = DIL_BLOCK
    U = DIL_UNROLL
    pair = pl.program_id(1)
    qf[...] = q_ref[...].astype(F32)
    kf[...] = k_ref[...].astype(F32)
    vf[...] = v_ref[...].astype(F32)

    lane = lax.broadcasted_iota(jnp.int32, (1, LANES), 1)
    first = lane < DIL_DH
    ii = lax.broadcasted_iota(jnp.int32, (B, B), 0)
    jj = lax.broadcasted_iota(jnp.int32, (B, B), 1)
    upper = jj > ii
    eye = jj == ii
    dist = jnp.bitwise_and(ii - jj, B - 1).astype(F32)
    neg = jnp.float32(-jnp.inf)
    neg_tile = jnp.full((B, B), neg, F32)
    zero_tile = jnp.zeros((B, LANES), BF16)

    for cfg, (window, r) in enumerate(DIL_CONFIGS):
        nb = seq // r // B
        cs = nb + 1
        bias_prev, bias_cur = [], []
        for hh in range(2):
            slope = slope_ref[2 * pair + hh] * (float(r) * LOG2E)
            bias = dist * (-slope)
            bias_prev.append(jnp.where(upper, bias, jnp.where(eye, -slope * float(B), neg)))
            bias_cur.append(jnp.where(upper, neg, bias))

        for c in range(r):
            rows0 = slice(c * cs * B, (c * cs + 1) * B)
            kd[rows0, :] = zero_tile
            va[rows0, :] = zero_tile
            vb[rows0, :] = zero_tile

        def prep(t4, carry, r=r, nb=nb, cs=cs):
            for j in range(4):
                t = t4 * 4 + j
                c = t // nb
                n = t % nb
                start = c + n * (B * r)
                rows = pl.ds(start, B, stride=r) if r > 1 else pl.ds(pl.multiple_of(start, B), B)
                dst = pl.ds(pl.multiple_of((c * cs + 1 + n) * B, B), B)
                kd[dst, :] = kf[rows, :].astype(BF16)
                v = vf[rows, :]
                va[dst, :] = jnp.where(first, v, 1.0).astype(BF16)
                vb[dst, :] = jnp.where(first, 1.0, v).astype(BF16)
            return carry

        lax.fori_loop(0, seq // B // 4, prep, 0)

        def geom(it, r=r, nb=nb, cs=cs):
            c = it // nb
            n = it % nb
            start = c + n * (B * r)
            rows = pl.ds(start, B, stride=r) if r > 1 else pl.ds(pl.multiple_of(start, B), B)
            kv = pl.ds(pl.multiple_of((c * cs + n) * B, B), 2 * B)
            return rows, kv

        def body(t, carry, cfg=cfg, nb=nb, bias_prev=bias_prev, bias_cur=bias_cur, geom=geom):
            geo = [geom(t * U + u) for u in range(U)]
            scores = []
            for u in range(U):
                rows, kv = geo[u]
                q = qf[rows, :]
                q_heads = jnp.concatenate([jnp.where(first, q, 0.0), jnp.where(first, 0.0, q)], axis=0)
                scores.append(_dot_nt(q_heads.astype(BF16), kd[kv, :]))
            probs, maxes = [], []
            for u in range(U):
                if nb % U == 0:
                    has_prev = True if u > 0 else (t * U) % nb > 0
                else:
                    assert U % nb == 0
                    has_prev = (u % nb) > 0
                for hh in range(2):
                    s2 = scores[u][hh * B:(hh + 1) * B]
                    if has_prev is True:
                        bp = bias_prev[hh]
                    elif has_prev is False:
                        bp = neg_tile
                    else:
                        bp = jnp.where(has_prev, bias_prev[hh], neg)
                    s_prev = s2[:, 0:B] + bp
                    s_cur = s2[:, B:2 * B] + bias_cur[hh]
                    m = jnp.max(jnp.maximum(s_prev, s_cur), axis=-1, keepdims=True)
                    probs.append(jnp.concatenate([jnp.exp2(s_prev - m), jnp.exp2(s_cur - m)], axis=1).astype(BF16))
                    maxes.append(m)
            for u in range(U):
                rows, kv = geo[u]
                acc0 = _dot(probs[2 * u], va[kv, :])
                acc1 = _dot(probs[2 * u + 1], vb[kv, :])
                num = jnp.where(first, acc0, acc1)
                den = pltpu.roll(jnp.where(first, acc1, acc0), DIL_DH, axis=1)
                oc[cfg, rows, :] = num * (1.0 / den)
                lc[cfg, rows, :] = (jnp.where(first, maxes[2 * u], maxes[2 * u + 1]) + jnp.log2(den)) * LN2
            return carry

        lax.fori_loop(0, seq // B // U, body, 0)

    g = g_ref[...]
    CH = 512

    def mix(i, carry):
        rows = pl.ds(pl.multiple_of(i * CH, CH), CH)
        l0, l1, l2 = lc[0, rows, :], lc[1, rows, :], lc[2, rows, :]
        m = jnp.maximum(jnp.maximum(l0, l1), l2)
        e0, e1, e2 = jnp.exp(l0 - m), jnp.exp(l1 - m), jnp.exp(l2 - m)
        den = e0 + e1 + e2
        o = (e0 / den) * oc[0, rows, :] + (e1 / den) * oc[1, rows, :] + (e2 / den) * oc[2, rows, :]
        sq = o * o
        ms_a = jnp.sum(jnp.where(first, sq, 0.0), axis=-1, keepdims=True) * (1.0 / DIL_DH)
        ms_b = jnp.sum(jnp.where(first, 0.0, sq), axis=-1, keepdims=True) * (1.0 / DIL_DH)
        ms = jnp.where(first, ms_a, ms_b)
        o_ref[rows, :] = (o * lax.rsqrt(ms + EPS) * g).astype(BF16)
        return carry

    lax.fori_loop(0, seq // CH, mix, 0)


def _dilated(slopes, dq, dk, dv, g2, batch, seq):
    blk = pl.BlockSpec((seq, LANES), lambda b, p, s: (b, p))
    return pl.pallas_call(
        functools.partial(_dil_kernel, seq=seq),
        grid_spec=pltpu.PrefetchScalarGridSpec(
            num_scalar_prefetch=1,
            grid=(batch, DIL_WIDTH // LANES),
            in_specs=[blk, blk, blk, pl.BlockSpec((1, LANES), lambda b, p, s: (0, 0))],
            out_specs=blk,
            scratch_shapes=[pltpu.VMEM((seq, LANES), F32)] * 3
                           + [pltpu.VMEM((seq + DIL_PAD, LANES), BF16)] * 3
                           + [pltpu.VMEM((3, seq, LANES), F32)] * 2,
        ),
        out_shape=jax.ShapeDtypeStruct((batch * seq, DIL_WIDTH), BF16),
        compiler_params=pltpu.CompilerParams(
            dimension_semantics=("parallel", "parallel"), vmem_limit_bytes=VMEM_LIMIT),
        name="dilated",
    )(slopes, dq, dk, dv, g2)


def _out_proj_kernel(og_ref, od_ref, x_ref, wg_ref, wd_ref, g_ref, b_ref,
                     rwh_ref, rwl_ref, rb_ref, h_ref, hp_ref, eid_ref, gate_ref, cnt_ref):
    mix = _dot(og_ref[...], wg_ref[...]) + _dot(od_ref[...], wd_ref[...])
    h = _layer_norm(DEEPNORM_ALPHA * x_ref[...] + mix, g_ref[...], b_ref[...])
    h_ref[...] = h
    hp_ref[...] = _pack_bf16_halves(h)
    h_hi, h_lo = _split_bf16(h)
    lt = (_dot_nt(rwh_ref[...], h_hi) + _dot_nt(rwh_ref[...], h_lo) + _dot_nt(rwl_ref[...], h_hi)
          + rb_ref[...])
    tm = lt.shape[1]
    row = lax.broadcasted_iota(jnp.int32, (EXPERTS_PER_GROUP, tm), 0).astype(F32)
    neg = jnp.float32(-jnp.inf)
    big = jnp.float32(1e9)
    coarse = jnp.where(row < N_GROUPS, lt[N_EXPERTS:N_EXPERTS + EXPERTS_PER_GROUP, :], neg)
    cmax = jnp.max(coarse, axis=0, keepdims=True)
    g_idx = jnp.min(jnp.where(coarse == cmax, row, big), axis=0, keepdims=True)
    p_group = 1.0 / jnp.sum(jnp.exp(coarse - cmax), axis=0, keepdims=True)
    fine = lt[(N_GROUPS - 1) * EXPERTS_PER_GROUP:N_EXPERTS, :]
    for g in range(N_GROUPS - 2, -1, -1):
        fine = jnp.where(g_idx == g, lt[g * EXPERTS_PER_GROUP:(g + 1) * EXPERTS_PER_GROUP, :], fine)
    v1 = jnp.max(fine, axis=0, keepdims=True)
    i1 = jnp.min(jnp.where(fine == v1, row, big), axis=0, keepdims=True)
    fine2 = jnp.where(row == i1, neg, fine)
    v2 = jnp.max(fine2, axis=0, keepdims=True)
    i2 = jnp.min(jnp.where(fine2 == v2, row, big), axis=0, keepdims=True)
    e2 = jnp.exp(v2 - v1)
    den = 1.0 + e2
    gate1 = p_group * (1.0 / den)
    gate2 = p_group * (e2 / den)
    id1 = g_idx * EXPERTS_PER_GROUP + i1
    id2 = g_idx * EXPERTS_PER_GROUP + i2
    eid_ref[...] = jnp.concatenate([id1, id2], axis=0).astype(jnp.int32)
    slab = jnp.concatenate([gate1, gate2, jnp.zeros((LANES - 2, tm), F32)], axis=0)
    gate_ref[...] = slab.T[:, 0:2]
    sub = lax.broadcasted_iota(jnp.int32, (LANES, tm), 0).astype(F32)
    onehot = jnp.logical_or(sub == id1, sub == id2).astype(BF16)

    @pl.when(pl.program_id(0) == 0)
    def _():
        cnt_ref[...] = jnp.zeros_like(cnt_ref)

    cnt_ref[...] += _dot(onehot, jnp.ones((tm, LANES), BF16))


def _out_proj(og, od, x2, wg, wd, g, b, rwh, rwl, rb, tm=1024):
    T = x2.shape[0]
    row = lambda w: pl.BlockSpec((tm, w), lambda i: (i, 0))
    full = lambda a: pl.BlockSpec(a.shape, lambda i: (0,) * a.ndim)
    return pl.pallas_call(
        _out_proj_kernel,
        grid=(T // tm,),
        in_specs=[row(512), row(512), row(D_MODEL), full(wg), full(wd), full(g), full(b),
                  full(rwh), full(rwl), full(rb)],
        out_specs=[row(D_MODEL), row(D_MODEL // 2), pl.BlockSpec((2, tm), lambda i: (0, i)), row(2),
                   pl.BlockSpec((LANES, LANES), lambda i: (0, 0))],
        out_shape=[jax.ShapeDtypeStruct((T, D_MODEL), F32),
                   jax.ShapeDtypeStruct((T, D_MODEL // 2), jnp.int32),
                   jax.ShapeDtypeStruct((2, T), jnp.int32),
                   jax.ShapeDtypeStruct((T, 2), F32),
                   jax.ShapeDtypeStruct((LANES, LANES), F32)],
        compiler_params=pltpu.CompilerParams(
            dimension_semantics=("arbitrary",), vmem_limit_bytes=VMEM_LIMIT),
        name="out_proj_router",
    )(og, od, x2, wg, wd, g, b, rwh, rwl, rb)


def _positions_kernel(eid_ref, cnt_ref, dest_ref, be_ref, nv_ref, carry_ref, sp_ref, tri_ref, *, tb):
    i = pl.program_id(0)

    @pl.when(i == 0)
    def _():
        shift = int(math.log2(ROW_BLOCK))
        nb_col = (cnt_ref[...].astype(jnp.int32) + (ROW_BLOCK - 1)) >> shift
        r = lax.broadcasted_iota(jnp.int32, (LANES, LANES), 0)
        c = lax.broadcasted_iota(jnp.int32, (LANES, LANES), 1)
        nb_f = jnp.where(r < N_EXPERTS, nb_col, 0).astype(F32)
        start_col = _dot((c < r).astype(BF16), nb_f.astype(BF16))
        sp_ref[...] = start_col * float(ROW_BLOCK)
        carry_ref[...] = jnp.zeros_like(carry_ref)
        be_ref[...] = jnp.concatenate([start_col.T[0:1, :], nb_f.T[0:1, :]], axis=1).astype(jnp.int32)
        total = jnp.sum(nb_f[:, 0:1], axis=0, keepdims=True)
        nv_ref[...] = jnp.broadcast_to(total, (1, LANES)).astype(jnp.int32)
        tr = lax.broadcasted_iota(jnp.int32, (tb, tb), 0)
        tc = lax.broadcasted_iota(jnp.int32, (tb, tb), 1)
        tri_ref[...] = (tr < tc).astype(BF16)

    sub = lax.broadcasted_iota(jnp.int32, (LANES, tb), 0)
    oh1 = sub == eid_ref[0:1, :]
    oh2 = sub == eid_ref[1:2, :]
    oh = jnp.logical_or(oh1, oh2).astype(BF16)
    offset = jnp.tile(carry_ref[...] + sp_ref[...], (1, tb // LANES))
    before = _dot(oh, tri_ref[...]) + offset
    d1 = jnp.sum(jnp.where(oh1, before, 0.0), axis=0, keepdims=True)
    d2 = jnp.sum(jnp.where(oh2, before, 0.0), axis=0, keepdims=True)
    dest_ref[...] = jnp.concatenate([d1, d2], axis=0).astype(jnp.int32)
    carry_ref[...] += _dot(oh, jnp.ones((tb, LANES), BF16))


def _positions(eid_t, cnt, n_blocks_pad, tb=1024):
    T = eid_t.shape[1]
    return pl.pallas_call(
        functools.partial(_positions_kernel, tb=tb),
        grid=(T // tb,),
        in_specs=[pl.BlockSpec((2, tb), lambda i: (0, i)), pl.BlockSpec((LANES, LANES), lambda i: (0, 0))],
        out_specs=[pl.BlockSpec((2, tb), lambda i: (0, i)),
                   pl.BlockSpec((1, n_blocks_pad), lambda i: (0, 0)),
                   pl.BlockSpec((1, LANES), lambda i: (0, 0))],
        out_shape=[jax.ShapeDtypeStruct((2, T), jnp.int32),
                   jax.ShapeDtypeStruct((1, n_blocks_pad), jnp.int32),
                   jax.ShapeDtypeStruct((1, LANES), jnp.int32)],
        scratch_shapes=[pltpu.VMEM((LANES, LANES), F32), pltpu.VMEM((LANES, LANES), F32),
                        pltpu.VMEM((tb, tb), BF16)],
        compiler_params=pltpu.CompilerParams(dimension_semantics=("arbitrary",)),
        name="positions",
    )(eid_t, cnt)


def _sc_gather_rows(table, idx):
    n = idx.shape[0]
    d = table.shape[1]
    info = plsc.get_sparse_core_info()
    nc, ns = info.num_cores, info.num_subcores
    per_w = n // (nc * ns)
    assert per_w * nc * ns == n and per_w % SC_INDEX_WINDOW == 0
    mesh = plsc.VectorSubcoreMesh(core_axis_name="core", subcore_axis_name="subcore")
    nchunk = per_w // SC_GATHER_ROWS
    nbuf = SC_GATHER_BUFFERS

    @functools.partial(
        pl.kernel, out_type=jax.ShapeDtypeStruct((n, d), table.dtype), mesh=mesh,
        scratch_types=[pltpu.VMEM((per_w,), jnp.int32),
                       pltpu.VMEM((nbuf, SC_GATHER_ROWS, d), table.dtype),
                       pltpu.SemaphoreType.DMA((nbuf,)), pltpu.SemaphoreType.DMA((nbuf,))],
        name="sc_gather_rows")
    def gather(x_hbm, i_hbm, o_hbm, idx_v, buf, gsem, wsem):
        wid = lax.axis_index("subcore") * nc + lax.axis_index("core")
        base = wid * per_w
        pltpu.sync_copy(i_hbm.at[pl.ds(base, per_w)], idx_v)

        def gather_copy(c):
            rows = idx_v.at[pl.ds(c * SC_GATHER_ROWS, SC_GATHER_ROWS)]
            return pltpu.make_async_copy(x_hbm.at[rows], buf.at[c % nbuf], gsem.at[c % nbuf])

        def write_copy(c):
            dst = o_hbm.at[pl.ds(base + c * SC_GATHER_ROWS, SC_GATHER_ROWS)]
            return pltpu.make_async_copy(buf.at[c % nbuf], dst, wsem.at[c % nbuf])

        for c in range(min(nbuf - 1, nchunk)):
            gather_copy(c).start()
        for c in range(nchunk):
            gather_copy(c).wait()
            write_copy(c).start()
            if c + nbuf - 1 < nchunk:
                if c >= 1:
                    write_copy(c - 1).wait()
                gather_copy(c + nbuf - 1).start()
        for c in range(max(0, nchunk - nbuf), nchunk):
            write_copy(c).wait()

    return gather(table, idx)


def _sc_inverse_rows(dest_flat, n_rows, chunk=2048):
    n = dest_flat.shape[0]
    n_tokens = n // 2
    assert n_rows <= 3 * n_tokens
    nc = plsc.get_sparse_core_info().num_cores
    mesh = plsc.VectorSubcoreMesh(core_axis_name="core", subcore_axis_name="subcore")

    @functools.partial(
        pl.kernel, out_type=jax.ShapeDtypeStruct((n_rows,), jnp.int32), mesh=mesh,
        scratch_types=[pltpu.VMEM((n_rows,), jnp.int32), pltpu.VMEM((chunk,), jnp.int32)],
        compiler_params=pltpu.CompilerParams(needs_layout_passes=False),
        name="sc_inverse_rows")
    def inverse(d_hbm, o_hbm, inv_v, d_v):
        wid = lax.axis_index("subcore") * nc + lax.axis_index("core")

        @pl.when(wid == 0)
        def _():
            lanes = lax.iota(jnp.int32, SC_LANES)

            @pl.loop(0, n_rows // SC_LANES)
            def _(i):
                r = lanes + i * SC_LANES
                r = jnp.where(r >= n_tokens, r - n_tokens, r)
                inv_v[pl.ds(i * SC_LANES, SC_LANES)] = jnp.where(r >= n_tokens, r - n_tokens, r)

            @pl.loop(0, n // chunk)
            def _(c):
                pltpu.sync_copy(d_hbm.at[pl.ds(c * chunk, chunk)], d_v)

                @pl.loop(0, chunk // SC_LANES)
                def _(j):
                    rows = d_v[pl.ds(j * SC_LANES, SC_LANES)]
                    pair = lax.iota(jnp.int32, SC_LANES) + (c * chunk + j * SC_LANES)
                    plsc.store_scatter(inv_v, [rows], jnp.where(pair >= n_tokens, pair - n_tokens, pair))

            pltpu.sync_copy(inv_v, o_hbm)

    return inverse(dest_flat)


def _ffn_kernel(first_ref, count_ref, nv_ref, wg_hbm, wu_hbm, wd_hbm, xs_hbm, y_hbm,
                wg32, wu32, wd32, wgb, wub, wdb, xbuf, ybuf, wsem, isem, osem, *, n_blocks):
    nv = nv_ref[0]
    nbuf = FFN_BUFFERS

    def next_expert(e):
        def more(t):
            return jnp.logical_and(t < N_EXPERTS, count_ref[jnp.minimum(t, N_EXPERTS - 1)] == 0)
        return lax.while_loop(more, lambda t: t + 1, e + 1)

    def weight_copies(e, slot):
        ee = jnp.minimum(e, N_EXPERTS - 1)
        copies = []
        for i, (src, dst) in enumerate(((wg_hbm, wg32), (wu_hbm, wu32), (wd_hbm, wd32))):
            rows = src.shape[1] // FFN_WEIGHT_CHUNKS
            for c in range(FFN_WEIGHT_CHUNKS):
                part = pl.ds(c * rows, rows)
                copies.append(pltpu.make_async_copy(src.at[ee, part], dst.at[slot, part],
                                                    wsem.at[slot, i * FFN_WEIGHT_CHUNKS + c]))
        return copies

    def fetch_weights(e, slot):
        @pl.when(e < N_EXPERTS)
        def _():
            for c in weight_copies(e, slot):
                c.start()

    def take_weights(e, slot):
        for c in weight_copies(e, slot):
            c.wait()
        wgb[...] = wg32[slot].astype(BF16)
        wub[...] = wu32[slot].astype(BF16)
        wdb[...] = wd32[slot].astype(BF16)
        fetch_weights(next_expert(next_expert(e)), slot)

    def rows_of(b):
        return pl.ds(pl.multiple_of(b * ROW_BLOCK, ROW_BLOCK), ROW_BLOCK)

    def buf_rows(b, nblk):
        return pl.ds(pl.multiple_of((b % nbuf) * ROW_BLOCK, ROW_BLOCK), nblk * ROW_BLOCK)

    def in_copy(b):
        return pltpu.make_async_copy(xs_hbm.at[rows_of(b)], xbuf.at[buf_rows(b, 1)], isem.at[b % nbuf])

    def out_copy(b):
        return pltpu.make_async_copy(ybuf.at[buf_rows(b, 1)], y_hbm.at[rows_of(b)], osem.at[b % nbuf])

    def expert_mlp(words):
        x_hi, x_lo = _unpack_bf16_halves(words)
        xb = jnp.concatenate([x_hi.astype(BF16), x_lo.astype(BF16)], axis=1)
        a = _dot(xb, wgb[...])
        u = _dot(xb, wub[...])
        hid = (a * jax.nn.sigmoid(a) * u).astype(BF16)
        return _pack_bf16_halves(_dot(hid, wdb[...]))

    ahead = nbuf - 2

    @pl.when(nv > 0)
    def _():
        e0 = next_expert(jnp.int32(-1))
        for i in range(ahead):
            @pl.when(i < nv)
            def _():
                in_copy(i).start()
        fetch_weights(e0, 0)
        fetch_weights(next_expert(e0), 1)
        take_weights(e0, 0)

        def step(carry):
            b, e, k = carry
            switch = b >= first_ref[e] + count_ref[e]
            e_new = jnp.where(switch, next_expert(e), e)
            k_new = jnp.where(switch, k + 1, k)

            @pl.when(switch)
            def _():
                take_weights(e_new, k_new % 2)

            pair = jnp.logical_and(b + 1 < first_ref[e_new] + count_ref[e_new], b % nbuf != nbuf - 1)
            n = jnp.where(pair, 2, 1)
            for i in range(2):
                @pl.when(jnp.logical_and(i < n, b + ahead + i < nv))
                def _():
                    in_copy(b + ahead + i).start()
            for i in range(2):
                @pl.when(i < n)
                def _():
                    in_copy(b + i).wait()

                    @pl.when(b + i >= nbuf)
                    def _():
                        out_copy(b + i - nbuf).wait()

            @pl.when(pair)
            def _():
                ybuf[buf_rows(b, 2), :] = expert_mlp(xbuf[buf_rows(b, 2), :])

            @pl.when(jnp.logical_not(pair))
            def _():
                ybuf[buf_rows(b, 1), :] = expert_mlp(xbuf[buf_rows(b, 1), :])

            for i in range(2):
                @pl.when(i < n)
                def _():
                    out_copy(b + i).start()
            return b + n, e_new, k_new

        lax.while_loop(lambda c: c[0] < nv, step, (jnp.int32(0), e0, jnp.int32(0)))

        for i in range(nbuf):
            @pl.when(nv > i)
            def _():
                out_copy(nv - 1 - i).wait()

    ybuf[0:ROW_BLOCK, :] = jnp.zeros((ROW_BLOCK, D_MODEL // 2), jnp.int32)

    def fill(b, carry):
        pltpu.sync_copy(ybuf.at[pl.ds(0, ROW_BLOCK)], y_hbm.at[rows_of(b)])
        return carry

    lax.fori_loop(nv, n_blocks, fill, 0)


def _ffn(first_blk, n_blk, nv, xs, w_gate, w_up, w_down):
    n_rows = xs.shape[0]
    n_blocks = n_rows // ROW_BLOCK
    anyspec = pl.BlockSpec(memory_space=pl.ANY)
    return pl.pallas_call(
        functools.partial(_ffn_kernel, n_blocks=n_blocks),
        grid_spec=pltpu.PrefetchScalarGridSpec(
            num_scalar_prefetch=3,
            grid=(1,),
            in_specs=[anyspec, anyspec, anyspec, anyspec],
            out_specs=anyspec,
            scratch_shapes=[pltpu.VMEM((2, D_MODEL, D_FF), F32), pltpu.VMEM((2, D_MODEL, D_FF), F32),
                            pltpu.VMEM((2, D_FF, D_MODEL), F32),
                            pltpu.VMEM((D_MODEL, D_FF), BF16), pltpu.VMEM((D_MODEL, D_FF), BF16),
                            pltpu.VMEM((D_FF, D_MODEL), BF16),
                            pltpu.VMEM((FFN_BUFFERS * ROW_BLOCK, D_MODEL // 2), jnp.int32),
                            pltpu.VMEM((FFN_BUFFERS * ROW_BLOCK, D_MODEL // 2), jnp.int32),
                            pltpu.SemaphoreType.DMA((2, 3 * FFN_WEIGHT_CHUNKS)),
                            pltpu.SemaphoreType.DMA((FFN_BUFFERS,)),
                            pltpu.SemaphoreType.DMA((FFN_BUFFERS,))],
        ),
        out_shape=jax.ShapeDtypeStruct((n_rows, D_MODEL // 2), jnp.int32),
        compiler_params=pltpu.CompilerParams(
            dimension_semantics=("arbitrary",), vmem_limit_bytes=VMEM_LIMIT),
        name="expert_ffn",
    )(first_blk, n_blk, nv, w_gate, w_up, w_down, xs)


def _combine_kernel(h_ref, ya_ref, yb_ref, gate_ref, g_ref, b_ref, o_ref):
    gate = gate_ref[...]
    a_hi, a_lo = _unpack_bf16_halves(ya_ref[...])
    b_hi, b_lo = _unpack_bf16_halves(yb_ref[...])
    g0, g1 = gate[:, 0:1], gate[:, 1:2]
    ffn = jnp.concatenate([a_hi * g0 + b_hi * g1, a_lo * g0 + b_lo * g1], axis=1)
    o_ref[...] = _layer_norm(DEEPNORM_ALPHA * h_ref[...] + ffn, g_ref[...], b_ref[...])


def _combine(h, y2, gate, g, b, tm=512):
    T = h.shape[0]
    nt = T // tm
    return pl.pallas_call(
        _combine_kernel,
        grid=(nt,),
        in_specs=[pl.BlockSpec((tm, D_MODEL), lambda i: (i, 0)),
                  pl.BlockSpec((tm, D_MODEL // 2), lambda i: (i, 0)),
                  pl.BlockSpec((tm, D_MODEL // 2), lambda i: (i + nt, 0)),
                  pl.BlockSpec((tm, 2), lambda i: (i, 0)),
                  pl.BlockSpec((1, D_MODEL), lambda i: (0, 0)),
                  pl.BlockSpec((1, D_MODEL), lambda i: (0, 0))],
        out_specs=pl.BlockSpec((tm, D_MODEL), lambda i: (i, 0)),
        out_shape=jax.ShapeDtypeStruct((T, D_MODEL), F32),
        compiler_params=pltpu.CompilerParams(
            dimension_semantics=("parallel",), vmem_limit_bytes=VMEM_LIMIT),
        name="combine",
    )(h, y2, y2, gate, g, b)


def kernel(x, w_in, gla_gate_w2, gla_gate_b, gla_norm_g, dil_norm_g, w_out, ln1_g, ln1_b,
           router_coarse_w, router_coarse_b, router_fine_w, router_fine_b,
           expert_w_gate, expert_w_up, expert_w_down, ln2_g, ln2_b):
    B, S, D = x.shape
    T = B * S
    depth = w_in.shape[0]
    slopes = jnp.exp2(-8.0 * jnp.arange(1, DIL_HEADS + 1, dtype=F32) / DIL_HEADS)
    n_rows = 2 * T + N_EXPERTS * ROW_BLOCK
    n_blocks = n_rows // ROW_BLOCK
    n_blocks_pad = -(-n_blocks // (2 * LANES)) * (2 * LANES)
    assert n_blocks_pad == 2 * LANES
    a0 = 1536
    h = x.reshape(T, D)
    for l in range(depth):
        w = w_in[l]
        wm = jnp.concatenate([w[:, :a0], w[:, a0 + GLA_GATE_RANK:]], axis=1).astype(BF16)
        wa = jnp.pad(w[:, a0:a0 + GLA_GATE_RANK], ((0, 0), (0, LANES - GLA_GATE_RANK))).astype(BF16)
        w2 = jnp.pad(gla_gate_w2[l], ((0, LANES - GLA_GATE_RANK), (0, 0)))
        w2h, w2l = _split_bf16(w2)
        q, k, v, r, la, dq, dk, dv = _in_proj(h, wm, wa, w2h, w2l, gla_gate_b[l][None, :])
        o_gla = _gla(q, k, v, r, la, gla_norm_g[l][None, :], B, S)
        g2 = jnp.tile(dil_norm_g[l], 2)[None, :]
        o_dil = _dilated(slopes, dq, dk, dv, g2, B, S)
        wo = w_out[l].astype(BF16)
        rw = jnp.concatenate([router_fine_w[l].reshape(D, N_EXPERTS), router_coarse_w[l]], axis=1)
        rw = jnp.pad(rw, ((0, 0), (0, LANES - N_EXPERTS - N_GROUPS))).T
        rwh, rwl = _split_bf16(rw)
        rb = jnp.concatenate([router_fine_b[l].reshape(N_EXPERTS), router_coarse_b[l]])
        rb = jnp.pad(rb, (0, LANES - N_EXPERTS - N_GROUPS))[:, None]
        h1, h1p, eid_t, gate, cnt = _out_proj(o_gla, o_dil, h, wo[:GLA_WIDTH], wo[GLA_WIDTH:],
                                              ln1_g[l][None, :], ln1_b[l][None, :], rwh, rwl, rb)
        dest_t, be, nv = _positions(eid_t, cnt, n_blocks_pad)
        dest_flat = dest_t.reshape(2 * T)
        src_tok = _sc_inverse_rows(dest_flat, n_rows)
        xs = _sc_gather_rows(h1p, src_tok)
        be = be.reshape(n_blocks_pad)
        y = _ffn(be[:N_EXPERTS], be[LANES:LANES + N_EXPERTS], nv.reshape(LANES)[:1], xs,
                 expert_w_gate[l], expert_w_up[l], expert_w_down[l])
        y2 = _sc_gather_rows(y, dest_flat)
        h = _combine(h1, y2, gate, ln2_g[l][None, :], ln2_b[l][None, :])
    return h.reshape(B, S, D)
```

```python
import functools
import math

import jax
import jax.numpy as jnp
import numpy as np
from jax import lax
from jax.experimental import pallas as pl
from jax.experimental.pallas import tpu as pltpu
from jax.experimental.pallas import tpu_sc as plsc

D_MODEL = 1024
GLA_HEADS = 4
GLA_DK = 64
GLA_DV = 128
GLA_KEY_WIDTH = GLA_HEADS * GLA_DK
GLA_WIDTH = GLA_HEADS * GLA_DV
GLA_GATE_RANK = 16
GLA_GATE_TEMP = 16.0
DIL_HEADS = 8
DIL_DH = 64
DIL_WIDTH = DIL_HEADS * DIL_DH
DIL_CONFIGS = ((128, 1), (512, 4), (2048, 16))
DIL_BLOCK = 128
DIL_MAX_R = max(r for _, r in DIL_CONFIGS)
DIL_PAD = DIL_BLOCK * DIL_MAX_R
DIL_UNROLL = 16
N_GROUPS = 4
EXPERTS_PER_GROUP = 8
N_EXPERTS = N_GROUPS * EXPERTS_PER_GROUP
D_FF = 512
DEEPNORM_ALPHA = 2.0 ** 0.25
EPS = 1e-5
LOG2E = math.log2(math.e)
LN2 = math.log(2.0)

LANES = 128
GLA_CHUNK = 128
GLA_SUB = 64
GLA_UNROLL = 4
SC_LANES = 16
SC_INDEX_WINDOW = 128
SC_GATHER_BUFFERS = 6
SC_GATHER_ROWS = 32
FFN_BUFFERS = 12
FFN_GROUPS = (4, 2)
FFN_WEIGHT_CHUNKS = 4
ROW_BLOCK = 256
VMEM_LIMIT = 56 * 1024 * 1024

F32 = jnp.float32
BF16 = jnp.bfloat16


def _dot(a, b):
    return jnp.dot(a, b, preferred_element_type=F32)


def _dot_nt(a, b):
    return lax.dot_general(a, b, (((1,), (1,)), ((), ())), preferred_element_type=F32)


def _dot_tn(a, b):
    return lax.dot_general(a, b, (((0,), (0,)), ((), ())), preferred_element_type=F32)


def _split_bf16(v):
    hi = v.astype(BF16)
    lo = (v - hi.astype(F32)).astype(BF16)
    return hi, lo


def _pack_bf16_halves(v):
    w = v.shape[1] // 2
    hi = lax.bitcast_convert_type(v[:, :w].astype(BF16).astype(F32), jnp.int32)
    lo = lax.bitcast_convert_type(v[:, w:].astype(BF16).astype(F32), jnp.int32)
    return hi | lax.shift_right_logical(lo, 16)


def _unpack_bf16_halves(words):
    hi = lax.bitcast_convert_type(words & jnp.int32(-65536), F32)
    lo = lax.bitcast_convert_type(lax.shift_left(words, 16), F32)
    return hi, lo


def _layer_norm(v, g, b):
    mu = jnp.mean(v, axis=-1, keepdims=True)
    c = v - mu
    var = jnp.mean(c * c, axis=-1, keepdims=True)
    return c * lax.rsqrt(var + EPS) * g + b


def _in_proj_kernel(x_ref, wm_ref, wa_ref, w2h_ref, w2l_ref, gb_ref,
                    q_ref, k_ref, v_ref, r_ref, la_ref, dq_ref, dk_ref, dv_ref):
    xb = x_ref[...].astype(BF16)

    def piece(c0, c1):
        return _dot(xb, wm_ref[:, c0:c1])

    q_ref[...] = (piece(0, 256) * (GLA_DK ** -0.5)).astype(BF16)
    k_ref[...] = piece(256, 512).astype(BF16)
    v_ref[...] = piece(512, 1024).astype(BF16)
    r_ref[...] = piece(1024, 1536).astype(BF16)
    dq_ref[...] = (piece(1536, 2048) * (DIL_DH ** -0.5 * LOG2E)).astype(BF16)
    dk_ref[...] = piece(2048, 2560).astype(BF16)
    dv_ref[...] = piece(2560, 3072).astype(BF16)
    ga = _dot(xb, wa_ref[...])
    ga_hi, ga_lo = _split_bf16(ga)
    z = _dot(ga_hi, w2h_ref[...]) + _dot(ga_lo, w2h_ref[...]) + _dot(ga_hi, w2l_ref[...]) + gb_ref[...]
    log_sig = jnp.minimum(z, 0.0) - jnp.log1p(jnp.exp(-jnp.abs(z)))
    la_ref[...] = log_sig * (1.0 / GLA_GATE_TEMP)


def _in_proj(x2, wm, wa, w2h, w2l, gb, tm=1024):
    T = x2.shape[0]
    row = lambda w: pl.BlockSpec((tm, w), lambda i: (i, 0))
    full = lambda a: pl.BlockSpec(a.shape, lambda i: (0,) * a.ndim)
    outs = [(256, BF16), (256, BF16), (512, BF16), (512, BF16), (256, F32),
            (512, BF16), (512, BF16), (512, BF16)]
    return pl.pallas_call(
        _in_proj_kernel,
        grid=(T // tm,),
        in_specs=[row(D_MODEL), full(wm), full(wa), full(w2h), full(w2l), full(gb)],
        out_specs=[row(w) for w, _ in outs],
        out_shape=[jax.ShapeDtypeStruct((T, w), dt) for w, dt in outs],
        compiler_params=pltpu.CompilerParams(
            dimension_semantics=("parallel",), vmem_limit_bytes=VMEM_LIMIT),
        name="in_proj",
    )(x2, wm, wa, w2h, w2l, gb)


def _gla_kernel(q_ref, k_ref, v_ref, r_ref, la_ref, g_ref, o_ref, s_ref, *, seq_block):
    C = GLA_CHUNK
    H = GLA_SUB
    assert C == 2 * H

    @pl.when(pl.program_id(1) == 0)
    def _():
        s_ref[...] = jnp.zeros_like(s_ref)

    ri = lax.broadcasted_iota(jnp.int32, (C, C), 0)
    ci = lax.broadcasted_iota(jnp.int32, (C, C), 1)
    same_sub = (ri // H) == (ci // H)
    sum_ops = jnp.concatenate([jnp.logical_and(same_sub, ci <= ri).astype(BF16), same_sub.astype(BF16),
                               jnp.ones((C, C), BF16)], axis=0)
    diag_mask = jnp.logical_and(same_sub, ci <= ri)
    off_mask = (ri // H) > (ci // H)
    second = lax.broadcasted_iota(jnp.int32, (C, 1), 0) >= H
    ones_cl = jnp.ones((C, LANES), BF16)
    lane_k = lax.broadcasted_iota(jnp.int32, (1, GLA_KEY_WIDTH), 1) // GLA_DK
    head_masks = [(lane_k == h).astype(F32) for h in range(GLA_HEADS)]
    srow = lax.broadcasted_iota(jnp.int32, (GLA_KEY_WIDTH, GLA_WIDTH), 0) // GLA_DK
    scol = lax.broadcasted_iota(jnp.int32, (GLA_KEY_WIDTH, GLA_WIDTH), 1) // GLA_DV
    state_mask = (srow == scol).astype(F32)
    g = g_ref[...]

    def trip(t, carry):
        U = GLA_UNROLL
        rows = [pl.ds(pl.multiple_of((t * U + u) * C, C), C) for u in range(U)]
        la2s, sums = [], []
        for u in range(U):
            la_hi, la_lo = _split_bf16(la_ref[rows[u], :])
            la2 = jnp.concatenate([la_hi, la_lo], axis=1)
            la2s.append(la2)
            sm = _dot(sum_ops, la2)
            sums.append(sm[:, 0:GLA_KEY_WIDTH] + sm[:, GLA_KEY_WIDTH:])
        q_states, k_states, scores = [], [], []
        for u in range(U):
            b = sums[u][0:C]
            t_sub = sums[u][C:2 * C]
            other = sums[u][2 * C:3 * C] - t_sub
            q = q_ref[rows[u], :].astype(F32)
            k = k_ref[rows[u], :].astype(F32)
            qd = q * jnp.exp(b)
            kd = (k * jnp.exp(-b)).astype(BF16)
            ke_f = k * jnp.exp(t_sub - b)
            cross = jnp.exp(other)
            q_states.append((qd * jnp.where(second, cross, 1.0)).astype(BF16))
            k_states.append((ke_f * jnp.where(second, 1.0, cross)).astype(BF16))
            q_heads = jnp.concatenate([(qd * head_masks[h]).astype(BF16) for h in range(GLA_HEADS)], axis=0)
            keys2 = jnp.concatenate([kd, ke_f.astype(BF16)], axis=0)
            scores.append(_dot_nt(q_heads, keys2))
        decs, upds = [], []
        for u in range(U):
            tot = _dot_tn(la2s[u], ones_cl)
            decs.append(jnp.exp(tot[0:GLA_KEY_WIDTH] + tot[GLA_KEY_WIDTH:]))
            upds.append(_dot_tn(k_states[u], v_ref[rows[u], :]) * state_mask)
        o_inters = []
        for u in range(U):
            state = s_ref[...]
            o_inters.append(_dot(q_states[u], state.astype(BF16)))
            for h in range(GLA_HEADS):
                cols = slice(h * GLA_DV, (h + 1) * GLA_DV)
                s_ref[:, cols] = state[:, cols] * decs[u] + upds[u][:, cols]
        for u in range(U):
            v = v_ref[rows[u], :]
            outs = []
            for h in range(GLA_HEADS):
                sh = scores[u][h * C:(h + 1) * C]
                a = jnp.where(diag_mask, sh[:, 0:C], 0.0) + jnp.where(off_mask, sh[:, C:2 * C], 0.0)
                cols = slice(h * GLA_DV, (h + 1) * GLA_DV)
                o = _dot(a.astype(BF16), v[:, cols]) + o_inters[u][:, cols]
                o = o * lax.rsqrt(jnp.mean(o * o, axis=-1, keepdims=True) + EPS) * g
                outs.append(o)
            o_all = jnp.concatenate(outs, axis=-1)
            rr = r_ref[rows[u], :].astype(F32)
            o_ref[rows[u], :] = (o_all * (rr * jax.nn.sigmoid(rr))).astype(BF16)
        return carry

    lax.fori_loop(0, seq_block // C // GLA_UNROLL, trip, 0)


def _gla(q, k, v, r, la, g, batch, seq, seq_block=1024):
    nsb = seq // seq_block
    row = lambda w: pl.BlockSpec((seq_block, w), lambda b, s: (b * nsb + s, 0))
    return pl.pallas_call(
        functools.partial(_gla_kernel, seq_block=seq_block),
        grid=(batch, nsb),
        in_specs=[row(256), row(256), row(512), row(512), row(256),
                  pl.BlockSpec((1, GLA_DV), lambda b, s: (0, 0))],
        out_specs=row(512),
        out_shape=jax.ShapeDtypeStruct((batch * seq, GLA_WIDTH), BF16),
        scratch_shapes=[pltpu.VMEM((GLA_KEY_WIDTH, GLA_WIDTH), F32)],
        compiler_params=pltpu.CompilerParams(
            dimension_semantics=("parallel", "arbitrary"), vmem_limit_bytes=VMEM_LIMIT),
        name="gla",
    )(q, k, v, r, la, g)


def _dil_kernel(slope_ref, q_ref, k_ref, v_ref, g_ref, o_ref,
                qf, kf, vf, kd, va, vb, oc, lc, *, seq):
    B = DIL_BLOCK
    U = DIL_UNROLL
    pair = pl.program_id(1)
    qf[...] = q_ref[...].astype(F32)
    kf[...] = k_ref[...].astype(F32)
    vf[...] = v_ref[...].astype(F32)

    lane = lax.broadcasted_iota(jnp.int32, (1, LANES), 1)
    first = lane < DIL_DH
    ii = lax.broadcasted_iota(jnp.int32, (B, B), 0)
    jj = lax.broadcasted_iota(jnp.int32, (B, B), 1)
    upper = jj > ii
    eye = jj == ii
    dist = jnp.bitwise_and(ii - jj, B - 1).astype(F32)
    neg = jnp.float32(-jnp.inf)
    neg_tile = jnp.full((B, B), neg, F32)
    zero_tile = jnp.zeros((B, LANES), BF16)

    for cfg, (window, r) in enumerate(DIL_CONFIGS):
        nb = seq // r // B
        cs = nb + 1
        bias_prev, bias_cur = [], []
        for hh in range(2):
            slope = slope_ref[2 * pair + hh] * (float(r) * LOG2E)
            bias = dist * (-slope)
            bias_prev.append(jnp.where(upper, bias, jnp.where(eye, -slope * float(B), neg)))
            bias_cur.append(jnp.where(upper, neg, bias))

        for c in range(r):
            rows0 = slice(c * cs * B, (c * cs + 1) * B)
            kd[rows0, :] = zero_tile
            va[rows0, :] = zero_tile
            vb[rows0, :] = zero_tile

        def prep(t4, carry, r=r, nb=nb, cs=cs):
            for j in range(4):
                t = t4 * 4 + j
                c = t // nb
                n = t % nb
                start = c + n * (B * r)
                rows = pl.ds(start, B, stride=r) if r > 1 else pl.ds(pl.multiple_of(start, B), B)
                dst = pl.ds(pl.multiple_of((c * cs + 1 + n) * B, B), B)
                kd[dst, :] = kf[rows, :].astype(BF16)
                v = vf[rows, :]
                va[dst, :] = jnp.where(first, v, 1.0).astype(BF16)
                vb[dst, :] = jnp.where(first, 1.0, v).astype(BF16)
            return carry

        lax.fori_loop(0, seq // B // 4, prep, 0)

        def geom(it, r=r, nb=nb, cs=cs):
            c = it // nb
            n = it % nb
            start = c + n * (B * r)
            rows = pl.ds(start, B, stride=r) if r > 1 else pl.ds(pl.multiple_of(start, B), B)
            kv = pl.ds(pl.multiple_of((c * cs + n) * B, B), 2 * B)
            return rows, kv

        def body(t, carry, cfg=cfg, nb=nb, bias_prev=bias_prev, bias_cur=bias_cur, geom=geom):
            geo = [geom(t * U + u) for u in range(U)]
            scores = []
            for u in range(U):
                rows, kv = geo[u]
                q = qf[rows, :]
                q_heads = jnp.concatenate([jnp.where(first, q, 0.0), jnp.where(first, 0.0, q)], axis=0)
                scores.append(_dot_nt(q_heads.astype(BF16), kd[kv, :]))
            probs, maxes = [], []
            for u in range(U):
                if nb % U == 0:
                    has_prev = True if u > 0 else (t * U) % nb > 0
                else:
                    assert U % nb == 0
                    has_prev = (u % nb) > 0
                for hh in range(2):
                    s2 = scores[u][hh * B:(hh + 1) * B]
                    if has_prev is True:
                        bp = bias_prev[hh]
                    elif has_prev is False:
                        bp = neg_tile
                    else:
                        bp = jnp.where(has_prev, bias_prev[hh], neg)
                    s_prev = s2[:, 0:B] + bp
                    s_cur = s2[:, B:2 * B] + bias_cur[hh]
                    m = jnp.max(jnp.maximum(s_prev, s_cur), axis=-1, keepdims=True)
                    probs.append(jnp.concatenate([jnp.exp2(s_prev - m), jnp.exp2(s_cur - m)], axis=1).astype(BF16))
                    maxes.append(m)
            for u in range(U):
                rows, kv = geo[u]
                acc0 = _dot(probs[2 * u], va[kv, :])
                acc1 = _dot(probs[2 * u + 1], vb[kv, :])
                num = jnp.where(first, acc0, acc1)
                den = pltpu.roll(jnp.where(first, acc1, acc0), DIL_DH, axis=1)
                oc[cfg, rows, :] = num * (1.0 / den)
                lc[cfg, rows, :] = (jnp.where(first, maxes[2 * u], maxes[2 * u + 1]) + jnp.log2(den)) * LN2
            return carry

        lax.fori_loop(0, seq // B // U, body, 0)

    g = g_ref[...]
    CH = 512

    def mix(i, carry):
        rows = pl.ds(pl.multiple_of(i * CH, CH), CH)
        l0, l1, l2 = lc[0, rows, :], lc[1, rows, :], lc[2, rows, :]
        m = jnp.maximum(jnp.maximum(l0, l1), l2)
        e0, e1, e2 = jnp.exp(l0 - m), jnp.exp(l1 - m), jnp.exp(l2 - m)
        den = e0 + e1 + e2
        o = (e0 / den) * oc[0, rows, :] + (e1 / den) * oc[1, rows, :] + (e2 / den) * oc[2, rows, :]
        sq = o * o
        ms_a = jnp.sum(jnp.where(first, sq, 0.0), axis=-1, keepdims=True) * (1.0 / DIL_DH)
        ms_b = jnp.sum(jnp.where(first, 0.0, sq), axis=-1, keepdims=True) * (1.0 / DIL_DH)
        ms = jnp.where(first, ms_a, ms_b)
        o_ref[rows, :] = (o * lax.rsqrt(ms + EPS) * g).astype(BF16)
        return carry

    lax.fori_loop(0, seq // CH, mix, 0)


def _dilated(slopes, dq, dk, dv, g2, batch, seq):
    blk = pl.BlockSpec((seq, LANES), lambda b, p, s: (b, p))
    return pl.pallas_call(
        functools.partial(_dil_kernel, seq=seq),
        grid_spec=pltpu.PrefetchScalarGridSpec(
            num_scalar_prefetch=1,
            grid=(batch, DIL_WIDTH // LANES),
            in_specs=[blk, blk, blk, pl.BlockSpec((1, LANES), lambda b, p, s: (0, 0))],
            out_specs=blk,
            scratch_shapes=[pltpu.VMEM((seq, LANES), F32)] * 3
                           + [pltpu.VMEM((seq + DIL_PAD, LANES), BF16)] * 3
                           + [pltpu.VMEM((3, seq, LANES), F32)] * 2,
        ),
        out_shape=jax.ShapeDtypeStruct((batch * seq, DIL_WIDTH), BF16),
        compiler_params=pltpu.CompilerParams(
            dimension_semantics=("parallel", "parallel"), vmem_limit_bytes=VMEM_LIMIT),
        name="dilated",
    )(slopes, dq, dk, dv, g2)


def _out_proj_kernel(og_ref, od_ref, x_ref, wg_ref, wd_ref, g_ref, b_ref,
                     rwh_ref, rwl_ref, rb_ref, h_ref, hp_ref, eid_ref, gate_ref, cnt_ref):
    mix = _dot(og_ref[...], wg_ref[...]) + _dot(od_ref[...], wd_ref[...])
    h = _layer_norm(DEEPNORM_ALPHA * x_ref[...] + mix, g_ref[...], b_ref[...])
    h_ref[...] = h
    hp_ref[...] = _pack_bf16_halves(h)
    h_hi, h_lo = _split_bf16(h)
    lt = (_dot_nt(rwh_ref[...], h_hi) + _dot_nt(rwh_ref[...], h_lo) + _dot_nt(rwl_ref[...], h_hi)
          + rb_ref[...])
    tm = lt.shape[1]
    row = lax.broadcasted_iota(jnp.int32, (EXPERTS_PER_GROUP, tm), 0).astype(F32)
    neg = jnp.float32(-jnp.inf)
    big = jnp.float32(1e9)
    coarse = jnp.where(row < N_GROUPS, lt[N_EXPERTS:N_EXPERTS + EXPERTS_PER_GROUP, :], neg)
    cmax = jnp.max(coarse, axis=0, keepdims=True)
    g_idx = jnp.min(jnp.where(coarse == cmax, row, big), axis=0, keepdims=True)
    p_group = 1.0 / jnp.sum(jnp.exp(coarse - cmax), axis=0, keepdims=True)
    fine = lt[(N_GROUPS - 1) * EXPERTS_PER_GROUP:N_EXPERTS, :]
    for g in range(N_GROUPS - 2, -1, -1):
        fine = jnp.where(g_idx == g, lt[g * EXPERTS_PER_GROUP:(g + 1) * EXPERTS_PER_GROUP, :], fine)
    v1 = jnp.max(fine, axis=0, keepdims=True)
    i1 = jnp.min(jnp.where(fine == v1, row, big), axis=0, keepdims=True)
    fine2 = jnp.where(row == i1, neg, fine)
    v2 = jnp.max(fine2, axis=0, keepdims=True)
    i2 = jnp.min(jnp.where(fine2 == v2, row, big), axis=0, keepdims=True)
    e2 = jnp.exp(v2 - v1)
    den = 1.0 + e2
    gate1 = p_group * (1.0 / den)
    gate2 = p_group * (e2 / den)
    id1 = g_idx * EXPERTS_PER_GROUP + i1
    id2 = g_idx * EXPERTS_PER_GROUP + i2
    eid_ref[...] = jnp.concatenate([id1, id2], axis=0).astype(jnp.int32)
    slab = jnp.concatenate([gate1, gate2, jnp.zeros((LANES - 2, tm), F32)], axis=0)
    gate_ref[...] = slab.T[:, 0:2]
    sub = lax.broadcasted_iota(jnp.int32, (LANES, tm), 0).astype(F32)
    onehot = jnp.logical_or(sub == id1, sub == id2).astype(BF16)

    @pl.when(pl.program_id(0) == 0)
    def _():
        cnt_ref[...] = jnp.zeros_like(cnt_ref)

    cnt_ref[...] += _dot(onehot, jnp.ones((tm, LANES), BF16))


def _out_proj(og, od, x2, wg, wd, g, b, rwh, rwl, rb, tm=1024):
    T = x2.shape[0]
    row = lambda w: pl.BlockSpec((tm, w), lambda i: (i, 0))
    full = lambda a: pl.BlockSpec(a.shape, lambda i: (0,) * a.ndim)
    return pl.pallas_call(
        _out_proj_kernel,
        grid=(T // tm,),
        in_specs=[row(512), row(512), row(D_MODEL), full(wg), full(wd), full(g), full(b),
                  full(rwh), full(rwl), full(rb)],
        out_specs=[row(D_MODEL), row(D_MODEL // 2), pl.BlockSpec((2, tm), lambda i: (0, i)), row(2),
                   pl.BlockSpec((LANES, LANES), lambda i: (0, 0))],
        out_shape=[jax.ShapeDtypeStruct((T, D_MODEL), F32),
                   jax.ShapeDtypeStruct((T, D_MODEL // 2), jnp.int32),
                   jax.ShapeDtypeStruct((2, T), jnp.int32),
                   jax.ShapeDtypeStruct((T, 2), F32),
                   jax.ShapeDtypeStruct((LANES, LANES), F32)],
        compiler_params=pltpu.CompilerParams(
            dimension_semantics=("arbitrary",), vmem_limit_bytes=VMEM_LIMIT),
        name="out_proj_router",
    )(og, od, x2, wg, wd, g, b, rwh, rwl, rb)


def _positions_kernel(eid_ref, cnt_ref, dest_ref, be_ref, nv_ref, carry_ref, sp_ref, tri_ref, *, tb):
    i = pl.program_id(0)

    @pl.when(i == 0)
    def _():
        shift = int(math.log2(ROW_BLOCK))
        nb_col = (cnt_ref[...].astype(jnp.int32) + (ROW_BLOCK - 1)) >> shift
        r = lax.broadcasted_iota(jnp.int32, (LANES, LANES), 0)
        c = lax.broadcasted_iota(jnp.int32, (LANES, LANES), 1)
        nb_f = jnp.where(r < N_EXPERTS, nb_col, 0).astype(F32)
        start_col = _dot((c < r).astype(BF16), nb_f.astype(BF16))
        sp_ref[...] = start_col * float(ROW_BLOCK)
        carry_ref[...] = jnp.zeros_like(carry_ref)
        be_ref[...] = jnp.concatenate([start_col.T[0:1, :], nb_f.T[0:1, :]], axis=1).astype(jnp.int32)
        total = jnp.sum(nb_f[:, 0:1], axis=0, keepdims=True)
        nv_ref[...] = jnp.broadcast_to(total, (1, LANES)).astype(jnp.int32)
        tr = lax.broadcasted_iota(jnp.int32, (tb, tb), 0)
        tc = lax.broadcasted_iota(jnp.int32, (tb, tb), 1)
        tri_ref[...] = (tr < tc).astype(BF16)

    sub = lax.broadcasted_iota(jnp.int32, (LANES, tb), 0)
    oh1 = sub == eid_ref[0:1, :]
    oh2 = sub == eid_ref[1:2, :]
    oh = jnp.logical_or(oh1, oh2).astype(BF16)
    offset = jnp.tile(carry_ref[...] + sp_ref[...], (1, tb // LANES))
    before = _dot(oh, tri_ref[...]) + offset
    d1 = jnp.sum(jnp.where(oh1, before, 0.0), axis=0, keepdims=True)
    d2 = jnp.sum(jnp.where(oh2, before, 0.0), axis=0, keepdims=True)
    dest_ref[...] = jnp.concatenate([d1, d2], axis=0).astype(jnp.int32)
    carry_ref[...] += _dot(oh, jnp.ones((tb, LANES), BF16))


def _positions(eid_t, cnt, n_blocks_pad, tb=1024):
    T = eid_t.shape[1]
    return pl.pallas_call(
        functools.partial(_positions_kernel, tb=tb),
        grid=(T // tb,),
        in_specs=[pl.BlockSpec((2, tb), lambda i: (0, i)), pl.BlockSpec((LANES, LANES), lambda i: (0, 0))],
        out_specs=[pl.BlockSpec((2, tb), lambda i: (0, i)),
                   pl.BlockSpec((1, n_blocks_pad), lambda i: (0, 0)),
                   pl.BlockSpec((1, LANES), lambda i: (0, 0))],
        out_shape=[jax.ShapeDtypeStruct((2, T), jnp.int32),
                   jax.ShapeDtypeStruct((1, n_blocks_pad), jnp.int32),
                   jax.ShapeDtypeStruct((1, LANES), jnp.int32)],
        scratch_shapes=[pltpu.VMEM((LANES, LANES), F32), pltpu.VMEM((LANES, LANES), F32),
                        pltpu.VMEM((tb, tb), BF16)],
        compiler_params=pltpu.CompilerParams(dimension_semantics=("arbitrary",)),
        name="positions",
    )(eid_t, cnt)


def _sc_gather_rows(table, idx):
    n = idx.shape[0]
    d = table.shape[1]
    info = plsc.get_sparse_core_info()
    nc, ns = info.num_cores, info.num_subcores
    per_w = n // (nc * ns)
    assert per_w * nc * ns == n and per_w % SC_INDEX_WINDOW == 0
    mesh = plsc.VectorSubcoreMesh(core_axis_name="core", subcore_axis_name="subcore")
    nchunk = per_w // SC_GATHER_ROWS
    nbuf = SC_GATHER_BUFFERS

    @functools.partial(
        pl.kernel, out_type=jax.ShapeDtypeStruct((n, d), table.dtype), mesh=mesh,
        scratch_types=[pltpu.VMEM((per_w,), jnp.int32),
                       pltpu.VMEM((nbuf, SC_GATHER_ROWS, d), table.dtype),
                       pltpu.SemaphoreType.DMA((nbuf,)), pltpu.SemaphoreType.DMA((nbuf,))],
        name="sc_gather_rows")
    def gather(x_hbm, i_hbm, o_hbm, idx_v, buf, gsem, wsem):
        wid = lax.axis_index("subcore") * nc + lax.axis_index("core")
        base = wid * per_w
        pltpu.sync_copy(i_hbm.at[pl.ds(base, per_w)], idx_v)

        def gather_copy(c):
            rows = idx_v.at[pl.ds(c * SC_GATHER_ROWS, SC_GATHER_ROWS)]
            return pltpu.make_async_copy(x_hbm.at[rows], buf.at[c % nbuf], gsem.at[c % nbuf])

        def write_copy(c):
            dst = o_hbm.at[pl.ds(base + c * SC_GATHER_ROWS, SC_GATHER_ROWS)]
            return pltpu.make_async_copy(buf.at[c % nbuf], dst, wsem.at[c % nbuf])

        for c in range(min(nbuf - 1, nchunk)):
            gather_copy(c).start()
        for c in range(nchunk):
            gather_copy(c).wait()
            write_copy(c).start()
            if c + nbuf - 1 < nchunk:
                if c >= 1:
                    write_copy(c - 1).wait()
                gather_copy(c + nbuf - 1).start()
        for c in range(max(0, nchunk - nbuf), nchunk):
            write_copy(c).wait()

    return gather(table, idx)


def _sc_inverse_rows(dest_flat, n_rows, chunk=2048):
    n = dest_flat.shape[0]
    n_tokens = n // 2
    assert n_rows <= 3 * n_tokens
    nc = plsc.get_sparse_core_info().num_cores
    mesh = plsc.VectorSubcoreMesh(core_axis_name="core", subcore_axis_name="subcore")

    @functools.partial(
        pl.kernel, out_type=jax.ShapeDtypeStruct((n_rows,), jnp.int32), mesh=mesh,
        scratch_types=[pltpu.VMEM((n_rows,), jnp.int32), pltpu.VMEM((chunk,), jnp.int32)],
        compiler_params=pltpu.CompilerParams(needs_layout_passes=False),
        name="sc_inverse_rows")
    def inverse(d_hbm, o_hbm, inv_v, d_v):
        wid = lax.axis_index("subcore") * nc + lax.axis_index("core")

        @pl.when(wid == 0)
        def _():
            lanes = lax.iota(jnp.int32, SC_LANES)

            @plsc.parallel_loop(0, n_rows // SC_LANES, unroll=8)
            def _(i):
                r = lanes + i * SC_LANES
                r = jnp.where(r >= n_tokens, r - n_tokens, r)
                inv_v[pl.ds(i * SC_LANES, SC_LANES)] = jnp.where(r >= n_tokens, r - n_tokens, r)

            @pl.loop(0, n // chunk)
            def _(c):
                pltpu.sync_copy(d_hbm.at[pl.ds(c * chunk, chunk)], d_v)

                @plsc.parallel_loop(0, chunk // SC_LANES, unroll=8)
                def _(j):
                    rows = d_v[pl.ds(j * SC_LANES, SC_LANES)]
                    pair = lax.iota(jnp.int32, SC_LANES) + (c * chunk + j * SC_LANES)
                    plsc.store_scatter(inv_v, [rows], jnp.where(pair >= n_tokens, pair - n_tokens, pair))

            pltpu.sync_copy(inv_v, o_hbm)

    return inverse(dest_flat)


def _ffn_kernel(first_ref, count_ref, nv_ref, wg_hbm, wu_hbm, wd_hbm, xs_hbm, y_hbm,
                wg32, wu32, wd32, wgb, wub, wdb, xbuf, ybuf, wsem, isem, osem, *, n_blocks):
    nv = nv_ref[0]
    nbuf = FFN_BUFFERS

    def next_expert(e):
        def more(t):
            return jnp.logical_and(t < N_EXPERTS, count_ref[jnp.minimum(t, N_EXPERTS - 1)] == 0)
        return lax.while_loop(more, lambda t: t + 1, e + 1)

    def weight_copies(e, slot):
        ee = jnp.minimum(e, N_EXPERTS - 1)
        copies = []
        for i, (src, dst) in enumerate(((wg_hbm, wg32), (wu_hbm, wu32), (wd_hbm, wd32))):
            rows = src.shape[1] // FFN_WEIGHT_CHUNKS
            for c in range(FFN_WEIGHT_CHUNKS):
                part = pl.ds(c * rows, rows)
                copies.append(pltpu.make_async_copy(src.at[ee, part], dst.at[slot, part],
                                                    wsem.at[slot, i * FFN_WEIGHT_CHUNKS + c]))
        return copies

    def fetch_weights(e, slot):
        @pl.when(e < N_EXPERTS)
        def _():
            for c in weight_copies(e, slot):
                c.start()

    def take_weights(e, slot):
        for c in weight_copies(e, slot):
            c.wait()
        wgb[...] = wg32[slot].astype(BF16)
        wub[...] = wu32[slot].astype(BF16)
        wdb[...] = wd32[slot].astype(BF16)
        fetch_weights(next_expert(next_expert(e)), slot)

    def rows_of(b):
        return pl.ds(pl.multiple_of(b * ROW_BLOCK, ROW_BLOCK), ROW_BLOCK)

    def buf_rows(b, nblk):
        return pl.ds(pl.multiple_of((b % nbuf) * ROW_BLOCK, ROW_BLOCK), nblk * ROW_BLOCK)

    def in_copy(b):
        return pltpu.make_async_copy(xs_hbm.at[rows_of(b)], xbuf.at[buf_rows(b, 1)], isem.at[b % nbuf])

    def out_copy(b):
        return pltpu.make_async_copy(ybuf.at[buf_rows(b, 1)], y_hbm.at[rows_of(b)], osem.at[b % nbuf])

    def expert_mlp(words):
        x_hi, x_lo = _unpack_bf16_halves(words)
        xb = jnp.concatenate([x_hi.astype(BF16), x_lo.astype(BF16)], axis=1)
        a = _dot(xb, wgb[...])
        u = _dot(xb, wub[...])
        hid = (a * jax.nn.sigmoid(a) * u).astype(BF16)
        return _pack_bf16_halves(_dot(hid, wdb[...]))

    ahead = nbuf - max(FFN_GROUPS)

    @pl.when(nv > 0)
    def _():
        e0 = next_expert(jnp.int32(-1))
        for i in range(ahead):
            @pl.when(i < nv)
            def _():
                in_copy(i).start()
        fetch_weights(e0, 0)
        fetch_weights(next_expert(e0), 1)
        take_weights(e0, 0)

        def step(carry):
            b, e, k = carry
            switch = b >= first_ref[e] + count_ref[e]
            e_new = jnp.where(switch, next_expert(e), e)
            k_new = jnp.where(switch, k + 1, k)

            @pl.when(switch)
            def _():
                take_weights(e_new, k_new % 2)

            end = first_ref[e_new] + count_ref[e_new]
            n = jnp.int32(1)
            for size in FFN_GROUPS:
                fits = jnp.logical_and(b + size <= end, b % nbuf + size <= nbuf)
                n = jnp.where(jnp.logical_and(n == 1, fits), size, n)
            for i in range(max(FFN_GROUPS)):
                @pl.when(jnp.logical_and(i < n, b + ahead + i < nv))
                def _():
                    in_copy(b + ahead + i).start()
            for i in range(max(FFN_GROUPS)):
                @pl.when(i < n)
                def _():
                    in_copy(b + i).wait()

                    @pl.when(b + i >= nbuf)
                    def _():
                        out_copy(b + i - nbuf).wait()

            for size in FFN_GROUPS + (1,):
                @pl.when(n == size)
                def _():
                    ybuf[buf_rows(b, size), :] = expert_mlp(xbuf[buf_rows(b, size), :])

            for i in range(max(FFN_GROUPS)):
                @pl.when(i < n)
                def _():
                    out_copy(b + i).start()
            return b + n, e_new, k_new

        lax.while_loop(lambda c: c[0] < nv, step, (jnp.int32(0), e0, jnp.int32(0)))

        for i in range(nbuf):
            @pl.when(nv > i)
            def _():
                out_copy(nv - 1 - i).wait()

    ybuf[0:ROW_BLOCK, :] = jnp.zeros((ROW_BLOCK, D_MODEL // 2), jnp.int32)

    def fill(b, carry):
        pltpu.sync_copy(ybuf.at[pl.ds(0, ROW_BLOCK)], y_hbm.at[rows_of(b)])
        return carry

    lax.fori_loop(nv, n_blocks, fill, 0)


def _ffn(first_blk, n_blk, nv, xs, w_gate, w_up, w_down):
    n_rows = xs.shape[0]
    n_blocks = n_rows // ROW_BLOCK
    anyspec = pl.BlockSpec(memory_space=pl.ANY)
    return pl.pallas_call(
        functools.partial(_ffn_kernel, n_blocks=n_blocks),
        grid_spec=pltpu.PrefetchScalarGridSpec(
            num_scalar_prefetch=3,
            grid=(1,),
            in_specs=[anyspec, anyspec, anyspec, anyspec],
            out_specs=anyspec,
            scratch_shapes=[pltpu.VMEM((2, D_MODEL, D_FF), F32), pltpu.VMEM((2, D_MODEL, D_FF), F32),
                            pltpu.VMEM((2, D_FF, D_MODEL), F32),
                            pltpu.VMEM((D_MODEL, D_FF), BF16), pltpu.VMEM((D_MODEL, D_FF), BF16),
                            pltpu.VMEM((D_FF, D_MODEL), BF16),
                            pltpu.VMEM((FFN_BUFFERS * ROW_BLOCK, D_MODEL // 2), jnp.int32),
                            pltpu.VMEM((FFN_BUFFERS * ROW_BLOCK, D_MODEL // 2), jnp.int32),
                            pltpu.SemaphoreType.DMA((2, 3 * FFN_WEIGHT_CHUNKS)),
                            pltpu.SemaphoreType.DMA((FFN_BUFFERS,)),
                            pltpu.SemaphoreType.DMA((FFN_BUFFERS,))],
        ),
        out_shape=jax.ShapeDtypeStruct((n_rows, D_MODEL // 2), jnp.int32),
        compiler_params=pltpu.CompilerParams(
            dimension_semantics=("arbitrary",), vmem_limit_bytes=VMEM_LIMIT),
        name="expert_ffn",
    )(first_blk, n_blk, nv, w_gate, w_up, w_down, xs)


def _combine_kernel(h_ref, ya_ref, yb_ref, gate_ref, g_ref, b_ref, o_ref):
    gate = gate_ref[...]
    a_hi, a_lo = _unpack_bf16_halves(ya_ref[...])
    b_hi, b_lo = _unpack_bf16_halves(yb_ref[...])
    g0, g1 = gate[:, 0:1], gate[:, 1:2]
    ffn = jnp.concatenate([a_hi * g0 + b_hi * g1, a_lo * g0 + b_lo * g1], axis=1)
    o_ref[...] = _layer_norm(DEEPNORM_ALPHA * h_ref[...] + ffn, g_ref[...], b_ref[...])


def _combine(h, y2, gate, g, b, tm=512):
    T = h.shape[0]
    nt = T // tm
    return pl.pallas_call(
        _combine_kernel,
        grid=(nt,),
        in_specs=[pl.BlockSpec((tm, D_MODEL), lambda i: (i, 0)),
                  pl.BlockSpec((tm, D_MODEL // 2), lambda i: (i, 0)),
                  pl.BlockSpec((tm, D_MODEL // 2), lambda i: (i + nt, 0)),
                  pl.BlockSpec((tm, 2), lambda i: (i, 0)),
                  pl.BlockSpec((1, D_MODEL), lambda i: (0, 0)),
                  pl.BlockSpec((1, D_MODEL), lambda i: (0, 0))],
        out_specs=pl.BlockSpec((tm, D_MODEL), lambda i: (i, 0)),
        out_shape=jax.ShapeDtypeStruct((T, D_MODEL), F32),
        compiler_params=pltpu.CompilerParams(
            dimension_semantics=("parallel",), vmem_limit_bytes=VMEM_LIMIT),
        name="combine",
    )(h, y2, y2, gate, g, b)


def kernel(x, w_in, gla_gate_w2, gla_gate_b, gla_norm_g, dil_norm_g, w_out, ln1_g, ln1_b,
           router_coarse_w, router_coarse_b, router_fine_w, router_fine_b,
           expert_w_gate, expert_w_up, expert_w_down, ln2_g, ln2_b):
    B, S, D = x.shape
    T = B * S
    depth = w_in.shape[0]
    slopes = jnp.exp2(-8.0 * jnp.arange(1, DIL_HEADS + 1, dtype=F32) / DIL_HEADS)
    n_rows = 2 * T + N_EXPERTS * ROW_BLOCK
    n_blocks = n_rows // ROW_BLOCK
    n_blocks_pad = -(-n_blocks // (2 * LANES)) * (2 * LANES)
    assert n_blocks_pad == 2 * LANES
    a0 = 1536
    h = x.reshape(T, D)
    for l in range(depth):
        w = w_in[l]
        wm = jnp.concatenate([w[:, :a0], w[:, a0 + GLA_GATE_RANK:]], axis=1).astype(BF16)
        wa = jnp.pad(w[:, a0:a0 + GLA_GATE_RANK], ((0, 0), (0, LANES - GLA_GATE_RANK))).astype(BF16)
        w2 = jnp.pad(gla_gate_w2[l], ((0, LANES - GLA_GATE_RANK), (0, 0)))
        w2h, w2l = _split_bf16(w2)
        q, k, v, r, la, dq, dk, dv = _in_proj(h, wm, wa, w2h, w2l, gla_gate_b[l][None, :])
        o_gla = _gla(q, k, v, r, la, gla_norm_g[l][None, :], B, S)
        g2 = jnp.tile(dil_norm_g[l], 2)[None, :]
        o_dil = _dilated(slopes, dq, dk, dv, g2, B, S)
        wo = w_out[l].astype(BF16)
        rw = jnp.concatenate([router_fine_w[l].reshape(D, N_EXPERTS), router_coarse_w[l]], axis=1)
        rw = jnp.pad(rw, ((0, 0), (0, LANES - N_EXPERTS - N_GROUPS))).T
        rwh, rwl = _split_bf16(rw)
        rb = jnp.concatenate([router_fine_b[l].reshape(N_EXPERTS), router_coarse_b[l]])
        rb = jnp.pad(rb, (0, LANES - N_EXPERTS - N_GROUPS))[:, None]
        h1, h1p, eid_t, gate, cnt = _out_proj(o_gla, o_dil, h, wo[:GLA_WIDTH], wo[GLA_WIDTH:],
                                              ln1_g[l][None, :], ln1_b[l][None, :], rwh, rwl, rb)
        dest_t, be, nv = _positions(eid_t, cnt, n_blocks_pad)
        dest_flat = dest_t.reshape(2 * T)
        src_tok = _sc_inverse_rows(dest_flat, n_rows)
        xs = _sc_gather_rows(h1p, src_tok)
        be = be.reshape(n_blocks_pad)
        y = _ffn(be[:N_EXPERTS], be[LANES:LANES + N_EXPERTS], nv.reshape(LANES)[:1], xs,
                 expert_w_gate[l], expert_w_up[l], expert_w_down[l])
        y2 = _sc_gather_rows(y, dest_flat)
        h = _combine(h1, y2, gate, ln2_g[l][None, :], ln2_b[l][None, :])
    return h.reshape(B, S, D)
```

```python
import functools
import math

import jax
import jax.numpy as jnp
import numpy as np
from jax import lax
from jax.experimental import pallas as pl
from jax.experimental.pallas import tpu as pltpu
from jax.experimental.pallas import tpu_sc as plsc

D_MODEL = 1024
GLA_HEADS = 4
GLA_DK = 64
GLA_DV = 128
GLA_KEY_WIDTH = GLA_HEADS * GLA_DK
GLA_WIDTH = GLA_HEADS * GLA_DV
GLA_GATE_RANK = 16
GLA_GATE_TEMP = 16.0
DIL_HEADS = 8
DIL_DH = 64
DIL_WIDTH = DIL_HEADS * DIL_DH
DIL_CONFIGS = ((128, 1), (512, 4), (2048, 16))
DIL_BLOCK = 128
DIL_MAX_R = max(r for _, r in DIL_CONFIGS)
DIL_PAD = DIL_BLOCK * DIL_MAX_R
DIL_UNROLL = 16
N_GROUPS = 4
EXPERTS_PER_GROUP = 8
N_EXPERTS = N_GROUPS * EXPERTS_PER_GROUP
D_FF = 512
DEEPNORM_ALPHA = 2.0 ** 0.25
EPS = 1e-5
LOG2E = math.log2(math.e)

LANES = 128
GLA_CHUNK = 128
GLA_SUB = 64
GLA_UNROLL = 4
SC_LANES = 16
SC_INDEX_WINDOW = 128
SC_GATHER_BUFFERS = 6
SC_GATHER_ROWS = 32
FFN_BUFFERS = 12
FFN_GROUPS = (4, 2)
FFN_WEIGHT_CHUNKS = 4
ROW_BLOCK = 256
VMEM_LIMIT = 56 * 1024 * 1024

F32 = jnp.float32
BF16 = jnp.bfloat16


def _dot(a, b):
    return jnp.dot(a, b, preferred_element_type=F32)


def _dot_nt(a, b):
    return lax.dot_general(a, b, (((1,), (1,)), ((), ())), preferred_element_type=F32)


def _dot_tn(a, b):
    return lax.dot_general(a, b, (((0,), (0,)), ((), ())), preferred_element_type=F32)


def _split_bf16(v):
    hi = v.astype(BF16)
    lo = (v - hi.astype(F32)).astype(BF16)
    return hi, lo


def _pack_bf16_halves(v):
    w = v.shape[1] // 2
    hi = lax.bitcast_convert_type(v[:, :w].astype(BF16).astype(F32), jnp.int32)
    lo = lax.bitcast_convert_type(v[:, w:].astype(BF16).astype(F32), jnp.int32)
    return hi | lax.shift_right_logical(lo, 16)


def _unpack_bf16_halves(words):
    hi = lax.bitcast_convert_type(words & jnp.int32(-65536), F32)
    lo = lax.bitcast_convert_type(lax.shift_left(words, 16), F32)
    return hi, lo


def _layer_norm(v, g, b):
    mu = jnp.mean(v, axis=-1, keepdims=True)
    c = v - mu
    var = jnp.mean(c * c, axis=-1, keepdims=True)
    return c * lax.rsqrt(var + EPS) * g + b


def _in_proj_kernel(x_ref, wg_ref, wd_ref, wa_ref, w2h_ref, w2l_ref, gb_ref,
                    q_ref, k_ref, v_ref, r_ref, la_ref, dq_ref, dk_ref, dv_ref):
    xb = x_ref[...].astype(BF16)

    def piece(w_ref, c0, c1):
        return _dot(xb, w_ref[:, c0:c1])

    q_ref[...] = (piece(wg_ref, 0, 256) * (GLA_DK ** -0.5)).astype(BF16)
    k_ref[...] = piece(wg_ref, 256, 512).astype(BF16)
    v_ref[...] = piece(wg_ref, 512, 1024).astype(BF16)
    r_ref[...] = piece(wg_ref, 1024, 1536).astype(BF16)
    dq_ref[...] = (piece(wd_ref, 0, 512) * (DIL_DH ** -0.5 * LOG2E)).astype(BF16)
    dk_ref[...] = piece(wd_ref, 512, 1024).astype(BF16)
    dv_ref[...] = piece(wd_ref, 1024, 1536).astype(BF16)
    ga = _dot(xb, wa_ref[...])
    ga_hi, ga_lo = _split_bf16(ga)
    z = _dot(ga_hi, w2h_ref[...]) + _dot(ga_lo, w2h_ref[...]) + _dot(ga_hi, w2l_ref[...]) + gb_ref[...]
    log_sig = jnp.minimum(z, 0.0) - jnp.log1p(jnp.exp(-jnp.abs(z)))
    la_ref[...] = log_sig * (1.0 / GLA_GATE_TEMP)


def _in_proj(x2, wg, wd, wa, w2h, w2l, gb, tm=1024):
    T = x2.shape[0]
    row = lambda w: pl.BlockSpec((tm, w), lambda i: (i, 0))
    full = lambda a: pl.BlockSpec(a.shape, lambda i: (0,) * a.ndim)
    outs = [(256, BF16), (256, BF16), (512, BF16), (512, BF16), (256, F32),
            (512, BF16), (512, BF16), (512, BF16)]
    return pl.pallas_call(
        _in_proj_kernel,
        grid=(T // tm,),
        in_specs=[row(D_MODEL), full(wg), full(wd), full(wa), full(w2h), full(w2l), full(gb)],
        out_specs=[row(w) for w, _ in outs],
        out_shape=[jax.ShapeDtypeStruct((T, w), dt) for w, dt in outs],
        compiler_params=pltpu.CompilerParams(
            dimension_semantics=("parallel",), vmem_limit_bytes=VMEM_LIMIT),
        name="in_proj",
    )(x2, wg, wd, wa, w2h, w2l, gb)


def _gla_kernel(q_ref, k_ref, v_ref, r_ref, la_ref, g_ref, o_ref, s_ref, *, seq_block):
    C = GLA_CHUNK
    H = GLA_SUB
    assert C == 2 * H

    @pl.when(pl.program_id(1) == 0)
    def _():
        s_ref[...] = jnp.zeros_like(s_ref)

    ri = lax.broadcasted_iota(jnp.int32, (C, C), 0)
    ci = lax.broadcasted_iota(jnp.int32, (C, C), 1)
    same_sub = (ri // H) == (ci // H)
    sum_ops = jnp.concatenate([jnp.logical_and(same_sub, ci <= ri).astype(BF16), same_sub.astype(BF16),
                               jnp.ones((C, C), BF16)], axis=0)
    diag_mask = jnp.logical_and(same_sub, ci <= ri)
    off_mask = (ri // H) > (ci // H)
    second = lax.broadcasted_iota(jnp.int32, (C, 1), 0) >= H
    ones_cl = jnp.ones((C, LANES), BF16)
    lane_k = lax.broadcasted_iota(jnp.int32, (1, GLA_KEY_WIDTH), 1) // GLA_DK
    head_masks = [(lane_k == h).astype(F32) for h in range(GLA_HEADS)]
    srow = lax.broadcasted_iota(jnp.int32, (GLA_KEY_WIDTH, GLA_WIDTH), 0) // GLA_DK
    scol = lax.broadcasted_iota(jnp.int32, (GLA_KEY_WIDTH, GLA_WIDTH), 1) // GLA_DV
    state_mask = (srow == scol).astype(F32)
    g = g_ref[...]

    def trip(t, carry):
        U = GLA_UNROLL
        rows = [pl.ds(pl.multiple_of((t * U + u) * C, C), C) for u in range(U)]
        la2s, sums = [], []
        for u in range(U):
            la_hi, la_lo = _split_bf16(la_ref[rows[u], :])
            la2 = jnp.concatenate([la_hi, la_lo], axis=1)
            la2s.append(la2)
            sm = _dot(sum_ops, la2)
            sums.append(sm[:, 0:GLA_KEY_WIDTH] + sm[:, GLA_KEY_WIDTH:])
        q_states, k_states, scores = [], [], []
        for u in range(U):
            b = sums[u][0:C]
            t_sub = sums[u][C:2 * C]
            other = sums[u][2 * C:3 * C] - t_sub
            q = q_ref[rows[u], :].astype(F32)
            k = k_ref[rows[u], :].astype(F32)
            qd = q * jnp.exp(b)
            kd = (k * jnp.exp(-b)).astype(BF16)
            ke_f = k * jnp.exp(t_sub - b)
            cross = jnp.exp(other)
            q_states.append((qd * jnp.where(second, cross, 1.0)).astype(BF16))
            k_states.append((ke_f * jnp.where(second, 1.0, cross)).astype(BF16))
            q_heads = jnp.concatenate([(qd * head_masks[h]).astype(BF16) for h in range(GLA_HEADS)], axis=0)
            keys2 = jnp.concatenate([kd, ke_f.astype(BF16)], axis=0)
            scores.append(_dot_nt(q_heads, keys2))
        decs, upds = [], []
        for u in range(U):
            tot = _dot_tn(la2s[u], ones_cl)
            decs.append(jnp.exp(tot[0:GLA_KEY_WIDTH] + tot[GLA_KEY_WIDTH:]))
            upds.append(_dot_tn(k_states[u], v_ref[rows[u], :]) * state_mask)
        o_inters = []
        for u in range(U):
            state = s_ref[...]
            o_inters.append(_dot(q_states[u], state.astype(BF16)))
            for h in range(GLA_HEADS):
                cols = slice(h * GLA_DV, (h + 1) * GLA_DV)
                s_ref[:, cols] = state[:, cols] * decs[u] + upds[u][:, cols]
        for u in range(U):
            v = v_ref[rows[u], :]
            outs = []
            for h in range(GLA_HEADS):
                sh = scores[u][h * C:(h + 1) * C]
                a = jnp.where(diag_mask, sh[:, 0:C], 0.0) + jnp.where(off_mask, sh[:, C:2 * C], 0.0)
                cols = slice(h * GLA_DV, (h + 1) * GLA_DV)
                o = _dot(a.astype(BF16), v[:, cols]) + o_inters[u][:, cols]
                o = o * lax.rsqrt(jnp.mean(o * o, axis=-1, keepdims=True) + EPS) * g
                outs.append(o)
            o_all = jnp.concatenate(outs, axis=-1)
            rr = r_ref[rows[u], :].astype(F32)
            o_ref[rows[u], :] = (o_all * (rr * jax.nn.sigmoid(rr))).astype(BF16)
        return carry

    lax.fori_loop(0, seq_block // C // GLA_UNROLL, trip, 0)


def _gla(q, k, v, r, la, g, batch, seq, seq_block=1024):
    nsb = seq // seq_block
    row = lambda w: pl.BlockSpec((seq_block, w), lambda b, s: (b * nsb + s, 0))
    return pl.pallas_call(
        functools.partial(_gla_kernel, seq_block=seq_block),
        grid=(batch, nsb),
        in_specs=[row(256), row(256), row(512), row(512), row(256),
                  pl.BlockSpec((1, GLA_DV), lambda b, s: (0, 0))],
        out_specs=row(512),
        out_shape=jax.ShapeDtypeStruct((batch * seq, GLA_WIDTH), BF16),
        scratch_shapes=[pltpu.VMEM((GLA_KEY_WIDTH, GLA_WIDTH), F32)],
        compiler_params=pltpu.CompilerParams(
            dimension_semantics=("parallel", "arbitrary"), vmem_limit_bytes=VMEM_LIMIT),
        name="gla",
    )(q, k, v, r, la, g)


def _dil_kernel(slope_ref, q_ref, k_ref, v_ref, g_ref, o_ref,
                qf, kf, vf, kd, va, vb, oc, lc, *, seq):
    B = DIL_BLOCK
    U = DIL_UNROLL
    pair = pl.program_id(1)
    qf[...] = q_ref[...].astype(F32)
    kf[...] = k_ref[...].astype(F32)
    vf[...] = v_ref[...].astype(F32)

    lane = lax.broadcasted_iota(jnp.int32, (1, LANES), 1)
    first = lane < DIL_DH
    ii = lax.broadcasted_iota(jnp.int32, (B, B), 0)
    jj = lax.broadcasted_iota(jnp.int32, (B, B), 1)
    upper = jj > ii
    eye = jj == ii
    dist = jnp.bitwise_and(ii - jj, B - 1).astype(F32)
    neg = jnp.float32(-jnp.inf)
    neg_tile = jnp.full((B, B), neg, F32)
    zero_tile = jnp.zeros((B, LANES), BF16)

    for cfg, (window, r) in enumerate(DIL_CONFIGS):
        nb = seq // r // B
        cs = nb + 1
        bias_prev, bias_cur = [], []
        for hh in range(2):
            slope = slope_ref[2 * pair + hh] * (float(r) * LOG2E)
            bias = dist * (-slope)
            bias_prev.append(jnp.where(upper, bias, jnp.where(eye, -slope * float(B), neg)))
            bias_cur.append(jnp.where(upper, neg, bias))

        for c in range(r):
            rows0 = slice(c * cs * B, (c * cs + 1) * B)
            kd[rows0, :] = zero_tile
            va[rows0, :] = zero_tile
            vb[rows0, :] = zero_tile

        def prep(t4, carry, r=r, nb=nb, cs=cs):
            for j in range(4):
                t = t4 * 4 + j
                c = t // nb
                n = t % nb
                start = c + n * (B * r)
                rows = pl.ds(start, B, stride=r) if r > 1 else pl.ds(pl.multiple_of(start, B), B)
                dst = pl.ds(pl.multiple_of((c * cs + 1 + n) * B, B), B)
                kd[dst, :] = kf[rows, :].astype(BF16)
                v = vf[rows, :]
                va[dst, :] = jnp.where(first, v, 1.0).astype(BF16)
                vb[dst, :] = jnp.where(first, 1.0, v).astype(BF16)
            return carry

        lax.fori_loop(0, seq // B // 4, prep, 0)

        def geom(it, r=r, nb=nb, cs=cs):
            c = it // nb
            n = it % nb
            start = c + n * (B * r)
            rows = pl.ds(start, B, stride=r) if r > 1 else pl.ds(pl.multiple_of(start, B), B)
            kv = pl.ds(pl.multiple_of((c * cs + n) * B, B), 2 * B)
            return rows, kv

        def body(t, carry, cfg=cfg, nb=nb, bias_prev=bias_prev, bias_cur=bias_cur, geom=geom):
            geo = [geom(t * U + u) for u in range(U)]
            scores = []
            for u in range(U):
                rows, kv = geo[u]
                q = qf[rows, :]
                q_heads = jnp.concatenate([jnp.where(first, q, 0.0), jnp.where(first, 0.0, q)], axis=0)
                scores.append(_dot_nt(q_heads.astype(BF16), kd[kv, :]))
            probs, maxes = [], []
            for u in range(U):
                if nb % U == 0:
                    has_prev = True if u > 0 else (t * U) % nb > 0
                else:
                    assert U % nb == 0
                    has_prev = (u % nb) > 0
                for hh in range(2):
                    s2 = scores[u][hh * B:(hh + 1) * B]
                    if has_prev is True:
                        bp = bias_prev[hh]
                    elif has_prev is False:
                        bp = neg_tile
                    else:
                        bp = jnp.where(has_prev, bias_prev[hh], neg)
                    s_prev = s2[:, 0:B] + bp
                    s_cur = s2[:, B:2 * B] + bias_cur[hh]
                    m = jnp.max(jnp.maximum(s_prev, s_cur), axis=-1, keepdims=True)
                    probs.append(jnp.concatenate([jnp.exp2(s_prev - m), jnp.exp2(s_cur - m)], axis=1).astype(BF16))
                    maxes.append(m)
            for u in range(U):
                rows, kv = geo[u]
                acc0 = _dot(probs[2 * u], va[kv, :])
                acc1 = _dot(probs[2 * u + 1], vb[kv, :])
                num = jnp.where(first, acc0, acc1)
                den = pltpu.roll(jnp.where(first, acc1, acc0), DIL_DH, axis=1)
                oc[cfg, rows, :] = num * (1.0 / den)
                lc[cfg, rows, :] = jnp.where(first, maxes[2 * u], maxes[2 * u + 1]) + jnp.log2(den)
            return carry

        lax.fori_loop(0, seq // B // U, body, 0)

    g = g_ref[...]
    CH = 512

    def mix(i, carry):
        rows = pl.ds(pl.multiple_of(i * CH, CH), CH)
        l0, l1, l2 = lc[0, rows, :], lc[1, rows, :], lc[2, rows, :]
        m = jnp.maximum(jnp.maximum(l0, l1), l2)
        e0, e1, e2 = jnp.exp2(l0 - m), jnp.exp2(l1 - m), jnp.exp2(l2 - m)
        inv = 1.0 / (e0 + e1 + e2)
        o = (e0 * oc[0, rows, :] + e1 * oc[1, rows, :] + e2 * oc[2, rows, :]) * inv
        sq = o * o
        ms_a = jnp.sum(jnp.where(first, sq, 0.0), axis=-1, keepdims=True) * (1.0 / DIL_DH)
        ms_b = jnp.sum(jnp.where(first, 0.0, sq), axis=-1, keepdims=True) * (1.0 / DIL_DH)
        ms = jnp.where(first, ms_a, ms_b)
        o_ref[rows, :] = (o * lax.rsqrt(ms + EPS) * g).astype(BF16)
        return carry

    lax.fori_loop(0, seq // CH, mix, 0)


def _dilated(slopes, dq, dk, dv, g2, batch, seq):
    blk = pl.BlockSpec((seq, LANES), lambda b, p, s: (b, p))
    return pl.pallas_call(
        functools.partial(_dil_kernel, seq=seq),
        grid_spec=pltpu.PrefetchScalarGridSpec(
            num_scalar_prefetch=1,
            grid=(batch, DIL_WIDTH // LANES),
            in_specs=[blk, blk, blk, pl.BlockSpec((1, LANES), lambda b, p, s: (0, 0))],
            out_specs=blk,
            scratch_shapes=[pltpu.VMEM((seq, LANES), F32)] * 3
                           + [pltpu.VMEM((seq + DIL_PAD, LANES), BF16)] * 3
                           + [pltpu.VMEM((3, seq, LANES), F32)] * 2,
        ),
        out_shape=jax.ShapeDtypeStruct((batch * seq, DIL_WIDTH), BF16),
        compiler_params=pltpu.CompilerParams(
            dimension_semantics=("parallel", "parallel"), vmem_limit_bytes=VMEM_LIMIT),
        name="dilated",
    )(slopes, dq, dk, dv, g2)


def _out_proj_kernel(og_ref, od_ref, x_ref, wg_ref, wd_ref, g_ref, b_ref,
                     rwh_ref, rwl_ref, rb_ref, h_ref, hp_ref, eid_ref, gate_ref, cnt_ref):
    mix = _dot(og_ref[...], wg_ref[...]) + _dot(od_ref[...], wd_ref[...])
    h = _layer_norm(DEEPNORM_ALPHA * x_ref[...] + mix, g_ref[...], b_ref[...])
    h_ref[...] = h
    hp_ref[...] = _pack_bf16_halves(h)
    h_hi, h_lo = _split_bf16(h)
    lt = (_dot_nt(rwh_ref[...], h_hi) + _dot_nt(rwh_ref[...], h_lo) + _dot_nt(rwl_ref[...], h_hi)
          + rb_ref[...])
    tm = lt.shape[1]
    row = lax.broadcasted_iota(jnp.int32, (EXPERTS_PER_GROUP, tm), 0).astype(F32)
    neg = jnp.float32(-jnp.inf)
    big = jnp.float32(1e9)
    coarse = jnp.where(row < N_GROUPS, lt[N_EXPERTS:N_EXPERTS + EXPERTS_PER_GROUP, :], neg)
    cmax = jnp.max(coarse, axis=0, keepdims=True)
    g_idx = jnp.min(jnp.where(coarse == cmax, row, big), axis=0, keepdims=True)
    p_group = 1.0 / jnp.sum(jnp.exp(coarse - cmax), axis=0, keepdims=True)
    fine = lt[(N_GROUPS - 1) * EXPERTS_PER_GROUP:N_EXPERTS, :]
    for g in range(N_GROUPS - 2, -1, -1):
        fine = jnp.where(g_idx == g, lt[g * EXPERTS_PER_GROUP:(g + 1) * EXPERTS_PER_GROUP, :], fine)
    v1 = jnp.max(fine, axis=0, keepdims=True)
    i1 = jnp.min(jnp.where(fine == v1, row, big), axis=0, keepdims=True)
    fine2 = jnp.where(row == i1, neg, fine)
    v2 = jnp.max(fine2, axis=0, keepdims=True)
    i2 = jnp.min(jnp.where(fine2 == v2, row, big), axis=0, keepdims=True)
    e2 = jnp.exp(v2 - v1)
    den = 1.0 + e2
    gate1 = p_group * (1.0 / den)
    gate2 = p_group * (e2 / den)
    id1 = g_idx * EXPERTS_PER_GROUP + i1
    id2 = g_idx * EXPERTS_PER_GROUP + i2
    eid_ref[...] = jnp.concatenate([id1, id2], axis=0).astype(jnp.int32)
    slab = jnp.concatenate([gate1, gate2, jnp.zeros((LANES - 2, tm), F32)], axis=0)
    gate_ref[...] = slab.T[:, 0:2]
    sub = lax.broadcasted_iota(jnp.int32, (LANES, tm), 0).astype(F32)
    onehot = jnp.logical_or(sub == id1, sub == id2).astype(BF16)

    @pl.when(pl.program_id(0) == 0)
    def _():
        cnt_ref[...] = jnp.zeros_like(cnt_ref)

    cnt_ref[...] += _dot(onehot, jnp.ones((tm, LANES), BF16))


def _out_proj(og, od, x2, wg, wd, g, b, rwh, rwl, rb, tm=1024):
    T = x2.shape[0]
    row = lambda w: pl.BlockSpec((tm, w), lambda i: (i, 0))
    full = lambda a: pl.BlockSpec(a.shape, lambda i: (0,) * a.ndim)
    return pl.pallas_call(
        _out_proj_kernel,
        grid=(T // tm,),
        in_specs=[row(512), row(512), row(D_MODEL), full(wg), full(wd), full(g), full(b),
                  full(rwh), full(rwl), full(rb)],
        out_specs=[row(D_MODEL), row(D_MODEL // 2), pl.BlockSpec((2, tm), lambda i: (0, i)), row(2),
                   pl.BlockSpec((LANES, LANES), lambda i: (0, 0))],
        out_shape=[jax.ShapeDtypeStruct((T, D_MODEL), F32),
                   jax.ShapeDtypeStruct((T, D_MODEL // 2), jnp.int32),
                   jax.ShapeDtypeStruct((2, T), jnp.int32),
                   jax.ShapeDtypeStruct((T, 2), F32),
                   jax.ShapeDtypeStruct((LANES, LANES), F32)],
        compiler_params=pltpu.CompilerParams(
            dimension_semantics=("arbitrary",), vmem_limit_bytes=VMEM_LIMIT),
        name="out_proj_router",
    )(og, od, x2, wg, wd, g, b, rwh, rwl, rb)


def _positions_kernel(eid_ref, cnt_ref, dest_ref, be_ref, nv_ref, carry_ref, sp_ref, tri_ref, *, tb):
    i = pl.program_id(0)

    @pl.when(i == 0)
    def _():
        shift = int(math.log2(ROW_BLOCK))
        nb_col = (cnt_ref[...].astype(jnp.int32) + (ROW_BLOCK - 1)) >> shift
        r = lax.broadcasted_iota(jnp.int32, (LANES, LANES), 0)
        c = lax.broadcasted_iota(jnp.int32, (LANES, LANES), 1)
        nb_f = jnp.where(r < N_EXPERTS, nb_col, 0).astype(F32)
        start_col = _dot((c < r).astype(BF16), nb_f.astype(BF16))
        sp_ref[...] = start_col * float(ROW_BLOCK)
        carry_ref[...] = jnp.zeros_like(carry_ref)
        be_ref[...] = jnp.concatenate([start_col.T[0:1, :], nb_f.T[0:1, :]], axis=1).astype(jnp.int32)
        total = jnp.sum(nb_f[:, 0:1], axis=0, keepdims=True)
        nv_ref[...] = jnp.broadcast_to(total, (1, LANES)).astype(jnp.int32)
        tr = lax.broadcasted_iota(jnp.int32, (tb, tb), 0)
        tc = lax.broadcasted_iota(jnp.int32, (tb, tb), 1)
        tri_ref[...] = (tr < tc).astype(BF16)

    sub = lax.broadcasted_iota(jnp.int32, (LANES, tb), 0)
    oh1 = sub == eid_ref[0:1, :]
    oh2 = sub == eid_ref[1:2, :]
    oh = jnp.logical_or(oh1, oh2).astype(BF16)
    offset = jnp.tile(carry_ref[...] + sp_ref[...], (1, tb // LANES))
    before = _dot(oh, tri_ref[...]) + offset
    d1 = jnp.sum(jnp.where(oh1, before, 0.0), axis=0, keepdims=True)
    d2 = jnp.sum(jnp.where(oh2, before, 0.0), axis=0, keepdims=True)
    dest_ref[...] = jnp.concatenate([d1, d2], axis=0).astype(jnp.int32)
    carry_ref[...] += _dot(oh, jnp.ones((tb, LANES), BF16))


def _positions(eid_t, cnt, n_blocks_pad, tb=1024):
    T = eid_t.shape[1]
    return pl.pallas_call(
        functools.partial(_positions_kernel, tb=tb),
        grid=(T // tb,),
        in_specs=[pl.BlockSpec((2, tb), lambda i: (0, i)), pl.BlockSpec((LANES, LANES), lambda i: (0, 0))],
        out_specs=[pl.BlockSpec((2, tb), lambda i: (0, i)),
                   pl.BlockSpec((1, n_blocks_pad), lambda i: (0, 0)),
                   pl.BlockSpec((1, LANES), lambda i: (0, 0))],
        out_shape=[jax.ShapeDtypeStruct((2, T), jnp.int32),
                   jax.ShapeDtypeStruct((1, n_blocks_pad), jnp.int32),
                   jax.ShapeDtypeStruct((1, LANES), jnp.int32)],
        scratch_shapes=[pltpu.VMEM((LANES, LANES), F32), pltpu.VMEM((LANES, LANES), F32),
                        pltpu.VMEM((tb, tb), BF16)],
        compiler_params=pltpu.CompilerParams(dimension_semantics=("arbitrary",)),
        name="positions",
    )(eid_t, cnt)


def _sc_gather_rows(table, idx):
    n = idx.shape[0]
    d = table.shape[1]
    info = plsc.get_sparse_core_info()
    nc, ns = info.num_cores, info.num_subcores
    per_w = n // (nc * ns)
    assert per_w * nc * ns == n and per_w % SC_INDEX_WINDOW == 0
    mesh = plsc.VectorSubcoreMesh(core_axis_name="core", subcore_axis_name="subcore")
    nchunk = per_w // SC_GATHER_ROWS
    nbuf = SC_GATHER_BUFFERS

    @functools.partial(
        pl.kernel, out_type=jax.ShapeDtypeStruct((n, d), table.dtype), mesh=mesh,
        scratch_types=[pltpu.VMEM((per_w,), jnp.int32),
                       pltpu.VMEM((nbuf, SC_GATHER_ROWS, d), table.dtype),
                       pltpu.SemaphoreType.DMA((nbuf,)), pltpu.SemaphoreType.DMA((nbuf,))],
        name="sc_gather_rows")
    def gather(x_hbm, i_hbm, o_hbm, idx_v, buf, gsem, wsem):
        wid = lax.axis_index("subcore") * nc + lax.axis_index("core")
        base = wid * per_w
        pltpu.sync_copy(i_hbm.at[pl.ds(base, per_w)], idx_v)

        def gather_copy(c):
            rows = idx_v.at[pl.ds(c * SC_GATHER_ROWS, SC_GATHER_ROWS)]
            return pltpu.make_async_copy(x_hbm.at[rows], buf.at[c % nbuf], gsem.at[c % nbuf])

        def write_copy(c):
            dst = o_hbm.at[pl.ds(base + c * SC_GATHER_ROWS, SC_GATHER_ROWS)]
            return pltpu.make_async_copy(buf.at[c % nbuf], dst, wsem.at[c % nbuf])

        for c in range(min(nbuf - 1, nchunk)):
            gather_copy(c).start()
        for c in range(nchunk):
            gather_copy(c).wait()
            write_copy(c).start()
            if c + nbuf - 1 < nchunk:
                if c >= 1:
                    write_copy(c - 1).wait()
                gather_copy(c + nbuf - 1).start()
        for c in range(max(0, nchunk - nbuf), nchunk):
            write_copy(c).wait()

    return gather(table, idx)


def _sc_inverse_rows(dest_flat, n_rows, chunk=2048):
    n = dest_flat.shape[0]
    n_tokens = n // 2
    assert n_rows <= 3 * n_tokens
    nc = plsc.get_sparse_core_info().num_cores
    mesh = plsc.VectorSubcoreMesh(core_axis_name="core", subcore_axis_name="subcore")

    @functools.partial(
        pl.kernel, out_type=jax.ShapeDtypeStruct((n_rows,), jnp.int32), mesh=mesh,
        scratch_types=[pltpu.VMEM((n_rows,), jnp.int32), pltpu.VMEM((chunk,), jnp.int32)],
        compiler_params=pltpu.CompilerParams(needs_layout_passes=False),
        name="sc_inverse_rows")
    def inverse(d_hbm, o_hbm, inv_v, d_v):
        wid = lax.axis_index("subcore") * nc + lax.axis_index("core")

        @pl.when(wid == 0)
        def _():
            lanes = lax.iota(jnp.int32, SC_LANES)

            @plsc.parallel_loop(0, n_rows // SC_LANES, unroll=8)
            def _(i):
                r = lanes + i * SC_LANES
                r = jnp.where(r >= n_tokens, r - n_tokens, r)
                inv_v[pl.ds(i * SC_LANES, SC_LANES)] = jnp.where(r >= n_tokens, r - n_tokens, r)

            @pl.loop(0, n // chunk)
            def _(c):
                pltpu.sync_copy(d_hbm.at[pl.ds(c * chunk, chunk)], d_v)

                @plsc.parallel_loop(0, chunk // SC_LANES, unroll=8)
                def _(j):
                    rows = d_v[pl.ds(j * SC_LANES, SC_LANES)]
                    pair = lax.iota(jnp.int32, SC_LANES) + (c * chunk + j * SC_LANES)
                    plsc.store_scatter(inv_v, [rows], jnp.where(pair >= n_tokens, pair - n_tokens, pair))

            pltpu.sync_copy(inv_v, o_hbm)

    return inverse(dest_flat)


def _ffn_kernel(first_ref, count_ref, nv_ref, wg_hbm, wu_hbm, wd_hbm, xs_hbm, y_hbm,
                wg32, wu32, wd32, wgb, wub, wdb, xbuf, ybuf, wsem, isem, osem, *, n_blocks):
    nv = nv_ref[0]
    nbuf = FFN_BUFFERS

    def next_expert(e):
        def more(t):
            return jnp.logical_and(t < N_EXPERTS, count_ref[jnp.minimum(t, N_EXPERTS - 1)] == 0)
        return lax.while_loop(more, lambda t: t + 1, e + 1)

    def weight_copies(e, slot):
        ee = jnp.minimum(e, N_EXPERTS - 1)
        copies = []
        for i, (src, dst) in enumerate(((wg_hbm, wg32), (wu_hbm, wu32), (wd_hbm, wd32))):
            rows = src.shape[1] // FFN_WEIGHT_CHUNKS
            for c in range(FFN_WEIGHT_CHUNKS):
                part = pl.ds(c * rows, rows)
                copies.append(pltpu.make_async_copy(src.at[ee, part], dst.at[slot, part],
                                                    wsem.at[slot, i * FFN_WEIGHT_CHUNKS + c]))
        return copies

    def fetch_weights(e, slot):
        @pl.when(e < N_EXPERTS)
        def _():
            for c in weight_copies(e, slot):
                c.start()

    def take_weights(e, slot):
        for c in weight_copies(e, slot):
            c.wait()
        wgb[...] = wg32[slot].astype(BF16)
        wub[...] = wu32[slot].astype(BF16)
        wdb[...] = wd32[slot].astype(BF16)
        fetch_weights(next_expert(next_expert(e)), slot)

    def rows_of(b):
        return pl.ds(pl.multiple_of(b * ROW_BLOCK, ROW_BLOCK), ROW_BLOCK)

    def buf_rows(b, nblk):
        return pl.ds(pl.multiple_of((b % nbuf) * ROW_BLOCK, ROW_BLOCK), nblk * ROW_BLOCK)

    def in_copy(b):
        return pltpu.make_async_copy(xs_hbm.at[rows_of(b)], xbuf.at[buf_rows(b, 1)], isem.at[b % nbuf])

    def out_copy(b):
        return pltpu.make_async_copy(ybuf.at[buf_rows(b, 1)], y_hbm.at[rows_of(b)], osem.at[b % nbuf])

    def expert_mlp(words):
        x_hi, x_lo = _unpack_bf16_halves(words)
        xb = jnp.concatenate([x_hi.astype(BF16), x_lo.astype(BF16)], axis=1)
        a = _dot(xb, wgb[...])
        u = _dot(xb, wub[...])
        hid = (a * jax.nn.sigmoid(a) * u).astype(BF16)
        return _pack_bf16_halves(_dot(hid, wdb[...]))

    ahead = nbuf - max(FFN_GROUPS)

    @pl.when(nv > 0)
    def _():
        e0 = next_expert(jnp.int32(-1))
        for i in range(ahead):
            @pl.when(i < nv)
            def _():
                in_copy(i).start()
        fetch_weights(e0, 0)
        fetch_weights(next_expert(e0), 1)
        take_weights(e0, 0)

        def step(carry):
            b, e, k = carry
            switch = b >= first_ref[e] + count_ref[e]
            e_new = jnp.where(switch, next_expert(e), e)
            k_new = jnp.where(switch, k + 1, k)

            @pl.when(switch)
            def _():
                take_weights(e_new, k_new % 2)

            end = first_ref[e_new] + count_ref[e_new]
            n = jnp.int32(1)
            for size in FFN_GROUPS:
                fits = jnp.logical_and(b + size <= end, b % nbuf + size <= nbuf)
                n = jnp.where(jnp.logical_and(n == 1, fits), size, n)
            for i in range(max(FFN_GROUPS)):
                @pl.when(jnp.logical_and(i < n, b + ahead + i < nv))
                def _():
                    in_copy(b + ahead + i).start()
            for i in range(max(FFN_GROUPS)):
                @pl.when(i < n)
                def _():
                    in_copy(b + i).wait()

                    @pl.when(b + i >= nbuf)
                    def _():
                        out_copy(b + i - nbuf).wait()

            for size in FFN_GROUPS + (1,):
                @pl.when(n == size)
                def _():
                    ybuf[buf_rows(b, size), :] = expert_mlp(xbuf[buf_rows(b, size), :])

            for i in range(max(FFN_GROUPS)):
                @pl.when(i < n)
                def _():
                    out_copy(b + i).start()
            return b + n, e_new, k_new

        lax.while_loop(lambda c: c[0] < nv, step, (jnp.int32(0), e0, jnp.int32(0)))

        for i in range(nbuf):
            @pl.when(nv > i)
            def _():
                out_copy(nv - 1 - i).wait()

    ybuf[0:ROW_BLOCK, :] = jnp.zeros((ROW_BLOCK, D_MODEL // 2), jnp.int32)

    def fill(b, carry):
        pltpu.sync_copy(ybuf.at[pl.ds(0, ROW_BLOCK)], y_hbm.at[rows_of(b)])
        return carry

    lax.fori_loop(nv, n_blocks, fill, 0)


def _ffn(first_blk, n_blk, nv, xs, w_gate, w_up, w_down):
    n_rows = xs.shape[0]
    n_blocks = n_rows // ROW_BLOCK
    anyspec = pl.BlockSpec(memory_space=pl.ANY)
    return pl.pallas_call(
        functools.partial(_ffn_kernel, n_blocks=n_blocks),
        grid_spec=pltpu.PrefetchScalarGridSpec(
            num_scalar_prefetch=3,
            grid=(1,),
            in_specs=[anyspec, anyspec, anyspec, anyspec],
            out_specs=anyspec,
            scratch_shapes=[pltpu.VMEM((2, D_MODEL, D_FF), F32), pltpu.VMEM((2, D_MODEL, D_FF), F32),
                            pltpu.VMEM((2, D_FF, D_MODEL), F32),
                            pltpu.VMEM((D_MODEL, D_FF), BF16), pltpu.VMEM((D_MODEL, D_FF), BF16),
                            pltpu.VMEM((D_FF, D_MODEL), BF16),
                            pltpu.VMEM((FFN_BUFFERS * ROW_BLOCK, D_MODEL // 2), jnp.int32),
                            pltpu.VMEM((FFN_BUFFERS * ROW_BLOCK, D_MODEL // 2), jnp.int32),
                            pltpu.SemaphoreType.DMA((2, 3 * FFN_WEIGHT_CHUNKS)),
                            pltpu.SemaphoreType.DMA((FFN_BUFFERS,)),
                            pltpu.SemaphoreType.DMA((FFN_BUFFERS,))],
        ),
        out_shape=jax.ShapeDtypeStruct((n_rows, D_MODEL // 2), jnp.int32),
        compiler_params=pltpu.CompilerParams(
            dimension_semantics=("arbitrary",), vmem_limit_bytes=VMEM_LIMIT),
        name="expert_ffn",
    )(first_blk, n_blk, nv, w_gate, w_up, w_down, xs)


def _combine_kernel(h_ref, ya_ref, yb_ref, gate_ref, g_ref, b_ref, o_ref):
    gate = gate_ref[...]
    a_hi, a_lo = _unpack_bf16_halves(ya_ref[...])
    b_hi, b_lo = _unpack_bf16_halves(yb_ref[...])
    g0, g1 = gate[:, 0:1], gate[:, 1:2]
    ffn = jnp.concatenate([a_hi * g0 + b_hi * g1, a_lo * g0 + b_lo * g1], axis=1)
    o_ref[...] = _layer_norm(DEEPNORM_ALPHA * h_ref[...] + ffn, g_ref[...], b_ref[...])


def _combine(h, y2, gate, g, b, tm=512):
    T = h.shape[0]
    nt = T // tm
    return pl.pallas_call(
        _combine_kernel,
        grid=(nt,),
        in_specs=[pl.BlockSpec((tm, D_MODEL), lambda i: (i, 0)),
                  pl.BlockSpec((tm, D_MODEL // 2), lambda i: (i, 0)),
                  pl.BlockSpec((tm, D_MODEL // 2), lambda i: (i + nt, 0)),
                  pl.BlockSpec((tm, 2), lambda i: (i, 0)),
                  pl.BlockSpec((1, D_MODEL), lambda i: (0, 0)),
                  pl.BlockSpec((1, D_MODEL), lambda i: (0, 0))],
        out_specs=pl.BlockSpec((tm, D_MODEL), lambda i: (i, 0)),
        out_shape=jax.ShapeDtypeStruct((T, D_MODEL), F32),
        compiler_params=pltpu.CompilerParams(
            dimension_semantics=("parallel",), vmem_limit_bytes=VMEM_LIMIT),
        name="combine",
    )(h, y2, y2, gate, g, b)


def kernel(x, w_in, gla_gate_w2, gla_gate_b, gla_norm_g, dil_norm_g, w_out, ln1_g, ln1_b,
           router_coarse_w, router_coarse_b, router_fine_w, router_fine_b,
           expert_w_gate, expert_w_up, expert_w_down, ln2_g, ln2_b):
    B, S, D = x.shape
    T = B * S
    depth = w_in.shape[0]
    slopes = jnp.exp2(-8.0 * jnp.arange(1, DIL_HEADS + 1, dtype=F32) / DIL_HEADS)
    n_rows = 2 * T + N_EXPERTS * ROW_BLOCK
    n_blocks = n_rows // ROW_BLOCK
    n_blocks_pad = -(-n_blocks // (2 * LANES)) * (2 * LANES)
    assert n_blocks_pad == 2 * LANES
    a0 = 1536
    h = x.reshape(T, D)
    for l in range(depth):
        w = w_in[l]
        w_gla = w[:, :a0].astype(BF16)
        w_dil = w[:, a0 + GLA_GATE_RANK:].astype(BF16)
        wa = jnp.pad(w[:, a0:a0 + GLA_GATE_RANK], ((0, 0), (0, LANES - GLA_GATE_RANK))).astype(BF16)
        w2 = jnp.pad(gla_gate_w2[l], ((0, LANES - GLA_GATE_RANK), (0, 0)))
        w2h, w2l = _split_bf16(w2)
        q, k, v, r, la, dq, dk, dv = _in_proj(h, w_gla, w_dil, wa, w2h, w2l, gla_gate_b[l][None, :])
        o_gla = _gla(q, k, v, r, la, gla_norm_g[l][None, :], B, S)
        g2 = jnp.tile(dil_norm_g[l], 2)[None, :]
        o_dil = _dilated(slopes, dq, dk, dv, g2, B, S)
        wo = w_out[l].astype(BF16)
        rw = jnp.concatenate([router_fine_w[l].reshape(D, N_EXPERTS), router_coarse_w[l]], axis=1)
        rw = jnp.pad(rw, ((0, 0), (0, LANES - N_EXPERTS - N_GROUPS))).T
        rwh, rwl = _split_bf16(rw)
        rb = jnp.concatenate([router_fine_b[l].reshape(N_EXPERTS), router_coarse_b[l]])
        rb = jnp.pad(rb, (0, LANES - N_EXPERTS - N_GROUPS))[:, None]
        h1, h1p, eid_t, gate, cnt = _out_proj(o_gla, o_dil, h, wo[:GLA_WIDTH], wo[GLA_WIDTH:],
                                              ln1_g[l][None, :], ln1_b[l][None, :], rwh, rwl, rb)
        dest_t, be, nv = _positions(eid_t, cnt, n_blocks_pad)
        dest_flat = dest_t.reshape(2 * T)
        src_tok = _sc_inverse_rows(dest_flat, n_rows)
        xs = _sc_gather_rows(h1p, src_tok)
        be = be.reshape(n_blocks_pad)
        y = _ffn(be[:N_EXPERTS], be[LANES:LANES + N_EXPERTS], nv.reshape(LANES)[:1], xs,
                 expert_w_gate[l], expert_w_up[l], expert_w_down[l])
        y2 = _sc_gather_rows(y, dest_flat)
        h = _combine(h1, y2, gate, ln2_g[l][None, :], ln2_b[l][None, :])
    return h.reshape(B, S, D)
```

```python
import functools
import math

import jax
import jax.numpy as jnp
import numpy as np
from jax import lax
from jax.experimental import pallas as pl
from jax.experimental.pallas import tpu as pltpu
from jax.experimental.pallas import tpu_sc as plsc

D_MODEL = 1024
GLA_HEADS = 4
GLA_DK = 64
GLA_DV = 128
GLA_KEY_WIDTH = GLA_HEADS * GLA_DK
GLA_WIDTH = GLA_HEADS * GLA_DV
GLA_GATE_RANK = 16
GLA_GATE_TEMP = 16.0
DIL_HEADS = 8
DIL_DH = 64
DIL_WIDTH = DIL_HEADS * DIL_DH
DIL_CONFIGS = ((128, 1), (512, 4), (2048, 16))
DIL_BLOCK = 128
DIL_MAX_R = max(r for _, r in DIL_CONFIGS)
DIL_PAD = DIL_BLOCK * DIL_MAX_R
DIL_UNROLL = 16
N_GROUPS = 4
EXPERTS_PER_GROUP = 8
N_EXPERTS = N_GROUPS * EXPERTS_PER_GROUP
D_FF = 512
DEEPNORM_ALPHA = 2.0 ** 0.25
EPS = 1e-5
IN_PROJ_GATE_COL = 2 * GLA_KEY_WIDTH + 2 * GLA_WIDTH
LOG2E = math.log2(math.e)

LANES = 128
GLA_CHUNK = 128
GLA_SUB = 64
GLA_UNROLL = 4
SC_LANES = 16
SC_INDEX_WINDOW = 128
SC_GATHER_BUFFERS = 6
SC_GATHER_ROWS = 32
FFN_BUFFERS = 12
FFN_GROUPS = (4, 2)
FFN_WEIGHT_CHUNKS = 4
ROW_BLOCK = 256
VMEM_LIMIT = 56 * 1024 * 1024

F32 = jnp.float32
BF16 = jnp.bfloat16


def _dot(a, b):
    return jnp.dot(a, b, preferred_element_type=F32)


def _dot_nt(a, b):
    return lax.dot_general(a, b, (((1,), (1,)), ((), ())), preferred_element_type=F32)


def _dot_tn(a, b):
    return lax.dot_general(a, b, (((0,), (0,)), ((), ())), preferred_element_type=F32)


def _split_bf16(v):
    hi = v.astype(BF16)
    lo = (v - hi.astype(F32)).astype(BF16)
    return hi, lo


def _pack_bf16_halves(v):
    w = v.shape[1] // 2
    hi = lax.bitcast_convert_type(v[:, :w].astype(BF16).astype(F32), jnp.int32)
    lo = lax.bitcast_convert_type(v[:, w:].astype(BF16).astype(F32), jnp.int32)
    return hi | lax.shift_right_logical(lo, 16)


def _unpack_bf16_halves(words):
    hi = lax.bitcast_convert_type(words & jnp.int32(-65536), F32)
    lo = lax.bitcast_convert_type(lax.shift_left(words, 16), F32)
    return hi, lo


def _layer_norm(v, g, b):
    mu = jnp.mean(v, axis=-1, keepdims=True)
    c = v - mu
    var = jnp.mean(c * c, axis=-1, keepdims=True)
    return c * lax.rsqrt(var + EPS) * g + b


def _in_proj_kernel(x_ref, w_ref, w2h_ref, w2l_ref, gb_ref,
                    q_ref, k_ref, v_ref, r_ref, la_ref, dq_ref, dk_ref, dv_ref, wg_s, wd_s, wa_s):
    a0 = IN_PROJ_GATE_COL

    @pl.when(pl.program_id(0) == 0)
    def _():
        wg_s[...] = w_ref[:, 0:a0].astype(BF16)
        gate_tile = w_ref[:, a0:a0 + LANES]
        lane = lax.broadcasted_iota(jnp.int32, gate_tile.shape, 1)
        wa_s[...] = jnp.where(lane < GLA_GATE_RANK, gate_tile, 0.0).astype(BF16)
        tail = w_ref[:, a0:]
        wd_s[...] = tail[:, GLA_GATE_RANK:].astype(BF16)

    xb = x_ref[...].astype(BF16)

    def piece(w_s, c0, c1):
        return _dot(xb, w_s[:, c0:c1])

    q_ref[...] = (piece(wg_s, 0, 256) * (GLA_DK ** -0.5)).astype(BF16)
    k_ref[...] = piece(wg_s, 256, 512).astype(BF16)
    v_ref[...] = piece(wg_s, 512, 1024).astype(BF16)
    r_ref[...] = piece(wg_s, 1024, 1536).astype(BF16)
    dq_ref[...] = (piece(wd_s, 0, 512) * (DIL_DH ** -0.5 * LOG2E)).astype(BF16)
    dk_ref[...] = piece(wd_s, 512, 1024).astype(BF16)
    dv_ref[...] = piece(wd_s, 1024, 1536).astype(BF16)
    ga = _dot(xb, wa_s[...])
    ga_hi, ga_lo = _split_bf16(ga)
    z = _dot(ga_hi, w2h_ref[...]) + _dot(ga_lo, w2h_ref[...]) + _dot(ga_hi, w2l_ref[...]) + gb_ref[...]
    log_sig = jnp.minimum(z, 0.0) - jnp.log1p(jnp.exp(-jnp.abs(z)))
    la_ref[...] = log_sig * (1.0 / GLA_GATE_TEMP)


def _in_proj(x2, w, w2h, w2l, gb, tm=1024):
    T = x2.shape[0]
    row = lambda wd: pl.BlockSpec((tm, wd), lambda i: (i, 0))
    full = lambda a: pl.BlockSpec(a.shape, lambda i: (0,) * a.ndim)
    outs = [(256, BF16), (256, BF16), (512, BF16), (512, BF16), (256, F32),
            (512, BF16), (512, BF16), (512, BF16)]
    group = w.shape[1] - IN_PROJ_GATE_COL - GLA_GATE_RANK
    return pl.pallas_call(
        _in_proj_kernel,
        grid=(T // tm,),
        in_specs=[row(D_MODEL), pl.BlockSpec(w.shape, lambda i: (0, 0), pipeline_mode=pl.Buffered(1)),
                  full(w2h), full(w2l), full(gb)],
        out_specs=[row(wd) for wd, _ in outs],
        out_shape=[jax.ShapeDtypeStruct((T, wd), dt) for wd, dt in outs],
        scratch_shapes=[pltpu.VMEM((D_MODEL, IN_PROJ_GATE_COL), BF16), pltpu.VMEM((D_MODEL, group), BF16),
                        pltpu.VMEM((D_MODEL, LANES), BF16)],
        compiler_params=pltpu.CompilerParams(
            dimension_semantics=("arbitrary",), vmem_limit_bytes=VMEM_LIMIT),
        name="in_proj",
    )(x2, w, w2h, w2l, gb)


def _gla_kernel(q_ref, k_ref, v_ref, r_ref, la_ref, g_ref, o_ref, s_ref, *, seq_block):
    C = GLA_CHUNK
    H = GLA_SUB
    assert C == 2 * H

    @pl.when(pl.program_id(1) == 0)
    def _():
        s_ref[...] = jnp.zeros_like(s_ref)

    ri = lax.broadcasted_iota(jnp.int32, (C, C), 0)
    ci = lax.broadcasted_iota(jnp.int32, (C, C), 1)
    same_sub = (ri // H) == (ci // H)
    sum_ops = jnp.concatenate([jnp.logical_and(same_sub, ci <= ri).astype(BF16), same_sub.astype(BF16),
                               jnp.ones((C, C), BF16)], axis=0)
    diag_mask = jnp.logical_and(same_sub, ci <= ri)
    off_mask = (ri // H) > (ci // H)
    second = lax.broadcasted_iota(jnp.int32, (C, 1), 0) >= H
    ones_cl = jnp.ones((C, LANES), BF16)
    lane_k = lax.broadcasted_iota(jnp.int32, (1, GLA_KEY_WIDTH), 1) // GLA_DK
    head_masks = [(lane_k == h).astype(F32) for h in range(GLA_HEADS)]
    srow = lax.broadcasted_iota(jnp.int32, (GLA_KEY_WIDTH, GLA_WIDTH), 0) // GLA_DK
    scol = lax.broadcasted_iota(jnp.int32, (GLA_KEY_WIDTH, GLA_WIDTH), 1) // GLA_DV
    state_mask = (srow == scol).astype(F32)
    g = g_ref[...]

    def trip(t, carry):
        U = GLA_UNROLL
        rows = [pl.ds(pl.multiple_of((t * U + u) * C, C), C) for u in range(U)]
        la2s, sums = [], []
        for u in range(U):
            la_hi, la_lo = _split_bf16(la_ref[rows[u], :])
            la2 = jnp.concatenate([la_hi, la_lo], axis=1)
            la2s.append(la2)
            sm = _dot(sum_ops, la2)
            sums.append(sm[:, 0:GLA_KEY_WIDTH] + sm[:, GLA_KEY_WIDTH:])
        q_states, k_states, scores = [], [], []
        for u in range(U):
            b = sums[u][0:C]
            t_sub = sums[u][C:2 * C]
            other = sums[u][2 * C:3 * C] - t_sub
            q = q_ref[rows[u], :].astype(F32)
            k = k_ref[rows[u], :].astype(F32)
            qd = q * jnp.exp(b)
            kd = (k * jnp.exp(-b)).astype(BF16)
            ke_f = k * jnp.exp(t_sub - b)
            cross = jnp.exp(other)
            q_states.append((qd * jnp.where(second, cross, 1.0)).astype(BF16))
            k_states.append((ke_f * jnp.where(second, 1.0, cross)).astype(BF16))
            q_heads = jnp.concatenate([(qd * head_masks[h]).astype(BF16) for h in range(GLA_HEADS)], axis=0)
            keys2 = jnp.concatenate([kd, ke_f.astype(BF16)], axis=0)
            scores.append(_dot_nt(q_heads, keys2))
        decs, upds = [], []
        for u in range(U):
            tot = _dot_tn(la2s[u], ones_cl)
            decs.append(jnp.exp(tot[0:GLA_KEY_WIDTH] + tot[GLA_KEY_WIDTH:]))
            upds.append(_dot_tn(k_states[u], v_ref[rows[u], :]) * state_mask)
        o_inters = []
        for u in range(U):
            state = s_ref[...]
            o_inters.append(_dot(q_states[u], state.astype(BF16)))
            for h in range(GLA_HEADS):
                cols = slice(h * GLA_DV, (h + 1) * GLA_DV)
                s_ref[:, cols] = state[:, cols] * decs[u] + upds[u][:, cols]
        for u in range(U):
            v = v_ref[rows[u], :]
            outs = []
            for h in range(GLA_HEADS):
                sh = scores[u][h * C:(h + 1) * C]
                a = jnp.where(diag_mask, sh[:, 0:C], 0.0) + jnp.where(off_mask, sh[:, C:2 * C], 0.0)
                cols = slice(h * GLA_DV, (h + 1) * GLA_DV)
                o = _dot(a.astype(BF16), v[:, cols]) + o_inters[u][:, cols]
                o = o * lax.rsqrt(jnp.mean(o * o, axis=-1, keepdims=True) + EPS) * g
                outs.append(o)
            o_all = jnp.concatenate(outs, axis=-1)
            rr = r_ref[rows[u], :].astype(F32)
            o_ref[rows[u], :] = (o_all * (rr * jax.nn.sigmoid(rr))).astype(BF16)
        return carry

    lax.fori_loop(0, seq_block // C // GLA_UNROLL, trip, 0)


def _gla(q, k, v, r, la, g, batch, seq, seq_block=1024):
    nsb = seq // seq_block
    row = lambda w: pl.BlockSpec((seq_block, w), lambda b, s: (b * nsb + s, 0))
    return pl.pallas_call(
        functools.partial(_gla_kernel, seq_block=seq_block),
        grid=(batch, nsb),
        in_specs=[row(256), row(256), row(512), row(512), row(256),
                  pl.BlockSpec((1, GLA_DV), lambda b, s: (0, 0))],
        out_specs=row(512),
        out_shape=jax.ShapeDtypeStruct((batch * seq, GLA_WIDTH), BF16),
        scratch_shapes=[pltpu.VMEM((GLA_KEY_WIDTH, GLA_WIDTH), F32)],
        compiler_params=pltpu.CompilerParams(
            dimension_semantics=("parallel", "arbitrary"), vmem_limit_bytes=VMEM_LIMIT),
        name="gla",
    )(q, k, v, r, la, g)


def _dil_kernel(slope_ref, q_ref, k_ref, v_ref, g_ref, o_ref,
                qf, kf, vf, kd, va, vb, oc, lc, *, seq):
    B = DIL_BLOCK
    U = DIL_UNROLL
    pair = pl.program_id(1)
    qf[...] = q_ref[...].astype(F32)
    kf[...] = k_ref[...].astype(F32)
    vf[...] = v_ref[...].astype(F32)

    lane = lax.broadcasted_iota(jnp.int32, (1, LANES), 1)
    first = lane < DIL_DH
    ii = lax.broadcasted_iota(jnp.int32, (B, B), 0)
    jj = lax.broadcasted_iota(jnp.int32, (B, B), 1)
    upper = jj > ii
    eye = jj == ii
    dist = jnp.bitwise_and(ii - jj, B - 1).astype(F32)
    neg = jnp.float32(-jnp.inf)
    neg_tile = jnp.full((B, B), neg, F32)
    zero_tile = jnp.zeros((B, LANES), BF16)

    for cfg, (window, r) in enumerate(DIL_CONFIGS):
        nb = seq // r // B
        cs = nb + 1
        bias_prev, bias_cur = [], []
        for hh in range(2):
            slope = slope_ref[2 * pair + hh] * (float(r) * LOG2E)
            bias = dist * (-slope)
            bias_prev.append(jnp.where(upper, bias, jnp.where(eye, -slope * float(B), neg)))
            bias_cur.append(jnp.where(upper, neg, bias))

        for c in range(r):
            rows0 = slice(c * cs * B, (c * cs + 1) * B)
            kd[rows0, :] = zero_tile
            va[rows0, :] = zero_tile
            vb[rows0, :] = zero_tile

        def prep(t4, carry, r=r, nb=nb, cs=cs):
            for j in range(4):
                t = t4 * 4 + j
                c = t // nb
                n = t % nb
                start = c + n * (B * r)
                rows = pl.ds(start, B, stride=r) if r > 1 else pl.ds(pl.multiple_of(start, B), B)
                dst = pl.ds(pl.multiple_of((c * cs + 1 + n) * B, B), B)
                kd[dst, :] = kf[rows, :].astype(BF16)
                v = vf[rows, :]
                va[dst, :] = jnp.where(first, v, 1.0).astype(BF16)
                vb[dst, :] = jnp.where(first, 1.0, v).astype(BF16)
            return carry

        lax.fori_loop(0, seq // B // 4, prep, 0)

        def geom(it, r=r, nb=nb, cs=cs):
            c = it // nb
            n = it % nb
            start = c + n * (B * r)
            rows = pl.ds(start, B, stride=r) if r > 1 else pl.ds(pl.multiple_of(start, B), B)
            kv = pl.ds(pl.multiple_of((c * cs + n) * B, B), 2 * B)
            return rows, kv

        def body(t, carry, cfg=cfg, nb=nb, bias_prev=bias_prev, bias_cur=bias_cur, geom=geom):
            geo = [geom(t * U + u) for u in range(U)]
            scores = []
            for u in range(U):
                rows, kv = geo[u]
                q = qf[rows, :]
                q_heads = jnp.concatenate([jnp.where(first, q, 0.0), jnp.where(first, 0.0, q)], axis=0)
                scores.append(_dot_nt(q_heads.astype(BF16), kd[kv, :]))
            probs, maxes = [], []
            for u in range(U):
                if nb % U == 0:
                    has_prev = True if u > 0 else (t * U) % nb > 0
                else:
                    assert U % nb == 0
                    has_prev = (u % nb) > 0
                for hh in range(2):
                    s2 = scores[u][hh * B:(hh + 1) * B]
                    if has_prev is True:
                        bp = bias_prev[hh]
                    elif has_prev is False:
                        bp = neg_tile
                    else:
                        bp = jnp.where(has_prev, bias_prev[hh], neg)
                    s_prev = s2[:, 0:B] + bp
                    s_cur = s2[:, B:2 * B] + bias_cur[hh]
                    m = jnp.max(jnp.maximum(s_prev, s_cur), axis=-1, keepdims=True)
                    probs.append(jnp.concatenate([jnp.exp2(s_prev - m), jnp.exp2(s_cur - m)], axis=1).astype(BF16))
                    maxes.append(m)
            for u in range(U):
                rows, kv = geo[u]
                acc0 = _dot(probs[2 * u], va[kv, :])
                acc1 = _dot(probs[2 * u + 1], vb[kv, :])
                num = jnp.where(first, acc0, acc1)
                den = pltpu.roll(jnp.where(first, acc1, acc0), DIL_DH, axis=1)
                oc[cfg, rows, :] = num * (1.0 / den)
                lc[cfg, rows, :] = jnp.where(first, maxes[2 * u], maxes[2 * u + 1]) + jnp.log2(den)
            return carry

        lax.fori_loop(0, seq // B // U, body, 0)

    g = g_ref[...]
    CH = 512

    def mix(i, carry):
        rows = pl.ds(pl.multiple_of(i * CH, CH), CH)
        l0, l1, l2 = lc[0, rows, :], lc[1, rows, :], lc[2, rows, :]
        m = jnp.maximum(jnp.maximum(l0, l1), l2)
        e0, e1, e2 = jnp.exp2(l0 - m), jnp.exp2(l1 - m), jnp.exp2(l2 - m)
        inv = 1.0 / (e0 + e1 + e2)
        o = (e0 * oc[0, rows, :] + e1 * oc[1, rows, :] + e2 * oc[2, rows, :]) * inv
        sq = o * o
        ms_a = jnp.sum(jnp.where(first, sq, 0.0), axis=-1, keepdims=True) * (1.0 / DIL_DH)
        ms_b = jnp.sum(jnp.where(first, 0.0, sq), axis=-1, keepdims=True) * (1.0 / DIL_DH)
        ms = jnp.where(first, ms_a, ms_b)
        o_ref[rows, :] = (o * lax.rsqrt(ms + EPS) * g).astype(BF16)
        return carry

    lax.fori_loop(0, seq // CH, mix, 0)


def _dilated(slopes, dq, dk, dv, g2, batch, seq):
    blk = pl.BlockSpec((seq, LANES), lambda b, p, s: (b, p))
    return pl.pallas_call(
        functools.partial(_dil_kernel, seq=seq),
        grid_spec=pltpu.PrefetchScalarGridSpec(
            num_scalar_prefetch=1,
            grid=(batch, DIL_WIDTH // LANES),
            in_specs=[blk, blk, blk, pl.BlockSpec((1, LANES), lambda b, p, s: (0, 0))],
            out_specs=blk,
            scratch_shapes=[pltpu.VMEM((seq, LANES), F32)] * 3
                           + [pltpu.VMEM((seq + DIL_PAD, LANES), BF16)] * 3
                           + [pltpu.VMEM((3, seq, LANES), F32)] * 2,
        ),
        out_shape=jax.ShapeDtypeStruct((batch * seq, DIL_WIDTH), BF16),
        compiler_params=pltpu.CompilerParams(
            dimension_semantics=("parallel", "parallel"), vmem_limit_bytes=VMEM_LIMIT),
        name="dilated",
    )(slopes, dq, dk, dv, g2)


def _out_proj_kernel(og_ref, od_ref, x_ref, wg_ref, wd_ref, g_ref, b_ref,
                     rwh_ref, rwl_ref, rb_ref, h_ref, hp_ref, eid_ref, gate_ref, cnt_ref, wg_s, wd_s):
    @pl.when(pl.program_id(0) == 0)
    def _():
        wg_s[...] = wg_ref[...].astype(BF16)
        wd_s[...] = wd_ref[...].astype(BF16)

    mix = _dot(og_ref[...], wg_s[...]) + _dot(od_ref[...], wd_s[...])
    h = _layer_norm(DEEPNORM_ALPHA * x_ref[...] + mix, g_ref[...], b_ref[...])
    h_ref[...] = h
    hp_ref[...] = _pack_bf16_halves(h)
    h_hi, h_lo = _split_bf16(h)
    lt = (_dot_nt(rwh_ref[...], h_hi) + _dot_nt(rwh_ref[...], h_lo) + _dot_nt(rwl_ref[...], h_hi)
          + rb_ref[...])
    tm = lt.shape[1]
    row = lax.broadcasted_iota(jnp.int32, (EXPERTS_PER_GROUP, tm), 0).astype(F32)
    neg = jnp.float32(-jnp.inf)
    big = jnp.float32(1e9)
    coarse = jnp.where(row < N_GROUPS, lt[N_EXPERTS:N_EXPERTS + EXPERTS_PER_GROUP, :], neg)
    cmax = jnp.max(coarse, axis=0, keepdims=True)
    g_idx = jnp.min(jnp.where(coarse == cmax, row, big), axis=0, keepdims=True)
    p_group = 1.0 / jnp.sum(jnp.exp(coarse - cmax), axis=0, keepdims=True)
    fine = lt[(N_GROUPS - 1) * EXPERTS_PER_GROUP:N_EXPERTS, :]
    for g in range(N_GROUPS - 2, -1, -1):
        fine = jnp.where(g_idx == g, lt[g * EXPERTS_PER_GROUP:(g + 1) * EXPERTS_PER_GROUP, :], fine)
    v1 = jnp.max(fine, axis=0, keepdims=True)
    i1 = jnp.min(jnp.where(fine == v1, row, big), axis=0, keepdims=True)
    fine2 = jnp.where(row == i1, neg, fine)
    v2 = jnp.max(fine2, axis=0, keepdims=True)
    i2 = jnp.min(jnp.where(fine2 == v2, row, big), axis=0, keepdims=True)
    e2 = jnp.exp(v2 - v1)
    den = 1.0 + e2
    gate1 = p_group * (1.0 / den)
    gate2 = p_group * (e2 / den)
    id1 = g_idx * EXPERTS_PER_GROUP + i1
    id2 = g_idx * EXPERTS_PER_GROUP + i2
    eid_ref[...] = jnp.concatenate([id1, id2], axis=0).astype(jnp.int32)
    slab = jnp.concatenate([gate1, gate2, jnp.zeros((LANES - 2, tm), F32)], axis=0)
    gate_ref[...] = slab.T[:, 0:2]
    sub = lax.broadcasted_iota(jnp.int32, (LANES, tm), 0).astype(F32)
    onehot = jnp.logical_or(sub == id1, sub == id2).astype(BF16)

    @pl.when(pl.program_id(0) == 0)
    def _():
        cnt_ref[...] = jnp.zeros_like(cnt_ref)

    cnt_ref[...] += _dot(onehot, jnp.ones((tm, LANES), BF16))


def _out_proj(og, od, x2, w_out, g, b, rwh, rwl, rb, tm=1024):
    T = x2.shape[0]
    row = lambda w: pl.BlockSpec((tm, w), lambda i: (i, 0))
    full = lambda a: pl.BlockSpec(a.shape, lambda i: (0,) * a.ndim)
    half = lambda p: pl.BlockSpec((GLA_WIDTH, D_MODEL), lambda i: (p, 0))
    return pl.pallas_call(
        _out_proj_kernel,
        grid=(T // tm,),
        in_specs=[row(512), row(512), row(D_MODEL), half(0), half(1), full(g), full(b),
                  full(rwh), full(rwl), full(rb)],
        out_specs=[row(D_MODEL), row(D_MODEL // 2), pl.BlockSpec((2, tm), lambda i: (0, i)), row(2),
                   pl.BlockSpec((LANES, LANES), lambda i: (0, 0))],
        out_shape=[jax.ShapeDtypeStruct((T, D_MODEL), F32),
                   jax.ShapeDtypeStruct((T, D_MODEL // 2), jnp.int32),
                   jax.ShapeDtypeStruct((2, T), jnp.int32),
                   jax.ShapeDtypeStruct((T, 2), F32),
                   jax.ShapeDtypeStruct((LANES, LANES), F32)],
        scratch_shapes=[pltpu.VMEM((GLA_WIDTH, D_MODEL), BF16), pltpu.VMEM((DIL_WIDTH, D_MODEL), BF16)],
        compiler_params=pltpu.CompilerParams(
            dimension_semantics=("arbitrary",), vmem_limit_bytes=VMEM_LIMIT),
        name="out_proj_router",
    )(og, od, x2, w_out, w_out, g, b, rwh, rwl, rb)


def _positions_kernel(eid_ref, cnt_ref, dest_ref, be_ref, nv_ref, carry_ref, sp_ref, tri_ref, *, tb):
    i = pl.program_id(0)

    @pl.when(i == 0)
    def _():
        shift = int(math.log2(ROW_BLOCK))
        nb_col = (cnt_ref[...].astype(jnp.int32) + (ROW_BLOCK - 1)) >> shift
        r = lax.broadcasted_iota(jnp.int32, (LANES, LANES), 0)
        c = lax.broadcasted_iota(jnp.int32, (LANES, LANES), 1)
        nb_f = jnp.where(r < N_EXPERTS, nb_col, 0).astype(F32)
        start_col = _dot((c < r).astype(BF16), nb_f.astype(BF16))
        sp_ref[...] = start_col * float(ROW_BLOCK)
        carry_ref[...] = jnp.zeros_like(carry_ref)
        be_ref[...] = jnp.concatenate([start_col.T[0:1, :], nb_f.T[0:1, :]], axis=1).astype(jnp.int32)
        total = jnp.sum(nb_f[:, 0:1], axis=0, keepdims=True)
        nv_ref[...] = jnp.broadcast_to(total, (1, LANES)).astype(jnp.int32)
        tr = lax.broadcasted_iota(jnp.int32, (tb, tb), 0)
        tc = lax.broadcasted_iota(jnp.int32, (tb, tb), 1)
        tri_ref[...] = (tr < tc).astype(BF16)

    sub = lax.broadcasted_iota(jnp.int32, (LANES, tb), 0)
    oh1 = sub == eid_ref[0:1, :]
    oh2 = sub == eid_ref[1:2, :]
    oh = jnp.logical_or(oh1, oh2).astype(BF16)
    offset = jnp.tile(carry_ref[...] + sp_ref[...], (1, tb // LANES))
    before = _dot(oh, tri_ref[...]) + offset
    d1 = jnp.sum(jnp.where(oh1, before, 0.0), axis=0, keepdims=True)
    d2 = jnp.sum(jnp.where(oh2, before, 0.0), axis=0, keepdims=True)
    dest_ref[...] = jnp.concatenate([d1, d2], axis=0).astype(jnp.int32)
    carry_ref[...] += _dot(oh, jnp.ones((tb, LANES), BF16))


def _positions(eid_t, cnt, n_blocks_pad, tb=1024):
    T = eid_t.shape[1]
    return pl.pallas_call(
        functools.partial(_positions_kernel, tb=tb),
        grid=(T // tb,),
        in_specs=[pl.BlockSpec((2, tb), lambda i: (0, i)), pl.BlockSpec((LANES, LANES), lambda i: (0, 0))],
        out_specs=[pl.BlockSpec((2, tb), lambda i: (0, i)),
                   pl.BlockSpec((1, n_blocks_pad), lambda i: (0, 0)),
                   pl.BlockSpec((1, LANES), lambda i: (0, 0))],
        out_shape=[jax.ShapeDtypeStruct((2, T), jnp.int32),
                   jax.ShapeDtypeStruct((1, n_blocks_pad), jnp.int32),
                   jax.ShapeDtypeStruct((1, LANES), jnp.int32)],
        scratch_shapes=[pltpu.VMEM((LANES, LANES), F32), pltpu.VMEM((LANES, LANES), F32),
                        pltpu.VMEM((tb, tb), BF16)],
        compiler_params=pltpu.CompilerParams(dimension_semantics=("arbitrary",)),
        name="positions",
    )(eid_t, cnt)


def _sc_gather_rows(table, idx):
    n = idx.shape[0]
    d = table.shape[1]
    info = plsc.get_sparse_core_info()
    nc, ns = info.num_cores, info.num_subcores
    per_w = n // (nc * ns)
    assert per_w * nc * ns == n and per_w % SC_INDEX_WINDOW == 0
    mesh = plsc.VectorSubcoreMesh(core_axis_name="core", subcore_axis_name="subcore")
    nchunk = per_w // SC_GATHER_ROWS
    nbuf = SC_GATHER_BUFFERS

    @functools.partial(
        pl.kernel, out_type=jax.ShapeDtypeStruct((n, d), table.dtype), mesh=mesh,
        scratch_types=[pltpu.VMEM((per_w,), jnp.int32),
                       pltpu.VMEM((nbuf, SC_GATHER_ROWS, d), table.dtype),
                       pltpu.SemaphoreType.DMA((nbuf,)), pltpu.SemaphoreType.DMA((nbuf,))],
        name="sc_gather_rows")
    def gather(x_hbm, i_hbm, o_hbm, idx_v, buf, gsem, wsem):
        wid = lax.axis_index("subcore") * nc + lax.axis_index("core")
        base = wid * per_w
        pltpu.sync_copy(i_hbm.at[pl.ds(base, per_w)], idx_v)

        def gather_copy(c):
            rows = idx_v.at[pl.ds(c * SC_GATHER_ROWS, SC_GATHER_ROWS)]
            return pltpu.make_async_copy(x_hbm.at[rows], buf.at[c % nbuf], gsem.at[c % nbuf])

        def write_copy(c):
            dst = o_hbm.at[pl.ds(base + c * SC_GATHER_ROWS, SC_GATHER_ROWS)]
            return pltpu.make_async_copy(buf.at[c % nbuf], dst, wsem.at[c % nbuf])

        for c in range(min(nbuf - 1, nchunk)):
            gather_copy(c).start()
        for c in range(nchunk):
            gather_copy(c).wait()
            write_copy(c).start()
            if c + nbuf - 1 < nchunk:
                if c >= 1:
                    write_copy(c - 1).wait()
                gather_copy(c + nbuf - 1).start()
        for c in range(max(0, nchunk - nbuf), nchunk):
            write_copy(c).wait()

    return gather(table, idx)


def _sc_inverse_rows(dest_flat, n_rows, chunk=2048):
    n = dest_flat.shape[0]
    n_tokens = n // 2
    assert n_rows <= 3 * n_tokens
    nc = plsc.get_sparse_core_info().num_cores
    mesh = plsc.VectorSubcoreMesh(core_axis_name="core", subcore_axis_name="subcore")

    @functools.partial(
        pl.kernel, out_type=jax.ShapeDtypeStruct((n_rows,), jnp.int32), mesh=mesh,
        scratch_types=[pltpu.VMEM((n_rows,), jnp.int32), pltpu.VMEM((chunk,), jnp.int32)],
        compiler_params=pltpu.CompilerParams(needs_layout_passes=False),
        name="sc_inverse_rows")
    def inverse(d_hbm, o_hbm, inv_v, d_v):
        wid = lax.axis_index("subcore") * nc + lax.axis_index("core")

        @pl.when(wid == 0)
        def _():
            lanes = lax.iota(jnp.int32, SC_LANES)

            @plsc.parallel_loop(0, n_rows // SC_LANES, unroll=8)
            def _(i):
                r = lanes + i * SC_LANES
                r = jnp.where(r >= n_tokens, r - n_tokens, r)
                inv_v[pl.ds(i * SC_LANES, SC_LANES)] = jnp.where(r >= n_tokens, r - n_tokens, r)

            @pl.loop(0, n // chunk)
            def _(c):
                pltpu.sync_copy(d_hbm.at[pl.ds(c * chunk, chunk)], d_v)

                @plsc.parallel_loop(0, chunk // SC_LANES, unroll=8)
                def _(j):
                    rows = d_v[pl.ds(j * SC_LANES, SC_LANES)]
                    pair = lax.iota(jnp.int32, SC_LANES) + (c * chunk + j * SC_LANES)
                    plsc.store_scatter(inv_v, [rows], jnp.where(pair >= n_tokens, pair - n_tokens, pair))

            pltpu.sync_copy(inv_v, o_hbm)

    return inverse(dest_flat)


def _ffn_kernel(first_ref, count_ref, nv_ref, wg_hbm, wu_hbm, wd_hbm, xs_hbm, y_hbm,
                wg32, wu32, wd32, wgb, wub, wdb, xbuf, ybuf, wsem, isem, osem, *, n_blocks):
    nv = nv_ref[0]
    nbuf = FFN_BUFFERS

    def next_expert(e):
        def more(t):
            return jnp.logical_and(t < N_EXPERTS, count_ref[jnp.minimum(t, N_EXPERTS - 1)] == 0)
        return lax.while_loop(more, lambda t: t + 1, e + 1)

    def weight_copies(e, slot):
        ee = jnp.minimum(e, N_EXPERTS - 1)
        copies = []
        for i, (src, dst) in enumerate(((wg_hbm, wg32), (wu_hbm, wu32), (wd_hbm, wd32))):
            rows = src.shape[1] // FFN_WEIGHT_CHUNKS
            for c in range(FFN_WEIGHT_CHUNKS):
                part = pl.ds(c * rows, rows)
                copies.append(pltpu.make_async_copy(src.at[ee, part], dst.at[slot, part],
                                                    wsem.at[slot, i * FFN_WEIGHT_CHUNKS + c]))
        return copies

    def fetch_weights(e, slot):
        @pl.when(e < N_EXPERTS)
        def _():
            for c in weight_copies(e, slot):
                c.start()

    def take_weights(e, slot):
        for c in weight_copies(e, slot):
            c.wait()
        wgb[...] = wg32[slot].astype(BF16)
        wub[...] = wu32[slot].astype(BF16)
        wdb[...] = wd32[slot].astype(BF16)
        fetch_weights(next_expert(next_expert(e)), slot)

    def rows_of(b):
        return pl.ds(pl.multiple_of(b * ROW_BLOCK, ROW_BLOCK), ROW_BLOCK)

    def buf_rows(b, nblk):
        return pl.ds(pl.multiple_of((b % nbuf) * ROW_BLOCK, ROW_BLOCK), nblk * ROW_BLOCK)

    def in_copy(b):
        return pltpu.make_async_copy(xs_hbm.at[rows_of(b)], xbuf.at[buf_rows(b, 1)], isem.at[b % nbuf])

    def out_copy(b):
        return pltpu.make_async_copy(ybuf.at[buf_rows(b, 1)], y_hbm.at[rows_of(b)], osem.at[b % nbuf])

    def expert_mlp(words):
        x_hi, x_lo = _unpack_bf16_halves(words)
        xb = jnp.concatenate([x_hi.astype(BF16), x_lo.astype(BF16)], axis=1)
        a = _dot(xb, wgb[...])
        u = _dot(xb, wub[...])
        hid = (a * jax.nn.sigmoid(a) * u).astype(BF16)
        return _pack_bf16_halves(_dot(hid, wdb[...]))

    ahead = nbuf - max(FFN_GROUPS)

    @pl.when(nv > 0)
    def _():
        e0 = next_expert(jnp.int32(-1))
        for i in range(ahead):
            @pl.when(i < nv)
            def _():
                in_copy(i).start()
        fetch_weights(e0, 0)
        fetch_weights(next_expert(e0), 1)
        take_weights(e0, 0)

        def step(carry):
            b, e, k = carry
            switch = b >= first_ref[e] + count_ref[e]
            e_new = jnp.where(switch, next_expert(e), e)
            k_new = jnp.where(switch, k + 1, k)

            @pl.when(switch)
            def _():
                take_weights(e_new, k_new % 2)

            end = first_ref[e_new] + count_ref[e_new]
            n = jnp.int32(1)
            for size in FFN_GROUPS:
                fits = jnp.logical_and(b + size <= end, b % nbuf + size <= nbuf)
                n = jnp.where(jnp.logical_and(n == 1, fits), size, n)
            for i in range(max(FFN_GROUPS)):
                @pl.when(jnp.logical_and(i < n, b + ahead + i < nv))
                def _():
                    in_copy(b + ahead + i).start()
            for i in range(max(FFN_GROUPS)):
                @pl.when(i < n)
                def _():
                    in_copy(b + i).wait()

                    @pl.when(b + i >= nbuf)
                    def _():
                        out_copy(b + i - nbuf).wait()

            for size in FFN_GROUPS + (1,):
                @pl.when(n == size)
                def _():
                    ybuf[buf_rows(b, size), :] = expert_mlp(xbuf[buf_rows(b, size), :])

            for i in range(max(FFN_GROUPS)):
                @pl.when(i < n)
                def _():
                    out_copy(b + i).start()
            return b + n, e_new, k_new

        lax.while_loop(lambda c: c[0] < nv, step, (jnp.int32(0), e0, jnp.int32(0)))

        for i in range(nbuf):
            @pl.when(nv > i)
            def _():
                out_copy(nv - 1 - i).wait()

    ybuf[0:ROW_BLOCK, :] = jnp.zeros((ROW_BLOCK, D_MODEL // 2), jnp.int32)

    def fill(b, carry):
        pltpu.sync_copy(ybuf.at[pl.ds(0, ROW_BLOCK)], y_hbm.at[rows_of(b)])
        return carry

    lax.fori_loop(nv, n_blocks, fill, 0)


def _ffn(first_blk, n_blk, nv, xs, w_gate, w_up, w_down):
    n_rows = xs.shape[0]
    n_blocks = n_rows // ROW_BLOCK
    anyspec = pl.BlockSpec(memory_space=pl.ANY)
    return pl.pallas_call(
        functools.partial(_ffn_kernel, n_blocks=n_blocks),
        grid_spec=pltpu.PrefetchScalarGridSpec(
            num_scalar_prefetch=3,
            grid=(1,),
            in_specs=[anyspec, anyspec, anyspec, anyspec],
            out_specs=anyspec,
            scratch_shapes=[pltpu.VMEM((2, D_MODEL, D_FF), F32), pltpu.VMEM((2, D_MODEL, D_FF), F32),
                            pltpu.VMEM((2, D_FF, D_MODEL), F32),
                            pltpu.VMEM((D_MODEL, D_FF), BF16), pltpu.VMEM((D_MODEL, D_FF), BF16),
                            pltpu.VMEM((D_FF, D_MODEL), BF16),
                            pltpu.VMEM((FFN_BUFFERS * ROW_BLOCK, D_MODEL // 2), jnp.int32),
                            pltpu.VMEM((FFN_BUFFERS * ROW_BLOCK, D_MODEL // 2), jnp.int32),
                            pltpu.SemaphoreType.DMA((2, 3 * FFN_WEIGHT_CHUNKS)),
                            pltpu.SemaphoreType.DMA((FFN_BUFFERS,)),
                            pltpu.SemaphoreType.DMA((FFN_BUFFERS,))],
        ),
        out_shape=jax.ShapeDtypeStruct((n_rows, D_MODEL // 2), jnp.int32),
        compiler_params=pltpu.CompilerParams(
            dimension_semantics=("arbitrary",), vmem_limit_bytes=VMEM_LIMIT),
        name="expert_ffn",
    )(first_blk, n_blk, nv, w_gate, w_up, w_down, xs)


def _combine_kernel(h_ref, ya_ref, yb_ref, gate_ref, g_ref, b_ref, o_ref):
    gate = gate_ref[...]
    a_hi, a_lo = _unpack_bf16_halves(ya_ref[...])
    b_hi, b_lo = _unpack_bf16_halves(yb_ref[...])
    g0, g1 = gate[:, 0:1], gate[:, 1:2]
    ffn = jnp.concatenate([a_hi * g0 + b_hi * g1, a_lo * g0 + b_lo * g1], axis=1)
    o_ref[...] = _layer_norm(DEEPNORM_ALPHA * h_ref[...] + ffn, g_ref[...], b_ref[...])


def _combine(h, y2, gate, g, b, tm=512):
    T = h.shape[0]
    nt = T // tm
    return pl.pallas_call(
        _combine_kernel,
        grid=(nt,),
        in_specs=[pl.BlockSpec((tm, D_MODEL), lambda i: (i, 0)),
                  pl.BlockSpec((tm, D_MODEL // 2), lambda i: (i, 0)),
                  pl.BlockSpec((tm, D_MODEL // 2), lambda i: (i + nt, 0)),
                  pl.BlockSpec((tm, 2), lambda i: (i, 0)),
                  pl.BlockSpec((1, D_MODEL), lambda i: (0, 0)),
                  pl.BlockSpec((1, D_MODEL), lambda i: (0, 0))],
        out_specs=pl.BlockSpec((tm, D_MODEL), lambda i: (i, 0)),
        out_shape=jax.ShapeDtypeStruct((T, D_MODEL), F32),
        compiler_params=pltpu.CompilerParams(
            dimension_semantics=("parallel",), vmem_limit_bytes=VMEM_LIMIT),
        name="combine",
    )(h, y2, y2, gate, g, b)


def kernel(x, w_in, gla_gate_w2, gla_gate_b, gla_norm_g, dil_norm_g, w_out, ln1_g, ln1_b,
           router_coarse_w, router_coarse_b, router_fine_w, router_fine_b,
           expert_w_gate, expert_w_up, expert_w_down, ln2_g, ln2_b):
    B, S, D = x.shape
    T = B * S
    depth = w_in.shape[0]
    slopes = jnp.exp2(-8.0 * jnp.arange(1, DIL_HEADS + 1, dtype=F32) / DIL_HEADS)
    n_rows = 2 * T + N_EXPERTS * ROW_BLOCK
    n_blocks = n_rows // ROW_BLOCK
    n_blocks_pad = -(-n_blocks // (2 * LANES)) * (2 * LANES)
    assert n_blocks_pad == 2 * LANES
    h = x.reshape(T, D)
    for l in range(depth):
        w2 = jnp.pad(gla_gate_w2[l], ((0, LANES - GLA_GATE_RANK), (0, 0)))
        w2h, w2l = _split_bf16(w2)
        q, k, v, r, la, dq, dk, dv = _in_proj(h, w_in[l], w2h, w2l, gla_gate_b[l][None, :])
        o_gla = _gla(q, k, v, r, la, gla_norm_g[l][None, :], B, S)
        g2 = jnp.tile(dil_norm_g[l], 2)[None, :]
        o_dil = _dilated(slopes, dq, dk, dv, g2, B, S)
        rw = jnp.concatenate([router_fine_w[l].reshape(D, N_EXPERTS), router_coarse_w[l]], axis=1)
        rw = jnp.pad(rw, ((0, 0), (0, LANES - N_EXPERTS - N_GROUPS))).T
        rwh, rwl = _split_bf16(rw)
        rb = jnp.concatenate([router_fine_b[l].reshape(N_EXPERTS), router_coarse_b[l]])
        rb = jnp.pad(rb, (0, LANES - N_EXPERTS - N_GROUPS))[:, None]
        h1, h1p, eid_t, gate, cnt = _out_proj(o_gla, o_dil, h, w_out[l],
                                              ln1_g[l][None, :], ln1_b[l][None, :], rwh, rwl, rb)
        dest_t, be, nv = _positions(eid_t, cnt, n_blocks_pad)
        dest_flat = dest_t.reshape(2 * T)
        src_tok = _sc_inverse_rows(dest_flat, n_rows)
        xs = _sc_gather_rows(h1p, src_tok)
        be = be.reshape(n_blocks_pad)
        y = _ffn(be[:N_EXPERTS], be[LANES:LANES + N_EXPERTS], nv.reshape(LANES)[:1], xs,
                 expert_w_gate[l], expert_w_up[l], expert_w_down[l])
        y2 = _sc_gather_rows(y, dest_flat)
        h = _combine(h1, y2, gate, ln2_g[l][None, :], ln2_b[l][None, :])
    return h.reshape(B, S, D)
```

```python
import functools
import math

import jax
import jax.numpy as jnp
from jax import lax
from jax.experimental import pallas as pl
from jax.experimental.pallas import tpu as pltpu
from jax.experimental.pallas import tpu_sc as plsc

D_MODEL = 1024
GLA_HEADS = 4
GLA_DK = 64
GLA_DV = 128
GLA_KEY_WIDTH = GLA_HEADS * GLA_DK
GLA_WIDTH = GLA_HEADS * GLA_DV
GLA_GATE_RANK = 16
GLA_GATE_TEMP = 16.0
DIL_HEADS = 8
DIL_DH = 64
DIL_WIDTH = DIL_HEADS * DIL_DH
DIL_CONFIGS = ((128, 1), (512, 4), (2048, 16))
DIL_BLOCK = 128
DIL_MAX_R = max(r for _, r in DIL_CONFIGS)
DIL_PAD = DIL_BLOCK * DIL_MAX_R
DIL_UNROLL = 16
N_GROUPS = 4
EXPERTS_PER_GROUP = 8
N_EXPERTS = N_GROUPS * EXPERTS_PER_GROUP
D_FF = 512
DEEPNORM_ALPHA = 2.0 ** 0.25
EPS = 1e-5
IN_PROJ_GATE_COL = 2 * GLA_KEY_WIDTH + 2 * GLA_WIDTH
LOG2E = math.log2(math.e)

LANES = 128
GLA_CHUNK = 128
GLA_SUB = 64
GLA_UNROLL = 4
SC_LANES = 16
SC_INDEX_WINDOW = 128
SC_GATHER_BUFFERS = 6
SC_GATHER_ROWS = 32
FFN_BUFFERS = 12
FFN_GROUPS = (4, 2)
ROW_BLOCK = 256
VMEM_LIMIT = 56 * 1024 * 1024

F32 = jnp.float32
BF16 = jnp.bfloat16


def _dot(a, b):
    return jnp.dot(a, b, preferred_element_type=F32)


def _dot_nt(a, b):
    return lax.dot_general(a, b, (((1,), (1,)), ((), ())), preferred_element_type=F32)


def _dot_tn(a, b):
    return lax.dot_general(a, b, (((0,), (0,)), ((), ())), preferred_element_type=F32)


def _split_bf16(v):
    hi = v.astype(BF16)
    lo = (v - hi.astype(F32)).astype(BF16)
    return hi, lo


def _pack_bf16_halves(v):
    w = v.shape[1] // 2
    hi = lax.bitcast_convert_type(v[:, :w].astype(BF16).astype(F32), jnp.int32)
    lo = lax.bitcast_convert_type(v[:, w:].astype(BF16).astype(F32), jnp.int32)
    return hi | lax.shift_right_logical(lo, 16)


def _unpack_bf16_halves(words):
    hi = lax.bitcast_convert_type(words & jnp.int32(-65536), F32)
    lo = lax.bitcast_convert_type(lax.shift_left(words, 16), F32)
    return hi, lo


def _layer_norm(v, g, b):
    mu = jnp.mean(v, axis=-1, keepdims=True)
    c = v - mu
    var = jnp.mean(c * c, axis=-1, keepdims=True)
    return c * lax.rsqrt(var + EPS) * g + b


def _in_proj_kernel(x_ref, w_ref, w2h_ref, w2l_ref, gb_ref,
                    q_ref, k_ref, v_ref, r_ref, la_ref, dq_ref, dk_ref, dv_ref, wg_s, wd_s, wa_s):
    a0 = IN_PROJ_GATE_COL

    @pl.when(pl.program_id(0) == 0)
    def _():
        wg_s[...] = w_ref[:, 0:a0].astype(BF16)
        gate_tile = w_ref[:, a0:a0 + LANES]
        lane = lax.broadcasted_iota(jnp.int32, gate_tile.shape, 1)
        wa_s[...] = jnp.where(lane < GLA_GATE_RANK, gate_tile, 0.0).astype(BF16)
        tail = w_ref[:, a0:]
        wd_s[...] = tail[:, GLA_GATE_RANK:].astype(BF16)

    xb = x_ref[...].astype(BF16)

    def piece(w_s, c0, c1):
        return _dot(xb, w_s[:, c0:c1])

    q_ref[...] = (piece(wg_s, 0, 256) * (GLA_DK ** -0.5)).astype(BF16)
    k_ref[...] = piece(wg_s, 256, 512).astype(BF16)
    v_ref[...] = piece(wg_s, 512, 1024).astype(BF16)
    r_ref[...] = piece(wg_s, 1024, 1536).astype(BF16)
    dq_ref[...] = (piece(wd_s, 0, 512) * (DIL_DH ** -0.5 * LOG2E)).astype(BF16)
    dk_ref[...] = piece(wd_s, 512, 1024).astype(BF16)
    dv_ref[...] = piece(wd_s, 1024, 1536).astype(BF16)
    ga = _dot(xb, wa_s[...])
    ga_hi, ga_lo = _split_bf16(ga)
    z = _dot(ga_hi, w2h_ref[...]) + _dot(ga_lo, w2h_ref[...]) + _dot(ga_hi, w2l_ref[...]) + gb_ref[...]
    log_sig = jnp.minimum(z, 0.0) - jnp.log1p(jnp.exp(-jnp.abs(z)))
    la_ref[...] = log_sig * (1.0 / GLA_GATE_TEMP)


def _in_proj(x2, w_in, layer, w2h, w2l, gb, tm=1024):
    T = x2.shape[0]
    row = lambda wd: pl.BlockSpec((tm, wd), lambda i: (i, 0))
    full = lambda a: pl.BlockSpec(a.shape, lambda i: (0,) * a.ndim)
    outs = [(GLA_KEY_WIDTH, BF16), (GLA_KEY_WIDTH, BF16), (GLA_WIDTH, BF16), (GLA_WIDTH, BF16), (GLA_KEY_WIDTH, F32),
            (DIL_WIDTH, BF16), (DIL_WIDTH, BF16), (DIL_WIDTH, BF16)]
    group = w_in.shape[2] - IN_PROJ_GATE_COL - GLA_GATE_RANK
    return pl.pallas_call(
        _in_proj_kernel,
        grid=(T // tm,),
        in_specs=[row(D_MODEL),
                  pl.BlockSpec((None,) + w_in.shape[1:], lambda i: (layer, 0, 0), pipeline_mode=pl.Buffered(1)),
                  full(w2h), full(w2l), full(gb)],
        out_specs=[row(wd) for wd, _ in outs],
        out_shape=[jax.ShapeDtypeStruct((T, wd), dt) for wd, dt in outs],
        scratch_shapes=[pltpu.VMEM((D_MODEL, IN_PROJ_GATE_COL), BF16), pltpu.VMEM((D_MODEL, group), BF16),
                        pltpu.VMEM((D_MODEL, LANES), BF16)],
        compiler_params=pltpu.CompilerParams(
            dimension_semantics=("arbitrary",), vmem_limit_bytes=VMEM_LIMIT),
        name="in_proj",
    )(x2, w_in, w2h, w2l, gb)


def _gla_kernel(q_ref, k_ref, v_ref, r_ref, la_ref, g_ref, o_ref, s_ref, *, seq_block):
    C = GLA_CHUNK
    H = GLA_SUB
    assert C == 2 * H

    @pl.when(pl.program_id(1) == 0)
    def _():
        s_ref[...] = jnp.zeros_like(s_ref)

    ri = lax.broadcasted_iota(jnp.int32, (C, C), 0)
    ci = lax.broadcasted_iota(jnp.int32, (C, C), 1)
    same_sub = (ri // H) == (ci // H)
    sum_ops = jnp.concatenate([jnp.logical_and(same_sub, ci <= ri).astype(BF16), same_sub.astype(BF16),
                               jnp.ones((C, C), BF16)], axis=0)
    diag_mask = jnp.logical_and(same_sub, ci <= ri)
    off_mask = (ri // H) > (ci // H)
    second = lax.broadcasted_iota(jnp.int32, (C, 1), 0) >= H
    ones_cl = jnp.ones((C, LANES), BF16)
    lane_k = lax.broadcasted_iota(jnp.int32, (1, GLA_KEY_WIDTH), 1) // GLA_DK
    head_masks = [(lane_k == h).astype(F32) for h in range(GLA_HEADS)]
    srow = lax.broadcasted_iota(jnp.int32, (GLA_KEY_WIDTH, GLA_WIDTH), 0) // GLA_DK
    scol = lax.broadcasted_iota(jnp.int32, (GLA_KEY_WIDTH, GLA_WIDTH), 1) // GLA_DV
    state_mask = (srow == scol).astype(F32)
    g = g_ref[...]

    def trip(t, carry):
        U = GLA_UNROLL
        rows = [pl.ds(pl.multiple_of((t * U + u) * C, C), C) for u in range(U)]
        la2s, sums = [], []
        for u in range(U):
            la_hi, la_lo = _split_bf16(la_ref[rows[u], :])
            la2 = jnp.concatenate([la_hi, la_lo], axis=1)
            la2s.append(la2)
            sm = _dot(sum_ops, la2)
            sums.append(sm[:, 0:GLA_KEY_WIDTH] + sm[:, GLA_KEY_WIDTH:])
        q_states, k_states, scores = [], [], []
        for u in range(U):
            b = sums[u][0:C]
            t_sub = sums[u][C:2 * C]
            other = sums[u][2 * C:3 * C] - t_sub
            q = q_ref[rows[u], :].astype(F32)
            k = k_ref[rows[u], :].astype(F32)
            qd = q * jnp.exp(b)
            kd = (k * jnp.exp(-b)).astype(BF16)
            ke_f = k * jnp.exp(t_sub - b)
            cross = jnp.exp(other)
            q_states.append((qd * jnp.where(second, cross, 1.0)).astype(BF16))
            k_states.append((ke_f * jnp.where(second, 1.0, cross)).astype(BF16))
            q_heads = jnp.concatenate([(qd * head_masks[h]).astype(BF16) for h in range(GLA_HEADS)], axis=0)
            keys2 = jnp.concatenate([kd, ke_f.astype(BF16)], axis=0)
            scores.append(_dot_nt(q_heads, keys2))
        decs, upds = [], []
        for u in range(U):
            tot = _dot_tn(la2s[u], ones_cl)
            decs.append(jnp.exp(tot[0:GLA_KEY_WIDTH] + tot[GLA_KEY_WIDTH:]))
            upds.append(_dot_tn(k_states[u], v_ref[rows[u], :]) * state_mask)
        o_inters = []
        for u in range(U):
            state = s_ref[...]
            o_inters.append(_dot(q_states[u], state.astype(BF16)))
            for h in range(GLA_HEADS):
                cols = slice(h * GLA_DV, (h + 1) * GLA_DV)
                s_ref[:, cols] = state[:, cols] * decs[u] + upds[u][:, cols]
        for u in range(U):
            v = v_ref[rows[u], :]
            outs = []
            for h in range(GLA_HEADS):
                sh = scores[u][h * C:(h + 1) * C]
                a = jnp.where(diag_mask, sh[:, 0:C], 0.0) + jnp.where(off_mask, sh[:, C:2 * C], 0.0)
                cols = slice(h * GLA_DV, (h + 1) * GLA_DV)
                o = _dot(a.astype(BF16), v[:, cols]) + o_inters[u][:, cols]
                o = o * lax.rsqrt(jnp.mean(o * o, axis=-1, keepdims=True) + EPS) * g
                outs.append(o)
            o_all = jnp.concatenate(outs, axis=-1)
            rr = r_ref[rows[u], :].astype(F32)
            o_ref[rows[u], :] = (o_all * (rr * jax.nn.sigmoid(rr))).astype(BF16)
        return carry

    lax.fori_loop(0, seq_block // C // GLA_UNROLL, trip, 0)


def _gla(q, k, v, r, la, g, batch, seq, seq_block=1024):
    nsb = seq // seq_block
    row = lambda w: pl.BlockSpec((seq_block, w), lambda b, s: (b * nsb + s, 0))
    return pl.pallas_call(
        functools.partial(_gla_kernel, seq_block=seq_block),
        grid=(batch, nsb),
        in_specs=[row(GLA_KEY_WIDTH), row(GLA_KEY_WIDTH), row(GLA_WIDTH), row(GLA_WIDTH), row(GLA_KEY_WIDTH),
                  pl.BlockSpec((1, GLA_DV), lambda b, s: (0, 0))],
        out_specs=row(GLA_WIDTH),
        out_shape=jax.ShapeDtypeStruct((batch * seq, GLA_WIDTH), BF16),
        scratch_shapes=[pltpu.VMEM((GLA_KEY_WIDTH, GLA_WIDTH), F32)],
        compiler_params=pltpu.CompilerParams(
            dimension_semantics=("parallel", "arbitrary"), vmem_limit_bytes=VMEM_LIMIT),
        name="gla",
    )(q, k, v, r, la, g)


def _dil_kernel(slope_ref, q_ref, k_ref, v_ref, g_ref, o_ref,
                qf, kf, vf, kd, va, vb, oc, lc, *, seq):
    B = DIL_BLOCK
    U = DIL_UNROLL
    pair = pl.program_id(1)
    qf[...] = q_ref[...].astype(F32)
    kf[...] = k_ref[...].astype(F32)
    vf[...] = v_ref[...].astype(F32)

    lane = lax.broadcasted_iota(jnp.int32, (1, LANES), 1)
    first = lane < DIL_DH
    ii = lax.broadcasted_iota(jnp.int32, (B, B), 0)
    jj = lax.broadcasted_iota(jnp.int32, (B, B), 1)
    upper = jj > ii
    eye = jj == ii
    dist = jnp.bitwise_and(ii - jj, B - 1).astype(F32)
    neg = jnp.float32(-jnp.inf)
    neg_tile = jnp.full((B, B), neg, F32)
    zero_tile = jnp.zeros((B, LANES), BF16)

    for cfg, (window, r) in enumerate(DIL_CONFIGS):
        nb = seq // r // B
        cs = nb + 1
        bias_prev, bias_cur = [], []
        for hh in range(2):
            slope = slope_ref[2 * pair + hh] * (float(r) * LOG2E)
            bias = dist * (-slope)
            bias_prev.append(jnp.where(upper, bias, jnp.where(eye, -slope * float(B), neg)))
            bias_cur.append(jnp.where(upper, neg, bias))

        for c in range(r):
            rows0 = slice(c * cs * B, (c * cs + 1) * B)
            kd[rows0, :] = zero_tile
            va[rows0, :] = zero_tile
            vb[rows0, :] = zero_tile

        def prep(t4, carry, r=r, nb=nb, cs=cs):
            for j in range(4):
                t = t4 * 4 + j
                c = t // nb
                n = t % nb
                start = c + n * (B * r)
                rows = pl.ds(start, B, stride=r) if r > 1 else pl.ds(pl.multiple_of(start, B), B)
                dst = pl.ds(pl.multiple_of((c * cs + 1 + n) * B, B), B)
                kd[dst, :] = kf[rows, :].astype(BF16)
                v = vf[rows, :]
                va[dst, :] = jnp.where(first, v, 1.0).astype(BF16)
                vb[dst, :] = jnp.where(first, 1.0, v).astype(BF16)
            return carry

        lax.fori_loop(0, seq // B // 4, prep, 0)

        def geom(it, r=r, nb=nb, cs=cs):
            c = it // nb
            n = it % nb
            start = c + n * (B * r)
            rows = pl.ds(start, B, stride=r) if r > 1 else pl.ds(pl.multiple_of(start, B), B)
            kv = pl.ds(pl.multiple_of((c * cs + n) * B, B), 2 * B)
            return rows, kv

        def body(t, carry, cfg=cfg, nb=nb, bias_prev=bias_prev, bias_cur=bias_cur, geom=geom):
            geo = [geom(t * U + u) for u in range(U)]
            scores = []
            for u in range(U):
                rows, kv = geo[u]
                q = qf[rows, :]
                q_heads = jnp.concatenate([jnp.where(first, q, 0.0), jnp.where(first, 0.0, q)], axis=0)
                scores.append(_dot_nt(q_heads.astype(BF16), kd[kv, :]))
            probs, maxes = [], []
            for u in range(U):
                if nb % U == 0:
                    has_prev = True if u > 0 else (t * U) % nb > 0
                else:
                    assert U % nb == 0
                    has_prev = (u % nb) > 0
                for hh in range(2):
                    s2 = scores[u][hh * B:(hh + 1) * B]
                    if has_prev is True:
                        bp = bias_prev[hh]
                    elif has_prev is False:
                        bp = neg_tile
                    else:
                        bp = jnp.where(has_prev, bias_prev[hh], neg)
                    s_prev = s2[:, 0:B] + bp
                    s_cur = s2[:, B:2 * B] + bias_cur[hh]
                    m = jnp.max(jnp.maximum(s_prev, s_cur), axis=-1, keepdims=True)
                    probs.append(jnp.concatenate([jnp.exp2(s_prev - m), jnp.exp2(s_cur - m)], axis=1).astype(BF16))
                    maxes.append(m)
            for u in range(U):
                rows, kv = geo[u]
                acc0 = _dot(probs[2 * u], va[kv, :])
                acc1 = _dot(probs[2 * u + 1], vb[kv, :])
                num = jnp.where(first, acc0, acc1)
                den = pltpu.roll(jnp.where(first, acc1, acc0), DIL_DH, axis=1)
                oc[cfg, rows, :] = num * (1.0 / den)
                lc[cfg, rows, :] = jnp.where(first, maxes[2 * u], maxes[2 * u + 1]) + jnp.log2(den)
            return carry

        lax.fori_loop(0, seq // B // U, body, 0)

    g = g_ref[...]
    CH = 512

    def mix(i, carry):
        rows = pl.ds(pl.multiple_of(i * CH, CH), CH)
        l0, l1, l2 = lc[0, rows, :], lc[1, rows, :], lc[2, rows, :]
        m = jnp.maximum(jnp.maximum(l0, l1), l2)
        e0, e1, e2 = jnp.exp2(l0 - m), jnp.exp2(l1 - m), jnp.exp2(l2 - m)
        inv = 1.0 / (e0 + e1 + e2)
        o = (e0 * oc[0, rows, :] + e1 * oc[1, rows, :] + e2 * oc[2, rows, :]) * inv
        sq = o * o
        ms_a = jnp.sum(jnp.where(first, sq, 0.0), axis=-1, keepdims=True) * (1.0 / DIL_DH)
        ms_b = jnp.sum(jnp.where(first, 0.0, sq), axis=-1, keepdims=True) * (1.0 / DIL_DH)
        ms = jnp.where(first, ms_a, ms_b)
        o_ref[rows, :] = (o * lax.rsqrt(ms + EPS) * g).astype(BF16)
        return carry

    lax.fori_loop(0, seq // CH, mix, 0)


def _dilated(slopes, dq, dk, dv, g2, batch, seq):
    blk = pl.BlockSpec((seq, LANES), lambda b, p, s: (b, p))
    return pl.pallas_call(
        functools.partial(_dil_kernel, seq=seq),
        grid_spec=pltpu.PrefetchScalarGridSpec(
            num_scalar_prefetch=1,
            grid=(batch, DIL_WIDTH // LANES),
            in_specs=[blk, blk, blk, pl.BlockSpec((1, LANES), lambda b, p, s: (0, 0))],
            out_specs=blk,
            scratch_shapes=[pltpu.VMEM((seq, LANES), F32)] * 3
                           + [pltpu.VMEM((seq + DIL_PAD, LANES), BF16)] * 3
                           + [pltpu.VMEM((3, seq, LANES), F32)] * 2,
        ),
        out_shape=jax.ShapeDtypeStruct((batch * seq, DIL_WIDTH), BF16),
        compiler_params=pltpu.CompilerParams(
            dimension_semantics=("parallel", "parallel"), vmem_limit_bytes=VMEM_LIMIT),
        name="dilated",
    )(slopes, dq, dk, dv, g2)


def _out_proj_kernel(og_ref, od_ref, x_ref, wg_ref, wd_ref, g_ref, b_ref,
                     rwh_ref, rwl_ref, rb_ref, h_ref, hp_ref, eid_ref, gate_ref, cnt_ref, wg_s, wd_s):
    @pl.when(pl.program_id(0) == 0)
    def _():
        wg_s[...] = wg_ref[...].astype(BF16)
        wd_s[...] = wd_ref[...].astype(BF16)

    mix = _dot(og_ref[...], wg_s[...]) + _dot(od_ref[...], wd_s[...])
    h = _layer_norm(DEEPNORM_ALPHA * x_ref[...] + mix, g_ref[...], b_ref[...])
    h_ref[...] = h
    hp_ref[...] = _pack_bf16_halves(h)
    h_hi, h_lo = _split_bf16(h)
    lt = (_dot_nt(rwh_ref[...], h_hi) + _dot_nt(rwh_ref[...], h_lo) + _dot_nt(rwl_ref[...], h_hi)
          + rb_ref[...])
    tm = lt.shape[1]
    row = lax.broadcasted_iota(jnp.int32, (EXPERTS_PER_GROUP, tm), 0).astype(F32)
    neg = jnp.float32(-jnp.inf)
    big = jnp.float32(1e9)
    coarse = jnp.where(row < N_GROUPS, lt[N_EXPERTS:N_EXPERTS + EXPERTS_PER_GROUP, :], neg)
    cmax = jnp.max(coarse, axis=0, keepdims=True)
    g_idx = jnp.min(jnp.where(coarse == cmax, row, big), axis=0, keepdims=True)
    p_group = 1.0 / jnp.sum(jnp.exp(coarse - cmax), axis=0, keepdims=True)
    fine = lt[(N_GROUPS - 1) * EXPERTS_PER_GROUP:N_EXPERTS, :]
    for g in range(N_GROUPS - 2, -1, -1):
        fine = jnp.where(g_idx == g, lt[g * EXPERTS_PER_GROUP:(g + 1) * EXPERTS_PER_GROUP, :], fine)
    v1 = jnp.max(fine, axis=0, keepdims=True)
    i1 = jnp.min(jnp.where(fine == v1, row, big), axis=0, keepdims=True)
    fine2 = jnp.where(row == i1, neg, fine)
    v2 = jnp.max(fine2, axis=0, keepdims=True)
    i2 = jnp.min(jnp.where(fine2 == v2, row, big), axis=0, keepdims=True)
    e2 = jnp.exp(v2 - v1)
    den = 1.0 + e2
    gate1 = p_group * (1.0 / den)
    gate2 = p_group * (e2 / den)
    id1 = g_idx * EXPERTS_PER_GROUP + i1
    id2 = g_idx * EXPERTS_PER_GROUP + i2
    eid_ref[...] = jnp.concatenate([id1, id2], axis=0).astype(jnp.int32)
    slab = jnp.concatenate([gate1, gate2, jnp.zeros((LANES - 2, tm), F32)], axis=0)
    gate_ref[...] = slab.T[:, 0:2]
    sub = lax.broadcasted_iota(jnp.int32, (LANES, tm), 0).astype(F32)
    onehot = jnp.logical_or(sub == id1, sub == id2).astype(BF16)

    @pl.when(pl.program_id(0) == 0)
    def _():
        cnt_ref[...] = jnp.zeros_like(cnt_ref)

    cnt_ref[...] += _dot(onehot, jnp.ones((tm, LANES), BF16))


def _out_proj(og, od, x2, w_out, layer, g, b, rwh, rwl, rb, tm=1024):
    T = x2.shape[0]
    row = lambda w: pl.BlockSpec((tm, w), lambda i: (i, 0))
    full = lambda a: pl.BlockSpec(a.shape, lambda i: (0,) * a.ndim)
    half = lambda p: pl.BlockSpec((None, GLA_WIDTH, D_MODEL), lambda i: (layer, p, 0))
    return pl.pallas_call(
        _out_proj_kernel,
        grid=(T // tm,),
        in_specs=[row(GLA_WIDTH), row(DIL_WIDTH), row(D_MODEL), half(0), half(1), full(g), full(b),
                  full(rwh), full(rwl), full(rb)],
        out_specs=[row(D_MODEL), row(D_MODEL // 2), pl.BlockSpec((2, tm), lambda i: (0, i)), row(2),
                   pl.BlockSpec((LANES, LANES), lambda i: (0, 0))],
        out_shape=[jax.ShapeDtypeStruct((T, D_MODEL), F32),
                   jax.ShapeDtypeStruct((T, D_MODEL // 2), jnp.int32),
                   jax.ShapeDtypeStruct((2, T), jnp.int32),
                   jax.ShapeDtypeStruct((T, 2), F32),
                   jax.ShapeDtypeStruct((LANES, LANES), F32)],
        scratch_shapes=[pltpu.VMEM((GLA_WIDTH, D_MODEL), BF16), pltpu.VMEM((DIL_WIDTH, D_MODEL), BF16)],
        compiler_params=pltpu.CompilerParams(
            dimension_semantics=("arbitrary",), vmem_limit_bytes=VMEM_LIMIT),
        name="out_proj_router",
    )(og, od, x2, w_out, w_out, g, b, rwh, rwl, rb)


def _positions_kernel(eid_ref, cnt_ref, dest_ref, be_ref, nv_ref, carry_ref, sp_ref, tri_ref, *, tb):
    i = pl.program_id(0)

    @pl.when(i == 0)
    def _():
        shift = int(math.log2(ROW_BLOCK))
        nb_col = (cnt_ref[...].astype(jnp.int32) + (ROW_BLOCK - 1)) >> shift
        r = lax.broadcasted_iota(jnp.int32, (LANES, LANES), 0)
        c = lax.broadcasted_iota(jnp.int32, (LANES, LANES), 1)
        nb_f = jnp.where(r < N_EXPERTS, nb_col, 0).astype(F32)
        start_col = _dot((c < r).astype(BF16), nb_f.astype(BF16))
        sp_ref[...] = start_col * float(ROW_BLOCK)
        carry_ref[...] = jnp.zeros_like(carry_ref)
        be_ref[...] = jnp.concatenate([start_col.T[0:1, :], nb_f.T[0:1, :]], axis=1).astype(jnp.int32)
        total = jnp.sum(nb_f[:, 0:1], axis=0, keepdims=True)
        nv_ref[...] = jnp.broadcast_to(total, (1, LANES)).astype(jnp.int32)
        tr = lax.broadcasted_iota(jnp.int32, (tb, tb), 0)
        tc = lax.broadcasted_iota(jnp.int32, (tb, tb), 1)
        tri_ref[...] = (tr < tc).astype(BF16)

    sub = lax.broadcasted_iota(jnp.int32, (LANES, tb), 0)
    oh1 = sub == eid_ref[0:1, :]
    oh2 = sub == eid_ref[1:2, :]
    oh = jnp.logical_or(oh1, oh2).astype(BF16)
    offset = jnp.tile(carry_ref[...] + sp_ref[...], (1, tb // LANES))
    before = _dot(oh, tri_ref[...]) + offset
    d1 = jnp.sum(jnp.where(oh1, before, 0.0), axis=0, keepdims=True)
    d2 = jnp.sum(jnp.where(oh2, before, 0.0), axis=0, keepdims=True)
    dest_ref[...] = jnp.concatenate([d1, d2], axis=0).astype(jnp.int32)
    carry_ref[...] += _dot(oh, jnp.ones((tb, LANES), BF16))


def _positions(eid_t, cnt, n_blocks_pad, tb=1024):
    T = eid_t.shape[1]
    return pl.pallas_call(
        functools.partial(_positions_kernel, tb=tb),
        grid=(T // tb,),
        in_specs=[pl.BlockSpec((2, tb), lambda i: (0, i)), pl.BlockSpec((LANES, LANES), lambda i: (0, 0))],
        out_specs=[pl.BlockSpec((2, tb), lambda i: (0, i)),
                   pl.BlockSpec((1, n_blocks_pad), lambda i: (0, 0)),
                   pl.BlockSpec((1, LANES), lambda i: (0, 0))],
        out_shape=[jax.ShapeDtypeStruct((2, T), jnp.int32),
                   jax.ShapeDtypeStruct((1, n_blocks_pad), jnp.int32),
                   jax.ShapeDtypeStruct((1, LANES), jnp.int32)],
        scratch_shapes=[pltpu.VMEM((LANES, LANES), F32), pltpu.VMEM((LANES, LANES), F32),
                        pltpu.VMEM((tb, tb), BF16)],
        compiler_params=pltpu.CompilerParams(dimension_semantics=("arbitrary",)),
        name="positions",
    )(eid_t, cnt)


def _sc_gather_rows(table, idx):
    n = idx.shape[0]
    d = table.shape[1]
    info = plsc.get_sparse_core_info()
    nc, ns = info.num_cores, info.num_subcores
    per_w = n // (nc * ns)
    assert per_w * nc * ns == n and per_w % SC_INDEX_WINDOW == 0
    mesh = plsc.VectorSubcoreMesh(core_axis_name="core", subcore_axis_name="subcore")
    nchunk = per_w // SC_GATHER_ROWS
    nbuf = SC_GATHER_BUFFERS

    @functools.partial(
        pl.kernel, out_type=jax.ShapeDtypeStruct((n, d), table.dtype), mesh=mesh,
        scratch_types=[pltpu.VMEM((per_w,), jnp.int32),
                       pltpu.VMEM((nbuf, SC_GATHER_ROWS, d), table.dtype),
                       pltpu.SemaphoreType.DMA((nbuf,)), pltpu.SemaphoreType.DMA((nbuf,))],
        name="sc_gather_rows")
    def gather(x_hbm, i_hbm, o_hbm, idx_v, buf, gsem, wsem):
        wid = lax.axis_index("subcore") * nc + lax.axis_index("core")
        base = wid * per_w
        pltpu.sync_copy(i_hbm.at[pl.ds(base, per_w)], idx_v)

        def gather_copy(c):
            rows = idx_v.at[pl.ds(c * SC_GATHER_ROWS, SC_GATHER_ROWS)]
            return pltpu.make_async_copy(x_hbm.at[rows], buf.at[c % nbuf], gsem.at[c % nbuf])

        def write_copy(c):
            dst = o_hbm.at[pl.ds(base + c * SC_GATHER_ROWS, SC_GATHER_ROWS)]
            return pltpu.make_async_copy(buf.at[c % nbuf], dst, wsem.at[c % nbuf])

        for c in range(min(nbuf - 1, nchunk)):
            gather_copy(c).start()
        for c in range(nchunk):
            gather_copy(c).wait()
            write_copy(c).start()
            if c + nbuf - 1 < nchunk:
                if c >= 1:
                    write_copy(c - 1).wait()
                gather_copy(c + nbuf - 1).start()
        for c in range(max(0, nchunk - nbuf), nchunk):
            write_copy(c).wait()

    return gather(table, idx)


def _sc_inverse_rows(dest_flat, n_rows, chunk=2048):
    n = dest_flat.shape[0]
    n_tokens = n // 2
    assert n_rows <= 3 * n_tokens
    nc = plsc.get_sparse_core_info().num_cores
    mesh = plsc.VectorSubcoreMesh(core_axis_name="core", subcore_axis_name="subcore")

    @functools.partial(
        pl.kernel, out_type=jax.ShapeDtypeStruct((n_rows,), jnp.int32), mesh=mesh,
        scratch_types=[pltpu.VMEM((n_rows,), jnp.int32), pltpu.VMEM((chunk,), jnp.int32)],
        compiler_params=pltpu.CompilerParams(needs_layout_passes=False),
        name="sc_inverse_rows")
    def inverse(d_hbm, o_hbm, inv_v, d_v):
        wid = lax.axis_index("subcore") * nc + lax.axis_index("core")

        @pl.when(wid == 0)
        def _():
            lanes = lax.iota(jnp.int32, SC_LANES)

            @plsc.parallel_loop(0, n_rows // SC_LANES, unroll=8)
            def _(i):
                r = lanes + i * SC_LANES
                r = jnp.where(r >= n_tokens, r - n_tokens, r)
                inv_v[pl.ds(i * SC_LANES, SC_LANES)] = jnp.where(r >= n_tokens, r - n_tokens, r)

            @pl.loop(0, n // chunk)
            def _(c):
                pltpu.sync_copy(d_hbm.at[pl.ds(c * chunk, chunk)], d_v)

                @plsc.parallel_loop(0, chunk // SC_LANES, unroll=8)
                def _(j):
                    rows = d_v[pl.ds(j * SC_LANES, SC_LANES)]
                    pair = lax.iota(jnp.int32, SC_LANES) + (c * chunk + j * SC_LANES)
                    plsc.store_scatter(inv_v, [rows], jnp.where(pair >= n_tokens, pair - n_tokens, pair))

            pltpu.sync_copy(inv_v, o_hbm)

    return inverse(dest_flat)


def _ffn_kernel(first_ref, count_ref, nv_ref, wg_hbm, wu_hbm, wd_hbm, xs_hbm, y_hbm,
                wg32, wu32, wd32, wgb, wub, wdb, xbuf, ybuf, wsem, isem, osem, *, n_blocks):
    nv = nv_ref[0]
    nbuf = FFN_BUFFERS

    def next_expert(e):
        def more(t):
            return jnp.logical_and(t < N_EXPERTS, count_ref[jnp.minimum(t, N_EXPERTS - 1)] == 0)
        return lax.while_loop(more, lambda t: t + 1, e + 1)

    def weight_copies(e, slot):
        ee = jnp.minimum(e, N_EXPERTS - 1)
        return [pltpu.make_async_copy(src.at[ee], dst.at[slot], wsem.at[slot, i])
                for i, (src, dst) in enumerate(((wg_hbm, wg32), (wu_hbm, wu32), (wd_hbm, wd32)))]

    def fetch_weights(e, slot):
        @pl.when(e < N_EXPERTS)
        def _():
            for c in weight_copies(e, slot):
                c.start()

    def take_weights(e, slot):
        for c in weight_copies(e, slot):
            c.wait()
        wgb[...] = wg32[slot].astype(BF16)
        wub[...] = wu32[slot].astype(BF16)
        wdb[...] = wd32[slot].astype(BF16)
        fetch_weights(next_expert(next_expert(e)), slot)

    def rows_of(b):
        return pl.ds(pl.multiple_of(b * ROW_BLOCK, ROW_BLOCK), ROW_BLOCK)

    def buf_rows(b, nblk):
        return pl.ds(pl.multiple_of((b % nbuf) * ROW_BLOCK, ROW_BLOCK), nblk * ROW_BLOCK)

    def in_copy(b):
        return pltpu.make_async_copy(xs_hbm.at[rows_of(b)], xbuf.at[buf_rows(b, 1)], isem.at[b % nbuf])

    def out_copy(b):
        return pltpu.make_async_copy(ybuf.at[buf_rows(b, 1)], y_hbm.at[rows_of(b)], osem.at[b % nbuf])

    def expert_mlp(words):
        x_hi, x_lo = _unpack_bf16_halves(words)
        xb = jnp.concatenate([x_hi.astype(BF16), x_lo.astype(BF16)], axis=1)
        a = _dot(xb, wgb[...])
        u = _dot(xb, wub[...])
        hid = (a * jax.nn.sigmoid(a) * u).astype(BF16)
        return _pack_bf16_halves(_dot(hid, wdb[...]))

    ahead = nbuf - max(FFN_GROUPS)

    @pl.when(nv > 0)
    def _():
        e0 = next_expert(jnp.int32(-1))
        for i in range(ahead):
            @pl.when(i < nv)
            def _():
                in_copy(i).start()
        fetch_weights(e0, 0)
        fetch_weights(next_expert(e0), 1)
        take_weights(e0, 0)

        def step(carry):
            b, e, k = carry
            switch = b >= first_ref[e] + count_ref[e]
            e_new = jnp.where(switch, next_expert(e), e)
            k_new = jnp.where(switch, k + 1, k)

            @pl.when(switch)
            def _():
                take_weights(e_new, k_new % 2)

            end = first_ref[e_new] + count_ref[e_new]
            n = jnp.int32(1)
            for size in FFN_GROUPS:
                fits = jnp.logical_and(b + size <= end, b % nbuf + size <= nbuf)
                n = jnp.where(jnp.logical_and(n == 1, fits), size, n)
            for i in range(max(FFN_GROUPS)):
                @pl.when(jnp.logical_and(i < n, b + ahead + i < nv))
                def _():
                    in_copy(b + ahead + i).start()
            for i in range(max(FFN_GROUPS)):
                @pl.when(i < n)
                def _():
                    in_copy(b + i).wait()

                    @pl.when(b + i >= nbuf)
                    def _():
                        out_copy(b + i - nbuf).wait()

            for size in FFN_GROUPS + (1,):
                @pl.when(n == size)
                def _():
                    ybuf[buf_rows(b, size), :] = expert_mlp(xbuf[buf_rows(b, size), :])

            for i in range(max(FFN_GROUPS)):
                @pl.when(i < n)
                def _():
                    out_copy(b + i).start()
            return b + n, e_new, k_new

        lax.while_loop(lambda c: c[0] < nv, step, (jnp.int32(0), e0, jnp.int32(0)))

        for i in range(nbuf):
            @pl.when(nv > i)
            def _():
                out_copy(nv - 1 - i).wait()

    ybuf[0:ROW_BLOCK, :] = jnp.zeros((ROW_BLOCK, D_MODEL // 2), jnp.int32)

    def fill(b, carry):
        pltpu.sync_copy(ybuf.at[pl.ds(0, ROW_BLOCK)], y_hbm.at[rows_of(b)])
        return carry

    lax.fori_loop(nv, n_blocks, fill, 0)


def _ffn(first_blk, n_blk, nv, xs, w_gate, w_up, w_down):
    n_rows = xs.shape[0]
    n_blocks = n_rows // ROW_BLOCK
    anyspec = pl.BlockSpec(memory_space=pl.ANY)
    return pl.pallas_call(
        functools.partial(_ffn_kernel, n_blocks=n_blocks),
        grid_spec=pltpu.PrefetchScalarGridSpec(
            num_scalar_prefetch=3,
            grid=(1,),
            in_specs=[anyspec, anyspec, anyspec, anyspec],
            out_specs=anyspec,
            scratch_shapes=[pltpu.VMEM((2, D_MODEL, D_FF), F32), pltpu.VMEM((2, D_MODEL, D_FF), F32),
                            pltpu.VMEM((2, D_FF, D_MODEL), F32),
                            pltpu.VMEM((D_MODEL, D_FF), BF16), pltpu.VMEM((D_MODEL, D_FF), BF16),
                            pltpu.VMEM((D_FF, D_MODEL), BF16),
                            pltpu.VMEM((FFN_BUFFERS * ROW_BLOCK, D_MODEL // 2), jnp.int32),
                            pltpu.VMEM((FFN_BUFFERS * ROW_BLOCK, D_MODEL // 2), jnp.int32),
                            pltpu.SemaphoreType.DMA((2, 3)),
                            pltpu.SemaphoreType.DMA((FFN_BUFFERS,)),
                            pltpu.SemaphoreType.DMA((FFN_BUFFERS,))],
        ),
        out_shape=jax.ShapeDtypeStruct((n_rows, D_MODEL // 2), jnp.int32),
        compiler_params=pltpu.CompilerParams(
            dimension_semantics=("arbitrary",), vmem_limit_bytes=VMEM_LIMIT),
        name="expert_ffn",
    )(first_blk, n_blk, nv, w_gate, w_up, w_down, xs)


def _combine_kernel(h_ref, ya_ref, yb_ref, gate_ref, g_ref, b_ref, o_ref):
    gate = gate_ref[...]
    a_hi, a_lo = _unpack_bf16_halves(ya_ref[...])
    b_hi, b_lo = _unpack_bf16_halves(yb_ref[...])
    g0, g1 = gate[:, 0:1], gate[:, 1:2]
    ffn = jnp.concatenate([a_hi * g0 + b_hi * g1, a_lo * g0 + b_lo * g1], axis=1)
    o_ref[...] = _layer_norm(DEEPNORM_ALPHA * h_ref[...] + ffn, g_ref[...], b_ref[...])


def _combine(h, y2, gate, g, b, tm=512):
    T = h.shape[0]
    nt = T // tm
    return pl.pallas_call(
        _combine_kernel,
        grid=(nt,),
        in_specs=[pl.BlockSpec((tm, D_MODEL), lambda i: (i, 0)),
                  pl.BlockSpec((tm, D_MODEL // 2), lambda i: (i, 0)),
                  pl.BlockSpec((tm, D_MODEL // 2), lambda i: (i + nt, 0)),
                  pl.BlockSpec((tm, 2), lambda i: (i, 0)),
                  pl.BlockSpec((1, D_MODEL), lambda i: (0, 0)),
                  pl.BlockSpec((1, D_MODEL), lambda i: (0, 0))],
        out_specs=pl.BlockSpec((tm, D_MODEL), lambda i: (i, 0)),
        out_shape=jax.ShapeDtypeStruct((T, D_MODEL), F32),
        compiler_params=pltpu.CompilerParams(
            dimension_semantics=("parallel",), vmem_limit_bytes=VMEM_LIMIT),
        name="combine",
    )(h, y2, y2, gate, g, b)


def kernel(x, w_in, gla_gate_w2, gla_gate_b, gla_norm_g, dil_norm_g, w_out, ln1_g, ln1_b,
           router_coarse_w, router_coarse_b, router_fine_w, router_fine_b,
           expert_w_gate, expert_w_up, expert_w_down, ln2_g, ln2_b):
    B, S, D = x.shape
    T = B * S
    depth = w_in.shape[0]
    slopes = jnp.exp2(-8.0 * jnp.arange(1, DIL_HEADS + 1, dtype=F32) / DIL_HEADS)
    n_rows = 2 * T + N_EXPERTS * ROW_BLOCK
    n_blocks_pad = 2 * LANES
    h = x.reshape(T, D)
    for l in range(depth):
        w2 = jnp.pad(gla_gate_w2[l], ((0, LANES - GLA_GATE_RANK), (0, 0)))
        w2h, w2l = _split_bf16(w2)
        q, k, v, r, la, dq, dk, dv = _in_proj(h, w_in, l, w2h, w2l, gla_gate_b[l][None, :])
        o_gla = _gla(q, k, v, r, la, gla_norm_g[l][None, :], B, S)
        g2 = jnp.tile(dil_norm_g[l], 2)[None, :]
        o_dil = _dilated(slopes, dq, dk, dv, g2, B, S)
        rw = jnp.concatenate([router_fine_w[l].reshape(D, N_EXPERTS), router_coarse_w[l]], axis=1)
        rw = jnp.pad(rw, ((0, 0), (0, LANES - N_EXPERTS - N_GROUPS))).T
        rwh, rwl = _split_bf16(rw)
        rb = jnp.concatenate([router_fine_b[l].reshape(N_EXPERTS), router_coarse_b[l]])
        rb = jnp.pad(rb, (0, LANES - N_EXPERTS - N_GROUPS))[:, None]
        h1, h1p, eid_t, gate, cnt = _out_proj(o_gla, o_dil, h, w_out, l,
                                              ln1_g[l][None, :], ln1_b[l][None, :], rwh, rwl, rb)
        dest_t, be, nv = _positions(eid_t, cnt, n_blocks_pad)
        dest_flat = dest_t.reshape(2 * T)
        src_tok = _sc_inverse_rows(dest_flat, n_rows)
        xs = _sc_gather_rows(h1p, src_tok)
        be = be.reshape(n_blocks_pad)
        y = _ffn(be[:N_EXPERTS], be[LANES:LANES + N_EXPERTS], nv.reshape(LANES)[:1], xs,
                 expert_w_gate[l], expert_w_up[l], expert_w_down[l])
        y2 = _sc_gather_rows(y, dest_flat)
        h = _combine(h1, y2, gate, ln2_g[l][None, :], ln2_b[l][None, :])
    return h.reshape(B, S, D)
```

```python
import functools
import math

import jax
import jax.numpy as jnp
from jax import lax
from jax.experimental import pallas as pl
from jax.experimental.pallas import tpu as pltpu
from jax.experimental.pallas import tpu_sc as plsc

D_MODEL = 1024
GLA_HEADS = 4
GLA_DK = 64
GLA_DV = 128
GLA_KEY_WIDTH = GLA_HEADS * GLA_DK
GLA_WIDTH = GLA_HEADS * GLA_DV
GLA_GATE_RANK = 16
GLA_GATE_TEMP = 16.0
DIL_HEADS = 8
DIL_DH = 64
DIL_WIDTH = DIL_HEADS * DIL_DH
DIL_CONFIGS = ((128, 1), (512, 4), (2048, 16))
DIL_BLOCK = 128
DIL_MAX_R = max(r for _, r in DIL_CONFIGS)
DIL_PAD = DIL_BLOCK * DIL_MAX_R
DIL_UNROLL = 16
DIL_SPLIT_CFG = 2
N_GROUPS = 4
EXPERTS_PER_GROUP = 8
N_EXPERTS = N_GROUPS * EXPERTS_PER_GROUP
D_FF = 512
DEEPNORM_ALPHA = 2.0 ** 0.25
EPS = 1e-5
IN_PROJ_GATE_COL = 2 * GLA_KEY_WIDTH + 2 * GLA_WIDTH
LOG2E = math.log2(math.e)

LANES = 128
SUBLANES = 8
assert DIL_SPLIT_CFG == len(DIL_CONFIGS) - 1 and DIL_CONFIGS[DIL_SPLIT_CFG][1] % (2 * SUBLANES) == 0
GLA_CHUNK = 128
GLA_SUB = 64
GLA_UNROLL = 4
SC_LANES = 16
SC_INDEX_WINDOW = 128
SC_GATHER_BUFFERS = 6
SC_GATHER_ROWS = 32
FFN_BUFFERS = 12
FFN_GROUPS = (4, 2)
ROW_BLOCK = 256
VMEM_LIMIT = 56 * 1024 * 1024

F32 = jnp.float32
BF16 = jnp.bfloat16


def _dot(a, b):
    return jnp.dot(a, b, preferred_element_type=F32)


def _dot_nt(a, b):
    return lax.dot_general(a, b, (((1,), (1,)), ((), ())), preferred_element_type=F32)


def _dot_tn(a, b):
    return lax.dot_general(a, b, (((0,), (0,)), ((), ())), preferred_element_type=F32)


def _split_bf16(v):
    hi = v.astype(BF16)
    lo = (v - hi.astype(F32)).astype(BF16)
    return hi, lo


def _pack_bf16_halves(v):
    w = v.shape[1] // 2
    hi = lax.bitcast_convert_type(v[:, :w].astype(BF16).astype(F32), jnp.int32)
    lo = lax.bitcast_convert_type(v[:, w:].astype(BF16).astype(F32), jnp.int32)
    return hi | lax.shift_right_logical(lo, 16)


def _unpack_bf16_halves(words):
    hi = lax.bitcast_convert_type(words & jnp.int32(-65536), F32)
    lo = lax.bitcast_convert_type(lax.shift_left(words, 16), F32)
    return hi, lo


def _layer_norm(v, g, b):
    mu = jnp.mean(v, axis=-1, keepdims=True)
    c = v - mu
    var = jnp.mean(c * c, axis=-1, keepdims=True)
    return c * lax.rsqrt(var + EPS) * g + b


def _in_proj_kernel(x_ref, w_ref, w2h_ref, w2l_ref, gb_ref,
                    q_ref, k_ref, v_ref, r_ref, la_ref, dq_ref, dk_ref, dv_ref, wg_s, wd_s, wa_s):
    a0 = IN_PROJ_GATE_COL

    @pl.when(pl.program_id(0) == 0)
    def _():
        wg_s[...] = w_ref[:, 0:a0].astype(BF16)
        gate_tile = w_ref[:, a0:a0 + LANES]
        lane = lax.broadcasted_iota(jnp.int32, gate_tile.shape, 1)
        wa_s[...] = jnp.where(lane < GLA_GATE_RANK, gate_tile, 0.0).astype(BF16)
        tail = w_ref[:, a0:]
        wd_s[...] = tail[:, GLA_GATE_RANK:].astype(BF16)

    xb = x_ref[...].astype(BF16)

    def piece(w_s, c0, c1):
        return _dot(xb, w_s[:, c0:c1])

    q_ref[...] = (piece(wg_s, 0, 256) * (GLA_DK ** -0.5)).astype(BF16)
    k_ref[...] = piece(wg_s, 256, 512).astype(BF16)
    v_ref[...] = piece(wg_s, 512, 1024).astype(BF16)
    r_ref[...] = piece(wg_s, 1024, 1536).astype(BF16)
    dq_ref[...] = (piece(wd_s, 0, 512) * (DIL_DH ** -0.5 * LOG2E)).astype(BF16)
    dk_ref[...] = piece(wd_s, 512, 1024).astype(BF16)
    dv_ref[...] = piece(wd_s, 1024, 1536).astype(BF16)
    ga = _dot(xb, wa_s[...])
    ga_hi, ga_lo = _split_bf16(ga)
    z = _dot(ga_hi, w2h_ref[...]) + _dot(ga_lo, w2h_ref[...]) + _dot(ga_hi, w2l_ref[...]) + gb_ref[...]
    log_sig = jnp.minimum(z, 0.0) - jnp.log1p(jnp.exp(-jnp.abs(z)))
    la_ref[...] = log_sig * (1.0 / GLA_GATE_TEMP)


def _in_proj(x2, w_in, layer, w2h, w2l, gb, tm=1024):
    T = x2.shape[0]
    row = lambda wd: pl.BlockSpec((tm, wd), lambda i: (i, 0))
    full = lambda a: pl.BlockSpec(a.shape, lambda i: (0,) * a.ndim)
    outs = [(GLA_KEY_WIDTH, BF16), (GLA_KEY_WIDTH, BF16), (GLA_WIDTH, BF16), (GLA_WIDTH, BF16), (GLA_KEY_WIDTH, F32),
            (DIL_WIDTH, BF16), (DIL_WIDTH, BF16), (DIL_WIDTH, BF16)]
    group = w_in.shape[2] - IN_PROJ_GATE_COL - GLA_GATE_RANK
    return pl.pallas_call(
        _in_proj_kernel,
        grid=(T // tm,),
        in_specs=[row(D_MODEL),
                  pl.BlockSpec((None,) + w_in.shape[1:], lambda i: (layer, 0, 0), pipeline_mode=pl.Buffered(1)),
                  full(w2h), full(w2l), full(gb)],
        out_specs=[row(wd) for wd, _ in outs],
        out_shape=[jax.ShapeDtypeStruct((T, wd), dt) for wd, dt in outs],
        scratch_shapes=[pltpu.VMEM((D_MODEL, IN_PROJ_GATE_COL), BF16), pltpu.VMEM((D_MODEL, group), BF16),
                        pltpu.VMEM((D_MODEL, LANES), BF16)],
        compiler_params=pltpu.CompilerParams(
            dimension_semantics=("arbitrary",), vmem_limit_bytes=VMEM_LIMIT),
        name="in_proj",
    )(x2, w_in, w2h, w2l, gb)


def _gla_kernel(q_ref, k_ref, v_ref, r_ref, la_ref, g_ref, o_ref, s_ref, *, seq_block):
    C = GLA_CHUNK
    H = GLA_SUB
    assert C == 2 * H

    @pl.when(pl.program_id(1) == 0)
    def _():
        s_ref[...] = jnp.zeros_like(s_ref)

    ri = lax.broadcasted_iota(jnp.int32, (C, C), 0)
    ci = lax.broadcasted_iota(jnp.int32, (C, C), 1)
    same_sub = (ri // H) == (ci // H)
    sum_ops = jnp.concatenate([jnp.logical_and(same_sub, ci <= ri).astype(BF16), same_sub.astype(BF16),
                               jnp.ones((C, C), BF16)], axis=0)
    diag_mask = jnp.logical_and(same_sub, ci <= ri)
    off_mask = (ri // H) > (ci // H)
    second = lax.broadcasted_iota(jnp.int32, (C, 1), 0) >= H
    ones_cl = jnp.ones((C, LANES), BF16)
    lane_k = lax.broadcasted_iota(jnp.int32, (1, GLA_KEY_WIDTH), 1) // GLA_DK
    head_masks = [(lane_k == h).astype(F32) for h in range(GLA_HEADS)]
    srow = lax.broadcasted_iota(jnp.int32, (GLA_KEY_WIDTH, GLA_WIDTH), 0) // GLA_DK
    scol = lax.broadcasted_iota(jnp.int32, (GLA_KEY_WIDTH, GLA_WIDTH), 1) // GLA_DV
    state_mask = (srow == scol).astype(F32)
    g = g_ref[...]

    def trip(t, carry):
        U = GLA_UNROLL
        rows = [pl.ds(pl.multiple_of((t * U + u) * C, C), C) for u in range(U)]
        la2s, sums = [], []
        for u in range(U):
            la_hi, la_lo = _split_bf16(la_ref[rows[u], :])
            la2 = jnp.concatenate([la_hi, la_lo], axis=1)
            la2s.append(la2)
            sm = _dot(sum_ops, la2)
            sums.append(sm[:, 0:GLA_KEY_WIDTH] + sm[:, GLA_KEY_WIDTH:])
        q_states, k_states, scores = [], [], []
        for u in range(U):
            b = sums[u][0:C]
            t_sub = sums[u][C:2 * C]
            other = sums[u][2 * C:3 * C] - t_sub
            q = q_ref[rows[u], :].astype(F32)
            k = k_ref[rows[u], :].astype(F32)
            qd = q * jnp.exp(b)
            kd = (k * jnp.exp(-b)).astype(BF16)
            ke_f = k * jnp.exp(t_sub - b)
            cross = jnp.exp(other)
            q_states.append((qd * jnp.where(second, cross, 1.0)).astype(BF16))
            k_states.append((ke_f * jnp.where(second, 1.0, cross)).astype(BF16))
            q_heads = jnp.concatenate([(qd * head_masks[h]).astype(BF16) for h in range(GLA_HEADS)], axis=0)
            keys2 = jnp.concatenate([kd, ke_f.astype(BF16)], axis=0)
            scores.append(_dot_nt(q_heads, keys2))
        decs, upds = [], []
        for u in range(U):
            tot = _dot_tn(la2s[u], ones_cl)
            decs.append(jnp.exp(tot[0:GLA_KEY_WIDTH] + tot[GLA_KEY_WIDTH:]))
            upds.append(_dot_tn(k_states[u], v_ref[rows[u], :]) * state_mask)
        o_inters = []
        for u in range(U):
            state = s_ref[...]
            o_inters.append(_dot(q_states[u], state.astype(BF16)))
            for h in range(GLA_HEADS):
                cols = slice(h * GLA_DV, (h + 1) * GLA_DV)
                s_ref[:, cols] = state[:, cols] * decs[u] + upds[u][:, cols]
        for u in range(U):
            v = v_ref[rows[u], :]
            outs = []
            for h in range(GLA_HEADS):
                sh = scores[u][h * C:(h + 1) * C]
                a = jnp.where(diag_mask, sh[:, 0:C], 0.0) + jnp.where(off_mask, sh[:, C:2 * C], 0.0)
                cols = slice(h * GLA_DV, (h + 1) * GLA_DV)
                o = _dot(a.astype(BF16), v[:, cols]) + o_inters[u][:, cols]
                o = o * lax.rsqrt(jnp.mean(o * o, axis=-1, keepdims=True) + EPS) * g
                outs.append(o)
            o_all = jnp.concatenate(outs, axis=-1)
            rr = r_ref[rows[u], :].astype(F32)
            o_ref[rows[u], :] = (o_all * (rr * jax.nn.sigmoid(rr))).astype(BF16)
        return carry

    lax.fori_loop(0, seq_block // C // GLA_UNROLL, trip, 0)


def _gla(q, k, v, r, la, g, batch, seq, seq_block=1024):
    nsb = seq // seq_block
    row = lambda w: pl.BlockSpec((seq_block, w), lambda b, s: (b * nsb + s, 0))
    return pl.pallas_call(
        functools.partial(_gla_kernel, seq_block=seq_block),
        grid=(batch, nsb),
        in_specs=[row(GLA_KEY_WIDTH), row(GLA_KEY_WIDTH), row(GLA_WIDTH), row(GLA_WIDTH), row(GLA_KEY_WIDTH),
                  pl.BlockSpec((1, GLA_DV), lambda b, s: (0, 0))],
        out_specs=row(GLA_WIDTH),
        out_shape=jax.ShapeDtypeStruct((batch * seq, GLA_WIDTH), BF16),
        scratch_shapes=[pltpu.VMEM((GLA_KEY_WIDTH, GLA_WIDTH), F32)],
        compiler_params=pltpu.CompilerParams(
            dimension_semantics=("parallel", "arbitrary"), vmem_limit_bytes=VMEM_LIMIT),
        name="gla",
    )(q, k, v, r, la, g)


def _dil_kernel(slope_ref, q_ref, k_ref, v_ref, g_ref, o_ref,
                qf, kf, vf, qs, ks, vs, kd, va, vb, oc, lc, ocs, lcs, *, seq):
    B = DIL_BLOCK
    U = DIL_UNROLL
    pair = pl.program_id(1)
    for src, nat, split in ((q_ref, qf, qs), (k_ref, kf, ks), (v_ref, vf, vs)):
        x = src[...].astype(F32)
        nat[...] = x
        x3 = x.reshape(seq // (2 * SUBLANES), 2 * SUBLANES, LANES)
        split[0] = x3[:, :SUBLANES, :].reshape(seq // 2, LANES)
        split[1] = x3[:, SUBLANES:, :].reshape(seq // 2, LANES)

    lane = lax.broadcasted_iota(jnp.int32, (1, LANES), 1)
    first = lane < DIL_DH
    ii = lax.broadcasted_iota(jnp.int32, (B, B), 0)
    jj = lax.broadcasted_iota(jnp.int32, (B, B), 1)
    upper = jj > ii
    eye = jj == ii
    dist = jnp.bitwise_and(ii - jj, B - 1).astype(F32)
    neg = jnp.float32(-jnp.inf)
    neg_tile = jnp.full((B, B), neg, F32)
    zero_tile = jnp.zeros((B, LANES), BF16)

    def split_rows(r, c, n):
        half = r // 2
        start = c % SUBLANES + c // (2 * SUBLANES) * SUBLANES + n * (B * half)
        return (c // SUBLANES) % 2, pl.ds(start, B, stride=half)

    def class_rows(nat, split, cfg, r, c, n):
        if r == 1:
            return nat[pl.ds(pl.multiple_of(n * B, B), B), :]
        if cfg == DIL_SPLIT_CFG:
            parity, rows = split_rows(r, c, n)
            return split[parity, rows, :]
        return nat[pl.ds(c + n * (B * r), B, stride=r), :]

    for cfg, (window, r) in enumerate(DIL_CONFIGS):
        nb = seq // r // B
        cs = nb + 1
        bias_prev, bias_cur = [], []
        for hh in range(2):
            slope = slope_ref[2 * pair + hh] * (float(r) * LOG2E)
            bias = dist * (-slope)
            bias_prev.append(jnp.where(upper, bias, jnp.where(eye, -slope * float(B), neg)))
            bias_cur.append(jnp.where(upper, neg, bias))

        for c in range(r):
            rows0 = slice(c * cs * B, (c * cs + 1) * B)
            kd[rows0, :] = zero_tile
            va[rows0, :] = zero_tile
            vb[rows0, :] = zero_tile

        def prep(t4, carry, cfg=cfg, r=r, nb=nb, cs=cs):
            for j in range(4):
                t = t4 * 4 + j
                c = t // nb
                n = t % nb
                dst = pl.ds(pl.multiple_of((c * cs + 1 + n) * B, B), B)
                kd[dst, :] = class_rows(kf, ks, cfg, r, c, n).astype(BF16)
                v = class_rows(vf, vs, cfg, r, c, n)
                va[dst, :] = jnp.where(first, v, 1.0).astype(BF16)
                vb[dst, :] = jnp.where(first, 1.0, v).astype(BF16)
            return carry

        lax.fori_loop(0, seq // B // 4, prep, 0)

        def geom(it, r=r, nb=nb, cs=cs):
            c = it // nb
            n = it % nb
            start = c + n * (B * r)
            rows = pl.ds(start, B, stride=r) if r > 1 else pl.ds(pl.multiple_of(start, B), B)
            kv = pl.ds(pl.multiple_of((c * cs + n) * B, B), 2 * B)
            return rows, kv, c, n

        def body(t, carry, cfg=cfg, r=r, nb=nb, bias_prev=bias_prev, bias_cur=bias_cur, geom=geom):
            geo = [geom(t * U + u) for u in range(U)]
            scores = []
            for u in range(U):
                _, kv, c, n = geo[u]
                q = class_rows(qf, qs, cfg, r, c, n)
                q_heads = jnp.concatenate([jnp.where(first, q, 0.0), jnp.where(first, 0.0, q)], axis=0)
                scores.append(_dot_nt(q_heads.astype(BF16), kd[kv, :]))
            probs, maxes = [], []
            for u in range(U):
                if nb % U == 0:
                    has_prev = True if u > 0 else (t * U) % nb > 0
                else:
                    assert U % nb == 0
                    has_prev = (u % nb) > 0
                for hh in range(2):
                    s2 = scores[u][hh * B:(hh + 1) * B]
                    if has_prev is True:
                        bp = bias_prev[hh]
                    elif has_prev is False:
                        bp = neg_tile
                    else:
                        bp = jnp.where(has_prev, bias_prev[hh], neg)
                    s_prev = s2[:, 0:B] + bp
                    s_cur = s2[:, B:2 * B] + bias_cur[hh]
                    m = jnp.max(jnp.maximum(s_prev, s_cur), axis=-1, keepdims=True)
                    probs.append(jnp.concatenate([jnp.exp2(s_prev - m), jnp.exp2(s_cur - m)], axis=1).astype(BF16))
                    maxes.append(m)
            for u in range(U):
                rows, kv, c, n = geo[u]
                acc0 = _dot(probs[2 * u], va[kv, :])
                acc1 = _dot(probs[2 * u + 1], vb[kv, :])
                num = jnp.where(first, acc0, acc1)
                den = pltpu.roll(jnp.where(first, acc1, acc0), DIL_DH, axis=1)
                out = num * (1.0 / den)
                lse = jnp.where(first, maxes[2 * u], maxes[2 * u + 1]) + jnp.log2(den)
                if cfg == DIL_SPLIT_CFG:
                    parity, dst = split_rows(r, c, n)
                    ocs[parity, dst, :] = out
                    lcs[parity, dst, :] = lse
                else:
                    oc[cfg, rows, :] = out
                    lc[cfg, rows, :] = lse
            return carry

        lax.fori_loop(0, seq // B // U, body, 0)

    g = g_ref[...]
    CH = 512

    def mix(i, carry):
        rows = pl.ds(pl.multiple_of(i * CH, CH), CH)
        half_rows = pl.ds(pl.multiple_of(i * (CH // 2), CH // 2), CH // 2)

        def tiles_interleaved(split):
            even = split[0, half_rows, :].reshape(CH // (2 * SUBLANES), 1, SUBLANES, LANES)
            odd = split[1, half_rows, :].reshape(CH // (2 * SUBLANES), 1, SUBLANES, LANES)
            return jnp.concatenate([even, odd], axis=1).reshape(CH, LANES)

        l0, l1, l2 = lc[0, rows, :], lc[1, rows, :], tiles_interleaved(lcs)
        m = jnp.maximum(jnp.maximum(l0, l1), l2)
        e0, e1, e2 = jnp.exp2(l0 - m), jnp.exp2(l1 - m), jnp.exp2(l2 - m)
        inv = 1.0 / (e0 + e1 + e2)
        o = (e0 * oc[0, rows, :] + e1 * oc[1, rows, :] + e2 * tiles_interleaved(ocs)) * inv
        sq = o * o
        ms_a = jnp.sum(jnp.where(first, sq, 0.0), axis=-1, keepdims=True) * (1.0 / DIL_DH)
        ms_b = jnp.sum(jnp.where(first, 0.0, sq), axis=-1, keepdims=True) * (1.0 / DIL_DH)
        ms = jnp.where(first, ms_a, ms_b)
        o_ref[rows, :] = (o * lax.rsqrt(ms + EPS) * g).astype(BF16)
        return carry

    lax.fori_loop(0, seq // CH, mix, 0)


def _dilated(slopes, dq, dk, dv, g2, batch, seq):
    blk = pl.BlockSpec((seq, LANES), lambda b, p, s: (b, p))
    return pl.pallas_call(
        functools.partial(_dil_kernel, seq=seq),
        grid_spec=pltpu.PrefetchScalarGridSpec(
            num_scalar_prefetch=1,
            grid=(batch, DIL_WIDTH // LANES),
            in_specs=[blk, blk, blk, pl.BlockSpec((1, LANES), lambda b, p, s: (0, 0))],
            out_specs=blk,
            scratch_shapes=[pltpu.VMEM((seq, LANES), F32)] * 3 + [pltpu.VMEM((2, seq // 2, LANES), F32)] * 3
                           + [pltpu.VMEM((seq + DIL_PAD, LANES), BF16)] * 3
                           + [pltpu.VMEM((2, seq, LANES), F32)] * 2 + [pltpu.VMEM((2, seq // 2, LANES), F32)] * 2,
        ),
        out_shape=jax.ShapeDtypeStruct((batch * seq, DIL_WIDTH), BF16),
        compiler_params=pltpu.CompilerParams(
            dimension_semantics=("parallel", "parallel"), vmem_limit_bytes=VMEM_LIMIT),
        name="dilated",
    )(slopes, dq, dk, dv, g2)


def _out_proj_kernel(og_ref, od_ref, x_ref, wg_ref, wd_ref, g_ref, b_ref,
                     rwh_ref, rwl_ref, rb_ref, h_ref, hp_ref, eid_ref, gate_ref, cnt_ref, wg_s, wd_s):
    @pl.when(pl.program_id(0) == 0)
    def _():
        wg_s[...] = wg_ref[...].astype(BF16)
        wd_s[...] = wd_ref[...].astype(BF16)

    mix = _dot(og_ref[...], wg_s[...]) + _dot(od_ref[...], wd_s[...])
    h = _layer_norm(DEEPNORM_ALPHA * x_ref[...] + mix, g_ref[...], b_ref[...])
    h_ref[...] = h
    hp_ref[...] = _pack_bf16_halves(h)
    h_hi, h_lo = _split_bf16(h)
    lt = (_dot_nt(rwh_ref[...], h_hi) + _dot_nt(rwh_ref[...], h_lo) + _dot_nt(rwl_ref[...], h_hi)
          + rb_ref[...])
    tm = lt.shape[1]
    row = lax.broadcasted_iota(jnp.int32, (EXPERTS_PER_GROUP, tm), 0).astype(F32)
    neg = jnp.float32(-jnp.inf)
    big = jnp.float32(1e9)
    coarse = jnp.where(row < N_GROUPS, lt[N_EXPERTS:N_EXPERTS + EXPERTS_PER_GROUP, :], neg)
    cmax = jnp.max(coarse, axis=0, keepdims=True)
    g_idx = jnp.min(jnp.where(coarse == cmax, row, big), axis=0, keepdims=True)
    p_group = 1.0 / jnp.sum(jnp.exp(coarse - cmax), axis=0, keepdims=True)
    fine = lt[(N_GROUPS - 1) * EXPERTS_PER_GROUP:N_EXPERTS, :]
    for g in range(N_GROUPS - 2, -1, -1):
        fine = jnp.where(g_idx == g, lt[g * EXPERTS_PER_GROUP:(g + 1) * EXPERTS_PER_GROUP, :], fine)
    v1 = jnp.max(fine, axis=0, keepdims=True)
    i1 = jnp.min(jnp.where(fine == v1, row, big), axis=0, keepdims=True)
    fine2 = jnp.where(row == i1, neg, fine)
    v2 = jnp.max(fine2, axis=0, keepdims=True)
    i2 = jnp.min(jnp.where(fine2 == v2, row, big), axis=0, keepdims=True)
    e2 = jnp.exp(v2 - v1)
    den = 1.0 + e2
    gate1 = p_group * (1.0 / den)
    gate2 = p_group * (e2 / den)
    id1 = g_idx * EXPERTS_PER_GROUP + i1
    id2 = g_idx * EXPERTS_PER_GROUP + i2
    eid_ref[...] = jnp.concatenate([id1, id2], axis=0).astype(jnp.int32)
    slab = jnp.concatenate([gate1, gate2, jnp.zeros((LANES - 2, tm), F32)], axis=0)
    gate_ref[...] = slab.T[:, 0:2]
    sub = lax.broadcasted_iota(jnp.int32, (LANES, tm), 0).astype(F32)
    onehot = jnp.logical_or(sub == id1, sub == id2).astype(BF16)

    @pl.when(pl.program_id(0) == 0)
    def _():
        cnt_ref[...] = jnp.zeros_like(cnt_ref)

    cnt_ref[...] += _dot(onehot, jnp.ones((tm, LANES), BF16))


def _out_proj(og, od, x2, w_out, layer, g, b, rwh, rwl, rb, tm=1024):
    T = x2.shape[0]
    row = lambda w: pl.BlockSpec((tm, w), lambda i: (i, 0))
    full = lambda a: pl.BlockSpec(a.shape, lambda i: (0,) * a.ndim)
    half = lambda p: pl.BlockSpec((None, GLA_WIDTH, D_MODEL), lambda i: (layer, p, 0))
    return pl.pallas_call(
        _out_proj_kernel,
        grid=(T // tm,),
        in_specs=[row(GLA_WIDTH), row(DIL_WIDTH), row(D_MODEL), half(0), half(1), full(g), full(b),
                  full(rwh), full(rwl), full(rb)],
        out_specs=[row(D_MODEL), row(D_MODEL // 2), pl.BlockSpec((2, tm), lambda i: (0, i)), row(2),
                   pl.BlockSpec((LANES, LANES), lambda i: (0, 0))],
        out_shape=[jax.ShapeDtypeStruct((T, D_MODEL), F32),
                   jax.ShapeDtypeStruct((T, D_MODEL // 2), jnp.int32),
                   jax.ShapeDtypeStruct((2, T), jnp.int32),
                   jax.ShapeDtypeStruct((T, 2), F32),
                   jax.ShapeDtypeStruct((LANES, LANES), F32)],
        scratch_shapes=[pltpu.VMEM((GLA_WIDTH, D_MODEL), BF16), pltpu.VMEM((DIL_WIDTH, D_MODEL), BF16)],
        compiler_params=pltpu.CompilerParams(
            dimension_semantics=("arbitrary",), vmem_limit_bytes=VMEM_LIMIT),
        name="out_proj_router",
    )(og, od, x2, w_out, w_out, g, b, rwh, rwl, rb)


def _positions_kernel(eid_ref, cnt_ref, dest_ref, be_ref, nv_ref, carry_ref, sp_ref, tri_ref, *, tb):
    i = pl.program_id(0)

    @pl.when(i == 0)
    def _():
        shift = int(math.log2(ROW_BLOCK))
        nb_col = (cnt_ref[...].astype(jnp.int32) + (ROW_BLOCK - 1)) >> shift
        r = lax.broadcasted_iota(jnp.int32, (LANES, LANES), 0)
        c = lax.broadcasted_iota(jnp.int32, (LANES, LANES), 1)
        nb_f = jnp.where(r < N_EXPERTS, nb_col, 0).astype(F32)
        start_col = _dot((c < r).astype(BF16), nb_f.astype(BF16))
        sp_ref[...] = start_col * float(ROW_BLOCK)
        carry_ref[...] = jnp.zeros_like(carry_ref)
        be_ref[...] = jnp.concatenate([start_col.T[0:1, :], nb_f.T[0:1, :]], axis=1).astype(jnp.int32)
        total = jnp.sum(nb_f[:, 0:1], axis=0, keepdims=True)
        nv_ref[...] = jnp.broadcast_to(total, (1, LANES)).astype(jnp.int32)
        tr = lax.broadcasted_iota(jnp.int32, (tb, tb), 0)
        tc = lax.broadcasted_iota(jnp.int32, (tb, tb), 1)
        tri_ref[...] = (tr < tc).astype(BF16)

    sub = lax.broadcasted_iota(jnp.int32, (LANES, tb), 0)
    oh1 = sub == eid_ref[0:1, :]
    oh2 = sub == eid_ref[1:2, :]
    oh = jnp.logical_or(oh1, oh2).astype(BF16)
    offset = jnp.tile(carry_ref[...] + sp_ref[...], (1, tb // LANES))
    before = _dot(oh, tri_ref[...]) + offset
    d1 = jnp.sum(jnp.where(oh1, before, 0.0), axis=0, keepdims=True)
    d2 = jnp.sum(jnp.where(oh2, before, 0.0), axis=0, keepdims=True)
    dest_ref[...] = jnp.concatenate([d1, d2], axis=0).astype(jnp.int32)
    carry_ref[...] += _dot(oh, jnp.ones((tb, LANES), BF16))


def _positions(eid_t, cnt, n_blocks_pad, tb=1024):
    T = eid_t.shape[1]
    return pl.pallas_call(
        functools.partial(_positions_kernel, tb=tb),
        grid=(T // tb,),
        in_specs=[pl.BlockSpec((2, tb), lambda i: (0, i)), pl.BlockSpec((LANES, LANES), lambda i: (0, 0))],
        out_specs=[pl.BlockSpec((2, tb), lambda i: (0, i)),
                   pl.BlockSpec((1, n_blocks_pad), lambda i: (0, 0)),
                   pl.BlockSpec((1, LANES), lambda i: (0, 0))],
        out_shape=[jax.ShapeDtypeStruct((2, T), jnp.int32),
                   jax.ShapeDtypeStruct((1, n_blocks_pad), jnp.int32),
                   jax.ShapeDtypeStruct((1, LANES), jnp.int32)],
        scratch_shapes=[pltpu.VMEM((LANES, LANES), F32), pltpu.VMEM((LANES, LANES), F32),
                        pltpu.VMEM((tb, tb), BF16)],
        compiler_params=pltpu.CompilerParams(dimension_semantics=("arbitrary",)),
        name="positions",
    )(eid_t, cnt)


def _sc_gather_rows(table, idx):
    n = idx.shape[0]
    d = table.shape[1]
    info = plsc.get_sparse_core_info()
    nc, ns = info.num_cores, info.num_subcores
    per_w = n // (nc * ns)
    assert per_w * nc * ns == n and per_w % SC_INDEX_WINDOW == 0
    mesh = plsc.VectorSubcoreMesh(core_axis_name="core", subcore_axis_name="subcore")
    nchunk = per_w // SC_GATHER_ROWS
    nbuf = SC_GATHER_BUFFERS

    @functools.partial(
        pl.kernel, out_type=jax.ShapeDtypeStruct((n, d), table.dtype), mesh=mesh,
        scratch_types=[pltpu.VMEM((per_w,), jnp.int32),
                       pltpu.VMEM((nbuf, SC_GATHER_ROWS, d), table.dtype),
                       pltpu.SemaphoreType.DMA((nbuf,)), pltpu.SemaphoreType.DMA((nbuf,))],
        name="sc_gather_rows")
    def gather(x_hbm, i_hbm, o_hbm, idx_v, buf, gsem, wsem):
        wid = lax.axis_index("subcore") * nc + lax.axis_index("core")
        base = wid * per_w
        pltpu.sync_copy(i_hbm.at[pl.ds(base, per_w)], idx_v)

        def gather_copy(c):
            rows = idx_v.at[pl.ds(c * SC_GATHER_ROWS, SC_GATHER_ROWS)]
            return pltpu.make_async_copy(x_hbm.at[rows], buf.at[c % nbuf], gsem.at[c % nbuf])

        def write_copy(c):
            dst = o_hbm.at[pl.ds(base + c * SC_GATHER_ROWS, SC_GATHER_ROWS)]
            return pltpu.make_async_copy(buf.at[c % nbuf], dst, wsem.at[c % nbuf])

        for c in range(min(nbuf - 1, nchunk)):
            gather_copy(c).start()
        for c in range(nchunk):
            gather_copy(c).wait()
            write_copy(c).start()
            if c + nbuf - 1 < nchunk:
                if c >= 1:
                    write_copy(c - 1).wait()
                gather_copy(c + nbuf - 1).start()
        for c in range(max(0, nchunk - nbuf), nchunk):
            write_copy(c).wait()

    return gather(table, idx)


def _sc_inverse_rows(dest_flat, n_rows, chunk=2048):
    n = dest_flat.shape[0]
    n_tokens = n // 2
    assert n_rows <= 3 * n_tokens
    nc = plsc.get_sparse_core_info().num_cores
    mesh = plsc.VectorSubcoreMesh(core_axis_name="core", subcore_axis_name="subcore")

    @functools.partial(
        pl.kernel, out_type=jax.ShapeDtypeStruct((n_rows,), jnp.int32), mesh=mesh,
        scratch_types=[pltpu.VMEM((n_rows,), jnp.int32), pltpu.VMEM((chunk,), jnp.int32)],
        compiler_params=pltpu.CompilerParams(needs_layout_passes=False),
        name="sc_inverse_rows")
    def inverse(d_hbm, o_hbm, inv_v, d_v):
        wid = lax.axis_index("subcore") * nc + lax.axis_index("core")

        @pl.when(wid == 0)
        def _():
            lanes = lax.iota(jnp.int32, SC_LANES)

            @plsc.parallel_loop(0, n_rows // SC_LANES, unroll=8)
            def _(i):
                r = lanes + i * SC_LANES
                r = jnp.where(r >= n_tokens, r - n_tokens, r)
                inv_v[pl.ds(i * SC_LANES, SC_LANES)] = jnp.where(r >= n_tokens, r - n_tokens, r)

            @pl.loop(0, n // chunk)
            def _(c):
                pltpu.sync_copy(d_hbm.at[pl.ds(c * chunk, chunk)], d_v)

                @plsc.parallel_loop(0, chunk // SC_LANES, unroll=8)
                def _(j):
                    rows = d_v[pl.ds(j * SC_LANES, SC_LANES)]
                    pair = lax.iota(jnp.int32, SC_LANES) + (c * chunk + j * SC_LANES)
                    plsc.store_scatter(inv_v, [rows], jnp.where(pair >= n_tokens, pair - n_tokens, pair))

            pltpu.sync_copy(inv_v, o_hbm)

    return inverse(dest_flat)


def _ffn_kernel(first_ref, count_ref, nv_ref, wg_hbm, wu_hbm, wd_hbm, xs_hbm, y_hbm,
                wg32, wu32, wd32, xbuf, ybuf, wsem, isem, osem, *, n_blocks):
    nv = nv_ref[0]
    nbuf = FFN_BUFFERS

    def next_expert(e):
        def more(t):
            return jnp.logical_and(t < N_EXPERTS, count_ref[jnp.minimum(t, N_EXPERTS - 1)] == 0)
        return lax.while_loop(more, lambda t: t + 1, e + 1)

    def weight_copies(e, slot):
        ee = jnp.minimum(e, N_EXPERTS - 1)
        return [pltpu.make_async_copy(src.at[ee], dst.at[slot], wsem.at[slot, i])
                for i, (src, dst) in enumerate(((wg_hbm, wg32), (wu_hbm, wu32), (wd_hbm, wd32)))]

    def fetch_weights(e, slot):
        @pl.when(e < N_EXPERTS)
        def _():
            for c in weight_copies(e, slot):
                c.start()

    def take_weights(e, slot):
        for c in weight_copies(e, slot):
            c.wait()
        fetch_weights(next_expert(e), 1 - slot)

    def rows_of(b):
        return pl.ds(pl.multiple_of(b * ROW_BLOCK, ROW_BLOCK), ROW_BLOCK)

    def buf_rows(b, nblk):
        return pl.ds(pl.multiple_of((b % nbuf) * ROW_BLOCK, ROW_BLOCK), nblk * ROW_BLOCK)

    def in_copy(b):
        return pltpu.make_async_copy(xs_hbm.at[rows_of(b)], xbuf.at[buf_rows(b, 1)], isem.at[b % nbuf])

    def out_copy(b):
        return pltpu.make_async_copy(ybuf.at[buf_rows(b, 1)], y_hbm.at[rows_of(b)], osem.at[b % nbuf])

    def expert_mlp(words, slot):
        x = jnp.concatenate(_unpack_bf16_halves(words), axis=1)
        a = _dot(x, wg32[slot])
        u = _dot(x, wu32[slot])
        return _pack_bf16_halves(_dot(a * jax.nn.sigmoid(a) * u, wd32[slot]))

    ahead = nbuf - max(FFN_GROUPS)

    @pl.when(nv > 0)
    def _():
        e0 = next_expert(jnp.int32(-1))
        for i in range(ahead):
            @pl.when(i < nv)
            def _():
                in_copy(i).start()
        fetch_weights(e0, 0)
        take_weights(e0, 0)

        def step(carry):
            b, e, k = carry
            switch = b >= first_ref[e] + count_ref[e]
            e_new = jnp.where(switch, next_expert(e), e)
            k_new = jnp.where(switch, k + 1, k)

            @pl.when(switch)
            def _():
                take_weights(e_new, k_new % 2)

            end = first_ref[e_new] + count_ref[e_new]
            n = jnp.int32(1)
            for size in FFN_GROUPS:
                fits = jnp.logical_and(b + size <= end, b % nbuf + size <= nbuf)
                n = jnp.where(jnp.logical_and(n == 1, fits), size, n)
            for i in range(max(FFN_GROUPS)):
                @pl.when(jnp.logical_and(i < n, b + ahead + i < nv))
                def _():
                    in_copy(b + ahead + i).start()
            for i in range(max(FFN_GROUPS)):
                @pl.when(i < n)
                def _():
                    in_copy(b + i).wait()

                    @pl.when(b + i >= nbuf)
                    def _():
                        out_copy(b + i - nbuf).wait()

            for size in FFN_GROUPS + (1,):
                @pl.when(n == size)
                def _():
                    ybuf[buf_rows(b, size), :] = expert_mlp(xbuf[buf_rows(b, size), :], k_new % 2)

            for i in range(max(FFN_GROUPS)):
                @pl.when(i < n)
                def _():
                    out_copy(b + i).start()
            return b + n, e_new, k_new

        lax.while_loop(lambda c: c[0] < nv, step, (jnp.int32(0), e0, jnp.int32(0)))

        for i in range(nbuf):
            @pl.when(nv > i)
            def _():
                out_copy(nv - 1 - i).wait()

    ybuf[0:ROW_BLOCK, :] = jnp.zeros((ROW_BLOCK, D_MODEL // 2), jnp.int32)

    def fill(b, carry):
        pltpu.sync_copy(ybuf.at[pl.ds(0, ROW_BLOCK)], y_hbm.at[rows_of(b)])
        return carry

    lax.fori_loop(nv, n_blocks, fill, 0)


def _ffn(first_blk, n_blk, nv, xs, w_gate, w_up, w_down):
    n_rows = xs.shape[0]
    n_blocks = n_rows // ROW_BLOCK
    anyspec = pl.BlockSpec(memory_space=pl.ANY)
    return pl.pallas_call(
        functools.partial(_ffn_kernel, n_blocks=n_blocks),
        grid_spec=pltpu.PrefetchScalarGridSpec(
            num_scalar_prefetch=3,
            grid=(1,),
            in_specs=[anyspec, anyspec, anyspec, anyspec],
            out_specs=anyspec,
            scratch_shapes=[pltpu.VMEM((2, D_MODEL, D_FF), F32), pltpu.VMEM((2, D_MODEL, D_FF), F32),
                            pltpu.VMEM((2, D_FF, D_MODEL), F32),
                            pltpu.VMEM((FFN_BUFFERS * ROW_BLOCK, D_MODEL // 2), jnp.int32),
                            pltpu.VMEM((FFN_BUFFERS * ROW_BLOCK, D_MODEL // 2), jnp.int32),
                            pltpu.SemaphoreType.DMA((2, 3)),
                            pltpu.SemaphoreType.DMA((FFN_BUFFERS,)),
                            pltpu.SemaphoreType.DMA((FFN_BUFFERS,))],
        ),
        out_shape=jax.ShapeDtypeStruct((n_rows, D_MODEL // 2), jnp.int32),
        compiler_params=pltpu.CompilerParams(
            dimension_semantics=("arbitrary",), vmem_limit_bytes=VMEM_LIMIT),
        name="expert_ffn",
    )(first_blk, n_blk, nv, w_gate, w_up, w_down, xs)


def _combine_kernel(h_ref, ya_ref, yb_ref, gate_ref, g_ref, b_ref, o_ref):
    gate = gate_ref[...]
    a_hi, a_lo = _unpack_bf16_halves(ya_ref[...])
    b_hi, b_lo = _unpack_bf16_halves(yb_ref[...])
    g0, g1 = gate[:, 0:1], gate[:, 1:2]
    ffn = jnp.concatenate([a_hi * g0 + b_hi * g1, a_lo * g0 + b_lo * g1], axis=1)
    o_ref[...] = _layer_norm(DEEPNORM_ALPHA * h_ref[...] + ffn, g_ref[...], b_ref[...])


def _combine(h, y2, gate, g, b, tm=1024):
    T = h.shape[0]
    nt = T // tm
    return pl.pallas_call(
        _combine_kernel,
        grid=(nt,),
        in_specs=[pl.BlockSpec((tm, D_MODEL), lambda i: (i, 0)),
                  pl.BlockSpec((tm, D_MODEL // 2), lambda i: (i, 0)),
                  pl.BlockSpec((tm, D_MODEL // 2), lambda i: (i + nt, 0)),
                  pl.BlockSpec((tm, 2), lambda i: (i, 0)),
                  pl.BlockSpec((1, D_MODEL), lambda i: (0, 0)),
                  pl.BlockSpec((1, D_MODEL), lambda i: (0, 0))],
        out_specs=pl.BlockSpec((tm, D_MODEL), lambda i: (i, 0)),
        out_shape=jax.ShapeDtypeStruct((T, D_MODEL), F32),
        compiler_params=pltpu.CompilerParams(
            dimension_semantics=("parallel",), vmem_limit_bytes=VMEM_LIMIT),
        name="combine",
    )(h, y2, y2, gate, g, b)


def kernel(x, w_in, gla_gate_w2, gla_gate_b, gla_norm_g, dil_norm_g, w_out, ln1_g, ln1_b,
           router_coarse_w, router_coarse_b, router_fine_w, router_fine_b,
           expert_w_gate, expert_w_up, expert_w_down, ln2_g, ln2_b):
    B, S, D = x.shape
    T = B * S
    depth = w_in.shape[0]
    slopes = jnp.exp2(-8.0 * jnp.arange(1, DIL_HEADS + 1, dtype=F32) / DIL_HEADS)
    n_rows = 2 * T + N_EXPERTS * ROW_BLOCK
    n_blocks_pad = 2 * LANES
    h = x.reshape(T, D)
    for l in range(depth):
        w2 = jnp.pad(gla_gate_w2[l], ((0, LANES - GLA_GATE_RANK), (0, 0)))
        w2h, w2l = _split_bf16(w2)
        q, k, v, r, la, dq, dk, dv = _in_proj(h, w_in, l, w2h, w2l, gla_gate_b[l][None, :])
        o_gla = _gla(q, k, v, r, la, gla_norm_g[l][None, :], B, S)
        g2 = jnp.tile(dil_norm_g[l], 2)[None, :]
        o_dil = _dilated(slopes, dq, dk, dv, g2, B, S)
        rw = jnp.concatenate([router_fine_w[l].reshape(D, N_EXPERTS), router_coarse_w[l]], axis=1)
        rw = jnp.pad(rw, ((0, 0), (0, LANES - N_EXPERTS - N_GROUPS))).T
        rwh, rwl = _split_bf16(rw)
        rb = jnp.concatenate([router_fine_b[l].reshape(N_EXPERTS), router_coarse_b[l]])
        rb = jnp.pad(rb, (0, LANES - N_EXPERTS - N_GROUPS))[:, None]
        h1, h1p, eid_t, gate, cnt = _out_proj(o_gla, o_dil, h, w_out, l,
                                              ln1_g[l][None, :], ln1_b[l][None, :], rwh, rwl, rb)
        dest_t, be, nv = _positions(eid_t, cnt, n_blocks_pad)
        dest_flat = dest_t.reshape(2 * T)
        src_tok = _sc_inverse_rows(dest_flat, n_rows)
        xs = _sc_gather_rows(h1p, src_tok)
        be = be.reshape(n_blocks_pad)
        y = _ffn(be[:N_EXPERTS], be[LANES:LANES + N_EXPERTS], nv.reshape(LANES)[:1], xs,
                 expert_w_gate[l], expert_w_up[l], expert_w_down[l])
        y2 = _sc_gather_rows(y, dest_flat)
        h = _combine(h1, y2, gate, ln2_g[l][None, :], ln2_b[l][None, :])
    return h.reshape(B, S, D)
```

```python
import functools
import math

import jax
import jax.numpy as jnp
from jax import lax
from jax.experimental import pallas as pl
from jax.experimental.pallas import tpu as pltpu
from jax.experimental.pallas import tpu_sc as plsc

D_MODEL = 1024
GLA_HEADS = 4
GLA_DK = 64
GLA_DV = 128
GLA_KEY_WIDTH = GLA_HEADS * GLA_DK
GLA_WIDTH = GLA_HEADS * GLA_DV
GLA_GATE_RANK = 16
GLA_GATE_TEMP = 16.0
DIL_HEADS = 8
DIL_DH = 64
DIL_WIDTH = DIL_HEADS * DIL_DH
DIL_CONFIGS = ((128, 1), (512, 4), (2048, 16))
DIL_BLOCK = 128
DIL_MAX_R = max(r for _, r in DIL_CONFIGS)
DIL_PAD = DIL_BLOCK * DIL_MAX_R
DIL_UNROLL = 16
DIL_SPLIT_CFG = 2
N_GROUPS = 4
EXPERTS_PER_GROUP = 8
N_EXPERTS = N_GROUPS * EXPERTS_PER_GROUP
D_FF = 512
DEEPNORM_ALPHA = 2.0 ** 0.25
EPS = 1e-5
IN_PROJ_GATE_COL = 2 * GLA_KEY_WIDTH + 2 * GLA_WIDTH
LOG2E = math.log2(math.e)

LANES = 128
SUBLANES = 8
assert DIL_SPLIT_CFG == len(DIL_CONFIGS) - 1 and DIL_CONFIGS[DIL_SPLIT_CFG][1] % (2 * SUBLANES) == 0
GLA_CHUNK = 128
GLA_SUB = 64
GLA_UNROLL = 4
SC_LANES = 16
SC_INDEX_WINDOW = 128
SC_GATHER_BUFFERS = 6
SC_GATHER_ROWS = 32
FFN_BUFFERS = 12
FFN_GROUPS = (4, 2)
ROW_BLOCK = 256
VMEM_LIMIT = 56 * 1024 * 1024

F32 = jnp.float32
BF16 = jnp.bfloat16


def _dot(a, b):
    return jnp.dot(a, b, preferred_element_type=F32)


def _dot_nt(a, b):
    return lax.dot_general(a, b, (((1,), (1,)), ((), ())), preferred_element_type=F32)


def _dot_tn(a, b):
    return lax.dot_general(a, b, (((0,), (0,)), ((), ())), preferred_element_type=F32)


def _split_bf16(v):
    hi = v.astype(BF16)
    lo = (v - hi.astype(F32)).astype(BF16)
    return hi, lo


def _pack_bf16_halves(v):
    w = v.shape[1] // 2
    hi = lax.bitcast_convert_type(v[:, :w].astype(BF16).astype(F32), jnp.int32)
    lo = lax.bitcast_convert_type(v[:, w:].astype(BF16).astype(F32), jnp.int32)
    return hi | lax.shift_right_logical(lo, 16)


def _unpack_bf16_halves(words):
    hi = lax.bitcast_convert_type(words & jnp.int32(-65536), F32)
    lo = lax.bitcast_convert_type(lax.shift_left(words, 16), F32)
    return hi, lo


def _layer_norm(v, g, b):
    mu = jnp.mean(v, axis=-1, keepdims=True)
    c = v - mu
    var = jnp.mean(c * c, axis=-1, keepdims=True)
    return c * lax.rsqrt(var + EPS) * g + b


def _in_proj_kernel(x_ref, w_ref, w2h_ref, w2l_ref, gb_ref,
                    q_ref, k_ref, v_ref, r_ref, la_ref, dq_ref, dk_ref, dv_ref, wg_s, wd_s, wa_s):
    a0 = IN_PROJ_GATE_COL

    @pl.when(pl.program_id(0) == 0)
    def _():
        wg_s[...] = w_ref[0:a0, :].T.astype(BF16)
        gate_tile = w_ref[a0:a0 + LANES, :].T
        lane = lax.broadcasted_iota(jnp.int32, gate_tile.shape, 1)
        wa_s[...] = jnp.where(lane < GLA_GATE_RANK, gate_tile, 0.0).astype(BF16)
        wd_s[...] = w_ref[a0 + GLA_GATE_RANK:, :].T.astype(BF16)

    xb = x_ref[...].astype(BF16)

    def piece(w_s, c0, c1):
        return _dot(xb, w_s[:, c0:c1])

    q_ref[...] = (piece(wg_s, 0, 256) * (GLA_DK ** -0.5)).astype(BF16)
    k_ref[...] = piece(wg_s, 256, 512).astype(BF16)
    v_ref[...] = piece(wg_s, 512, 1024).astype(BF16)
    r_ref[...] = piece(wg_s, 1024, 1536).astype(BF16)
    dq_ref[...] = (piece(wd_s, 0, 512) * (DIL_DH ** -0.5 * LOG2E)).astype(BF16)
    dk_ref[...] = piece(wd_s, 512, 1024).astype(BF16)
    dv_ref[...] = piece(wd_s, 1024, 1536).astype(BF16)
    ga = _dot(xb, wa_s[...])
    ga_hi, ga_lo = _split_bf16(ga)
    z = _dot(ga_hi, w2h_ref[...]) + _dot(ga_lo, w2h_ref[...]) + _dot(ga_hi, w2l_ref[...]) + gb_ref[...]
    log_sig = jnp.minimum(z, 0.0) - jnp.log1p(jnp.exp(-jnp.abs(z)))
    la_ref[...] = log_sig * (1.0 / GLA_GATE_TEMP)


def _in_proj(x2, w_in_t, layer, w2h, w2l, gb, tm=1024):
    T = x2.shape[0]
    row = lambda wd: pl.BlockSpec((tm, wd), lambda i: (i, 0))
    full = lambda a: pl.BlockSpec(a.shape, lambda i: (0,) * a.ndim)
    outs = [(GLA_KEY_WIDTH, BF16), (GLA_KEY_WIDTH, BF16), (GLA_WIDTH, BF16), (GLA_WIDTH, BF16), (GLA_KEY_WIDTH, F32),
            (DIL_WIDTH, BF16), (DIL_WIDTH, BF16), (DIL_WIDTH, BF16)]
    group = w_in_t.shape[1] - IN_PROJ_GATE_COL - GLA_GATE_RANK
    return pl.pallas_call(
        _in_proj_kernel,
        grid=(T // tm,),
        in_specs=[row(D_MODEL),
                  pl.BlockSpec((None,) + w_in_t.shape[1:], lambda i: (layer, 0, 0), pipeline_mode=pl.Buffered(1)),
                  full(w2h), full(w2l), full(gb)],
        out_specs=[row(wd) for wd, _ in outs],
        out_shape=[jax.ShapeDtypeStruct((T, wd), dt) for wd, dt in outs],
        scratch_shapes=[pltpu.VMEM((D_MODEL, IN_PROJ_GATE_COL), BF16), pltpu.VMEM((D_MODEL, group), BF16),
                        pltpu.VMEM((D_MODEL, LANES), BF16)],
        compiler_params=pltpu.CompilerParams(
            dimension_semantics=("arbitrary",), vmem_limit_bytes=VMEM_LIMIT),
        name="in_proj",
    )(x2, w_in_t, w2h, w2l, gb)


def _gla_kernel(q_ref, k_ref, v_ref, r_ref, la_ref, g_ref, o_ref, s_ref, *, seq_block):
    C = GLA_CHUNK
    H = GLA_SUB
    assert C == 2 * H

    @pl.when(pl.program_id(1) == 0)
    def _():
        s_ref[...] = jnp.zeros_like(s_ref)

    ri = lax.broadcasted_iota(jnp.int32, (C, C), 0)
    ci = lax.broadcasted_iota(jnp.int32, (C, C), 1)
    same_sub = (ri // H) == (ci // H)
    sum_ops = jnp.concatenate([jnp.logical_and(same_sub, ci <= ri).astype(BF16), same_sub.astype(BF16),
                               jnp.ones((C, C), BF16)], axis=0)
    diag_mask = jnp.logical_and(same_sub, ci <= ri)
    off_mask = (ri // H) > (ci // H)
    second = lax.broadcasted_iota(jnp.int32, (C, 1), 0) >= H
    ones_cl = jnp.ones((C, LANES), BF16)
    lane_k = lax.broadcasted_iota(jnp.int32, (1, GLA_KEY_WIDTH), 1) // GLA_DK
    head_masks = [(lane_k == h).astype(F32) for h in range(GLA_HEADS)]
    srow = lax.broadcasted_iota(jnp.int32, (GLA_KEY_WIDTH, GLA_WIDTH), 0) // GLA_DK
    scol = lax.broadcasted_iota(jnp.int32, (GLA_KEY_WIDTH, GLA_WIDTH), 1) // GLA_DV
    state_mask = (srow == scol).astype(F32)
    g = g_ref[...]

    def trip(t, carry):
        U = GLA_UNROLL
        rows = [pl.ds(pl.multiple_of((t * U + u) * C, C), C) for u in range(U)]
        la2s, sums = [], []
        for u in range(U):
            la_hi, la_lo = _split_bf16(la_ref[rows[u], :])
            la2 = jnp.concatenate([la_hi, la_lo], axis=1)
            la2s.append(la2)
            sm = _dot(sum_ops, la2)
            sums.append(sm[:, 0:GLA_KEY_WIDTH] + sm[:, GLA_KEY_WIDTH:])
        q_states, k_states, scores = [], [], []
        for u in range(U):
            b = sums[u][0:C]
            t_sub = sums[u][C:2 * C]
            other = sums[u][2 * C:3 * C] - t_sub
            q = q_ref[rows[u], :].astype(F32)
            k = k_ref[rows[u], :].astype(F32)
            qd = q * jnp.exp(b)
            kd = (k * jnp.exp(-b)).astype(BF16)
            ke_f = k * jnp.exp(t_sub - b)
            cross = jnp.exp(other)
            q_states.append((qd * jnp.where(second, cross, 1.0)).astype(BF16))
            k_states.append((ke_f * jnp.where(second, 1.0, cross)).astype(BF16))
            q_heads = jnp.concatenate([(qd * head_masks[h]).astype(BF16) for h in range(GLA_HEADS)], axis=0)
            keys2 = jnp.concatenate([kd, ke_f.astype(BF16)], axis=0)
            scores.append(_dot_nt(q_heads, keys2))
        decs, upds = [], []
        for u in range(U):
            tot = _dot_tn(la2s[u], ones_cl)
            decs.append(jnp.exp(tot[0:GLA_KEY_WIDTH] + tot[GLA_KEY_WIDTH:]))
            upds.append(_dot_tn(k_states[u], v_ref[rows[u], :]) * state_mask)
        o_inters = []
        for u in range(U):
            state = s_ref[...]
            o_inters.append(_dot(q_states[u], state.astype(BF16)))
            for h in range(GLA_HEADS):
                cols = slice(h * GLA_DV, (h + 1) * GLA_DV)
                s_ref[:, cols] = state[:, cols] * decs[u] + upds[u][:, cols]
        for u in range(U):
            v = v_ref[rows[u], :]
            outs = []
            for h in range(GLA_HEADS):
                sh = scores[u][h * C:(h + 1) * C]
                a = jnp.where(diag_mask, sh[:, 0:C], 0.0) + jnp.where(off_mask, sh[:, C:2 * C], 0.0)
                cols = slice(h * GLA_DV, (h + 1) * GLA_DV)
                o = _dot(a.astype(BF16), v[:, cols]) + o_inters[u][:, cols]
                o = o * lax.rsqrt(jnp.mean(o * o, axis=-1, keepdims=True) + EPS) * g
                outs.append(o)
            o_all = jnp.concatenate(outs, axis=-1)
            rr = r_ref[rows[u], :].astype(F32)
            o_ref[rows[u], :] = (o_all * (rr * jax.nn.sigmoid(rr))).astype(BF16)
        return carry

    lax.fori_loop(0, seq_block // C // GLA_UNROLL, trip, 0)


def _gla(q, k, v, r, la, g, batch, seq, seq_block=1024):
    nsb = seq // seq_block
    row = lambda w: pl.BlockSpec((seq_block, w), lambda b, s: (b * nsb + s, 0))
    return pl.pallas_call(
        functools.partial(_gla_kernel, seq_block=seq_block),
        grid=(batch, nsb),
        in_specs=[row(GLA_KEY_WIDTH), row(GLA_KEY_WIDTH), row(GLA_WIDTH), row(GLA_WIDTH), row(GLA_KEY_WIDTH),
                  pl.BlockSpec((1, GLA_DV), lambda b, s: (0, 0))],
        out_specs=row(GLA_WIDTH),
        out_shape=jax.ShapeDtypeStruct((batch * seq, GLA_WIDTH), BF16),
        scratch_shapes=[pltpu.VMEM((GLA_KEY_WIDTH, GLA_WIDTH), F32)],
        compiler_params=pltpu.CompilerParams(
            dimension_semantics=("parallel", "arbitrary"), vmem_limit_bytes=VMEM_LIMIT),
        name="gla",
    )(q, k, v, r, la, g)


def _dil_kernel(slope_ref, q_ref, k_ref, v_ref, g_ref, o_ref,
                qf, kf, vf, qs, ks, vs, kd, va, vb, oc, lc, ocs, lcs, *, seq):
    B = DIL_BLOCK
    U = DIL_UNROLL
    pair = pl.program_id(1)
    for src, nat, split in ((q_ref, qf, qs), (k_ref, kf, ks), (v_ref, vf, vs)):
        x = src[...].astype(F32)
        nat[...] = x
        x3 = x.reshape(seq // (2 * SUBLANES), 2 * SUBLANES, LANES)
        split[0] = x3[:, :SUBLANES, :].reshape(seq // 2, LANES)
        split[1] = x3[:, SUBLANES:, :].reshape(seq // 2, LANES)

    lane = lax.broadcasted_iota(jnp.int32, (1, LANES), 1)
    first = lane < DIL_DH
    ii = lax.broadcasted_iota(jnp.int32, (B, B), 0)
    jj = lax.broadcasted_iota(jnp.int32, (B, B), 1)
    upper = jj > ii
    eye = jj == ii
    dist = jnp.bitwise_and(ii - jj, B - 1).astype(F32)
    neg = jnp.float32(-jnp.inf)
    neg_tile = jnp.full((B, B), neg, F32)
    zero_tile = jnp.zeros((B, LANES), BF16)

    def split_rows(r, c, n):
        half = r // 2
        start = c % SUBLANES + c // (2 * SUBLANES) * SUBLANES + n * (B * half)
        return (c // SUBLANES) % 2, pl.ds(start, B, stride=half)

    def class_rows(nat, split, cfg, r, c, n):
        if r == 1:
            return nat[pl.ds(pl.multiple_of(n * B, B), B), :]
        if cfg == DIL_SPLIT_CFG:
            parity, rows = split_rows(r, c, n)
            return split[parity, rows, :]
        return nat[pl.ds(c + n * (B * r), B, stride=r), :]

    for cfg, (window, r) in enumerate(DIL_CONFIGS):
        nb = seq // r // B
        cs = nb + 1
        bias_prev, bias_cur = [], []
        for hh in range(2):
            slope = slope_ref[2 * pair + hh] * (float(r) * LOG2E)
            bias = dist * (-slope)
            bias_prev.append(jnp.where(upper, bias, jnp.where(eye, -slope * float(B), neg)))
            bias_cur.append(jnp.where(upper, neg, bias))

        for c in range(r):
            rows0 = slice(c * cs * B, (c * cs + 1) * B)
            kd[rows0, :] = zero_tile
            va[rows0, :] = zero_tile
            vb[rows0, :] = zero_tile

        def prep(t4, carry, cfg=cfg, r=r, nb=nb, cs=cs):
            for j in range(4):
                t = t4 * 4 + j
                c = t // nb
                n = t % nb
                dst = pl.ds(pl.multiple_of((c * cs + 1 + n) * B, B), B)
                kd[dst, :] = class_rows(kf, ks, cfg, r, c, n).astype(BF16)
                v = class_rows(vf, vs, cfg, r, c, n)
                va[dst, :] = jnp.where(first, v, 1.0).astype(BF16)
                vb[dst, :] = jnp.where(first, 1.0, v).astype(BF16)
            return carry

        lax.fori_loop(0, seq // B // 4, prep, 0)

        def geom(it, r=r, nb=nb, cs=cs):
            c = it // nb
            n = it % nb
            start = c + n * (B * r)
            rows = pl.ds(start, B, stride=r) if r > 1 else pl.ds(pl.multiple_of(start, B), B)
            kv = pl.ds(pl.multiple_of((c * cs + n) * B, B), 2 * B)
            return rows, kv, c, n

        def body(t, carry, cfg=cfg, r=r, nb=nb, bias_prev=bias_prev, bias_cur=bias_cur, geom=geom):
            geo = [geom(t * U + u) for u in range(U)]
            scores = []
            for u in range(U):
                _, kv, c, n = geo[u]
                q = class_rows(qf, qs, cfg, r, c, n)
                q_heads = jnp.concatenate([jnp.where(first, q, 0.0), jnp.where(first, 0.0, q)], axis=0)
                scores.append(_dot_nt(q_heads.astype(BF16), kd[kv, :]))
            probs, maxes = [], []
            for u in range(U):
                if nb % U == 0:
                    has_prev = True if u > 0 else (t * U) % nb > 0
                else:
                    assert U % nb == 0
                    has_prev = (u % nb) > 0
                for hh in range(2):
                    s2 = scores[u][hh * B:(hh + 1) * B]
                    if has_prev is True:
                        bp = bias_prev[hh]
                    elif has_prev is False:
                        bp = neg_tile
                    else:
                        bp = jnp.where(has_prev, bias_prev[hh], neg)
                    s_prev = s2[:, 0:B] + bp
                    s_cur = s2[:, B:2 * B] + bias_cur[hh]
                    m = jnp.max(jnp.maximum(s_prev, s_cur), axis=-1, keepdims=True)
                    probs.append(jnp.concatenate([jnp.exp2(s_prev - m), jnp.exp2(s_cur - m)], axis=1).astype(BF16))
                    maxes.append(m)
            for u in range(U):
                rows, kv, c, n = geo[u]
                acc0 = _dot(probs[2 * u], va[kv, :])
                acc1 = _dot(probs[2 * u + 1], vb[kv, :])
                num = jnp.where(first, acc0, acc1)
                den = pltpu.roll(jnp.where(first, acc1, acc0), DIL_DH, axis=1)
                out = num * (1.0 / den)
                lse = jnp.where(first, maxes[2 * u], maxes[2 * u + 1]) + jnp.log2(den)
                if cfg == DIL_SPLIT_CFG:
                    parity, dst = split_rows(r, c, n)
                    ocs[parity, dst, :] = out
                    lcs[parity, dst, :] = lse
                else:
                    oc[cfg, rows, :] = out
                    lc[cfg, rows, :] = lse
            return carry

        lax.fori_loop(0, seq // B // U, body, 0)

    g = g_ref[...]
    CH = 512

    def mix(i, carry):
        rows = pl.ds(pl.multiple_of(i * CH, CH), CH)
        half_rows = pl.ds(pl.multiple_of(i * (CH // 2), CH // 2), CH // 2)

        def tiles_interleaved(split):
            even = split[0, half_rows, :].reshape(CH // (2 * SUBLANES), 1, SUBLANES, LANES)
            odd = split[1, half_rows, :].reshape(CH // (2 * SUBLANES), 1, SUBLANES, LANES)
            return jnp.concatenate([even, odd], axis=1).reshape(CH, LANES)

        l0, l1, l2 = lc[0, rows, :], lc[1, rows, :], tiles_interleaved(lcs)
        m = jnp.maximum(jnp.maximum(l0, l1), l2)
        e0, e1, e2 = jnp.exp2(l0 - m), jnp.exp2(l1 - m), jnp.exp2(l2 - m)
        inv = 1.0 / (e0 + e1 + e2)
        o = (e0 * oc[0, rows, :] + e1 * oc[1, rows, :] + e2 * tiles_interleaved(ocs)) * inv
        sq = o * o
        ms_a = jnp.sum(jnp.where(first, sq, 0.0), axis=-1, keepdims=True) * (1.0 / DIL_DH)
        ms_b = jnp.sum(jnp.where(first, 0.0, sq), axis=-1, keepdims=True) * (1.0 / DIL_DH)
        ms = jnp.where(first, ms_a, ms_b)
        o_ref[rows, :] = (o * lax.rsqrt(ms + EPS) * g).astype(BF16)
        return carry

    lax.fori_loop(0, seq // CH, mix, 0)


def _dilated(slopes, dq, dk, dv, g2, batch, seq):
    blk = pl.BlockSpec((seq, LANES), lambda b, p, s: (b, p))
    return pl.pallas_call(
        functools.partial(_dil_kernel, seq=seq),
        grid_spec=pltpu.PrefetchScalarGridSpec(
            num_scalar_prefetch=1,
            grid=(batch, DIL_WIDTH // LANES),
            in_specs=[blk, blk, blk, pl.BlockSpec((1, LANES), lambda b, p, s: (0, 0))],
            out_specs=blk,
            scratch_shapes=[pltpu.VMEM((seq, LANES), F32)] * 3 + [pltpu.VMEM((2, seq // 2, LANES), F32)] * 3
                           + [pltpu.VMEM((seq + DIL_PAD, LANES), BF16)] * 3
                           + [pltpu.VMEM((2, seq, LANES), F32)] * 2 + [pltpu.VMEM((2, seq // 2, LANES), F32)] * 2,
        ),
        out_shape=jax.ShapeDtypeStruct((batch * seq, DIL_WIDTH), BF16),
        compiler_params=pltpu.CompilerParams(
            dimension_semantics=("parallel", "parallel"), vmem_limit_bytes=VMEM_LIMIT),
        name="dilated",
    )(slopes, dq, dk, dv, g2)


def _out_proj_kernel(og_ref, od_ref, x_ref, wg_ref, wd_ref, g_ref, b_ref,
                     rwh_ref, rwl_ref, rb_ref, h_ref, hp_ref, eid_ref, gate_ref, cnt_ref, wg_s, wd_s):
    @pl.when(pl.program_id(0) == 0)
    def _():
        wg_s[...] = wg_ref[...].astype(BF16)
        wd_s[...] = wd_ref[...].astype(BF16)

    mix = _dot(og_ref[...], wg_s[...]) + _dot(od_ref[...], wd_s[...])
    h = _layer_norm(DEEPNORM_ALPHA * x_ref[...] + mix, g_ref[...], b_ref[...])
    h_ref[...] = h
    hp_ref[...] = _pack_bf16_halves(h)
    h_hi, h_lo = _split_bf16(h)
    lt = (_dot_nt(rwh_ref[...], h_hi) + _dot_nt(rwh_ref[...], h_lo) + _dot_nt(rwl_ref[...], h_hi)
          + rb_ref[...])
    tm = lt.shape[1]
    row = lax.broadcasted_iota(jnp.int32, (EXPERTS_PER_GROUP, tm), 0).astype(F32)
    neg = jnp.float32(-jnp.inf)
    big = jnp.float32(1e9)
    coarse = jnp.where(row < N_GROUPS, lt[N_EXPERTS:N_EXPERTS + EXPERTS_PER_GROUP, :], neg)
    cmax = jnp.max(coarse, axis=0, keepdims=True)
    g_idx = jnp.min(jnp.where(coarse == cmax, row, big), axis=0, keepdims=True)
    p_group = 1.0 / jnp.sum(jnp.exp(coarse - cmax), axis=0, keepdims=True)
    fine = lt[(N_GROUPS - 1) * EXPERTS_PER_GROUP:N_EXPERTS, :]
    for g in range(N_GROUPS - 2, -1, -1):
        fine = jnp.where(g_idx == g, lt[g * EXPERTS_PER_GROUP:(g + 1) * EXPERTS_PER_GROUP, :], fine)
    v1 = jnp.max(fine, axis=0, keepdims=True)
    i1 = jnp.min(jnp.where(fine == v1, row, big), axis=0, keepdims=True)
    fine2 = jnp.where(row == i1, neg, fine)
    v2 = jnp.max(fine2, axis=0, keepdims=True)
    i2 = jnp.min(jnp.where(fine2 == v2, row, big), axis=0, keepdims=True)
    e2 = jnp.exp(v2 - v1)
    den = 1.0 + e2
    gate1 = p_group * (1.0 / den)
    gate2 = p_group * (e2 / den)
    id1 = g_idx * EXPERTS_PER_GROUP + i1
    id2 = g_idx * EXPERTS_PER_GROUP + i2
    eid_ref[...] = jnp.concatenate([id1, id2], axis=0).astype(jnp.int32)
    slab = jnp.concatenate([gate1, gate2, jnp.zeros((LANES - 2, tm), F32)], axis=0)
    gate_ref[...] = slab.T[:, 0:2]
    sub = lax.broadcasted_iota(jnp.int32, (LANES, tm), 0).astype(F32)
    onehot = jnp.logical_or(sub == id1, sub == id2).astype(BF16)

    @pl.when(pl.program_id(0) == 0)
    def _():
        cnt_ref[...] = jnp.zeros_like(cnt_ref)

    cnt_ref[...] += _dot(onehot, jnp.ones((tm, LANES), BF16))


def _out_proj(og, od, x2, w_out, layer, g, b, rwh, rwl, rb, tm=1024):
    T = x2.shape[0]
    row = lambda w: pl.BlockSpec((tm, w), lambda i: (i, 0))
    full = lambda a: pl.BlockSpec(a.shape, lambda i: (0,) * a.ndim)
    half = lambda p: pl.BlockSpec((None, GLA_WIDTH, D_MODEL), lambda i: (layer, p, 0))
    return pl.pallas_call(
        _out_proj_kernel,
        grid=(T // tm,),
        in_specs=[row(GLA_WIDTH), row(DIL_WIDTH), row(D_MODEL), half(0), half(1), full(g), full(b),
                  full(rwh), full(rwl), full(rb)],
        out_specs=[row(D_MODEL), row(D_MODEL // 2), pl.BlockSpec((2, tm), lambda i: (0, i)), row(2),
                   pl.BlockSpec((LANES, LANES), lambda i: (0, 0))],
        out_shape=[jax.ShapeDtypeStruct((T, D_MODEL), F32),
                   jax.ShapeDtypeStruct((T, D_MODEL // 2), jnp.int32),
                   jax.ShapeDtypeStruct((2, T), jnp.int32),
                   jax.ShapeDtypeStruct((T, 2), F32),
                   jax.ShapeDtypeStruct((LANES, LANES), F32)],
        scratch_shapes=[pltpu.VMEM((GLA_WIDTH, D_MODEL), BF16), pltpu.VMEM((DIL_WIDTH, D_MODEL), BF16)],
        compiler_params=pltpu.CompilerParams(
            dimension_semantics=("arbitrary",), vmem_limit_bytes=VMEM_LIMIT),
        name="out_proj_router",
    )(og, od, x2, w_out, w_out, g, b, rwh, rwl, rb)


def _positions_kernel(eid_ref, cnt_ref, dest_ref, be_ref, nv_ref, carry_ref, sp_ref, tri_ref, *, tb):
    i = pl.program_id(0)

    @pl.when(i == 0)
    def _():
        shift = int(math.log2(ROW_BLOCK))
        nb_col = (cnt_ref[...].astype(jnp.int32) + (ROW_BLOCK - 1)) >> shift
        r = lax.broadcasted_iota(jnp.int32, (LANES, LANES), 0)
        c = lax.broadcasted_iota(jnp.int32, (LANES, LANES), 1)
        nb_f = jnp.where(r < N_EXPERTS, nb_col, 0).astype(F32)
        start_col = _dot((c < r).astype(BF16), nb_f.astype(BF16))
        sp_ref[...] = start_col * float(ROW_BLOCK)
        carry_ref[...] = jnp.zeros_like(carry_ref)
        be_ref[...] = jnp.concatenate([start_col.T[0:1, :], nb_f.T[0:1, :]], axis=1).astype(jnp.int32)
        total = jnp.sum(nb_f[:, 0:1], axis=0, keepdims=True)
        nv_ref[...] = jnp.broadcast_to(total, (1, LANES)).astype(jnp.int32)
        tr = lax.broadcasted_iota(jnp.int32, (tb, tb), 0)
        tc = lax.broadcasted_iota(jnp.int32, (tb, tb), 1)
        tri_ref[...] = (tr < tc).astype(BF16)

    sub = lax.broadcasted_iota(jnp.int32, (LANES, tb), 0)
    oh1 = sub == eid_ref[0:1, :]
    oh2 = sub == eid_ref[1:2, :]
    oh = jnp.logical_or(oh1, oh2).astype(BF16)
    offset = jnp.tile(carry_ref[...] + sp_ref[...], (1, tb // LANES))
    before = _dot(oh, tri_ref[...]) + offset
    d1 = jnp.sum(jnp.where(oh1, before, 0.0), axis=0, keepdims=True)
    d2 = jnp.sum(jnp.where(oh2, before, 0.0), axis=0, keepdims=True)
    dest_ref[...] = jnp.concatenate([d1, d2], axis=0).astype(jnp.int32)
    carry_ref[...] += _dot(oh, jnp.ones((tb, LANES), BF16))


def _positions(eid_t, cnt, n_blocks_pad, tb=1024):
    T = eid_t.shape[1]
    return pl.pallas_call(
        functools.partial(_positions_kernel, tb=tb),
        grid=(T // tb,),
        in_specs=[pl.BlockSpec((2, tb), lambda i: (0, i)), pl.BlockSpec((LANES, LANES), lambda i: (0, 0))],
        out_specs=[pl.BlockSpec((2, tb), lambda i: (0, i)),
                   pl.BlockSpec((1, n_blocks_pad), lambda i: (0, 0)),
                   pl.BlockSpec((1, LANES), lambda i: (0, 0))],
        out_shape=[jax.ShapeDtypeStruct((2, T), jnp.int32),
                   jax.ShapeDtypeStruct((1, n_blocks_pad), jnp.int32),
                   jax.ShapeDtypeStruct((1, LANES), jnp.int32)],
        scratch_shapes=[pltpu.VMEM((LANES, LANES), F32), pltpu.VMEM((LANES, LANES), F32),
                        pltpu.VMEM((tb, tb), BF16)],
        compiler_params=pltpu.CompilerParams(dimension_semantics=("arbitrary",)),
        name="positions",
    )(eid_t, cnt)


def _sc_gather_rows(table, idx):
    n = idx.shape[0]
    d = table.shape[1]
    info = plsc.get_sparse_core_info()
    nc, ns = info.num_cores, info.num_subcores
    per_w = n // (nc * ns)
    assert per_w * nc * ns == n and per_w % SC_INDEX_WINDOW == 0
    mesh = plsc.VectorSubcoreMesh(core_axis_name="core", subcore_axis_name="subcore")
    nchunk = per_w // SC_GATHER_ROWS
    nbuf = SC_GATHER_BUFFERS

    @functools.partial(
        pl.kernel, out_type=jax.ShapeDtypeStruct((n, d), table.dtype), mesh=mesh,
        scratch_types=[pltpu.VMEM((per_w,), jnp.int32),
                       pltpu.VMEM((nbuf, SC_GATHER_ROWS, d), table.dtype),
                       pltpu.SemaphoreType.DMA((nbuf,)), pltpu.SemaphoreType.DMA((nbuf,))],
        name="sc_gather_rows")
    def gather(x_hbm, i_hbm, o_hbm, idx_v, buf, gsem, wsem):
        wid = lax.axis_index("subcore") * nc + lax.axis_index("core")
        base = wid * per_w
        pltpu.sync_copy(i_hbm.at[pl.ds(base, per_w)], idx_v)

        def gather_copy(c):
            rows = idx_v.at[pl.ds(c * SC_GATHER_ROWS, SC_GATHER_ROWS)]
            return pltpu.make_async_copy(x_hbm.at[rows], buf.at[c % nbuf], gsem.at[c % nbuf])

        def write_copy(c):
            dst = o_hbm.at[pl.ds(base + c * SC_GATHER_ROWS, SC_GATHER_ROWS)]
            return pltpu.make_async_copy(buf.at[c % nbuf], dst, wsem.at[c % nbuf])

        for c in range(min(nbuf - 1, nchunk)):
            gather_copy(c).start()
        for c in range(nchunk):
            gather_copy(c).wait()
            write_copy(c).start()
            if c + nbuf - 1 < nchunk:
                if c >= 1:
                    write_copy(c - 1).wait()
                gather_copy(c + nbuf - 1).start()
        for c in range(max(0, nchunk - nbuf), nchunk):
            write_copy(c).wait()

    return gather(table, idx)


def _sc_inverse_rows(dest_flat, n_rows, chunk=2048):
    n = dest_flat.shape[0]
    n_tokens = n // 2
    assert n_rows <= 3 * n_tokens
    nc = plsc.get_sparse_core_info().num_cores
    mesh = plsc.VectorSubcoreMesh(core_axis_name="core", subcore_axis_name="subcore")

    @functools.partial(
        pl.kernel, out_type=jax.ShapeDtypeStruct((n_rows,), jnp.int32), mesh=mesh,
        scratch_types=[pltpu.VMEM((n_rows,), jnp.int32), pltpu.VMEM((chunk,), jnp.int32)],
        compiler_params=pltpu.CompilerParams(needs_layout_passes=False),
        name="sc_inverse_rows")
    def inverse(d_hbm, o_hbm, inv_v, d_v):
        wid = lax.axis_index("subcore") * nc + lax.axis_index("core")

        @pl.when(wid == 0)
        def _():
            lanes = lax.iota(jnp.int32, SC_LANES)

            @plsc.parallel_loop(0, n_rows // SC_LANES, unroll=8)
            def _(i):
                r = lanes + i * SC_LANES
                r = jnp.where(r >= n_tokens, r - n_tokens, r)
                inv_v[pl.ds(i * SC_LANES, SC_LANES)] = jnp.where(r >= n_tokens, r - n_tokens, r)

            @pl.loop(0, n // chunk)
            def _(c):
                pltpu.sync_copy(d_hbm.at[pl.ds(c * chunk, chunk)], d_v)

                @plsc.parallel_loop(0, chunk // SC_LANES, unroll=8)
                def _(j):
                    rows = d_v[pl.ds(j * SC_LANES, SC_LANES)]
                    pair = lax.iota(jnp.int32, SC_LANES) + (c * chunk + j * SC_LANES)
                    plsc.store_scatter(inv_v, [rows], jnp.where(pair >= n_tokens, pair - n_tokens, pair))

            pltpu.sync_copy(inv_v, o_hbm)

    return inverse(dest_flat)


def _ffn_kernel(first_ref, count_ref, nv_ref, wg_hbm, wu_hbm, wd_hbm, xs_hbm, y_hbm,
                wg32, wu32, wd32, xbuf, ybuf, wsem, isem, osem, *, n_blocks):
    nv = nv_ref[0]
    nbuf = FFN_BUFFERS

    def next_expert(e):
        def more(t):
            return jnp.logical_and(t < N_EXPERTS, count_ref[jnp.minimum(t, N_EXPERTS - 1)] == 0)
        return lax.while_loop(more, lambda t: t + 1, e + 1)

    def weight_copies(e, slot):
        ee = jnp.minimum(e, N_EXPERTS - 1)
        return [pltpu.make_async_copy(src.at[ee], dst.at[slot], wsem.at[slot, i])
                for i, (src, dst) in enumerate(((wg_hbm, wg32), (wu_hbm, wu32), (wd_hbm, wd32)))]

    def fetch_weights(e, slot):
        @pl.when(e < N_EXPERTS)
        def _():
            for c in weight_copies(e, slot):
                c.start()

    def take_weights(e, slot):
        for c in weight_copies(e, slot):
            c.wait()
        fetch_weights(next_expert(e), 1 - slot)

    def rows_of(b):
        return pl.ds(pl.multiple_of(b * ROW_BLOCK, ROW_BLOCK), ROW_BLOCK)

    def buf_rows(b, nblk):
        return pl.ds(pl.multiple_of((b % nbuf) * ROW_BLOCK, ROW_BLOCK), nblk * ROW_BLOCK)

    def in_copy(b):
        return pltpu.make_async_copy(xs_hbm.at[rows_of(b)], xbuf.at[buf_rows(b, 1)], isem.at[b % nbuf])

    def out_copy(b):
        return pltpu.make_async_copy(ybuf.at[buf_rows(b, 1)], y_hbm.at[rows_of(b)], osem.at[b % nbuf])

    def expert_mlp(words, slot):
        x = jnp.concatenate(_unpack_bf16_halves(words), axis=1)
        a = _dot(x, wg32[slot])
        u = _dot(x, wu32[slot])
        return _pack_bf16_halves(_dot(a * jax.nn.sigmoid(a) * u, wd32[slot]))

    ahead = nbuf - max(FFN_GROUPS)

    @pl.when(nv > 0)
    def _():
        e0 = next_expert(jnp.int32(-1))
        for i in range(ahead):
            @pl.when(i < nv)
            def _():
                in_copy(i).start()
        fetch_weights(e0, 0)
        take_weights(e0, 0)

        def step(carry):
            b, e, k = carry
            switch = b >= first_ref[e] + count_ref[e]
            e_new = jnp.where(switch, next_expert(e), e)
            k_new = jnp.where(switch, k + 1, k)

            @pl.when(switch)
            def _():
                take_weights(e_new, k_new % 2)

            end = first_ref[e_new] + count_ref[e_new]
            n = jnp.int32(1)
            for size in FFN_GROUPS:
                fits = jnp.logical_and(b + size <= end, b % nbuf + size <= nbuf)
                n = jnp.where(jnp.logical_and(n == 1, fits), size, n)
            for i in range(max(FFN_GROUPS)):
                @pl.when(jnp.logical_and(i < n, b + ahead + i < nv))
                def _():
                    in_copy(b + ahead + i).start()
            for i in range(max(FFN_GROUPS)):
                @pl.when(i < n)
                def _():
                    in_copy(b + i).wait()

                    @pl.when(b + i >= nbuf)
                    def _():
                        out_copy(b + i - nbuf).wait()

            for size in FFN_GROUPS + (1,):
                @pl.when(n == size)
                def _():
                    ybuf[buf_rows(b, size), :] = expert_mlp(xbuf[buf_rows(b, size), :], k_new % 2)

            for i in range(max(FFN_GROUPS)):
                @pl.when(i < n)
                def _():
                    out_copy(b + i).start()
            return b + n, e_new, k_new

        lax.while_loop(lambda c: c[0] < nv, step, (jnp.int32(0), e0, jnp.int32(0)))

        for i in range(nbuf):
            @pl.when(nv > i)
            def _():
                out_copy(nv - 1 - i).wait()

    ybuf[0:ROW_BLOCK, :] = jnp.zeros((ROW_BLOCK, D_MODEL // 2), jnp.int32)

    def fill(b, carry):
        pltpu.sync_copy(ybuf.at[pl.ds(0, ROW_BLOCK)], y_hbm.at[rows_of(b)])
        return carry

    lax.fori_loop(nv, n_blocks, fill, 0)


def _ffn(first_blk, n_blk, nv, xs, w_gate, w_up, w_down):
    n_rows = xs.shape[0]
    n_blocks = n_rows // ROW_BLOCK
    anyspec = pl.BlockSpec(memory_space=pl.ANY)
    return pl.pallas_call(
        functools.partial(_ffn_kernel, n_blocks=n_blocks),
        grid_spec=pltpu.PrefetchScalarGridSpec(
            num_scalar_prefetch=3,
            grid=(1,),
            in_specs=[anyspec, anyspec, anyspec, anyspec],
            out_specs=anyspec,
            scratch_shapes=[pltpu.VMEM((2, D_MODEL, D_FF), F32), pltpu.VMEM((2, D_MODEL, D_FF), F32),
                            pltpu.VMEM((2, D_FF, D_MODEL), F32),
                            pltpu.VMEM((FFN_BUFFERS * ROW_BLOCK, D_MODEL // 2), jnp.int32),
                            pltpu.VMEM((FFN_BUFFERS * ROW_BLOCK, D_MODEL // 2), jnp.int32),
                            pltpu.SemaphoreType.DMA((2, 3)),
                            pltpu.SemaphoreType.DMA((FFN_BUFFERS,)),
                            pltpu.SemaphoreType.DMA((FFN_BUFFERS,))],
        ),
        out_shape=jax.ShapeDtypeStruct((n_rows, D_MODEL // 2), jnp.int32),
        compiler_params=pltpu.CompilerParams(
            dimension_semantics=("arbitrary",), vmem_limit_bytes=VMEM_LIMIT),
        name="expert_ffn",
    )(first_blk, n_blk, nv, w_gate, w_up, w_down, xs)


def _combine_kernel(h_ref, ya_ref, yb_ref, gate_ref, g_ref, b_ref, o_ref):
    gate = gate_ref[...]
    a_hi, a_lo = _unpack_bf16_halves(ya_ref[...])
    b_hi, b_lo = _unpack_bf16_halves(yb_ref[...])
    g0, g1 = gate[:, 0:1], gate[:, 1:2]
    ffn = jnp.concatenate([a_hi * g0 + b_hi * g1, a_lo * g0 + b_lo * g1], axis=1)
    o_ref[...] = _layer_norm(DEEPNORM_ALPHA * h_ref[...] + ffn, g_ref[...], b_ref[...])


def _combine(h, y2, gate, g, b, tm=1024):
    T = h.shape[0]
    nt = T // tm
    return pl.pallas_call(
        _combine_kernel,
        grid=(nt,),
        in_specs=[pl.BlockSpec((tm, D_MODEL), lambda i: (i, 0)),
                  pl.BlockSpec((tm, D_MODEL // 2), lambda i: (i, 0)),
                  pl.BlockSpec((tm, D_MODEL // 2), lambda i: (i + nt, 0)),
                  pl.BlockSpec((tm, 2), lambda i: (i, 0)),
                  pl.BlockSpec((1, D_MODEL), lambda i: (0, 0)),
                  pl.BlockSpec((1, D_MODEL), lambda i: (0, 0))],
        out_specs=pl.BlockSpec((tm, D_MODEL), lambda i: (i, 0)),
        out_shape=jax.ShapeDtypeStruct((T, D_MODEL), F32),
        compiler_params=pltpu.CompilerParams(
            dimension_semantics=("parallel",), vmem_limit_bytes=VMEM_LIMIT),
        name="combine",
    )(h, y2, y2, gate, g, b)


def kernel(x, w_in, gla_gate_w2, gla_gate_b, gla_norm_g, dil_norm_g, w_out, ln1_g, ln1_b,
           router_coarse_w, router_coarse_b, router_fine_w, router_fine_b,
           expert_w_gate, expert_w_up, expert_w_down, ln2_g, ln2_b):
    B, S, D = x.shape
    T = B * S
    depth = w_in.shape[0]
    slopes = jnp.exp2(-8.0 * jnp.arange(1, DIL_HEADS + 1, dtype=F32) / DIL_HEADS)
    n_rows = 2 * T + N_EXPERTS * ROW_BLOCK
    n_blocks_pad = 2 * LANES
    h = x.reshape(T, D)
    w_in_t = jnp.swapaxes(w_in, 1, 2)
    for l in range(depth):
        w2 = jnp.pad(gla_gate_w2[l], ((0, LANES - GLA_GATE_RANK), (0, 0)))
        w2h, w2l = _split_bf16(w2)
        q, k, v, r, la, dq, dk, dv = _in_proj(h, w_in_t, l, w2h, w2l, gla_gate_b[l][None, :])
        o_gla = _gla(q, k, v, r, la, gla_norm_g[l][None, :], B, S)
        g2 = jnp.tile(dil_norm_g[l], 2)[None, :]
        o_dil = _dilated(slopes, dq, dk, dv, g2, B, S)
        rw = jnp.concatenate([router_fine_w[l].reshape(D, N_EXPERTS), router_coarse_w[l]], axis=1)
        rw = jnp.pad(rw, ((0, 0), (0, LANES - N_EXPERTS - N_GROUPS))).T
        rwh, rwl = _split_bf16(rw)
        rb = jnp.concatenate([router_fine_b[l].reshape(N_EXPERTS), router_coarse_b[l]])
        rb = jnp.pad(rb, (0, LANES - N_EXPERTS - N_GROUPS))[:, None]
        h1, h1p, eid_t, gate, cnt = _out_proj(o_gla, o_dil, h, w_out, l,
                                              ln1_g[l][None, :], ln1_b[l][None, :], rwh, rwl, rb)
        dest_t, be, nv = _positions(eid_t, cnt, n_blocks_pad)
        dest_flat = dest_t.reshape(2 * T)
        src_tok = _sc_inverse_rows(dest_flat, n_rows)
        xs = _sc_gather_rows(h1p, src_tok)
        be = be.reshape(n_blocks_pad)
        y = _ffn(be[:N_EXPERTS], be[LANES:LANES + N_EXPERTS], nv.reshape(LANES)[:1], xs,
                 expert_w_gate[l], expert_w_up[l], expert_w_down[l])
        y2 = _sc_gather_rows(y, dest_flat)
        h = _combine(h1, y2, gate, ln2_g[l][None, :], ln2_b[l][None, :])
    return h.reshape(B, S, D)
```

```python
import functools
import math

import jax
import jax.numpy as jnp
from jax import lax
from jax.experimental import pallas as pl
from jax.experimental.pallas import tpu as pltpu
from jax.experimental.pallas import tpu_sc as plsc

D_MODEL = 1024
GLA_HEADS = 4
GLA_DK = 64
GLA_DV = 128
GLA_KEY_WIDTH = GLA_HEADS * GLA_DK
GLA_WIDTH = GLA_HEADS * GLA_DV
GLA_GATE_RANK = 16
GLA_GATE_TEMP = 16.0
DIL_HEADS = 8
DIL_DH = 64
DIL_WIDTH = DIL_HEADS * DIL_DH
DIL_CONFIGS = ((128, 1), (512, 4), (2048, 16))
DIL_BLOCK = 128
DIL_MAX_R = max(r for _, r in DIL_CONFIGS)
DIL_PAD = DIL_BLOCK * DIL_MAX_R
DIL_UNROLL = 16
DIL_SPLIT_CFG = 2
N_GROUPS = 4
EXPERTS_PER_GROUP = 8
N_EXPERTS = N_GROUPS * EXPERTS_PER_GROUP
D_FF = 512
DEEPNORM_ALPHA = 2.0 ** 0.25
EPS = 1e-5
IN_PROJ_GATE_COL = 2 * GLA_KEY_WIDTH + 2 * GLA_WIDTH
LOG2E = math.log2(math.e)

LANES = 128
SUBLANES = 8
assert DIL_SPLIT_CFG == len(DIL_CONFIGS) - 1 and DIL_CONFIGS[DIL_SPLIT_CFG][1] % (2 * SUBLANES) == 0
GLA_CHUNK = 128
GLA_SUB = 64
GLA_UNROLL = 4
SC_INDEX_WINDOW = 128
SC_GATHER_BUFFERS = 6
SC_GATHER_ROWS = 32
FFN_BUFFERS = 12
FFN_GROUPS = (4, 2)
ROW_BLOCK = 256
VMEM_LIMIT = 56 * 1024 * 1024

F32 = jnp.float32
BF16 = jnp.bfloat16


def _dot(a, b):
    return jnp.dot(a, b, preferred_element_type=F32)


def _dot_nt(a, b):
    return lax.dot_general(a, b, (((1,), (1,)), ((), ())), preferred_element_type=F32)


def _dot_tn(a, b):
    return lax.dot_general(a, b, (((0,), (0,)), ((), ())), preferred_element_type=F32)


def _split_bf16(v):
    hi = v.astype(BF16)
    lo = (v - hi.astype(F32)).astype(BF16)
    return hi, lo


def _pack_bf16_halves(v):
    w = v.shape[1] // 2
    hi = lax.bitcast_convert_type(v[:, :w].astype(BF16).astype(F32), jnp.int32)
    lo = lax.bitcast_convert_type(v[:, w:].astype(BF16).astype(F32), jnp.int32)
    return hi | lax.shift_right_logical(lo, 16)


def _unpack_bf16_halves(words):
    hi = lax.bitcast_convert_type(words & jnp.int32(-65536), F32)
    lo = lax.bitcast_convert_type(lax.shift_left(words, 16), F32)
    return hi, lo


def _layer_norm(v, g, b):
    mu = jnp.mean(v, axis=-1, keepdims=True)
    c = v - mu
    var = jnp.mean(c * c, axis=-1, keepdims=True)
    return c * lax.rsqrt(var + EPS) * g + b


def _in_proj_kernel(x_ref, w_ref, w2h_ref, w2l_ref, gb_ref,
                    q_ref, k_ref, v_ref, r_ref, la_ref, dq_ref, dk_ref, dv_ref, wg_s, wd_s, wa_s):
    a0 = IN_PROJ_GATE_COL

    @pl.when(pl.program_id(0) == 0)
    def _():
        wg_s[...] = w_ref[0:a0, :].T.astype(BF16)
        gate_tile = w_ref[a0:a0 + LANES, :].T
        lane = lax.broadcasted_iota(jnp.int32, gate_tile.shape, 1)
        wa_s[...] = jnp.where(lane < GLA_GATE_RANK, gate_tile, 0.0).astype(BF16)
        wd_s[...] = w_ref[a0 + GLA_GATE_RANK:, :].T.astype(BF16)

    xb = x_ref[...].astype(BF16)

    def piece(w_s, c0, c1):
        return _dot(xb, w_s[:, c0:c1])

    q_ref[...] = (piece(wg_s, 0, 256) * (GLA_DK ** -0.5)).astype(BF16)
    k_ref[...] = piece(wg_s, 256, 512).astype(BF16)
    v_ref[...] = piece(wg_s, 512, 1024).astype(BF16)
    r_ref[...] = piece(wg_s, 1024, 1536).astype(BF16)
    dq_ref[...] = (piece(wd_s, 0, 512) * (DIL_DH ** -0.5 * LOG2E)).astype(BF16)
    dk_ref[...] = piece(wd_s, 512, 1024).astype(BF16)
    dv_ref[...] = piece(wd_s, 1024, 1536).astype(BF16)
    ga = _dot(xb, wa_s[...])
    ga_hi, ga_lo = _split_bf16(ga)
    z = _dot(ga_hi, w2h_ref[...]) + _dot(ga_lo, w2h_ref[...]) + _dot(ga_hi, w2l_ref[...]) + gb_ref[...]
    log_sig = jnp.minimum(z, 0.0) - jnp.log1p(jnp.exp(-jnp.abs(z)))
    la_ref[...] = log_sig * (1.0 / GLA_GATE_TEMP)


def _in_proj(x2, w_in_t, layer, w2h, w2l, gb, tm=1024):
    T = x2.shape[0]
    row = lambda wd: pl.BlockSpec((tm, wd), lambda i: (i, 0))
    full = lambda a: pl.BlockSpec(a.shape, lambda i: (0,) * a.ndim)
    outs = [(GLA_KEY_WIDTH, BF16), (GLA_KEY_WIDTH, BF16), (GLA_WIDTH, BF16), (GLA_WIDTH, BF16), (GLA_KEY_WIDTH, F32),
            (DIL_WIDTH, BF16), (DIL_WIDTH, BF16), (DIL_WIDTH, BF16)]
    group = w_in_t.shape[1] - IN_PROJ_GATE_COL - GLA_GATE_RANK
    return pl.pallas_call(
        _in_proj_kernel,
        grid=(T // tm,),
        in_specs=[row(D_MODEL),
                  pl.BlockSpec((None,) + w_in_t.shape[1:], lambda i: (layer, 0, 0), pipeline_mode=pl.Buffered(1)),
                  full(w2h), full(w2l), full(gb)],
        out_specs=[row(wd) for wd, _ in outs],
        out_shape=[jax.ShapeDtypeStruct((T, wd), dt) for wd, dt in outs],
        scratch_shapes=[pltpu.VMEM((D_MODEL, IN_PROJ_GATE_COL), BF16), pltpu.VMEM((D_MODEL, group), BF16),
                        pltpu.VMEM((D_MODEL, LANES), BF16)],
        compiler_params=pltpu.CompilerParams(
            dimension_semantics=("arbitrary",), vmem_limit_bytes=VMEM_LIMIT),
        name="in_proj",
    )(x2, w_in_t, w2h, w2l, gb)


def _gla_kernel(q_ref, k_ref, v_ref, r_ref, la_ref, g_ref, o_ref, s_ref, *, seq_block):
    C = GLA_CHUNK
    H = GLA_SUB
    assert C == 2 * H

    @pl.when(pl.program_id(1) == 0)
    def _():
        s_ref[...] = jnp.zeros_like(s_ref)

    ri = lax.broadcasted_iota(jnp.int32, (C, C), 0)
    ci = lax.broadcasted_iota(jnp.int32, (C, C), 1)
    same_sub = (ri // H) == (ci // H)
    sum_ops = jnp.concatenate([jnp.logical_and(same_sub, ci <= ri).astype(BF16), same_sub.astype(BF16),
                               jnp.ones((C, C), BF16)], axis=0)
    diag_mask = jnp.logical_and(same_sub, ci <= ri)
    off_mask = (ri // H) > (ci // H)
    second = lax.broadcasted_iota(jnp.int32, (C, 1), 0) >= H
    ones_cl = jnp.ones((C, LANES), BF16)
    lane_k = lax.broadcasted_iota(jnp.int32, (1, GLA_KEY_WIDTH), 1) // GLA_DK
    head_masks = [(lane_k == h).astype(F32) for h in range(GLA_HEADS)]
    srow = lax.broadcasted_iota(jnp.int32, (GLA_KEY_WIDTH, GLA_WIDTH), 0) // GLA_DK
    scol = lax.broadcasted_iota(jnp.int32, (GLA_KEY_WIDTH, GLA_WIDTH), 1) // GLA_DV
    state_mask = (srow == scol).astype(F32)
    g = g_ref[...]

    def trip(t, carry):
        U = GLA_UNROLL
        rows = [pl.ds(pl.multiple_of((t * U + u) * C, C), C) for u in range(U)]
        la2s, sums = [], []
        for u in range(U):
            la_hi, la_lo = _split_bf16(la_ref[rows[u], :])
            la2 = jnp.concatenate([la_hi, la_lo], axis=1)
            la2s.append(la2)
            sm = _dot(sum_ops, la2)
            sums.append(sm[:, 0:GLA_KEY_WIDTH] + sm[:, GLA_KEY_WIDTH:])
        q_states, k_states, scores = [], [], []
        for u in range(U):
            b = sums[u][0:C]
            t_sub = sums[u][C:2 * C]
            other = sums[u][2 * C:3 * C] - t_sub
            q = q_ref[rows[u], :].astype(F32)
            k = k_ref[rows[u], :].astype(F32)
            qd = q * jnp.exp(b)
            kd = (k * jnp.exp(-b)).astype(BF16)
            ke_f = k * jnp.exp(t_sub - b)
            cross = jnp.exp(other)
            q_states.append((qd * jnp.where(second, cross, 1.0)).astype(BF16))
            k_states.append((ke_f * jnp.where(second, 1.0, cross)).astype(BF16))
            q_heads = jnp.concatenate([(qd * head_masks[h]).astype(BF16) for h in range(GLA_HEADS)], axis=0)
            keys2 = jnp.concatenate([kd, ke_f.astype(BF16)], axis=0)
            scores.append(_dot_nt(q_heads, keys2))
        decs, upds = [], []
        for u in range(U):
            tot = _dot_tn(la2s[u], ones_cl)
            decs.append(jnp.exp(tot[0:GLA_KEY_WIDTH] + tot[GLA_KEY_WIDTH:]))
            upds.append(_dot_tn(k_states[u], v_ref[rows[u], :]) * state_mask)
        o_inters = []
        for u in range(U):
            state = s_ref[...]
            o_inters.append(_dot(q_states[u], state.astype(BF16)))
            for h in range(GLA_HEADS):
                cols = slice(h * GLA_DV, (h + 1) * GLA_DV)
                s_ref[:, cols] = state[:, cols] * decs[u] + upds[u][:, cols]
        for u in range(U):
            v = v_ref[rows[u], :]
            outs = []
            for h in range(GLA_HEADS):
                sh = scores[u][h * C:(h + 1) * C]
                a = jnp.where(diag_mask, sh[:, 0:C], 0.0) + jnp.where(off_mask, sh[:, C:2 * C], 0.0)
                cols = slice(h * GLA_DV, (h + 1) * GLA_DV)
                o = _dot(a.astype(BF16), v[:, cols]) + o_inters[u][:, cols]
                o = o * lax.rsqrt(jnp.mean(o * o, axis=-1, keepdims=True) + EPS) * g
                outs.append(o)
            o_all = jnp.concatenate(outs, axis=-1)
            rr = r_ref[rows[u], :].astype(F32)
            o_ref[rows[u], :] = (o_all * (rr * jax.nn.sigmoid(rr))).astype(BF16)
        return carry

    lax.fori_loop(0, seq_block // C // GLA_UNROLL, trip, 0)


def _gla(q, k, v, r, la, g, batch, seq, seq_block=1024):
    nsb = seq // seq_block
    row = lambda w: pl.BlockSpec((seq_block, w), lambda b, s: (b * nsb + s, 0))
    return pl.pallas_call(
        functools.partial(_gla_kernel, seq_block=seq_block),
        grid=(batch, nsb),
        in_specs=[row(GLA_KEY_WIDTH), row(GLA_KEY_WIDTH), row(GLA_WIDTH), row(GLA_WIDTH), row(GLA_KEY_WIDTH),
                  pl.BlockSpec((1, GLA_DV), lambda b, s: (0, 0))],
        out_specs=row(GLA_WIDTH),
        out_shape=jax.ShapeDtypeStruct((batch * seq, GLA_WIDTH), BF16),
        scratch_shapes=[pltpu.VMEM((GLA_KEY_WIDTH, GLA_WIDTH), F32)],
        compiler_params=pltpu.CompilerParams(
            dimension_semantics=("parallel", "arbitrary"), vmem_limit_bytes=VMEM_LIMIT),
        name="gla",
    )(q, k, v, r, la, g)


def _dil_kernel(slope_ref, q_ref, k_ref, v_ref, g_ref, o_ref,
                qf, kf, vf, qs, ks, vs, kd, va, vb, oc, lc, ocs, lcs, *, seq):
    B = DIL_BLOCK
    U = DIL_UNROLL
    pair = pl.program_id(1)
    for src, nat, split in ((q_ref, qf, qs), (k_ref, kf, ks), (v_ref, vf, vs)):
        x = src[...].astype(F32)
        nat[...] = x
        x3 = x.reshape(seq // (2 * SUBLANES), 2 * SUBLANES, LANES)
        split[0] = x3[:, :SUBLANES, :].reshape(seq // 2, LANES)
        split[1] = x3[:, SUBLANES:, :].reshape(seq // 2, LANES)

    lane = lax.broadcasted_iota(jnp.int32, (1, LANES), 1)
    first = lane < DIL_DH
    ii = lax.broadcasted_iota(jnp.int32, (B, B), 0)
    jj = lax.broadcasted_iota(jnp.int32, (B, B), 1)
    upper = jj > ii
    eye = jj == ii
    dist = jnp.bitwise_and(ii - jj, B - 1).astype(F32)
    neg = jnp.float32(-jnp.inf)
    neg_tile = jnp.full((B, B), neg, F32)
    zero_tile = jnp.zeros((B, LANES), BF16)

    def split_rows(r, c, n):
        half = r // 2
        start = c % SUBLANES + c // (2 * SUBLANES) * SUBLANES + n * (B * half)
        return (c // SUBLANES) % 2, pl.ds(start, B, stride=half)

    def class_rows(nat, split, cfg, r, c, n):
        if r == 1:
            return nat[pl.ds(pl.multiple_of(n * B, B), B), :]
        if cfg == DIL_SPLIT_CFG:
            parity, rows = split_rows(r, c, n)
            return split[parity, rows, :]
        return nat[pl.ds(c + n * (B * r), B, stride=r), :]

    for cfg, (window, r) in enumerate(DIL_CONFIGS):
        nb = seq // r // B
        cs = nb + 1
        bias_prev, bias_cur = [], []
        for hh in range(2):
            slope = slope_ref[2 * pair + hh] * (float(r) * LOG2E)
            bias = dist * (-slope)
            bias_prev.append(jnp.where(upper, bias, jnp.where(eye, -slope * float(B), neg)))
            bias_cur.append(jnp.where(upper, neg, bias))

        for c in range(r):
            rows0 = slice(c * cs * B, (c * cs + 1) * B)
            kd[rows0, :] = zero_tile
            va[rows0, :] = zero_tile
            vb[rows0, :] = zero_tile

        def prep(t4, carry, cfg=cfg, r=r, nb=nb, cs=cs):
            for j in range(4):
                t = t4 * 4 + j
                c = t // nb
                n = t % nb
                dst = pl.ds(pl.multiple_of((c * cs + 1 + n) * B, B), B)
                kd[dst, :] = class_rows(kf, ks, cfg, r, c, n).astype(BF16)
                v = class_rows(vf, vs, cfg, r, c, n)
                va[dst, :] = jnp.where(first, v, 1.0).astype(BF16)
                vb[dst, :] = jnp.where(first, 1.0, v).astype(BF16)
            return carry

        lax.fori_loop(0, seq // B // 4, prep, 0)

        def geom(it, r=r, nb=nb, cs=cs):
            c = it // nb
            n = it % nb
            start = c + n * (B * r)
            rows = pl.ds(start, B, stride=r) if r > 1 else pl.ds(pl.multiple_of(start, B), B)
            kv = pl.ds(pl.multiple_of((c * cs + n) * B, B), 2 * B)
            return rows, kv, c, n

        def body(t, carry, cfg=cfg, r=r, nb=nb, bias_prev=bias_prev, bias_cur=bias_cur, geom=geom):
            geo = [geom(t * U + u) for u in range(U)]
            scores = []
            for u in range(U):
                _, kv, c, n = geo[u]
                q = class_rows(qf, qs, cfg, r, c, n)
                q_heads = jnp.concatenate([jnp.where(first, q, 0.0), jnp.where(first, 0.0, q)], axis=0)
                scores.append(_dot_nt(q_heads.astype(BF16), kd[kv, :]))
            probs, maxes = [], []
            for u in range(U):
                if nb % U == 0:
                    has_prev = True if u > 0 else (t * U) % nb > 0
                else:
                    assert U % nb == 0
                    has_prev = (u % nb) > 0
                for hh in range(2):
                    s2 = scores[u][hh * B:(hh + 1) * B]
                    if has_prev is True:
                        bp = bias_prev[hh]
                    elif has_prev is False:
                        bp = neg_tile
                    else:
                        bp = jnp.where(has_prev, bias_prev[hh], neg)
                    s_prev = s2[:, 0:B] + bp
                    s_cur = s2[:, B:2 * B] + bias_cur[hh]
                    m = jnp.max(jnp.maximum(s_prev, s_cur), axis=-1, keepdims=True)
                    probs.append(jnp.concatenate([jnp.exp2(s_prev - m), jnp.exp2(s_cur - m)], axis=1).astype(BF16))
                    maxes.append(m)
            for u in range(U):
                rows, kv, c, n = geo[u]
                acc0 = _dot(probs[2 * u], va[kv, :])
                acc1 = _dot(probs[2 * u + 1], vb[kv, :])
                num = jnp.where(first, acc0, acc1)
                den = pltpu.roll(jnp.where(first, acc1, acc0), DIL_DH, axis=1)
                out = num * (1.0 / den)
                lse = jnp.where(first, maxes[2 * u], maxes[2 * u + 1]) + jnp.log2(den)
                if cfg == DIL_SPLIT_CFG:
                    parity, dst = split_rows(r, c, n)
                    ocs[parity, dst, :] = out
                    lcs[parity, dst, :] = lse
                else:
                    oc[cfg, rows, :] = out
                    lc[cfg, rows, :] = lse
            return carry

        lax.fori_loop(0, seq // B // U, body, 0)

    g = g_ref[...]
    CH = 512

    def mix(i, carry):
        rows = pl.ds(pl.multiple_of(i * CH, CH), CH)
        half_rows = pl.ds(pl.multiple_of(i * (CH // 2), CH // 2), CH // 2)

        def tiles_interleaved(split):
            even = split[0, half_rows, :].reshape(CH // (2 * SUBLANES), 1, SUBLANES, LANES)
            odd = split[1, half_rows, :].reshape(CH // (2 * SUBLANES), 1, SUBLANES, LANES)
            return jnp.concatenate([even, odd], axis=1).reshape(CH, LANES)

        l0, l1, l2 = lc[0, rows, :], lc[1, rows, :], tiles_interleaved(lcs)
        m = jnp.maximum(jnp.maximum(l0, l1), l2)
        e0, e1, e2 = jnp.exp2(l0 - m), jnp.exp2(l1 - m), jnp.exp2(l2 - m)
        inv = 1.0 / (e0 + e1 + e2)
        o = (e0 * oc[0, rows, :] + e1 * oc[1, rows, :] + e2 * tiles_interleaved(ocs)) * inv
        sq = o * o
        ms_a = jnp.sum(jnp.where(first, sq, 0.0), axis=-1, keepdims=True) * (1.0 / DIL_DH)
        ms_b = jnp.sum(jnp.where(first, 0.0, sq), axis=-1, keepdims=True) * (1.0 / DIL_DH)
        ms = jnp.where(first, ms_a, ms_b)
        o_ref[rows, :] = (o * lax.rsqrt(ms + EPS) * g).astype(BF16)
        return carry

    lax.fori_loop(0, seq // CH, mix, 0)


def _dilated(slopes, dq, dk, dv, g2, batch, seq):
    blk = pl.BlockSpec((seq, LANES), lambda b, p, s: (b, p))
    return pl.pallas_call(
        functools.partial(_dil_kernel, seq=seq),
        grid_spec=pltpu.PrefetchScalarGridSpec(
            num_scalar_prefetch=1,
            grid=(batch, DIL_WIDTH // LANES),
            in_specs=[blk, blk, blk, pl.BlockSpec((1, LANES), lambda b, p, s: (0, 0))],
            out_specs=blk,
            scratch_shapes=[pltpu.VMEM((seq, LANES), F32)] * 3 + [pltpu.VMEM((2, seq // 2, LANES), F32)] * 3
                           + [pltpu.VMEM((seq + DIL_PAD, LANES), BF16)] * 3
                           + [pltpu.VMEM((2, seq, LANES), F32)] * 2 + [pltpu.VMEM((2, seq // 2, LANES), F32)] * 2,
        ),
        out_shape=jax.ShapeDtypeStruct((batch * seq, DIL_WIDTH), BF16),
        compiler_params=pltpu.CompilerParams(
            dimension_semantics=("parallel", "parallel"), vmem_limit_bytes=VMEM_LIMIT),
        name="dilated",
    )(slopes, dq, dk, dv, g2)


def _out_proj_kernel(og_ref, od_ref, x_ref, wg_ref, wd_ref, g_ref, b_ref,
                     rwh_ref, rwl_ref, rb_ref, h_ref, hp_ref, eid_ref, gate_ref, cnt_ref, wg_s, wd_s):
    @pl.when(pl.program_id(0) == 0)
    def _():
        wg_s[...] = wg_ref[...].astype(BF16)
        wd_s[...] = wd_ref[...].astype(BF16)

    mix = _dot(og_ref[...], wg_s[...]) + _dot(od_ref[...], wd_s[...])
    h = _layer_norm(DEEPNORM_ALPHA * x_ref[...] + mix, g_ref[...], b_ref[...])
    h_ref[...] = h
    hp_ref[...] = _pack_bf16_halves(h)
    h_hi, h_lo = _split_bf16(h)
    lt = (_dot_nt(rwh_ref[...], h_hi) + _dot_nt(rwh_ref[...], h_lo) + _dot_nt(rwl_ref[...], h_hi)
          + rb_ref[...])
    tm = lt.shape[1]
    row = lax.broadcasted_iota(jnp.int32, (EXPERTS_PER_GROUP, tm), 0).astype(F32)
    neg = jnp.float32(-jnp.inf)
    big = jnp.float32(1e9)
    coarse = jnp.where(row < N_GROUPS, lt[N_EXPERTS:N_EXPERTS + EXPERTS_PER_GROUP, :], neg)
    cmax = jnp.max(coarse, axis=0, keepdims=True)
    g_idx = jnp.min(jnp.where(coarse == cmax, row, big), axis=0, keepdims=True)
    p_group = 1.0 / jnp.sum(jnp.exp(coarse - cmax), axis=0, keepdims=True)
    fine = lt[(N_GROUPS - 1) * EXPERTS_PER_GROUP:N_EXPERTS, :]
    for g in range(N_GROUPS - 2, -1, -1):
        fine = jnp.where(g_idx == g, lt[g * EXPERTS_PER_GROUP:(g + 1) * EXPERTS_PER_GROUP, :], fine)
    v1 = jnp.max(fine, axis=0, keepdims=True)
    i1 = jnp.min(jnp.where(fine == v1, row, big), axis=0, keepdims=True)
    fine2 = jnp.where(row == i1, neg, fine)
    v2 = jnp.max(fine2, axis=0, keepdims=True)
    i2 = jnp.min(jnp.where(fine2 == v2, row, big), axis=0, keepdims=True)
    e2 = jnp.exp(v2 - v1)
    den = 1.0 + e2
    gate1 = p_group * (1.0 / den)
    gate2 = p_group * (e2 / den)
    id1 = g_idx * EXPERTS_PER_GROUP + i1
    id2 = g_idx * EXPERTS_PER_GROUP + i2
    eid_ref[...] = jnp.concatenate([id1, id2], axis=0).astype(jnp.int32)
    slab = jnp.concatenate([gate1, gate2, jnp.zeros((LANES - 2, tm), F32)], axis=0)
    gate_ref[...] = slab.T[:, 0:2]
    sub = lax.broadcasted_iota(jnp.int32, (LANES, tm), 0).astype(F32)
    onehot = jnp.logical_or(sub == id1, sub == id2).astype(BF16)

    @pl.when(pl.program_id(0) == 0)
    def _():
        cnt_ref[...] = jnp.zeros_like(cnt_ref)

    cnt_ref[...] += _dot(onehot, jnp.ones((tm, LANES), BF16))


def _out_proj(og, od, x2, w_out, layer, g, b, rwh, rwl, rb, tm=1024):
    T = x2.shape[0]
    row = lambda w: pl.BlockSpec((tm, w), lambda i: (i, 0))
    full = lambda a: pl.BlockSpec(a.shape, lambda i: (0,) * a.ndim)
    half = lambda p: pl.BlockSpec((None, GLA_WIDTH, D_MODEL), lambda i: (layer, p, 0))
    return pl.pallas_call(
        _out_proj_kernel,
        grid=(T // tm,),
        in_specs=[row(GLA_WIDTH), row(DIL_WIDTH), row(D_MODEL), half(0), half(1), full(g), full(b),
                  full(rwh), full(rwl), full(rb)],
        out_specs=[row(D_MODEL), row(D_MODEL // 2), pl.BlockSpec((2, tm), lambda i: (0, i)), row(2),
                   pl.BlockSpec((LANES, LANES), lambda i: (0, 0))],
        out_shape=[jax.ShapeDtypeStruct((T, D_MODEL), F32),
                   jax.ShapeDtypeStruct((T, D_MODEL // 2), jnp.int32),
                   jax.ShapeDtypeStruct((2, T), jnp.int32),
                   jax.ShapeDtypeStruct((T, 2), F32),
                   jax.ShapeDtypeStruct((LANES, LANES), F32)],
        scratch_shapes=[pltpu.VMEM((GLA_WIDTH, D_MODEL), BF16), pltpu.VMEM((DIL_WIDTH, D_MODEL), BF16)],
        compiler_params=pltpu.CompilerParams(
            dimension_semantics=("arbitrary",), vmem_limit_bytes=VMEM_LIMIT),
        name="out_proj_router",
    )(og, od, x2, w_out, w_out, g, b, rwh, rwl, rb)


def _positions_kernel(eid_ref, cnt_ref, dest_ref, be_ref, nv_ref, carry_ref, sp_ref, tri_ref, *, tb):
    i = pl.program_id(0)

    @pl.when(i == 0)
    def _():
        shift = int(math.log2(ROW_BLOCK))
        nb_col = (cnt_ref[...].astype(jnp.int32) + (ROW_BLOCK - 1)) >> shift
        r = lax.broadcasted_iota(jnp.int32, (LANES, LANES), 0)
        c = lax.broadcasted_iota(jnp.int32, (LANES, LANES), 1)
        nb_f = jnp.where(r < N_EXPERTS, nb_col, 0).astype(F32)
        start_col = _dot((c < r).astype(BF16), nb_f.astype(BF16))
        sp_ref[...] = start_col * float(ROW_BLOCK)
        carry_ref[...] = jnp.zeros_like(carry_ref)
        be_ref[...] = jnp.concatenate([start_col.T[0:1, :], nb_f.T[0:1, :]], axis=1).astype(jnp.int32)
        total = jnp.sum(nb_f[:, 0:1], axis=0, keepdims=True)
        nv_ref[...] = jnp.broadcast_to(total, (1, LANES)).astype(jnp.int32)
        tr = lax.broadcasted_iota(jnp.int32, (tb, tb), 0)
        tc = lax.broadcasted_iota(jnp.int32, (tb, tb), 1)
        tri_ref[...] = (tr < tc).astype(BF16)

    sub = lax.broadcasted_iota(jnp.int32, (LANES, tb), 0)
    oh1 = sub == eid_ref[0:1, :]
    oh2 = sub == eid_ref[1:2, :]
    oh = jnp.logical_or(oh1, oh2).astype(BF16)
    offset = jnp.tile(carry_ref[...] + sp_ref[...], (1, tb // LANES))
    before = _dot(oh, tri_ref[...]) + offset
    d1 = jnp.sum(jnp.where(oh1, before, 0.0), axis=0, keepdims=True)
    d2 = jnp.sum(jnp.where(oh2, before, 0.0), axis=0, keepdims=True)
    dest_ref[...] = jnp.concatenate([d1, d2], axis=0).astype(jnp.int32)
    carry_ref[...] += _dot(oh, jnp.ones((tb, LANES), BF16))


def _positions(eid_t, cnt, n_blocks_pad, tb=1024):
    T = eid_t.shape[1]
    return pl.pallas_call(
        functools.partial(_positions_kernel, tb=tb),
        grid=(T // tb,),
        in_specs=[pl.BlockSpec((2, tb), lambda i: (0, i)), pl.BlockSpec((LANES, LANES), lambda i: (0, 0))],
        out_specs=[pl.BlockSpec((2, tb), lambda i: (0, i)),
                   pl.BlockSpec((1, n_blocks_pad), lambda i: (0, 0)),
                   pl.BlockSpec((1, LANES), lambda i: (0, 0))],
        out_shape=[jax.ShapeDtypeStruct((2, T), jnp.int32),
                   jax.ShapeDtypeStruct((1, n_blocks_pad), jnp.int32),
                   jax.ShapeDtypeStruct((1, LANES), jnp.int32)],
        scratch_shapes=[pltpu.VMEM((LANES, LANES), F32), pltpu.VMEM((LANES, LANES), F32),
                        pltpu.VMEM((tb, tb), BF16)],
        compiler_params=pltpu.CompilerParams(dimension_semantics=("arbitrary",)),
        name="positions",
    )(eid_t, cnt)


def _sc_gather_rows(table, idx):
    n = idx.shape[0]
    d = table.shape[1]
    info = plsc.get_sparse_core_info()
    nc, ns = info.num_cores, info.num_subcores
    per_w = n // (nc * ns)
    assert per_w * nc * ns == n and per_w % SC_INDEX_WINDOW == 0
    mesh = plsc.VectorSubcoreMesh(core_axis_name="core", subcore_axis_name="subcore")
    nchunk = per_w // SC_GATHER_ROWS
    nbuf = SC_GATHER_BUFFERS

    @functools.partial(
        pl.kernel, out_type=jax.ShapeDtypeStruct((n, d), table.dtype), mesh=mesh,
        scratch_types=[pltpu.VMEM((per_w,), jnp.int32),
                       pltpu.VMEM((nbuf, SC_GATHER_ROWS, d), table.dtype),
                       pltpu.SemaphoreType.DMA((nbuf,)), pltpu.SemaphoreType.DMA((nbuf,))],
        name="sc_gather_rows")
    def gather(x_hbm, i_hbm, o_hbm, idx_v, buf, gsem, wsem):
        wid = lax.axis_index("subcore") * nc + lax.axis_index("core")
        base = wid * per_w
        pltpu.sync_copy(i_hbm.at[pl.ds(base, per_w)], idx_v)

        def gather_copy(c):
            rows = idx_v.at[pl.ds(c * SC_GATHER_ROWS, SC_GATHER_ROWS)]
            return pltpu.make_async_copy(x_hbm.at[rows], buf.at[c % nbuf], gsem.at[c % nbuf])

        def write_copy(c):
            dst = o_hbm.at[pl.ds(base + c * SC_GATHER_ROWS, SC_GATHER_ROWS)]
            return pltpu.make_async_copy(buf.at[c % nbuf], dst, wsem.at[c % nbuf])

        for c in range(min(nbuf - 1, nchunk)):
            gather_copy(c).start()
        for c in range(nchunk):
            gather_copy(c).wait()
            write_copy(c).start()
            if c + nbuf - 1 < nchunk:
                if c >= 1:
                    write_copy(c - 1).wait()
                gather_copy(c + nbuf - 1).start()
        for c in range(max(0, nchunk - nbuf), nchunk):
            write_copy(c).wait()

    return gather(table, idx)


def _sc_scatter_rows(table, dest_flat, n_rows):
    n_tok, d = table.shape
    assert dest_flat.shape[0] == 2 * n_tok
    info = plsc.get_sparse_core_info()
    nc, ns = info.num_cores, info.num_subcores
    per_w = n_tok // (nc * ns)
    assert per_w * nc * ns == n_tok and per_w % SC_INDEX_WINDOW == 0
    mesh = plsc.VectorSubcoreMesh(core_axis_name="core", subcore_axis_name="subcore")
    nchunk = per_w // SC_GATHER_ROWS
    nbuf = SC_GATHER_BUFFERS

    @functools.partial(
        pl.kernel, out_type=jax.ShapeDtypeStruct((n_rows, d), table.dtype), mesh=mesh,
        scratch_types=[pltpu.VMEM((per_w,), jnp.int32), pltpu.VMEM((per_w,), jnp.int32),
                       pltpu.VMEM((nbuf, SC_GATHER_ROWS, d), table.dtype),
                       pltpu.SemaphoreType.DMA((nbuf,)), pltpu.SemaphoreType.DMA((nbuf, 2))],
        name="sc_scatter_rows")
    def scatter(x_hbm, i_hbm, o_hbm, idx_a, idx_b, buf, rsem, wsem):
        wid = lax.axis_index("subcore") * nc + lax.axis_index("core")
        base = wid * per_w
        pltpu.sync_copy(i_hbm.at[pl.ds(base, per_w)], idx_a)
        pltpu.sync_copy(i_hbm.at[pl.ds(n_tok + base, per_w)], idx_b)

        def read_copy(c):
            src = x_hbm.at[pl.ds(base + c * SC_GATHER_ROWS, SC_GATHER_ROWS)]
            return pltpu.make_async_copy(src, buf.at[c % nbuf], rsem.at[c % nbuf])

        def scatter_copies(c):
            window = pl.ds(c * SC_GATHER_ROWS, SC_GATHER_ROWS)
            return [pltpu.make_async_copy(buf.at[c % nbuf], o_hbm.at[idx.at[window]], wsem.at[c % nbuf, k])
                    for k, idx in enumerate((idx_a, idx_b))]

        for c in range(min(nbuf - 1, nchunk)):
            read_copy(c).start()
        for c in range(nchunk):
            read_copy(c).wait()
            for cp in scatter_copies(c):
                cp.start()
            if c + nbuf - 1 < nchunk:
                if c >= 1:
                    for cp in scatter_copies(c - 1):
                        cp.wait()
                read_copy(c + nbuf - 1).start()
        for c in range(max(0, nchunk - nbuf), nchunk):
            for cp in scatter_copies(c):
                cp.wait()

    return scatter(table, dest_flat)


def _ffn_kernel(first_ref, count_ref, nv_ref, wg_hbm, wu_hbm, wd_hbm, xs_hbm, y_hbm,
                wg32, wu32, wd32, xbuf, ybuf, wsem, isem, osem, *, n_blocks):
    nv = nv_ref[0]
    nbuf = FFN_BUFFERS

    def next_expert(e):
        def more(t):
            return jnp.logical_and(t < N_EXPERTS, count_ref[jnp.minimum(t, N_EXPERTS - 1)] == 0)
        return lax.while_loop(more, lambda t: t + 1, e + 1)

    def weight_copies(e, slot):
        ee = jnp.minimum(e, N_EXPERTS - 1)
        return [pltpu.make_async_copy(src.at[ee], dst.at[slot], wsem.at[slot, i])
                for i, (src, dst) in enumerate(((wg_hbm, wg32), (wu_hbm, wu32), (wd_hbm, wd32)))]

    def fetch_weights(e, slot):
        @pl.when(e < N_EXPERTS)
        def _():
            for c in weight_copies(e, slot):
                c.start()

    def take_weights(e, slot):
        for c in weight_copies(e, slot):
            c.wait()
        fetch_weights(next_expert(e), 1 - slot)

    def rows_of(b):
        return pl.ds(pl.multiple_of(b * ROW_BLOCK, ROW_BLOCK), ROW_BLOCK)

    def buf_rows(b, nblk):
        return pl.ds(pl.multiple_of((b % nbuf) * ROW_BLOCK, ROW_BLOCK), nblk * ROW_BLOCK)

    def in_copy(b):
        return pltpu.make_async_copy(xs_hbm.at[rows_of(b)], xbuf.at[buf_rows(b, 1)], isem.at[b % nbuf])

    def out_copy(b):
        return pltpu.make_async_copy(ybuf.at[buf_rows(b, 1)], y_hbm.at[rows_of(b)], osem.at[b % nbuf])

    def expert_mlp(words, slot):
        x = jnp.concatenate(_unpack_bf16_halves(words), axis=1)
        a = _dot(x, wg32[slot])
        u = _dot(x, wu32[slot])
        return _pack_bf16_halves(_dot(a * jax.nn.sigmoid(a) * u, wd32[slot]))

    ahead = nbuf - max(FFN_GROUPS)

    @pl.when(nv > 0)
    def _():
        e0 = next_expert(jnp.int32(-1))
        for i in range(ahead):
            @pl.when(i < nv)
            def _():
                in_copy(i).start()
        fetch_weights(e0, 0)
        take_weights(e0, 0)

        def step(carry):
            b, e, k = carry
            switch = b >= first_ref[e] + count_ref[e]
            e_new = jnp.where(switch, next_expert(e), e)
            k_new = jnp.where(switch, k + 1, k)

            @pl.when(switch)
            def _():
                take_weights(e_new, k_new % 2)

            end = first_ref[e_new] + count_ref[e_new]
            n = jnp.int32(1)
            for size in FFN_GROUPS:
                fits = jnp.logical_and(b + size <= end, b % nbuf + size <= nbuf)
                n = jnp.where(jnp.logical_and(n == 1, fits), size, n)
            for i in range(max(FFN_GROUPS)):
                @pl.when(jnp.logical_and(i < n, b + ahead + i < nv))
                def _():
                    in_copy(b + ahead + i).start()
            for i in range(max(FFN_GROUPS)):
                @pl.when(i < n)
                def _():
                    in_copy(b + i).wait()

                    @pl.when(b + i >= nbuf)
                    def _():
                        out_copy(b + i - nbuf).wait()

            for size in FFN_GROUPS + (1,):
                @pl.when(n == size)
                def _():
                    ybuf[buf_rows(b, size), :] = expert_mlp(xbuf[buf_rows(b, size), :], k_new % 2)

            for i in range(max(FFN_GROUPS)):
                @pl.when(i < n)
                def _():
                    out_copy(b + i).start()
            return b + n, e_new, k_new

        lax.while_loop(lambda c: c[0] < nv, step, (jnp.int32(0), e0, jnp.int32(0)))

        for i in range(nbuf):
            @pl.when(nv > i)
            def _():
                out_copy(nv - 1 - i).wait()

    ybuf[0:ROW_BLOCK, :] = jnp.zeros((ROW_BLOCK, D_MODEL // 2), jnp.int32)

    def fill(b, carry):
        pltpu.sync_copy(ybuf.at[pl.ds(0, ROW_BLOCK)], y_hbm.at[rows_of(b)])
        return carry

    lax.fori_loop(nv, n_blocks, fill, 0)


def _ffn(first_blk, n_blk, nv, xs, w_gate, w_up, w_down):
    n_rows = xs.shape[0]
    n_blocks = n_rows // ROW_BLOCK
    anyspec = pl.BlockSpec(memory_space=pl.ANY)
    return pl.pallas_call(
        functools.partial(_ffn_kernel, n_blocks=n_blocks),
        grid_spec=pltpu.PrefetchScalarGridSpec(
            num_scalar_prefetch=3,
            grid=(1,),
            in_specs=[anyspec, anyspec, anyspec, anyspec],
            out_specs=anyspec,
            scratch_shapes=[pltpu.VMEM((2, D_MODEL, D_FF), F32), pltpu.VMEM((2, D_MODEL, D_FF), F32),
                            pltpu.VMEM((2, D_FF, D_MODEL), F32),
                            pltpu.VMEM((FFN_BUFFERS * ROW_BLOCK, D_MODEL // 2), jnp.int32),
                            pltpu.VMEM((FFN_BUFFERS * ROW_BLOCK, D_MODEL // 2), jnp.int32),
                            pltpu.SemaphoreType.DMA((2, 3)),
                            pltpu.SemaphoreType.DMA((FFN_BUFFERS,)),
                            pltpu.SemaphoreType.DMA((FFN_BUFFERS,))],
        ),
        out_shape=jax.ShapeDtypeStruct((n_rows, D_MODEL // 2), jnp.int32),
        compiler_params=pltpu.CompilerParams(
            dimension_semantics=("arbitrary",), vmem_limit_bytes=VMEM_LIMIT),
        name="expert_ffn",
    )(first_blk, n_blk, nv, w_gate, w_up, w_down, xs)


def _combine_kernel(h_ref, ya_ref, yb_ref, gate_ref, g_ref, b_ref, o_ref):
    gate = gate_ref[...]
    a_hi, a_lo = _unpack_bf16_halves(ya_ref[...])
    b_hi, b_lo = _unpack_bf16_halves(yb_ref[...])
    g0, g1 = gate[:, 0:1], gate[:, 1:2]
    ffn = jnp.concatenate([a_hi * g0 + b_hi * g1, a_lo * g0 + b_lo * g1], axis=1)
    o_ref[...] = _layer_norm(DEEPNORM_ALPHA * h_ref[...] + ffn, g_ref[...], b_ref[...])


def _combine(h, y2, gate, g, b, tm=1024):
    T = h.shape[0]
    nt = T // tm
    return pl.pallas_call(
        _combine_kernel,
        grid=(nt,),
        in_specs=[pl.BlockSpec((tm, D_MODEL), lambda i: (i, 0)),
                  pl.BlockSpec((tm, D_MODEL // 2), lambda i: (i, 0)),
                  pl.BlockSpec((tm, D_MODEL // 2), lambda i: (i + nt, 0)),
                  pl.BlockSpec((tm, 2), lambda i: (i, 0)),
                  pl.BlockSpec((1, D_MODEL), lambda i: (0, 0)),
                  pl.BlockSpec((1, D_MODEL), lambda i: (0, 0))],
        out_specs=pl.BlockSpec((tm, D_MODEL), lambda i: (i, 0)),
        out_shape=jax.ShapeDtypeStruct((T, D_MODEL), F32),
        compiler_params=pltpu.CompilerParams(
            dimension_semantics=("parallel",), vmem_limit_bytes=VMEM_LIMIT),
        name="combine",
    )(h, y2, y2, gate, g, b)


def kernel(x, w_in, gla_gate_w2, gla_gate_b, gla_norm_g, dil_norm_g, w_out, ln1_g, ln1_b,
           router_coarse_w, router_coarse_b, router_fine_w, router_fine_b,
           expert_w_gate, expert_w_up, expert_w_down, ln2_g, ln2_b):
    B, S, D = x.shape
    T = B * S
    depth = w_in.shape[0]
    slopes = jnp.exp2(-8.0 * jnp.arange(1, DIL_HEADS + 1, dtype=F32) / DIL_HEADS)
    n_rows = 2 * T + N_EXPERTS * ROW_BLOCK
    n_blocks_pad = 2 * LANES
    h = x.reshape(T, D)
    w_in_t = jnp.swapaxes(w_in, 1, 2)
    for l in range(depth):
        w2 = jnp.pad(gla_gate_w2[l], ((0, LANES - GLA_GATE_RANK), (0, 0)))
        w2h, w2l = _split_bf16(w2)
        q, k, v, r, la, dq, dk, dv = _in_proj(h, w_in_t, l, w2h, w2l, gla_gate_b[l][None, :])
        o_gla = _gla(q, k, v, r, la, gla_norm_g[l][None, :], B, S)
        g2 = jnp.tile(dil_norm_g[l], 2)[None, :]
        o_dil = _dilated(slopes, dq, dk, dv, g2, B, S)
        rw = jnp.concatenate([router_fine_w[l].reshape(D, N_EXPERTS), router_coarse_w[l]], axis=1)
        rw = jnp.pad(rw, ((0, 0), (0, LANES - N_EXPERTS - N_GROUPS))).T
        rwh, rwl = _split_bf16(rw)
        rb = jnp.concatenate([router_fine_b[l].reshape(N_EXPERTS), router_coarse_b[l]])
        rb = jnp.pad(rb, (0, LANES - N_EXPERTS - N_GROUPS))[:, None]
        h1, h1p, eid_t, gate, cnt = _out_proj(o_gla, o_dil, h, w_out, l,
                                              ln1_g[l][None, :], ln1_b[l][None, :], rwh, rwl, rb)
        dest_t, be, nv = _positions(eid_t, cnt, n_blocks_pad)
        dest_flat = dest_t.reshape(2 * T)
        xs = _sc_scatter_rows(h1p, dest_flat, n_rows)
        be = be.reshape(n_blocks_pad)
        y = _ffn(be[:N_EXPERTS], be[LANES:LANES + N_EXPERTS], nv.reshape(LANES)[:1], xs,
                 expert_w_gate[l], expert_w_up[l], expert_w_down[l])
        y2 = _sc_gather_rows(y, dest_flat)
        h = _combine(h1, y2, gate, ln2_g[l][None, :], ln2_b[l][None, :])
    return h.reshape(B, S, D)
```

```python
import functools
import math

import jax
import jax.numpy as jnp
from jax import lax
from jax.experimental import pallas as pl
from jax.experimental.pallas import tpu as pltpu
from jax.experimental.pallas import tpu_sc as plsc

D_MODEL = 1024
GLA_HEADS = 4
GLA_DK = 64
GLA_DV = 128
GLA_KEY_WIDTH = GLA_HEADS * GLA_DK
GLA_WIDTH = GLA_HEADS * GLA_DV
GLA_GATE_RANK = 16
GLA_GATE_TEMP = 16.0
DIL_HEADS = 8
DIL_DH = 64
DIL_WIDTH = DIL_HEADS * DIL_DH
DIL_CONFIGS = ((128, 1), (512, 4), (2048, 16))
DIL_BLOCK = 128
DIL_MAX_R = max(r for _, r in DIL_CONFIGS)
DIL_PAD = DIL_BLOCK * DIL_MAX_R
DIL_UNROLL = 16
DIL_SPLIT_CFG = 2
N_GROUPS = 4
EXPERTS_PER_GROUP = 8
N_EXPERTS = N_GROUPS * EXPERTS_PER_GROUP
D_FF = 512
DEEPNORM_ALPHA = 2.0 ** 0.25
EPS = 1e-5
IN_PROJ_GATE_COL = 2 * GLA_KEY_WIDTH + 2 * GLA_WIDTH
LOG2E = math.log2(math.e)

LANES = 128
SUBLANES = 8
assert DIL_SPLIT_CFG == len(DIL_CONFIGS) - 1 and DIL_CONFIGS[DIL_SPLIT_CFG][1] % (2 * SUBLANES) == 0
GLA_CHUNK = 128
GLA_SUB = 64
GLA_UNROLL = 4
SC_INDEX_WINDOW = 128
SC_GATHER_BUFFERS = 6
SC_GATHER_ROWS = 32
FFN_BUFFERS = 12
FFN_GROUPS = (4, 2)
ROW_BLOCK = 256
VMEM_LIMIT = 56 * 1024 * 1024

F32 = jnp.float32
BF16 = jnp.bfloat16


def _dot(a, b):
    return jnp.dot(a, b, preferred_element_type=F32)


def _dot_nt(a, b):
    return lax.dot_general(a, b, (((1,), (1,)), ((), ())), preferred_element_type=F32)


def _dot_tn(a, b):
    return lax.dot_general(a, b, (((0,), (0,)), ((), ())), preferred_element_type=F32)


def _split_bf16(v):
    hi = v.astype(BF16)
    lo = (v - hi.astype(F32)).astype(BF16)
    return hi, lo


def _pack_bf16_halves(v):
    w = v.shape[1] // 2
    hi = lax.bitcast_convert_type(v[:, :w].astype(BF16).astype(F32), jnp.int32)
    lo = lax.bitcast_convert_type(v[:, w:].astype(BF16).astype(F32), jnp.int32)
    return hi | lax.shift_right_logical(lo, 16)


def _unpack_bf16_halves(words):
    hi = lax.bitcast_convert_type(words & jnp.int32(-65536), F32)
    lo = lax.bitcast_convert_type(lax.shift_left(words, 16), F32)
    return hi, lo


def _layer_norm(v, g, b):
    mu = jnp.mean(v, axis=-1, keepdims=True)
    c = v - mu
    var = jnp.mean(c * c, axis=-1, keepdims=True)
    return c * lax.rsqrt(var + EPS) * g + b


def _in_proj_kernel(x_ref, w_ref, w2h_ref, w2l_ref, gb_ref,
                    q_ref, k_ref, v_ref, r_ref, la_ref, dq_ref, dk_ref, dv_ref, wg_s, wd_s, wa_s):
    a0 = IN_PROJ_GATE_COL

    @pl.when(pl.program_id(0) == 0)
    def _():
        wg_s[...] = w_ref[0:a0, :].T.astype(BF16)
        gate_tile = w_ref[a0:a0 + LANES, :].T
        lane = lax.broadcasted_iota(jnp.int32, gate_tile.shape, 1)
        wa_s[...] = jnp.where(lane < GLA_GATE_RANK, gate_tile, 0.0).astype(BF16)
        wd_s[...] = w_ref[a0 + GLA_GATE_RANK:, :].T.astype(BF16)

    xb = x_ref[...].astype(BF16)

    def piece(w_s, c0, c1):
        return _dot(xb, w_s[:, c0:c1])

    q_ref[...] = (piece(wg_s, 0, 256) * (GLA_DK ** -0.5)).astype(BF16)
    k_ref[...] = piece(wg_s, 256, 512).astype(BF16)
    v_ref[...] = piece(wg_s, 512, 1024).astype(BF16)
    r_ref[...] = piece(wg_s, 1024, 1536).astype(BF16)
    dq_ref[...] = (piece(wd_s, 0, 512) * (DIL_DH ** -0.5 * LOG2E)).astype(BF16)
    dk_ref[...] = piece(wd_s, 512, 1024).astype(BF16)
    dv_ref[...] = piece(wd_s, 1024, 1536).astype(BF16)
    ga = _dot(xb, wa_s[...])
    ga_hi, ga_lo = _split_bf16(ga)
    z = _dot(ga_hi, w2h_ref[...]) + _dot(ga_lo, w2h_ref[...]) + _dot(ga_hi, w2l_ref[...]) + gb_ref[...]
    log_sig = jnp.minimum(z, 0.0) - jnp.log1p(jnp.exp(-jnp.abs(z)))
    la_ref[...] = log_sig * (1.0 / GLA_GATE_TEMP)


def _in_proj(x2, w_in_t, layer, w2h, w2l, gb, tm=1024):
    T = x2.shape[0]
    row = lambda wd: pl.BlockSpec((tm, wd), lambda i: (i, 0))
    full = lambda a: pl.BlockSpec(a.shape, lambda i: (0,) * a.ndim)
    outs = [(GLA_KEY_WIDTH, BF16), (GLA_KEY_WIDTH, BF16), (GLA_WIDTH, BF16), (GLA_WIDTH, BF16), (GLA_KEY_WIDTH, F32),
            (DIL_WIDTH, BF16), (DIL_WIDTH, BF16), (DIL_WIDTH, BF16)]
    group = w_in_t.shape[1] - IN_PROJ_GATE_COL - GLA_GATE_RANK
    return pl.pallas_call(
        _in_proj_kernel,
        grid=(T // tm,),
        in_specs=[row(D_MODEL),
                  pl.BlockSpec((None,) + w_in_t.shape[1:], lambda i: (layer, 0, 0), pipeline_mode=pl.Buffered(1)),
                  full(w2h), full(w2l), full(gb)],
        out_specs=[row(wd) for wd, _ in outs],
        out_shape=[jax.ShapeDtypeStruct((T, wd), dt) for wd, dt in outs],
        scratch_shapes=[pltpu.VMEM((D_MODEL, IN_PROJ_GATE_COL), BF16), pltpu.VMEM((D_MODEL, group), BF16),
                        pltpu.VMEM((D_MODEL, LANES), BF16)],
        compiler_params=pltpu.CompilerParams(
            dimension_semantics=("arbitrary",), vmem_limit_bytes=VMEM_LIMIT),
        name="in_proj",
    )(x2, w_in_t, w2h, w2l, gb)


def _gla_kernel(q_ref, k_ref, v_ref, r_ref, la_ref, g_ref, o_ref, s_ref, *, seq_block):
    C = GLA_CHUNK
    H = GLA_SUB
    assert C == 2 * H

    @pl.when(pl.program_id(1) == 0)
    def _():
        s_ref[...] = jnp.zeros_like(s_ref)

    ri = lax.broadcasted_iota(jnp.int32, (C, C), 0)
    ci = lax.broadcasted_iota(jnp.int32, (C, C), 1)
    same_sub = (ri // H) == (ci // H)
    sum_ops = jnp.concatenate([jnp.logical_and(same_sub, ci <= ri).astype(BF16), same_sub.astype(BF16),
                               jnp.ones((C, C), BF16)], axis=0)
    diag_mask = jnp.logical_and(same_sub, ci <= ri)
    off_mask = (ri // H) > (ci // H)
    second = lax.broadcasted_iota(jnp.int32, (C, 1), 0) >= H
    ones_cl = jnp.ones((C, LANES), BF16)
    lane_k = lax.broadcasted_iota(jnp.int32, (1, GLA_KEY_WIDTH), 1) // GLA_DK
    head_masks = [(lane_k == h).astype(F32) for h in range(GLA_HEADS)]
    srow = lax.broadcasted_iota(jnp.int32, (GLA_KEY_WIDTH, GLA_WIDTH), 0) // GLA_DK
    scol = lax.broadcasted_iota(jnp.int32, (GLA_KEY_WIDTH, GLA_WIDTH), 1) // GLA_DV
    state_mask = (srow == scol).astype(F32)
    g = g_ref[...]

    def trip(t, carry):
        U = GLA_UNROLL
        rows = [pl.ds(pl.multiple_of((t * U + u) * C, C), C) for u in range(U)]
        la2s, sums = [], []
        for u in range(U):
            la_hi, la_lo = _split_bf16(la_ref[rows[u], :])
            la2 = jnp.concatenate([la_hi, la_lo], axis=1)
            la2s.append(la2)
            sm = _dot(sum_ops, la2)
            sums.append(sm[:, 0:GLA_KEY_WIDTH] + sm[:, GLA_KEY_WIDTH:])
        q_states, k_states, scores = [], [], []
        for u in range(U):
            b = sums[u][0:C]
            t_sub = sums[u][C:2 * C]
            other = sums[u][2 * C:3 * C] - t_sub
            q = q_ref[rows[u], :].astype(F32)
            k = k_ref[rows[u], :].astype(F32)
            qd = q * jnp.exp(b)
            kd = (k * jnp.exp(-b)).astype(BF16)
            ke_f = k * jnp.exp(t_sub - b)
            cross = jnp.exp(other)
            q_states.append((qd * jnp.where(second, cross, 1.0)).astype(BF16))
            k_states.append((ke_f * jnp.where(second, 1.0, cross)).astype(BF16))
            q_heads = jnp.concatenate([(qd * head_masks[h]).astype(BF16) for h in range(GLA_HEADS)], axis=0)
            keys2 = jnp.concatenate([kd, ke_f.astype(BF16)], axis=0)
            scores.append(_dot_nt(q_heads, keys2))
        decs, upds = [], []
        for u in range(U):
            tot = _dot_tn(la2s[u], ones_cl)
            decs.append(jnp.exp(tot[0:GLA_KEY_WIDTH] + tot[GLA_KEY_WIDTH:]))
            upds.append(_dot_tn(k_states[u], v_ref[rows[u], :]) * state_mask)
        o_inters = []
        for u in range(U):
            state = s_ref[...]
            o_inters.append(_dot(q_states[u], state.astype(BF16)))
            for h in range(GLA_HEADS):
                cols = slice(h * GLA_DV, (h + 1) * GLA_DV)
                s_ref[:, cols] = state[:, cols] * decs[u] + upds[u][:, cols]
        for u in range(U):
            v = v_ref[rows[u], :]
            outs = []
            for h in range(GLA_HEADS):
                sh = scores[u][h * C:(h + 1) * C]
                a = jnp.where(diag_mask, sh[:, 0:C], 0.0) + jnp.where(off_mask, sh[:, C:2 * C], 0.0)
                cols = slice(h * GLA_DV, (h + 1) * GLA_DV)
                o = _dot(a.astype(BF16), v[:, cols]) + o_inters[u][:, cols]
                o = o * lax.rsqrt(jnp.mean(o * o, axis=-1, keepdims=True) + EPS) * g
                outs.append(o)
            o_all = jnp.concatenate(outs, axis=-1)
            rr = r_ref[rows[u], :].astype(F32)
            o_ref[rows[u], :] = (o_all * (rr * jax.nn.sigmoid(rr))).astype(BF16)
        return carry

    lax.fori_loop(0, seq_block // C // GLA_UNROLL, trip, 0)


def _gla(q, k, v, r, la, g, batch, seq, seq_block=1024):
    nsb = seq // seq_block
    row = lambda w: pl.BlockSpec((seq_block, w), lambda b, s: (b * nsb + s, 0))
    return pl.pallas_call(
        functools.partial(_gla_kernel, seq_block=seq_block),
        grid=(batch, nsb),
        in_specs=[row(GLA_KEY_WIDTH), row(GLA_KEY_WIDTH), row(GLA_WIDTH), row(GLA_WIDTH), row(GLA_KEY_WIDTH),
                  pl.BlockSpec((1, GLA_DV), lambda b, s: (0, 0))],
        out_specs=row(GLA_WIDTH),
        out_shape=jax.ShapeDtypeStruct((batch * seq, GLA_WIDTH), BF16),
        scratch_shapes=[pltpu.VMEM((GLA_KEY_WIDTH, GLA_WIDTH), F32)],
        compiler_params=pltpu.CompilerParams(
            dimension_semantics=("parallel", "arbitrary"), vmem_limit_bytes=VMEM_LIMIT),
        name="gla",
    )(q, k, v, r, la, g)


def _dil_kernel(slope_ref, q_ref, k_ref, v_ref, g_ref, o_ref,
                qf, kf, vf, qs, ks, vs, kd, va, vb, oc, lc, ocs, lcs, *, seq):
    B = DIL_BLOCK
    U = DIL_UNROLL
    pair = pl.program_id(1)
    for src, nat, split in ((q_ref, qf, qs), (k_ref, kf, ks), (v_ref, vf, vs)):
        x = src[...].astype(F32)
        nat[...] = x
        x3 = x.reshape(seq // (2 * SUBLANES), 2 * SUBLANES, LANES)
        split[0] = x3[:, :SUBLANES, :].reshape(seq // 2, LANES)
        split[1] = x3[:, SUBLANES:, :].reshape(seq // 2, LANES)

    lane = lax.broadcasted_iota(jnp.int32, (1, LANES), 1)
    first = lane < DIL_DH
    ii = lax.broadcasted_iota(jnp.int32, (B, B), 0)
    jj = lax.broadcasted_iota(jnp.int32, (B, B), 1)
    upper = jj > ii
    eye = jj == ii
    dist = jnp.bitwise_and(ii - jj, B - 1).astype(F32)
    neg = jnp.float32(-jnp.inf)
    neg_tile = jnp.full((B, B), neg, F32)
    zero_tile = jnp.zeros((B, LANES), BF16)

    def split_rows(r, c, n):
        half = r // 2
        start = c % SUBLANES + c // (2 * SUBLANES) * SUBLANES + n * (B * half)
        return (c // SUBLANES) % 2, pl.ds(start, B, stride=half)

    def class_rows(nat, split, cfg, r, c, n):
        if r == 1:
            return nat[pl.ds(pl.multiple_of(n * B, B), B), :]
        if cfg == DIL_SPLIT_CFG:
            parity, rows = split_rows(r, c, n)
            return split[parity, rows, :]
        return nat[pl.ds(c + n * (B * r), B, stride=r), :]

    for cfg, (window, r) in enumerate(DIL_CONFIGS):
        nb = seq // r // B
        cs = nb + 1
        bias_prev, bias_cur = [], []
        for hh in range(2):
            slope = slope_ref[2 * pair + hh] * (float(r) * LOG2E)
            bias = dist * (-slope)
            bias_prev.append(jnp.where(upper, bias, jnp.where(eye, -slope * float(B), neg)))
            bias_cur.append(jnp.where(upper, neg, bias))

        for c in range(r):
            rows0 = slice(c * cs * B, (c * cs + 1) * B)
            kd[rows0, :] = zero_tile
            va[rows0, :] = zero_tile
            vb[rows0, :] = zero_tile

        def prep(t4, carry, cfg=cfg, r=r, nb=nb, cs=cs):
            for j in range(4):
                t = t4 * 4 + j
                c = t // nb
                n = t % nb
                dst = pl.ds(pl.multiple_of((c * cs + 1 + n) * B, B), B)
                kd[dst, :] = class_rows(kf, ks, cfg, r, c, n).astype(BF16)
                v = class_rows(vf, vs, cfg, r, c, n)
                va[dst, :] = jnp.where(first, v, 1.0).astype(BF16)
                vb[dst, :] = jnp.where(first, 1.0, v).astype(BF16)
            return carry

        lax.fori_loop(0, seq // B // 4, prep, 0)

        def geom(it, r=r, nb=nb, cs=cs):
            c = it // nb
            n = it % nb
            start = c + n * (B * r)
            rows = pl.ds(start, B, stride=r) if r > 1 else pl.ds(pl.multiple_of(start, B), B)
            kv = pl.ds(pl.multiple_of((c * cs + n) * B, B), 2 * B)
            return rows, kv, c, n

        def body(t, carry, cfg=cfg, r=r, nb=nb, bias_prev=bias_prev, bias_cur=bias_cur, geom=geom):
            geo = [geom(t * U + u) for u in range(U)]
            scores = []
            for u in range(U):
                _, kv, c, n = geo[u]
                q = class_rows(qf, qs, cfg, r, c, n)
                q_heads = jnp.concatenate([jnp.where(first, q, 0.0), jnp.where(first, 0.0, q)], axis=0)
                scores.append(_dot_nt(q_heads.astype(BF16), kd[kv, :]))
            probs, maxes = [], []
            for u in range(U):
                if nb % U == 0:
                    has_prev = True if u > 0 else (t * U) % nb > 0
                else:
                    assert U % nb == 0
                    has_prev = (u % nb) > 0
                for hh in range(2):
                    s2 = scores[u][hh * B:(hh + 1) * B]
                    if has_prev is True:
                        bp = bias_prev[hh]
                    elif has_prev is False:
                        bp = neg_tile
                    else:
                        bp = jnp.where(has_prev, bias_prev[hh], neg)
                    s_prev = s2[:, 0:B] + bp
                    s_cur = s2[:, B:2 * B] + bias_cur[hh]
                    m = jnp.max(jnp.maximum(s_prev, s_cur), axis=-1, keepdims=True)
                    probs.append(jnp.concatenate([jnp.exp2(s_prev - m), jnp.exp2(s_cur - m)], axis=1).astype(BF16))
                    maxes.append(m)
            for u in range(U):
                rows, kv, c, n = geo[u]
                acc0 = _dot(probs[2 * u], va[kv, :])
                acc1 = _dot(probs[2 * u + 1], vb[kv, :])
                num = jnp.where(first, acc0, acc1)
                den = pltpu.roll(jnp.where(first, acc1, acc0), DIL_DH, axis=1)
                out = num * (1.0 / den)
                lse = jnp.where(first, maxes[2 * u], maxes[2 * u + 1]) + jnp.log2(den)
                if cfg == DIL_SPLIT_CFG:
                    parity, dst = split_rows(r, c, n)
                    ocs[parity, dst, :] = out
                    lcs[parity, dst, :] = lse
                else:
                    oc[cfg, rows, :] = out
                    lc[cfg, rows, :] = lse
            return carry

        lax.fori_loop(0, seq // B // U, body, 0)

    g = g_ref[...]
    CH = 512

    def mix(i, carry):
        rows = pl.ds(pl.multiple_of(i * CH, CH), CH)
        half_rows = pl.ds(pl.multiple_of(i * (CH // 2), CH // 2), CH // 2)

        def tiles_interleaved(split):
            even = split[0, half_rows, :].reshape(CH // (2 * SUBLANES), 1, SUBLANES, LANES)
            odd = split[1, half_rows, :].reshape(CH // (2 * SUBLANES), 1, SUBLANES, LANES)
            return jnp.concatenate([even, odd], axis=1).reshape(CH, LANES)

        l0, l1, l2 = lc[0, rows, :], lc[1, rows, :], tiles_interleaved(lcs)
        m = jnp.maximum(jnp.maximum(l0, l1), l2)
        e0, e1, e2 = jnp.exp2(l0 - m), jnp.exp2(l1 - m), jnp.exp2(l2 - m)
        inv = 1.0 / (e0 + e1 + e2)
        o = (e0 * oc[0, rows, :] + e1 * oc[1, rows, :] + e2 * tiles_interleaved(ocs)) * inv
        sq = o * o
        ms_a = jnp.sum(jnp.where(first, sq, 0.0), axis=-1, keepdims=True) * (1.0 / DIL_DH)
        ms_b = jnp.sum(jnp.where(first, 0.0, sq), axis=-1, keepdims=True) * (1.0 / DIL_DH)
        ms = jnp.where(first, ms_a, ms_b)
        o_ref[rows, :] = (o * lax.rsqrt(ms + EPS) * g).astype(BF16)
        return carry

    lax.fori_loop(0, seq // CH, mix, 0)


def _dilated(slopes, dq, dk, dv, g2, batch, seq):
    blk = pl.BlockSpec((seq, LANES), lambda b, p, s: (b, p))
    return pl.pallas_call(
        functools.partial(_dil_kernel, seq=seq),
        grid_spec=pltpu.PrefetchScalarGridSpec(
            num_scalar_prefetch=1,
            grid=(batch, DIL_WIDTH // LANES),
            in_specs=[blk, blk, blk, pl.BlockSpec((1, LANES), lambda b, p, s: (0, 0))],
            out_specs=blk,
            scratch_shapes=[pltpu.VMEM((seq, LANES), F32)] * 3 + [pltpu.VMEM((2, seq // 2, LANES), F32)] * 3
                           + [pltpu.VMEM((seq + DIL_PAD, LANES), BF16)] * 3
                           + [pltpu.VMEM((2, seq, LANES), F32)] * 2 + [pltpu.VMEM((2, seq // 2, LANES), F32)] * 2,
        ),
        out_shape=jax.ShapeDtypeStruct((batch * seq, DIL_WIDTH), BF16),
        compiler_params=pltpu.CompilerParams(
            dimension_semantics=("parallel", "parallel"), vmem_limit_bytes=VMEM_LIMIT),
        name="dilated",
    )(slopes, dq, dk, dv, g2)


def _out_proj_kernel(og_ref, od_ref, x_ref, wg_ref, wd_ref, g_ref, b_ref,
                     rw_ref, rb_ref, h_ref, hp_ref, eid_ref, gate_ref, cnt_ref, wg_s, wd_s):
    @pl.when(pl.program_id(0) == 0)
    def _():
        wg_s[...] = wg_ref[...].astype(BF16)
        wd_s[...] = wd_ref[...].astype(BF16)

    mix = _dot(og_ref[...], wg_s[...]) + _dot(od_ref[...], wd_s[...])
    h = _layer_norm(DEEPNORM_ALPHA * x_ref[...] + mix, g_ref[...], b_ref[...])
    h_ref[...] = h
    hp_ref[...] = _pack_bf16_halves(h)
    lt = _dot_nt(rw_ref[...], h) + rb_ref[...]
    tm = lt.shape[1]
    row = lax.broadcasted_iota(jnp.int32, (EXPERTS_PER_GROUP, tm), 0).astype(F32)
    neg = jnp.float32(-jnp.inf)
    big = jnp.float32(1e9)
    coarse = jnp.where(row < N_GROUPS, lt[N_EXPERTS:N_EXPERTS + EXPERTS_PER_GROUP, :], neg)
    cmax = jnp.max(coarse, axis=0, keepdims=True)
    g_idx = jnp.min(jnp.where(coarse == cmax, row, big), axis=0, keepdims=True)
    p_group = 1.0 / jnp.sum(jnp.exp(coarse - cmax), axis=0, keepdims=True)
    fine = lt[(N_GROUPS - 1) * EXPERTS_PER_GROUP:N_EXPERTS, :]
    for g in range(N_GROUPS - 2, -1, -1):
        fine = jnp.where(g_idx == g, lt[g * EXPERTS_PER_GROUP:(g + 1) * EXPERTS_PER_GROUP, :], fine)
    v1 = jnp.max(fine, axis=0, keepdims=True)
    i1 = jnp.min(jnp.where(fine == v1, row, big), axis=0, keepdims=True)
    fine2 = jnp.where(row == i1, neg, fine)
    v2 = jnp.max(fine2, axis=0, keepdims=True)
    i2 = jnp.min(jnp.where(fine2 == v2, row, big), axis=0, keepdims=True)
    e2 = jnp.exp(v2 - v1)
    den = 1.0 + e2
    gate1 = p_group * (1.0 / den)
    gate2 = p_group * (e2 / den)
    id1 = g_idx * EXPERTS_PER_GROUP + i1
    id2 = g_idx * EXPERTS_PER_GROUP + i2
    eid_ref[...] = jnp.concatenate([id1, id2], axis=0).astype(jnp.int32)
    slab = jnp.concatenate([gate1, gate2, jnp.zeros((LANES - 2, tm), F32)], axis=0)
    gate_ref[...] = slab.T[:, 0:2]
    sub = lax.broadcasted_iota(jnp.int32, (LANES, tm), 0).astype(F32)
    onehot = jnp.logical_or(sub == id1, sub == id2).astype(BF16)

    @pl.when(pl.program_id(0) == 0)
    def _():
        cnt_ref[...] = jnp.zeros_like(cnt_ref)

    cnt_ref[...] += _dot(onehot, jnp.ones((tm, LANES), BF16))


def _out_proj(og, od, x2, w_out, layer, g, b, rw, rb, tm=1024):
    T = x2.shape[0]
    row = lambda w: pl.BlockSpec((tm, w), lambda i: (i, 0))
    full = lambda a: pl.BlockSpec(a.shape, lambda i: (0,) * a.ndim)
    half = lambda p: pl.BlockSpec((None, GLA_WIDTH, D_MODEL), lambda i: (layer, p, 0))
    return pl.pallas_call(
        _out_proj_kernel,
        grid=(T // tm,),
        in_specs=[row(GLA_WIDTH), row(DIL_WIDTH), row(D_MODEL), half(0), half(1), full(g), full(b),
                  full(rw), full(rb)],
        out_specs=[row(D_MODEL), row(D_MODEL // 2), pl.BlockSpec((2, tm), lambda i: (0, i)), row(2),
                   pl.BlockSpec((LANES, LANES), lambda i: (0, 0))],
        out_shape=[jax.ShapeDtypeStruct((T, D_MODEL), F32),
                   jax.ShapeDtypeStruct((T, D_MODEL // 2), jnp.int32),
                   jax.ShapeDtypeStruct((2, T), jnp.int32),
                   jax.ShapeDtypeStruct((T, 2), F32),
                   jax.ShapeDtypeStruct((LANES, LANES), F32)],
        scratch_shapes=[pltpu.VMEM((GLA_WIDTH, D_MODEL), BF16), pltpu.VMEM((DIL_WIDTH, D_MODEL), BF16)],
        compiler_params=pltpu.CompilerParams(
            dimension_semantics=("arbitrary",), vmem_limit_bytes=VMEM_LIMIT),
        name="out_proj_router",
    )(og, od, x2, w_out, w_out, g, b, rw, rb)


def _positions_kernel(eid_ref, cnt_ref, dest_ref, be_ref, nv_ref, carry_ref, sp_ref, tri_ref, *, tb):
    i = pl.program_id(0)

    @pl.when(i == 0)
    def _():
        shift = int(math.log2(ROW_BLOCK))
        nb_col = (cnt_ref[...].astype(jnp.int32) + (ROW_BLOCK - 1)) >> shift
        r = lax.broadcasted_iota(jnp.int32, (LANES, LANES), 0)
        c = lax.broadcasted_iota(jnp.int32, (LANES, LANES), 1)
        nb_f = jnp.where(r < N_EXPERTS, nb_col, 0).astype(F32)
        start_col = _dot((c < r).astype(BF16), nb_f.astype(BF16))
        sp_ref[...] = start_col * float(ROW_BLOCK)
        carry_ref[...] = jnp.zeros_like(carry_ref)
        be_ref[...] = jnp.concatenate([start_col.T[0:1, :], nb_f.T[0:1, :]], axis=1).astype(jnp.int32)
        total = jnp.sum(nb_f[:, 0:1], axis=0, keepdims=True)
        nv_ref[...] = jnp.broadcast_to(total, (1, LANES)).astype(jnp.int32)
        tr = lax.broadcasted_iota(jnp.int32, (tb, tb), 0)
        tc = lax.broadcasted_iota(jnp.int32, (tb, tb), 1)
        tri_ref[...] = (tr < tc).astype(BF16)

    sub = lax.broadcasted_iota(jnp.int32, (LANES, tb), 0)
    oh1 = sub == eid_ref[0:1, :]
    oh2 = sub == eid_ref[1:2, :]
    oh = jnp.logical_or(oh1, oh2).astype(BF16)
    offset = jnp.tile(carry_ref[...] + sp_ref[...], (1, tb // LANES))
    before = _dot(oh, tri_ref[...]) + offset
    d1 = jnp.sum(jnp.where(oh1, before, 0.0), axis=0, keepdims=True)
    d2 = jnp.sum(jnp.where(oh2, before, 0.0), axis=0, keepdims=True)
    dest_ref[...] = jnp.concatenate([d1, d2], axis=0).astype(jnp.int32)
    carry_ref[...] += _dot(oh, jnp.ones((tb, LANES), BF16))


def _positions(eid_t, cnt, n_blocks_pad, tb=1024):
    T = eid_t.shape[1]
    return pl.pallas_call(
        functools.partial(_positions_kernel, tb=tb),
        grid=(T // tb,),
        in_specs=[pl.BlockSpec((2, tb), lambda i: (0, i)), pl.BlockSpec((LANES, LANES), lambda i: (0, 0))],
        out_specs=[pl.BlockSpec((2, tb), lambda i: (0, i)),
                   pl.BlockSpec((1, n_blocks_pad), lambda i: (0, 0)),
                   pl.BlockSpec((1, LANES), lambda i: (0, 0))],
        out_shape=[jax.ShapeDtypeStruct((2, T), jnp.int32),
                   jax.ShapeDtypeStruct((1, n_blocks_pad), jnp.int32),
                   jax.ShapeDtypeStruct((1, LANES), jnp.int32)],
        scratch_shapes=[pltpu.VMEM((LANES, LANES), F32), pltpu.VMEM((LANES, LANES), F32),
                        pltpu.VMEM((tb, tb), BF16)],
        compiler_params=pltpu.CompilerParams(dimension_semantics=("arbitrary",)),
        name="positions",
    )(eid_t, cnt)


def _sc_gather_rows(table, idx):
    n = idx.shape[0]
    d = table.shape[1]
    info = plsc.get_sparse_core_info()
    nc, ns = info.num_cores, info.num_subcores
    per_w = n // (nc * ns)
    assert per_w * nc * ns == n and per_w % SC_INDEX_WINDOW == 0
    mesh = plsc.VectorSubcoreMesh(core_axis_name="core", subcore_axis_name="subcore")
    nchunk = per_w // SC_GATHER_ROWS
    nbuf = SC_GATHER_BUFFERS

    @functools.partial(
        pl.kernel, out_type=jax.ShapeDtypeStruct((n, d), table.dtype), mesh=mesh,
        scratch_types=[pltpu.VMEM((per_w,), jnp.int32),
                       pltpu.VMEM((nbuf, SC_GATHER_ROWS, d), table.dtype),
                       pltpu.SemaphoreType.DMA((nbuf,)), pltpu.SemaphoreType.DMA((nbuf,))],
        name="sc_gather_rows")
    def gather(x_hbm, i_hbm, o_hbm, idx_v, buf, gsem, wsem):
        wid = lax.axis_index("subcore") * nc + lax.axis_index("core")
        base = wid * per_w
        pltpu.sync_copy(i_hbm.at[pl.ds(base, per_w)], idx_v)

        def gather_copy(c):
            rows = idx_v.at[pl.ds(c * SC_GATHER_ROWS, SC_GATHER_ROWS)]
            return pltpu.make_async_copy(x_hbm.at[rows], buf.at[c % nbuf], gsem.at[c % nbuf])

        def write_copy(c):
            dst = o_hbm.at[pl.ds(base + c * SC_GATHER_ROWS, SC_GATHER_ROWS)]
            return pltpu.make_async_copy(buf.at[c % nbuf], dst, wsem.at[c % nbuf])

        for c in range(min(nbuf - 1, nchunk)):
            gather_copy(c).start()
        for c in range(nchunk):
            gather_copy(c).wait()
            write_copy(c).start()
            if c + nbuf - 1 < nchunk:
                if c >= 1:
                    write_copy(c - 1).wait()
                gather_copy(c + nbuf - 1).start()
        for c in range(max(0, nchunk - nbuf), nchunk):
            write_copy(c).wait()

    return gather(table, idx)


def _sc_scatter_rows(table, dest_flat, n_rows):
    n_tok, d = table.shape
    assert dest_flat.shape[0] == 2 * n_tok
    info = plsc.get_sparse_core_info()
    nc, ns = info.num_cores, info.num_subcores
    per_w = n_tok // (nc * ns)
    assert per_w * nc * ns == n_tok and per_w % SC_INDEX_WINDOW == 0
    mesh = plsc.VectorSubcoreMesh(core_axis_name="core", subcore_axis_name="subcore")
    nchunk = per_w // SC_GATHER_ROWS
    nbuf = SC_GATHER_BUFFERS

    @functools.partial(
        pl.kernel, out_type=jax.ShapeDtypeStruct((n_rows, d), table.dtype), mesh=mesh,
        scratch_types=[pltpu.VMEM((per_w,), jnp.int32), pltpu.VMEM((per_w,), jnp.int32),
                       pltpu.VMEM((nbuf, SC_GATHER_ROWS, d), table.dtype),
                       pltpu.SemaphoreType.DMA((nbuf,)), pltpu.SemaphoreType.DMA((nbuf, 2))],
        name="sc_scatter_rows")
    def scatter(x_hbm, i_hbm, o_hbm, idx_a, idx_b, buf, rsem, wsem):
        wid = lax.axis_index("subcore") * nc + lax.axis_index("core")
        base = wid * per_w
        pltpu.sync_copy(i_hbm.at[pl.ds(base, per_w)], idx_a)
        pltpu.sync_copy(i_hbm.at[pl.ds(n_tok + base, per_w)], idx_b)

        def read_copy(c):
            src = x_hbm.at[pl.ds(base + c * SC_GATHER_ROWS, SC_GATHER_ROWS)]
            return pltpu.make_async_copy(src, buf.at[c % nbuf], rsem.at[c % nbuf])

        def scatter_copies(c):
            window = pl.ds(c * SC_GATHER_ROWS, SC_GATHER_ROWS)
            return [pltpu.make_async_copy(buf.at[c % nbuf], o_hbm.at[idx.at[window]], wsem.at[c % nbuf, k])
                    for k, idx in enumerate((idx_a, idx_b))]

        for c in range(min(nbuf - 1, nchunk)):
            read_copy(c).start()
        for c in range(nchunk):
            read_copy(c).wait()
            for cp in scatter_copies(c):
                cp.start()
            if c + nbuf - 1 < nchunk:
                if c >= 1:
                    for cp in scatter_copies(c - 1):
                        cp.wait()
                read_copy(c + nbuf - 1).start()
        for c in range(max(0, nchunk - nbuf), nchunk):
            for cp in scatter_copies(c):
                cp.wait()

    return scatter(table, dest_flat)


def _ffn_kernel(first_ref, count_ref, nv_ref, wg_hbm, wu_hbm, wd_hbm, xs_hbm, y_hbm,
                wg32, wu32, wd32, xbuf, ybuf, wsem, isem, osem, *, n_blocks):
    nv = nv_ref[0]
    nbuf = FFN_BUFFERS

    def next_expert(e):
        def more(t):
            return jnp.logical_and(t < N_EXPERTS, count_ref[jnp.minimum(t, N_EXPERTS - 1)] == 0)
        return lax.while_loop(more, lambda t: t + 1, e + 1)

    def weight_copies(e, slot):
        ee = jnp.minimum(e, N_EXPERTS - 1)
        return [pltpu.make_async_copy(src.at[ee], dst.at[slot], wsem.at[slot, i])
                for i, (src, dst) in enumerate(((wg_hbm, wg32), (wu_hbm, wu32), (wd_hbm, wd32)))]

    def fetch_weights(e, slot):
        @pl.when(e < N_EXPERTS)
        def _():
            for c in weight_copies(e, slot):
                c.start()

    def take_weights(e, slot):
        for c in weight_copies(e, slot):
            c.wait()
        fetch_weights(next_expert(e), 1 - slot)

    def rows_of(b):
        return pl.ds(pl.multiple_of(b * ROW_BLOCK, ROW_BLOCK), ROW_BLOCK)

    def buf_rows(b, nblk):
        return pl.ds(pl.multiple_of((b % nbuf) * ROW_BLOCK, ROW_BLOCK), nblk * ROW_BLOCK)

    def in_copy(b):
        return pltpu.make_async_copy(xs_hbm.at[rows_of(b)], xbuf.at[buf_rows(b, 1)], isem.at[b % nbuf])

    def out_copy(b):
        return pltpu.make_async_copy(ybuf.at[buf_rows(b, 1)], y_hbm.at[rows_of(b)], osem.at[b % nbuf])

    def expert_mlp(words, slot):
        x = jnp.concatenate(_unpack_bf16_halves(words), axis=1)
        a = _dot(x, wg32[slot])
        u = _dot(x, wu32[slot])
        return _pack_bf16_halves(_dot(a * jax.nn.sigmoid(a) * u, wd32[slot]))

    ahead = nbuf - max(FFN_GROUPS)

    @pl.when(nv > 0)
    def _():
        e0 = next_expert(jnp.int32(-1))
        for i in range(ahead):
            @pl.when(i < nv)
            def _():
                in_copy(i).start()
        fetch_weights(e0, 0)
        take_weights(e0, 0)

        def step(carry):
            b, e, k = carry
            switch = b >= first_ref[e] + count_ref[e]
            e_new = jnp.where(switch, next_expert(e), e)
            k_new = jnp.where(switch, k + 1, k)

            @pl.when(switch)
            def _():
                take_weights(e_new, k_new % 2)

            end = first_ref[e_new] + count_ref[e_new]
            n = jnp.int32(1)
            for size in FFN_GROUPS:
                fits = jnp.logical_and(b + size <= end, b % nbuf + size <= nbuf)
                n = jnp.where(jnp.logical_and(n == 1, fits), size, n)
            for i in range(max(FFN_GROUPS)):
                @pl.when(jnp.logical_and(i < n, b + ahead + i < nv))
                def _():
                    in_copy(b + ahead + i).start()
            for i in range(max(FFN_GROUPS)):
                @pl.when(i < n)
                def _():
                    in_copy(b + i).wait()

                    @pl.when(b + i >= nbuf)
                    def _():
                        out_copy(b + i - nbuf).wait()

            for size in FFN_GROUPS + (1,):
                @pl.when(n == size)
                def _():
                    ybuf[buf_rows(b, size), :] = expert_mlp(xbuf[buf_rows(b, size), :], k_new % 2)

            for i in range(max(FFN_GROUPS)):
                @pl.when(i < n)
                def _():
                    out_copy(b + i).start()
            return b + n, e_new, k_new

        lax.while_loop(lambda c: c[0] < nv, step, (jnp.int32(0), e0, jnp.int32(0)))

        for i in range(nbuf):
            @pl.when(nv > i)
            def _():
                out_copy(nv - 1 - i).wait()

    ybuf[0:ROW_BLOCK, :] = jnp.zeros((ROW_BLOCK, D_MODEL // 2), jnp.int32)

    def fill(b, carry):
        pltpu.sync_copy(ybuf.at[pl.ds(0, ROW_BLOCK)], y_hbm.at[rows_of(b)])
        return carry

    lax.fori_loop(nv, n_blocks, fill, 0)


def _ffn(first_blk, n_blk, nv, xs, w_gate, w_up, w_down):
    n_rows = xs.shape[0]
    n_blocks = n_rows // ROW_BLOCK
    anyspec = pl.BlockSpec(memory_space=pl.ANY)
    return pl.pallas_call(
        functools.partial(_ffn_kernel, n_blocks=n_blocks),
        grid_spec=pltpu.PrefetchScalarGridSpec(
            num_scalar_prefetch=3,
            grid=(1,),
            in_specs=[anyspec, anyspec, anyspec, anyspec],
            out_specs=anyspec,
            scratch_shapes=[pltpu.VMEM((2, D_MODEL, D_FF), F32), pltpu.VMEM((2, D_MODEL, D_FF), F32),
                            pltpu.VMEM((2, D_FF, D_MODEL), F32),
                            pltpu.VMEM((FFN_BUFFERS * ROW_BLOCK, D_MODEL // 2), jnp.int32),
                            pltpu.VMEM((FFN_BUFFERS * ROW_BLOCK, D_MODEL // 2), jnp.int32),
                            pltpu.SemaphoreType.DMA((2, 3)),
                            pltpu.SemaphoreType.DMA((FFN_BUFFERS,)),
                            pltpu.SemaphoreType.DMA((FFN_BUFFERS,))],
        ),
        out_shape=jax.ShapeDtypeStruct((n_rows, D_MODEL // 2), jnp.int32),
        compiler_params=pltpu.CompilerParams(
            dimension_semantics=("arbitrary",), vmem_limit_bytes=VMEM_LIMIT),
        name="expert_ffn",
    )(first_blk, n_blk, nv, w_gate, w_up, w_down, xs)


def _combine_kernel(h_ref, ya_ref, yb_ref, gate_ref, g_ref, b_ref, o_ref):
    gate = gate_ref[...]
    a_hi, a_lo = _unpack_bf16_halves(ya_ref[...])
    b_hi, b_lo = _unpack_bf16_halves(yb_ref[...])
    g0, g1 = gate[:, 0:1], gate[:, 1:2]
    ffn = jnp.concatenate([a_hi * g0 + b_hi * g1, a_lo * g0 + b_lo * g1], axis=1)
    o_ref[...] = _layer_norm(DEEPNORM_ALPHA * h_ref[...] + ffn, g_ref[...], b_ref[...])


def _combine(h, y2, gate, g, b, tm=1024):
    T = h.shape[0]
    nt = T // tm
    return pl.pallas_call(
        _combine_kernel,
        grid=(nt,),
        in_specs=[pl.BlockSpec((tm, D_MODEL), lambda i: (i, 0)),
                  pl.BlockSpec((tm, D_MODEL // 2), lambda i: (i, 0)),
                  pl.BlockSpec((tm, D_MODEL // 2), lambda i: (i + nt, 0)),
                  pl.BlockSpec((tm, 2), lambda i: (i, 0)),
                  pl.BlockSpec((1, D_MODEL), lambda i: (0, 0)),
                  pl.BlockSpec((1, D_MODEL), lambda i: (0, 0))],
        out_specs=pl.BlockSpec((tm, D_MODEL), lambda i: (i, 0)),
        out_shape=jax.ShapeDtypeStruct((T, D_MODEL), F32),
        compiler_params=pltpu.CompilerParams(
            dimension_semantics=("parallel",), vmem_limit_bytes=VMEM_LIMIT),
        name="combine",
    )(h, y2, y2, gate, g, b)


def kernel(x, w_in, gla_gate_w2, gla_gate_b, gla_norm_g, dil_norm_g, w_out, ln1_g, ln1_b,
           router_coarse_w, router_coarse_b, router_fine_w, router_fine_b,
           expert_w_gate, expert_w_up, expert_w_down, ln2_g, ln2_b):
    B, S, D = x.shape
    T = B * S
    depth = w_in.shape[0]
    slopes = jnp.exp2(-8.0 * jnp.arange(1, DIL_HEADS + 1, dtype=F32) / DIL_HEADS)
    n_rows = 2 * T + N_EXPERTS * ROW_BLOCK
    n_blocks_pad = 2 * LANES
    h = x.reshape(T, D)
    w_in_t = jnp.swapaxes(w_in, 1, 2)
    for l in range(depth):
        w2 = jnp.pad(gla_gate_w2[l], ((0, LANES - GLA_GATE_RANK), (0, 0)))
        w2h, w2l = _split_bf16(w2)
        q, k, v, r, la, dq, dk, dv = _in_proj(h, w_in_t, l, w2h, w2l, gla_gate_b[l][None, :])
        o_gla = _gla(q, k, v, r, la, gla_norm_g[l][None, :], B, S)
        g2 = jnp.tile(dil_norm_g[l], 2)[None, :]
        o_dil = _dilated(slopes, dq, dk, dv, g2, B, S)
        rw = jnp.concatenate([router_fine_w[l].reshape(D, N_EXPERTS), router_coarse_w[l]], axis=1)
        rw = jnp.pad(rw, ((0, 0), (0, LANES - N_EXPERTS - N_GROUPS))).T
        rb = jnp.concatenate([router_fine_b[l].reshape(N_EXPERTS), router_coarse_b[l]])
        rb = jnp.pad(rb, (0, LANES - N_EXPERTS - N_GROUPS))[:, None]
        h1, h1p, eid_t, gate, cnt = _out_proj(o_gla, o_dil, h, w_out, l,
                                              ln1_g[l][None, :], ln1_b[l][None, :], rw, rb)
        dest_t, be, nv = _positions(eid_t, cnt, n_blocks_pad)
        dest_flat = dest_t.reshape(2 * T)
        xs = _sc_scatter_rows(h1p, dest_flat, n_rows)
        be = be.reshape(n_blocks_pad)
        y = _ffn(be[:N_EXPERTS], be[LANES:LANES + N_EXPERTS], nv.reshape(LANES)[:1], xs,
                 expert_w_gate[l], expert_w_up[l], expert_w_down[l])
        y2 = _sc_gather_rows(y, dest_flat)
        h = _combine(h1, y2, gate, ln2_g[l][None, :], ln2_b[l][None, :])
    return h.reshape(B, S, D)
```

```python
import functools
import math

import jax
import jax.numpy as jnp
from jax import lax
from jax.experimental import pallas as pl
from jax.experimental.pallas import tpu as pltpu
from jax.experimental.pallas import tpu_sc as plsc

D_MODEL = 1024
GLA_HEADS = 4
GLA_DK = 64
GLA_DV = 128
GLA_KEY_WIDTH = GLA_HEADS * GLA_DK
GLA_WIDTH = GLA_HEADS * GLA_DV
GLA_GATE_RANK = 16
GLA_GATE_TEMP = 16.0
DIL_HEADS = 8
DIL_DH = 64
DIL_WIDTH = DIL_HEADS * DIL_DH
DIL_CONFIGS = ((128, 1), (512, 4), (2048, 16))
DIL_BLOCK = 128
DIL_MAX_R = max(r for _, r in DIL_CONFIGS)
DIL_PAD = DIL_BLOCK * DIL_MAX_R
DIL_UNROLL = 16
DIL_SPLIT_CFG = 2
N_GROUPS = 4
EXPERTS_PER_GROUP = 8
N_EXPERTS = N_GROUPS * EXPERTS_PER_GROUP
D_FF = 512
DEEPNORM_ALPHA = 2.0 ** 0.25
EPS = 1e-5
IN_PROJ_GATE_COL = 2 * GLA_KEY_WIDTH + 2 * GLA_WIDTH
LOG2E = math.log2(math.e)

LANES = 128
SUBLANES = 8
assert DIL_SPLIT_CFG == len(DIL_CONFIGS) - 1 and DIL_CONFIGS[DIL_SPLIT_CFG][1] % (2 * SUBLANES) == 0
GLA_CHUNK = 128
GLA_SUB = 64
GLA_UNROLL = 4
SC_INDEX_WINDOW = 128
SC_GATHER_BUFFERS = 6
SC_GATHER_ROWS = 32
FFN_BUFFERS = 12
FFN_GROUPS = (4, 2)
ROW_BLOCK = 256
VMEM_LIMIT = 56 * 1024 * 1024

F32 = jnp.float32
BF16 = jnp.bfloat16


def _dot(a, b):
    return jnp.dot(a, b, preferred_element_type=F32)


def _dot_nt(a, b):
    return lax.dot_general(a, b, (((1,), (1,)), ((), ())), preferred_element_type=F32)


def _dot_tn(a, b):
    return lax.dot_general(a, b, (((0,), (0,)), ((), ())), preferred_element_type=F32)


def _split_bf16(v):
    hi = v.astype(BF16)
    lo = (v - hi.astype(F32)).astype(BF16)
    return hi, lo


def _pack_bf16_halves(v):
    w = v.shape[1] // 2
    hi = lax.bitcast_convert_type(v[:, :w].astype(BF16).astype(F32), jnp.int32)
    lo = lax.bitcast_convert_type(v[:, w:].astype(BF16).astype(F32), jnp.int32)
    return hi | lax.shift_right_logical(lo, 16)


def _unpack_bf16_halves(words):
    hi = lax.bitcast_convert_type(words & jnp.int32(-65536), F32)
    lo = lax.bitcast_convert_type(lax.shift_left(words, 16), F32)
    return hi, lo


def _layer_norm(v, g, b):
    mu = jnp.mean(v, axis=-1, keepdims=True)
    c = v - mu
    var = jnp.mean(c * c, axis=-1, keepdims=True)
    return c * lax.rsqrt(var + EPS) * g + b


def _in_proj_kernel(x_ref, w_ref, w2_ref, gb_ref,
                    q_ref, k_ref, v_ref, r_ref, la_ref, dq_ref, dk_ref, dv_ref, wg_s, wd_s, wa_s):
    a0 = IN_PROJ_GATE_COL

    @pl.when(pl.program_id(0) == 0)
    def _():
        wg_s[...] = w_ref[0:a0, :].T.astype(BF16)
        gate_tile = w_ref[a0:a0 + LANES, :].T
        lane = lax.broadcasted_iota(jnp.int32, gate_tile.shape, 1)
        wa_s[...] = jnp.where(lane < GLA_GATE_RANK, gate_tile, 0.0).astype(BF16)
        wd_s[...] = w_ref[a0 + GLA_GATE_RANK:, :].T.astype(BF16)

    xb = x_ref[...].astype(BF16)

    def piece(w_s, c0, c1):
        return _dot(xb, w_s[:, c0:c1])

    q_ref[...] = (piece(wg_s, 0, 256) * (GLA_DK ** -0.5)).astype(BF16)
    k_ref[...] = piece(wg_s, 256, 512).astype(BF16)
    v_ref[...] = piece(wg_s, 512, 1024).astype(BF16)
    r_ref[...] = piece(wg_s, 1024, 1536).astype(BF16)
    dq_ref[...] = (piece(wd_s, 0, 512) * (DIL_DH ** -0.5 * LOG2E)).astype(BF16)
    dk_ref[...] = piece(wd_s, 512, 1024).astype(BF16)
    dv_ref[...] = piece(wd_s, 1024, 1536).astype(BF16)
    ga = _dot(xb, wa_s[...])
    z = _dot(ga, w2_ref[...]) + gb_ref[...]
    log_sig = jnp.minimum(z, 0.0) - jnp.log1p(jnp.exp(-jnp.abs(z)))
    la_ref[...] = log_sig * (1.0 / GLA_GATE_TEMP)


def _in_proj(x2, w_in_t, layer, w2, gb, tm=1024):
    T = x2.shape[0]
    row = lambda wd: pl.BlockSpec((tm, wd), lambda i: (i, 0))
    full = lambda a: pl.BlockSpec(a.shape, lambda i: (0,) * a.ndim)
    outs = [(GLA_KEY_WIDTH, BF16), (GLA_KEY_WIDTH, BF16), (GLA_WIDTH, BF16), (GLA_WIDTH, BF16), (GLA_KEY_WIDTH, F32),
            (DIL_WIDTH, BF16), (DIL_WIDTH, BF16), (DIL_WIDTH, BF16)]
    group = w_in_t.shape[1] - IN_PROJ_GATE_COL - GLA_GATE_RANK
    return pl.pallas_call(
        _in_proj_kernel,
        grid=(T // tm,),
        in_specs=[row(D_MODEL),
                  pl.BlockSpec((None,) + w_in_t.shape[1:], lambda i: (layer, 0, 0), pipeline_mode=pl.Buffered(1)),
                  full(w2), full(gb)],
        out_specs=[row(wd) for wd, _ in outs],
        out_shape=[jax.ShapeDtypeStruct((T, wd), dt) for wd, dt in outs],
        scratch_shapes=[pltpu.VMEM((D_MODEL, IN_PROJ_GATE_COL), BF16), pltpu.VMEM((D_MODEL, group), BF16),
                        pltpu.VMEM((D_MODEL, LANES), BF16)],
        compiler_params=pltpu.CompilerParams(
            dimension_semantics=("arbitrary",), vmem_limit_bytes=VMEM_LIMIT),
        name="in_proj",
    )(x2, w_in_t, w2, gb)


def _gla_kernel(q_ref, k_ref, v_ref, r_ref, la_ref, g_ref, o_ref, s_ref, *, seq_block):
    C = GLA_CHUNK
    H = GLA_SUB
    assert C == 2 * H

    @pl.when(pl.program_id(1) == 0)
    def _():
        s_ref[...] = jnp.zeros_like(s_ref)

    ri = lax.broadcasted_iota(jnp.int32, (C, C), 0)
    ci = lax.broadcasted_iota(jnp.int32, (C, C), 1)
    same_sub = (ri // H) == (ci // H)
    sum_ops = jnp.concatenate([jnp.logical_and(same_sub, ci <= ri).astype(BF16), same_sub.astype(BF16),
                               jnp.ones((C, C), BF16)], axis=0)
    diag_mask = jnp.logical_and(same_sub, ci <= ri)
    off_mask = (ri // H) > (ci // H)
    second = lax.broadcasted_iota(jnp.int32, (C, 1), 0) >= H
    ones_cl = jnp.ones((C, LANES), BF16)
    lane_k = lax.broadcasted_iota(jnp.int32, (1, GLA_KEY_WIDTH), 1) // GLA_DK
    head_masks = [(lane_k == h).astype(F32) for h in range(GLA_HEADS)]
    srow = lax.broadcasted_iota(jnp.int32, (GLA_KEY_WIDTH, GLA_WIDTH), 0) // GLA_DK
    scol = lax.broadcasted_iota(jnp.int32, (GLA_KEY_WIDTH, GLA_WIDTH), 1) // GLA_DV
    state_mask = (srow == scol).astype(F32)
    g = g_ref[...]

    def trip(t, carry):
        U = GLA_UNROLL
        rows = [pl.ds(pl.multiple_of((t * U + u) * C, C), C) for u in range(U)]
        la2s, sums = [], []
        for u in range(U):
            la_hi, la_lo = _split_bf16(la_ref[rows[u], :])
            la2 = jnp.concatenate([la_hi, la_lo], axis=1)
            la2s.append(la2)
            sm = _dot(sum_ops, la2)
            sums.append(sm[:, 0:GLA_KEY_WIDTH] + sm[:, GLA_KEY_WIDTH:])
        q_states, k_states, scores = [], [], []
        for u in range(U):
            b = sums[u][0:C]
            t_sub = sums[u][C:2 * C]
            other = sums[u][2 * C:3 * C] - t_sub
            q = q_ref[rows[u], :].astype(F32)
            k = k_ref[rows[u], :].astype(F32)
            qd = q * jnp.exp(b)
            kd = (k * jnp.exp(-b)).astype(BF16)
            ke_f = k * jnp.exp(t_sub - b)
            cross = jnp.exp(other)
            q_states.append((qd * jnp.where(second, cross, 1.0)).astype(BF16))
            k_states.append((ke_f * jnp.where(second, 1.0, cross)).astype(BF16))
            q_heads = jnp.concatenate([(qd * head_masks[h]).astype(BF16) for h in range(GLA_HEADS)], axis=0)
            keys2 = jnp.concatenate([kd, ke_f.astype(BF16)], axis=0)
            scores.append(_dot_nt(q_heads, keys2))
        decs, upds = [], []
        for u in range(U):
            tot = _dot_tn(la2s[u], ones_cl)
            decs.append(jnp.exp(tot[0:GLA_KEY_WIDTH] + tot[GLA_KEY_WIDTH:]))
            upds.append(_dot_tn(k_states[u], v_ref[rows[u], :]) * state_mask)
        o_inters = []
        for u in range(U):
            state = s_ref[...]
            o_inters.append(_dot(q_states[u], state.astype(BF16)))
            for h in range(GLA_HEADS):
                cols = slice(h * GLA_DV, (h + 1) * GLA_DV)
                s_ref[:, cols] = state[:, cols] * decs[u] + upds[u][:, cols]
        for u in range(U):
            v = v_ref[rows[u], :]
            outs = []
            for h in range(GLA_HEADS):
                sh = scores[u][h * C:(h + 1) * C]
                a = jnp.where(diag_mask, sh[:, 0:C], 0.0) + jnp.where(off_mask, sh[:, C:2 * C], 0.0)
                cols = slice(h * GLA_DV, (h + 1) * GLA_DV)
                o = _dot(a.astype(BF16), v[:, cols]) + o_inters[u][:, cols]
                o = o * lax.rsqrt(jnp.mean(o * o, axis=-1, keepdims=True) + EPS) * g
                outs.append(o)
            o_all = jnp.concatenate(outs, axis=-1)
            rr = r_ref[rows[u], :].astype(F32)
            o_ref[rows[u], :] = (o_all * (rr * jax.nn.sigmoid(rr))).astype(BF16)
        return carry

    lax.fori_loop(0, seq_block // C // GLA_UNROLL, trip, 0)


def _gla(q, k, v, r, la, g, batch, seq, seq_block=1024):
    nsb = seq // seq_block
    row = lambda w: pl.BlockSpec((seq_block, w), lambda b, s: (b * nsb + s, 0))
    return pl.pallas_call(
        functools.partial(_gla_kernel, seq_block=seq_block),
        grid=(batch, nsb),
        in_specs=[row(GLA_KEY_WIDTH), row(GLA_KEY_WIDTH), row(GLA_WIDTH), row(GLA_WIDTH), row(GLA_KEY_WIDTH),
                  pl.BlockSpec((1, GLA_DV), lambda b, s: (0, 0))],
        out_specs=row(GLA_WIDTH),
        out_shape=jax.ShapeDtypeStruct((batch * seq, GLA_WIDTH), BF16),
        scratch_shapes=[pltpu.VMEM((GLA_KEY_WIDTH, GLA_WIDTH), F32)],
        compiler_params=pltpu.CompilerParams(
            dimension_semantics=("parallel", "arbitrary"), vmem_limit_bytes=VMEM_LIMIT),
        name="gla",
    )(q, k, v, r, la, g)


def _dil_kernel(slope_ref, q_ref, k_ref, v_ref, g_ref, o_ref,
                qf, kf, vf, qs, ks, vs, kd, va, vb, oc, lc, ocs, lcs, *, seq):
    B = DIL_BLOCK
    U = DIL_UNROLL
    pair = pl.program_id(1)
    for src, nat, split in ((q_ref, qf, qs), (k_ref, kf, ks), (v_ref, vf, vs)):
        x = src[...].astype(F32)
        nat[...] = x
        x3 = x.reshape(seq // (2 * SUBLANES), 2 * SUBLANES, LANES)
        split[0] = x3[:, :SUBLANES, :].reshape(seq // 2, LANES)
        split[1] = x3[:, SUBLANES:, :].reshape(seq // 2, LANES)

    lane = lax.broadcasted_iota(jnp.int32, (1, LANES), 1)
    first = lane < DIL_DH
    ii = lax.broadcasted_iota(jnp.int32, (B, B), 0)
    jj = lax.broadcasted_iota(jnp.int32, (B, B), 1)
    upper = jj > ii
    eye = jj == ii
    dist = jnp.bitwise_and(ii - jj, B - 1).astype(F32)
    neg = jnp.float32(-jnp.inf)
    neg_tile = jnp.full((B, B), neg, F32)
    zero_tile = jnp.zeros((B, LANES), BF16)

    def split_rows(r, c, n):
        half = r // 2
        start = c % SUBLANES + c // (2 * SUBLANES) * SUBLANES + n * (B * half)
        return (c // SUBLANES) % 2, pl.ds(start, B, stride=half)

    def class_rows(nat, split, cfg, r, c, n):
        if r == 1:
            return nat[pl.ds(pl.multiple_of(n * B, B), B), :]
        if cfg == DIL_SPLIT_CFG:
            parity, rows = split_rows(r, c, n)
            return split[parity, rows, :]
        return nat[pl.ds(c + n * (B * r), B, stride=r), :]

    for cfg, (window, r) in enumerate(DIL_CONFIGS):
        nb = seq // r // B
        cs = nb + 1
        bias_prev, bias_cur = [], []
        for hh in range(2):
            slope = slope_ref[2 * pair + hh] * (float(r) * LOG2E)
            bias = dist * (-slope)
            bias_prev.append(jnp.where(upper, bias, jnp.where(eye, -slope * float(B), neg)))
            bias_cur.append(jnp.where(upper, neg, bias))

        for c in range(r):
            rows0 = slice(c * cs * B, (c * cs + 1) * B)
            kd[rows0, :] = zero_tile
            va[rows0, :] = zero_tile
            vb[rows0, :] = zero_tile

        def prep(t4, carry, cfg=cfg, r=r, nb=nb, cs=cs):
            for j in range(4):
                t = t4 * 4 + j
                c = t // nb
                n = t % nb
                dst = pl.ds(pl.multiple_of((c * cs + 1 + n) * B, B), B)
                kd[dst, :] = class_rows(kf, ks, cfg, r, c, n).astype(BF16)
                v = class_rows(vf, vs, cfg, r, c, n)
                va[dst, :] = jnp.where(first, v, 1.0).astype(BF16)
                vb[dst, :] = jnp.where(first, 1.0, v).astype(BF16)
            return carry

        lax.fori_loop(0, seq // B // 4, prep, 0)

        def geom(it, r=r, nb=nb, cs=cs):
            c = it // nb
            n = it % nb
            start = c + n * (B * r)
            rows = pl.ds(start, B, stride=r) if r > 1 else pl.ds(pl.multiple_of(start, B), B)
            kv = pl.ds(pl.multiple_of((c * cs + n) * B, B), 2 * B)
            return rows, kv, c, n

        def body(t, carry, cfg=cfg, r=r, nb=nb, bias_prev=bias_prev, bias_cur=bias_cur, geom=geom):
            geo = [geom(t * U + u) for u in range(U)]
            scores = []
            for u in range(U):
                _, kv, c, n = geo[u]
                q = class_rows(qf, qs, cfg, r, c, n)
                q_heads = jnp.concatenate([jnp.where(first, q, 0.0), jnp.where(first, 0.0, q)], axis=0)
                scores.append(_dot_nt(q_heads.astype(BF16), kd[kv, :]))
            probs, maxes = [], []
            for u in range(U):
                if nb % U == 0:
                    has_prev = True if u > 0 else (t * U) % nb > 0
                else:
                    assert U % nb == 0
                    has_prev = (u % nb) > 0
                for hh in range(2):
                    s2 = scores[u][hh * B:(hh + 1) * B]
                    if has_prev is True:
                        bp = bias_prev[hh]
                    elif has_prev is False:
                        bp = neg_tile
                    else:
                        bp = jnp.where(has_prev, bias_prev[hh], neg)
                    s_prev = s2[:, 0:B] + bp
                    s_cur = s2[:, B:2 * B] + bias_cur[hh]
                    m = jnp.max(jnp.maximum(s_prev, s_cur), axis=-1, keepdims=True)
                    probs.append(jnp.concatenate([jnp.exp2(s_prev - m), jnp.exp2(s_cur - m)], axis=1).astype(BF16))
                    maxes.append(m)
            for u in range(U):
                rows, kv, c, n = geo[u]
                acc0 = _dot(probs[2 * u], va[kv, :])
                acc1 = _dot(probs[2 * u + 1], vb[kv, :])
                num = jnp.where(first, acc0, acc1)
                den = pltpu.roll(jnp.where(first, acc1, acc0), DIL_DH, axis=1)
                out = num * (1.0 / den)
                lse = jnp.where(first, maxes[2 * u], maxes[2 * u + 1]) + jnp.log2(den)
                if cfg == DIL_SPLIT_CFG:
                    parity, dst = split_rows(r, c, n)
                    ocs[parity, dst, :] = out
                    lcs[parity, dst, :] = lse
                else:
                    oc[cfg, rows, :] = out
                    lc[cfg, rows, :] = lse
            return carry

        lax.fori_loop(0, seq // B // U, body, 0)

    g = g_ref[...]
    CH = 512

    def mix(i, carry):
        rows = pl.ds(pl.multiple_of(i * CH, CH), CH)
        half_rows = pl.ds(pl.multiple_of(i * (CH // 2), CH // 2), CH // 2)

        def tiles_interleaved(split):
            even = split[0, half_rows, :].reshape(CH // (2 * SUBLANES), 1, SUBLANES, LANES)
            odd = split[1, half_rows, :].reshape(CH // (2 * SUBLANES), 1, SUBLANES, LANES)
            return jnp.concatenate([even, odd], axis=1).reshape(CH, LANES)

        l0, l1, l2 = lc[0, rows, :], lc[1, rows, :], tiles_interleaved(lcs)
        m = jnp.maximum(jnp.maximum(l0, l1), l2)
        e0, e1, e2 = jnp.exp2(l0 - m), jnp.exp2(l1 - m), jnp.exp2(l2 - m)
        inv = 1.0 / (e0 + e1 + e2)
        o = (e0 * oc[0, rows, :] + e1 * oc[1, rows, :] + e2 * tiles_interleaved(ocs)) * inv
        sq = o * o
        ms_a = jnp.sum(jnp.where(first, sq, 0.0), axis=-1, keepdims=True) * (1.0 / DIL_DH)
        ms_b = jnp.sum(jnp.where(first, 0.0, sq), axis=-1, keepdims=True) * (1.0 / DIL_DH)
        ms = jnp.where(first, ms_a, ms_b)
        o_ref[rows, :] = (o * lax.rsqrt(ms + EPS) * g).astype(BF16)
        return carry

    lax.fori_loop(0, seq // CH, mix, 0)


def _dilated(slopes, dq, dk, dv, g2, batch, seq):
    blk = pl.BlockSpec((seq, LANES), lambda b, p, s: (b, p))
    return pl.pallas_call(
        functools.partial(_dil_kernel, seq=seq),
        grid_spec=pltpu.PrefetchScalarGridSpec(
            num_scalar_prefetch=1,
            grid=(batch, DIL_WIDTH // LANES),
            in_specs=[blk, blk, blk, pl.BlockSpec((1, LANES), lambda b, p, s: (0, 0))],
            out_specs=blk,
            scratch_shapes=[pltpu.VMEM((seq, LANES), F32)] * 3 + [pltpu.VMEM((2, seq // 2, LANES), F32)] * 3
                           + [pltpu.VMEM((seq + DIL_PAD, LANES), BF16)] * 3
                           + [pltpu.VMEM((2, seq, LANES), F32)] * 2 + [pltpu.VMEM((2, seq // 2, LANES), F32)] * 2,
        ),
        out_shape=jax.ShapeDtypeStruct((batch * seq, DIL_WIDTH), BF16),
        compiler_params=pltpu.CompilerParams(
            dimension_semantics=("parallel", "parallel"), vmem_limit_bytes=VMEM_LIMIT),
        name="dilated",
    )(slopes, dq, dk, dv, g2)


def _out_proj_kernel(og_ref, od_ref, x_ref, wg_ref, wd_ref, g_ref, b_ref,
                     rw_ref, rb_ref, h_ref, hp_ref, eid_ref, gate_ref, cnt_ref, wg_s, wd_s):
    @pl.when(pl.program_id(0) == 0)
    def _():
        wg_s[...] = wg_ref[...].astype(BF16)
        wd_s[...] = wd_ref[...].astype(BF16)

    mix = _dot(og_ref[...], wg_s[...]) + _dot(od_ref[...], wd_s[...])
    h = _layer_norm(DEEPNORM_ALPHA * x_ref[...] + mix, g_ref[...], b_ref[...])
    h_ref[...] = h
    hp_ref[...] = _pack_bf16_halves(h)
    lt = _dot_nt(rw_ref[...], h) + rb_ref[...]
    tm = lt.shape[1]
    row = lax.broadcasted_iota(jnp.int32, (EXPERTS_PER_GROUP, tm), 0).astype(F32)
    neg = jnp.float32(-jnp.inf)
    big = jnp.float32(1e9)
    coarse = jnp.where(row < N_GROUPS, lt[N_EXPERTS:N_EXPERTS + EXPERTS_PER_GROUP, :], neg)
    cmax = jnp.max(coarse, axis=0, keepdims=True)
    g_idx = jnp.min(jnp.where(coarse == cmax, row, big), axis=0, keepdims=True)
    p_group = 1.0 / jnp.sum(jnp.exp(coarse - cmax), axis=0, keepdims=True)
    fine = lt[(N_GROUPS - 1) * EXPERTS_PER_GROUP:N_EXPERTS, :]
    for g in range(N_GROUPS - 2, -1, -1):
        fine = jnp.where(g_idx == g, lt[g * EXPERTS_PER_GROUP:(g + 1) * EXPERTS_PER_GROUP, :], fine)
    v1 = jnp.max(fine, axis=0, keepdims=True)
    i1 = jnp.min(jnp.where(fine == v1, row, big), axis=0, keepdims=True)
    fine2 = jnp.where(row == i1, neg, fine)
    v2 = jnp.max(fine2, axis=0, keepdims=True)
    i2 = jnp.min(jnp.where(fine2 == v2, row, big), axis=0, keepdims=True)
    e2 = jnp.exp(v2 - v1)
    den = 1.0 + e2
    gate1 = p_group * (1.0 / den)
    gate2 = p_group * (e2 / den)
    id1 = g_idx * EXPERTS_PER_GROUP + i1
    id2 = g_idx * EXPERTS_PER_GROUP + i2
    eid_ref[...] = jnp.concatenate([id1, id2], axis=0).astype(jnp.int32)
    slab = jnp.concatenate([gate1, gate2, jnp.zeros((LANES - 2, tm), F32)], axis=0)
    gate_ref[...] = slab.T[:, 0:2]
    sub = lax.broadcasted_iota(jnp.int32, (LANES, tm), 0).astype(F32)
    onehot = jnp.logical_or(sub == id1, sub == id2).astype(BF16)

    @pl.when(pl.program_id(0) == 0)
    def _():
        cnt_ref[...] = jnp.zeros_like(cnt_ref)

    cnt_ref[...] += _dot(onehot, jnp.ones((tm, LANES), BF16))


def _out_proj(og, od, x2, w_out, layer, g, b, rw, rb, tm=1024):
    T = x2.shape[0]
    row = lambda w: pl.BlockSpec((tm, w), lambda i: (i, 0))
    full = lambda a: pl.BlockSpec(a.shape, lambda i: (0,) * a.ndim)
    half = lambda p: pl.BlockSpec((None, GLA_WIDTH, D_MODEL), lambda i: (layer, p, 0))
    return pl.pallas_call(
        _out_proj_kernel,
        grid=(T // tm,),
        in_specs=[row(GLA_WIDTH), row(DIL_WIDTH), row(D_MODEL), half(0), half(1), full(g), full(b),
                  full(rw), full(rb)],
        out_specs=[row(D_MODEL), row(D_MODEL // 2), pl.BlockSpec((2, tm), lambda i: (0, i)), row(2),
                   pl.BlockSpec((LANES, LANES), lambda i: (0, 0))],
        out_shape=[jax.ShapeDtypeStruct((T, D_MODEL), F32),
                   jax.ShapeDtypeStruct((T, D_MODEL // 2), jnp.int32),
                   jax.ShapeDtypeStruct((2, T), jnp.int32),
                   jax.ShapeDtypeStruct((T, 2), F32),
                   jax.ShapeDtypeStruct((LANES, LANES), F32)],
        scratch_shapes=[pltpu.VMEM((GLA_WIDTH, D_MODEL), BF16), pltpu.VMEM((DIL_WIDTH, D_MODEL), BF16)],
        compiler_params=pltpu.CompilerParams(
            dimension_semantics=("arbitrary",), vmem_limit_bytes=VMEM_LIMIT),
        name="out_proj_router",
    )(og, od, x2, w_out, w_out, g, b, rw, rb)


def _positions_kernel(eid_ref, cnt_ref, dest_ref, be_ref, nv_ref, carry_ref, sp_ref, tri_ref, *, tb):
    i = pl.program_id(0)

    @pl.when(i == 0)
    def _():
        shift = int(math.log2(ROW_BLOCK))
        nb_col = (cnt_ref[...].astype(jnp.int32) + (ROW_BLOCK - 1)) >> shift
        r = lax.broadcasted_iota(jnp.int32, (LANES, LANES), 0)
        c = lax.broadcasted_iota(jnp.int32, (LANES, LANES), 1)
        nb_f = jnp.where(r < N_EXPERTS, nb_col, 0).astype(F32)
        start_col = _dot((c < r).astype(BF16), nb_f.astype(BF16))
        sp_ref[...] = start_col * float(ROW_BLOCK)
        carry_ref[...] = jnp.zeros_like(carry_ref)
        be_ref[...] = jnp.concatenate([start_col.T[0:1, :], nb_f.T[0:1, :]], axis=1).astype(jnp.int32)
        total = jnp.sum(nb_f[:, 0:1], axis=0, keepdims=True)
        nv_ref[...] = jnp.broadcast_to(total, (1, LANES)).astype(jnp.int32)
        tr = lax.broadcasted_iota(jnp.int32, (tb, tb), 0)
        tc = lax.broadcasted_iota(jnp.int32, (tb, tb), 1)
        tri_ref[...] = (tr < tc).astype(BF16)

    sub = lax.broadcasted_iota(jnp.int32, (LANES, tb), 0)
    oh1 = sub == eid_ref[0:1, :]
    oh2 = sub == eid_ref[1:2, :]
    oh = jnp.logical_or(oh1, oh2).astype(BF16)
    offset = jnp.tile(carry_ref[...] + sp_ref[...], (1, tb // LANES))
    before = _dot(oh, tri_ref[...]) + offset
    d1 = jnp.sum(jnp.where(oh1, before, 0.0), axis=0, keepdims=True)
    d2 = jnp.sum(jnp.where(oh2, before, 0.0), axis=0, keepdims=True)
    dest_ref[...] = jnp.concatenate([d1, d2], axis=0).astype(jnp.int32)
    carry_ref[...] += _dot(oh, jnp.ones((tb, LANES), BF16))


def _positions(eid_t, cnt, n_blocks_pad, tb=1024):
    T = eid_t.shape[1]
    return pl.pallas_call(
        functools.partial(_positions_kernel, tb=tb),
        grid=(T // tb,),
        in_specs=[pl.BlockSpec((2, tb), lambda i: (0, i)), pl.BlockSpec((LANES, LANES), lambda i: (0, 0))],
        out_specs=[pl.BlockSpec((2, tb), lambda i: (0, i)),
                   pl.BlockSpec((1, n_blocks_pad), lambda i: (0, 0)),
                   pl.BlockSpec((1, LANES), lambda i: (0, 0))],
        out_shape=[jax.ShapeDtypeStruct((2, T), jnp.int32),
                   jax.ShapeDtypeStruct((1, n_blocks_pad), jnp.int32),
                   jax.ShapeDtypeStruct((1, LANES), jnp.int32)],
        scratch_shapes=[pltpu.VMEM((LANES, LANES), F32), pltpu.VMEM((LANES, LANES), F32),
                        pltpu.VMEM((tb, tb), BF16)],
        compiler_params=pltpu.CompilerParams(dimension_semantics=("arbitrary",)),
        name="positions",
    )(eid_t, cnt)


def _sc_gather_rows(table, idx):
    n = idx.shape[0]
    d = table.shape[1]
    info = plsc.get_sparse_core_info()
    nc, ns = info.num_cores, info.num_subcores
    per_w = n // (nc * ns)
    assert per_w * nc * ns == n and per_w % SC_INDEX_WINDOW == 0
    mesh = plsc.VectorSubcoreMesh(core_axis_name="core", subcore_axis_name="subcore")
    nchunk = per_w // SC_GATHER_ROWS
    nbuf = SC_GATHER_BUFFERS

    @functools.partial(
        pl.kernel, out_type=jax.ShapeDtypeStruct((n, d), table.dtype), mesh=mesh,
        scratch_types=[pltpu.VMEM((per_w,), jnp.int32),
                       pltpu.VMEM((nbuf, SC_GATHER_ROWS, d), table.dtype),
                       pltpu.SemaphoreType.DMA((nbuf,)), pltpu.SemaphoreType.DMA((nbuf,))],
        name="sc_gather_rows")
    def gather(x_hbm, i_hbm, o_hbm, idx_v, buf, gsem, wsem):
        wid = lax.axis_index("subcore") * nc + lax.axis_index("core")
        base = wid * per_w
        pltpu.sync_copy(i_hbm.at[pl.ds(base, per_w)], idx_v)

        def gather_copy(c):
            rows = idx_v.at[pl.ds(c * SC_GATHER_ROWS, SC_GATHER_ROWS)]
            return pltpu.make_async_copy(x_hbm.at[rows], buf.at[c % nbuf], gsem.at[c % nbuf])

        def write_copy(c):
            dst = o_hbm.at[pl.ds(base + c * SC_GATHER_ROWS, SC_GATHER_ROWS)]
            return pltpu.make_async_copy(buf.at[c % nbuf], dst, wsem.at[c % nbuf])

        for c in range(min(nbuf - 1, nchunk)):
            gather_copy(c).start()
        for c in range(nchunk):
            gather_copy(c).wait()
            write_copy(c).start()
            if c + nbuf - 1 < nchunk:
                if c >= 1:
                    write_copy(c - 1).wait()
                gather_copy(c + nbuf - 1).start()
        for c in range(max(0, nchunk - nbuf), nchunk):
            write_copy(c).wait()

    return gather(table, idx)


def _sc_scatter_rows(table, dest_flat, n_rows):
    n_tok, d = table.shape
    assert dest_flat.shape[0] == 2 * n_tok
    info = plsc.get_sparse_core_info()
    nc, ns = info.num_cores, info.num_subcores
    per_w = n_tok // (nc * ns)
    assert per_w * nc * ns == n_tok and per_w % SC_INDEX_WINDOW == 0
    mesh = plsc.VectorSubcoreMesh(core_axis_name="core", subcore_axis_name="subcore")
    nchunk = per_w // SC_GATHER_ROWS
    nbuf = SC_GATHER_BUFFERS

    @functools.partial(
        pl.kernel, out_type=jax.ShapeDtypeStruct((n_rows, d), table.dtype), mesh=mesh,
        scratch_types=[pltpu.VMEM((per_w,), jnp.int32), pltpu.VMEM((per_w,), jnp.int32),
                       pltpu.VMEM((nbuf, SC_GATHER_ROWS, d), table.dtype),
                       pltpu.SemaphoreType.DMA((nbuf,)), pltpu.SemaphoreType.DMA((nbuf, 2))],
        name="sc_scatter_rows")
    def scatter(x_hbm, i_hbm, o_hbm, idx_a, idx_b, buf, rsem, wsem):
        wid = lax.axis_index("subcore") * nc + lax.axis_index("core")
        base = wid * per_w
        pltpu.sync_copy(i_hbm.at[pl.ds(base, per_w)], idx_a)
        pltpu.sync_copy(i_hbm.at[pl.ds(n_tok + base, per_w)], idx_b)

        def read_copy(c):
            src = x_hbm.at[pl.ds(base + c * SC_GATHER_ROWS, SC_GATHER_ROWS)]
            return pltpu.make_async_copy(src, buf.at[c % nbuf], rsem.at[c % nbuf])

        def scatter_copies(c):
            window = pl.ds(c * SC_GATHER_ROWS, SC_GATHER_ROWS)
            return [pltpu.make_async_copy(buf.at[c % nbuf], o_hbm.at[idx.at[window]], wsem.at[c % nbuf, k])
                    for k, idx in enumerate((idx_a, idx_b))]

        for c in range(min(nbuf - 1, nchunk)):
            read_copy(c).start()
        for c in range(nchunk):
            read_copy(c).wait()
            for cp in scatter_copies(c):
                cp.start()
            if c + nbuf - 1 < nchunk:
                if c >= 1:
                    for cp in scatter_copies(c - 1):
                        cp.wait()
                read_copy(c + nbuf - 1).start()
        for c in range(max(0, nchunk - nbuf), nchunk):
            for cp in scatter_copies(c):
                cp.wait()

    return scatter(table, dest_flat)


def _ffn_kernel(first_ref, count_ref, nv_ref, wg_hbm, wu_hbm, wd_hbm, xs_hbm, y_hbm,
                wg32, wu32, wd32, xbuf, ybuf, wsem, isem, osem, *, n_blocks):
    nv = nv_ref[0]
    nbuf = FFN_BUFFERS

    def next_expert(e):
        def more(t):
            return jnp.logical_and(t < N_EXPERTS, count_ref[jnp.minimum(t, N_EXPERTS - 1)] == 0)
        return lax.while_loop(more, lambda t: t + 1, e + 1)

    def weight_copies(e, slot):
        ee = jnp.minimum(e, N_EXPERTS - 1)
        return [pltpu.make_async_copy(src.at[ee], dst.at[slot], wsem.at[slot, i])
                for i, (src, dst) in enumerate(((wg_hbm, wg32), (wu_hbm, wu32), (wd_hbm, wd32)))]

    def fetch_weights(e, slot):
        @pl.when(e < N_EXPERTS)
        def _():
            for c in weight_copies(e, slot):
                c.start()

    def take_weights(e, slot):
        for c in weight_copies(e, slot):
            c.wait()
        fetch_weights(next_expert(e), 1 - slot)

    def rows_of(b):
        return pl.ds(pl.multiple_of(b * ROW_BLOCK, ROW_BLOCK), ROW_BLOCK)

    def buf_rows(b, nblk):
        return pl.ds(pl.multiple_of((b % nbuf) * ROW_BLOCK, ROW_BLOCK), nblk * ROW_BLOCK)

    def in_copy(b):
        return pltpu.make_async_copy(xs_hbm.at[rows_of(b)], xbuf.at[buf_rows(b, 1)], isem.at[b % nbuf])

    def out_copy(b):
        return pltpu.make_async_copy(ybuf.at[buf_rows(b, 1)], y_hbm.at[rows_of(b)], osem.at[b % nbuf])

    def expert_mlp(words, slot):
        x = jnp.concatenate(_unpack_bf16_halves(words), axis=1)
        a = _dot(x, wg32[slot])
        u = _dot(x, wu32[slot])
        return _pack_bf16_halves(_dot(a * jax.nn.sigmoid(a) * u, wd32[slot]))

    ahead = nbuf - max(FFN_GROUPS)

    @pl.when(nv > 0)
    def _():
        e0 = next_expert(jnp.int32(-1))
        for i in range(ahead):
            @pl.when(i < nv)
            def _():
                in_copy(i).start()
        fetch_weights(e0, 0)
        take_weights(e0, 0)

        def step(carry):
            b, e, k = carry
            switch = b >= first_ref[e] + count_ref[e]
            e_new = jnp.where(switch, next_expert(e), e)
            k_new = jnp.where(switch, k + 1, k)

            @pl.when(switch)
            def _():
                take_weights(e_new, k_new % 2)

            end = first_ref[e_new] + count_ref[e_new]
            n = jnp.int32(1)
            for size in FFN_GROUPS:
                fits = jnp.logical_and(b + size <= end, b % nbuf + size <= nbuf)
                n = jnp.where(jnp.logical_and(n == 1, fits), size, n)
            for i in range(max(FFN_GROUPS)):
                @pl.when(jnp.logical_and(i < n, b + ahead + i < nv))
                def _():
                    in_copy(b + ahead + i).start()
            for i in range(max(FFN_GROUPS)):
                @pl.when(i < n)
                def _():
                    in_copy(b + i).wait()

                    @pl.when(b + i >= nbuf)
                    def _():
                        out_copy(b + i - nbuf).wait()

            for size in FFN_GROUPS + (1,):
                @pl.when(n == size)
                def _():
                    ybuf[buf_rows(b, size), :] = expert_mlp(xbuf[buf_rows(b, size), :], k_new % 2)

            for i in range(max(FFN_GROUPS)):
                @pl.when(i < n)
                def _():
                    out_copy(b + i).start()
            return b + n, e_new, k_new

        lax.while_loop(lambda c: c[0] < nv, step, (jnp.int32(0), e0, jnp.int32(0)))

        for i in range(nbuf):
            @pl.when(nv > i)
            def _():
                out_copy(nv - 1 - i).wait()

    ybuf[0:ROW_BLOCK, :] = jnp.zeros((ROW_BLOCK, D_MODEL // 2), jnp.int32)

    def fill(b, carry):
        pltpu.sync_copy(ybuf.at[pl.ds(0, ROW_BLOCK)], y_hbm.at[rows_of(b)])
        return carry

    lax.fori_loop(nv, n_blocks, fill, 0)


def _ffn(first_blk, n_blk, nv, xs, w_gate, w_up, w_down):
    n_rows = xs.shape[0]
    n_blocks = n_rows // ROW_BLOCK
    anyspec = pl.BlockSpec(memory_space=pl.ANY)
    return pl.pallas_call(
        functools.partial(_ffn_kernel, n_blocks=n_blocks),
        grid_spec=pltpu.PrefetchScalarGridSpec(
            num_scalar_prefetch=3,
            grid=(1,),
            in_specs=[anyspec, anyspec, anyspec, anyspec],
            out_specs=anyspec,
            scratch_shapes=[pltpu.VMEM((2, D_MODEL, D_FF), F32), pltpu.VMEM((2, D_MODEL, D_FF), F32),
                            pltpu.VMEM((2, D_FF, D_MODEL), F32),
                            pltpu.VMEM((FFN_BUFFERS * ROW_BLOCK, D_MODEL // 2), jnp.int32),
                            pltpu.VMEM((FFN_BUFFERS * ROW_BLOCK, D_MODEL // 2), jnp.int32),
                            pltpu.SemaphoreType.DMA((2, 3)),
                            pltpu.SemaphoreType.DMA((FFN_BUFFERS,)),
                            pltpu.SemaphoreType.DMA((FFN_BUFFERS,))],
        ),
        out_shape=jax.ShapeDtypeStruct((n_rows, D_MODEL // 2), jnp.int32),
        compiler_params=pltpu.CompilerParams(
            dimension_semantics=("arbitrary",), vmem_limit_bytes=VMEM_LIMIT),
        name="expert_ffn",
    )(first_blk, n_blk, nv, w_gate, w_up, w_down, xs)


def _combine_kernel(h_ref, ya_ref, yb_ref, gate_ref, g_ref, b_ref, o_ref):
    gate = gate_ref[...]
    a_hi, a_lo = _unpack_bf16_halves(ya_ref[...])
    b_hi, b_lo = _unpack_bf16_halves(yb_ref[...])
    g0, g1 = gate[:, 0:1], gate[:, 1:2]
    ffn = jnp.concatenate([a_hi * g0 + b_hi * g1, a_lo * g0 + b_lo * g1], axis=1)
    o_ref[...] = _layer_norm(DEEPNORM_ALPHA * h_ref[...] + ffn, g_ref[...], b_ref[...])


def _combine(h, y2, gate, g, b, tm=1024):
    T = h.shape[0]
    nt = T // tm
    return pl.pallas_call(
        _combine_kernel,
        grid=(nt,),
        in_specs=[pl.BlockSpec((tm, D_MODEL), lambda i: (i, 0)),
                  pl.BlockSpec((tm, D_MODEL // 2), lambda i: (i, 0)),
                  pl.BlockSpec((tm, D_MODEL // 2), lambda i: (i + nt, 0)),
                  pl.BlockSpec((tm, 2), lambda i: (i, 0)),
                  pl.BlockSpec((1, D_MODEL), lambda i: (0, 0)),
                  pl.BlockSpec((1, D_MODEL), lambda i: (0, 0))],
        out_specs=pl.BlockSpec((tm, D_MODEL), lambda i: (i, 0)),
        out_shape=jax.ShapeDtypeStruct((T, D_MODEL), F32),
        compiler_params=pltpu.CompilerParams(
            dimension_semantics=("parallel",), vmem_limit_bytes=VMEM_LIMIT),
        name="combine",
    )(h, y2, y2, gate, g, b)


def kernel(x, w_in, gla_gate_w2, gla_gate_b, gla_norm_g, dil_norm_g, w_out, ln1_g, ln1_b,
           router_coarse_w, router_coarse_b, router_fine_w, router_fine_b,
           expert_w_gate, expert_w_up, expert_w_down, ln2_g, ln2_b):
    B, S, D = x.shape
    T = B * S
    depth = w_in.shape[0]
    slopes = jnp.exp2(-8.0 * jnp.arange(1, DIL_HEADS + 1, dtype=F32) / DIL_HEADS)
    n_rows = 2 * T + N_EXPERTS * ROW_BLOCK
    n_blocks_pad = 2 * LANES
    h = x.reshape(T, D)
    w_in_t = jnp.swapaxes(w_in, 1, 2)
    for l in range(depth):
        w2 = jnp.pad(gla_gate_w2[l], ((0, LANES - GLA_GATE_RANK), (0, 0)))
        q, k, v, r, la, dq, dk, dv = _in_proj(h, w_in_t, l, w2, gla_gate_b[l][None, :])
        o_gla = _gla(q, k, v, r, la, gla_norm_g[l][None, :], B, S)
        g2 = jnp.tile(dil_norm_g[l], 2)[None, :]
        o_dil = _dilated(slopes, dq, dk, dv, g2, B, S)
        rw = jnp.concatenate([router_fine_w[l].reshape(D, N_EXPERTS), router_coarse_w[l]], axis=1)
        rw = jnp.pad(rw, ((0, 0), (0, LANES - N_EXPERTS - N_GROUPS))).T
        rb = jnp.concatenate([router_fine_b[l].reshape(N_EXPERTS), router_coarse_b[l]])
        rb = jnp.pad(rb, (0, LANES - N_EXPERTS - N_GROUPS))[:, None]
        h1, h1p, eid_t, gate, cnt = _out_proj(o_gla, o_dil, h, w_out, l,
                                              ln1_g[l][None, :], ln1_b[l][None, :], rw, rb)
        dest_t, be, nv = _positions(eid_t, cnt, n_blocks_pad)
        dest_flat = dest_t.reshape(2 * T)
        xs = _sc_scatter_rows(h1p, dest_flat, n_rows)
        be = be.reshape(n_blocks_pad)
        y = _ffn(be[:N_EXPERTS], be[LANES:LANES + N_EXPERTS], nv.reshape(LANES)[:1], xs,
                 expert_w_gate[l], expert_w_up[l], expert_w_down[l])
        y2 = _sc_gather_rows(y, dest_flat)
        h = _combine(h1, y2, gate, ln2_g[l][None, :], ln2_b[l][None, :])
    return h.reshape(B, S, D)
```

```python
import functools
import math

import jax
import jax.numpy as jnp
from jax import lax
from jax.experimental import pallas as pl
from jax.experimental.pallas import tpu as pltpu
from jax.experimental.pallas import tpu_sc as plsc

D_MODEL = 1024
GLA_HEADS = 4
GLA_DK = 64
GLA_DV = 128
GLA_KEY_WIDTH = GLA_HEADS * GLA_DK
GLA_WIDTH = GLA_HEADS * GLA_DV
GLA_GATE_RANK = 16
GLA_GATE_TEMP = 16.0
DIL_HEADS = 8
DIL_DH = 64
DIL_WIDTH = DIL_HEADS * DIL_DH
DIL_CONFIGS = ((128, 1), (512, 4), (2048, 16))
DIL_BLOCK = 128
DIL_MAX_R = max(r for _, r in DIL_CONFIGS)
DIL_PAD = DIL_BLOCK * DIL_MAX_R
DIL_UNROLL = 16
DIL_SPLIT_CFG = 2
N_GROUPS = 4
EXPERTS_PER_GROUP = 8
N_EXPERTS = N_GROUPS * EXPERTS_PER_GROUP
D_FF = 512
DEEPNORM_ALPHA = 2.0 ** 0.25
EPS = 1e-5
IN_PROJ_GATE_COL = 2 * GLA_KEY_WIDTH + 2 * GLA_WIDTH
LOG2E = math.log2(math.e)

LANES = 128
SUBLANES = 8
assert DIL_SPLIT_CFG == len(DIL_CONFIGS) - 1 and DIL_CONFIGS[DIL_SPLIT_CFG][1] % (2 * SUBLANES) == 0
GLA_CHUNK = 128
GLA_SUB = 64
GLA_UNROLL = 4
SC_INDEX_WINDOW = 128
SC_GATHER_BUFFERS = 6
SC_GATHER_ROWS = 32
FFN_BUFFERS = 12
FFN_GROUPS = (4, 2)
ROW_BLOCK = 256
VMEM_LIMIT = 56 * 1024 * 1024

F32 = jnp.float32
BF16 = jnp.bfloat16


def _dot(a, b):
    return jnp.dot(a, b, preferred_element_type=F32)


def _dot_nt(a, b):
    return lax.dot_general(a, b, (((1,), (1,)), ((), ())), preferred_element_type=F32)


def _dot_tn(a, b):
    return lax.dot_general(a, b, (((0,), (0,)), ((), ())), preferred_element_type=F32)


def _split_bf16(v):
    hi = v.astype(BF16)
    lo = (v - hi.astype(F32)).astype(BF16)
    return hi, lo


def _pack_bf16_halves(v):
    w = v.shape[1] // 2
    hi = lax.bitcast_convert_type(v[:, :w].astype(BF16).astype(F32), jnp.int32)
    lo = lax.bitcast_convert_type(v[:, w:].astype(BF16).astype(F32), jnp.int32)
    return hi | lax.shift_right_logical(lo, 16)


def _unpack_bf16_halves(words):
    hi = lax.bitcast_convert_type(words & jnp.int32(-65536), F32)
    lo = lax.bitcast_convert_type(lax.shift_left(words, 16), F32)
    return hi, lo


def _layer_norm(v, g, b):
    mu = jnp.mean(v, axis=-1, keepdims=True)
    c = v - mu
    var = jnp.mean(c * c, axis=-1, keepdims=True)
    return c * lax.rsqrt(var + EPS) * g + b


def _in_proj_kernel(x_ref, w_ref, w2_ref, gb_ref,
                    q_ref, k_ref, v_ref, r_ref, la_ref, dq_ref, dk_ref, dv_ref, wg_s, wd_s, wa_s):
    a0 = IN_PROJ_GATE_COL

    @pl.when(pl.program_id(0) == 0)
    def _():
        wg_s[...] = w_ref[0:a0, :].T.astype(BF16)
        gate_tile = w_ref[a0:a0 + LANES, :].T
        lane = lax.broadcasted_iota(jnp.int32, gate_tile.shape, 1)
        wa_s[...] = jnp.where(lane < GLA_GATE_RANK, gate_tile, 0.0).astype(BF16)
        wd_s[...] = w_ref[a0 + GLA_GATE_RANK:, :].T.astype(BF16)

    xb = x_ref[...].astype(BF16)

    def piece(w_s, c0, c1):
        return _dot(xb, w_s[:, c0:c1])

    q_ref[...] = (piece(wg_s, 0, 256) * (GLA_DK ** -0.5)).astype(BF16)
    k_ref[...] = piece(wg_s, 256, 512).astype(BF16)
    v_ref[...] = piece(wg_s, 512, 1024).astype(BF16)
    r_ref[...] = piece(wg_s, 1024, 1536).astype(BF16)
    dq_ref[...] = (piece(wd_s, 0, 512) * (DIL_DH ** -0.5 * LOG2E)).astype(BF16)
    dk_ref[...] = piece(wd_s, 512, 1024).astype(BF16)
    dv_ref[...] = piece(wd_s, 1024, 1536).astype(BF16)
    ga = _dot(xb, wa_s[...])
    z = _dot(ga, w2_ref[...]) + gb_ref[...]
    log_sig = jnp.minimum(z, 0.0) - jnp.log1p(jnp.exp(-jnp.abs(z)))
    la_ref[...] = log_sig * (1.0 / GLA_GATE_TEMP)


def _in_proj(x2, w_in_t, layer, w2, gb, tm=1024):
    T = x2.shape[0]
    row = lambda wd: pl.BlockSpec((tm, wd), lambda i: (i, 0))
    full = lambda a: pl.BlockSpec(a.shape, lambda i: (0,) * a.ndim)
    outs = [(GLA_KEY_WIDTH, BF16), (GLA_KEY_WIDTH, BF16), (GLA_WIDTH, BF16), (GLA_WIDTH, BF16), (GLA_KEY_WIDTH, F32),
            (DIL_WIDTH, BF16), (DIL_WIDTH, BF16), (DIL_WIDTH, BF16)]
    group = w_in_t.shape[1] - IN_PROJ_GATE_COL - GLA_GATE_RANK
    return pl.pallas_call(
        _in_proj_kernel,
        grid=(T // tm,),
        in_specs=[row(D_MODEL),
                  pl.BlockSpec((None,) + w_in_t.shape[1:], lambda i: (layer, 0, 0), pipeline_mode=pl.Buffered(1)),
                  full(w2), full(gb)],
        out_specs=[row(wd) for wd, _ in outs],
        out_shape=[jax.ShapeDtypeStruct((T, wd), dt) for wd, dt in outs],
        scratch_shapes=[pltpu.VMEM((D_MODEL, IN_PROJ_GATE_COL), BF16), pltpu.VMEM((D_MODEL, group), BF16),
                        pltpu.VMEM((D_MODEL, LANES), BF16)],
        compiler_params=pltpu.CompilerParams(
            dimension_semantics=("arbitrary",), vmem_limit_bytes=VMEM_LIMIT),
        name="in_proj",
    )(x2, w_in_t, w2, gb)


def _gla_kernel(q_ref, k_ref, v_ref, r_ref, la_ref, g_ref, o_ref, s_ref, *, seq_block):
    C = GLA_CHUNK
    H = GLA_SUB
    assert C == 2 * H

    @pl.when(pl.program_id(1) == 0)
    def _():
        s_ref[...] = jnp.zeros_like(s_ref)

    ri = lax.broadcasted_iota(jnp.int32, (C, C), 0)
    ci = lax.broadcasted_iota(jnp.int32, (C, C), 1)
    same_sub = (ri // H) == (ci // H)
    sum_ops = jnp.concatenate([jnp.logical_and(same_sub, ci <= ri).astype(BF16), same_sub.astype(BF16),
                               jnp.ones((C, C), BF16)], axis=0)
    diag_mask = jnp.logical_and(same_sub, ci <= ri)
    off_mask = (ri // H) > (ci // H)
    second = lax.broadcasted_iota(jnp.int32, (C, 1), 0) >= H
    ones_cl = jnp.ones((C, LANES), BF16)
    lane_k = lax.broadcasted_iota(jnp.int32, (1, GLA_KEY_WIDTH), 1) // GLA_DK
    head_masks = [(lane_k == h).astype(F32) for h in range(GLA_HEADS)]
    srow = lax.broadcasted_iota(jnp.int32, (GLA_KEY_WIDTH, GLA_WIDTH), 0) // GLA_DK
    scol = lax.broadcasted_iota(jnp.int32, (GLA_KEY_WIDTH, GLA_WIDTH), 1) // GLA_DV
    state_mask = (srow == scol).astype(F32)
    g = g_ref[...]

    def trip(t, carry):
        U = GLA_UNROLL
        rows = [pl.ds(pl.multiple_of((t * U + u) * C, C), C) for u in range(U)]
        la2s, sums = [], []
        for u in range(U):
            la_hi, la_lo = _split_bf16(la_ref[rows[u], :])
            la2 = jnp.concatenate([la_hi, la_lo], axis=1)
            la2s.append(la2)
            sm = _dot(sum_ops, la2)
            sums.append(sm[:, 0:GLA_KEY_WIDTH] + sm[:, GLA_KEY_WIDTH:])
        q_states, k_states, scores = [], [], []
        for u in range(U):
            b = sums[u][0:C]
            t_sub = sums[u][C:2 * C]
            other = sums[u][2 * C:3 * C] - t_sub
            q = q_ref[rows[u], :].astype(F32)
            k = k_ref[rows[u], :].astype(F32)
            qd = q * jnp.exp(b)
            kd = (k * jnp.exp(-b)).astype(BF16)
            ke_f = k * jnp.exp(t_sub - b)
            cross = jnp.exp(other)
            q_states.append((qd * jnp.where(second, cross, 1.0)).astype(BF16))
            k_states.append((ke_f * jnp.where(second, 1.0, cross)).astype(BF16))
            q_heads = jnp.concatenate([(qd * head_masks[h]).astype(BF16) for h in range(GLA_HEADS)], axis=0)
            keys2 = jnp.concatenate([kd, ke_f.astype(BF16)], axis=0)
            scores.append(_dot_nt(q_heads, keys2))
        decs, upds = [], []
        for u in range(U):
            tot = _dot_tn(la2s[u], ones_cl)
            decs.append(jnp.exp(tot[0:GLA_KEY_WIDTH] + tot[GLA_KEY_WIDTH:]))
            upds.append(_dot_tn(k_states[u], v_ref[rows[u], :]) * state_mask)
        o_inters = []
        for u in range(U):
            state = s_ref[...]
            o_inters.append(_dot(q_states[u], state.astype(BF16)))
            for h in range(GLA_HEADS):
                cols = slice(h * GLA_DV, (h + 1) * GLA_DV)
                s_ref[:, cols] = state[:, cols] * decs[u] + upds[u][:, cols]
        for u in range(U):
            v = v_ref[rows[u], :]
            outs = []
            for h in range(GLA_HEADS):
                sh = scores[u][h * C:(h + 1) * C]
                a = jnp.where(diag_mask, sh[:, 0:C], 0.0) + jnp.where(off_mask, sh[:, C:2 * C], 0.0)
                cols = slice(h * GLA_DV, (h + 1) * GLA_DV)
                o = _dot(a.astype(BF16), v[:, cols]) + o_inters[u][:, cols]
                o = o * lax.rsqrt(jnp.mean(o * o, axis=-1, keepdims=True) + EPS) * g
                outs.append(o)
            o_all = jnp.concatenate(outs, axis=-1)
            rr = r_ref[rows[u], :].astype(F32)
            o_ref[rows[u], :] = (o_all * (rr * jax.nn.sigmoid(rr))).astype(BF16)
        return carry

    lax.fori_loop(0, seq_block // C // GLA_UNROLL, trip, 0)


def _gla(q, k, v, r, la, g, batch, seq, seq_block=1024):
    nsb = seq // seq_block
    row = lambda w: pl.BlockSpec((seq_block, w), lambda b, s: (b * nsb + s, 0))
    return pl.pallas_call(
        functools.partial(_gla_kernel, seq_block=seq_block),
        grid=(batch, nsb),
        in_specs=[row(GLA_KEY_WIDTH), row(GLA_KEY_WIDTH), row(GLA_WIDTH), row(GLA_WIDTH), row(GLA_KEY_WIDTH),
                  pl.BlockSpec((1, GLA_DV), lambda b, s: (0, 0))],
        out_specs=row(GLA_WIDTH),
        out_shape=jax.ShapeDtypeStruct((batch * seq, GLA_WIDTH), BF16),
        scratch_shapes=[pltpu.VMEM((GLA_KEY_WIDTH, GLA_WIDTH), F32)],
        compiler_params=pltpu.CompilerParams(
            dimension_semantics=("parallel", "arbitrary"), vmem_limit_bytes=VMEM_LIMIT),
        name="gla",
    )(q, k, v, r, la, g)


def _dil_kernel(slope_ref, q_ref, k_ref, v_ref, g_ref, o_ref,
                qf, kf, vf, qs, kd, va, vb, oc, lc, ocs, lcs, *, seq):
    B = DIL_BLOCK
    U = DIL_UNROLL
    pair = pl.program_id(1)
    x = q_ref[...].astype(F32)
    qf[...] = x
    x3 = x.reshape(seq // (2 * SUBLANES), 2 * SUBLANES, LANES)
    qs[0] = x3[:, :SUBLANES, :].reshape(seq // 2, LANES)
    qs[1] = x3[:, SUBLANES:, :].reshape(seq // 2, LANES)
    kf[...] = k_ref[...].astype(F32)
    vf[...] = v_ref[...].astype(F32)

    lane = lax.broadcasted_iota(jnp.int32, (1, LANES), 1)
    first = lane < DIL_DH
    ii = lax.broadcasted_iota(jnp.int32, (B, B), 0)
    jj = lax.broadcasted_iota(jnp.int32, (B, B), 1)
    upper = jj > ii
    eye = jj == ii
    dist = jnp.bitwise_and(ii - jj, B - 1).astype(F32)
    neg = jnp.float32(-jnp.inf)
    neg_tile = jnp.full((B, B), neg, F32)
    zero_tile = jnp.zeros((B, LANES), BF16)

    def split_rows(r, c, n):
        half = r // 2
        start = c % SUBLANES + c // (2 * SUBLANES) * SUBLANES + n * (B * half)
        return (c // SUBLANES) % 2, pl.ds(start, B, stride=half)

    def class_rows(nat, split, cfg, r, c, n):
        if r == 1:
            return nat[pl.ds(pl.multiple_of(n * B, B), B), :]
        if cfg == DIL_SPLIT_CFG:
            parity, rows = split_rows(r, c, n)
            return split[parity, rows, :]
        return nat[pl.ds(c + n * (B * r), B, stride=r), :]

    for cfg, (window, r) in enumerate(DIL_CONFIGS):
        nb = seq // r // B
        cs = nb + 1
        bias_prev, bias_cur = [], []
        for hh in range(2):
            slope = slope_ref[2 * pair + hh] * (float(r) * LOG2E)
            bias = dist * (-slope)
            bias_prev.append(jnp.where(upper, bias, jnp.where(eye, -slope * float(B), neg)))
            bias_cur.append(jnp.where(upper, neg, bias))

        for c in range(r):
            rows0 = slice(c * cs * B, (c * cs + 1) * B)
            kd[rows0, :] = zero_tile
            va[rows0, :] = zero_tile
            vb[rows0, :] = zero_tile

        def prep(t4, carry, cfg=cfg, r=r, nb=nb, cs=cs):
            for j in range(4):
                t = t4 * 4 + j
                c = t // nb
                n = t % nb
                dst = pl.ds(pl.multiple_of((c * cs + 1 + n) * B, B), B)
                kd[dst, :] = class_rows(kf, None, cfg, r, c, n).astype(BF16)
                v = class_rows(vf, None, cfg, r, c, n)
                va[dst, :] = jnp.where(first, v, 1.0).astype(BF16)
                vb[dst, :] = jnp.where(first, 1.0, v).astype(BF16)
            return carry

        def prep_split(i, carry, r=r, cs=cs):
            reg = pl.ds(pl.multiple_of(i * (B * r), B * r), B * r)
            kx = kf[reg, :].reshape(B, r, LANES).swapaxes(0, 1)
            vx = vf[reg, :].reshape(B, r, LANES).swapaxes(0, 1)
            for c in range(r):
                dst = pl.ds(pl.multiple_of((c * cs + 1 + i) * B, B), B)
                kd[dst, :] = kx[c].astype(BF16)
                va[dst, :] = jnp.where(first, vx[c], 1.0).astype(BF16)
                vb[dst, :] = jnp.where(first, 1.0, vx[c]).astype(BF16)
            return carry

        if cfg == DIL_SPLIT_CFG:
            lax.fori_loop(0, nb, prep_split, 0)
        else:
            lax.fori_loop(0, seq // B // 4, prep, 0)

        def geom(it, r=r, nb=nb, cs=cs):
            c = it // nb
            n = it % nb
            start = c + n * (B * r)
            rows = pl.ds(start, B, stride=r) if r > 1 else pl.ds(pl.multiple_of(start, B), B)
            kv = pl.ds(pl.multiple_of((c * cs + n) * B, B), 2 * B)
            return rows, kv, c, n

        def body(t, carry, cfg=cfg, r=r, nb=nb, bias_prev=bias_prev, bias_cur=bias_cur, geom=geom):
            geo = [geom(t * U + u) for u in range(U)]
            scores = []
            for u in range(U):
                _, kv, c, n = geo[u]
                q = class_rows(qf, qs, cfg, r, c, n)
                q_heads = jnp.concatenate([jnp.where(first, q, 0.0), jnp.where(first, 0.0, q)], axis=0)
                scores.append(_dot_nt(q_heads.astype(BF16), kd[kv, :]))
            probs, maxes = [], []
            for u in range(U):
                if nb % U == 0:
                    has_prev = True if u > 0 else (t * U) % nb > 0
                else:
                    assert U % nb == 0
                    has_prev = (u % nb) > 0
                for hh in range(2):
                    s2 = scores[u][hh * B:(hh + 1) * B]
                    if has_prev is True:
                        bp = bias_prev[hh]
                    elif has_prev is False:
                        bp = neg_tile
                    else:
                        bp = jnp.where(has_prev, bias_prev[hh], neg)
                    s_prev = s2[:, 0:B] + bp
                    s_cur = s2[:, B:2 * B] + bias_cur[hh]
                    m = jnp.max(jnp.maximum(s_prev, s_cur), axis=-1, keepdims=True)
                    probs.append(jnp.concatenate([jnp.exp2(s_prev - m), jnp.exp2(s_cur - m)], axis=1).astype(BF16))
                    maxes.append(m)
            for u in range(U):
                rows, kv, c, n = geo[u]
                acc0 = _dot(probs[2 * u], va[kv, :])
                acc1 = _dot(probs[2 * u + 1], vb[kv, :])
                num = jnp.where(first, acc0, acc1)
                den = pltpu.roll(jnp.where(first, acc1, acc0), DIL_DH, axis=1)
                out = num * (1.0 / den)
                lse = jnp.where(first, maxes[2 * u], maxes[2 * u + 1]) + jnp.log2(den)
                if cfg == DIL_SPLIT_CFG:
                    parity, dst = split_rows(r, c, n)
                    ocs[parity, dst, :] = out
                    lcs[parity, dst, :] = lse
                else:
                    oc[cfg, rows, :] = out
                    lc[cfg, rows, :] = lse
            return carry

        lax.fori_loop(0, seq // B // U, body, 0)

    g = g_ref[...]
    CH = 2048

    def mix(i, carry):
        rows = pl.ds(pl.multiple_of(i * CH, CH), CH)
        half_rows = pl.ds(pl.multiple_of(i * (CH // 2), CH // 2), CH // 2)

        def tiles_interleaved(split):
            even = split[0, half_rows, :].reshape(CH // (2 * SUBLANES), 1, SUBLANES, LANES)
            odd = split[1, half_rows, :].reshape(CH // (2 * SUBLANES), 1, SUBLANES, LANES)
            return jnp.concatenate([even, odd], axis=1).reshape(CH, LANES)

        l0, l1, l2 = lc[0, rows, :], lc[1, rows, :], tiles_interleaved(lcs)
        m = jnp.maximum(jnp.maximum(l0, l1), l2)
        e0, e1, e2 = jnp.exp2(l0 - m), jnp.exp2(l1 - m), jnp.exp2(l2 - m)
        inv = 1.0 / (e0 + e1 + e2)
        o = (e0 * oc[0, rows, :] + e1 * oc[1, rows, :] + e2 * tiles_interleaved(ocs)) * inv
        sq = o * o
        ms_a = jnp.sum(jnp.where(first, sq, 0.0), axis=-1, keepdims=True) * (1.0 / DIL_DH)
        ms_b = jnp.sum(jnp.where(first, 0.0, sq), axis=-1, keepdims=True) * (1.0 / DIL_DH)
        ms = jnp.where(first, ms_a, ms_b)
        o_ref[rows, :] = (o * lax.rsqrt(ms + EPS) * g).astype(BF16)
        return carry

    lax.fori_loop(0, seq // CH, mix, 0)


def _dilated(slopes, dq, dk, dv, g2, batch, seq):
    blk = pl.BlockSpec((seq, LANES), lambda b, p, s: (b, p))
    return pl.pallas_call(
        functools.partial(_dil_kernel, seq=seq),
        grid_spec=pltpu.PrefetchScalarGridSpec(
            num_scalar_prefetch=1,
            grid=(batch, DIL_WIDTH // LANES),
            in_specs=[blk, blk, blk, pl.BlockSpec((1, LANES), lambda b, p, s: (0, 0))],
            out_specs=blk,
            scratch_shapes=[pltpu.VMEM((seq, LANES), F32)] * 3 + [pltpu.VMEM((2, seq // 2, LANES), F32)]
                           + [pltpu.VMEM((seq + DIL_PAD, LANES), BF16)] * 3
                           + [pltpu.VMEM((2, seq, LANES), F32)] * 2 + [pltpu.VMEM((2, seq // 2, LANES), F32)] * 2,
        ),
        out_shape=jax.ShapeDtypeStruct((batch * seq, DIL_WIDTH), BF16),
        compiler_params=pltpu.CompilerParams(
            dimension_semantics=("parallel", "parallel"), vmem_limit_bytes=VMEM_LIMIT),
        name="dilated",
    )(slopes, dq, dk, dv, g2)


def _out_proj_kernel(og_ref, od_ref, x_ref, wg_ref, wd_ref, g_ref, b_ref,
                     rw_ref, rb_ref, h_ref, hp_ref, eid_ref, gate_ref, cnt_ref, wg_s, wd_s):
    @pl.when(pl.program_id(0) == 0)
    def _():
        wg_s[...] = wg_ref[...].astype(BF16)
        wd_s[...] = wd_ref[...].astype(BF16)

    mix = _dot(og_ref[...], wg_s[...]) + _dot(od_ref[...], wd_s[...])
    h = _layer_norm(DEEPNORM_ALPHA * x_ref[...] + mix, g_ref[...], b_ref[...])
    h_ref[...] = h
    hp_ref[...] = _pack_bf16_halves(h)
    lt = _dot_nt(rw_ref[...], h) + rb_ref[...]
    tm = lt.shape[1]
    row = lax.broadcasted_iota(jnp.int32, (EXPERTS_PER_GROUP, tm), 0).astype(F32)
    neg = jnp.float32(-jnp.inf)
    big = jnp.float32(1e9)
    coarse = jnp.where(row < N_GROUPS, lt[N_EXPERTS:N_EXPERTS + EXPERTS_PER_GROUP, :], neg)
    cmax = jnp.max(coarse, axis=0, keepdims=True)
    g_idx = jnp.min(jnp.where(coarse == cmax, row, big), axis=0, keepdims=True)
    p_group = 1.0 / jnp.sum(jnp.exp(coarse - cmax), axis=0, keepdims=True)
    fine = lt[(N_GROUPS - 1) * EXPERTS_PER_GROUP:N_EXPERTS, :]
    for g in range(N_GROUPS - 2, -1, -1):
        fine = jnp.where(g_idx == g, lt[g * EXPERTS_PER_GROUP:(g + 1) * EXPERTS_PER_GROUP, :], fine)
    v1 = jnp.max(fine, axis=0, keepdims=True)
    i1 = jnp.min(jnp.where(fine == v1, row, big), axis=0, keepdims=True)
    fine2 = jnp.where(row == i1, neg, fine)
    v2 = jnp.max(fine2, axis=0, keepdims=True)
    i2 = jnp.min(jnp.where(fine2 == v2, row, big), axis=0, keepdims=True)
    e2 = jnp.exp(v2 - v1)
    den = 1.0 + e2
    gate1 = p_group * (1.0 / den)
    gate2 = p_group * (e2 / den)
    id1 = g_idx * EXPERTS_PER_GROUP + i1
    id2 = g_idx * EXPERTS_PER_GROUP + i2
    eid_ref[...] = jnp.concatenate([id1, id2], axis=0).astype(jnp.int32)
    slab = jnp.concatenate([gate1, gate2, jnp.zeros((LANES - 2, tm), F32)], axis=0)
    gate_ref[...] = slab.T[:, 0:2]
    sub = lax.broadcasted_iota(jnp.int32, (LANES, tm), 0).astype(F32)
    onehot = jnp.logical_or(sub == id1, sub == id2).astype(BF16)

    @pl.when(pl.program_id(0) == 0)
    def _():
        cnt_ref[...] = jnp.zeros_like(cnt_ref)

    cnt_ref[...] += _dot(onehot, jnp.ones((tm, LANES), BF16))


def _out_proj(og, od, x2, w_out, layer, g, b, rw, rb, tm=1024):
    T = x2.shape[0]
    row = lambda w: pl.BlockSpec((tm, w), lambda i: (i, 0))
    full = lambda a: pl.BlockSpec(a.shape, lambda i: (0,) * a.ndim)
    half = lambda p: pl.BlockSpec((None, GLA_WIDTH, D_MODEL), lambda i: (layer, p, 0))
    return pl.pallas_call(
        _out_proj_kernel,
        grid=(T // tm,),
        in_specs=[row(GLA_WIDTH), row(DIL_WIDTH), row(D_MODEL), half(0), half(1), full(g), full(b),
                  full(rw), full(rb)],
        out_specs=[row(D_MODEL), row(D_MODEL // 2), pl.BlockSpec((2, tm), lambda i: (0, i)), row(2),
                   pl.BlockSpec((LANES, LANES), lambda i: (0, 0))],
        out_shape=[jax.ShapeDtypeStruct((T, D_MODEL), F32),
                   jax.ShapeDtypeStruct((T, D_MODEL // 2), jnp.int32),
                   jax.ShapeDtypeStruct((2, T), jnp.int32),
                   jax.ShapeDtypeStruct((T, 2), F32),
                   jax.ShapeDtypeStruct((LANES, LANES), F32)],
        scratch_shapes=[pltpu.VMEM((GLA_WIDTH, D_MODEL), BF16), pltpu.VMEM((DIL_WIDTH, D_MODEL), BF16)],
        compiler_params=pltpu.CompilerParams(
            dimension_semantics=("arbitrary",), vmem_limit_bytes=VMEM_LIMIT),
        name="out_proj_router",
    )(og, od, x2, w_out, w_out, g, b, rw, rb)


def _positions_kernel(eid_ref, cnt_ref, dest_ref, be_ref, nv_ref, carry_ref, sp_ref, tri_ref, *, tb):
    i = pl.program_id(0)

    @pl.when(i == 0)
    def _():
        shift = int(math.log2(ROW_BLOCK))
        nb_col = (cnt_ref[...].astype(jnp.int32) + (ROW_BLOCK - 1)) >> shift
        r = lax.broadcasted_iota(jnp.int32, (LANES, LANES), 0)
        c = lax.broadcasted_iota(jnp.int32, (LANES, LANES), 1)
        nb_f = jnp.where(r < N_EXPERTS, nb_col, 0).astype(F32)
        start_col = _dot((c < r).astype(BF16), nb_f.astype(BF16))
        sp_ref[...] = start_col * float(ROW_BLOCK)
        carry_ref[...] = jnp.zeros_like(carry_ref)
        be_ref[...] = jnp.concatenate([start_col.T[0:1, :], nb_f.T[0:1, :]], axis=1).astype(jnp.int32)
        total = jnp.sum(nb_f[:, 0:1], axis=0, keepdims=True)
        nv_ref[...] = jnp.broadcast_to(total, (1, LANES)).astype(jnp.int32)
        tr = lax.broadcasted_iota(jnp.int32, (tb, tb), 0)
        tc = lax.broadcasted_iota(jnp.int32, (tb, tb), 1)
        tri_ref[...] = (tr < tc).astype(BF16)

    sub = lax.broadcasted_iota(jnp.int32, (LANES, tb), 0)
    oh1 = sub == eid_ref[0:1, :]
    oh2 = sub == eid_ref[1:2, :]
    oh = jnp.logical_or(oh1, oh2).astype(BF16)
    offset = jnp.tile(carry_ref[...] + sp_ref[...], (1, tb // LANES))
    before = _dot(oh, tri_ref[...]) + offset
    d1 = jnp.sum(jnp.where(oh1, before, 0.0), axis=0, keepdims=True)
    d2 = jnp.sum(jnp.where(oh2, before, 0.0), axis=0, keepdims=True)
    dest_ref[...] = jnp.concatenate([d1, d2], axis=0).astype(jnp.int32)
    carry_ref[...] += _dot(oh, jnp.ones((tb, LANES), BF16))


def _positions(eid_t, cnt, n_blocks_pad, tb=1024):
    T = eid_t.shape[1]
    return pl.pallas_call(
        functools.partial(_positions_kernel, tb=tb),
        grid=(T // tb,),
        in_specs=[pl.BlockSpec((2, tb), lambda i: (0, i)), pl.BlockSpec((LANES, LANES), lambda i: (0, 0))],
        out_specs=[pl.BlockSpec((2, tb), lambda i: (0, i)),
                   pl.BlockSpec((1, n_blocks_pad), lambda i: (0, 0)),
                   pl.BlockSpec((1, LANES), lambda i: (0, 0))],
        out_shape=[jax.ShapeDtypeStruct((2, T), jnp.int32),
                   jax.ShapeDtypeStruct((1, n_blocks_pad), jnp.int32),
                   jax.ShapeDtypeStruct((1, LANES), jnp.int32)],
        scratch_shapes=[pltpu.VMEM((LANES, LANES), F32), pltpu.VMEM((LANES, LANES), F32),
                        pltpu.VMEM((tb, tb), BF16)],
        compiler_params=pltpu.CompilerParams(dimension_semantics=("arbitrary",)),
        name="positions",
    )(eid_t, cnt)


def _sc_gather_rows(table, idx):
    n = idx.shape[0]
    d = table.shape[1]
    info = plsc.get_sparse_core_info()
    nc, ns = info.num_cores, info.num_subcores
    per_w = n // (nc * ns)
    assert per_w * nc * ns == n and per_w % SC_INDEX_WINDOW == 0
    mesh = plsc.VectorSubcoreMesh(core_axis_name="core", subcore_axis_name="subcore")
    nchunk = per_w // SC_GATHER_ROWS
    nbuf = SC_GATHER_BUFFERS

    @functools.partial(
        pl.kernel, out_type=jax.ShapeDtypeStruct((n, d), table.dtype), mesh=mesh,
        scratch_types=[pltpu.VMEM((per_w,), jnp.int32),
                       pltpu.VMEM((nbuf, SC_GATHER_ROWS, d), table.dtype),
                       pltpu.SemaphoreType.DMA((nbuf,)), pltpu.SemaphoreType.DMA((nbuf,))],
        name="sc_gather_rows")
    def gather(x_hbm, i_hbm, o_hbm, idx_v, buf, gsem, wsem):
        wid = lax.axis_index("subcore") * nc + lax.axis_index("core")
        base = wid * per_w
        pltpu.sync_copy(i_hbm.at[pl.ds(base, per_w)], idx_v)

        def gather_copy(c):
            rows = idx_v.at[pl.ds(c * SC_GATHER_ROWS, SC_GATHER_ROWS)]
            return pltpu.make_async_copy(x_hbm.at[rows], buf.at[c % nbuf], gsem.at[c % nbuf])

        def write_copy(c):
            dst = o_hbm.at[pl.ds(base + c * SC_GATHER_ROWS, SC_GATHER_ROWS)]
            return pltpu.make_async_copy(buf.at[c % nbuf], dst, wsem.at[c % nbuf])

        for c in range(min(nbuf - 1, nchunk)):
            gather_copy(c).start()
        for c in range(nchunk):
            gather_copy(c).wait()
            write_copy(c).start()
            if c + nbuf - 1 < nchunk:
                if c >= 1:
                    write_copy(c - 1).wait()
                gather_copy(c + nbuf - 1).start()
        for c in range(max(0, nchunk - nbuf), nchunk):
            write_copy(c).wait()

    return gather(table, idx)


def _sc_scatter_rows(table, dest_flat, n_rows):
    n_tok, d = table.shape
    assert dest_flat.shape[0] == 2 * n_tok
    info = plsc.get_sparse_core_info()
    nc, ns = info.num_cores, info.num_subcores
    per_w = n_tok // (nc * ns)
    assert per_w * nc * ns == n_tok and per_w % SC_INDEX_WINDOW == 0
    mesh = plsc.VectorSubcoreMesh(core_axis_name="core", subcore_axis_name="subcore")
    nchunk = per_w // SC_GATHER_ROWS
    nbuf = SC_GATHER_BUFFERS

    @functools.partial(
        pl.kernel, out_type=jax.ShapeDtypeStruct((n_rows, d), table.dtype), mesh=mesh,
        scratch_types=[pltpu.VMEM((per_w,), jnp.int32), pltpu.VMEM((per_w,), jnp.int32),
                       pltpu.VMEM((nbuf, SC_GATHER_ROWS, d), table.dtype),
                       pltpu.SemaphoreType.DMA((nbuf,)), pltpu.SemaphoreType.DMA((nbuf, 2))],
        name="sc_scatter_rows")
    def scatter(x_hbm, i_hbm, o_hbm, idx_a, idx_b, buf, rsem, wsem):
        wid = lax.axis_index("subcore") * nc + lax.axis_index("core")
        base = wid * per_w
        pltpu.sync_copy(i_hbm.at[pl.ds(base, per_w)], idx_a)
        pltpu.sync_copy(i_hbm.at[pl.ds(n_tok + base, per_w)], idx_b)

        def read_copy(c):
            src = x_hbm.at[pl.ds(base + c * SC_GATHER_ROWS, SC_GATHER_ROWS)]
            return pltpu.make_async_copy(src, buf.at[c % nbuf], rsem.at[c % nbuf])

        def scatter_copies(c):
            window = pl.ds(c * SC_GATHER_ROWS, SC_GATHER_ROWS)
            return [pltpu.make_async_copy(buf.at[c % nbuf], o_hbm.at[idx.at[window]], wsem.at[c % nbuf, k])
                    for k, idx in enumerate((idx_a, idx_b))]

        for c in range(min(nbuf - 1, nchunk)):
            read_copy(c).start()
        for c in range(nchunk):
            read_copy(c).wait()
            for cp in scatter_copies(c):
                cp.start()
            if c + nbuf - 1 < nchunk:
                if c >= 1:
                    for cp in scatter_copies(c - 1):
                        cp.wait()
                read_copy(c + nbuf - 1).start()
        for c in range(max(0, nchunk - nbuf), nchunk):
            for cp in scatter_copies(c):
                cp.wait()

    return scatter(table, dest_flat)


def _ffn_kernel(first_ref, count_ref, nv_ref, wg_hbm, wu_hbm, wd_hbm, xs_hbm, y_hbm,
                wg32, wu32, wd32, xbuf, ybuf, wsem, isem, osem, *, n_blocks):
    nv = nv_ref[0]
    nbuf = FFN_BUFFERS

    def next_expert(e):
        def more(t):
            return jnp.logical_and(t < N_EXPERTS, count_ref[jnp.minimum(t, N_EXPERTS - 1)] == 0)
        return lax.while_loop(more, lambda t: t + 1, e + 1)

    def weight_copies(e, slot):
        ee = jnp.minimum(e, N_EXPERTS - 1)
        return [pltpu.make_async_copy(src.at[ee], dst.at[slot], wsem.at[slot, i])
                for i, (src, dst) in enumerate(((wg_hbm, wg32), (wu_hbm, wu32), (wd_hbm, wd32)))]

    def fetch_weights(e, slot):
        @pl.when(e < N_EXPERTS)
        def _():
            for c in weight_copies(e, slot):
                c.start()

    def take_weights(e, slot):
        for c in weight_copies(e, slot):
            c.wait()
        fetch_weights(next_expert(e), 1 - slot)

    def rows_of(b):
        return pl.ds(pl.multiple_of(b * ROW_BLOCK, ROW_BLOCK), ROW_BLOCK)

    def buf_rows(b, nblk):
        return pl.ds(pl.multiple_of((b % nbuf) * ROW_BLOCK, ROW_BLOCK), nblk * ROW_BLOCK)

    def in_copy(b):
        return pltpu.make_async_copy(xs_hbm.at[rows_of(b)], xbuf.at[buf_rows(b, 1)], isem.at[b % nbuf])

    def out_copy(b):
        return pltpu.make_async_copy(ybuf.at[buf_rows(b, 1)], y_hbm.at[rows_of(b)], osem.at[b % nbuf])

    def expert_mlp(words, slot):
        x = jnp.concatenate(_unpack_bf16_halves(words), axis=1)
        a = _dot(x, wg32[slot])
        u = _dot(x, wu32[slot])
        return _pack_bf16_halves(_dot(a * jax.nn.sigmoid(a) * u, wd32[slot]))

    ahead = nbuf - max(FFN_GROUPS)

    @pl.when(nv > 0)
    def _():
        e0 = next_expert(jnp.int32(-1))
        for i in range(ahead):
            @pl.when(i < nv)
            def _():
                in_copy(i).start()
        fetch_weights(e0, 0)
        take_weights(e0, 0)

        def step(carry):
            b, e, k = carry
            switch = b >= first_ref[e] + count_ref[e]
            e_new = jnp.where(switch, next_expert(e), e)
            k_new = jnp.where(switch, k + 1, k)

            @pl.when(switch)
            def _():
                take_weights(e_new, k_new % 2)

            end = first_ref[e_new] + count_ref[e_new]
            n = jnp.int32(1)
            for size in FFN_GROUPS:
                fits = jnp.logical_and(b + size <= end, b % nbuf + size <= nbuf)
                n = jnp.where(jnp.logical_and(n == 1, fits), size, n)
            for i in range(max(FFN_GROUPS)):
                @pl.when(jnp.logical_and(i < n, b + ahead + i < nv))
                def _():
                    in_copy(b + ahead + i).start()
            for i in range(max(FFN_GROUPS)):
                @pl.when(i < n)
                def _():
                    in_copy(b + i).wait()

                    @pl.when(b + i >= nbuf)
                    def _():
                        out_copy(b + i - nbuf).wait()

            for size in FFN_GROUPS + (1,):
                @pl.when(n == size)
                def _():
                    ybuf[buf_rows(b, size), :] = expert_mlp(xbuf[buf_rows(b, size), :], k_new % 2)

            for i in range(max(FFN_GROUPS)):
                @pl.when(i < n)
                def _():
                    out_copy(b + i).start()
            return b + n, e_new, k_new

        lax.while_loop(lambda c: c[0] < nv, step, (jnp.int32(0), e0, jnp.int32(0)))

        for i in range(nbuf):
            @pl.when(nv > i)
            def _():
                out_copy(nv - 1 - i).wait()

    ybuf[0:ROW_BLOCK, :] = jnp.zeros((ROW_BLOCK, D_MODEL // 2), jnp.int32)

    def fill(b, carry):
        pltpu.sync_copy(ybuf.at[pl.ds(0, ROW_BLOCK)], y_hbm.at[rows_of(b)])
        return carry

    lax.fori_loop(nv, n_blocks, fill, 0)


def _ffn(first_blk, n_blk, nv, xs, w_gate, w_up, w_down):
    n_rows = xs.shape[0]
    n_blocks = n_rows // ROW_BLOCK
    anyspec = pl.BlockSpec(memory_space=pl.ANY)
    return pl.pallas_call(
        functools.partial(_ffn_kernel, n_blocks=n_blocks),
        grid_spec=pltpu.PrefetchScalarGridSpec(
            num_scalar_prefetch=3,
            grid=(1,),
            in_specs=[anyspec, anyspec, anyspec, anyspec],
            out_specs=anyspec,
            scratch_shapes=[pltpu.VMEM((2, D_MODEL, D_FF), F32), pltpu.VMEM((2, D_MODEL, D_FF), F32),
                            pltpu.VMEM((2, D_FF, D_MODEL), F32),
                            pltpu.VMEM((FFN_BUFFERS * ROW_BLOCK, D_MODEL // 2), jnp.int32),
                            pltpu.VMEM((FFN_BUFFERS * ROW_BLOCK, D_MODEL // 2), jnp.int32),
                            pltpu.SemaphoreType.DMA((2, 3)),
                            pltpu.SemaphoreType.DMA((FFN_BUFFERS,)),
                            pltpu.SemaphoreType.DMA((FFN_BUFFERS,))],
        ),
        out_shape=jax.ShapeDtypeStruct((n_rows, D_MODEL // 2), jnp.int32),
        compiler_params=pltpu.CompilerParams(
            dimension_semantics=("arbitrary",), vmem_limit_bytes=VMEM_LIMIT),
        name="expert_ffn",
    )(first_blk, n_blk, nv, w_gate, w_up, w_down, xs)


def _combine_kernel(h_ref, ya_ref, yb_ref, gate_ref, g_ref, b_ref, o_ref):
    gate = gate_ref[...]
    a_hi, a_lo = _unpack_bf16_halves(ya_ref[...])
    b_hi, b_lo = _unpack_bf16_halves(yb_ref[...])
    g0, g1 = gate[:, 0:1], gate[:, 1:2]
    ffn = jnp.concatenate([a_hi * g0 + b_hi * g1, a_lo * g0 + b_lo * g1], axis=1)
    o_ref[...] = _layer_norm(DEEPNORM_ALPHA * h_ref[...] + ffn, g_ref[...], b_ref[...])


def _combine(h, y2, gate, g, b, tm=1024):
    T = h.shape[0]
    nt = T // tm
    return pl.pallas_call(
        _combine_kernel,
        grid=(nt,),
        in_specs=[pl.BlockSpec((tm, D_MODEL), lambda i: (i, 0)),
                  pl.BlockSpec((tm, D_MODEL // 2), lambda i: (i, 0)),
                  pl.BlockSpec((tm, D_MODEL // 2), lambda i: (i + nt, 0)),
                  pl.BlockSpec((tm, 2), lambda i: (i, 0)),
                  pl.BlockSpec((1, D_MODEL), lambda i: (0, 0)),
                  pl.BlockSpec((1, D_MODEL), lambda i: (0, 0))],
        out_specs=pl.BlockSpec((tm, D_MODEL), lambda i: (i, 0)),
        out_shape=jax.ShapeDtypeStruct((T, D_MODEL), F32),
        compiler_params=pltpu.CompilerParams(
            dimension_semantics=("parallel",), vmem_limit_bytes=VMEM_LIMIT),
        name="combine",
    )(h, y2, y2, gate, g, b)


def kernel(x, w_in, gla_gate_w2, gla_gate_b, gla_norm_g, dil_norm_g, w_out, ln1_g, ln1_b,
           router_coarse_w, router_coarse_b, router_fine_w, router_fine_b,
           expert_w_gate, expert_w_up, expert_w_down, ln2_g, ln2_b):
    B, S, D = x.shape
    T = B * S
    depth = w_in.shape[0]
    slopes = jnp.exp2(-8.0 * jnp.arange(1, DIL_HEADS + 1, dtype=F32) / DIL_HEADS)
    n_rows = 2 * T + N_EXPERTS * ROW_BLOCK
    n_blocks_pad = 2 * LANES
    h = x.reshape(T, D)
    w_in_t = jnp.swapaxes(w_in, 1, 2)
    for l in range(depth):
        w2 = jnp.pad(gla_gate_w2[l], ((0, LANES - GLA_GATE_RANK), (0, 0)))
        q, k, v, r, la, dq, dk, dv = _in_proj(h, w_in_t, l, w2, gla_gate_b[l][None, :])
        o_gla = _gla(q, k, v, r, la, gla_norm_g[l][None, :], B, S)
        g2 = jnp.tile(dil_norm_g[l], 2)[None, :]
        o_dil = _dilated(slopes, dq, dk, dv, g2, B, S)
        rw = jnp.concatenate([router_fine_w[l].reshape(D, N_EXPERTS), router_coarse_w[l]], axis=1)
        rw = jnp.pad(rw, ((0, 0), (0, LANES - N_EXPERTS - N_GROUPS))).T
        rb = jnp.concatenate([router_fine_b[l].reshape(N_EXPERTS), router_coarse_b[l]])
        rb = jnp.pad(rb, (0, LANES - N_EXPERTS - N_GROUPS))[:, None]
        h1, h1p, eid_t, gate, cnt = _out_proj(o_gla, o_dil, h, w_out, l,
                                              ln1_g[l][None, :], ln1_b[l][None, :], rw, rb)
        dest_t, be, nv = _positions(eid_t, cnt, n_blocks_pad)
        dest_flat = dest_t.reshape(2 * T)
        xs = _sc_scatter_rows(h1p, dest_flat, n_rows)
        be = be.reshape(n_blocks_pad)
        y = _ffn(be[:N_EXPERTS], be[LANES:LANES + N_EXPERTS], nv.reshape(LANES)[:1], xs,
                 expert_w_gate[l], expert_w_up[l], expert_w_down[l])
        y2 = _sc_gather_rows(y, dest_flat)
        h = _combine(h1, y2, gate, ln2_g[l][None, :], ln2_b[l][None, :])
    return h.reshape(B, S, D)
```

```python
import functools
import math

import jax
import jax.numpy as jnp
from jax import lax
from jax.experimental import pallas as pl
from jax.experimental.pallas import tpu as pltpu
from jax.experimental.pallas import tpu_sc as plsc

D_MODEL = 1024
GLA_HEADS = 4
GLA_DK = 64
GLA_DV = 128
GLA_KEY_WIDTH = GLA_HEADS * GLA_DK
GLA_WIDTH = GLA_HEADS * GLA_DV
GLA_GATE_RANK = 16
GLA_GATE_TEMP = 16.0
DIL_HEADS = 8
DIL_DH = 64
DIL_WIDTH = DIL_HEADS * DIL_DH
DIL_CONFIGS = ((128, 1), (512, 4), (2048, 16))
DIL_BLOCK = 128
DIL_MAX_R = max(r for _, r in DIL_CONFIGS)
DIL_PAD = DIL_BLOCK * DIL_MAX_R
DIL_UNROLL = 16
DIL_SPLIT_CFG = 2
N_GROUPS = 4
EXPERTS_PER_GROUP = 8
N_EXPERTS = N_GROUPS * EXPERTS_PER_GROUP
D_FF = 512
DEEPNORM_ALPHA = 2.0 ** 0.25
EPS = 1e-5
IN_PROJ_GATE_COL = 2 * GLA_KEY_WIDTH + 2 * GLA_WIDTH
LOG2E = math.log2(math.e)

LANES = 128
SUBLANES = 8
assert DIL_SPLIT_CFG == len(DIL_CONFIGS) - 1 and DIL_CONFIGS[DIL_SPLIT_CFG][1] % (2 * SUBLANES) == 0
GLA_CHUNK = 128
GLA_SUB = 64
GLA_UNROLL = 4
SC_INDEX_WINDOW = 128
SC_GATHER_BUFFERS = 6
SC_GATHER_ROWS = 32
FFN_BUFFERS = 12
FFN_GROUPS = (4, 2)
ROW_BLOCK = 256
VMEM_LIMIT = 56 * 1024 * 1024

F32 = jnp.float32
BF16 = jnp.bfloat16


def _dot(a, b):
    return jnp.dot(a, b, preferred_element_type=F32)


def _dot_nt(a, b):
    return lax.dot_general(a, b, (((1,), (1,)), ((), ())), preferred_element_type=F32)


def _dot_tn(a, b):
    return lax.dot_general(a, b, (((0,), (0,)), ((), ())), preferred_element_type=F32)


def _split_bf16(v):
    hi = v.astype(BF16)
    lo = (v - hi.astype(F32)).astype(BF16)
    return hi, lo


def _pack_bf16_halves(v):
    w = v.shape[1] // 2
    hi = lax.bitcast_convert_type(v[:, :w].astype(BF16).astype(F32), jnp.int32)
    lo = lax.bitcast_convert_type(v[:, w:].astype(BF16).astype(F32), jnp.int32)
    return hi | lax.shift_right_logical(lo, 16)


def _unpack_bf16_halves(words):
    hi = lax.bitcast_convert_type(words & jnp.int32(-65536), F32)
    lo = lax.bitcast_convert_type(lax.shift_left(words, 16), F32)
    return hi, lo


def _layer_norm(v, g, b):
    mu = jnp.mean(v, axis=-1, keepdims=True)
    c = v - mu
    var = jnp.mean(c * c, axis=-1, keepdims=True)
    return c * lax.rsqrt(var + EPS) * g + b


def _in_proj_kernel(x_ref, w_ref, w2_ref, gb_ref,
                    q_ref, k_ref, v_ref, r_ref, la_ref, dq_ref, dk_ref, dv_ref, wg_s, wd_s, wa_s):
    a0 = IN_PROJ_GATE_COL

    @pl.when(pl.program_id(0) == 0)
    def _():
        wg_s[...] = w_ref[0:a0, :].T.astype(BF16)
        gate_tile = w_ref[a0:a0 + LANES, :].T
        lane = lax.broadcasted_iota(jnp.int32, gate_tile.shape, 1)
        wa_s[...] = jnp.where(lane < GLA_GATE_RANK, gate_tile, 0.0).astype(BF16)
        wd_s[...] = w_ref[a0 + GLA_GATE_RANK:, :].T.astype(BF16)

    xb = x_ref[...].astype(BF16)

    def piece(w_s, c0, c1):
        return _dot(xb, w_s[:, c0:c1])

    q_ref[...] = (piece(wg_s, 0, 256) * (GLA_DK ** -0.5)).astype(BF16)
    k_ref[...] = piece(wg_s, 256, 512).astype(BF16)
    v_ref[...] = piece(wg_s, 512, 1024).astype(BF16)
    r_ref[...] = piece(wg_s, 1024, 1536).astype(BF16)
    dq_ref[...] = (piece(wd_s, 0, 512) * (DIL_DH ** -0.5 * LOG2E)).astype(BF16)
    dk_ref[...] = piece(wd_s, 512, 1024).astype(BF16)
    dv_ref[...] = piece(wd_s, 1024, 1536).astype(BF16)
    ga = _dot(xb, wa_s[...])
    z = _dot(ga, w2_ref[...]) + gb_ref[...]
    log_sig = jnp.minimum(z, 0.0) - jnp.log1p(jnp.exp(-jnp.abs(z)))
    la_ref[...] = log_sig * (1.0 / GLA_GATE_TEMP)


def _in_proj(x2, w_in_t, layer, w2, gb, tm=1024):
    T = x2.shape[0]
    row = lambda wd: pl.BlockSpec((tm, wd), lambda i: (i, 0))
    full = lambda a: pl.BlockSpec(a.shape, lambda i: (0,) * a.ndim)
    outs = [(GLA_KEY_WIDTH, BF16), (GLA_KEY_WIDTH, BF16), (GLA_WIDTH, BF16), (GLA_WIDTH, BF16), (GLA_KEY_WIDTH, F32),
            (DIL_WIDTH, BF16), (DIL_WIDTH, BF16), (DIL_WIDTH, BF16)]
    group = w_in_t.shape[1] - IN_PROJ_GATE_COL - GLA_GATE_RANK
    return pl.pallas_call(
        _in_proj_kernel,
        grid=(T // tm,),
        in_specs=[row(D_MODEL),
                  pl.BlockSpec((None,) + w_in_t.shape[1:], lambda i: (layer, 0, 0), pipeline_mode=pl.Buffered(1)),
                  full(w2), full(gb)],
        out_specs=[row(wd) for wd, _ in outs],
        out_shape=[jax.ShapeDtypeStruct((T, wd), dt) for wd, dt in outs],
        scratch_shapes=[pltpu.VMEM((D_MODEL, IN_PROJ_GATE_COL), BF16), pltpu.VMEM((D_MODEL, group), BF16),
                        pltpu.VMEM((D_MODEL, LANES), BF16)],
        compiler_params=pltpu.CompilerParams(
            dimension_semantics=("arbitrary",), vmem_limit_bytes=VMEM_LIMIT),
        name="in_proj",
    )(x2, w_in_t, w2, gb)


def _gla_kernel(q_ref, k_ref, v_ref, r_ref, la_ref, g_ref, o_ref, s_ref, *, seq_block):
    C = GLA_CHUNK
    H = GLA_SUB
    assert C == 2 * H

    @pl.when(pl.program_id(1) == 0)
    def _():
        s_ref[...] = jnp.zeros_like(s_ref)

    ri = lax.broadcasted_iota(jnp.int32, (C, C), 0)
    ci = lax.broadcasted_iota(jnp.int32, (C, C), 1)
    same_sub = (ri // H) == (ci // H)
    sum_ops = jnp.concatenate([jnp.logical_and(same_sub, ci <= ri).astype(BF16), same_sub.astype(BF16),
                               jnp.ones((C, C), BF16)], axis=0)
    diag_mask = jnp.logical_and(same_sub, ci <= ri)
    off_mask = (ri // H) > (ci // H)
    second = lax.broadcasted_iota(jnp.int32, (C, 1), 0) >= H
    ones_cl = jnp.ones((C, LANES), BF16)
    lane_k = lax.broadcasted_iota(jnp.int32, (1, GLA_KEY_WIDTH), 1) // GLA_DK
    head_masks = [(lane_k == h).astype(F32) for h in range(GLA_HEADS)]
    srow = lax.broadcasted_iota(jnp.int32, (GLA_KEY_WIDTH, GLA_WIDTH), 0) // GLA_DK
    scol = lax.broadcasted_iota(jnp.int32, (GLA_KEY_WIDTH, GLA_WIDTH), 1) // GLA_DV
    state_mask = (srow == scol).astype(F32)
    g = g_ref[...]

    def trip(t, carry):
        U = GLA_UNROLL
        rows = [pl.ds(pl.multiple_of((t * U + u) * C, C), C) for u in range(U)]
        la2s, sums = [], []
        for u in range(U):
            la_hi, la_lo = _split_bf16(la_ref[rows[u], :])
            la2 = jnp.concatenate([la_hi, la_lo], axis=1)
            la2s.append(la2)
            sm = _dot(sum_ops, la2)
            sums.append(sm[:, 0:GLA_KEY_WIDTH] + sm[:, GLA_KEY_WIDTH:])
        q_states, k_states, scores = [], [], []
        for u in range(U):
            b = sums[u][0:C]
            t_sub = sums[u][C:2 * C]
            other = sums[u][2 * C:3 * C] - t_sub
            q = q_ref[rows[u], :].astype(F32)
            k = k_ref[rows[u], :].astype(F32)
            qd = q * jnp.exp(b)
            kd = (k * jnp.exp(-b)).astype(BF16)
            ke_f = k * jnp.exp(t_sub - b)
            cross = jnp.exp(other)
            q_states.append((qd * jnp.where(second, cross, 1.0)).astype(BF16))
            k_states.append((ke_f * jnp.where(second, 1.0, cross)).astype(BF16))
            q_heads = jnp.concatenate([(qd * head_masks[h]).astype(BF16) for h in range(GLA_HEADS)], axis=0)
            keys2 = jnp.concatenate([kd, ke_f.astype(BF16)], axis=0)
            scores.append(_dot_nt(q_heads, keys2))
        decs, upds = [], []
        for u in range(U):
            tot = _dot_tn(la2s[u], ones_cl)
            decs.append(jnp.exp(tot[0:GLA_KEY_WIDTH] + tot[GLA_KEY_WIDTH:]))
            upds.append(_dot_tn(k_states[u], v_ref[rows[u], :]) * state_mask)
        o_inters = []
        for u in range(U):
            state = s_ref[...]
            o_inters.append(_dot(q_states[u], state.astype(BF16)))
            for h in range(GLA_HEADS):
                cols = slice(h * GLA_DV, (h + 1) * GLA_DV)
                s_ref[:, cols] = state[:, cols] * decs[u] + upds[u][:, cols]
        for u in range(U):
            v = v_ref[rows[u], :]
            outs = []
            for h in range(GLA_HEADS):
                sh = scores[u][h * C:(h + 1) * C]
                a = jnp.where(diag_mask, sh[:, 0:C], 0.0) + jnp.where(off_mask, sh[:, C:2 * C], 0.0)
                cols = slice(h * GLA_DV, (h + 1) * GLA_DV)
                o = _dot(a.astype(BF16), v[:, cols]) + o_inters[u][:, cols]
                o = o * lax.rsqrt(jnp.mean(o * o, axis=-1, keepdims=True) + EPS) * g
                outs.append(o)
            o_all = jnp.concatenate(outs, axis=-1)
            rr = r_ref[rows[u], :].astype(F32)
            o_ref[rows[u], :] = (o_all * (rr * jax.nn.sigmoid(rr))).astype(BF16)
        return carry

    lax.fori_loop(0, seq_block // C // GLA_UNROLL, trip, 0)


def _gla(q, k, v, r, la, g, batch, seq, seq_block=1024):
    nsb = seq // seq_block
    row = lambda w: pl.BlockSpec((seq_block, w), lambda b, s: (b * nsb + s, 0))
    return pl.pallas_call(
        functools.partial(_gla_kernel, seq_block=seq_block),
        grid=(batch, nsb),
        in_specs=[row(GLA_KEY_WIDTH), row(GLA_KEY_WIDTH), row(GLA_WIDTH), row(GLA_WIDTH), row(GLA_KEY_WIDTH),
                  pl.BlockSpec((1, GLA_DV), lambda b, s: (0, 0))],
        out_specs=row(GLA_WIDTH),
        out_shape=jax.ShapeDtypeStruct((batch * seq, GLA_WIDTH), BF16),
        scratch_shapes=[pltpu.VMEM((GLA_KEY_WIDTH, GLA_WIDTH), F32)],
        compiler_params=pltpu.CompilerParams(
            dimension_semantics=("parallel", "arbitrary"), vmem_limit_bytes=VMEM_LIMIT),
        name="gla",
    )(q, k, v, r, la, g)


def _dil_kernel(slope_ref, q_ref, k_ref, v_ref, g_ref, o_ref,
                qf, kf, vf, qs, kd, va, vb, oc, lc, ocs, lcs, *, seq):
    B = DIL_BLOCK
    U = DIL_UNROLL
    pair = pl.program_id(1)
    x = q_ref[...].astype(F32)
    qf[...] = x
    x3 = x.reshape(seq // (2 * SUBLANES), 2 * SUBLANES, LANES)
    qs[0] = x3[:, :SUBLANES, :].reshape(seq // 2, LANES)
    qs[1] = x3[:, SUBLANES:, :].reshape(seq // 2, LANES)
    kf[...] = k_ref[...].astype(F32)
    vf[...] = v_ref[...].astype(F32)

    lane = lax.broadcasted_iota(jnp.int32, (1, LANES), 1)
    first = lane < DIL_DH
    ii = lax.broadcasted_iota(jnp.int32, (B, B), 0)
    jj = lax.broadcasted_iota(jnp.int32, (B, B), 1)
    upper = jj > ii
    eye = jj == ii
    dist = jnp.bitwise_and(ii - jj, B - 1).astype(F32)
    neg = jnp.float32(-jnp.inf)
    neg_tile = jnp.full((B, B), neg, F32)
    zero_tile = jnp.zeros((B, LANES), BF16)

    def split_rows(r, c, n):
        half = r // 2
        start = c % SUBLANES + c // (2 * SUBLANES) * SUBLANES + n * (B * half)
        return (c // SUBLANES) % 2, pl.ds(start, B, stride=half)

    def class_rows(nat, split, cfg, r, c, n):
        if r == 1:
            return nat[pl.ds(pl.multiple_of(n * B, B), B), :]
        if cfg == DIL_SPLIT_CFG:
            parity, rows = split_rows(r, c, n)
            return split[parity, rows, :]
        return nat[pl.ds(c + n * (B * r), B, stride=r), :]

    for cfg, (window, r) in enumerate(DIL_CONFIGS):
        nb = seq // r // B
        cs = nb + 1
        bias_prev, bias_cur = [], []
        for hh in range(2):
            slope = slope_ref[2 * pair + hh] * (float(r) * LOG2E)
            bias = dist * (-slope)
            bias_prev.append(jnp.where(upper, bias, jnp.where(eye, -slope * float(B), neg)))
            bias_cur.append(jnp.where(upper, neg, bias))

        for c in range(r):
            rows0 = slice(c * cs * B, (c * cs + 1) * B)
            kd[rows0, :] = zero_tile
            va[rows0, :] = zero_tile
            vb[rows0, :] = zero_tile

        def prep(t4, carry, cfg=cfg, r=r, nb=nb, cs=cs):
            for j in range(4):
                t = t4 * 4 + j
                c = t // nb
                n = t % nb
                dst = pl.ds(pl.multiple_of((c * cs + 1 + n) * B, B), B)
                kd[dst, :] = class_rows(kf, None, cfg, r, c, n).astype(BF16)
                v = class_rows(vf, None, cfg, r, c, n)
                va[dst, :] = jnp.where(first, v, 1.0).astype(BF16)
                vb[dst, :] = jnp.where(first, 1.0, v).astype(BF16)
            return carry

        def prep_split(i, carry, r=r, cs=cs):
            reg = pl.ds(pl.multiple_of(i * (B * r), B * r), B * r)
            kx = kf[reg, :].reshape(B, r, LANES).swapaxes(0, 1)
            vx = vf[reg, :].reshape(B, r, LANES).swapaxes(0, 1)
            for c in range(r):
                dst = pl.ds(pl.multiple_of((c * cs + 1 + i) * B, B), B)
                kd[dst, :] = kx[c].astype(BF16)
                va[dst, :] = jnp.where(first, vx[c], 1.0).astype(BF16)
                vb[dst, :] = jnp.where(first, 1.0, vx[c]).astype(BF16)
            return carry

        if cfg == DIL_SPLIT_CFG:
            lax.fori_loop(0, nb, prep_split, 0)
        else:
            lax.fori_loop(0, seq // B // 4, prep, 0)

        def geom(it, r=r, nb=nb, cs=cs):
            c = it // nb
            n = it % nb
            start = c + n * (B * r)
            rows = pl.ds(start, B, stride=r) if r > 1 else pl.ds(pl.multiple_of(start, B), B)
            kv = pl.ds(pl.multiple_of((c * cs + n) * B, B), 2 * B)
            return rows, kv, c, n

        def body(t, carry, cfg=cfg, r=r, nb=nb, bias_prev=bias_prev, bias_cur=bias_cur, geom=geom):
            geo = [geom(t * U + u) for u in range(U)]
            scores = []
            for u in range(U):
                _, kv, c, n = geo[u]
                q = class_rows(qf, qs, cfg, r, c, n)
                q_heads = jnp.concatenate([jnp.where(first, q, 0.0), jnp.where(first, 0.0, q)], axis=0)
                scores.append(_dot_nt(q_heads.astype(BF16), kd[kv, :]))
            probs, maxes = [], []
            for u in range(U):
                if nb % U == 0:
                    has_prev = True if u > 0 else (t * U) % nb > 0
                else:
                    assert U % nb == 0
                    has_prev = (u % nb) > 0
                for hh in range(2):
                    s2 = scores[u][hh * B:(hh + 1) * B]
                    if has_prev is True:
                        bp = bias_prev[hh]
                    elif has_prev is False:
                        bp = neg_tile
                    else:
                        bp = jnp.where(has_prev, bias_prev[hh], neg)
                    s_prev = s2[:, 0:B] + bp
                    s_cur = s2[:, B:2 * B] + bias_cur[hh]
                    m = jnp.max(jnp.maximum(s_prev, s_cur), axis=-1, keepdims=True)
                    probs.append(jnp.concatenate([jnp.exp2(s_prev - m), jnp.exp2(s_cur - m)], axis=1).astype(BF16))
                    maxes.append(m)
            for u in range(U):
                rows, kv, c, n = geo[u]
                acc0 = _dot(probs[2 * u], va[kv, :])
                acc1 = _dot(probs[2 * u + 1], vb[kv, :])
                num = jnp.where(first, acc0, acc1)
                den = pltpu.roll(jnp.where(first, acc1, acc0), DIL_DH, axis=1)
                out = num * (1.0 / den)
                lse = jnp.where(first, maxes[2 * u], maxes[2 * u + 1]) + jnp.log2(den)
                if cfg == DIL_SPLIT_CFG:
                    parity, dst = split_rows(r, c, n)
                    ocs[parity, dst, :] = out
                    lcs[parity, dst, :] = lse
                else:
                    oc[cfg, rows, :] = out
                    lc[cfg, rows, :] = lse
            return carry

        lax.fori_loop(0, seq // B // U, body, 0)

    g = g_ref[...]
    CH = 2048

    def mix(i, carry):
        rows = pl.ds(pl.multiple_of(i * CH, CH), CH)
        half_rows = pl.ds(pl.multiple_of(i * (CH // 2), CH // 2), CH // 2)

        def tiles_interleaved(split):
            even = split[0, half_rows, :].reshape(CH // (2 * SUBLANES), 1, SUBLANES, LANES)
            odd = split[1, half_rows, :].reshape(CH // (2 * SUBLANES), 1, SUBLANES, LANES)
            return jnp.concatenate([even, odd], axis=1).reshape(CH, LANES)

        l0, l1, l2 = lc[0, rows, :], lc[1, rows, :], tiles_interleaved(lcs)
        m = jnp.maximum(jnp.maximum(l0, l1), l2)
        e0, e1, e2 = jnp.exp2(l0 - m), jnp.exp2(l1 - m), jnp.exp2(l2 - m)
        inv = 1.0 / (e0 + e1 + e2)
        o = (e0 * oc[0, rows, :] + e1 * oc[1, rows, :] + e2 * tiles_interleaved(ocs)) * inv
        sq = o * o
        ms_a = jnp.sum(jnp.where(first, sq, 0.0), axis=-1, keepdims=True) * (1.0 / DIL_DH)
        ms_b = jnp.sum(jnp.where(first, 0.0, sq), axis=-1, keepdims=True) * (1.0 / DIL_DH)
        ms = jnp.where(first, ms_a, ms_b)
        o_ref[rows, :] = (o * lax.rsqrt(ms + EPS) * g).astype(BF16)
        return carry

    lax.fori_loop(0, seq // CH, mix, 0)


def _dilated(slopes, dq, dk, dv, g2, batch, seq):
    blk = pl.BlockSpec((seq, LANES), lambda b, p, s: (b, p))
    return pl.pallas_call(
        functools.partial(_dil_kernel, seq=seq),
        grid_spec=pltpu.PrefetchScalarGridSpec(
            num_scalar_prefetch=1,
            grid=(batch, DIL_WIDTH // LANES),
            in_specs=[blk, blk, blk, pl.BlockSpec((1, LANES), lambda b, p, s: (0, 0))],
            out_specs=blk,
            scratch_shapes=[pltpu.VMEM((seq, LANES), F32)] * 3 + [pltpu.VMEM((2, seq // 2, LANES), F32)]
                           + [pltpu.VMEM((seq + DIL_PAD, LANES), BF16)] * 3
                           + [pltpu.VMEM((2, seq, LANES), F32)] * 2 + [pltpu.VMEM((2, seq // 2, LANES), F32)] * 2,
        ),
        out_shape=jax.ShapeDtypeStruct((batch * seq, DIL_WIDTH), BF16),
        compiler_params=pltpu.CompilerParams(
            dimension_semantics=("parallel", "parallel"), vmem_limit_bytes=VMEM_LIMIT),
        name="dilated",
    )(slopes, dq, dk, dv, g2)


def _out_proj_kernel(og_ref, od_ref, x_ref, wg_ref, wd_ref, g_ref, b_ref,
                     rw_ref, rb_ref, h_ref, hp_ref, eid_ref, gate_ref, cnt_ref, wg_s, wd_s):
    @pl.when(pl.program_id(0) == 0)
    def _():
        wg_s[...] = wg_ref[...].astype(BF16)
        wd_s[...] = wd_ref[...].astype(BF16)

    mix = _dot(og_ref[...], wg_s[...]) + _dot(od_ref[...], wd_s[...])
    h = _layer_norm(DEEPNORM_ALPHA * x_ref[...] + mix, g_ref[...], b_ref[...])
    h_ref[...] = h
    hp_ref[...] = _pack_bf16_halves(h)
    lt = _dot_nt(rw_ref[...], h) + rb_ref[...]
    tm = lt.shape[1]
    row = lax.broadcasted_iota(jnp.int32, (EXPERTS_PER_GROUP, tm), 0).astype(F32)
    neg = jnp.float32(-jnp.inf)
    big = jnp.float32(1e9)
    coarse = jnp.where(row < N_GROUPS, lt[N_EXPERTS:N_EXPERTS + EXPERTS_PER_GROUP, :], neg)
    cmax = jnp.max(coarse, axis=0, keepdims=True)
    g_idx = jnp.min(jnp.where(coarse == cmax, row, big), axis=0, keepdims=True)
    p_group = 1.0 / jnp.sum(jnp.exp(coarse - cmax), axis=0, keepdims=True)
    fine = lt[(N_GROUPS - 1) * EXPERTS_PER_GROUP:N_EXPERTS, :]
    for g in range(N_GROUPS - 2, -1, -1):
        fine = jnp.where(g_idx == g, lt[g * EXPERTS_PER_GROUP:(g + 1) * EXPERTS_PER_GROUP, :], fine)
    v1 = jnp.max(fine, axis=0, keepdims=True)
    i1 = jnp.min(jnp.where(fine == v1, row, big), axis=0, keepdims=True)
    fine2 = jnp.where(row == i1, neg, fine)
    v2 = jnp.max(fine2, axis=0, keepdims=True)
    i2 = jnp.min(jnp.where(fine2 == v2, row, big), axis=0, keepdims=True)
    e2 = jnp.exp(v2 - v1)
    den = 1.0 + e2
    gate1 = p_group * (1.0 / den)
    gate2 = p_group * (e2 / den)
    id1 = g_idx * EXPERTS_PER_GROUP + i1
    id2 = g_idx * EXPERTS_PER_GROUP + i2
    eid_ref[...] = jnp.concatenate([id1, id2], axis=0).astype(jnp.int32)
    slab = jnp.concatenate([gate1, gate2, jnp.zeros((LANES - 2, tm), F32)], axis=0)
    gate_ref[...] = slab.T[:, 0:2]
    sub = lax.broadcasted_iota(jnp.int32, (LANES, tm), 0).astype(F32)
    onehot = jnp.logical_or(sub == id1, sub == id2).astype(BF16)

    @pl.when(pl.program_id(0) == 0)
    def _():
        cnt_ref[...] = jnp.zeros_like(cnt_ref)

    cnt_ref[...] += _dot(onehot, jnp.ones((tm, LANES), BF16))


def _out_proj(og, od, x2, w_out, layer, g, b, rw, rb, tm=1024):
    T = x2.shape[0]
    row = lambda w: pl.BlockSpec((tm, w), lambda i: (i, 0))
    full = lambda a: pl.BlockSpec(a.shape, lambda i: (0,) * a.ndim)
    half = lambda p: pl.BlockSpec((None, GLA_WIDTH, D_MODEL), lambda i: (layer, p, 0))
    return pl.pallas_call(
        _out_proj_kernel,
        grid=(T // tm,),
        in_specs=[row(GLA_WIDTH), row(DIL_WIDTH), row(D_MODEL), half(0), half(1), full(g), full(b),
                  full(rw), full(rb)],
        out_specs=[row(D_MODEL), row(D_MODEL // 2), pl.BlockSpec((2, tm), lambda i: (0, i)), row(2),
                   pl.BlockSpec((LANES, LANES), lambda i: (0, 0))],
        out_shape=[jax.ShapeDtypeStruct((T, D_MODEL), F32),
                   jax.ShapeDtypeStruct((T, D_MODEL // 2), jnp.int32),
                   jax.ShapeDtypeStruct((2, T), jnp.int32),
                   jax.ShapeDtypeStruct((T, 2), F32),
                   jax.ShapeDtypeStruct((LANES, LANES), F32)],
        scratch_shapes=[pltpu.VMEM((GLA_WIDTH, D_MODEL), BF16), pltpu.VMEM((DIL_WIDTH, D_MODEL), BF16)],
        compiler_params=pltpu.CompilerParams(
            dimension_semantics=("arbitrary",), vmem_limit_bytes=VMEM_LIMIT),
        name="out_proj_router",
    )(og, od, x2, w_out, w_out, g, b, rw, rb)


def _positions_kernel(eid_ref, cnt_ref, dest_ref, be_ref, nv_ref, carry_ref, sp_ref, tri_ref, *, tb):
    i = pl.program_id(0)

    @pl.when(i == 0)
    def _():
        shift = int(math.log2(ROW_BLOCK))
        nb_col = (cnt_ref[...].astype(jnp.int32) + (ROW_BLOCK - 1)) >> shift
        r = lax.broadcasted_iota(jnp.int32, (LANES, LANES), 0)
        c = lax.broadcasted_iota(jnp.int32, (LANES, LANES), 1)
        nb_f = jnp.where(r < N_EXPERTS, nb_col, 0).astype(F32)
        start_col = _dot((c < r).astype(BF16), nb_f.astype(BF16))
        sp_ref[...] = start_col * float(ROW_BLOCK)
        carry_ref[...] = jnp.zeros_like(carry_ref)
        be_ref[...] = jnp.concatenate([start_col.T[0:1, :], nb_f.T[0:1, :]], axis=1).astype(jnp.int32)
        total = jnp.sum(nb_f[:, 0:1], axis=0, keepdims=True)
        nv_ref[...] = jnp.broadcast_to(total, (1, LANES)).astype(jnp.int32)
        tr = lax.broadcasted_iota(jnp.int32, (tb, tb), 0)
        tc = lax.broadcasted_iota(jnp.int32, (tb, tb), 1)
        tri_ref[...] = (tr < tc).astype(BF16)

    sub = lax.broadcasted_iota(jnp.int32, (LANES, tb), 0)
    oh1 = sub == eid_ref[0:1, :]
    oh2 = sub == eid_ref[1:2, :]
    oh = jnp.logical_or(oh1, oh2).astype(BF16)
    offset = jnp.tile(carry_ref[...] + sp_ref[...], (1, tb // LANES))
    before = _dot(oh, tri_ref[...]) + offset
    d1 = jnp.sum(jnp.where(oh1, before, 0.0), axis=0, keepdims=True)
    d2 = jnp.sum(jnp.where(oh2, before, 0.0), axis=0, keepdims=True)
    dest_ref[...] = jnp.concatenate([d1, d2], axis=0).astype(jnp.int32)
    carry_ref[...] += _dot(oh, jnp.ones((tb, LANES), BF16))


def _positions(eid_t, cnt, n_blocks_pad, tb=1024):
    T = eid_t.shape[1]
    return pl.pallas_call(
        functools.partial(_positions_kernel, tb=tb),
        grid=(T // tb,),
        in_specs=[pl.BlockSpec((2, tb), lambda i: (0, i)), pl.BlockSpec((LANES, LANES), lambda i: (0, 0))],
        out_specs=[pl.BlockSpec((2, tb), lambda i: (0, i)),
                   pl.BlockSpec((1, n_blocks_pad), lambda i: (0, 0)),
                   pl.BlockSpec((1, LANES), lambda i: (0, 0))],
        out_shape=[jax.ShapeDtypeStruct((2, T), jnp.int32),
                   jax.ShapeDtypeStruct((1, n_blocks_pad), jnp.int32),
                   jax.ShapeDtypeStruct((1, LANES), jnp.int32)],
        scratch_shapes=[pltpu.VMEM((LANES, LANES), F32), pltpu.VMEM((LANES, LANES), F32),
                        pltpu.VMEM((tb, tb), BF16)],
        compiler_params=pltpu.CompilerParams(dimension_semantics=("arbitrary",)),
        name="positions",
    )(eid_t, cnt)


def _sc_gather_rows(table, idx):
    n = idx.shape[0]
    d = table.shape[1]
    info = plsc.get_sparse_core_info()
    nc, ns = info.num_cores, info.num_subcores
    per_w = n // (nc * ns)
    assert per_w * nc * ns == n and per_w % SC_INDEX_WINDOW == 0
    mesh = plsc.VectorSubcoreMesh(core_axis_name="core", subcore_axis_name="subcore")
    nchunk = per_w // SC_GATHER_ROWS
    nbuf = SC_GATHER_BUFFERS

    @functools.partial(
        pl.kernel, out_type=jax.ShapeDtypeStruct((n, d), table.dtype), mesh=mesh,
        scratch_types=[pltpu.VMEM((per_w,), jnp.int32),
                       pltpu.VMEM((nbuf, SC_GATHER_ROWS, d), table.dtype),
                       pltpu.SemaphoreType.DMA((nbuf,)), pltpu.SemaphoreType.DMA((nbuf,))],
        name="sc_gather_rows")
    def gather(x_hbm, i_hbm, o_hbm, idx_v, buf, gsem, wsem):
        wid = lax.axis_index("subcore") * nc + lax.axis_index("core")
        base = wid * per_w
        pltpu.sync_copy(i_hbm.at[pl.ds(base, per_w)], idx_v)

        def gather_copy(c):
            rows = idx_v.at[pl.ds(c * SC_GATHER_ROWS, SC_GATHER_ROWS)]
            return pltpu.make_async_copy(x_hbm.at[rows], buf.at[c % nbuf], gsem.at[c % nbuf])

        def write_copy(c):
            dst = o_hbm.at[pl.ds(base + c * SC_GATHER_ROWS, SC_GATHER_ROWS)]
            return pltpu.make_async_copy(buf.at[c % nbuf], dst, wsem.at[c % nbuf])

        for c in range(min(nbuf - 1, nchunk)):
            gather_copy(c).start()
        for c in range(nchunk):
            gather_copy(c).wait()
            write_copy(c).start()
            if c + nbuf - 1 < nchunk:
                if c >= 1:
                    write_copy(c - 1).wait()
                gather_copy(c + nbuf - 1).start()
        for c in range(max(0, nchunk - nbuf), nchunk):
            write_copy(c).wait()

    return gather(table, idx)


def _sc_scatter_rows(table, dest_flat, n_rows):
    n_tok, d = table.shape
    assert dest_flat.shape[0] == 2 * n_tok
    info = plsc.get_sparse_core_info()
    nc, ns = info.num_cores, info.num_subcores
    per_w = n_tok // (nc * ns)
    assert per_w * nc * ns == n_tok and per_w % SC_INDEX_WINDOW == 0
    mesh = plsc.VectorSubcoreMesh(core_axis_name="core", subcore_axis_name="subcore")
    nchunk = per_w // SC_GATHER_ROWS
    nbuf = SC_GATHER_BUFFERS

    @functools.partial(
        pl.kernel, out_type=jax.ShapeDtypeStruct((n_rows, d), table.dtype), mesh=mesh,
        scratch_types=[pltpu.VMEM((per_w,), jnp.int32), pltpu.VMEM((per_w,), jnp.int32),
                       pltpu.VMEM((nbuf, SC_GATHER_ROWS, d), table.dtype),
                       pltpu.SemaphoreType.DMA((nbuf,)), pltpu.SemaphoreType.DMA((nbuf, 2))],
        name="sc_scatter_rows")
    def scatter(x_hbm, i_hbm, o_hbm, idx_a, idx_b, buf, rsem, wsem):
        wid = lax.axis_index("subcore") * nc + lax.axis_index("core")
        base = wid * per_w
        pltpu.sync_copy(i_hbm.at[pl.ds(base, per_w)], idx_a)
        pltpu.sync_copy(i_hbm.at[pl.ds(n_tok + base, per_w)], idx_b)

        def read_copy(c):
            src = x_hbm.at[pl.ds(base + c * SC_GATHER_ROWS, SC_GATHER_ROWS)]
            return pltpu.make_async_copy(src, buf.at[c % nbuf], rsem.at[c % nbuf])

        def scatter_copies(c):
            window = pl.ds(c * SC_GATHER_ROWS, SC_GATHER_ROWS)
            return [pltpu.make_async_copy(buf.at[c % nbuf], o_hbm.at[idx.at[window]], wsem.at[c % nbuf, k])
                    for k, idx in enumerate((idx_a, idx_b))]

        for c in range(min(nbuf - 1, nchunk)):
            read_copy(c).start()
        for c in range(nchunk):
            read_copy(c).wait()
            for cp in scatter_copies(c):
                cp.start()
            if c + nbuf - 1 < nchunk:
                if c >= 1:
                    for cp in scatter_copies(c - 1):
                        cp.wait()
                read_copy(c + nbuf - 1).start()
        for c in range(max(0, nchunk - nbuf), nchunk):
            for cp in scatter_copies(c):
                cp.wait()

    return scatter(table, dest_flat)


def _ffn_kernel(first_ref, count_ref, nv_ref, wg_hbm, wu_hbm, wd_hbm, xs_hbm, y_hbm,
                wg32, wu32, wd32, xbuf, ybuf, wsem, isem, osem, *, n_blocks):
    nv = nv_ref[0]
    nbuf = FFN_BUFFERS

    def next_expert(e):
        def more(t):
            return jnp.logical_and(t < N_EXPERTS, count_ref[jnp.minimum(t, N_EXPERTS - 1)] == 0)
        return lax.while_loop(more, lambda t: t + 1, e + 1)

    def weight_copies(e, slot):
        ee = jnp.minimum(e, N_EXPERTS - 1)
        return [pltpu.make_async_copy(src.at[ee], dst.at[slot], wsem.at[slot, i])
                for i, (src, dst) in enumerate(((wg_hbm, wg32), (wu_hbm, wu32), (wd_hbm, wd32)))]

    def fetch_weights(e, slot):
        @pl.when(e < N_EXPERTS)
        def _():
            for c in weight_copies(e, slot):
                c.start(priority=1)

    def take_weights(e, slot):
        for c in weight_copies(e, slot):
            c.wait()
        fetch_weights(next_expert(e), 1 - slot)

    def rows_of(b):
        return pl.ds(pl.multiple_of(b * ROW_BLOCK, ROW_BLOCK), ROW_BLOCK)

    def buf_rows(b, nblk):
        return pl.ds(pl.multiple_of((b % nbuf) * ROW_BLOCK, ROW_BLOCK), nblk * ROW_BLOCK)

    def in_copy(b):
        return pltpu.make_async_copy(xs_hbm.at[rows_of(b)], xbuf.at[buf_rows(b, 1)], isem.at[b % nbuf])

    def out_copy(b):
        return pltpu.make_async_copy(ybuf.at[buf_rows(b, 1)], y_hbm.at[rows_of(b)], osem.at[b % nbuf])

    def expert_mlp(words, slot):
        x = jnp.concatenate(_unpack_bf16_halves(words), axis=1)
        a = _dot(x, wg32[slot])
        u = _dot(x, wu32[slot])
        return _pack_bf16_halves(_dot(a * jax.nn.sigmoid(a) * u, wd32[slot]))

    ahead = nbuf - max(FFN_GROUPS)

    @pl.when(nv > 0)
    def _():
        e0 = next_expert(jnp.int32(-1))
        for i in range(ahead):
            @pl.when(i < nv)
            def _():
                in_copy(i).start()
        fetch_weights(e0, 0)
        take_weights(e0, 0)

        def step(carry):
            b, e, k = carry
            switch = b >= first_ref[e] + count_ref[e]
            e_new = jnp.where(switch, next_expert(e), e)
            k_new = jnp.where(switch, k + 1, k)

            @pl.when(switch)
            def _():
                take_weights(e_new, k_new % 2)

            end = first_ref[e_new] + count_ref[e_new]
            n = jnp.int32(1)
            for size in FFN_GROUPS:
                fits = jnp.logical_and(b + size <= end, b % nbuf + size <= nbuf)
                n = jnp.where(jnp.logical_and(n == 1, fits), size, n)
            for i in range(max(FFN_GROUPS)):
                @pl.when(jnp.logical_and(i < n, b + ahead + i < nv))
                def _():
                    in_copy(b + ahead + i).start()
            for i in range(max(FFN_GROUPS)):
                @pl.when(i < n)
                def _():
                    in_copy(b + i).wait()

                    @pl.when(b + i >= nbuf)
                    def _():
                        out_copy(b + i - nbuf).wait()

            for size in FFN_GROUPS + (1,):
                @pl.when(n == size)
                def _():
                    ybuf[buf_rows(b, size), :] = expert_mlp(xbuf[buf_rows(b, size), :], k_new % 2)

            for i in range(max(FFN_GROUPS)):
                @pl.when(i < n)
                def _():
                    out_copy(b + i).start()
            return b + n, e_new, k_new

        lax.while_loop(lambda c: c[0] < nv, step, (jnp.int32(0), e0, jnp.int32(0)))

        for i in range(nbuf):
            @pl.when(nv > i)
            def _():
                out_copy(nv - 1 - i).wait()

    ybuf[0:ROW_BLOCK, :] = jnp.zeros((ROW_BLOCK, D_MODEL // 2), jnp.int32)

    def fill(b, carry):
        pltpu.sync_copy(ybuf.at[pl.ds(0, ROW_BLOCK)], y_hbm.at[rows_of(b)])
        return carry

    lax.fori_loop(nv, n_blocks, fill, 0)


def _ffn(first_blk, n_blk, nv, xs, w_gate, w_up, w_down):
    n_rows = xs.shape[0]
    n_blocks = n_rows // ROW_BLOCK
    anyspec = pl.BlockSpec(memory_space=pl.ANY)
    return pl.pallas_call(
        functools.partial(_ffn_kernel, n_blocks=n_blocks),
        grid_spec=pltpu.PrefetchScalarGridSpec(
            num_scalar_prefetch=3,
            grid=(1,),
            in_specs=[anyspec, anyspec, anyspec, anyspec],
            out_specs=anyspec,
            scratch_shapes=[pltpu.VMEM((2, D_MODEL, D_FF), F32), pltpu.VMEM((2, D_MODEL, D_FF), F32),
                            pltpu.VMEM((2, D_FF, D_MODEL), F32),
                            pltpu.VMEM((FFN_BUFFERS * ROW_BLOCK, D_MODEL // 2), jnp.int32),
                            pltpu.VMEM((FFN_BUFFERS * ROW_BLOCK, D_MODEL // 2), jnp.int32),
                            pltpu.SemaphoreType.DMA((2, 3)),
                            pltpu.SemaphoreType.DMA((FFN_BUFFERS,)),
                            pltpu.SemaphoreType.DMA((FFN_BUFFERS,))],
        ),
        out_shape=jax.ShapeDtypeStruct((n_rows, D_MODEL // 2), jnp.int32),
        compiler_params=pltpu.CompilerParams(
            dimension_semantics=("arbitrary",), vmem_limit_bytes=VMEM_LIMIT),
        name="expert_ffn",
    )(first_blk, n_blk, nv, w_gate, w_up, w_down, xs)


def _combine_kernel(h_ref, ya_ref, yb_ref, gate_ref, g_ref, b_ref, o_ref):
    gate = gate_ref[...]
    a_hi, a_lo = _unpack_bf16_halves(ya_ref[...])
    b_hi, b_lo = _unpack_bf16_halves(yb_ref[...])
    g0, g1 = gate[:, 0:1], gate[:, 1:2]
    ffn = jnp.concatenate([a_hi * g0 + b_hi * g1, a_lo * g0 + b_lo * g1], axis=1)
    o_ref[...] = _layer_norm(DEEPNORM_ALPHA * h_ref[...] + ffn, g_ref[...], b_ref[...])


def _combine(h, y2, gate, g, b, tm=1024):
    T = h.shape[0]
    nt = T // tm
    return pl.pallas_call(
        _combine_kernel,
        grid=(nt,),
        in_specs=[pl.BlockSpec((tm, D_MODEL), lambda i: (i, 0)),
                  pl.BlockSpec((tm, D_MODEL // 2), lambda i: (i, 0)),
                  pl.BlockSpec((tm, D_MODEL // 2), lambda i: (i + nt, 0)),
                  pl.BlockSpec((tm, 2), lambda i: (i, 0)),
                  pl.BlockSpec((1, D_MODEL), lambda i: (0, 0)),
                  pl.BlockSpec((1, D_MODEL), lambda i: (0, 0))],
        out_specs=pl.BlockSpec((tm, D_MODEL), lambda i: (i, 0)),
        out_shape=jax.ShapeDtypeStruct((T, D_MODEL), F32),
        compiler_params=pltpu.CompilerParams(
            dimension_semantics=("parallel",), vmem_limit_bytes=VMEM_LIMIT),
        name="combine",
    )(h, y2, y2, gate, g, b)


def kernel(x, w_in, gla_gate_w2, gla_gate_b, gla_norm_g, dil_norm_g, w_out, ln1_g, ln1_b,
           router_coarse_w, router_coarse_b, router_fine_w, router_fine_b,
           expert_w_gate, expert_w_up, expert_w_down, ln2_g, ln2_b):
    B, S, D = x.shape
    T = B * S
    depth = w_in.shape[0]
    slopes = jnp.exp2(-8.0 * jnp.arange(1, DIL_HEADS + 1, dtype=F32) / DIL_HEADS)
    n_rows = 2 * T + N_EXPERTS * ROW_BLOCK
    n_blocks_pad = 2 * LANES
    h = x.reshape(T, D)
    w_in_t = jnp.swapaxes(w_in, 1, 2)
    for l in range(depth):
        w2 = jnp.pad(gla_gate_w2[l], ((0, LANES - GLA_GATE_RANK), (0, 0)))
        q, k, v, r, la, dq, dk, dv = _in_proj(h, w_in_t, l, w2, gla_gate_b[l][None, :])
        o_gla = _gla(q, k, v, r, la, gla_norm_g[l][None, :], B, S)
        g2 = jnp.tile(dil_norm_g[l], 2)[None, :]
        o_dil = _dilated(slopes, dq, dk, dv, g2, B, S)
        rw = jnp.concatenate([router_fine_w[l].reshape(D, N_EXPERTS), router_coarse_w[l]], axis=1)
        rw = jnp.pad(rw, ((0, 0), (0, LANES - N_EXPERTS - N_GROUPS))).T
        rb = jnp.concatenate([router_fine_b[l].reshape(N_EXPERTS), router_coarse_b[l]])
        rb = jnp.pad(rb, (0, LANES - N_EXPERTS - N_GROUPS))[:, None]
        h1, h1p, eid_t, gate, cnt = _out_proj(o_gla, o_dil, h, w_out, l,
                                              ln1_g[l][None, :], ln1_b[l][None, :], rw, rb)
        dest_t, be, nv = _positions(eid_t, cnt, n_blocks_pad)
        dest_flat = dest_t.reshape(2 * T)
        xs = _sc_scatter_rows(h1p, dest_flat, n_rows)
        be = be.reshape(n_blocks_pad)
        y = _ffn(be[:N_EXPERTS], be[LANES:LANES + N_EXPERTS], nv.reshape(LANES)[:1], xs,
                 expert_w_gate[l], expert_w_up[l], expert_w_down[l])
        y2 = _sc_gather_rows(y, dest_flat)
        h = _combine(h1, y2, gate, ln2_g[l][None, :], ln2_b[l][None, :])
    return h.reshape(B, S, D)
```

```python
import functools
import math

import jax
import jax.numpy as jnp
from jax import lax
from jax.experimental import pallas as pl
from jax.experimental.pallas import tpu as pltpu
from jax.experimental.pallas import tpu_sc as plsc

D_MODEL = 1024
GLA_HEADS = 4
GLA_DK = 64
GLA_DV = 128
GLA_KEY_WIDTH = GLA_HEADS * GLA_DK
GLA_WIDTH = GLA_HEADS * GLA_DV
GLA_GATE_RANK = 16
GLA_GATE_TEMP = 16.0
DIL_HEADS = 8
DIL_DH = 64
DIL_WIDTH = DIL_HEADS * DIL_DH
DIL_CONFIGS = ((128, 1), (512, 4), (2048, 16))
DIL_BLOCK = 128
DIL_MAX_R = max(r for _, r in DIL_CONFIGS)
DIL_PAD = DIL_BLOCK * DIL_MAX_R
DIL_UNROLL = 16
DIL_SPLIT_CFG = 2
N_GROUPS = 4
EXPERTS_PER_GROUP = 8
N_EXPERTS = N_GROUPS * EXPERTS_PER_GROUP
D_FF = 512
DEEPNORM_ALPHA = 2.0 ** 0.25
EPS = 1e-5
IN_PROJ_GATE_COL = 2 * GLA_KEY_WIDTH + 2 * GLA_WIDTH
LOG2E = math.log2(math.e)

LANES = 128
SUBLANES = 8
assert DIL_SPLIT_CFG == len(DIL_CONFIGS) - 1 and DIL_CONFIGS[DIL_SPLIT_CFG][1] % (2 * SUBLANES) == 0
GLA_CHUNK = 128
GLA_SUB = 64
GLA_UNROLL = 4
SC_INDEX_WINDOW = 128
SC_GATHER_BUFFERS = 6
SC_GATHER_ROWS = 32
FFN_BUFFERS = 12
FFN_GROUPS = (4, 2)
ROW_BLOCK = 256
VMEM_LIMIT = 56 * 1024 * 1024

F32 = jnp.float32
BF16 = jnp.bfloat16


def _dot(a, b):
    return jnp.dot(a, b, preferred_element_type=F32)


def _dot_nt(a, b):
    return lax.dot_general(a, b, (((1,), (1,)), ((), ())), preferred_element_type=F32)


def _dot_tn(a, b):
    return lax.dot_general(a, b, (((0,), (0,)), ((), ())), preferred_element_type=F32)


def _split_bf16(v):
    hi = v.astype(BF16)
    lo = (v - hi.astype(F32)).astype(BF16)
    return hi, lo


def _pack_bf16_halves(v):
    w = v.shape[1] // 2
    hi = lax.bitcast_convert_type(v[:, :w].astype(BF16).astype(F32), jnp.int32)
    lo = lax.bitcast_convert_type(v[:, w:].astype(BF16).astype(F32), jnp.int32)
    return hi | lax.shift_right_logical(lo, 16)


def _unpack_bf16_halves(words):
    hi = lax.bitcast_convert_type(words & jnp.int32(-65536), F32)
    lo = lax.bitcast_convert_type(lax.shift_left(words, 16), F32)
    return hi, lo


def _layer_norm(v, g, b):
    mu = jnp.mean(v, axis=-1, keepdims=True)
    c = v - mu
    var = jnp.mean(c * c, axis=-1, keepdims=True)
    return c * lax.rsqrt(var + EPS) * g + b


def _in_proj_kernel(x_ref, w_ref, w2_ref, gb_ref,
                    q_ref, k_ref, v_ref, r_ref, la_ref, dq_ref, dk_ref, dv_ref, wg_s, wd_s, wa_s):
    a0 = IN_PROJ_GATE_COL

    @pl.when(pl.program_id(0) == 0)
    def _():
        wg_s[...] = w_ref[0:a0, :].T.astype(BF16)
        gate_tile = w_ref[a0:a0 + LANES, :].T
        lane = lax.broadcasted_iota(jnp.int32, gate_tile.shape, 1)
        wa_s[...] = jnp.where(lane < GLA_GATE_RANK, gate_tile, 0.0).astype(BF16)
        wd_s[...] = w_ref[a0 + GLA_GATE_RANK:, :].T.astype(BF16)

    xb = x_ref[...].astype(BF16)

    def piece(w_s, c0, c1):
        return _dot(xb, w_s[:, c0:c1])

    q_ref[...] = (piece(wg_s, 0, 256) * (GLA_DK ** -0.5)).astype(BF16)
    k_ref[...] = piece(wg_s, 256, 512).astype(BF16)
    v_ref[...] = piece(wg_s, 512, 1024).astype(BF16)
    r_ref[...] = piece(wg_s, 1024, 1536).astype(BF16)
    dq_ref[...] = (piece(wd_s, 0, 512) * (DIL_DH ** -0.5 * LOG2E)).astype(BF16)
    dk_ref[...] = piece(wd_s, 512, 1024).astype(BF16)
    dv_ref[...] = piece(wd_s, 1024, 1536).astype(BF16)
    ga = _dot(xb, wa_s[...])
    z = _dot(ga, w2_ref[...]) + gb_ref[...]
    log_sig = jnp.minimum(z, 0.0) - jnp.log1p(jnp.exp(-jnp.abs(z)))
    la_ref[...] = log_sig * (1.0 / GLA_GATE_TEMP)


def _in_proj(x2, w_in_t, layer, w2, gb, tm=1024):
    T = x2.shape[0]
    row = lambda wd: pl.BlockSpec((tm, wd), lambda i: (i, 0))
    full = lambda a: pl.BlockSpec(a.shape, lambda i: (0,) * a.ndim)
    outs = [(GLA_KEY_WIDTH, BF16), (GLA_KEY_WIDTH, BF16), (GLA_WIDTH, BF16), (GLA_WIDTH, BF16), (GLA_KEY_WIDTH, F32),
            (DIL_WIDTH, BF16), (DIL_WIDTH, BF16), (DIL_WIDTH, BF16)]
    group = w_in_t.shape[1] - IN_PROJ_GATE_COL - GLA_GATE_RANK
    return pl.pallas_call(
        _in_proj_kernel,
        grid=(T // tm,),
        in_specs=[row(D_MODEL),
                  pl.BlockSpec((None,) + w_in_t.shape[1:], lambda i: (layer, 0, 0), pipeline_mode=pl.Buffered(1)),
                  full(w2), full(gb)],
        out_specs=[row(wd) for wd, _ in outs],
        out_shape=[jax.ShapeDtypeStruct((T, wd), dt) for wd, dt in outs],
        scratch_shapes=[pltpu.VMEM((D_MODEL, IN_PROJ_GATE_COL), BF16), pltpu.VMEM((D_MODEL, group), BF16),
                        pltpu.VMEM((D_MODEL, LANES), BF16)],
        compiler_params=pltpu.CompilerParams(
            dimension_semantics=("arbitrary",), vmem_limit_bytes=VMEM_LIMIT),
        name="in_proj",
    )(x2, w_in_t, w2, gb)


def _gla_kernel(q_ref, k_ref, v_ref, r_ref, la_ref, g_ref, o_ref, s_ref, *, seq_block):
    C = GLA_CHUNK
    H = GLA_SUB
    assert C == 2 * H

    @pl.when(pl.program_id(1) == 0)
    def _():
        s_ref[...] = jnp.zeros_like(s_ref)

    ri = lax.broadcasted_iota(jnp.int32, (C, C), 0)
    ci = lax.broadcasted_iota(jnp.int32, (C, C), 1)
    same_sub = (ri // H) == (ci // H)
    sum_ops = jnp.concatenate([jnp.logical_and(same_sub, ci <= ri).astype(BF16), same_sub.astype(BF16),
                               jnp.ones((C, C), BF16)], axis=0)
    diag_mask = jnp.logical_and(same_sub, ci <= ri)
    off_mask = (ri // H) > (ci // H)
    second = lax.broadcasted_iota(jnp.int32, (C, 1), 0) >= H
    ones_cl = jnp.ones((C, LANES), BF16)
    lane_k = lax.broadcasted_iota(jnp.int32, (1, GLA_KEY_WIDTH), 1) // GLA_DK
    head_masks = [(lane_k == h).astype(F32) for h in range(GLA_HEADS)]
    srow = lax.broadcasted_iota(jnp.int32, (GLA_KEY_WIDTH, GLA_WIDTH), 0) // GLA_DK
    scol = lax.broadcasted_iota(jnp.int32, (GLA_KEY_WIDTH, GLA_WIDTH), 1) // GLA_DV
    state_mask = (srow == scol).astype(F32)
    g = g_ref[...]

    def trip(t, carry):
        U = GLA_UNROLL
        rows = [pl.ds(pl.multiple_of((t * U + u) * C, C), C) for u in range(U)]
        la2s, sums = [], []
        for u in range(U):
            la_hi, la_lo = _split_bf16(la_ref[rows[u], :])
            la2 = jnp.concatenate([la_hi, la_lo], axis=1)
            la2s.append(la2)
            sm = _dot(sum_ops, la2)
            sums.append(sm[:, 0:GLA_KEY_WIDTH] + sm[:, GLA_KEY_WIDTH:])
        q_states, k_states, scores = [], [], []
        for u in range(U):
            b = sums[u][0:C]
            t_sub = sums[u][C:2 * C]
            other = sums[u][2 * C:3 * C] - t_sub
            q = q_ref[rows[u], :].astype(F32)
            k = k_ref[rows[u], :].astype(F32)
            qd = q * jnp.exp(b)
            kd = (k * jnp.exp(-b)).astype(BF16)
            ke_f = k * jnp.exp(t_sub - b)
            cross = jnp.exp(other)
            q_states.append((qd * jnp.where(second, cross, 1.0)).astype(BF16))
            k_states.append((ke_f * jnp.where(second, 1.0, cross)).astype(BF16))
            q_heads = jnp.concatenate([(qd * head_masks[h]).astype(BF16) for h in range(GLA_HEADS)], axis=0)
            keys2 = jnp.concatenate([kd, ke_f.astype(BF16)], axis=0)
            scores.append(_dot_nt(q_heads, keys2))
        decs, upds = [], []
        for u in range(U):
            tot = _dot_tn(la2s[u], ones_cl)
            decs.append(jnp.exp(tot[0:GLA_KEY_WIDTH] + tot[GLA_KEY_WIDTH:]))
            upds.append(_dot_tn(k_states[u], v_ref[rows[u], :]) * state_mask)
        o_inters = []
        for u in range(U):
            state = s_ref[...]
            o_inters.append(_dot(q_states[u], state.astype(BF16)))
            for h in range(GLA_HEADS):
                cols = slice(h * GLA_DV, (h + 1) * GLA_DV)
                s_ref[:, cols] = state[:, cols] * decs[u] + upds[u][:, cols]
        for u in range(U):
            v = v_ref[rows[u], :]
            outs = []
            for h in range(GLA_HEADS):
                sh = scores[u][h * C:(h + 1) * C]
                a = jnp.where(diag_mask, sh[:, 0:C], 0.0) + jnp.where(off_mask, sh[:, C:2 * C], 0.0)
                cols = slice(h * GLA_DV, (h + 1) * GLA_DV)
                o = _dot(a.astype(BF16), v[:, cols]) + o_inters[u][:, cols]
                o = o * lax.rsqrt(jnp.mean(o * o, axis=-1, keepdims=True) + EPS) * g
                outs.append(o)
            o_all = jnp.concatenate(outs, axis=-1)
            rr = r_ref[rows[u], :].astype(F32)
            o_ref[rows[u], :] = (o_all * (rr * jax.nn.sigmoid(rr))).astype(BF16)
        return carry

    lax.fori_loop(0, seq_block // C // GLA_UNROLL, trip, 0)


def _gla(q, k, v, r, la, g, batch, seq, seq_block=1024):
    nsb = seq // seq_block
    row = lambda w: pl.BlockSpec((seq_block, w), lambda b, s: (b * nsb + s, 0))
    return pl.pallas_call(
        functools.partial(_gla_kernel, seq_block=seq_block),
        grid=(batch, nsb),
        in_specs=[row(GLA_KEY_WIDTH), row(GLA_KEY_WIDTH), row(GLA_WIDTH), row(GLA_WIDTH), row(GLA_KEY_WIDTH),
                  pl.BlockSpec((1, GLA_DV), lambda b, s: (0, 0))],
        out_specs=row(GLA_WIDTH),
        out_shape=jax.ShapeDtypeStruct((batch * seq, GLA_WIDTH), BF16),
        scratch_shapes=[pltpu.VMEM((GLA_KEY_WIDTH, GLA_WIDTH), F32)],
        compiler_params=pltpu.CompilerParams(
            dimension_semantics=("parallel", "arbitrary"), vmem_limit_bytes=VMEM_LIMIT),
        name="gla",
    )(q, k, v, r, la, g)


def _dil_kernel(slope_ref, q_ref, k_ref, v_ref, g_ref, o_ref,
                qf, kf, vf, qs, kd, va, vb, oc, lc, ocs, lcs, *, seq):
    B = DIL_BLOCK
    U = DIL_UNROLL
    pair = pl.program_id(1)
    x = q_ref[...].astype(F32)
    qf[...] = x
    x3 = x.reshape(seq // (2 * SUBLANES), 2 * SUBLANES, LANES)
    qs[0] = x3[:, :SUBLANES, :].reshape(seq // 2, LANES)
    qs[1] = x3[:, SUBLANES:, :].reshape(seq // 2, LANES)
    kf[...] = k_ref[...].astype(F32)
    vf[...] = v_ref[...].astype(F32)

    lane = lax.broadcasted_iota(jnp.int32, (1, LANES), 1)
    first = lane < DIL_DH
    ii = lax.broadcasted_iota(jnp.int32, (B, B), 0)
    jj = lax.broadcasted_iota(jnp.int32, (B, B), 1)
    upper = jj > ii
    eye = jj == ii
    dist = jnp.bitwise_and(ii - jj, B - 1).astype(F32)
    neg = jnp.float32(-jnp.inf)
    neg_tile = jnp.full((B, B), neg, F32)
    zero_tile = jnp.zeros((B, LANES), BF16)

    def split_rows(r, c, n):
        half = r // 2
        start = c % SUBLANES + c // (2 * SUBLANES) * SUBLANES + n * (B * half)
        return (c // SUBLANES) % 2, pl.ds(start, B, stride=half)

    def class_rows(nat, split, cfg, r, c, n):
        if r == 1:
            return nat[pl.ds(pl.multiple_of(n * B, B), B), :]
        if cfg == DIL_SPLIT_CFG:
            parity, rows = split_rows(r, c, n)
            return split[parity, rows, :]
        return nat[pl.ds(c + n * (B * r), B, stride=r), :]

    for cfg, (window, r) in enumerate(DIL_CONFIGS):
        nb = seq // r // B
        cs = nb + 1
        bias_prev, bias_cur = [], []
        for hh in range(2):
            slope = slope_ref[2 * pair + hh] * (float(r) * LOG2E)
            bias = dist * (-slope)
            bias_prev.append(jnp.where(upper, bias, jnp.where(eye, -slope * float(B), neg)))
            bias_cur.append(jnp.where(upper, neg, bias))

        for c in range(r):
            rows0 = slice(c * cs * B, (c * cs + 1) * B)
            kd[rows0, :] = zero_tile
            va[rows0, :] = zero_tile
            vb[rows0, :] = zero_tile

        def prep(t4, carry, cfg=cfg, r=r, nb=nb, cs=cs):
            for j in range(4):
                t = t4 * 4 + j
                c = t // nb
                n = t % nb
                dst = pl.ds(pl.multiple_of((c * cs + 1 + n) * B, B), B)
                kd[dst, :] = class_rows(kf, None, cfg, r, c, n).astype(BF16)
                v = class_rows(vf, None, cfg, r, c, n)
                va[dst, :] = jnp.where(first, v, 1.0).astype(BF16)
                vb[dst, :] = jnp.where(first, 1.0, v).astype(BF16)
            return carry

        def prep_split(i, carry, r=r, cs=cs):
            reg = pl.ds(pl.multiple_of(i * (B * r), B * r), B * r)
            kx = kf[reg, :].reshape(B, r, LANES).swapaxes(0, 1)
            vx = vf[reg, :].reshape(B, r, LANES).swapaxes(0, 1)
            for c in range(r):
                dst = pl.ds(pl.multiple_of((c * cs + 1 + i) * B, B), B)
                kd[dst, :] = kx[c].astype(BF16)
                va[dst, :] = jnp.where(first, vx[c], 1.0).astype(BF16)
                vb[dst, :] = jnp.where(first, 1.0, vx[c]).astype(BF16)
            return carry

        if cfg == DIL_SPLIT_CFG:
            lax.fori_loop(0, nb, prep_split, 0)
        else:
            lax.fori_loop(0, seq // B // 4, prep, 0)

        def geom(it, r=r, nb=nb, cs=cs):
            c = it // nb
            n = it % nb
            start = c + n * (B * r)
            rows = pl.ds(start, B, stride=r) if r > 1 else pl.ds(pl.multiple_of(start, B), B)
            kv = pl.ds(pl.multiple_of((c * cs + n) * B, B), 2 * B)
            return rows, kv, c, n

        def body(t, carry, cfg=cfg, r=r, nb=nb, bias_prev=bias_prev, bias_cur=bias_cur, geom=geom):
            geo = [geom(t * U + u) for u in range(U)]
            scores = []
            for u in range(U):
                _, kv, c, n = geo[u]
                q = class_rows(qf, qs, cfg, r, c, n)
                q_heads = jnp.concatenate([jnp.where(first, q, 0.0), jnp.where(first, 0.0, q)], axis=0)
                scores.append(_dot_nt(q_heads.astype(BF16), kd[kv, :]))
            probs, maxes = [], []
            for u in range(U):
                if nb % U == 0:
                    has_prev = True if u > 0 else (t * U) % nb > 0
                else:
                    assert U % nb == 0
                    has_prev = (u % nb) > 0
                for hh in range(2):
                    s2 = scores[u][hh * B:(hh + 1) * B]
                    if has_prev is True:
                        bp = bias_prev[hh]
                    elif has_prev is False:
                        bp = neg_tile
                    else:
                        bp = jnp.where(has_prev, bias_prev[hh], neg)
                    s_prev = s2[:, 0:B] + bp
                    s_cur = s2[:, B:2 * B] + bias_cur[hh]
                    m = jnp.max(jnp.maximum(s_prev, s_cur), axis=-1, keepdims=True)
                    probs.append(jnp.concatenate([jnp.exp2(s_prev - m), jnp.exp2(s_cur - m)], axis=1).astype(BF16))
                    maxes.append(m)
            for u in range(U):
                rows, kv, c, n = geo[u]
                acc0 = _dot(probs[2 * u], va[kv, :])
                acc1 = _dot(probs[2 * u + 1], vb[kv, :])
                num = jnp.where(first, acc0, acc1)
                den = pltpu.roll(jnp.where(first, acc1, acc0), DIL_DH, axis=1)
                out = num * (1.0 / den)
                lse = jnp.where(first, maxes[2 * u], maxes[2 * u + 1]) + jnp.log2(den)
                if cfg == DIL_SPLIT_CFG:
                    parity, dst = split_rows(r, c, n)
                    ocs[parity, dst, :] = out
                    lcs[parity, dst, :] = lse
                else:
                    oc[cfg, rows, :] = out
                    lc[cfg, rows, :] = lse
            return carry

        lax.fori_loop(0, seq // B // U, body, 0)

    g = g_ref[...]
    CH = 2048

    def mix(i, carry):
        rows = pl.ds(pl.multiple_of(i * CH, CH), CH)
        half_rows = pl.ds(pl.multiple_of(i * (CH // 2), CH // 2), CH // 2)

        def tiles_interleaved(split):
            even = split[0, half_rows, :].reshape(CH // (2 * SUBLANES), 1, SUBLANES, LANES)
            odd = split[1, half_rows, :].reshape(CH // (2 * SUBLANES), 1, SUBLANES, LANES)
            return jnp.concatenate([even, odd], axis=1).reshape(CH, LANES)

        l0, l1, l2 = lc[0, rows, :], lc[1, rows, :], tiles_interleaved(lcs)
        m = jnp.maximum(jnp.maximum(l0, l1), l2)
        e0, e1, e2 = jnp.exp2(l0 - m), jnp.exp2(l1 - m), jnp.exp2(l2 - m)
        inv = 1.0 / (e0 + e1 + e2)
        o = (e0 * oc[0, rows, :] + e1 * oc[1, rows, :] + e2 * tiles_interleaved(ocs)) * inv
        sq = o * o
        ms_a = jnp.sum(jnp.where(first, sq, 0.0), axis=-1, keepdims=True) * (1.0 / DIL_DH)
        ms_b = jnp.sum(jnp.where(first, 0.0, sq), axis=-1, keepdims=True) * (1.0 / DIL_DH)
        ms = jnp.where(first, ms_a, ms_b)
        o_ref[rows, :] = (o * lax.rsqrt(ms + EPS) * g).astype(BF16)
        return carry

    lax.fori_loop(0, seq // CH, mix, 0)


def _dilated(slopes, dq, dk, dv, g2, batch, seq):
    blk = pl.BlockSpec((seq, LANES), lambda b, p, s: (b, p))
    return pl.pallas_call(
        functools.partial(_dil_kernel, seq=seq),
        grid_spec=pltpu.PrefetchScalarGridSpec(
            num_scalar_prefetch=1,
            grid=(batch, DIL_WIDTH // LANES),
            in_specs=[blk, blk, blk, pl.BlockSpec((1, LANES), lambda b, p, s: (0, 0))],
            out_specs=blk,
            scratch_shapes=[pltpu.VMEM((seq, LANES), F32)] * 3 + [pltpu.VMEM((2, seq // 2, LANES), F32)]
                           + [pltpu.VMEM((seq + DIL_PAD, LANES), BF16)] * 3
                           + [pltpu.VMEM((2, seq, LANES), F32)] * 2 + [pltpu.VMEM((2, seq // 2, LANES), F32)] * 2,
        ),
        out_shape=jax.ShapeDtypeStruct((batch * seq, DIL_WIDTH), BF16),
        compiler_params=pltpu.CompilerParams(
            dimension_semantics=("parallel", "parallel"), vmem_limit_bytes=VMEM_LIMIT),
        name="dilated",
    )(slopes, dq, dk, dv, g2)


def _out_proj_kernel(og_ref, od_ref, x_ref, wg_ref, wd_ref, g_ref, b_ref,
                     rw_ref, rb_ref, h_ref, hp_ref, eid_ref, gate_ref, cnt_ref, wg_s, wd_s):
    @pl.when(pl.program_id(0) == 0)
    def _():
        wg_s[...] = wg_ref[...].astype(BF16)
        wd_s[...] = wd_ref[...].astype(BF16)

    mix = _dot(og_ref[...], wg_s[...]) + _dot(od_ref[...], wd_s[...])
    h = _layer_norm(DEEPNORM_ALPHA * x_ref[...] + mix, g_ref[...], b_ref[...])
    h_ref[...] = h
    hp_ref[...] = _pack_bf16_halves(h)
    lt = _dot_nt(rw_ref[...], h) + rb_ref[...]
    tm = lt.shape[1]
    row = lax.broadcasted_iota(jnp.int32, (EXPERTS_PER_GROUP, tm), 0).astype(F32)
    neg = jnp.float32(-jnp.inf)
    big = jnp.float32(1e9)
    coarse = jnp.where(row < N_GROUPS, lt[N_EXPERTS:N_EXPERTS + EXPERTS_PER_GROUP, :], neg)
    cmax = jnp.max(coarse, axis=0, keepdims=True)
    g_idx = jnp.min(jnp.where(coarse == cmax, row, big), axis=0, keepdims=True)
    p_group = 1.0 / jnp.sum(jnp.exp(coarse - cmax), axis=0, keepdims=True)
    fine = lt[(N_GROUPS - 1) * EXPERTS_PER_GROUP:N_EXPERTS, :]
    for g in range(N_GROUPS - 2, -1, -1):
        fine = jnp.where(g_idx == g, lt[g * EXPERTS_PER_GROUP:(g + 1) * EXPERTS_PER_GROUP, :], fine)
    v1 = jnp.max(fine, axis=0, keepdims=True)
    i1 = jnp.min(jnp.where(fine == v1, row, big), axis=0, keepdims=True)
    fine2 = jnp.where(row == i1, neg, fine)
    v2 = jnp.max(fine2, axis=0, keepdims=True)
    i2 = jnp.min(jnp.where(fine2 == v2, row, big), axis=0, keepdims=True)
    e2 = jnp.exp(v2 - v1)
    den = 1.0 + e2
    gate1 = p_group * (1.0 / den)
    gate2 = p_group * (e2 / den)
    id1 = g_idx * EXPERTS_PER_GROUP + i1
    id2 = g_idx * EXPERTS_PER_GROUP + i2
    eid_ref[...] = jnp.concatenate([id1, id2], axis=0).astype(jnp.int32)
    slab = jnp.concatenate([gate1, gate2, jnp.zeros((LANES - 2, tm), F32)], axis=0)
    gate_ref[...] = slab.T[:, 0:2]
    sub = lax.broadcasted_iota(jnp.int32, (LANES, tm), 0).astype(F32)
    onehot = jnp.logical_or(sub == id1, sub == id2).astype(BF16)

    @pl.when(pl.program_id(0) == 0)
    def _():
        cnt_ref[...] = jnp.zeros_like(cnt_ref)

    cnt_ref[...] += _dot(onehot, jnp.ones((tm, LANES), BF16))


def _out_proj(og, od, x2, w_out, layer, g, b, rw, rb, tm=1024):
    T = x2.shape[0]
    row = lambda w: pl.BlockSpec((tm, w), lambda i: (i, 0))
    full = lambda a: pl.BlockSpec(a.shape, lambda i: (0,) * a.ndim)
    half = lambda p: pl.BlockSpec((None, GLA_WIDTH, D_MODEL), lambda i: (layer, p, 0))
    return pl.pallas_call(
        _out_proj_kernel,
        grid=(T // tm,),
        in_specs=[row(GLA_WIDTH), row(DIL_WIDTH), row(D_MODEL), half(0), half(1), full(g), full(b),
                  full(rw), full(rb)],
        out_specs=[row(D_MODEL), row(D_MODEL // 2), pl.BlockSpec((2, tm), lambda i: (0, i)), row(2),
                   pl.BlockSpec((LANES, LANES), lambda i: (0, 0))],
        out_shape=[jax.ShapeDtypeStruct((T, D_MODEL), F32),
                   jax.ShapeDtypeStruct((T, D_MODEL // 2), jnp.int32),
                   jax.ShapeDtypeStruct((2, T), jnp.int32),
                   jax.ShapeDtypeStruct((T, 2), F32),
                   jax.ShapeDtypeStruct((LANES, LANES), F32)],
        scratch_shapes=[pltpu.VMEM((GLA_WIDTH, D_MODEL), BF16), pltpu.VMEM((DIL_WIDTH, D_MODEL), BF16)],
        compiler_params=pltpu.CompilerParams(
            dimension_semantics=("arbitrary",), vmem_limit_bytes=VMEM_LIMIT),
        name="out_proj_router",
    )(og, od, x2, w_out, w_out, g, b, rw, rb)


def _positions_kernel(eid_ref, cnt_ref, dest_ref, be_ref, nv_ref, carry_ref, sp_ref, tri_ref, *, tb):
    i = pl.program_id(0)

    @pl.when(i == 0)
    def _():
        shift = int(math.log2(ROW_BLOCK))
        nb_col = (cnt_ref[...].astype(jnp.int32) + (ROW_BLOCK - 1)) >> shift
        r = lax.broadcasted_iota(jnp.int32, (LANES, LANES), 0)
        c = lax.broadcasted_iota(jnp.int32, (LANES, LANES), 1)
        nb_f = jnp.where(r < N_EXPERTS, nb_col, 0).astype(F32)
        start_col = _dot((c < r).astype(BF16), nb_f.astype(BF16))
        sp_ref[...] = start_col * float(ROW_BLOCK)
        carry_ref[...] = jnp.zeros_like(carry_ref)
        be_ref[...] = jnp.concatenate([start_col.T[0:1, :], nb_f.T[0:1, :]], axis=1).astype(jnp.int32)
        total = jnp.sum(nb_f[:, 0:1], axis=0, keepdims=True)
        nv_ref[...] = jnp.broadcast_to(total, (1, LANES)).astype(jnp.int32)
        tr = lax.broadcasted_iota(jnp.int32, (tb, tb), 0)
        tc = lax.broadcasted_iota(jnp.int32, (tb, tb), 1)
        tri_ref[...] = (tr < tc).astype(BF16)

    sub = lax.broadcasted_iota(jnp.int32, (LANES, tb), 0)
    oh1 = sub == eid_ref[0:1, :]
    oh2 = sub == eid_ref[1:2, :]
    oh = jnp.logical_or(oh1, oh2).astype(BF16)
    offset = jnp.tile(carry_ref[...] + sp_ref[...], (1, tb // LANES))
    before = _dot(oh, tri_ref[...]) + offset
    d1 = jnp.sum(jnp.where(oh1, before, 0.0), axis=0, keepdims=True)
    d2 = jnp.sum(jnp.where(oh2, before, 0.0), axis=0, keepdims=True)
    dest_ref[...] = jnp.concatenate([d1, d2], axis=0).astype(jnp.int32)
    carry_ref[...] += _dot(oh, jnp.ones((tb, LANES), BF16))


def _positions(eid_t, cnt, n_blocks_pad, tb=1024):
    T = eid_t.shape[1]
    return pl.pallas_call(
        functools.partial(_positions_kernel, tb=tb),
        grid=(T // tb,),
        in_specs=[pl.BlockSpec((2, tb), lambda i: (0, i)), pl.BlockSpec((LANES, LANES), lambda i: (0, 0))],
        out_specs=[pl.BlockSpec((2, tb), lambda i: (0, i)),
                   pl.BlockSpec((1, n_blocks_pad), lambda i: (0, 0)),
                   pl.BlockSpec((1, LANES), lambda i: (0, 0))],
        out_shape=[jax.ShapeDtypeStruct((2, T), jnp.int32),
                   jax.ShapeDtypeStruct((1, n_blocks_pad), jnp.int32),
                   jax.ShapeDtypeStruct((1, LANES), jnp.int32)],
        scratch_shapes=[pltpu.VMEM((LANES, LANES), F32), pltpu.VMEM((LANES, LANES), F32),
                        pltpu.VMEM((tb, tb), BF16)],
        compiler_params=pltpu.CompilerParams(dimension_semantics=("arbitrary",)),
        name="positions",
    )(eid_t, cnt)


def _sc_gather_rows(table, idx):
    n = idx.shape[0]
    d = table.shape[1]
    info = plsc.get_sparse_core_info()
    nc, ns = info.num_cores, info.num_subcores
    per_w = n // (nc * ns)
    assert per_w * nc * ns == n and per_w % SC_INDEX_WINDOW == 0
    mesh = plsc.VectorSubcoreMesh(core_axis_name="core", subcore_axis_name="subcore")
    nchunk = per_w // SC_GATHER_ROWS
    nbuf = SC_GATHER_BUFFERS

    @functools.partial(
        pl.kernel, out_type=jax.ShapeDtypeStruct((n, d), table.dtype), mesh=mesh,
        scratch_types=[pltpu.VMEM((per_w,), jnp.int32),
                       pltpu.VMEM((nbuf, SC_GATHER_ROWS, d), table.dtype),
                       pltpu.SemaphoreType.DMA((nbuf,)), pltpu.SemaphoreType.DMA((nbuf,))],
        name="sc_gather_rows")
    def gather(x_hbm, i_hbm, o_hbm, idx_v, buf, gsem, wsem):
        wid = lax.axis_index("subcore") * nc + lax.axis_index("core")
        base = wid * per_w
        pltpu.sync_copy(i_hbm.at[pl.ds(base, per_w)], idx_v)

        def gather_copy(c):
            rows = idx_v.at[pl.ds(c * SC_GATHER_ROWS, SC_GATHER_ROWS)]
            return pltpu.make_async_copy(x_hbm.at[rows], buf.at[c % nbuf], gsem.at[c % nbuf])

        def write_copy(c):
            dst = o_hbm.at[pl.ds(base + c * SC_GATHER_ROWS, SC_GATHER_ROWS)]
            return pltpu.make_async_copy(buf.at[c % nbuf], dst, wsem.at[c % nbuf])

        for c in range(min(nbuf - 1, nchunk)):
            gather_copy(c).start()
        for c in range(nchunk):
            gather_copy(c).wait()
            write_copy(c).start()
            if c + nbuf - 1 < nchunk:
                if c >= 1:
                    write_copy(c - 1).wait()
                gather_copy(c + nbuf - 1).start()
        for c in range(max(0, nchunk - nbuf), nchunk):
            write_copy(c).wait()

    return gather(table, idx)


def _sc_scatter_rows(table, dest_flat, n_rows):
    n_tok, d = table.shape
    assert dest_flat.shape[0] == 2 * n_tok
    info = plsc.get_sparse_core_info()
    nc, ns = info.num_cores, info.num_subcores
    per_w = n_tok // (nc * ns)
    assert per_w * nc * ns == n_tok and per_w % SC_INDEX_WINDOW == 0
    mesh = plsc.VectorSubcoreMesh(core_axis_name="core", subcore_axis_name="subcore")
    nchunk = per_w // SC_GATHER_ROWS
    nbuf = SC_GATHER_BUFFERS

    @functools.partial(
        pl.kernel, out_type=jax.ShapeDtypeStruct((n_rows, d), table.dtype), mesh=mesh,
        scratch_types=[pltpu.VMEM((per_w,), jnp.int32), pltpu.VMEM((per_w,), jnp.int32),
                       pltpu.VMEM((nbuf, SC_GATHER_ROWS, d), table.dtype),
                       pltpu.SemaphoreType.DMA((nbuf,)), pltpu.SemaphoreType.DMA((nbuf, 2))],
        name="sc_scatter_rows")
    def scatter(x_hbm, i_hbm, o_hbm, idx_a, idx_b, buf, rsem, wsem):
        wid = lax.axis_index("subcore") * nc + lax.axis_index("core")
        base = wid * per_w
        pltpu.sync_copy(i_hbm.at[pl.ds(base, per_w)], idx_a)
        pltpu.sync_copy(i_hbm.at[pl.ds(n_tok + base, per_w)], idx_b)

        def read_copy(c):
            src = x_hbm.at[pl.ds(base + c * SC_GATHER_ROWS, SC_GATHER_ROWS)]
            return pltpu.make_async_copy(src, buf.at[c % nbuf], rsem.at[c % nbuf])

        def scatter_copies(c):
            window = pl.ds(c * SC_GATHER_ROWS, SC_GATHER_ROWS)
            return [pltpu.make_async_copy(buf.at[c % nbuf], o_hbm.at[idx.at[window]], wsem.at[c % nbuf, k])
                    for k, idx in enumerate((idx_a, idx_b))]

        for c in range(min(nbuf - 1, nchunk)):
            read_copy(c).start()
        for c in range(nchunk):
            read_copy(c).wait()
            for cp in scatter_copies(c):
                cp.start()
            if c + nbuf - 1 < nchunk:
                if c >= 1:
                    for cp in scatter_copies(c - 1):
                        cp.wait()
                read_copy(c + nbuf - 1).start()
        for c in range(max(0, nchunk - nbuf), nchunk):
            for cp in scatter_copies(c):
                cp.wait()

    return scatter(table, dest_flat)


def _ffn_kernel(first_ref, count_ref, nv_ref, wg_hbm, wu_hbm, wd_hbm, xs_hbm, y_hbm,
                wg32, wu32, wd32, xbuf, ybuf, wsem, isem, osem, *, n_blocks):
    nv = nv_ref[0]
    nbuf = FFN_BUFFERS

    def next_expert(e):
        def more(t):
            return jnp.logical_and(t < N_EXPERTS, count_ref[jnp.minimum(t, N_EXPERTS - 1)] == 0)
        return lax.while_loop(more, lambda t: t + 1, e + 1)

    def weight_copies(e, slot):
        ee = jnp.minimum(e, N_EXPERTS - 1)
        return [pltpu.make_async_copy(src.at[ee], dst.at[slot], wsem.at[slot, i])
                for i, (src, dst) in enumerate(((wg_hbm, wg32), (wu_hbm, wu32), (wd_hbm, wd32)))]

    def fetch_weights(e, slot):
        @pl.when(e < N_EXPERTS)
        def _():
            for c in weight_copies(e, slot):
                c.start()

    def take_weights(e, slot):
        for c in weight_copies(e, slot):
            c.wait()
        fetch_weights(next_expert(e), 1 - slot)

    def rows_of(b):
        return pl.ds(pl.multiple_of(b * ROW_BLOCK, ROW_BLOCK), ROW_BLOCK)

    def buf_rows(b, nblk):
        return pl.ds(pl.multiple_of((b % nbuf) * ROW_BLOCK, ROW_BLOCK), nblk * ROW_BLOCK)

    def in_copy(b):
        return pltpu.make_async_copy(xs_hbm.at[rows_of(b)], xbuf.at[buf_rows(b, 1)], isem.at[b % nbuf])

    def out_copy(b):
        return pltpu.make_async_copy(ybuf.at[buf_rows(b, 1)], y_hbm.at[rows_of(b)], osem.at[b % nbuf])

    def expert_mlp(words, slot):
        x = jnp.concatenate(_unpack_bf16_halves(words), axis=1)
        a = _dot(x, wg32[slot])
        u = _dot(x, wu32[slot])
        return _pack_bf16_halves(_dot(a * jax.nn.sigmoid(a) * u, wd32[slot]))

    ahead = nbuf - max(FFN_GROUPS)

    @pl.when(nv > 0)
    def _():
        e0 = next_expert(jnp.int32(-1))
        for i in range(ahead):
            @pl.when(i < nv)
            def _():
                in_copy(i).start()
        fetch_weights(e0, 0)
        take_weights(e0, 0)

        def step(carry):
            b, e, k = carry
            switch = b >= first_ref[e] + count_ref[e]
            e_new = jnp.where(switch, next_expert(e), e)
            k_new = jnp.where(switch, k + 1, k)

            @pl.when(switch)
            def _():
                take_weights(e_new, k_new % 2)

            end = first_ref[e_new] + count_ref[e_new]
            n = jnp.int32(1)
            for size in FFN_GROUPS:
                fits = jnp.logical_and(b + size <= end, b % nbuf + size <= nbuf)
                n = jnp.where(jnp.logical_and(n == 1, fits), size, n)
            for i in range(max(FFN_GROUPS)):
                @pl.when(jnp.logical_and(i < n, b + ahead + i < nv))
                def _():
                    in_copy(b + ahead + i).start()
            for i in range(max(FFN_GROUPS)):
                @pl.when(i < n)
                def _():
                    in_copy(b + i).wait()

                    @pl.when(b + i >= nbuf)
                    def _():
                        out_copy(b + i - nbuf).wait()

            for size in FFN_GROUPS + (1,):
                @pl.when(n == size)
                def _():
                    ybuf[buf_rows(b, size), :] = expert_mlp(xbuf[buf_rows(b, size), :], k_new % 2)

            for i in range(max(FFN_GROUPS)):
                @pl.when(i < n)
                def _():
                    out_copy(b + i).start(priority=1)
            return b + n, e_new, k_new

        lax.while_loop(lambda c: c[0] < nv, step, (jnp.int32(0), e0, jnp.int32(0)))

        for i in range(nbuf):
            @pl.when(nv > i)
            def _():
                out_copy(nv - 1 - i).wait()

    ybuf[0:ROW_BLOCK, :] = jnp.zeros((ROW_BLOCK, D_MODEL // 2), jnp.int32)

    def fill(b, carry):
        pltpu.sync_copy(ybuf.at[pl.ds(0, ROW_BLOCK)], y_hbm.at[rows_of(b)])
        return carry

    lax.fori_loop(nv, n_blocks, fill, 0)


def _ffn(first_blk, n_blk, nv, xs, w_gate, w_up, w_down):
    n_rows = xs.shape[0]
    n_blocks = n_rows // ROW_BLOCK
    anyspec = pl.BlockSpec(memory_space=pl.ANY)
    return pl.pallas_call(
        functools.partial(_ffn_kernel, n_blocks=n_blocks),
        grid_spec=pltpu.PrefetchScalarGridSpec(
            num_scalar_prefetch=3,
            grid=(1,),
            in_specs=[anyspec, anyspec, anyspec, anyspec],
            out_specs=anyspec,
            scratch_shapes=[pltpu.VMEM((2, D_MODEL, D_FF), F32), pltpu.VMEM((2, D_MODEL, D_FF), F32),
                            pltpu.VMEM((2, D_FF, D_MODEL), F32),
                            pltpu.VMEM((FFN_BUFFERS * ROW_BLOCK, D_MODEL // 2), jnp.int32),
                            pltpu.VMEM((FFN_BUFFERS * ROW_BLOCK, D_MODEL // 2), jnp.int32),
                            pltpu.SemaphoreType.DMA((2, 3)),
                            pltpu.SemaphoreType.DMA((FFN_BUFFERS,)),
                            pltpu.SemaphoreType.DMA((FFN_BUFFERS,))],
        ),
        out_shape=jax.ShapeDtypeStruct((n_rows, D_MODEL // 2), jnp.int32),
        compiler_params=pltpu.CompilerParams(
            dimension_semantics=("arbitrary",), vmem_limit_bytes=VMEM_LIMIT),
        name="expert_ffn",
    )(first_blk, n_blk, nv, w_gate, w_up, w_down, xs)


def _combine_kernel(h_ref, ya_ref, yb_ref, gate_ref, g_ref, b_ref, o_ref):
    gate = gate_ref[...]
    a_hi, a_lo = _unpack_bf16_halves(ya_ref[...])
    b_hi, b_lo = _unpack_bf16_halves(yb_ref[...])
    g0, g1 = gate[:, 0:1], gate[:, 1:2]
    ffn = jnp.concatenate([a_hi * g0 + b_hi * g1, a_lo * g0 + b_lo * g1], axis=1)
    o_ref[...] = _layer_norm(DEEPNORM_ALPHA * h_ref[...] + ffn, g_ref[...], b_ref[...])


def _combine(h, y2, gate, g, b, tm=1024):
    T = h.shape[0]
    nt = T // tm
    return pl.pallas_call(
        _combine_kernel,
        grid=(nt,),
        in_specs=[pl.BlockSpec((tm, D_MODEL), lambda i: (i, 0)),
                  pl.BlockSpec((tm, D_MODEL // 2), lambda i: (i, 0)),
                  pl.BlockSpec((tm, D_MODEL // 2), lambda i: (i + nt, 0)),
                  pl.BlockSpec((tm, 2), lambda i: (i, 0)),
                  pl.BlockSpec((1, D_MODEL), lambda i: (0, 0)),
                  pl.BlockSpec((1, D_MODEL), lambda i: (0, 0))],
        out_specs=pl.BlockSpec((tm, D_MODEL), lambda i: (i, 0)),
        out_shape=jax.ShapeDtypeStruct((T, D_MODEL), F32),
        compiler_params=pltpu.CompilerParams(
            dimension_semantics=("parallel",), vmem_limit_bytes=VMEM_LIMIT),
        name="combine",
    )(h, y2, y2, gate, g, b)


def kernel(x, w_in, gla_gate_w2, gla_gate_b, gla_norm_g, dil_norm_g, w_out, ln1_g, ln1_b,
           router_coarse_w, router_coarse_b, router_fine_w, router_fine_b,
           expert_w_gate, expert_w_up, expert_w_down, ln2_g, ln2_b):
    B, S, D = x.shape
    T = B * S
    depth = w_in.shape[0]
    slopes = jnp.exp2(-8.0 * jnp.arange(1, DIL_HEADS + 1, dtype=F32) / DIL_HEADS)
    n_rows = 2 * T + N_EXPERTS * ROW_BLOCK
    n_blocks_pad = 2 * LANES
    h = x.reshape(T, D)
    w_in_t = jnp.swapaxes(w_in, 1, 2)
    for l in range(depth):
        w2 = jnp.pad(gla_gate_w2[l], ((0, LANES - GLA_GATE_RANK), (0, 0)))
        q, k, v, r, la, dq, dk, dv = _in_proj(h, w_in_t, l, w2, gla_gate_b[l][None, :])
        o_gla = _gla(q, k, v, r, la, gla_norm_g[l][None, :], B, S)
        g2 = jnp.tile(dil_norm_g[l], 2)[None, :]
        o_dil = _dilated(slopes, dq, dk, dv, g2, B, S)
        rw = jnp.concatenate([router_fine_w[l].reshape(D, N_EXPERTS), router_coarse_w[l]], axis=1)
        rw = jnp.pad(rw, ((0, 0), (0, LANES - N_EXPERTS - N_GROUPS))).T
        rb = jnp.concatenate([router_fine_b[l].reshape(N_EXPERTS), router_coarse_b[l]])
        rb = jnp.pad(rb, (0, LANES - N_EXPERTS - N_GROUPS))[:, None]
        h1, h1p, eid_t, gate, cnt = _out_proj(o_gla, o_dil, h, w_out, l,
                                              ln1_g[l][None, :], ln1_b[l][None, :], rw, rb)
        dest_t, be, nv = _positions(eid_t, cnt, n_blocks_pad)
        dest_flat = dest_t.reshape(2 * T)
        xs = _sc_scatter_rows(h1p, dest_flat, n_rows)
        be = be.reshape(n_blocks_pad)
        y = _ffn(be[:N_EXPERTS], be[LANES:LANES + N_EXPERTS], nv.reshape(LANES)[:1], xs,
                 expert_w_gate[l], expert_w_up[l], expert_w_down[l])
        y2 = _sc_gather_rows(y, dest_flat)
        h = _combine(h1, y2, gate, ln2_g[l][None, :], ln2_b[l][None, :])
    return h.reshape(B, S, D)
```

```python
import functools
import math

import jax
import jax.numpy as jnp
from jax import lax
from jax.experimental import pallas as pl
from jax.experimental.pallas import tpu as pltpu
from jax.experimental.pallas import tpu_sc as plsc

D_MODEL = 1024
GLA_HEADS = 4
GLA_DK = 64
GLA_DV = 128
GLA_KEY_WIDTH = GLA_HEADS * GLA_DK
GLA_WIDTH = GLA_HEADS * GLA_DV
GLA_GATE_RANK = 16
GLA_GATE_TEMP = 16.0
DIL_HEADS = 8
DIL_DH = 64
DIL_WIDTH = DIL_HEADS * DIL_DH
DIL_CONFIGS = ((128, 1), (512, 4), (2048, 16))
DIL_BLOCK = 128
DIL_MAX_R = max(r for _, r in DIL_CONFIGS)
DIL_PAD = DIL_BLOCK * DIL_MAX_R
DIL_UNROLL = 16
DIL_SPLIT_CFG = 2
N_GROUPS = 4
EXPERTS_PER_GROUP = 8
N_EXPERTS = N_GROUPS * EXPERTS_PER_GROUP
D_FF = 512
DEEPNORM_ALPHA = 2.0 ** 0.25
EPS = 1e-5
IN_PROJ_GATE_COL = 2 * GLA_KEY_WIDTH + 2 * GLA_WIDTH
LOG2E = math.log2(math.e)

LANES = 128
SUBLANES = 8
assert DIL_SPLIT_CFG == len(DIL_CONFIGS) - 1 and DIL_CONFIGS[DIL_SPLIT_CFG][1] % (2 * SUBLANES) == 0
GLA_CHUNK = 128
GLA_SUB = 64
GLA_UNROLL = 4
SC_INDEX_WINDOW = 128
SC_GATHER_BUFFERS = 6
SC_GATHER_ROWS = 32
FFN_BUFFERS = 12
FFN_GROUPS = (4, 2)
ROW_BLOCK = 256
VMEM_LIMIT = 56 * 1024 * 1024

F32 = jnp.float32
BF16 = jnp.bfloat16


def _dot(a, b):
    return jnp.dot(a, b, preferred_element_type=F32)


def _dot_nt(a, b):
    return lax.dot_general(a, b, (((1,), (1,)), ((), ())), preferred_element_type=F32)


def _dot_tn(a, b):
    return lax.dot_general(a, b, (((0,), (0,)), ((), ())), preferred_element_type=F32)


def _split_bf16(v):
    hi = v.astype(BF16)
    lo = (v - hi.astype(F32)).astype(BF16)
    return hi, lo


def _pack_bf16_halves(v):
    w = v.shape[1] // 2
    hi = lax.bitcast_convert_type(v[:, :w].astype(BF16).astype(F32), jnp.int32)
    lo = lax.bitcast_convert_type(v[:, w:].astype(BF16).astype(F32), jnp.int32)
    return hi | lax.shift_right_logical(lo, 16)


def _unpack_bf16_halves(words):
    hi = lax.bitcast_convert_type(words & jnp.int32(-65536), F32)
    lo = lax.bitcast_convert_type(lax.shift_left(words, 16), F32)
    return hi, lo


def _layer_norm(v, g, b):
    mu = jnp.mean(v, axis=-1, keepdims=True)
    c = v - mu
    var = jnp.mean(c * c, axis=-1, keepdims=True)
    return c * lax.rsqrt(var + EPS) * g + b


def _in_proj_kernel(x_ref, w_ref, w2_ref, gb_ref,
                    q_ref, k_ref, v_ref, r_ref, la_ref, dq_ref, dk_ref, dv_ref, wg_s, wd_s, wa_s):
    a0 = IN_PROJ_GATE_COL

    @pl.when(pl.program_id(0) == 0)
    def _():
        wg_s[...] = w_ref[0:a0, :].T.astype(BF16)
        gate_tile = w_ref[a0:a0 + LANES, :].T
        lane = lax.broadcasted_iota(jnp.int32, gate_tile.shape, 1)
        wa_s[...] = jnp.where(lane < GLA_GATE_RANK, gate_tile, 0.0).astype(BF16)
        wd_s[...] = w_ref[a0 + GLA_GATE_RANK:, :].T.astype(BF16)

    xb = x_ref[...].astype(BF16)

    def piece(w_s, c0, c1):
        return _dot(xb, w_s[:, c0:c1])

    q_ref[...] = (piece(wg_s, 0, 256) * (GLA_DK ** -0.5)).astype(BF16)
    k_ref[...] = piece(wg_s, 256, 512).astype(BF16)
    v_ref[...] = piece(wg_s, 512, 1024).astype(BF16)
    r_ref[...] = piece(wg_s, 1024, 1536).astype(BF16)
    dq_ref[...] = (piece(wd_s, 0, 512) * (DIL_DH ** -0.5 * LOG2E)).astype(BF16)
    dk_ref[...] = piece(wd_s, 512, 1024).astype(BF16)
    dv_ref[...] = piece(wd_s, 1024, 1536).astype(BF16)
    ga = _dot(xb, wa_s[...])
    z = _dot(ga, w2_ref[...]) + gb_ref[...]
    log_sig = jnp.minimum(z, 0.0) - jnp.log1p(jnp.exp(-jnp.abs(z)))
    la_ref[...] = log_sig * (1.0 / GLA_GATE_TEMP)


def _in_proj(x2, w_in_t, layer, w2, gb, tm=1024):
    T = x2.shape[0]
    row = lambda wd: pl.BlockSpec((tm, wd), lambda i: (i, 0))
    full = lambda a: pl.BlockSpec(a.shape, lambda i: (0,) * a.ndim)
    outs = [(GLA_KEY_WIDTH, BF16), (GLA_KEY_WIDTH, BF16), (GLA_WIDTH, BF16), (GLA_WIDTH, BF16), (GLA_KEY_WIDTH, F32),
            (DIL_WIDTH, BF16), (DIL_WIDTH, BF16), (DIL_WIDTH, BF16)]
    group = w_in_t.shape[1] - IN_PROJ_GATE_COL - GLA_GATE_RANK
    return pl.pallas_call(
        _in_proj_kernel,
        grid=(T // tm,),
        in_specs=[row(D_MODEL),
                  pl.BlockSpec((None,) + w_in_t.shape[1:], lambda i: (layer, 0, 0), pipeline_mode=pl.Buffered(1)),
                  full(w2), full(gb)],
        out_specs=[row(wd) for wd, _ in outs],
        out_shape=[jax.ShapeDtypeStruct((T, wd), dt) for wd, dt in outs],
        scratch_shapes=[pltpu.VMEM((D_MODEL, IN_PROJ_GATE_COL), BF16), pltpu.VMEM((D_MODEL, group), BF16),
                        pltpu.VMEM((D_MODEL, LANES), BF16)],
        compiler_params=pltpu.CompilerParams(
            dimension_semantics=("arbitrary",), vmem_limit_bytes=VMEM_LIMIT),
        name="in_proj",
    )(x2, w_in_t, w2, gb)


def _gla_kernel(q_ref, k_ref, v_ref, r_ref, la_ref, g_ref, o_ref, s_ref, *, seq_block):
    C = GLA_CHUNK
    H = GLA_SUB
    assert C == 2 * H

    @pl.when(pl.program_id(1) == 0)
    def _():
        s_ref[...] = jnp.zeros_like(s_ref)

    ri = lax.broadcasted_iota(jnp.int32, (C, C), 0)
    ci = lax.broadcasted_iota(jnp.int32, (C, C), 1)
    same_sub = (ri // H) == (ci // H)
    sum_ops = jnp.concatenate([jnp.logical_and(same_sub, ci <= ri).astype(BF16), same_sub.astype(BF16),
                               jnp.ones((C, C), BF16)], axis=0)
    diag_mask = jnp.logical_and(same_sub, ci <= ri)
    off_mask = (ri // H) > (ci // H)
    second = lax.broadcasted_iota(jnp.int32, (C, 1), 0) >= H
    ones_cl = jnp.ones((C, LANES), BF16)
    lane_k = lax.broadcasted_iota(jnp.int32, (1, GLA_KEY_WIDTH), 1) // GLA_DK
    head_masks = [(lane_k == h).astype(F32) for h in range(GLA_HEADS)]
    srow = lax.broadcasted_iota(jnp.int32, (GLA_KEY_WIDTH, GLA_WIDTH), 0) // GLA_DK
    scol = lax.broadcasted_iota(jnp.int32, (GLA_KEY_WIDTH, GLA_WIDTH), 1) // GLA_DV
    state_mask = (srow == scol).astype(F32)
    g = g_ref[...]

    def trip(t, carry):
        U = GLA_UNROLL
        rows = [pl.ds(pl.multiple_of((t * U + u) * C, C), C) for u in range(U)]
        la2s, sums = [], []
        for u in range(U):
            la_hi, la_lo = _split_bf16(la_ref[rows[u], :])
            la2 = jnp.concatenate([la_hi, la_lo], axis=1)
            la2s.append(la2)
            sm = _dot(sum_ops, la2)
            sums.append(sm[:, 0:GLA_KEY_WIDTH] + sm[:, GLA_KEY_WIDTH:])
        q_states, k_states, scores = [], [], []
        for u in range(U):
            b = sums[u][0:C]
            t_sub = sums[u][C:2 * C]
            other = sums[u][2 * C:3 * C] - t_sub
            q = q_ref[rows[u], :].astype(F32)
            k = k_ref[rows[u], :].astype(F32)
            qd = q * jnp.exp(b)
            kd = (k * jnp.exp(-b)).astype(BF16)
            ke_f = k * jnp.exp(t_sub - b)
            cross = jnp.exp(other)
            q_states.append((qd * jnp.where(second, cross, 1.0)).astype(BF16))
            k_states.append((ke_f * jnp.where(second, 1.0, cross)).astype(BF16))
            q_heads = jnp.concatenate([(qd * head_masks[h]).astype(BF16) for h in range(GLA_HEADS)], axis=0)
            keys2 = jnp.concatenate([kd, ke_f.astype(BF16)], axis=0)
            scores.append(_dot_nt(q_heads, keys2))
        decs, upds = [], []
        for u in range(U):
            tot = _dot_tn(la2s[u], ones_cl)
            decs.append(jnp.exp(tot[0:GLA_KEY_WIDTH] + tot[GLA_KEY_WIDTH:]))
            upds.append(_dot_tn(k_states[u], v_ref[rows[u], :]) * state_mask)
        o_inters = []
        for u in range(U):
            state = s_ref[...]
            o_inters.append(_dot(q_states[u], state.astype(BF16)))
            for h in range(GLA_HEADS):
                cols = slice(h * GLA_DV, (h + 1) * GLA_DV)
                s_ref[:, cols] = state[:, cols] * decs[u] + upds[u][:, cols]
        for u in range(U):
            v = v_ref[rows[u], :]
            outs = []
            for h in range(GLA_HEADS):
                sh = scores[u][h * C:(h + 1) * C]
                a = jnp.where(diag_mask, sh[:, 0:C], 0.0) + jnp.where(off_mask, sh[:, C:2 * C], 0.0)
                cols = slice(h * GLA_DV, (h + 1) * GLA_DV)
                o = _dot(a.astype(BF16), v[:, cols]) + o_inters[u][:, cols]
                o = o * lax.rsqrt(jnp.mean(o * o, axis=-1, keepdims=True) + EPS) * g
                outs.append(o)
            o_all = jnp.concatenate(outs, axis=-1)
            rr = r_ref[rows[u], :].astype(F32)
            o_ref[rows[u], :] = (o_all * (rr * jax.nn.sigmoid(rr))).astype(BF16)
        return carry

    lax.fori_loop(0, seq_block // C // GLA_UNROLL, trip, 0)


def _gla(q, k, v, r, la, g, batch, seq, seq_block=1024):
    nsb = seq // seq_block
    row = lambda w: pl.BlockSpec((seq_block, w), lambda b, s: (b * nsb + s, 0))
    return pl.pallas_call(
        functools.partial(_gla_kernel, seq_block=seq_block),
        grid=(batch, nsb),
        in_specs=[row(GLA_KEY_WIDTH), row(GLA_KEY_WIDTH), row(GLA_WIDTH), row(GLA_WIDTH), row(GLA_KEY_WIDTH),
                  pl.BlockSpec((1, GLA_DV), lambda b, s: (0, 0))],
        out_specs=row(GLA_WIDTH),
        out_shape=jax.ShapeDtypeStruct((batch * seq, GLA_WIDTH), BF16),
        scratch_shapes=[pltpu.VMEM((GLA_KEY_WIDTH, GLA_WIDTH), F32)],
        compiler_params=pltpu.CompilerParams(
            dimension_semantics=("parallel", "arbitrary"), vmem_limit_bytes=VMEM_LIMIT),
        name="gla",
    )(q, k, v, r, la, g)


def _dil_kernel(slope_ref, q_ref, k_ref, v_ref, g_ref, o_ref,
                qf, kf, vf, qs, kd, va, vb, oc, lc, ocs, lcs, *, seq):
    B = DIL_BLOCK
    U = DIL_UNROLL
    pair = pl.program_id(1)
    x = q_ref[...].astype(F32)
    qf[...] = x
    x3 = x.reshape(seq // (2 * SUBLANES), 2 * SUBLANES, LANES)
    qs[0] = x3[:, :SUBLANES, :].reshape(seq // 2, LANES)
    qs[1] = x3[:, SUBLANES:, :].reshape(seq // 2, LANES)
    kf[...] = k_ref[...].astype(F32)
    vf[...] = v_ref[...].astype(F32)

    lane = lax.broadcasted_iota(jnp.int32, (1, LANES), 1)
    first = lane < DIL_DH
    ii = lax.broadcasted_iota(jnp.int32, (B, B), 0)
    jj = lax.broadcasted_iota(jnp.int32, (B, B), 1)
    upper = jj > ii
    eye = jj == ii
    dist = jnp.bitwise_and(ii - jj, B - 1).astype(F32)
    neg = jnp.float32(-jnp.inf)
    neg_tile = jnp.full((B, B), neg, F32)
    zero_tile = jnp.zeros((B, LANES), BF16)

    def split_rows(r, c, n):
        half = r // 2
        start = c % SUBLANES + c // (2 * SUBLANES) * SUBLANES + n * (B * half)
        return (c // SUBLANES) % 2, pl.ds(start, B, stride=half)

    def class_rows(nat, split, cfg, r, c, n):
        if r == 1:
            return nat[pl.ds(pl.multiple_of(n * B, B), B), :]
        if cfg == DIL_SPLIT_CFG:
            parity, rows = split_rows(r, c, n)
            return split[parity, rows, :]
        return nat[pl.ds(c + n * (B * r), B, stride=r), :]

    for cfg, (window, r) in enumerate(DIL_CONFIGS):
        nb = seq // r // B
        cs = nb + 1
        bias_prev, bias_cur = [], []
        for hh in range(2):
            slope = slope_ref[2 * pair + hh] * (float(r) * LOG2E)
            bias = dist * (-slope)
            bias_prev.append(jnp.where(upper, bias, jnp.where(eye, -slope * float(B), neg)))
            bias_cur.append(jnp.where(upper, neg, bias))

        for c in range(r):
            rows0 = slice(c * cs * B, (c * cs + 1) * B)
            kd[rows0, :] = zero_tile
            va[rows0, :] = zero_tile
            vb[rows0, :] = zero_tile

        def prep(t4, carry, cfg=cfg, r=r, nb=nb, cs=cs):
            for j in range(4):
                t = t4 * 4 + j
                c = t // nb
                n = t % nb
                dst = pl.ds(pl.multiple_of((c * cs + 1 + n) * B, B), B)
                kd[dst, :] = class_rows(kf, None, cfg, r, c, n).astype(BF16)
                v = class_rows(vf, None, cfg, r, c, n)
                va[dst, :] = jnp.where(first, v, 1.0).astype(BF16)
                vb[dst, :] = jnp.where(first, 1.0, v).astype(BF16)
            return carry

        def prep_split(i, carry, r=r, cs=cs):
            reg = pl.ds(pl.multiple_of(i * (B * r), B * r), B * r)
            kx = kf[reg, :].reshape(B, r, LANES).swapaxes(0, 1)
            vx = vf[reg, :].reshape(B, r, LANES).swapaxes(0, 1)
            for c in range(r):
                dst = pl.ds(pl.multiple_of((c * cs + 1 + i) * B, B), B)
                kd[dst, :] = kx[c].astype(BF16)
                va[dst, :] = jnp.where(first, vx[c], 1.0).astype(BF16)
                vb[dst, :] = jnp.where(first, 1.0, vx[c]).astype(BF16)
            return carry

        if cfg == DIL_SPLIT_CFG:
            lax.fori_loop(0, nb, prep_split, 0)
        else:
            lax.fori_loop(0, seq // B // 4, prep, 0)

        def geom(it, r=r, nb=nb, cs=cs):
            c = it // nb
            n = it % nb
            start = c + n * (B * r)
            rows = pl.ds(start, B, stride=r) if r > 1 else pl.ds(pl.multiple_of(start, B), B)
            kv = pl.ds(pl.multiple_of((c * cs + n) * B, B), 2 * B)
            return rows, kv, c, n

        def body(t, carry, cfg=cfg, r=r, nb=nb, bias_prev=bias_prev, bias_cur=bias_cur, geom=geom):
            geo = [geom(t * U + u) for u in range(U)]
            scores = []
            for u in range(U):
                _, kv, c, n = geo[u]
                q = class_rows(qf, qs, cfg, r, c, n)
                q_heads = jnp.concatenate([jnp.where(first, q, 0.0), jnp.where(first, 0.0, q)], axis=0)
                scores.append(_dot_nt(q_heads.astype(BF16), kd[kv, :]))
            probs, maxes = [], []
            for u in range(U):
                if nb % U == 0:
                    has_prev = True if u > 0 else (t * U) % nb > 0
                else:
                    assert U % nb == 0
                    has_prev = (u % nb) > 0
                for hh in range(2):
                    s2 = scores[u][hh * B:(hh + 1) * B]
                    if has_prev is True:
                        bp = bias_prev[hh]
                    elif has_prev is False:
                        bp = neg_tile
                    else:
                        bp = jnp.where(has_prev, bias_prev[hh], neg)
                    s_prev = s2[:, 0:B] + bp
                    s_cur = s2[:, B:2 * B] + bias_cur[hh]
                    m = jnp.max(jnp.maximum(s_prev, s_cur), axis=-1, keepdims=True)
                    probs.append(jnp.concatenate([jnp.exp2(s_prev - m), jnp.exp2(s_cur - m)], axis=1).astype(BF16))
                    maxes.append(m)
            for u in range(U):
                rows, kv, c, n = geo[u]
                acc0 = _dot(probs[2 * u], va[kv, :])
                acc1 = _dot(probs[2 * u + 1], vb[kv, :])
                num = jnp.where(first, acc0, acc1)
                den = pltpu.roll(jnp.where(first, acc1, acc0), DIL_DH, axis=1)
                out = num * (1.0 / den)
                lse = jnp.where(first, maxes[2 * u], maxes[2 * u + 1]) + jnp.log2(den)
                if cfg == DIL_SPLIT_CFG:
                    parity, dst = split_rows(r, c, n)
                    ocs[parity, dst, :] = out
                    lcs[parity, dst, :] = lse
                else:
                    oc[cfg, rows, :] = out
                    lc[cfg, rows, :] = lse
            return carry

        lax.fori_loop(0, seq // B // U, body, 0)

    g = g_ref[...]
    CH = 2048

    def mix(i, carry):
        rows = pl.ds(pl.multiple_of(i * CH, CH), CH)
        half_rows = pl.ds(pl.multiple_of(i * (CH // 2), CH // 2), CH // 2)

        def tiles_interleaved(split):
            even = split[0, half_rows, :].reshape(CH // (2 * SUBLANES), 1, SUBLANES, LANES)
            odd = split[1, half_rows, :].reshape(CH // (2 * SUBLANES), 1, SUBLANES, LANES)
            return jnp.concatenate([even, odd], axis=1).reshape(CH, LANES)

        l0, l1, l2 = lc[0, rows, :], lc[1, rows, :], tiles_interleaved(lcs)
        m = jnp.maximum(jnp.maximum(l0, l1), l2)
        e0, e1, e2 = jnp.exp2(l0 - m), jnp.exp2(l1 - m), jnp.exp2(l2 - m)
        inv = 1.0 / (e0 + e1 + e2)
        o = (e0 * oc[0, rows, :] + e1 * oc[1, rows, :] + e2 * tiles_interleaved(ocs)) * inv
        sq = o * o
        ms_a = jnp.sum(jnp.where(first, sq, 0.0), axis=-1, keepdims=True) * (1.0 / DIL_DH)
        ms_b = jnp.sum(jnp.where(first, 0.0, sq), axis=-1, keepdims=True) * (1.0 / DIL_DH)
        ms = jnp.where(first, ms_a, ms_b)
        o_ref[rows, :] = (o * lax.rsqrt(ms + EPS) * g).astype(BF16)
        return carry

    lax.fori_loop(0, seq // CH, mix, 0)


def _dilated(slopes, dq, dk, dv, g2, batch, seq):
    blk = pl.BlockSpec((seq, LANES), lambda b, p, s: (b, p))
    return pl.pallas_call(
        functools.partial(_dil_kernel, seq=seq),
        grid_spec=pltpu.PrefetchScalarGridSpec(
            num_scalar_prefetch=1,
            grid=(batch, DIL_WIDTH // LANES),
            in_specs=[blk, blk, blk, pl.BlockSpec((1, LANES), lambda b, p, s: (0, 0))],
            out_specs=blk,
            scratch_shapes=[pltpu.VMEM((seq, LANES), F32)] * 3 + [pltpu.VMEM((2, seq // 2, LANES), F32)]
                           + [pltpu.VMEM((seq + DIL_PAD, LANES), BF16)] * 3
                           + [pltpu.VMEM((2, seq, LANES), F32)] * 2 + [pltpu.VMEM((2, seq // 2, LANES), F32)] * 2,
        ),
        out_shape=jax.ShapeDtypeStruct((batch * seq, DIL_WIDTH), BF16),
        compiler_params=pltpu.CompilerParams(
            dimension_semantics=("parallel", "parallel"), vmem_limit_bytes=VMEM_LIMIT),
        name="dilated",
    )(slopes, dq, dk, dv, g2)


def _out_proj_kernel(og_ref, od_ref, x_ref, wg_ref, wd_ref, g_ref, b_ref,
                     rw_ref, rb_ref, h_ref, hp_ref, eid_ref, gate_ref, cnt_ref, w_s):
    @pl.when(pl.program_id(0) == 0)
    def _():
        w_s[0:GLA_WIDTH, :] = wg_ref[...].astype(BF16)
        w_s[GLA_WIDTH:, :] = wd_ref[...].astype(BF16)

    mix = _dot(jnp.concatenate([og_ref[...], od_ref[...]], axis=1), w_s[...])
    h = _layer_norm(DEEPNORM_ALPHA * x_ref[...] + mix, g_ref[...], b_ref[...])
    h_ref[...] = h
    hp_ref[...] = _pack_bf16_halves(h)
    lt = _dot_nt(rw_ref[...], h) + rb_ref[...]
    tm = lt.shape[1]
    row = lax.broadcasted_iota(jnp.int32, (EXPERTS_PER_GROUP, tm), 0).astype(F32)
    neg = jnp.float32(-jnp.inf)
    big = jnp.float32(1e9)
    coarse = jnp.where(row < N_GROUPS, lt[N_EXPERTS:N_EXPERTS + EXPERTS_PER_GROUP, :], neg)
    cmax = jnp.max(coarse, axis=0, keepdims=True)
    g_idx = jnp.min(jnp.where(coarse == cmax, row, big), axis=0, keepdims=True)
    p_group = 1.0 / jnp.sum(jnp.exp(coarse - cmax), axis=0, keepdims=True)
    fine = lt[(N_GROUPS - 1) * EXPERTS_PER_GROUP:N_EXPERTS, :]
    for g in range(N_GROUPS - 2, -1, -1):
        fine = jnp.where(g_idx == g, lt[g * EXPERTS_PER_GROUP:(g + 1) * EXPERTS_PER_GROUP, :], fine)
    v1 = jnp.max(fine, axis=0, keepdims=True)
    i1 = jnp.min(jnp.where(fine == v1, row, big), axis=0, keepdims=True)
    fine2 = jnp.where(row == i1, neg, fine)
    v2 = jnp.max(fine2, axis=0, keepdims=True)
    i2 = jnp.min(jnp.where(fine2 == v2, row, big), axis=0, keepdims=True)
    e2 = jnp.exp(v2 - v1)
    den = 1.0 + e2
    gate1 = p_group * (1.0 / den)
    gate2 = p_group * (e2 / den)
    id1 = g_idx * EXPERTS_PER_GROUP + i1
    id2 = g_idx * EXPERTS_PER_GROUP + i2
    eid_ref[...] = jnp.concatenate([id1, id2], axis=0).astype(jnp.int32)
    slab = jnp.concatenate([gate1, gate2, jnp.zeros((LANES - 2, tm), F32)], axis=0)
    gate_ref[...] = slab.T[:, 0:2]
    sub = lax.broadcasted_iota(jnp.int32, (LANES, tm), 0).astype(F32)
    onehot = jnp.logical_or(sub == id1, sub == id2).astype(BF16)

    @pl.when(pl.program_id(0) == 0)
    def _():
        cnt_ref[...] = jnp.zeros_like(cnt_ref)

    cnt_ref[...] += _dot(onehot, jnp.ones((tm, LANES), BF16))


def _out_proj(og, od, x2, w_out, layer, g, b, rw, rb, tm=1024):
    T = x2.shape[0]
    row = lambda w: pl.BlockSpec((tm, w), lambda i: (i, 0))
    full = lambda a: pl.BlockSpec(a.shape, lambda i: (0,) * a.ndim)
    half = lambda p: pl.BlockSpec((None, GLA_WIDTH, D_MODEL), lambda i: (layer, p, 0))
    return pl.pallas_call(
        _out_proj_kernel,
        grid=(T // tm,),
        in_specs=[row(GLA_WIDTH), row(DIL_WIDTH), row(D_MODEL), half(0), half(1), full(g), full(b),
                  full(rw), full(rb)],
        out_specs=[row(D_MODEL), row(D_MODEL // 2), pl.BlockSpec((2, tm), lambda i: (0, i)), row(2),
                   pl.BlockSpec((LANES, LANES), lambda i: (0, 0))],
        out_shape=[jax.ShapeDtypeStruct((T, D_MODEL), F32),
                   jax.ShapeDtypeStruct((T, D_MODEL // 2), jnp.int32),
                   jax.ShapeDtypeStruct((2, T), jnp.int32),
                   jax.ShapeDtypeStruct((T, 2), F32),
                   jax.ShapeDtypeStruct((LANES, LANES), F32)],
        scratch_shapes=[pltpu.VMEM((GLA_WIDTH + DIL_WIDTH, D_MODEL), BF16)],
        compiler_params=pltpu.CompilerParams(
            dimension_semantics=("arbitrary",), vmem_limit_bytes=VMEM_LIMIT),
        name="out_proj_router",
    )(og, od, x2, w_out, w_out, g, b, rw, rb)


def _positions_kernel(eid_ref, cnt_ref, dest_ref, be_ref, nv_ref, carry_ref, sp_ref, tri_ref, *, tb):
    i = pl.program_id(0)

    @pl.when(i == 0)
    def _():
        shift = int(math.log2(ROW_BLOCK))
        nb_col = (cnt_ref[...].astype(jnp.int32) + (ROW_BLOCK - 1)) >> shift
        r = lax.broadcasted_iota(jnp.int32, (LANES, LANES), 0)
        c = lax.broadcasted_iota(jnp.int32, (LANES, LANES), 1)
        nb_f = jnp.where(r < N_EXPERTS, nb_col, 0).astype(F32)
        start_col = _dot((c < r).astype(BF16), nb_f.astype(BF16))
        sp_ref[...] = start_col * float(ROW_BLOCK)
        carry_ref[...] = jnp.zeros_like(carry_ref)
        be_ref[...] = jnp.concatenate([start_col.T[0:1, :], nb_f.T[0:1, :]], axis=1).astype(jnp.int32)
        total = jnp.sum(nb_f[:, 0:1], axis=0, keepdims=True)
        nv_ref[...] = jnp.broadcast_to(total, (1, LANES)).astype(jnp.int32)
        tr = lax.broadcasted_iota(jnp.int32, (tb, tb), 0)
        tc = lax.broadcasted_iota(jnp.int32, (tb, tb), 1)
        tri_ref[...] = (tr < tc).astype(BF16)

    sub = lax.broadcasted_iota(jnp.int32, (LANES, tb), 0)
    oh1 = sub == eid_ref[0:1, :]
    oh2 = sub == eid_ref[1:2, :]
    oh = jnp.logical_or(oh1, oh2).astype(BF16)
    offset = jnp.tile(carry_ref[...] + sp_ref[...], (1, tb // LANES))
    before = _dot(oh, tri_ref[...]) + offset
    d1 = jnp.sum(jnp.where(oh1, before, 0.0), axis=0, keepdims=True)
    d2 = jnp.sum(jnp.where(oh2, before, 0.0), axis=0, keepdims=True)
    dest_ref[...] = jnp.concatenate([d1, d2], axis=0).astype(jnp.int32)
    carry_ref[...] += _dot(oh, jnp.ones((tb, LANES), BF16))


def _positions(eid_t, cnt, n_blocks_pad, tb=1024):
    T = eid_t.shape[1]
    return pl.pallas_call(
        functools.partial(_positions_kernel, tb=tb),
        grid=(T // tb,),
        in_specs=[pl.BlockSpec((2, tb), lambda i: (0, i)), pl.BlockSpec((LANES, LANES), lambda i: (0, 0))],
        out_specs=[pl.BlockSpec((2, tb), lambda i: (0, i)),
                   pl.BlockSpec((1, n_blocks_pad), lambda i: (0, 0)),
                   pl.BlockSpec((1, LANES), lambda i: (0, 0))],
        out_shape=[jax.ShapeDtypeStruct((2, T), jnp.int32),
                   jax.ShapeDtypeStruct((1, n_blocks_pad), jnp.int32),
                   jax.ShapeDtypeStruct((1, LANES), jnp.int32)],
        scratch_shapes=[pltpu.VMEM((LANES, LANES), F32), pltpu.VMEM((LANES, LANES), F32),
                        pltpu.VMEM((tb, tb), BF16)],
        compiler_params=pltpu.CompilerParams(dimension_semantics=("arbitrary",)),
        name="positions",
    )(eid_t, cnt)


def _sc_gather_rows(table, idx):
    n = idx.shape[0]
    d = table.shape[1]
    info = plsc.get_sparse_core_info()
    nc, ns = info.num_cores, info.num_subcores
    per_w = n // (nc * ns)
    assert per_w * nc * ns == n and per_w % SC_INDEX_WINDOW == 0
    mesh = plsc.VectorSubcoreMesh(core_axis_name="core", subcore_axis_name="subcore")
    nchunk = per_w // SC_GATHER_ROWS
    nbuf = SC_GATHER_BUFFERS

    @functools.partial(
        pl.kernel, out_type=jax.ShapeDtypeStruct((n, d), table.dtype), mesh=mesh,
        scratch_types=[pltpu.VMEM((per_w,), jnp.int32),
                       pltpu.VMEM((nbuf, SC_GATHER_ROWS, d), table.dtype),
                       pltpu.SemaphoreType.DMA((nbuf,)), pltpu.SemaphoreType.DMA((nbuf,))],
        name="sc_gather_rows")
    def gather(x_hbm, i_hbm, o_hbm, idx_v, buf, gsem, wsem):
        wid = lax.axis_index("subcore") * nc + lax.axis_index("core")
        base = wid * per_w
        pltpu.sync_copy(i_hbm.at[pl.ds(base, per_w)], idx_v)

        def gather_copy(c):
            rows = idx_v.at[pl.ds(c * SC_GATHER_ROWS, SC_GATHER_ROWS)]
            return pltpu.make_async_copy(x_hbm.at[rows], buf.at[c % nbuf], gsem.at[c % nbuf])

        def write_copy(c):
            dst = o_hbm.at[pl.ds(base + c * SC_GATHER_ROWS, SC_GATHER_ROWS)]
            return pltpu.make_async_copy(buf.at[c % nbuf], dst, wsem.at[c % nbuf])

        for c in range(min(nbuf - 1, nchunk)):
            gather_copy(c).start()
        for c in range(nchunk):
            gather_copy(c).wait()
            write_copy(c).start()
            if c + nbuf - 1 < nchunk:
                if c >= 1:
                    write_copy(c - 1).wait()
                gather_copy(c + nbuf - 1).start()
        for c in range(max(0, nchunk - nbuf), nchunk):
            write_copy(c).wait()

    return gather(table, idx)


def _sc_scatter_rows(table, dest_flat, n_rows):
    n_tok, d = table.shape
    assert dest_flat.shape[0] == 2 * n_tok
    info = plsc.get_sparse_core_info()
    nc, ns = info.num_cores, info.num_subcores
    per_w = n_tok // (nc * ns)
    assert per_w * nc * ns == n_tok and per_w % SC_INDEX_WINDOW == 0
    mesh = plsc.VectorSubcoreMesh(core_axis_name="core", subcore_axis_name="subcore")
    nchunk = per_w // SC_GATHER_ROWS
    nbuf = SC_GATHER_BUFFERS

    @functools.partial(
        pl.kernel, out_type=jax.ShapeDtypeStruct((n_rows, d), table.dtype), mesh=mesh,
        scratch_types=[pltpu.VMEM((per_w,), jnp.int32), pltpu.VMEM((per_w,), jnp.int32),
                       pltpu.VMEM((nbuf, SC_GATHER_ROWS, d), table.dtype),
                       pltpu.SemaphoreType.DMA((nbuf,)), pltpu.SemaphoreType.DMA((nbuf, 2))],
        name="sc_scatter_rows")
    def scatter(x_hbm, i_hbm, o_hbm, idx_a, idx_b, buf, rsem, wsem):
        wid = lax.axis_index("subcore") * nc + lax.axis_index("core")
        base = wid * per_w
        pltpu.sync_copy(i_hbm.at[pl.ds(base, per_w)], idx_a)
        pltpu.sync_copy(i_hbm.at[pl.ds(n_tok + base, per_w)], idx_b)

        def read_copy(c):
            src = x_hbm.at[pl.ds(base + c * SC_GATHER_ROWS, SC_GATHER_ROWS)]
            return pltpu.make_async_copy(src, buf.at[c % nbuf], rsem.at[c % nbuf])

        def scatter_copies(c):
            window = pl.ds(c * SC_GATHER_ROWS, SC_GATHER_ROWS)
            return [pltpu.make_async_copy(buf.at[c % nbuf], o_hbm.at[idx.at[window]], wsem.at[c % nbuf, k])
                    for k, idx in enumerate((idx_a, idx_b))]

        for c in range(min(nbuf - 1, nchunk)):
            read_copy(c).start()
        for c in range(nchunk):
            read_copy(c).wait()
            for cp in scatter_copies(c):
                cp.start()
            if c + nbuf - 1 < nchunk:
                if c >= 1:
                    for cp in scatter_copies(c - 1):
                        cp.wait()
                read_copy(c + nbuf - 1).start()
        for c in range(max(0, nchunk - nbuf), nchunk):
            for cp in scatter_copies(c):
                cp.wait()

    return scatter(table, dest_flat)


def _ffn_kernel(first_ref, count_ref, nv_ref, wg_hbm, wu_hbm, wd_hbm, xs_hbm, y_hbm,
                wg32, wu32, wd32, xbuf, ybuf, wsem, isem, osem, *, n_blocks):
    nv = nv_ref[0]
    nbuf = FFN_BUFFERS

    def next_expert(e):
        def more(t):
            return jnp.logical_and(t < N_EXPERTS, count_ref[jnp.minimum(t, N_EXPERTS - 1)] == 0)
        return lax.while_loop(more, lambda t: t + 1, e + 1)

    def weight_copies(e, slot):
        ee = jnp.minimum(e, N_EXPERTS - 1)
        return [pltpu.make_async_copy(src.at[ee], dst.at[slot], wsem.at[slot, i])
                for i, (src, dst) in enumerate(((wg_hbm, wg32), (wu_hbm, wu32), (wd_hbm, wd32)))]

    def fetch_weights(e, slot):
        @pl.when(e < N_EXPERTS)
        def _():
            for c in weight_copies(e, slot):
                c.start()

    def take_weights(e, slot):
        for c in weight_copies(e, slot):
            c.wait()
        fetch_weights(next_expert(e), 1 - slot)

    def rows_of(b):
        return pl.ds(pl.multiple_of(b * ROW_BLOCK, ROW_BLOCK), ROW_BLOCK)

    def buf_rows(b, nblk):
        return pl.ds(pl.multiple_of((b % nbuf) * ROW_BLOCK, ROW_BLOCK), nblk * ROW_BLOCK)

    def in_copy(b):
        return pltpu.make_async_copy(xs_hbm.at[rows_of(b)], xbuf.at[buf_rows(b, 1)], isem.at[b % nbuf])

    def out_copy(b):
        return pltpu.make_async_copy(ybuf.at[buf_rows(b, 1)], y_hbm.at[rows_of(b)], osem.at[b % nbuf])

    def expert_mlp(words, slot):
        x = jnp.concatenate(_unpack_bf16_halves(words), axis=1)
        a = _dot(x, wg32[slot])
        u = _dot(x, wu32[slot])
        return _pack_bf16_halves(_dot(a * jax.nn.sigmoid(a) * u, wd32[slot]))

    ahead = nbuf - max(FFN_GROUPS)

    @pl.when(nv > 0)
    def _():
        e0 = next_expert(jnp.int32(-1))
        for i in range(ahead):
            @pl.when(i < nv)
            def _():
                in_copy(i).start()
        fetch_weights(e0, 0)
        take_weights(e0, 0)

        def step(carry):
            b, e, k = carry
            switch = b >= first_ref[e] + count_ref[e]
            e_new = jnp.where(switch, next_expert(e), e)
            k_new = jnp.where(switch, k + 1, k)

            @pl.when(switch)
            def _():
                take_weights(e_new, k_new % 2)

            end = first_ref[e_new] + count_ref[e_new]
            n = jnp.int32(1)
            for size in FFN_GROUPS:
                fits = jnp.logical_and(b + size <= end, b % nbuf + size <= nbuf)
                n = jnp.where(jnp.logical_and(n == 1, fits), size, n)
            for i in range(max(FFN_GROUPS)):
                @pl.when(jnp.logical_and(i < n, b + ahead + i < nv))
                def _():
                    in_copy(b + ahead + i).start()
            for i in range(max(FFN_GROUPS)):
                @pl.when(i < n)
                def _():
                    in_copy(b + i).wait()

                    @pl.when(b + i >= nbuf)
                    def _():
                        out_copy(b + i - nbuf).wait()

            for size in FFN_GROUPS + (1,):
                @pl.when(n == size)
                def _():
                    ybuf[buf_rows(b, size), :] = expert_mlp(xbuf[buf_rows(b, size), :], k_new % 2)

            for i in range(max(FFN_GROUPS)):
                @pl.when(i < n)
                def _():
                    out_copy(b + i).start()
            return b + n, e_new, k_new

        lax.while_loop(lambda c: c[0] < nv, step, (jnp.int32(0), e0, jnp.int32(0)))

        for i in range(nbuf):
            @pl.when(nv > i)
            def _():
                out_copy(nv - 1 - i).wait()

    ybuf[0:ROW_BLOCK, :] = jnp.zeros((ROW_BLOCK, D_MODEL // 2), jnp.int32)

    def fill(b, carry):
        pltpu.sync_copy(ybuf.at[pl.ds(0, ROW_BLOCK)], y_hbm.at[rows_of(b)])
        return carry

    lax.fori_loop(nv, n_blocks, fill, 0)


def _ffn(first_blk, n_blk, nv, xs, w_gate, w_up, w_down):
    n_rows = xs.shape[0]
    n_blocks = n_rows // ROW_BLOCK
    anyspec = pl.BlockSpec(memory_space=pl.ANY)
    return pl.pallas_call(
        functools.partial(_ffn_kernel, n_blocks=n_blocks),
        grid_spec=pltpu.PrefetchScalarGridSpec(
            num_scalar_prefetch=3,
            grid=(1,),
            in_specs=[anyspec, anyspec, anyspec, anyspec],
            out_specs=anyspec,
            scratch_shapes=[pltpu.VMEM((2, D_MODEL, D_FF), F32), pltpu.VMEM((2, D_MODEL, D_FF), F32),
                            pltpu.VMEM((2, D_FF, D_MODEL), F32),
                            pltpu.VMEM((FFN_BUFFERS * ROW_BLOCK, D_MODEL // 2), jnp.int32),
                            pltpu.VMEM((FFN_BUFFERS * ROW_BLOCK, D_MODEL // 2), jnp.int32),
                            pltpu.SemaphoreType.DMA((2, 3)),
                            pltpu.SemaphoreType.DMA((FFN_BUFFERS,)),
                            pltpu.SemaphoreType.DMA((FFN_BUFFERS,))],
        ),
        out_shape=jax.ShapeDtypeStruct((n_rows, D_MODEL // 2), jnp.int32),
        compiler_params=pltpu.CompilerParams(
            dimension_semantics=("arbitrary",), vmem_limit_bytes=VMEM_LIMIT),
        name="expert_ffn",
    )(first_blk, n_blk, nv, w_gate, w_up, w_down, xs)


def _combine_kernel(h_ref, ya_ref, yb_ref, gate_ref, g_ref, b_ref, o_ref):
    gate = gate_ref[...]
    a_hi, a_lo = _unpack_bf16_halves(ya_ref[...])
    b_hi, b_lo = _unpack_bf16_halves(yb_ref[...])
    g0, g1 = gate[:, 0:1], gate[:, 1:2]
    ffn = jnp.concatenate([a_hi * g0 + b_hi * g1, a_lo * g0 + b_lo * g1], axis=1)
    o_ref[...] = _layer_norm(DEEPNORM_ALPHA * h_ref[...] + ffn, g_ref[...], b_ref[...])


def _combine(h, y2, gate, g, b, tm=1024):
    T = h.shape[0]
    nt = T // tm
    return pl.pallas_call(
        _combine_kernel,
        grid=(nt,),
        in_specs=[pl.BlockSpec((tm, D_MODEL), lambda i: (i, 0)),
                  pl.BlockSpec((tm, D_MODEL // 2), lambda i: (i, 0)),
                  pl.BlockSpec((tm, D_MODEL // 2), lambda i: (i + nt, 0)),
                  pl.BlockSpec((tm, 2), lambda i: (i, 0)),
                  pl.BlockSpec((1, D_MODEL), lambda i: (0, 0)),
                  pl.BlockSpec((1, D_MODEL), lambda i: (0, 0))],
        out_specs=pl.BlockSpec((tm, D_MODEL), lambda i: (i, 0)),
        out_shape=jax.ShapeDtypeStruct((T, D_MODEL), F32),
        compiler_params=pltpu.CompilerParams(
            dimension_semantics=("parallel",), vmem_limit_bytes=VMEM_LIMIT),
        name="combine",
    )(h, y2, y2, gate, g, b)


def kernel(x, w_in, gla_gate_w2, gla_gate_b, gla_norm_g, dil_norm_g, w_out, ln1_g, ln1_b,
           router_coarse_w, router_coarse_b, router_fine_w, router_fine_b,
           expert_w_gate, expert_w_up, expert_w_down, ln2_g, ln2_b):
    B, S, D = x.shape
    T = B * S
    depth = w_in.shape[0]
    slopes = jnp.exp2(-8.0 * jnp.arange(1, DIL_HEADS + 1, dtype=F32) / DIL_HEADS)
    n_rows = 2 * T + N_EXPERTS * ROW_BLOCK
    n_blocks_pad = 2 * LANES
    h = x.reshape(T, D)
    w_in_t = jnp.swapaxes(w_in, 1, 2)
    for l in range(depth):
        w2 = jnp.pad(gla_gate_w2[l], ((0, LANES - GLA_GATE_RANK), (0, 0)))
        q, k, v, r, la, dq, dk, dv = _in_proj(h, w_in_t, l, w2, gla_gate_b[l][None, :])
        o_gla = _gla(q, k, v, r, la, gla_norm_g[l][None, :], B, S)
        g2 = jnp.tile(dil_norm_g[l], 2)[None, :]
        o_dil = _dilated(slopes, dq, dk, dv, g2, B, S)
        rw = jnp.concatenate([router_fine_w[l].reshape(D, N_EXPERTS), router_coarse_w[l]], axis=1)
        rw = jnp.pad(rw, ((0, 0), (0, LANES - N_EXPERTS - N_GROUPS))).T
        rb = jnp.concatenate([router_fine_b[l].reshape(N_EXPERTS), router_coarse_b[l]])
        rb = jnp.pad(rb, (0, LANES - N_EXPERTS - N_GROUPS))[:, None]
        h1, h1p, eid_t, gate, cnt = _out_proj(o_gla, o_dil, h, w_out, l,
                                              ln1_g[l][None, :], ln1_b[l][None, :], rw, rb)
        dest_t, be, nv = _positions(eid_t, cnt, n_blocks_pad)
        dest_flat = dest_t.reshape(2 * T)
        xs = _sc_scatter_rows(h1p, dest_flat, n_rows)
        be = be.reshape(n_blocks_pad)
        y = _ffn(be[:N_EXPERTS], be[LANES:LANES + N_EXPERTS], nv.reshape(LANES)[:1], xs,
                 expert_w_gate[l], expert_w_up[l], expert_w_down[l])
        y2 = _sc_gather_rows(y, dest_flat)
        h = _combine(h1, y2, gate, ln2_g[l][None, :], ln2_b[l][None, :])
    return h.reshape(B, S, D)
```
